```python
import math
import jax, jax.numpy as jnp
from jax import lax
import numpy as np


D_MODEL = 1024
BATCH = 16
SEQ = 4096
DEPTH = 2

N_MIXERS = 2
N_A = (DEPTH + 1) // 2
N_B = DEPTH // 2

SSM_EXPAND = 2
D_INNER = SSM_EXPAND * D_MODEL
SSM_HEAD_DIM = 64
SSM_HEADS = D_INNER // SSM_HEAD_DIM
SSM_GROUPS = 8
SSM_HPG = SSM_HEADS // SSM_GROUPS
SSM_STATE = 128
CONV_K = 4
CONV_DIM = D_INNER + 2 * SSM_GROUPS * SSM_STATE
SSM_IN_DIM = D_INNER + CONV_DIM + SSM_HEADS
NORM_GROUP = D_INNER // SSM_GROUPS
CHUNK = 128
DT_MIN = 0.001
DT_MAX = 0.1

MLA_HEADS = 16
Q_LORA = 384
KV_LORA = 256
QK_NOPE = 64
QK_ROPE = 32
V_DIM = 64
QK_DIM = QK_NOPE + QK_ROPE
MLA_IN_DIM = Q_LORA + KV_LORA + QK_ROPE
ROPE_THETA = 10000.0
Q_BLOCK = 128

D_FF = 2816
EPS = 1e-6

kernel_name = "hybrid_ssd_mla_macaron_trunk"


def rms_norm(x, w):
    xf = x.astype(jnp.float32)
    y = xf * lax.rsqrt(jnp.mean(xf * xf, axis=-1, keepdims=True) + EPS)
    return (y * w.astype(jnp.float32)).astype(x.dtype)


def swiglu(h, w_gate, w_up, w_down):
    return (jax.nn.silu(h @ w_gate) * (h @ w_up)) @ w_down


def causal_depthwise_conv(u, w, b):
    y = lax.conv_general_dilated(u, w[:, None, :].astype(u.dtype), window_strides=(1,),
                                 padding=[(CONV_K - 1, 0)],
                                 dimension_numbers=("NWC", "WIO", "NWC"),
                                 feature_group_count=u.shape[-1])
    return y + b


def ssd_chunked(xs, dt, a, bm, cm):
    b, s = xs.shape[:2]
    nc = s // CHUNK

    def to_chunks(t):
        return jnp.swapaxes(t.reshape((b, nc, CHUNK) + t.shape[2:]), 0, 1)

    mask = jnp.tril(jnp.ones((CHUNK, CHUNK), dtype=bool))[None, :, :, None, None]

    def step(state, inp):
        xc, dtc, bc, cc = inp
        acs = jnp.cumsum(dtc * a, axis=1)
        seg = acs[:, :, None] - acs[:, None, :]
        decay = jnp.exp(jnp.where(mask, seg, -jnp.inf))
        cb = jnp.einsum("blgn,bsgn->blsg", cc, bc)
        xdt = xc * dtc[..., None]
        y_diag = jnp.einsum("blsg,blsgr,bsgrp->blgrp", cb, decay, xdt)
        y_off = jnp.einsum("blgn,bgrpn->blgrp", cc, state) * jnp.exp(acs)[..., None]
        decay_to_end = jnp.exp(acs[:, -1:] - acs)
        new_state = (state * jnp.exp(acs[:, -1])[..., None, None]
                     + jnp.einsum("bsgn,bsgr,bsgrp->bgrpn", bc, decay_to_end, xdt))
        return new_state, y_diag + y_off

    state0 = jnp.zeros((b, SSM_GROUPS, SSM_HPG, SSM_HEAD_DIM, SSM_STATE), jnp.float32)
    _, ys = lax.scan(step, state0, (to_chunks(xs), to_chunks(dt), to_chunks(bm), to_chunks(cm)))
    return jnp.swapaxes(ys, 0, 1).reshape(xs.shape)


def mamba2_mixer(h, w_in, conv_w, conv_b, dt_bias, a_log, d_skip, norm_w, w_out):
    b, s, _ = h.shape
    proj = h @ w_in
    z = proj[..., :D_INNER]
    xbc = proj[..., D_INNER:D_INNER + CONV_DIM]
    dt = proj[..., D_INNER + CONV_DIM:]
    xbc = jax.nn.silu(causal_depthwise_conv(xbc, conv_w, conv_b))
    xs = xbc[..., :D_INNER].astype(jnp.float32).reshape(b, s, SSM_GROUPS, SSM_HPG, SSM_HEAD_DIM)
    bm = xbc[..., D_INNER:D_INNER + SSM_GROUPS * SSM_STATE].astype(jnp.float32).reshape(b, s, SSM_GROUPS, SSM_STATE)
    cm = xbc[..., D_INNER + SSM_GROUPS * SSM_STATE:].astype(jnp.float32).reshape(b, s, SSM_GROUPS, SSM_STATE)
    dt = jax.nn.softplus(dt.astype(jnp.float32) + dt_bias.astype(jnp.float32)).reshape(b, s, SSM_GROUPS, SSM_HPG)
    a = (-jnp.exp(a_log.astype(jnp.float32))).reshape(SSM_GROUPS, SSM_HPG)
    y = ssd_chunked(xs, dt, a, bm, cm)
    y = y + d_skip.astype(jnp.float32).reshape(SSM_GROUPS, SSM_HPG)[:, :, None] * xs
    g = (y.reshape(b, s, D_INNER) * jax.nn.silu(z.astype(jnp.float32))).reshape(b, s, SSM_GROUPS, NORM_GROUP)
    g = g * lax.rsqrt(jnp.mean(g * g, axis=-1, keepdims=True) + EPS)
    g = g.reshape(b, s, D_INNER) * norm_w.astype(jnp.float32)
    return g.astype(h.dtype) @ w_out


def rope_cos_sin(positions):
    inv_freq = 1.0 / (ROPE_THETA ** (jnp.arange(0, QK_ROPE, 2, dtype=jnp.float32) / QK_ROPE))
    ang = positions.astype(jnp.float32)[..., None] * inv_freq
    return jnp.cos(ang), jnp.sin(ang)


def apply_rope(x, cos, sin):
    xf = x.astype(jnp.float32)
    x1, x2 = xf[..., :QK_ROPE // 2], xf[..., QK_ROPE // 2:]
    return jnp.concatenate([x1 * cos - x2 * sin, x2 * cos + x1 * sin], axis=-1).astype(x.dtype)


def causal_block_attention(q_nope, q_rope, k_nope, k_rope, v):
    s = q_nope.shape[1]
    scale = QK_DIM ** -0.5
    outs = []
    for i in range(s // Q_BLOCK):
        q0 = i * Q_BLOCK
        kend = q0 + Q_BLOCK
        sc = (jnp.einsum("bqhd,bkhd->bhqk", q_nope[:, q0:kend], k_nope[:, :kend])
              + jnp.einsum("bqhr,bkr->bhqk", q_rope[:, q0:kend], k_rope[:, :kend]))
        sc = sc.astype(jnp.float32) * scale
        qi = q0 + jnp.arange(Q_BLOCK)
        ki = jnp.arange(kend)
        sc = jnp.where(ki[None, :] <= qi[:, None], sc, -jnp.inf)
        p = jax.nn.softmax(sc, axis=-1).astype(v.dtype)
        outs.append(jnp.einsum("bhqk,bkhd->bqhd", p, v[:, :kend]))
    return jnp.concatenate(outs, axis=1)


def mla_mixer(h, positions, w_in, q_a_norm, kv_a_norm, w_q_b, w_kv_b, q_norm, k_norm, w_out):
    b, s, _ = h.shape
    proj = h @ w_in
    cq = rms_norm(proj[..., :Q_LORA], q_a_norm)
    ckv = rms_norm(proj[..., Q_LORA:Q_LORA + KV_LORA], kv_a_norm)
    k_rope = proj[..., Q_LORA + KV_LORA:]
    q = (cq @ w_q_b).reshape(b, s, MLA_HEADS, QK_DIM)
    kv = (ckv @ w_kv_b).reshape(b, s, MLA_HEADS, QK_NOPE + V_DIM)
    q_nope, q_rope = q[..., :QK_NOPE], q[..., QK_NOPE:]
    k_nope, v = kv[..., :QK_NOPE], kv[..., QK_NOPE:]
    q_nope = rms_norm(q_nope, q_norm[:QK_NOPE])
    q_rope = rms_norm(q_rope, q_norm[QK_NOPE:])
    k_nope = rms_norm(k_nope, k_norm[:QK_NOPE])
    k_rope = rms_norm(k_rope, k_norm[QK_NOPE:])
    cos, sin = rope_cos_sin(positions)
    q_rope = apply_rope(q_rope, cos[:, :, None], sin[:, :, None])
    k_rope = apply_rope(k_rope, cos, sin)
    o = causal_block_attention(q_nope, q_rope, k_nope, k_rope, v)
    return o.reshape(b, s, MLA_HEADS * V_DIM) @ w_out


def _fwd_setup_inputs(seed: int = 0) -> dict:
    key = jax.random.key(seed)
    ks = jax.random.split(key, 24)
    f32 = jnp.float32

    def nrm(k, shape, fan_in):
        return jax.random.normal(k, shape, f32) * fan_in ** -0.5

    def gain(k, shape):
        return 1.0 + 0.05 * jax.random.normal(k, shape, f32)

    x = jax.random.normal(ks[0], (BATCH, SEQ, D_MODEL), f32)
    offs = jax.random.randint(ks[1], (BATCH, 1), 0, 512, dtype=jnp.int32)
    positions = (jnp.arange(SEQ, dtype=jnp.int32)[None, :] + offs).astype(jnp.int32)

    norm_w = gain(ks[2], (DEPTH, 3, D_MODEL))
    ffn_w_gate = nrm(ks[3], (DEPTH, 2, D_MODEL, D_FF), D_MODEL)
    ffn_w_up = nrm(ks[4], (DEPTH, 2, D_MODEL, D_FF), D_MODEL)
    ffn_w_down = nrm(ks[5], (DEPTH, 2, D_FF, D_MODEL), D_FF)

    ssm_w_in = nrm(ks[6], (N_A, D_MODEL, SSM_IN_DIM), D_MODEL)
    ssm_conv_w = nrm(ks[7], (N_A, CONV_K, CONV_DIM), CONV_K)
    ssm_conv_b = 0.02 * jax.random.normal(ks[8], (N_A, CONV_DIM), f32)
    u = jax.random.uniform(ks[9], (N_A, SSM_HEADS), f32)
    dt0 = jnp.exp(u * (math.log(DT_MAX) - math.log(DT_MIN)) + math.log(DT_MIN))
    ssm_dt_bias = dt0 + jnp.log(-jnp.expm1(-dt0))
    ssm_a_log = jnp.log(jax.random.uniform(ks[10], (N_A, SSM_HEADS), f32, minval=1.0, maxval=16.0))
    ssm_d = 1.0 + 0.1 * jax.random.normal(ks[11], (N_A, SSM_HEADS), f32)
    ssm_norm_w = gain(ks[12], (N_A, D_INNER))
    ssm_w_out = nrm(ks[13], (N_A, D_INNER, D_MODEL), D_INNER)

    mla_w_in = nrm(ks[14], (N_B, D_MODEL, MLA_IN_DIM), D_MODEL)
    mla_q_a_norm = gain(ks[15], (N_B, Q_LORA))
    mla_kv_a_norm = gain(ks[16], (N_B, KV_LORA))
    mla_w_q_b = nrm(ks[17], (N_B, Q_LORA, MLA_HEADS * QK_DIM), Q_LORA)
    mla_w_kv_b = nrm(ks[18], (N_B, KV_LORA, MLA_HEADS * (QK_NOPE + V_DIM)), KV_LORA)
    mla_q_norm = gain(ks[19], (N_B, QK_DIM))
    mla_k_norm = gain(ks[20], (N_B, QK_DIM))
    mla_w_out = nrm(ks[21], (N_B, MLA_HEADS * V_DIM, D_MODEL), MLA_HEADS * V_DIM)

    return {"x": x, "positions": positions, "norm_w": norm_w,
            "ffn_w_gate": ffn_w_gate, "ffn_w_up": ffn_w_up, "ffn_w_down": ffn_w_down,
            "ssm_w_in": ssm_w_in, "ssm_conv_w": ssm_conv_w, "ssm_conv_b": ssm_conv_b,
            "ssm_dt_bias": ssm_dt_bias, "ssm_a_log": ssm_a_log, "ssm_d": ssm_d,
            "ssm_norm_w": ssm_norm_w, "ssm_w_out": ssm_w_out,
            "mla_w_in": mla_w_in, "mla_q_a_norm": mla_q_a_norm, "mla_kv_a_norm": mla_kv_a_norm,
            "mla_w_q_b": mla_w_q_b, "mla_w_kv_b": mla_w_kv_b, "mla_q_norm": mla_q_norm,
            "mla_k_norm": mla_k_norm, "mla_w_out": mla_w_out}


def _fwd_reference(x, positions, norm_w, ffn_w_gate, ffn_w_up, ffn_w_down,
              ssm_w_in, ssm_conv_w, ssm_conv_b, ssm_dt_bias, ssm_a_log, ssm_d,
              ssm_norm_w, ssm_w_out,
              mla_w_in, mla_q_a_norm, mla_kv_a_norm, mla_w_q_b, mla_w_kv_b,
              mla_q_norm, mla_k_norm, mla_w_out):
    for i in range(DEPTH):
        x = x + 0.5 * swiglu(rms_norm(x, norm_w[i, 0]), ffn_w_gate[i, 0], ffn_w_up[i, 0], ffn_w_down[i, 0])
        h = rms_norm(x, norm_w[i, 1])
        j = i // N_MIXERS
        if i % N_MIXERS == 0:
            x = x + mamba2_mixer(h, ssm_w_in[j], ssm_conv_w[j], ssm_conv_b[j], ssm_dt_bias[j],
                                 ssm_a_log[j], ssm_d[j], ssm_norm_w[j], ssm_w_out[j])
        else:
            x = x + mla_mixer(h, positions, mla_w_in[j], mla_q_a_norm[j], mla_kv_a_norm[j],
                              mla_w_q_b[j], mla_w_kv_b[j], mla_q_norm[j], mla_k_norm[j], mla_w_out[j])
        x = x + 0.5 * swiglu(rms_norm(x, norm_w[i, 2]), ffn_w_gate[i, 1], ffn_w_up[i, 1], ffn_w_down[i, 1])
    return x


import jax as _jax
import jax.numpy as _jnp

TWIN_FORMAT = 'train_step'
FWD_PARAMS = ['x', 'positions', 'norm_w', 'ffn_w_gate', 'ffn_w_up', 'ffn_w_down', 'ssm_w_in', 'ssm_conv_w', 'ssm_conv_b', 'ssm_dt_bias', 'ssm_a_log', 'ssm_d', 'ssm_norm_w', 'ssm_w_out', 'mla_w_in', 'mla_q_a_norm', 'mla_kv_a_norm', 'mla_w_q_b', 'mla_w_kv_b', 'mla_q_norm', 'mla_k_norm', 'mla_w_out']
TWIN_WEIGHTS = ['norm_w', 'ffn_w_gate', 'ffn_w_up', 'ffn_w_down', 'ssm_w_in', 'ssm_conv_w', 'ssm_conv_b', 'ssm_dt_bias', 'ssm_a_log', 'ssm_d', 'ssm_norm_w', 'ssm_w_out', 'mla_w_in', 'mla_q_a_norm', 'mla_kv_a_norm', 'mla_w_q_b', 'mla_w_kv_b', 'mla_q_norm', 'mla_k_norm', 'mla_w_out']
TWIN_DIFF_INPUT = 'x'
TWIN_INPUTS = ['x', 'positions', 'norm_w', 'ffn_w_gate', 'ffn_w_up', 'ffn_w_down', 'ssm_w_in', 'ssm_conv_w', 'ssm_conv_b', 'ssm_dt_bias', 'ssm_a_log', 'ssm_d', 'ssm_norm_w', 'ssm_w_out', 'mla_w_in', 'mla_q_a_norm', 'mla_kv_a_norm', 'mla_w_q_b', 'mla_w_kv_b', 'mla_q_norm', 'mla_k_norm', 'mla_w_out', 'loss_target', 'm_norm_w', 'm_ffn_w_gate', 'm_ffn_w_up', 'm_ffn_w_down', 'm_ssm_w_in', 'm_ssm_conv_w', 'm_ssm_conv_b', 'm_ssm_dt_bias', 'm_ssm_a_log', 'm_ssm_d', 'm_ssm_norm_w', 'm_ssm_w_out', 'm_mla_w_in', 'm_mla_q_a_norm', 'm_mla_kv_a_norm', 'm_mla_w_q_b', 'm_mla_w_kv_b', 'm_mla_q_norm', 'm_mla_k_norm', 'm_mla_w_out', 'v_norm_w', 'v_ffn_w_gate', 'v_ffn_w_up', 'v_ffn_w_down', 'v_ssm_w_in', 'v_ssm_conv_w', 'v_ssm_conv_b', 'v_ssm_dt_bias', 'v_ssm_a_log', 'v_ssm_d', 'v_ssm_norm_w', 'v_ssm_w_out', 'v_mla_w_in', 'v_mla_q_a_norm', 'v_mla_kv_a_norm', 'v_mla_w_q_b', 'v_mla_w_kv_b', 'v_mla_q_norm', 'v_mla_k_norm', 'v_mla_w_out']
TWIN_OUTPUTS = ['loss', 'grad_x', 'grad_norm_w', 'grad_ffn_w_gate', 'grad_ffn_w_up', 'grad_ffn_w_down', 'grad_ssm_w_in', 'grad_ssm_conv_w', 'grad_ssm_conv_b', 'grad_ssm_dt_bias', 'grad_ssm_a_log', 'grad_ssm_d', 'grad_ssm_norm_w', 'grad_ssm_w_out', 'grad_mla_w_in', 'grad_mla_q_a_norm', 'grad_mla_kv_a_norm', 'grad_mla_w_q_b', 'grad_mla_w_kv_b', 'grad_mla_q_norm', 'grad_mla_k_norm', 'grad_mla_w_out', 'delta_norm_w', 'delta_ffn_w_gate', 'delta_ffn_w_up', 'delta_ffn_w_down', 'delta_ssm_w_in', 'delta_ssm_conv_w', 'delta_ssm_conv_b', 'delta_ssm_dt_bias', 'delta_ssm_a_log', 'delta_ssm_d', 'delta_ssm_norm_w', 'delta_ssm_w_out', 'delta_mla_w_in', 'delta_mla_q_a_norm', 'delta_mla_kv_a_norm', 'delta_mla_w_q_b', 'delta_mla_w_kv_b', 'delta_mla_q_norm', 'delta_mla_k_norm', 'delta_mla_w_out', 'new_m_norm_w', 'new_m_ffn_w_gate', 'new_m_ffn_w_up', 'new_m_ffn_w_down', 'new_m_ssm_w_in', 'new_m_ssm_conv_w', 'new_m_ssm_conv_b', 'new_m_ssm_dt_bias', 'new_m_ssm_a_log', 'new_m_ssm_d', 'new_m_ssm_norm_w', 'new_m_ssm_w_out', 'new_m_mla_w_in', 'new_m_mla_q_a_norm', 'new_m_mla_kv_a_norm', 'new_m_mla_w_q_b', 'new_m_mla_w_kv_b', 'new_m_mla_q_norm', 'new_m_mla_k_norm', 'new_m_mla_w_out', 'new_v_norm_w', 'new_v_ffn_w_gate', 'new_v_ffn_w_up', 'new_v_ffn_w_down', 'new_v_ssm_w_in', 'new_v_ssm_conv_w', 'new_v_ssm_conv_b', 'new_v_ssm_dt_bias', 'new_v_ssm_a_log', 'new_v_ssm_d', 'new_v_ssm_norm_w', 'new_v_ssm_w_out', 'new_v_mla_w_in', 'new_v_mla_q_a_norm', 'new_v_mla_kv_a_norm', 'new_v_mla_w_q_b', 'new_v_mla_w_kv_b', 'new_v_mla_q_norm', 'new_v_mla_k_norm', 'new_v_mla_w_out']
TWIN_LEAF_KINDS = {'loss': 'loss', 'grad_x': 'grad_x', 'grad_norm_w': 'grad_w', 'grad_ffn_w_gate': 'grad_w', 'grad_ffn_w_up': 'grad_w', 'grad_ffn_w_down': 'grad_w', 'grad_ssm_w_in': 'grad_w', 'grad_ssm_conv_w': 'grad_w', 'grad_ssm_conv_b': 'grad_w', 'grad_ssm_dt_bias': 'grad_w', 'grad_ssm_a_log': 'grad_w', 'grad_ssm_d': 'grad_w', 'grad_ssm_norm_w': 'grad_w', 'grad_ssm_w_out': 'grad_w', 'grad_mla_w_in': 'grad_w', 'grad_mla_q_a_norm': 'grad_w', 'grad_mla_kv_a_norm': 'grad_w', 'grad_mla_w_q_b': 'grad_w', 'grad_mla_w_kv_b': 'grad_w', 'grad_mla_q_norm': 'grad_w', 'grad_mla_k_norm': 'grad_w', 'grad_mla_w_out': 'grad_w', 'delta_norm_w': 'delta_w', 'delta_ffn_w_gate': 'delta_w', 'delta_ffn_w_up': 'delta_w', 'delta_ffn_w_down': 'delta_w', 'delta_ssm_w_in': 'delta_w', 'delta_ssm_conv_w': 'delta_w', 'delta_ssm_conv_b': 'delta_w', 'delta_ssm_dt_bias': 'delta_w', 'delta_ssm_a_log': 'delta_w', 'delta_ssm_d': 'delta_w', 'delta_ssm_norm_w': 'delta_w', 'delta_ssm_w_out': 'delta_w', 'delta_mla_w_in': 'delta_w', 'delta_mla_q_a_norm': 'delta_w', 'delta_mla_kv_a_norm': 'delta_w', 'delta_mla_w_q_b': 'delta_w', 'delta_mla_w_kv_b': 'delta_w', 'delta_mla_q_norm': 'delta_w', 'delta_mla_k_norm': 'delta_w', 'delta_mla_w_out': 'delta_w', 'new_m_norm_w': 'new_m', 'new_m_ffn_w_gate': 'new_m', 'new_m_ffn_w_up': 'new_m', 'new_m_ffn_w_down': 'new_m', 'new_m_ssm_w_in': 'new_m', 'new_m_ssm_conv_w': 'new_m', 'new_m_ssm_conv_b': 'new_m', 'new_m_ssm_dt_bias': 'new_m', 'new_m_ssm_a_log': 'new_m', 'new_m_ssm_d': 'new_m', 'new_m_ssm_norm_w': 'new_m', 'new_m_ssm_w_out': 'new_m', 'new_m_mla_w_in': 'new_m', 'new_m_mla_q_a_norm': 'new_m', 'new_m_mla_kv_a_norm': 'new_m', 'new_m_mla_w_q_b': 'new_m', 'new_m_mla_w_kv_b': 'new_m', 'new_m_mla_q_norm': 'new_m', 'new_m_mla_k_norm': 'new_m', 'new_m_mla_w_out': 'new_m', 'new_v_norm_w': 'new_v', 'new_v_ffn_w_gate': 'new_v', 'new_v_ffn_w_up': 'new_v', 'new_v_ffn_w_down': 'new_v', 'new_v_ssm_w_in': 'new_v', 'new_v_ssm_conv_w': 'new_v', 'new_v_ssm_conv_b': 'new_v', 'new_v_ssm_dt_bias': 'new_v', 'new_v_ssm_a_log': 'new_v', 'new_v_ssm_d': 'new_v', 'new_v_ssm_norm_w': 'new_v', 'new_v_ssm_w_out': 'new_v', 'new_v_mla_w_in': 'new_v', 'new_v_mla_q_a_norm': 'new_v', 'new_v_mla_kv_a_norm': 'new_v', 'new_v_mla_w_q_b': 'new_v', 'new_v_mla_w_kv_b': 'new_v', 'new_v_mla_q_norm': 'new_v', 'new_v_mla_k_norm': 'new_v', 'new_v_mla_w_out': 'new_v'}


def _forward(args):
    return _fwd_reference(*[args[k] for k in FWD_PARAMS])


def _output_shape():
    out = _jax.eval_shape(lambda: _forward(_fwd_setup_inputs(0)))
    return out.shape, out.dtype

N_MICROBATCH = 1
ADAM_LR = 0.001
ADAM_B1 = 0.9
ADAM_B2 = 0.999
ADAM_EPS = 1e-08
ADAM_WD = 0.01
ADAM_STEP = 10
PER_EXAMPLE_BATCH_AXIS = {'x': 0, 'positions': 0, 'loss_target': 0}
SHARED_INPUTS = []
_WEIGHT_DTYPES = {'norm_w': _jnp.float32, 'ffn_w_gate': _jnp.float32, 'ffn_w_up': _jnp.float32, 'ffn_w_down': _jnp.float32, 'ssm_w_in': _jnp.float32, 'ssm_conv_w': _jnp.float32, 'ssm_conv_b': _jnp.float32, 'ssm_dt_bias': _jnp.float32, 'ssm_a_log': _jnp.float32, 'ssm_d': _jnp.float32, 'ssm_norm_w': _jnp.float32, 'ssm_w_out': _jnp.float32, 'mla_w_in': _jnp.float32, 'mla_q_a_norm': _jnp.float32, 'mla_kv_a_norm': _jnp.float32, 'mla_w_q_b': _jnp.float32, 'mla_w_kv_b': _jnp.float32, 'mla_q_norm': _jnp.float32, 'mla_k_norm': _jnp.float32, 'mla_w_out': _jnp.float32}
MOMENT_SCALE = {'norm_w': 9.434747e+00, 'ffn_w_gate': 1.596715e-01, 'ffn_w_up': 1.734832e-01, 'ffn_w_down': 2.854189e-01, 'ssm_w_in': 3.722360e-01, 'ssm_conv_w': 1.506740e+00, 'ssm_conv_b': 5.770780e+00, 'ssm_dt_bias': 2.795450e+00, 'ssm_a_log': 1.584390e+01, 'ssm_d': 2.268438e+01, 'ssm_norm_w': 4.282709e+01, 'ssm_w_out': 3.984376e+00, 'mla_w_in': 2.663749e+00, 'mla_q_a_norm': 1.771222e-01, 'mla_kv_a_norm': 3.534762e+00, 'mla_w_q_b': 8.599781e-02, 'mla_w_kv_b': 1.044977e+00, 'mla_q_norm': 2.377192e+00, 'mla_k_norm': 2.382645e+00, 'mla_w_out': 1.286789e+00}


def _to_microbatches(a, axis):
    t = _jnp.moveaxis(a, axis, 0)
    t = t.reshape((N_MICROBATCH, t.shape[0] // N_MICROBATCH) + t.shape[1:])
    return _jnp.moveaxis(t, 1, axis + 1)


def setup_inputs(seed: int = 0) -> dict:
    inp = _fwd_setup_inputs(seed)
    key = _jax.random.fold_in(_jax.random.key(seed), 7919)
    shape, _ = _output_shape()
    out = dict(inp)
    out["loss_target"] = _jax.random.normal(_jax.random.fold_in(key, 0), shape, _jnp.float32)
    for i, name in enumerate(TWIN_WEIGHTS):
        w = inp[name].astype(_jnp.float32)
        if MOMENT_SCALE is None:
            s = _jnp.sqrt(_jnp.mean(_jnp.square(w)) + 1e-30)
        else:
            s = MOMENT_SCALE[name]
        km, kv = _jax.random.split(_jax.random.fold_in(key, i + 1))
        out[name] = w
        out["m_" + name] = s * _jax.random.normal(km, w.shape, _jnp.float32)
        out["v_" + name] = (s * s) * _jax.random.uniform(kv, w.shape, _jnp.float32, 0.5, 1.5)
    if N_MICROBATCH > 1:
        for name, axis in PER_EXAMPLE_BATCH_AXIS.items():
            out[name] = _to_microbatches(out[name], axis)
    return {'x': out['x'], 'positions': out['positions'], 'norm_w': out['norm_w'], 'ffn_w_gate': out['ffn_w_gate'], 'ffn_w_up': out['ffn_w_up'], 'ffn_w_down': out['ffn_w_down'], 'ssm_w_in': out['ssm_w_in'], 'ssm_conv_w': out['ssm_conv_w'], 'ssm_conv_b': out['ssm_conv_b'], 'ssm_dt_bias': out['ssm_dt_bias'], 'ssm_a_log': out['ssm_a_log'], 'ssm_d': out['ssm_d'], 'ssm_norm_w': out['ssm_norm_w'], 'ssm_w_out': out['ssm_w_out'], 'mla_w_in': out['mla_w_in'], 'mla_q_a_norm': out['mla_q_a_norm'], 'mla_kv_a_norm': out['mla_kv_a_norm'], 'mla_w_q_b': out['mla_w_q_b'], 'mla_w_kv_b': out['mla_w_kv_b'], 'mla_q_norm': out['mla_q_norm'], 'mla_k_norm': out['mla_k_norm'], 'mla_w_out': out['mla_w_out'], 'loss_target': out['loss_target'], 'm_norm_w': out['m_norm_w'], 'm_ffn_w_gate': out['m_ffn_w_gate'], 'm_ffn_w_up': out['m_ffn_w_up'], 'm_ffn_w_down': out['m_ffn_w_down'], 'm_ssm_w_in': out['m_ssm_w_in'], 'm_ssm_conv_w': out['m_ssm_conv_w'], 'm_ssm_conv_b': out['m_ssm_conv_b'], 'm_ssm_dt_bias': out['m_ssm_dt_bias'], 'm_ssm_a_log': out['m_ssm_a_log'], 'm_ssm_d': out['m_ssm_d'], 'm_ssm_norm_w': out['m_ssm_norm_w'], 'm_ssm_w_out': out['m_ssm_w_out'], 'm_mla_w_in': out['m_mla_w_in'], 'm_mla_q_a_norm': out['m_mla_q_a_norm'], 'm_mla_kv_a_norm': out['m_mla_kv_a_norm'], 'm_mla_w_q_b': out['m_mla_w_q_b'], 'm_mla_w_kv_b': out['m_mla_w_kv_b'], 'm_mla_q_norm': out['m_mla_q_norm'], 'm_mla_k_norm': out['m_mla_k_norm'], 'm_mla_w_out': out['m_mla_w_out'], 'v_norm_w': out['v_norm_w'], 'v_ffn_w_gate': out['v_ffn_w_gate'], 'v_ffn_w_up': out['v_ffn_w_up'], 'v_ffn_w_down': out['v_ffn_w_down'], 'v_ssm_w_in': out['v_ssm_w_in'], 'v_ssm_conv_w': out['v_ssm_conv_w'], 'v_ssm_conv_b': out['v_ssm_conv_b'], 'v_ssm_dt_bias': out['v_ssm_dt_bias'], 'v_ssm_a_log': out['v_ssm_a_log'], 'v_ssm_d': out['v_ssm_d'], 'v_ssm_norm_w': out['v_ssm_norm_w'], 'v_ssm_w_out': out['v_ssm_w_out'], 'v_mla_w_in': out['v_mla_w_in'], 'v_mla_q_a_norm': out['v_mla_q_a_norm'], 'v_mla_kv_a_norm': out['v_mla_kv_a_norm'], 'v_mla_w_q_b': out['v_mla_w_q_b'], 'v_mla_w_kv_b': out['v_mla_w_kv_b'], 'v_mla_q_norm': out['v_mla_q_norm'], 'v_mla_k_norm': out['v_mla_k_norm'], 'v_mla_w_out': out['v_mla_w_out']}


def _loss(weights, diff, rest, loss_target):
    with _jax.named_scope("forward"):
        args = {**rest, TWIN_DIFF_INPUT: diff, **{k: w.astype(_WEIGHT_DTYPES[k]) for k, w in weights.items()}}
        y = _forward(args)
    with _jax.named_scope("loss_head"):
        err = _jnp.square(y.astype(_jnp.float32) - loss_target)
        return 0.5 * _jnp.sum(_jnp.mean(err, axis=-1)) if err.ndim else 0.5 * err


def _adamw(w, g, m, v):
    m = ADAM_B1 * m + (1.0 - ADAM_B1) * g
    v = ADAM_B2 * v + (1.0 - ADAM_B2) * _jnp.square(g)
    m_hat = m / (1.0 - ADAM_B1 ** ADAM_STEP)
    v_hat = v / (1.0 - ADAM_B2 ** ADAM_STEP)
    delta = -ADAM_LR * (m_hat / (_jnp.sqrt(v_hat) + ADAM_EPS) + ADAM_WD * w)
    return delta, m, v


def reference(x, positions, norm_w, ffn_w_gate, ffn_w_up, ffn_w_down, ssm_w_in, ssm_conv_w, ssm_conv_b, ssm_dt_bias, ssm_a_log, ssm_d, ssm_norm_w, ssm_w_out, mla_w_in, mla_q_a_norm, mla_kv_a_norm, mla_w_q_b, mla_w_kv_b, mla_q_norm, mla_k_norm, mla_w_out, loss_target, m_norm_w, m_ffn_w_gate, m_ffn_w_up, m_ffn_w_down, m_ssm_w_in, m_ssm_conv_w, m_ssm_conv_b, m_ssm_dt_bias, m_ssm_a_log, m_ssm_d, m_ssm_norm_w, m_ssm_w_out, m_mla_w_in, m_mla_q_a_norm, m_mla_kv_a_norm, m_mla_w_q_b, m_mla_w_kv_b, m_mla_q_norm, m_mla_k_norm, m_mla_w_out, v_norm_w, v_ffn_w_gate, v_ffn_w_up, v_ffn_w_down, v_ssm_w_in, v_ssm_conv_w, v_ssm_conv_b, v_ssm_dt_bias, v_ssm_a_log, v_ssm_d, v_ssm_norm_w, v_ssm_w_out, v_mla_w_in, v_mla_q_a_norm, v_mla_kv_a_norm, v_mla_w_q_b, v_mla_w_kv_b, v_mla_q_norm, v_mla_k_norm, v_mla_w_out):
    given = dict(x=x, positions=positions, norm_w=norm_w, ffn_w_gate=ffn_w_gate, ffn_w_up=ffn_w_up, ffn_w_down=ffn_w_down, ssm_w_in=ssm_w_in, ssm_conv_w=ssm_conv_w, ssm_conv_b=ssm_conv_b, ssm_dt_bias=ssm_dt_bias, ssm_a_log=ssm_a_log, ssm_d=ssm_d, ssm_norm_w=ssm_norm_w, ssm_w_out=ssm_w_out, mla_w_in=mla_w_in, mla_q_a_norm=mla_q_a_norm, mla_kv_a_norm=mla_kv_a_norm, mla_w_q_b=mla_w_q_b, mla_w_kv_b=mla_w_kv_b, mla_q_norm=mla_q_norm, mla_k_norm=mla_k_norm, mla_w_out=mla_w_out, loss_target=loss_target, m_norm_w=m_norm_w, m_ffn_w_gate=m_ffn_w_gate, m_ffn_w_up=m_ffn_w_up, m_ffn_w_down=m_ffn_w_down, m_ssm_w_in=m_ssm_w_in, m_ssm_conv_w=m_ssm_conv_w, m_ssm_conv_b=m_ssm_conv_b, m_ssm_dt_bias=m_ssm_dt_bias, m_ssm_a_log=m_ssm_a_log, m_ssm_d=m_ssm_d, m_ssm_norm_w=m_ssm_norm_w, m_ssm_w_out=m_ssm_w_out, m_mla_w_in=m_mla_w_in, m_mla_q_a_norm=m_mla_q_a_norm, m_mla_kv_a_norm=m_mla_kv_a_norm, m_mla_w_q_b=m_mla_w_q_b, m_mla_w_kv_b=m_mla_w_kv_b, m_mla_q_norm=m_mla_q_norm, m_mla_k_norm=m_mla_k_norm, m_mla_w_out=m_mla_w_out, v_norm_w=v_norm_w, v_ffn_w_gate=v_ffn_w_gate, v_ffn_w_up=v_ffn_w_up, v_ffn_w_down=v_ffn_w_down, v_ssm_w_in=v_ssm_w_in, v_ssm_conv_w=v_ssm_conv_w, v_ssm_conv_b=v_ssm_conv_b, v_ssm_dt_bias=v_ssm_dt_bias, v_ssm_a_log=v_ssm_a_log, v_ssm_d=v_ssm_d, v_ssm_norm_w=v_ssm_norm_w, v_ssm_w_out=v_ssm_w_out, v_mla_w_in=v_mla_w_in, v_mla_q_a_norm=v_mla_q_a_norm, v_mla_kv_a_norm=v_mla_kv_a_norm, v_mla_w_q_b=v_mla_w_q_b, v_mla_w_kv_b=v_mla_w_kv_b, v_mla_q_norm=v_mla_q_norm, v_mla_k_norm=v_mla_k_norm, v_mla_w_out=v_mla_w_out)
    weights = {n: given[n] for n in TWIN_WEIGHTS}
    shared = {n: given[n] for n in SHARED_INPUTS}
    per_example = {n: given[n] for n in ['x', 'positions']}
    grad_fn = _jax.value_and_grad(_loss, argnums=(0, 1))

    def one_microbatch(ex, loss_target):
        ex = dict(ex)
        diff = ex.pop(TWIN_DIFF_INPUT)
        return grad_fn(weights, diff, {**shared, **ex}, loss_target)

    if N_MICROBATCH == 1:
        loss, (grad_w, grad_x) = one_microbatch(per_example, given["loss_target"])
    else:
        def body(carry, xs):
            loss_sum, grad_sum = carry
            l_k, (gw_k, gx_k) = one_microbatch(xs[0], xs[1])
            with _jax.named_scope("update"):
                return (loss_sum + l_k, _jax.tree.map(_jnp.add, grad_sum, gw_k)), gx_k

        init = (_jnp.zeros((), _jnp.float32), _jax.tree.map(_jnp.zeros_like, weights))
        (loss, grad_w), grad_x = _jax.lax.scan(body, init, (per_example, given["loss_target"]))
    with _jax.named_scope("update"):
        delta_w, new_m, new_v = {}, {}, {}
        for n in TWIN_WEIGHTS:
            delta_w[n], new_m[n], new_v[n] = _adamw(weights[n], grad_w[n], given["m_" + n], given["v_" + n])
    return (loss, grad_x, *[grad_w[n] for n in TWIN_WEIGHTS], *[delta_w[n] for n in TWIN_WEIGHTS],
            *[new_m[n] for n in TWIN_WEIGHTS], *[new_v[n] for n in TWIN_WEIGHTS])
```

```python
import functools
import math

import jax
import jax.numpy as jnp
import numpy as np
from jax import lax
from jax.experimental import pallas as pl
from jax.experimental.pallas import tpu as pltpu

F32 = jnp.float32
BF = jnp.bfloat16
SDS = jax.ShapeDtypeStruct

N_DEV = 8
D_MODEL = 1024
D_FF = 2816
FF_SHARD = D_FF // N_DEV
D_INNER = 2048
SSM_HEADS = 32
SSM_GROUPS = 8
SSM_HPG = 4
SSM_STATE = 128
CONV_K = 4
CONV_DIM = 4096
SSM_IN_DIM = 6176
SSM_IN_SHARD = SSM_IN_DIM // N_DEV
NORM_GROUP = 256
CHUNK = 128
MLA_HEADS = 16
Q_LORA = 384
KV_LORA = 256
QK_NOPE = 64
QK_ROPE = 32
QK_DIM = 96
MLA_IN_DIM = 672
HEAD_PAD = 128
ROPE_THETA = 10000.0
EPS = 1e-6
LANES = 128

ADAM_LR = 0.001
ADAM_B1 = 0.9
ADAM_B2 = 0.999
ADAM_EPS = 1e-08
ADAM_WD = 0.01
ADAM_STEP = 10

VMEM_BIG = 56 * 1024 * 1024

NN = ((1,), (0,))
NT = ((1,), (1,))
TN = ((0,), (0,))


def _dotf(a, b, dn):
    return lax.dot_general(a.astype(BF), b.astype(BF), (dn, ((), ())), preferred_element_type=F32)


def _dot_hi(a, b, dn=NN):
    return lax.dot_general(a, b, (dn, ((), ())), precision=lax.Precision.HIGHEST, preferred_element_type=F32)


def _sigmoid(x):
    return 1.0 / (1.0 + jnp.exp(-x))


def _silu(x):
    return x * _sigmoid(x)


def _softplus(x):
    return jnp.maximum(x, 0.0) + jnp.log(1.0 + jnp.exp(-jnp.abs(x)))


def _cparams(n_grid, vmem=None):
    return pltpu.CompilerParams(dimension_semantics=("arbitrary",) * n_grid, vmem_limit_bytes=vmem)


def _fmm(name, grid_mn, pairs, outs, *, epi=None, extras=(), n_acc=1, acc_shape=None, vmem=None, alias=None,
         joint=False):
    if joint:
        nk_total = pairs[0][7]
        assert all(p[7] == nk_total for p in pairs)
        starts = [0] * len(pairs)
    else:
        nk_total = sum(p[7] for p in pairs)
        starts = []
        s = 0
        for p in pairs:
            starts.append(s)
            s += p[7]
    n_pairs, n_extras, n_outs = len(pairs), len(extras), len(outs)
    single = nk_total == 1

    def body(*refs):
        ab_refs = refs[: 2 * n_pairs]
        e_refs = refs[2 * n_pairs: 2 * n_pairs + n_extras]
        pos = 2 * n_pairs + n_extras + (1 if alias is not None else 0)
        o_refs = refs[pos: pos + n_outs]
        acc_refs = refs[pos + n_outs:]
        i, j, k = pl.program_id(0), pl.program_id(1), pl.program_id(2)

        def finish(accs):
            res = epi(accs, *[e[...] for e in e_refs]) if epi is not None else accs
            if not isinstance(res, (tuple, list)):
                res = (res,)
            first = (i == 0) & (j == 0)
            for o, r, spec in zip(o_refs, res, outs):
                if spec[3]:
                    @pl.when(first)
                    def _(o=o, r=r):
                        o[...] = r.astype(o.dtype)

                    @pl.when(jnp.logical_not(first))
                    def _(o=o, r=r):
                        o[...] += r.astype(o.dtype)
                else:
                    o[...] = r.astype(o.dtype)

        if single:
            accs = [None] * n_acc
            for p, pr in enumerate(pairs):
                d = _dotf(ab_refs[2 * p][...], ab_refs[2 * p + 1][...], pr[6])
                accs[pr[8]] = d if accs[pr[8]] is None else accs[pr[8]] + d
            finish(accs)
            return

        @pl.when(k == 0)
        def _():
            for a in acc_refs:
                a[...] = jnp.zeros_like(a)

        for p, pr in enumerate(pairs):
            def step(p=p, pr=pr):
                acc_refs[pr[8]][...] += _dotf(ab_refs[2 * p][...], ab_refs[2 * p + 1][...], pr[6])

            if n_pairs == 1 or joint:
                step()
            else:
                pl.when((k >= starts[p]) & (k < starts[p] + pr[7]))(step)

        @pl.when(k == nk_total - 1)
        def _():
            finish([a[...] for a in acc_refs])

    in_specs, args = [], []
    for p, pr in enumerate(pairs):
        a, a_blk, a_idx, b, b_blk, b_idx, _, nk, _ = pr
        st = starts[p]

        def amap(i, j, k, a_idx=a_idx, st=st, nk=nk):
            return a_idx(i, j, jnp.clip(k - st, 0, nk - 1))

        def bmap(i, j, k, b_idx=b_idx, st=st, nk=nk):
            return b_idx(i, j, jnp.clip(k - st, 0, nk - 1))

        in_specs += [pl.BlockSpec(a_blk, amap), pl.BlockSpec(b_blk, bmap)]
        args += [a, b]
    for arr, blk, idx in extras:
        in_specs.append(pl.BlockSpec(blk, lambda i, j, k, idx=idx: idx(i, j)))
        args.append(arr)
    io_alias = {}
    if alias is not None:
        in_specs.append(pl.BlockSpec(memory_space=pl.ANY))
        io_alias = {len(args): 0}
        args.append(alias)
    out_specs = [pl.BlockSpec(blk, lambda i, j, k, idx=idx: idx(i, j)) for _, blk, idx, _ in outs]
    out_shape = [o[0] for o in outs]
    scratch = [] if single else [pltpu.VMEM(acc_shape, F32) for _ in range(n_acc)]
    res = pl.pallas_call(
        body, name=name, grid=(grid_mn[0], grid_mn[1], nk_total), in_specs=in_specs, out_specs=out_specs,
        out_shape=out_shape, scratch_shapes=scratch, input_output_aliases=io_alias,
        compiler_params=_cparams(3, vmem),
    )(*args)
    return res


def _ew(name, grid, fn, ins, outs, *, acc_axes=(), vmem=None):
    n_in = len(ins)

    def body(*refs):
        res = fn(*[r[...] for r in refs[:n_in]])
        if not isinstance(res, (tuple, list)):
            res = (res,)
        first = None
        for ax in acc_axes:
            c = pl.program_id(ax) == 0
            first = c if first is None else (first & c)
        for o, r, spec in zip(refs[n_in:], res, outs):
            if spec[3]:
                @pl.when(first)
                def _(o=o, r=r):
                    o[...] = r.astype(o.dtype)

                @pl.when(jnp.logical_not(first))
                def _(o=o, r=r):
                    o[...] += r.astype(o.dtype)
            else:
                o[...] = r.astype(o.dtype)

    return pl.pallas_call(
        body, name=name, grid=grid,
        in_specs=[pl.BlockSpec(blk, idx) for _, blk, idx in ins],
        out_specs=[pl.BlockSpec(blk, idx) for _, blk, idx, _ in outs],
        out_shape=[o[0] for o in outs],
        compiler_params=_cparams(len(grid), vmem),
    )(*[a for a, _, _ in ins])


def _row_tile(t, want):
    tm = min(want, t)
    assert t % tm == 0, (t, tm)
    return tm


def _rms_fn(x, w):
    return x * lax.rsqrt(jnp.mean(x * x, axis=-1, keepdims=True) + EPS) * w


def _rms_fwd(name, x, w):
    t, d = x.shape
    tm = _row_tile(t, 512)
    return _ew(name, (t // tm,), _rms_fn,
               [(x, (tm, d), lambda i: (i, 0)), (w, (1, d), lambda i: (0, 0))],
               [(SDS((t, d), BF), (tm, d), lambda i: (i, 0), False)])[0]


def _rms_bwd(name, x, w, dh, dres):
    t, d = x.shape
    tm = _row_tile(t, 512)

    def fn(x, w, dh, dres):
        _, vjp = jax.vjp(_rms_fn, x, w)
        dx, dw = vjp(dh.astype(F32))
        return dx + dres, dw

    row = lambda i: (i, 0)
    return _ew(name, (t // tm,), fn,
               [(x, (tm, d), row), (w, (1, d), lambda i: (0, 0)), (dh, (tm, d), row), (dres, (tm, d), row)],
               [(SDS((t, d), F32), (tm, d), row, False), (SDS((1, d), F32), (1, d), lambda i: (0, 0), True)],
               acc_axes=(0,))


def _loss_and_grad(y, target):
    t, d = y.shape
    tm = _row_tile(t, 512)

    def fn(y, tg):
        e = y - tg
        return e * (1.0 / d), jnp.sum(e * e, axis=0, keepdims=True) * (0.5 / d)

    row = lambda i: (i, 0)
    return _ew("loss_head", (t // tm,), fn, [(y, (tm, d), row), (target, (tm, d), row)],
               [(SDS((t, d), F32), (tm, d), row, False), (SDS((1, d), F32), (1, d), lambda i: (0, 0), True)],
               acc_axes=(0,))


def _ffn_fwd(tag, x, nw, wf, n):
    t, d = x.shape
    h = _rms_fwd(tag + "_rms", x, nw)
    tm, tn = _row_tile(t, 1024), 256

    def epi(accs):
        g, u = accs
        return g, u, _silu(g) * u

    hblk = (h, (tm, d), lambda i, j, k: (i, 0))
    col = lambda i, j: (i, j)
    g, u, a = _fmm(
        tag + "_up", (t // tm, D_FF // tn),
        [hblk + (wf, (None, tn, d), lambda i, j, k: (3 * n, j, 0), NT, 1, 0),
         hblk + (wf, (None, tn, d), lambda i, j, k: (3 * n + 1, j, 0), NT, 1, 1)],
        [(SDS((t, D_FF), BF), (tm, tn), col, False)] * 3, epi=epi, n_acc=2, joint=True)
    tm2 = _row_tile(t, 512)
    y = _fmm(
        tag + "_down", (t // tm2, 1),
        [(a, (tm2, D_FF), lambda i, j, k: (i, 0), wf, (None, D_FF, d), lambda i, j, k: (3 * n + 2, 0, 0), NN, 1, 0)],
        [(SDS((t, d), F32), (tm2, d), lambda i, j: (i, 0), False)],
        extras=[(x, (tm2, d), lambda i, j: (i, 0))],
        epi=lambda accs, xr: xr + 0.5 * accs[0], vmem=VMEM_BIG)[0]
    return y, (x, h, g, u, a)


def _wgrad(name, a, b, m, n, *, tm, tn, tk=512, scale=None, out=None, out_idx=None, out_dtype=BF):
    t = a.shape[0]
    tk = _row_tile(t, tk)
    epi = (lambda accs: accs[0] * scale) if scale is not None else None
    if out is None:
        spec = (SDS((m, n), out_dtype), (tm, tn), lambda i, j: (i, j), False)
    else:
        spec = (SDS(out.shape, out.dtype), (None, tm, tn), lambda i, j: (out_idx, i, j), False)
    return _fmm(name, (m // tm, n // tn),
                [(a, (tk, tm), lambda i, j, k: (k, i), b, (tk, tn), lambda i, j, k: (k, j), TN, t // tk, 0)],
                [spec], epi=epi, acc_shape=(tm, tn), alias=out, vmem=VMEM_BIG)[0]


def _ffn_bwd(tag, dy, nw, wf, n, saved, dg_slab):
    x, h, g, u, a = saved
    t, d = x.shape
    tm, tn = _row_tile(t, 1024), 256

    def epi(accs, g, u):
        da = 0.5 * accs[0]
        g = g.astype(F32)
        u = u.astype(F32)
        s = _sigmoid(g)
        return da * u * (s * (1.0 + g * (1.0 - s))), da * (g * s)

    col = lambda i, j: (i, j)
    dg, du = _fmm(
        tag + "_bwd_act", (t // tm, D_FF // tn),
        [(dy, (tm, d), lambda i, j, k: (i, 0), wf, (None, tn, d), lambda i, j, k: (3 * n + 2, j, 0), NT, 1, 0)],
        [(SDS((t, D_FF), BF), (tm, tn), col, False)] * 2,
        extras=[(g, (tm, tn), col), (u, (tm, tn), col)], epi=epi)
    tm2 = _row_tile(t, 256)
    dh = _fmm(
        tag + "_bwd_dh", (t // tm2, 1),
        [(dg, (tm2, D_FF), lambda i, j, k: (i, 0), wf, (None, D_FF, d), lambda i, j, k: (3 * n, 0, 0), NN, 1, 0),
         (du, (tm2, D_FF), lambda i, j, k: (i, 0), wf, (None, D_FF, d), lambda i, j, k: (3 * n + 1, 0, 0), NN, 1, 0)],
        [(SDS((t, d), BF), (tm2, d), lambda i, j: (i, 0), False)], vmem=VMEM_BIG, joint=True)[0]
    dx, dnw = _rms_bwd(tag + "_bwd_rms", x, nw, dh, dy)
    half = D_FF // 2
    dg_slab = _wgrad(tag + "_wg", dg, h, D_FF, d, tm=half, tn=d, out=dg_slab, out_idx=3 * n)
    dg_slab = _wgrad(tag + "_wu", du, h, D_FF, d, tm=half, tn=d, out=dg_slab, out_idx=3 * n + 1)
    dg_slab = _wgrad(tag + "_wd", a, dy, D_FF, d, tm=half, tn=d, scale=0.5, out=dg_slab, out_idx=3 * n + 2)
    return dx, dnw, dg_slab


def _shift_down(cur, prev8, j):
    rolled = pltpu.roll(cur, j, 0)
    sub = lax.broadcasted_iota(jnp.int32, prev8.shape, 0)
    top = jnp.where(sub < j, pltpu.roll(prev8, j, 0), rolled[:8])
    return jnp.concatenate([top, rolled[8:]], axis=0)


def _shift_up(cur, next8, j):
    n = cur.shape[0]
    rolled = pltpu.roll(cur, n - j, 0)
    sub = lax.broadcasted_iota(jnp.int32, next8.shape, 0)
    bot = jnp.where(sub >= 8 - j, pltpu.roll(next8, 8 - j, 0), rolled[n - 8:])
    return jnp.concatenate([rolled[: n - 8], bot], axis=0)


HALO = 16


def _conv_fwd(xbc, w, b, seq):
    t, c = xbc.shape
    ts, tc = _row_tile(seq, 256), 512
    tiles_per_seq = seq // ts
    hb = ts // HALO

    def fn(cur, prev, w, b):
        i = pl.program_id(1)
        cur = cur.astype(F32)
        prev8 = jnp.where(i % tiles_per_seq == 0, 0.0, prev.astype(F32)[HALO - 8:])
        out = b + w[3:4] * cur
        for j in range(1, CONV_K):
            out = out + w[3 - j:4 - j] * _shift_down(cur, prev8, j)
        return out, _silu(out)

    return _ew("ssm_conv_fwd", (c // tc, t // ts), fn,
               [(xbc, (ts, tc), lambda j, i: (i, j)),
                (xbc, (HALO, tc), lambda j, i: (jnp.maximum(i * hb - 1, 0), j)),
                (w, (CONV_K, tc), lambda j, i: (0, j)), (b, (1, tc), lambda j, i: (0, j))],
               [(SDS((t, c), BF), (ts, tc), lambda j, i: (i, j), False)] * 2)


def _conv_bwd(tag, dxa, cpre, xbc, w, col0, seq):
    t, width = dxa.shape
    ts, tc = _row_tile(seq, 256), 512
    tiles_per_seq = seq // ts
    hb = ts // HALO
    cb0 = col0 // tc
    n_halo_blocks = t // HALO

    def dsilu(cv, dv):
        cv = cv.astype(F32)
        s = _sigmoid(cv)
        return dv.astype(F32) * (s * (1.0 + cv * (1.0 - s)))

    def fn(dxa_c, dxa_n, c_c, c_n, x_c, x_p, w):
        i = pl.program_id(1)
        dc = dsilu(c_c, dxa_c)
        last = i % tiles_per_seq == tiles_per_seq - 1
        dc_n = jnp.where(last, 0.0, dsilu(c_n, dxa_n)[:8])
        dx = w[3:4] * dc
        for j in range(1, CONV_K):
            dx = dx + w[3 - j:4 - j] * _shift_up(dc, dc_n, j)
        cur = x_c.astype(F32)
        prev8 = jnp.where(i % tiles_per_seq == 0, 0.0, x_p.astype(F32)[HALO - 8:])
        rows = [jnp.sum(dc * cur, axis=0, keepdims=True)]
        for j in range(1, CONV_K):
            rows.append(jnp.sum(dc * _shift_down(cur, prev8, j), axis=0, keepdims=True))
        sub8 = lax.broadcasted_iota(jnp.int32, (8, dc.shape[1]), 0)
        dw = jnp.zeros((8, dc.shape[1]), F32)
        for kk in range(CONV_K):
            dw = jnp.where(sub8 == kk, rows[CONV_K - 1 - kk], dw)
        return dx, dw, jnp.sum(dc, axis=0, keepdims=True)

    nxt = lambda j, i: (jnp.minimum((i + 1) * hb, n_halo_blocks - 1), j)
    nxt_off = lambda j, i: (jnp.minimum((i + 1) * hb, n_halo_blocks - 1), j + cb0)
    return _ew(tag, (width // tc, t // ts), fn,
               [(dxa, (ts, tc), lambda j, i: (i, j)), (dxa, (HALO, tc), nxt),
                (cpre, (ts, tc), lambda j, i: (i, j + cb0)), (cpre, (HALO, tc), nxt_off),
                (xbc, (ts, tc), lambda j, i: (i, j + cb0)),
                (xbc, (HALO, tc), lambda j, i: (jnp.maximum(i * hb - 1, 0), j + cb0)),
                (w, (CONV_K, tc), lambda j, i: (0, j + cb0))],
               [(SDS((t, width), BF), (ts, tc), lambda j, i: (i, j), False),
                (SDS((8, width), F32), (8, tc), lambda j, i: (0, j), True),
                (SDS((1, width), F32), (1, tc), lambda j, i: (0, j), True)],
               acc_axes=(1,))


def _ssd_chunk(xs, bm, cm, dtr, st, dtb, alog, dsk):
    ell = xs.shape[0]
    xs = xs.astype(F32)
    lane = lax.broadcasted_iota(jnp.int32, (ell, LANES), 1)
    sub = lax.broadcasted_iota(jnp.int32, (ell, LANES), 0)
    lane1 = lax.broadcasted_iota(jnp.int32, (1, LANES), 1)
    causal = sub >= lane
    dt = _softplus(dtr + dtb)
    da = dt * (-jnp.exp(alog))
    acs = _dot_hi(causal.astype(F32), da)
    acs_t = acs.T
    cb = _dotf(cm, bm, NT)
    lo = lane < 64
    ys, news = [], []
    for p in range(2):
        xp = xs[:, LANES * p:LANES * (p + 1)]
        sp = st[LANES * p:LANES * (p + 1), :]
        col, dtc, last, dsel = [], [], [], []
        y_diag = None
        for q in range(2):
            r = 2 * p + q
            col_r = jnp.sum(jnp.where(lane == r, acs, 0.0), axis=1, keepdims=True)
            row_r = jnp.sum(jnp.where(sub == r, acs_t, 0.0), axis=0, keepdims=True)
            dtc_r = jnp.sum(jnp.where(lane == r, dt, 0.0), axis=1, keepdims=True)
            decay = jnp.exp(jnp.where(causal, col_r - row_r, -jnp.inf))
            head = lo if q == 0 else jnp.logical_not(lo)
            d = _dotf(cb * decay, jnp.where(head, xp * dtc_r, 0.0), NN)
            y_diag = d if y_diag is None else y_diag + d
            col.append(col_r)
            dtc.append(dtc_r)
            last.append(jnp.sum(jnp.where(sub[:, :1] == ell - 1, col_r, 0.0), axis=0, keepdims=True))
            dsel.append(jnp.sum(jnp.where(lane1 == r, dsk, 0.0), axis=1, keepdims=True))
        y_off = _dotf(cm, sp, NT) * jnp.where(lo, jnp.exp(col[0]), jnp.exp(col[1]))
        xw = jnp.where(lo, xp * (dtc[0] * jnp.exp(last[0] - col[0])), xp * (dtc[1] * jnp.exp(last[1] - col[1])))
        new = sp * jnp.where(sub < 64, jnp.exp(last[0]), jnp.exp(last[1])) + _dotf(xw, bm, TN)
        ys.append(y_diag + y_off + jnp.where(lo, dsel[0], dsel[1]) * xp)
        news.append(new)
    return jnp.concatenate(ys, axis=1), jnp.concatenate(news, axis=0)


def _ssd_specs(nb, nc):
    row = lambda g, b, c: (b * nc + c, g)
    return row, [
        ((CHUNK, 2 * LANES), row),
        ((CHUNK, LANES), lambda g, b, c: (b * nc + c, 16 + g)),
        ((CHUNK, LANES), lambda g, b, c: (b * nc + c, 24 + g)),
    ]


def _ssd_fwd(xa, dtr, dtb, alog, dsk, nb, seq):
    t = xa.shape[0]
    nc = seq // CHUNK
    row, xspecs = _ssd_specs(nb, nc)
    par = ((1, LANES), lambda g, b, c: (0, g))

    def body(xs, bm, cm, dtr, dtb, alog, dsk, y_ref, st_out, st_ref):
        @pl.when(pl.program_id(2) == 0)
        def _():
            st_ref[...] = jnp.zeros_like(st_ref)

        st = st_ref[...]
        st_out[...] = st
        y, new = _ssd_chunk(xs[...], bm[...], cm[...], dtr[...], st, dtb[...], alog[...], dsk[...])
        y_ref[...] = y
        st_ref[...] = new

    specs = [pl.BlockSpec(b, i) for b, i in xspecs] + [pl.BlockSpec((CHUNK, LANES), row)] + [pl.BlockSpec(*par)] * 3
    return pl.pallas_call(
        body, name="ssd_fwd", grid=(SSM_GROUPS, nb, nc), in_specs=specs,
        out_specs=[pl.BlockSpec((CHUNK, 2 * LANES), row),
                   pl.BlockSpec((None, None, None, 2 * LANES, LANES), lambda g, b, c: (g, b, c, 0, 0))],
        out_shape=[SDS((t, D_INNER), F32), SDS((SSM_GROUPS, nb, nc, 2 * LANES, LANES), F32)],
        scratch_shapes=[pltpu.VMEM((2 * LANES, LANES), F32)],
        compiler_params=_cparams(3),
    )(xa, xa, xa, dtr, dtb, alog, dsk)


def _ssd_bwd(xa, dtr, dtb, alog, dsk, states, dy, nb, seq):
    t = xa.shape[0]
    nc = seq // CHUNK
    rev = lambda c: nc - 1 - c
    row = lambda g, b, c: (b * nc + rev(c), g)
    par = ((1, LANES), lambda g, b, c: (0, g))

    def body(xs, bm, cm, dtr, dtb, alog, dsk, st_in, dy, dxs, dbm, dcm, ddtr, ddtb, dalog, ddsk, dst_ref):
        @pl.when(pl.program_id(2) == 0)
        def _():
            dst_ref[...] = jnp.zeros_like(dst_ref)

        _, vjp = jax.vjp(_ssd_chunk, xs[...], bm[...], cm[...], dtr[...], st_in[...], dtb[...], alog[...], dsk[...])
        g = vjp((dy[...], dst_ref[...]))
        dxs[...] = g[0]
        dbm[...] = g[1]
        dcm[...] = g[2]
        ddtr[...] = g[3]
        dst_ref[...] = g[4]
        first = (pl.program_id(1) == 0) & (pl.program_id(2) == 0)
        for o, v in ((ddtb, g[5]), (dalog, g[6]), (ddsk, g[7])):
            @pl.when(first)
            def _(o=o, v=v):
                o[...] = v

            @pl.when(jnp.logical_not(first))
            def _(o=o, v=v):
                o[...] += v

    in_specs = [
        pl.BlockSpec((CHUNK, 2 * LANES), row),
        pl.BlockSpec((CHUNK, LANES), lambda g, b, c: (b * nc + rev(c), 16 + g)),
        pl.BlockSpec((CHUNK, LANES), lambda g, b, c: (b * nc + rev(c), 24 + g)),
        pl.BlockSpec((CHUNK, LANES), row),
        pl.BlockSpec(*par), pl.BlockSpec(*par), pl.BlockSpec(*par),
        pl.BlockSpec((None, None, None, 2 * LANES, LANES), lambda g, b, c: (g, b, rev(c), 0, 0)),
        pl.BlockSpec((CHUNK, 2 * LANES), row),
    ]
    out_specs = [pl.BlockSpec((CHUNK, 2 * LANES), row), pl.BlockSpec((CHUNK, LANES), row),
                 pl.BlockSpec((CHUNK, LANES), row), pl.BlockSpec((CHUNK, LANES), row),
                 pl.BlockSpec(*par), pl.BlockSpec(*par), pl.BlockSpec(*par)]
    out_shape = [SDS((t, D_INNER), BF), SDS((t, SSM_GROUPS * LANES), BF), SDS((t, SSM_GROUPS * LANES), BF),
                 SDS((t, SSM_GROUPS * LANES), F32)] + [SDS((1, SSM_GROUPS * LANES), F32)] * 3
    return pl.pallas_call(
        body, name="ssd_bwd", grid=(SSM_GROUPS, nb, nc), in_specs=in_specs, out_specs=out_specs,
        out_shape=out_shape, scratch_shapes=[pltpu.VMEM((2 * LANES, LANES), F32)],
        compiler_params=_cparams(3, VMEM_BIG),
    )(xa, xa, xa, dtr, dtb, alog, dsk, states, dy)


def _gated_fn(y, z, w):
    g = y * _silu(z.astype(F32))
    return g * lax.rsqrt(jnp.mean(g * g, axis=-1, keepdims=True) + EPS) * w


def _gated_norm_fwd(y, z, w):
    t = y.shape[0]
    tm = _row_tile(t, 512)
    blk = ((tm, NORM_GROUP), lambda g, i: (i, g))
    return _ew("ssm_gnorm_fwd", (SSM_GROUPS, t // tm), _gated_fn,
               [(y,) + blk, (z,) + blk, (w, (1, NORM_GROUP), lambda g, i: (0, g))],
               [(SDS((t, D_INNER), BF),) + blk + (False,)])[0]


def _gated_norm_bwd(y, z, w, dout):
    t = y.shape[0]
    tm = _row_tile(t, 512)
    blk = ((tm, NORM_GROUP), lambda g, i: (i, g))
    par = ((1, NORM_GROUP), lambda g, i: (0, g))

    def fn(y, z, w, dout):
        _, vjp = jax.vjp(_gated_fn, y, z, w)
        return vjp(dout.astype(F32))

    return _ew("ssm_gnorm_bwd", (SSM_GROUPS, t // tm), fn,
               [(y,) + blk, (z,) + blk, (w,) + par, (dout,) + blk],
               [(SDS((t, D_INNER), F32),) + blk + (False,), (SDS((t, D_INNER), BF),) + blk + (False,),
                (SDS((1, D_INNER), F32),) + par + (True,)],
               acc_axes=(1,))


def _proj_nt(name, h, wt, n, out_dtype, tn=256):
    t, kdim = h.shape
    tm = _row_tile(t, 1024)
    return _fmm(name, (t // tm, n // tn),
                [(h, (tm, kdim), lambda i, j, k: (i, 0), wt, (tn, kdim), lambda i, j, k: (j, 0), NT, 1, 0)],
                [(SDS((t, n), out_dtype), (tm, tn), lambda i, j: (i, j), False)])[0]


def _seg_nn(name, parts, n, out_dtype, tk=256):
    t = parts[0][0].shape[0]
    tm = _row_tile(t, 512)
    pairs = []
    for a, w, row0 in parts:
        kp = a.shape[1]
        tkp = min(tk, kp)
        r0 = row0 // tkp
        pairs.append((a, (tm, tkp), lambda i, j, k: (i, k), w, (tkp, n), lambda i, j, k, r0=r0: (k + r0, 0),
                      NN, kp // tkp, 0))
    return _fmm(name, (t // tm, 1), pairs, [(SDS((t, n), out_dtype), (tm, n), lambda i, j: (i, 0), False)],
                acc_shape=(tm, n))[0]


def _ssm_fwd(x, nw, ws, small, nb, seq):
    t, d = x.shape
    h = _rms_fwd("ssm_rms", x, nw)
    z = _proj_nt("ssm_in_z", h, ws["z_t"], D_INNER, BF)
    xbc = _proj_nt("ssm_in_xbc", h, ws["xbc_t"], CONV_DIM, BF)
    dtr = _proj_nt("ssm_in_dt", h, ws["dt_t"], SSM_GROUPS * LANES, F32)
    cpre, xa = _conv_fwd(xbc, small["conv_w"], small["conv_b"], seq)
    y, states = _ssd_fwd(xa, dtr, small["dt_bias"], small["a_log"], small["d_skip"], nb, seq)
    gn = _gated_norm_fwd(y, z, small["ssm_norm_w"])
    tm = _row_tile(t, 512)
    out = _fmm("ssm_out", (t // tm, 1),
               [(gn, (tm, D_INNER), lambda i, j, k: (i, 0), ws["out"], (D_INNER, d), lambda i, j, k: (0, 0), NN, 1, 0)],
               [(SDS((t, d), F32), (tm, d), lambda i, j: (i, 0), False)],
               extras=[(x, (tm, d), lambda i, j: (i, 0))], epi=lambda accs, xr: xr + accs[0])[0]
    return out, (x, h, z, xbc, dtr, cpre, xa, y, states, gn)


def _ssm_bwd(dy, nw, ws, small, saved, nb, seq):
    x, h, z, xbc, dtr, cpre, xa, y, states, gn = saved
    t, d = x.shape
    dgn = _proj_nt("ssm_bwd_dgn", dy, ws["out"], D_INNER, BF)
    d_out = _wgrad("ssm_w_out_g", gn, dy, D_INNER, d, tm=D_INNER // 2, tn=d)
    dyssd, dz, d_normw = _gated_norm_bwd(y, z, small["ssm_norm_w"], dgn)
    dxs, dbm, dcm, ddtr, d_dtb, d_alog, d_dsk = _ssd_bwd(
        xa, dtr, small["dt_bias"], small["a_log"], small["d_skip"], states, dyssd, nb, seq)
    dxbc_x, dcw_x, dcb_x = _conv_bwd("ssm_conv_bwd_x", dxs, cpre, xbc, small["conv_w"], 0, seq)
    dxbc_b, dcw_b, dcb_b = _conv_bwd("ssm_conv_bwd_b", dbm, cpre, xbc, small["conv_w"], D_INNER, seq)
    dxbc_c, dcw_c, dcb_c = _conv_bwd("ssm_conv_bwd_c", dcm, cpre, xbc, small["conv_w"], D_INNER + 1024, seq)
    dh = _seg_nn("ssm_bwd_dh", [(dz, ws["z_t"], 0), (dxbc_x, ws["xbc_t"], 0), (dxbc_b, ws["xbc_t"], D_INNER),
                                (dxbc_c, ws["xbc_t"], D_INNER + 1024), (ddtr, ws["dt_t"], 0)], d, BF)
    dx, dnw = _rms_bwd("ssm_bwd_rms", x, nw, dh, dy)
    g = {
        "z_t": _wgrad("ssm_w_z_g", dz, h, D_INNER, d, tm=1024, tn=d, out_dtype=F32),
        "x_t": _wgrad("ssm_w_x_g", dxbc_x, h, D_INNER, d, tm=1024, tn=d, out_dtype=F32),
        "b_t": _wgrad("ssm_w_b_g", dxbc_b, h, 1024, d, tm=1024, tn=d, out_dtype=F32),
        "c_t": _wgrad("ssm_w_c_g", dxbc_c, h, 1024, d, tm=1024, tn=d, out_dtype=F32),
        "dt_t": _wgrad("ssm_w_dt_g", ddtr, h, 1024, d, tm=1024, tn=d, out_dtype=F32),
        "out": d_out,
        "conv_w": jnp.concatenate([dcw_x[:CONV_K], dcw_b[:CONV_K], dcw_c[:CONV_K]], axis=1),
        "conv_b": jnp.concatenate([dcb_x, dcb_b, dcb_c], axis=1),
        "dt_bias": d_dtb, "a_log": d_alog, "d_skip": d_dsk, "ssm_norm_w": d_normw,
    }
    return dx, dnw, g


def _lane_masks(shape):
    lane = lax.broadcasted_iota(jnp.int32, shape, len(shape) - 1)
    return lane < QK_NOPE, (lane >= QK_NOPE) & (lane < QK_DIM)


def _swap_matrix():
    p = np.zeros((HEAD_PAD, HEAD_PAD), np.float32)
    for i in range(QK_ROPE // 2):
        p[QK_NOPE + QK_ROPE // 2 + i, QK_NOPE + i] = 1.0
        p[QK_NOPE + i, QK_NOPE + QK_ROPE // 2 + i] = 1.0
    return jnp.asarray(p)


def _rope_tables(positions_col):
    t = positions_col.shape[0]
    tm = _row_tile(t, 512)
    freq = np.zeros((1, HEAD_PAD), np.float32)
    inv = 1.0 / (ROPE_THETA ** (np.arange(0, QK_ROPE, 2, dtype=np.float32) / QK_ROPE))
    freq[0, QK_NOPE:QK_NOPE + QK_ROPE // 2] = inv
    freq[0, QK_NOPE + QK_ROPE // 2:QK_DIM] = inv
    sign = np.zeros((1, HEAD_PAD), np.float32)
    sign[0, QK_NOPE:QK_NOPE + QK_ROPE // 2] = -1.0
    sign[0, QK_NOPE + QK_ROPE // 2:QK_DIM] = 1.0

    def fn(pos, freq, sign):
        ang = pos.astype(F32) * freq
        nope, rope = _lane_masks(ang.shape)
        return jnp.where(nope, 1.0, jnp.where(rope, jnp.cos(ang), 0.0)), jnp.sin(ang) * sign

    row = lambda i: (i, 0)
    par = ((1, HEAD_PAD), lambda i: (0, 0))
    return _ew("mla_rope_tables", (t // tm,), fn,
               [(positions_col, (tm, 1), row), (jnp.asarray(freq),) + par, (jnp.asarray(sign),) + par],
               [(SDS((t, HEAD_PAD), F32), (tm, HEAD_PAD), row, False)] * 2)


def _rope(xn, cos, sin_signed, swap):
    return xn * cos + _dot_hi(xn, swap) * sin_signed


def _krope_fn(kr, w, cos, sin_signed, swap):
    _, rope = _lane_masks(kr.shape)
    ss = jnp.sum(jnp.where(rope, kr * kr, 0.0), axis=-1, keepdims=True)
    xn = jnp.where(rope, kr * lax.rsqrt(ss * (1.0 / QK_ROPE) + EPS) * w, 0.0)
    return _rope(xn, cos, sin_signed, swap)


def _head_fn(q, kv, kr, cos, sin_signed, qn, kn, swap):
    nope, rope = _lane_masks(q.shape)

    def rstd(x, mask, n):
        return lax.rsqrt(jnp.sum(jnp.where(mask, x * x, 0.0), axis=-1, keepdims=True) * (1.0 / n) + EPS)

    qs = jnp.where(nope, rstd(q, nope, QK_NOPE), rstd(q, rope, QK_ROPE))
    qp = _rope(jnp.where(nope | rope, q * qs * qn, 0.0), cos, sin_signed, swap)
    kp = jnp.where(nope, kv * rstd(kv, nope, QK_NOPE) * kn, 0.0) + kr
    vp = jnp.where(nope, 0.0, kv)
    return qp, kp, vp


def _heads_fwd(q_raw, kv_raw, kr, cos, sin_signed, qn, kn, swap):
    nh, t, _ = q_raw.shape
    tm = _row_tile(t, 512)
    hblk = ((None, tm, HEAD_PAD), lambda i, h: (h, i, 0))
    tblk = ((tm, HEAD_PAD), lambda i, h: (i, 0))
    par = ((1, HEAD_PAD), lambda i, h: (0, 0))
    sw = ((HEAD_PAD, HEAD_PAD), lambda i, h: (0, 0))
    return _ew("mla_heads_fwd", (t // tm, nh), _head_fn,
               [(q_raw,) + hblk, (kv_raw,) + hblk, (kr,) + tblk, (cos,) + tblk, (sin_signed,) + tblk,
                (qn,) + par, (kn,) + par, (swap,) + sw],
               [(SDS((nh, t, HEAD_PAD), BF),) + hblk + (False,)] * 3)


def _heads_bwd(q_raw, kv_raw, kr, cos, sin_signed, qn, kn, swap, dqp, dkp, dvp):
    nh, t, _ = q_raw.shape
    tm = _row_tile(t, 512)
    hblk = ((None, tm, HEAD_PAD), lambda i, h: (h, i, 0))
    tblk = ((tm, HEAD_PAD), lambda i, h: (i, 0))
    par = ((1, HEAD_PAD), lambda i, h: (0, 0))
    sw = ((HEAD_PAD, HEAD_PAD), lambda i, h: (0, 0))

    def body(q, kv, kr, cos, sn, qn, kn, swap, dqp, dkp, dvp, dq, dkv, dkr, dqn, dkn):
        f = lambda q, kv, kr, qn, kn: _head_fn(q, kv, kr, cos[...], sn[...], qn, kn, swap[...])
        _, vjp = jax.vjp(f, q[...], kv[...], kr[...], qn[...], kn[...])
        g = vjp((dqp[...].astype(F32), dkp[...].astype(F32), dvp[...].astype(F32)))
        dq[...] = g[0].astype(dq.dtype)
        dkv[...] = g[1].astype(dkv.dtype)
        h0 = pl.program_id(1) == 0
        first = h0 & (pl.program_id(0) == 0)
        for o, v, c in ((dkr, g[2], h0), (dqn, g[3], first), (dkn, g[4], first)):
            @pl.when(c)
            def _(o=o, v=v):
                o[...] = v

            @pl.when(jnp.logical_not(c))
            def _(o=o, v=v):
                o[...] += v

    spec = lambda b: pl.BlockSpec(*b)
    return pl.pallas_call(
        body, name="mla_heads_bwd", grid=(t // tm, nh),
        in_specs=[spec(hblk), spec(hblk), spec(tblk), spec(tblk), spec(tblk), spec(par), spec(par), spec(sw),
                  spec(hblk), spec(hblk), spec(hblk)],
        out_specs=[spec(hblk), spec(hblk), spec(tblk), spec(par), spec(par)],
        out_shape=[SDS((nh, t, HEAD_PAD), BF), SDS((nh, t, HEAD_PAD), BF), SDS((t, HEAD_PAD), F32),
                   SDS((1, HEAD_PAD), F32), SDS((1, HEAD_PAD), F32)],
        compiler_params=_cparams(2),
    )(q_raw, kv_raw, kr, cos, sin_signed, qn, kn, swap, dqp, dkp, dvp)


ATT_TILE = 256


def _causal_mask(qi, kj, tq, tk):
    r = qi * tq + lax.broadcasted_iota(jnp.int32, (tq, tk), 0)
    c = kj * tk + lax.broadcasted_iota(jnp.int32, (tq, tk), 1)
    return c <= r


def _flash_fwd(q, k, v, nb, seq):
    nh, t, dh = q.shape
    tq = _row_tile(seq, ATT_TILE)
    nq = seq // tq
    scale = QK_DIM ** -0.5

    def body(q_ref, k_ref, v_ref, o_ref, lse_ref):
        qi = pl.program_id(2)
        qt = q_ref[...]

        def step(j, carry):
            m, l, acc = carry
            ks = k_ref[pl.ds(pl.multiple_of(j * tq, tq), tq), :]
            vs = v_ref[pl.ds(pl.multiple_of(j * tq, tq), tq), :]
            s = _dotf(qt, ks, NT) * scale
            s = jnp.where(_causal_mask(qi, j, tq, tq), s, -jnp.inf)
            m_new = jnp.maximum(m, jnp.max(s, axis=-1, keepdims=True))
            alpha = jnp.exp(m - m_new)
            p = jnp.exp(s - m_new)
            return m_new, alpha * l + jnp.sum(p, axis=-1, keepdims=True), alpha * acc + _dotf(p, vs, NN)

        init = (jnp.full((tq, 1), -jnp.inf, F32), jnp.zeros((tq, 1), F32), jnp.zeros((tq, dh), F32))
        m, l, acc = lax.fori_loop(0, qi + 1, step, init)
        o_ref[...] = (acc / l).astype(o_ref.dtype)
        lse_ref[...] = m + jnp.log(l)

    qblk = pl.BlockSpec((None, tq, dh), lambda h, b, i: (h, b * nq + i, 0))
    kblk = pl.BlockSpec((None, seq, dh), lambda h, b, i: (h, b, 0))
    return pl.pallas_call(
        body, name="mla_flash_fwd", grid=(nh, nb, nq), in_specs=[qblk, kblk, kblk],
        out_specs=[qblk, pl.BlockSpec((None, tq, 1), lambda h, b, i: (h, b * nq + i, 0))],
        out_shape=[SDS((nh, t, dh), BF), SDS((nh, t, 1), F32)],
        compiler_params=_cparams(3),
    )(q, k, v)


def _flash_bwd(q, k, v, o, lse, do, nb, seq):
    nh, t, dh = q.shape
    tq = _row_tile(seq, ATT_TILE)
    nq = seq // tq
    scale = QK_DIM ** -0.5

    def dq_body(q_ref, k_ref, v_ref, o_ref, lse_ref, do_ref, dq_ref, delta_ref):
        qi = pl.program_id(2)
        qt, dot_, lse_t = q_ref[...], do_ref[...], lse_ref[...]
        delta = jnp.sum(dot_.astype(F32) * o_ref[...].astype(F32), axis=-1, keepdims=True)
        delta_ref[...] = delta

        def step(j, dq):
            ks = k_ref[pl.ds(pl.multiple_of(j * tq, tq), tq), :]
            vs = v_ref[pl.ds(pl.multiple_of(j * tq, tq), tq), :]
            s = _dotf(qt, ks, NT) * scale
            p = jnp.where(_causal_mask(qi, j, tq, tq), jnp.exp(s - lse_t), 0.0)
            ds = p * (_dotf(dot_, vs, NT) - delta) * scale
            return dq + _dotf(ds, ks, NN)

        dq_ref[...] = lax.fori_loop(0, qi + 1, step, jnp.zeros((tq, dh), F32)).astype(dq_ref.dtype)

    qblk = pl.BlockSpec((None, tq, dh), lambda h, b, i: (h, b * nq + i, 0))
    sblk = pl.BlockSpec((None, tq, 1), lambda h, b, i: (h, b * nq + i, 0))
    full = pl.BlockSpec((None, seq, dh), lambda h, b, i: (h, b, 0))
    sfull = pl.BlockSpec((None, seq, 1), lambda h, b, i: (h, b, 0))
    dq, delta = pl.pallas_call(
        dq_body, name="mla_flash_bwd_dq", grid=(nh, nb, nq), in_specs=[qblk, full, full, qblk, sblk, qblk],
        out_specs=[qblk, sblk], out_shape=[SDS((nh, t, dh), F32), SDS((nh, t, 1), F32)],
        compiler_params=_cparams(3),
    )(q, k, v, o, lse, do)

    def dkv_body(k_ref, v_ref, q_ref, do_ref, lse_ref, delta_ref, dk_ref, dv_ref):
        kj = pl.program_id(2)
        ks, vs = k_ref[...], v_ref[...]

        def step(i, carry):
            dk, dv = carry
            rows = pl.ds(pl.multiple_of(i * tq, tq), tq)
            qt, dot_ = q_ref[rows, :], do_ref[rows, :]
            s = _dotf(qt, ks, NT) * scale
            p = jnp.where(_causal_mask(i, kj, tq, tq), jnp.exp(s - lse_ref[rows, :]), 0.0)
            ds = p * (_dotf(dot_, vs, NT) - delta_ref[rows, :]) * scale
            return dk + _dotf(ds, qt, TN), dv + _dotf(p, dot_, TN)

        dk, dv = lax.fori_loop(kj, nq, step, (jnp.zeros((tq, dh), F32), jnp.zeros((tq, dh), F32)))
        dk_ref[...] = dk.astype(dk_ref.dtype)
        dv_ref[...] = dv.astype(dv_ref.dtype)

    dk, dv = pl.pallas_call(
        dkv_body, name="mla_flash_bwd_dkv", grid=(nh, nb, nq), in_specs=[qblk, qblk, full, full, sfull, sfull],
        out_specs=[qblk, qblk], out_shape=[SDS((nh, t, dh), F32), SDS((nh, t, dh), F32)],
        compiler_params=_cparams(3),
    )(k, v, q, do, lse, delta)
    return dq, dk, dv


def _heads_nt(name, a, wt, out_dtype):
    t, kdim = a.shape
    tm = _row_tile(t, 1024)
    return _fmm(name, (t // tm, MLA_HEADS),
                [(a, (tm, kdim), lambda i, j, k: (i, 0), wt, (HEAD_PAD, kdim), lambda i, j, k: (j, 0), NT, 1, 0)],
                [(SDS((MLA_HEADS, t, HEAD_PAD), out_dtype), (None, tm, HEAD_PAD), lambda i, j: (j, i, 0), False)])[0]


def _heads_nn(name, a, w, n, out_dtype, res=None):
    t = a.shape[1]
    tm = _row_tile(t, 512)
    extras = [(res, (tm, n), lambda i, j: (i, 0))] if res is not None else []
    epi = (lambda accs, r: r + accs[0]) if res is not None else None
    return _fmm(name, (t // tm, 1),
                [(a, (None, tm, HEAD_PAD), lambda i, j, k: (k, i, 0), w, (HEAD_PAD, n), lambda i, j, k: (k, 0),
                  NN, MLA_HEADS, 0)],
                [(SDS((t, n), out_dtype), (tm, n), lambda i, j: (i, 0), False)],
                extras=extras, epi=epi, acc_shape=(tm, n))[0]


def _heads_wgrad(name, a, b, n):
    t = b.shape[0]
    tk = _row_tile(t, 512)
    return _fmm(name, (MLA_HEADS, 1),
                [(a, (None, tk, HEAD_PAD), lambda i, j, k: (i, k, 0), b, (tk, n), lambda i, j, k: (k, 0),
                  TN, t // tk, 0)],
                [(SDS((MLA_HEADS * HEAD_PAD, n), F32), (HEAD_PAD, n), lambda i, j: (i, 0), False)],
                acc_shape=(HEAD_PAD, n))[0]


PM_CKV, PM_KR, PM_CQ = 0, KV_LORA, KV_LORA + HEAD_PAD
PM_DIM = KV_LORA + HEAD_PAD + Q_LORA


def _lat_specs(t, tm):
    return (((tm, KV_LORA), lambda i: (i, 0)), ((tm, HEAD_PAD), lambda i: (i, PM_KR // HEAD_PAD)),
            ((tm, Q_LORA), lambda i: (i, PM_CQ // Q_LORA)))


def _mla_fwd(x, nw, wm, small, tables, nb, seq):
    t, d = x.shape
    cos, sin_signed, swap = tables
    h = _rms_fwd("mla_rms", x, nw)
    pm = _proj_nt("mla_in", h, wm["in_t"], PM_DIM, F32, tn=PM_DIM // 3)
    tm = _row_tile(t, 512)
    ckv_s, kr_s, cq_s = _lat_specs(t, tm)
    row = lambda i: (i, 0)
    par = lambda n: ((1, n), lambda i: (0, 0))
    ckvn = _ew("mla_ckv_norm", (t // tm,), _rms_fn, [(pm,) + ckv_s, (small["kv_a_norm"],) + par(KV_LORA)],
               [(SDS((t, KV_LORA), BF), (tm, KV_LORA), row, False)])[0]
    cqn = _ew("mla_cq_norm", (t // tm,), _rms_fn, [(pm,) + cq_s, (small["q_a_norm"],) + par(Q_LORA)],
              [(SDS((t, Q_LORA), BF), (tm, Q_LORA), row, False)])[0]
    tb = ((tm, HEAD_PAD), row)
    kr = _ew("mla_krope", (t // tm,), _krope_fn,
             [(pm,) + kr_s, (small["k_norm"],) + par(HEAD_PAD), (cos,) + tb, (sin_signed,) + tb,
              (swap, (HEAD_PAD, HEAD_PAD), lambda i: (0, 0))],
             [(SDS((t, HEAD_PAD), F32),) + tb + (False,)])[0]
    q_raw = _heads_nt("mla_q_b", cqn, wm["qb_t"], F32)
    kv_raw = _heads_nt("mla_kv_b", ckvn, wm["kvb_t"], F32)
    qp, kp, vp = _heads_fwd(q_raw, kv_raw, kr, cos, sin_signed, small["q_norm"], small["k_norm"], swap)
    o, lse = _flash_fwd(qp, kp, vp, nb, seq)
    out = _heads_nn("mla_out", o, wm["out"], d, F32, res=x)
    return out, (x, h, pm, ckvn, cqn, kr, q_raw, kv_raw, qp, kp, vp, o, lse)


def _mla_bwd(dy, nw, wm, small, tables, saved, nb, seq):
    x, h, pm, ckvn, cqn, kr, q_raw, kv_raw, qp, kp, vp, o, lse = saved
    t, d = x.shape
    cos, sin_signed, swap = tables
    do = _heads_nt("mla_bwd_do", dy, wm["out"], BF)
    g_out = _heads_wgrad("mla_w_out_g", o, dy, d)
    dqp, dkp, dvp = _flash_bwd(qp, kp, vp, o, lse, do, nb, seq)
    dq_raw, dkv_raw, dkr, d_qn, d_kn = _heads_bwd(q_raw, kv_raw, kr, cos, sin_signed, small["q_norm"],
                                                   small["k_norm"], swap, dqp, dkp, dvp)
    dcqn = _heads_nn("mla_bwd_dcq", dq_raw, wm["qb_t"], Q_LORA, F32)
    dckvn = _heads_nn("mla_bwd_dckv", dkv_raw, wm["kvb_t"], KV_LORA, F32)
    g_qb = _heads_wgrad("mla_w_qb_g", dq_raw, cqn, Q_LORA)
    g_kvb = _heads_wgrad("mla_w_kvb_g", dkv_raw, ckvn, KV_LORA)
    tm = _row_tile(t, 512)
    ckv_s, kr_s, cq_s = _lat_specs(t, tm)
    row = lambda i: (i, 0)
    par = lambda n: ((1, n), lambda i: (0, 0))

    def rms_b(xv, w, dv):
        _, vjp = jax.vjp(_rms_fn, xv, w)
        return vjp(dv)

    dckv, d_kva = _ew("mla_ckv_norm_bwd", (t // tm,), rms_b,
                      [(pm,) + ckv_s, (small["kv_a_norm"],) + par(KV_LORA), (dckvn, (tm, KV_LORA), row)],
                      [(SDS((t, KV_LORA), BF), (tm, KV_LORA), row, False),
                       (SDS((1, KV_LORA), F32),) + par(KV_LORA) + (True,)], acc_axes=(0,))
    dcq, d_qa = _ew("mla_cq_norm_bwd", (t // tm,), rms_b,
                    [(pm,) + cq_s, (small["q_a_norm"],) + par(Q_LORA), (dcqn, (tm, Q_LORA), row)],
                    [(SDS((t, Q_LORA), BF), (tm, Q_LORA), row, False),
                     (SDS((1, Q_LORA), F32),) + par(Q_LORA) + (True,)], acc_axes=(0,))
    tb = ((tm, HEAD_PAD), row)

    def kr_b(krv, w, cosv, sinv, sw, dv):
        _, vjp = jax.vjp(lambda a, b: _krope_fn(a, b, cosv, sinv, sw), krv, w)
        return vjp(dv)

    dkr_raw, d_kn2 = _ew("mla_krope_bwd", (t // tm,), kr_b,
                         [(pm,) + kr_s, (small["k_norm"],) + par(HEAD_PAD), (cos,) + tb, (sin_signed,) + tb,
                          (swap, (HEAD_PAD, HEAD_PAD), lambda i: (0, 0)), (dkr,) + tb],
                         [(SDS((t, HEAD_PAD), BF),) + tb + (False,),
                          (SDS((1, HEAD_PAD), F32),) + par(HEAD_PAD) + (True,)], acc_axes=(0,))
    dh = _seg_nn("mla_bwd_dh", [(dckv, wm["in_t"], PM_CKV), (dkr_raw, wm["in_t"], PM_KR),
                                (dcq, wm["in_t"], PM_CQ)], d, BF, tk=128)
    dx, dnw = _rms_bwd("mla_bwd_rms", x, nw, dh, dy)
    g = {
        "in_ckv_t": _wgrad("mla_w_in_ckv_g", dckv, h, KV_LORA, d, tm=KV_LORA, tn=d, out_dtype=F32),
        "in_kr_t": _wgrad("mla_w_in_kr_g", dkr_raw, h, HEAD_PAD, d, tm=HEAD_PAD, tn=d, out_dtype=F32),
        "in_cq_t": _wgrad("mla_w_in_cq_g", dcq, h, Q_LORA, d, tm=Q_LORA, tn=d, out_dtype=F32),
        "qb_t": g_qb, "kvb_t": g_kvb, "out": g_out,
        "q_a_norm": d_qa, "kv_a_norm": d_kva, "q_norm": d_qn, "k_norm": d_kn + d_kn2,
    }
    return dx, dnw, g


def _mesh_pos():
    return lax.axis_index("x"), lax.axis_index("y"), lax.axis_index("c")


def _peer(pos, k):
    x, y, c = pos
    return (x ^ ((k >> 2) & 1), y ^ ((k >> 1) & 1), c ^ (k & 1))


def _flat(pos):
    return 4 * pos[0] + 2 * pos[1] + pos[2]


def _slab(ref, axis, start, size):
    idx = [slice(None)] * axis + [pl.ds(start, size)]
    return ref.at[tuple(idx)]


def _all_gather(name, items):
    n = len(items)
    sizes = [a.shape[ax] for a, ax in items]

    def body(*refs):
        srcs, dsts = refs[:n], refs[n:2 * n]
        send_sems, recv_sems, local_sems = refs[2 * n:]
        pos = _mesh_pos()
        me = _flat(pos)
        local, sends = [], []
        for t, (src, dst) in enumerate(zip(srcs, dsts)):
            ax, sz = items[t][1], sizes[t]
            mine = _slab(dst, ax, me * sz, sz)
            cp = pltpu.make_async_copy(src, mine, local_sems.at[t])
            cp.start()
            local.append(cp)
            for k in range(1, N_DEV):
                rc = pltpu.make_async_remote_copy(
                    src_ref=src, dst_ref=mine, send_sem=send_sems.at[t, k - 1], recv_sem=recv_sems.at[t, k - 1],
                    device_id=_peer(pos, k), device_id_type=pl.DeviceIdType.MESH)
                rc.start()
                sends.append(rc)
        for t, (src, dst) in enumerate(zip(srcs, dsts)):
            ax, sz = items[t][1], sizes[t]
            for k in range(1, N_DEV):
                theirs = _slab(dst, ax, _flat(_peer(pos, k)) * sz, sz)
                pltpu.make_async_remote_copy(
                    src_ref=src, dst_ref=theirs, send_sem=send_sems.at[t, k - 1], recv_sem=recv_sems.at[t, k - 1],
                    device_id=_peer(pos, k), device_id_type=pl.DeviceIdType.MESH).wait_recv()
        for rc in sends:
            rc.wait_send()
        for cp in local:
            cp.wait()

    out_shape = []
    for (a, ax), sz in zip(items, sizes):
        shp = list(a.shape)
        shp[ax] = sz * N_DEV
        out_shape.append(SDS(tuple(shp), a.dtype))
    hbm = pl.BlockSpec(memory_space=pl.ANY)
    return pl.pallas_call(
        body, name=name, in_specs=[hbm] * n, out_specs=[hbm] * n, out_shape=out_shape,
        scratch_shapes=[pltpu.SemaphoreType.DMA((n, N_DEV - 1)), pltpu.SemaphoreType.DMA((n, N_DEV - 1)),
                        pltpu.SemaphoreType.DMA((n,))],
    )(*[a for a, _ in items])


def _scatter_partials(name, items):
    n = len(items)
    sizes = [a.shape[ax] // N_DEV for a, ax in items]

    def body(*refs):
        srcs, dsts = refs[:n], refs[n:2 * n]
        send_sems, recv_sems, local_sems = refs[2 * n:]
        pos = _mesh_pos()
        me = _flat(pos)
        local, sends = [], []
        for t, (src, dst) in enumerate(zip(srcs, dsts)):
            ax, sz = items[t][1], sizes[t]
            cp = pltpu.make_async_copy(_slab(src, ax, me * sz, sz), dst.at[me], local_sems.at[t])
            cp.start()
            local.append(cp)
            for k in range(1, N_DEV):
                peer = _peer(pos, k)
                rc = pltpu.make_async_remote_copy(
                    src_ref=_slab(src, ax, _flat(peer) * sz, sz), dst_ref=dst.at[me],
                    send_sem=send_sems.at[t, k - 1], recv_sem=recv_sems.at[t, k - 1],
                    device_id=peer, device_id_type=pl.DeviceIdType.MESH)
                rc.start()
                sends.append(rc)
        for t, (src, dst) in enumerate(zip(srcs, dsts)):
            ax, sz = items[t][1], sizes[t]
            for k in range(1, N_DEV):
                peer = _peer(pos, k)
                pltpu.make_async_remote_copy(
                    src_ref=_slab(src, ax, me * sz, sz), dst_ref=dst.at[_flat(peer)],
                    send_sem=send_sems.at[t, k - 1], recv_sem=recv_sems.at[t, k - 1],
                    device_id=peer, device_id_type=pl.DeviceIdType.MESH).wait_recv()
        for rc in sends:
            rc.wait_send()
        for cp in local:
            cp.wait()

    out_shape = []
    for (a, ax), sz in zip(items, sizes):
        shp = list(a.shape)
        shp[ax] = sz
        out_shape.append(SDS((N_DEV,) + tuple(shp), a.dtype))
    hbm = pl.BlockSpec(memory_space=pl.ANY)
    return pl.pallas_call(
        body, name=name, in_specs=[hbm] * n, out_specs=[hbm] * n, out_shape=out_shape,
        scratch_shapes=[pltpu.SemaphoreType.DMA((n, N_DEV - 1)), pltpu.SemaphoreType.DMA((n, N_DEV - 1)),
                        pltpu.SemaphoreType.DMA((n,))],
    )(*[a for a, _ in items])


def _adam_math(w, g, m, v):
    m = ADAM_B1 * m + (1.0 - ADAM_B1) * g
    v = ADAM_B2 * v + (1.0 - ADAM_B2) * (g * g)
    m_hat = m / (1.0 - ADAM_B1 ** ADAM_STEP)
    v_hat = v / (1.0 - ADAM_B2 ** ADAM_STEP)
    delta = -ADAM_LR * (m_hat / (jnp.sqrt(v_hat) + ADAM_EPS) + ADAM_WD * w)
    return delta, m, v


def _adam(name, land, land_blk, land_idx, w, m, v, transposed, ck):
    n, r, c = w.shape
    wblk = ((None, ck, c), lambda a, i: (a, i, 0))

    def fn(parts, w, m, v):
        g = parts[0].astype(F32)
        for s in range(1, N_DEV):
            g = g + parts[s].astype(F32)
        if transposed:
            g = g.T
        delta, m2, v2 = _adam_math(w, g, m, v)
        return g, delta, m2, v2

    return _ew(name, (n, r // ck), fn,
               [(land, land_blk, land_idx), (w,) + wblk, (m,) + wblk, (v,) + wblk],
               [(SDS(w.shape, F32),) + wblk + (False,)] * 4, vmem=VMEM_BIG)


def _prep_ffn(gate, up, down):
    def body(g, u, dn, o):
        o[0] = g[...].T.astype(BF)
        o[1] = u[...].T.astype(BF)
        o[2] = dn[...].astype(BF)

    cblk = pl.BlockSpec((None, None, D_MODEL, FF_SHARD), lambda l, i: (l, i, 0, 0))
    rblk = pl.BlockSpec((None, None, FF_SHARD, D_MODEL), lambda l, i: (l, i, 0, 0))
    return pl.pallas_call(
        body, name="prep_ffn", grid=(2, 2), in_specs=[cblk, cblk, rblk],
        out_specs=pl.BlockSpec((3, FF_SHARD, D_MODEL), lambda l, i: (2 * l + i, 0, 0)),
        out_shape=SDS((12, FF_SHARD, D_MODEL), BF), compiler_params=_cparams(2, VMEM_BIG),
    )(gate, up, down)


def _transpose_cast(name, w, dtype):
    def body(a, o):
        o[...] = a[...].T.astype(dtype)

    r, c = w.shape
    return pl.pallas_call(body, name=name, out_shape=SDS((c, r), dtype),
                          compiler_params=pltpu.CompilerParams(vmem_limit_bytes=VMEM_BIG))(w)


SMALL_SHARDED = (("norm_w", 6 * 128), ("conv_w", CONV_K * 512), ("q_a_norm", 48), ("kv_a_norm", 32))
SMALL_PACK = 3072


def _dyn(a, start, size):
    return lax.dynamic_slice_in_dim(a, start, size, axis=a.ndim - 1)


def _layout(ssm_out_all, mla_out_all, ssm_in_t, mla_in_all, qb_all, kvb_all, conv_w, conv_b, dt_bias, a_log, d_skip,
            ssm_norm_w, q_a_norm, kv_a_norm, q_norm, k_norm):
    d = ssm_in_t.shape[1]
    ssm_in_t = ssm_in_t.astype(BF)
    dt_rows = ssm_in_t[D_INNER + CONV_DIM:].reshape(SSM_GROUPS, SSM_HPG, d)
    ws = {"z_t": ssm_in_t[:D_INNER], "xbc_t": ssm_in_t[D_INNER:D_INNER + CONV_DIM],
          "dt_t": jnp.pad(dt_rows, ((0, 0), (0, LANES - SSM_HPG), (0, 0))).reshape(SSM_GROUPS * LANES, d),
          "out": ssm_out_all}
    in_t = mla_in_all.T
    kr_rows = jnp.pad(in_t[Q_LORA + KV_LORA:], ((QK_NOPE, HEAD_PAD - QK_DIM), (0, 0)))
    qb_heads = jnp.pad(qb_all.reshape(MLA_HEADS, QK_DIM, Q_LORA), ((0, 0), (0, HEAD_PAD - QK_DIM), (0, 0)))
    out_heads = jnp.pad(mla_out_all.reshape(MLA_HEADS, 64, d), ((0, 0), (64, 0), (0, 0)))
    wm = {"in_t": jnp.concatenate([in_t[Q_LORA:Q_LORA + KV_LORA], kr_rows, in_t[:Q_LORA]], axis=0),
          "qb_t": qb_heads.reshape(MLA_HEADS * HEAD_PAD, Q_LORA), "kvb_t": kvb_all,
          "out": out_heads.reshape(MLA_HEADS * HEAD_PAD, d)}
    lane_heads = lambda p: jnp.pad(p.reshape(SSM_GROUPS, SSM_HPG), ((0, 0), (0, LANES - SSM_HPG))).reshape(1, -1)
    pad_head = lambda p: jnp.pad(p.reshape(1, QK_DIM), ((0, 0), (0, HEAD_PAD - QK_DIM)))
    small = {"conv_w": conv_w, "conv_b": conv_b, "dt_bias": lane_heads(dt_bias), "a_log": lane_heads(a_log),
             "d_skip": lane_heads(d_skip), "ssm_norm_w": ssm_norm_w, "q_a_norm": q_a_norm, "kv_a_norm": kv_a_norm,
             "q_norm": pad_head(q_norm), "k_norm": pad_head(k_norm)}
    return ws, wm, small


def _local_step(x, positions, loss_target, norm_full, ffn_all, ws, wm, small):
    nb, seq, d = x.shape
    t = nb * seq
    xf = x.reshape(t, d)
    tables = list(_rope_tables(positions.reshape(t, 1))) + [_swap_matrix()]
    x1, s_f0 = _ffn_fwd("ffn0", xf, norm_full[0, 0], ffn_all, 0)
    x2, s_ssm = _ssm_fwd(x1, norm_full[0, 1], ws, small, nb, seq)
    x3, s_f1 = _ffn_fwd("ffn1", x2, norm_full[0, 2], ffn_all, 1)
    x4, s_f2 = _ffn_fwd("ffn2", x3, norm_full[1, 0], ffn_all, 2)
    x5, s_mla = _mla_fwd(x4, norm_full[1, 1], wm, small, tables, nb, seq)
    x6, s_f3 = _ffn_fwd("ffn3", x5, norm_full[1, 2], ffn_all, 3)
    dy, loss_cols = _loss_and_grad(x6, loss_target.reshape(t, d))

    dg_slab = jnp.zeros((12, D_FF, d), BF)
    dx5, dn12, dg_slab = _ffn_bwd("ffn3", dy, norm_full[1, 2], ffn_all, 3, s_f3, dg_slab)
    dx4, dn11, g_mla = _mla_bwd(dx5, norm_full[1, 1], wm, small, tables, s_mla, nb, seq)
    dx3, dn10, dg_slab = _ffn_bwd("ffn2", dx4, norm_full[1, 0], ffn_all, 2, s_f2, dg_slab)
    dx2, dn02, dg_slab = _ffn_bwd("ffn1", dx3, norm_full[0, 2], ffn_all, 1, s_f1, dg_slab)
    dx1, dn01, g_ssm = _ssm_bwd(dx2, norm_full[0, 1], ws, small, s_ssm, nb, seq)
    dx0, dn00, dg_slab = _ffn_bwd("ffn0", dx1, norm_full[0, 0], ffn_all, 0, s_f0, dg_slab)
    return loss_cols, dx0.reshape(nb, seq, d), (dn00, dn01, dn02, dn10, dn11, dn12), dg_slab, g_ssm, g_mla


def kernel(x, positions, norm_w, ffn_w_gate, ffn_w_up, ffn_w_down, ssm_w_in, ssm_conv_w, ssm_conv_b, ssm_dt_bias, ssm_a_log, ssm_d, ssm_norm_w, ssm_w_out, mla_w_in, mla_q_a_norm, mla_kv_a_norm, mla_w_q_b, mla_w_kv_b, mla_q_norm, mla_k_norm, mla_w_out, loss_target, m_norm_w, m_ffn_w_gate, m_ffn_w_up, m_ffn_w_down, m_ssm_w_in, m_ssm_conv_w, m_ssm_conv_b, m_ssm_dt_bias, m_ssm_a_log, m_ssm_d, m_ssm_norm_w, m_ssm_w_out, m_mla_w_in, m_mla_q_a_norm, m_mla_kv_a_norm, m_mla_w_q_b, m_mla_w_kv_b, m_mla_q_norm, m_mla_k_norm, m_mla_w_out, v_norm_w, v_ffn_w_gate, v_ffn_w_up, v_ffn_w_down, v_ssm_w_in, v_ssm_conv_w, v_ssm_conv_b, v_ssm_dt_bias, v_ssm_a_log, v_ssm_d, v_ssm_norm_w, v_ssm_w_out, v_mla_w_in, v_mla_q_a_norm, v_mla_kv_a_norm, v_mla_w_q_b, v_mla_w_kv_b, v_mla_q_norm, v_mla_k_norm, v_mla_w_out):
    nb, seq, d = x.shape
    t = nb * seq
    me = _flat(_mesh_pos())

    ffn_loc = _prep_ffn(ffn_w_gate, ffn_w_up, ffn_w_down)
    ssm_in_loc = _transpose_cast("prep_ssm_in", ssm_w_in[0], F32).reshape(SSM_IN_SHARD, 8, LANES)
    qb_loc = _transpose_cast("prep_q_b", mla_w_q_b[0], BF)
    kvb_loc = _transpose_cast("prep_kv_b", mla_w_kv_b[0], BF)
    small_loc = jnp.concatenate([norm_w.reshape(-1), ssm_conv_w.reshape(-1), mla_q_a_norm.reshape(-1),
                                 mla_kv_a_norm.reshape(-1)])
    small_loc = jnp.pad(small_loc, (0, SMALL_PACK - small_loc.shape[0])).reshape(1, 1, SMALL_PACK)
    (ffn_all, ssm_out_all, mla_out_all, ssm_in_all, mla_in_all, qb_all, kvb_all, small_all) = _all_gather(
        "gather_weights",
        [(ffn_loc, 1), (ssm_w_out[0].astype(BF), 0), (mla_w_out[0].astype(BF), 0), (ssm_in_loc, 0),
         (mla_w_in[0].astype(BF), 0), (qb_loc, 0), (kvb_loc, 0), (small_loc, 0)])

    sm = small_all.reshape(N_DEV, SMALL_PACK)
    norm_full = sm[:, :768].reshape(N_DEV, 6, 128).transpose(1, 0, 2).reshape(2, 3, 1, d)
    conv_w_full = sm[:, 768:768 + 2048].reshape(N_DEV, CONV_K, 512).transpose(1, 0, 2).reshape(CONV_K, CONV_DIM)
    ws, wm, small = _layout(ssm_out_all, mla_out_all, ssm_in_all.reshape(SSM_IN_DIM, d), mla_in_all, qb_all, kvb_all,
                            conv_w_full, ssm_conv_b, ssm_dt_bias, ssm_a_log, ssm_d, ssm_norm_w,
                            sm[:, 2816:2864].reshape(1, Q_LORA), sm[:, 2864:2896].reshape(1, KV_LORA),
                            mla_q_norm, mla_k_norm)
    loss_cols, grad_x, dns, dg_slab, g_ssm, g_mla = _local_step(x, positions, loss_target, norm_full, ffn_all, ws, wm,
                                                                small)
    loss = lax.psum(jnp.sum(loss_cols), ("x", "y", "c"))
    dn00, dn01, dn02, dn10, dn11, dn12 = dns

    heads_of = lambda a: a.reshape(SSM_GROUPS, LANES, -1)[:, :SSM_HPG].reshape(SSM_HEADS, -1)
    g_ssm_in_t = jnp.concatenate([g_ssm["z_t"], g_ssm["x_t"], g_ssm["b_t"], g_ssm["c_t"], heads_of(g_ssm["dt_t"])],
                                 axis=0).reshape(SSM_IN_DIM, 8, LANES)
    g_mla_in = jnp.concatenate([g_mla["in_cq_t"], g_mla["in_ckv_t"], g_mla["in_kr_t"][QK_NOPE:QK_DIM]], axis=0).T
    g_qb = g_mla["qb_t"].reshape(MLA_HEADS, HEAD_PAD, Q_LORA)[:, :QK_DIM].reshape(MLA_HEADS * QK_DIM, Q_LORA)
    g_mla_out = g_mla["out"].reshape(MLA_HEADS, HEAD_PAD, d)[:, 64:].reshape(MLA_HEADS * 64, d)
    (l_ffn, l_ssm_out, l_mla_out, l_ssm_in, l_mla_in, l_qb, l_kvb) = _scatter_partials(
        "scatter_grads",
        [(dg_slab, 1), (g_ssm["out"], 0), (g_mla_out, 0), (g_ssm_in_t, 0), (g_mla_in, 0), (g_qb, 0),
         (g_mla["kvb_t"], 0)])

    unlane = lambda a: a.reshape(SSM_GROUPS, LANES)[:, :SSM_HPG].reshape(1, SSM_HEADS)
    small_g = jnp.concatenate([
        jnp.concatenate([dn00, dn01, dn02, dn10, dn11, dn12], axis=0).reshape(-1),
        g_ssm["conv_w"].reshape(-1), g_ssm["conv_b"].reshape(-1), unlane(g_ssm["dt_bias"]).reshape(-1),
        unlane(g_ssm["a_log"]).reshape(-1), unlane(g_ssm["d_skip"]).reshape(-1), g_ssm["ssm_norm_w"].reshape(-1),
        g_mla["q_a_norm"].reshape(-1), g_mla["kv_a_norm"].reshape(-1), g_mla["q_norm"][0, :QK_DIM],
        g_mla["k_norm"][0, :QK_DIM]])
    n_small = small_g.shape[0]
    n_small_pad = -(-n_small // LANES) * LANES
    small_g = jnp.pad(small_g, (0, n_small_pad - n_small)).reshape(1, 1, n_small_pad)
    gs = _all_gather("gather_small_grads", [(small_g, 0)])[0].reshape(N_DEV, n_small_pad)

    outs = {}

    def put(name, res, shape):
        for key, val in zip(("grad", "delta", "new_m", "new_v"), res):
            outs[(key, name)] = val.reshape(shape)

    ck = 256
    for j, (name, w, m, v) in enumerate((("ffn_w_gate", ffn_w_gate, m_ffn_w_gate, v_ffn_w_gate),
                                         ("ffn_w_up", ffn_w_up, m_ffn_w_up, v_ffn_w_up))):
        res = _adam("adam_" + name, l_ffn, (N_DEV, None, FF_SHARD, ck), lambda a, i, j=j: (0, 3 * a + j, 0, i),
                    w.reshape(4, d, FF_SHARD), m.reshape(4, d, FF_SHARD), v.reshape(4, d, FF_SHARD), True, ck)
        put(name, res, w.shape)
    res = _adam("adam_ffn_w_down", l_ffn, (N_DEV, None, 176, d), lambda a, i: (0, 3 * a + 2, i, 0),
                ffn_w_down.reshape(4, FF_SHARD, d), m_ffn_w_down.reshape(4, FF_SHARD, d),
                v_ffn_w_down.reshape(4, FF_SHARD, d), False, 176)
    put("ffn_w_down", res, ffn_w_down.shape)
    l_ssm_in2 = l_ssm_in.reshape(N_DEV, SSM_IN_SHARD, d)
    res = _adam("adam_ssm_w_in", l_ssm_in2, (N_DEV, SSM_IN_SHARD, 128), lambda a, i: (0, 0, i),
                ssm_w_in, m_ssm_w_in, v_ssm_w_in, True, 128)
    put("ssm_w_in", res, ssm_w_in.shape)
    res = _adam("adam_ssm_w_out", l_ssm_out, (N_DEV, 128, d), lambda a, i: (0, i, 0),
                ssm_w_out, m_ssm_w_out, v_ssm_w_out, False, 128)
    put("ssm_w_out", res, ssm_w_out.shape)
    res = _adam("adam_mla_w_in", l_mla_in, (N_DEV, 128, MLA_IN_DIM), lambda a, i: (0, 0, 0),
                mla_w_in, m_mla_w_in, v_mla_w_in, False, 128)
    put("mla_w_in", res, mla_w_in.shape)
    res = _adam("adam_mla_w_q_b", l_qb, (N_DEV, 192, 128), lambda a, i: (0, 0, i),
                mla_w_q_b, m_mla_w_q_b, v_mla_w_q_b, True, 128)
    put("mla_w_q_b", res, mla_w_q_b.shape)
    res = _adam("adam_mla_w_kv_b", l_kvb, (N_DEV, 256, 128), lambda a, i: (0, 0, i),
                mla_w_kv_b, m_mla_w_kv_b, v_mla_w_kv_b, True, 128)
    put("mla_w_kv_b", res, mla_w_kv_b.shape)
    res = _adam("adam_mla_w_out", l_mla_out, (N_DEV, 128, d), lambda a, i: (0, 0, 0),
                mla_w_out, m_mla_w_out, v_mla_w_out, False, 128)
    put("mla_w_out", res, mla_w_out.shape)

    small_params = (
        ("norm_w", norm_w, m_norm_w, v_norm_w, 6 * d, 6, 128), ("ssm_conv_w", ssm_conv_w, m_ssm_conv_w, v_ssm_conv_w,
                                                               CONV_K * CONV_DIM, CONV_K, 512),
        ("ssm_conv_b", ssm_conv_b, m_ssm_conv_b, v_ssm_conv_b, CONV_DIM, 0, 0),
        ("ssm_dt_bias", ssm_dt_bias, m_ssm_dt_bias, v_ssm_dt_bias, SSM_HEADS, 0, 0),
        ("ssm_a_log", ssm_a_log, m_ssm_a_log, v_ssm_a_log, SSM_HEADS, 0, 0),
        ("ssm_d", ssm_d, m_ssm_d, v_ssm_d, SSM_HEADS, 0, 0),
        ("ssm_norm_w", ssm_norm_w, m_ssm_norm_w, v_ssm_norm_w, D_INNER, 0, 0),
        ("mla_q_a_norm", mla_q_a_norm, m_mla_q_a_norm, v_mla_q_a_norm, Q_LORA, 1, 48),
        ("mla_kv_a_norm", mla_kv_a_norm, m_mla_kv_a_norm, v_mla_kv_a_norm, KV_LORA, 1, 32),
        ("mla_q_norm", mla_q_norm, m_mla_q_norm, v_mla_q_norm, QK_DIM, 0, 0),
        ("mla_k_norm", mla_k_norm, m_mla_k_norm, v_mla_k_norm, QK_DIM, 0, 0),
    )
    parts, ws_, ms_, vs_, off = [], [], [], [], 0
    for name, w, m, v, full, rows, shard in small_params:
        seg = gs[:, off:off + full]
        if rows:
            seg = _dyn(seg.reshape(N_DEV, rows, full // rows), me * shard, shard).reshape(N_DEV, rows * shard)
        parts.append(seg)
        ws_.append(w.reshape(1, -1))
        ms_.append(m.reshape(1, -1))
        vs_.append(v.reshape(1, -1))
        off += full
    n_loc = sum(p.shape[1] for p in parts)
    n_loc_pad = -(-n_loc // LANES) * LANES
    padc = lambda a, val=0.0: jnp.pad(jnp.concatenate(a, axis=1), ((0, 0), (0, n_loc_pad - n_loc)),
                                      constant_values=val)
    res = _adam("adam_small", padc(parts).reshape(N_DEV, 1, n_loc_pad), (N_DEV, 1, n_loc_pad), lambda a, i: (0, 0, 0),
                padc(ws_).reshape(1, 1, n_loc_pad), padc(ms_).reshape(1, 1, n_loc_pad),
                padc(vs_, 1.0).reshape(1, 1, n_loc_pad), False, 1)
    off = 0
    for name, w, m, v, full, rows, shard in small_params:
        nloc = w.size
        put(name, [r.reshape(-1)[off:off + nloc] for r in res], w.shape)
        off += nloc

    order = ("norm_w", "ffn_w_gate", "ffn_w_up", "ffn_w_down", "ssm_w_in", "ssm_conv_w", "ssm_conv_b", "ssm_dt_bias",
             "ssm_a_log", "ssm_d", "ssm_norm_w", "ssm_w_out", "mla_w_in", "mla_q_a_norm", "mla_kv_a_norm",
             "mla_w_q_b", "mla_w_kv_b", "mla_q_norm", "mla_k_norm", "mla_w_out")
    return (loss, grad_x, *[outs[(k, n)] for k in ("grad", "delta", "new_m", "new_v") for n in order])
```

```python
import functools
import math

import jax
import jax.numpy as jnp
import numpy as np
from jax import lax
from jax.experimental import pallas as pl
from jax.experimental.pallas import tpu as pltpu

F32 = jnp.float32
BF = jnp.bfloat16
SDS = jax.ShapeDtypeStruct

N_DEV = 8
D_MODEL = 1024
D_FF = 2816
FF_SHARD = D_FF // N_DEV
D_INNER = 2048
SSM_HEADS = 32
SSM_GROUPS = 8
SSM_HPG = 4
SSM_STATE = 128
CONV_K = 4
CONV_DIM = 4096
SSM_IN_DIM = 6176
SSM_IN_SHARD = SSM_IN_DIM // N_DEV
NORM_GROUP = 256
CHUNK = 128
MLA_HEADS = 16
Q_LORA = 384
KV_LORA = 256
QK_NOPE = 64
QK_ROPE = 32
QK_DIM = 96
MLA_IN_DIM = 672
HEAD_PAD = 128
ROPE_THETA = 10000.0
EPS = 1e-6
LANES = 128

ADAM_LR = 0.001
ADAM_B1 = 0.9
ADAM_B2 = 0.999
ADAM_EPS = 1e-08
ADAM_WD = 0.01
ADAM_STEP = 10

VMEM_BIG = 56 * 1024 * 1024

NN = ((1,), (0,))
NT = ((1,), (1,))
TN = ((0,), (0,))


def _dotf(a, b, dn):
    return lax.dot_general(a.astype(BF), b.astype(BF), (dn, ((), ())), preferred_element_type=F32)


def _dot_hi(a, b, dn=NN):
    return lax.dot_general(a, b, (dn, ((), ())), precision=lax.Precision.HIGHEST, preferred_element_type=F32)


def _sigmoid(x):
    return 1.0 / (1.0 + jnp.exp(-x))


def _silu(x):
    return x * _sigmoid(x)


def _softplus(x):
    return jnp.maximum(x, 0.0) + jnp.log(1.0 + jnp.exp(-jnp.abs(x)))


def _cparams(n_grid, vmem=None):
    return pltpu.CompilerParams(dimension_semantics=("arbitrary",) * n_grid, vmem_limit_bytes=vmem)


def _fmm(name, grid_mn, pairs, outs, *, epi=None, extras=(), n_acc=1, acc_shape=None, vmem=None, alias=None,
         joint=False):
    if joint:
        nk_total = pairs[0][7]
        assert all(p[7] == nk_total for p in pairs)
        starts = [0] * len(pairs)
    else:
        nk_total = sum(p[7] for p in pairs)
        starts = []
        s = 0
        for p in pairs:
            starts.append(s)
            s += p[7]
    n_pairs, n_extras, n_outs = len(pairs), len(extras), len(outs)
    single = nk_total == 1

    def body(*refs):
        ab_refs = refs[: 2 * n_pairs]
        e_refs = refs[2 * n_pairs: 2 * n_pairs + n_extras]
        pos = 2 * n_pairs + n_extras + (1 if alias is not None else 0)
        o_refs = refs[pos: pos + n_outs]
        acc_refs = refs[pos + n_outs:]
        i, j, k = pl.program_id(0), pl.program_id(1), pl.program_id(2)

        def finish(accs):
            res = epi(accs, *[e[...] for e in e_refs]) if epi is not None else accs
            if not isinstance(res, (tuple, list)):
                res = (res,)
            first = (i == 0) & (j == 0)
            for o, r, spec in zip(o_refs, res, outs):
                if spec[3]:
                    @pl.when(first)
                    def _(o=o, r=r):
                        o[...] = r.astype(o.dtype)

                    @pl.when(jnp.logical_not(first))
                    def _(o=o, r=r):
                        o[...] += r.astype(o.dtype)
                else:
                    o[...] = r.astype(o.dtype)

        if single:
            accs = [None] * n_acc
            for p, pr in enumerate(pairs):
                d = _dotf(ab_refs[2 * p][...], ab_refs[2 * p + 1][...], pr[6])
                accs[pr[8]] = d if accs[pr[8]] is None else accs[pr[8]] + d
            finish(accs)
            return

        @pl.when(k == 0)
        def _():
            for a in acc_refs:
                a[...] = jnp.zeros_like(a)

        for p, pr in enumerate(pairs):
            def step(p=p, pr=pr):
                acc_refs[pr[8]][...] += _dotf(ab_refs[2 * p][...], ab_refs[2 * p + 1][...], pr[6])

            if n_pairs == 1 or joint:
                step()
            else:
                pl.when((k >= starts[p]) & (k < starts[p] + pr[7]))(step)

        @pl.when(k == nk_total - 1)
        def _():
            finish([a[...] for a in acc_refs])

    in_specs, args = [], []
    for p, pr in enumerate(pairs):
        a, a_blk, a_idx, b, b_blk, b_idx, _, nk, _ = pr
        st = starts[p]

        def amap(i, j, k, a_idx=a_idx, st=st, nk=nk):
            return a_idx(i, j, jnp.clip(k - st, 0, nk - 1))

        def bmap(i, j, k, b_idx=b_idx, st=st, nk=nk):
            return b_idx(i, j, jnp.clip(k - st, 0, nk - 1))

        in_specs += [pl.BlockSpec(a_blk, amap), pl.BlockSpec(b_blk, bmap)]
        args += [a, b]
    for arr, blk, idx in extras:
        in_specs.append(pl.BlockSpec(blk, lambda i, j, k, idx=idx: idx(i, j)))
        args.append(arr)
    io_alias = {}
    if alias is not None:
        in_specs.append(pl.BlockSpec(memory_space=pl.ANY))
        io_alias = {len(args): 0}
        args.append(alias)
    out_specs = [pl.BlockSpec(blk, lambda i, j, k, idx=idx: idx(i, j)) for _, blk, idx, _ in outs]
    out_shape = [o[0] for o in outs]
    scratch = [] if single else [pltpu.VMEM(acc_shape, F32) for _ in range(n_acc)]
    res = pl.pallas_call(
        body, name=name, grid=(grid_mn[0], grid_mn[1], nk_total), in_specs=in_specs, out_specs=out_specs,
        out_shape=out_shape, scratch_shapes=scratch, input_output_aliases=io_alias,
        compiler_params=_cparams(3, vmem),
    )(*args)
    return res


def _ew(name, grid, fn, ins, outs, *, acc_axes=(), vmem=None):
    n_in = len(ins)

    def body(*refs):
        res = fn(*[r[...] for r in refs[:n_in]])
        if not isinstance(res, (tuple, list)):
            res = (res,)
        first = None
        for ax in acc_axes:
            c = pl.program_id(ax) == 0
            first = c if first is None else (first & c)
        for o, r, spec in zip(refs[n_in:], res, outs):
            if spec[3]:
                @pl.when(first)
                def _(o=o, r=r):
                    o[...] = r.astype(o.dtype)

                @pl.when(jnp.logical_not(first))
                def _(o=o, r=r):
                    o[...] += r.astype(o.dtype)
            else:
                o[...] = r.astype(o.dtype)

    return pl.pallas_call(
        body, name=name, grid=grid,
        in_specs=[pl.BlockSpec(blk, idx) for _, blk, idx in ins],
        out_specs=[pl.BlockSpec(blk, idx) for _, blk, idx, _ in outs],
        out_shape=[o[0] for o in outs],
        compiler_params=_cparams(len(grid), vmem),
    )(*[a for a, _, _ in ins])


def _row_tile(t, want):
    tm = min(want, t)
    assert t % tm == 0, (t, tm)
    return tm


def _rms_fn(x, w):
    return x * lax.rsqrt(jnp.mean(x * x, axis=-1, keepdims=True) + EPS) * w


def _rms_fwd(name, x, w):
    t, d = x.shape
    tm = _row_tile(t, 512)
    return _ew(name, (t // tm,), _rms_fn,
               [(x, (tm, d), lambda i: (i, 0)), (w, (1, d), lambda i: (0, 0))],
               [(SDS((t, d), BF), (tm, d), lambda i: (i, 0), False)])[0]


def _rms_bwd(name, x, w, dh, dres):
    t, d = x.shape
    tm = _row_tile(t, 512)

    def fn(x, w, dh, dres):
        _, vjp = jax.vjp(_rms_fn, x, w)
        dx, dw = vjp(dh.astype(F32))
        return dx + dres, dw

    row = lambda i: (i, 0)
    return _ew(name, (t // tm,), fn,
               [(x, (tm, d), row), (w, (1, d), lambda i: (0, 0)), (dh, (tm, d), row), (dres, (tm, d), row)],
               [(SDS((t, d), F32), (tm, d), row, False), (SDS((1, d), F32), (1, d), lambda i: (0, 0), True)],
               acc_axes=(0,))


def _loss_and_grad(y, target):
    t, d = y.shape
    tm = _row_tile(t, 512)

    def fn(y, tg):
        e = y - tg
        return e * (1.0 / d), jnp.sum(e * e, axis=0, keepdims=True) * (0.5 / d)

    row = lambda i: (i, 0)
    return _ew("loss_head", (t // tm,), fn, [(y, (tm, d), row), (target, (tm, d), row)],
               [(SDS((t, d), F32), (tm, d), row, False), (SDS((1, d), F32), (1, d), lambda i: (0, 0), True)],
               acc_axes=(0,))


def _ffn_fwd(tag, x, nw, wf, n):
    t, d = x.shape
    h = _rms_fwd(tag + "_rms", x, nw)
    tm, tn = _row_tile(t, 1024), 256

    def epi(accs):
        g, u = accs
        return g, u, _silu(g) * u

    hblk = (h, (tm, d), lambda i, j, k: (i, 0))
    col = lambda i, j: (i, j)
    g, u, a = _fmm(
        tag + "_up", (t // tm, D_FF // tn),
        [hblk + (wf, (None, tn, d), lambda i, j, k: (3 * n, j, 0), NT, 1, 0),
         hblk + (wf, (None, tn, d), lambda i, j, k: (3 * n + 1, j, 0), NT, 1, 1)],
        [(SDS((t, D_FF), BF), (tm, tn), col, False)] * 3, epi=epi, n_acc=2, joint=True)
    tm2 = _row_tile(t, 512)
    y = _fmm(
        tag + "_down", (t // tm2, 1),
        [(a, (tm2, D_FF), lambda i, j, k: (i, 0), wf, (None, D_FF, d), lambda i, j, k: (3 * n + 2, 0, 0), NN, 1, 0)],
        [(SDS((t, d), F32), (tm2, d), lambda i, j: (i, 0), False)],
        extras=[(x, (tm2, d), lambda i, j: (i, 0))],
        epi=lambda accs, xr: xr + 0.5 * accs[0], vmem=VMEM_BIG)[0]
    return y, (x, h, g, u, a)


def _wgrad(name, a, b, m, n, *, tm, tn, tk=512, scale=None, out=None, out_idx=None, out_dtype=BF):
    t = a.shape[0]
    tk = _row_tile(t, tk)
    epi = (lambda accs: accs[0] * scale) if scale is not None else None
    if out is None:
        spec = (SDS((m, n), out_dtype), (tm, tn), lambda i, j: (i, j), False)
    else:
        spec = (SDS(out.shape, out.dtype), (None, tm, tn), lambda i, j: (out_idx, i, j), False)
    return _fmm(name, (m // tm, n // tn),
                [(a, (tk, tm), lambda i, j, k: (k, i), b, (tk, tn), lambda i, j, k: (k, j), TN, t // tk, 0)],
                [spec], epi=epi, acc_shape=(tm, tn), alias=out, vmem=VMEM_BIG)[0]


def _ffn_bwd(tag, dy, nw, wf, n, saved, dg_slab):
    x, h, g, u, a = saved
    t, d = x.shape
    tm, tn = _row_tile(t, 1024), 256

    def epi(accs, g, u):
        da = 0.5 * accs[0]
        g = g.astype(F32)
        u = u.astype(F32)
        s = _sigmoid(g)
        return da * u * (s * (1.0 + g * (1.0 - s))), da * (g * s)

    col = lambda i, j: (i, j)
    dg, du = _fmm(
        tag + "_bwd_act", (t // tm, D_FF // tn),
        [(dy, (tm, d), lambda i, j, k: (i, 0), wf, (None, tn, d), lambda i, j, k: (3 * n + 2, j, 0), NT, 1, 0)],
        [(SDS((t, D_FF), BF), (tm, tn), col, False)] * 2,
        extras=[(g, (tm, tn), col), (u, (tm, tn), col)], epi=epi)
    tm2 = _row_tile(t, 256)
    dh = _fmm(
        tag + "_bwd_dh", (t // tm2, 1),
        [(dg, (tm2, D_FF), lambda i, j, k: (i, 0), wf, (None, D_FF, d), lambda i, j, k: (3 * n, 0, 0), NN, 1, 0),
         (du, (tm2, D_FF), lambda i, j, k: (i, 0), wf, (None, D_FF, d), lambda i, j, k: (3 * n + 1, 0, 0), NN, 1, 0)],
        [(SDS((t, d), BF), (tm2, d), lambda i, j: (i, 0), False)], vmem=VMEM_BIG, joint=True)[0]
    dx, dnw = _rms_bwd(tag + "_bwd_rms", x, nw, dh, dy)
    half = D_FF // 2
    dg_slab = _wgrad(tag + "_wg", dg, h, D_FF, d, tm=half, tn=d, out=dg_slab, out_idx=3 * n)
    dg_slab = _wgrad(tag + "_wu", du, h, D_FF, d, tm=half, tn=d, out=dg_slab, out_idx=3 * n + 1)
    dg_slab = _wgrad(tag + "_wd", a, dy, D_FF, d, tm=half, tn=d, scale=0.5, out=dg_slab, out_idx=3 * n + 2)
    return dx, dnw, dg_slab


def _shift_down(cur, prev8, j):
    rolled = pltpu.roll(cur, j, 0)
    sub = lax.broadcasted_iota(jnp.int32, prev8.shape, 0)
    top = jnp.where(sub < j, pltpu.roll(prev8, j, 0), rolled[:8])
    return jnp.concatenate([top, rolled[8:]], axis=0)


def _shift_up(cur, next8, j):
    n = cur.shape[0]
    rolled = pltpu.roll(cur, n - j, 0)
    sub = lax.broadcasted_iota(jnp.int32, next8.shape, 0)
    bot = jnp.where(sub >= 8 - j, pltpu.roll(next8, 8 - j, 0), rolled[n - 8:])
    return jnp.concatenate([rolled[: n - 8], bot], axis=0)


HALO = 16


def _conv_fwd(xbc, w, b, seq):
    t, c = xbc.shape
    ts, tc = _row_tile(seq, 256), 512
    tiles_per_seq = seq // ts
    hb = ts // HALO

    def fn(cur, prev, w, b):
        i = pl.program_id(1)
        cur = cur.astype(F32)
        prev8 = jnp.where(i % tiles_per_seq == 0, 0.0, prev.astype(F32)[HALO - 8:])
        out = b + w[3:4] * cur
        for j in range(1, CONV_K):
            out = out + w[3 - j:4 - j] * _shift_down(cur, prev8, j)
        return out, _silu(out)

    return _ew("ssm_conv_fwd", (c // tc, t // ts), fn,
               [(xbc, (ts, tc), lambda j, i: (i, j)),
                (xbc, (HALO, tc), lambda j, i: (jnp.maximum(i * hb - 1, 0), j)),
                (w, (CONV_K, tc), lambda j, i: (0, j)), (b, (1, tc), lambda j, i: (0, j))],
               [(SDS((t, c), BF), (ts, tc), lambda j, i: (i, j), False)] * 2)


def _conv_bwd(tag, dxa, cpre, xbc, w, col0, seq):
    t, width = dxa.shape
    ts, tc = _row_tile(seq, 256), 512
    tiles_per_seq = seq // ts
    hb = ts // HALO
    cb0 = col0 // tc
    n_halo_blocks = t // HALO

    def dsilu(cv, dv):
        cv = cv.astype(F32)
        s = _sigmoid(cv)
        return dv.astype(F32) * (s * (1.0 + cv * (1.0 - s)))

    def fn(dxa_c, dxa_n, c_c, c_n, x_c, x_p, w):
        i = pl.program_id(1)
        dc = dsilu(c_c, dxa_c)
        last = i % tiles_per_seq == tiles_per_seq - 1
        dc_n = jnp.where(last, 0.0, dsilu(c_n, dxa_n)[:8])
        dx = w[3:4] * dc
        for j in range(1, CONV_K):
            dx = dx + w[3 - j:4 - j] * _shift_up(dc, dc_n, j)
        cur = x_c.astype(F32)
        prev8 = jnp.where(i % tiles_per_seq == 0, 0.0, x_p.astype(F32)[HALO - 8:])
        rows = [jnp.sum(dc * cur, axis=0, keepdims=True)]
        for j in range(1, CONV_K):
            rows.append(jnp.sum(dc * _shift_down(cur, prev8, j), axis=0, keepdims=True))
        sub8 = lax.broadcasted_iota(jnp.int32, (8, dc.shape[1]), 0)
        dw = jnp.zeros((8, dc.shape[1]), F32)
        for kk in range(CONV_K):
            dw = jnp.where(sub8 == kk, rows[CONV_K - 1 - kk], dw)
        return dx, dw, jnp.sum(dc, axis=0, keepdims=True)

    nxt = lambda j, i: (jnp.minimum((i + 1) * hb, n_halo_blocks - 1), j)
    nxt_off = lambda j, i: (jnp.minimum((i + 1) * hb, n_halo_blocks - 1), j + cb0)
    return _ew(tag, (width // tc, t // ts), fn,
               [(dxa, (ts, tc), lambda j, i: (i, j)), (dxa, (HALO, tc), nxt),
                (cpre, (ts, tc), lambda j, i: (i, j + cb0)), (cpre, (HALO, tc), nxt_off),
                (xbc, (ts, tc), lambda j, i: (i, j + cb0)),
                (xbc, (HALO, tc), lambda j, i: (jnp.maximum(i * hb - 1, 0), j + cb0)),
                (w, (CONV_K, tc), lambda j, i: (0, j + cb0))],
               [(SDS((t, width), BF), (ts, tc), lambda j, i: (i, j), False),
                (SDS((8, width), F32), (8, tc), lambda j, i: (0, j), True),
                (SDS((1, width), F32), (1, tc), lambda j, i: (0, j), True)],
               acc_axes=(1,))


def _ssd_chunk(xs, bm, cm, dtr, st, dtb, alog, dsk):
    ell = xs.shape[0]
    xs = xs.astype(F32)
    lane = lax.broadcasted_iota(jnp.int32, (ell, LANES), 1)
    sub = lax.broadcasted_iota(jnp.int32, (ell, LANES), 0)
    lane1 = lax.broadcasted_iota(jnp.int32, (1, LANES), 1)
    causal = sub >= lane
    dt = _softplus(dtr + dtb)
    da = dt * (-jnp.exp(alog))
    acs = _dot_hi(causal.astype(F32), da)
    acs_t = acs.T
    cb = _dotf(cm, bm, NT)
    lo = lane < 64
    ys, news = [], []
    for p in range(2):
        xp = xs[:, LANES * p:LANES * (p + 1)]
        sp = st[LANES * p:LANES * (p + 1), :]
        col, dtc, last, dsel = [], [], [], []
        y_diag = None
        for q in range(2):
            r = 2 * p + q
            col_r = jnp.sum(jnp.where(lane == r, acs, 0.0), axis=1, keepdims=True)
            row_r = jnp.sum(jnp.where(sub == r, acs_t, 0.0), axis=0, keepdims=True)
            dtc_r = jnp.sum(jnp.where(lane == r, dt, 0.0), axis=1, keepdims=True)
            decay = jnp.exp(jnp.where(causal, col_r - row_r, -jnp.inf))
            head = lo if q == 0 else jnp.logical_not(lo)
            d = _dotf(cb * decay, jnp.where(head, xp * dtc_r, 0.0), NN)
            y_diag = d if y_diag is None else y_diag + d
            col.append(col_r)
            dtc.append(dtc_r)
            last.append(jnp.sum(jnp.where(sub[:, :1] == ell - 1, col_r, 0.0), axis=0, keepdims=True))
            dsel.append(jnp.sum(jnp.where(lane1 == r, dsk, 0.0), axis=1, keepdims=True))
        y_off = _dotf(cm, sp, NT) * jnp.where(lo, jnp.exp(col[0]), jnp.exp(col[1]))
        xw = jnp.where(lo, xp * (dtc[0] * jnp.exp(last[0] - col[0])), xp * (dtc[1] * jnp.exp(last[1] - col[1])))
        new = sp * jnp.where(sub < 64, jnp.exp(last[0]), jnp.exp(last[1])) + _dotf(xw, bm, TN)
        ys.append(y_diag + y_off + jnp.where(lo, dsel[0], dsel[1]) * xp)
        news.append(new)
    return jnp.concatenate(ys, axis=1), jnp.concatenate(news, axis=0)


def _ssd_specs(nb, nc):
    row = lambda g, b, c: (b * nc + c, g)
    return row, [
        ((CHUNK, 2 * LANES), row),
        ((CHUNK, LANES), lambda g, b, c: (b * nc + c, 16 + g)),
        ((CHUNK, LANES), lambda g, b, c: (b * nc + c, 24 + g)),
    ]


def _ssd_fwd(xa, dtr, dtb, alog, dsk, nb, seq):
    t = xa.shape[0]
    nc = seq // CHUNK
    row, xspecs = _ssd_specs(nb, nc)
    par = ((1, LANES), lambda g, b, c: (0, g))

    def body(xs, bm, cm, dtr, dtb, alog, dsk, y_ref, st_out, st_ref):
        @pl.when(pl.program_id(2) == 0)
        def _():
            st_ref[...] = jnp.zeros_like(st_ref)

        st = st_ref[...]
        st_out[...] = st
        y, new = _ssd_chunk(xs[...], bm[...], cm[...], dtr[...], st, dtb[...], alog[...], dsk[...])
        y_ref[...] = y
        st_ref[...] = new

    specs = [pl.BlockSpec(b, i) for b, i in xspecs] + [pl.BlockSpec((CHUNK, LANES), row)] + [pl.BlockSpec(*par)] * 3
    return pl.pallas_call(
        body, name="ssd_fwd", grid=(SSM_GROUPS, nb, nc), in_specs=specs,
        out_specs=[pl.BlockSpec((CHUNK, 2 * LANES), row),
                   pl.BlockSpec((None, None, None, 2 * LANES, LANES), lambda g, b, c: (g, b, c, 0, 0))],
        out_shape=[SDS((t, D_INNER), F32), SDS((SSM_GROUPS, nb, nc, 2 * LANES, LANES), F32)],
        scratch_shapes=[pltpu.VMEM((2 * LANES, LANES), F32)],
        compiler_params=_cparams(3),
    )(xa, xa, xa, dtr, dtb, alog, dsk)


def _ssd_bwd(xa, dtr, dtb, alog, dsk, states, dy, nb, seq):
    t = xa.shape[0]
    nc = seq // CHUNK
    rev = lambda c: nc - 1 - c
    row = lambda g, b, c: (b * nc + rev(c), g)
    par = ((1, LANES), lambda g, b, c: (0, g))

    def body(xs, bm, cm, dtr, dtb, alog, dsk, st_in, dy, dxs, dbm, dcm, ddtr, ddtb, dalog, ddsk, dst_ref):
        @pl.when(pl.program_id(2) == 0)
        def _():
            dst_ref[...] = jnp.zeros_like(dst_ref)

        _, vjp = jax.vjp(_ssd_chunk, xs[...], bm[...], cm[...], dtr[...], st_in[...], dtb[...], alog[...], dsk[...])
        g = vjp((dy[...], dst_ref[...]))
        dxs[...] = g[0]
        dbm[...] = g[1]
        dcm[...] = g[2]
        ddtr[...] = g[3]
        dst_ref[...] = g[4]
        first = (pl.program_id(1) == 0) & (pl.program_id(2) == 0)
        for o, v in ((ddtb, g[5]), (dalog, g[6]), (ddsk, g[7])):
            @pl.when(first)
            def _(o=o, v=v):
                o[...] = v

            @pl.when(jnp.logical_not(first))
            def _(o=o, v=v):
                o[...] += v

    in_specs = [
        pl.BlockSpec((CHUNK, 2 * LANES), row),
        pl.BlockSpec((CHUNK, LANES), lambda g, b, c: (b * nc + rev(c), 16 + g)),
        pl.BlockSpec((CHUNK, LANES), lambda g, b, c: (b * nc + rev(c), 24 + g)),
        pl.BlockSpec((CHUNK, LANES), row),
        pl.BlockSpec(*par), pl.BlockSpec(*par), pl.BlockSpec(*par),
        pl.BlockSpec((None, None, None, 2 * LANES, LANES), lambda g, b, c: (g, b, rev(c), 0, 0)),
        pl.BlockSpec((CHUNK, 2 * LANES), row),
    ]
    out_specs = [pl.BlockSpec((CHUNK, 2 * LANES), row), pl.BlockSpec((CHUNK, LANES), row),
                 pl.BlockSpec((CHUNK, LANES), row), pl.BlockSpec((CHUNK, LANES), row),
                 pl.BlockSpec(*par), pl.BlockSpec(*par), pl.BlockSpec(*par)]
    out_shape = [SDS((t, D_INNER), BF), SDS((t, SSM_GROUPS * LANES), BF), SDS((t, SSM_GROUPS * LANES), BF),
                 SDS((t, SSM_GROUPS * LANES), F32)] + [SDS((1, SSM_GROUPS * LANES), F32)] * 3
    return pl.pallas_call(
        body, name="ssd_bwd", grid=(SSM_GROUPS, nb, nc), in_specs=in_specs, out_specs=out_specs,
        out_shape=out_shape, scratch_shapes=[pltpu.VMEM((2 * LANES, LANES), F32)],
        compiler_params=_cparams(3, VMEM_BIG),
    )(xa, xa, xa, dtr, dtb, alog, dsk, states, dy)


def _gated_fn(y, z, w):
    g = y * _silu(z.astype(F32))
    return g * lax.rsqrt(jnp.mean(g * g, axis=-1, keepdims=True) + EPS) * w


def _gated_norm_fwd(y, z, w):
    t = y.shape[0]
    tm = _row_tile(t, 512)
    blk = ((tm, NORM_GROUP), lambda g, i: (i, g))
    return _ew("ssm_gnorm_fwd", (SSM_GROUPS, t // tm), _gated_fn,
               [(y,) + blk, (z,) + blk, (w, (1, NORM_GROUP), lambda g, i: (0, g))],
               [(SDS((t, D_INNER), BF),) + blk + (False,)])[0]


def _gated_norm_bwd(y, z, w, dout):
    t = y.shape[0]
    tm = _row_tile(t, 512)
    blk = ((tm, NORM_GROUP), lambda g, i: (i, g))
    par = ((1, NORM_GROUP), lambda g, i: (0, g))

    def fn(y, z, w, dout):
        _, vjp = jax.vjp(_gated_fn, y, z, w)
        return vjp(dout.astype(F32))

    return _ew("ssm_gnorm_bwd", (SSM_GROUPS, t // tm), fn,
               [(y,) + blk, (z,) + blk, (w,) + par, (dout,) + blk],
               [(SDS((t, D_INNER), F32),) + blk + (False,), (SDS((t, D_INNER), BF),) + blk + (False,),
                (SDS((1, D_INNER), F32),) + par + (True,)],
               acc_axes=(1,))


def _proj_nt(name, h, wt, n, out_dtype, tn=256):
    t, kdim = h.shape
    tm = _row_tile(t, 1024)
    return _fmm(name, (t // tm, n // tn),
                [(h, (tm, kdim), lambda i, j, k: (i, 0), wt, (tn, kdim), lambda i, j, k: (j, 0), NT, 1, 0)],
                [(SDS((t, n), out_dtype), (tm, tn), lambda i, j: (i, j), False)])[0]


def _seg_nn(name, parts, n, out_dtype, tk=256):
    t = parts[0][0].shape[0]
    tm = _row_tile(t, 512)
    pairs = []
    for a, w, row0 in parts:
        kp = a.shape[1]
        tkp = min(tk, kp)
        r0 = row0 // tkp
        pairs.append((a, (tm, tkp), lambda i, j, k: (i, k), w, (tkp, n), lambda i, j, k, r0=r0: (k + r0, 0),
                      NN, kp // tkp, 0))
    return _fmm(name, (t // tm, 1), pairs, [(SDS((t, n), out_dtype), (tm, n), lambda i, j: (i, 0), False)],
                acc_shape=(tm, n))[0]


def _ssm_fwd(x, nw, ws, small, nb, seq):
    t, d = x.shape
    h = _rms_fwd("ssm_rms", x, nw)
    z = _proj_nt("ssm_in_z", h, ws["z_t"], D_INNER, BF)
    xbc = _proj_nt("ssm_in_xbc", h, ws["xbc_t"], CONV_DIM, BF)
    dtr = _proj_nt("ssm_in_dt", h, ws["dt_t"], SSM_GROUPS * LANES, F32)
    cpre, xa = _conv_fwd(xbc, small["conv_w"], small["conv_b"], seq)
    y, states = _ssd_fwd(xa, dtr, small["dt_bias"], small["a_log"], small["d_skip"], nb, seq)
    gn = _gated_norm_fwd(y, z, small["ssm_norm_w"])
    tm = _row_tile(t, 512)
    out = _fmm("ssm_out", (t // tm, 1),
               [(gn, (tm, D_INNER), lambda i, j, k: (i, 0), ws["out"], (D_INNER, d), lambda i, j, k: (0, 0), NN, 1, 0)],
               [(SDS((t, d), F32), (tm, d), lambda i, j: (i, 0), False)],
               extras=[(x, (tm, d), lambda i, j: (i, 0))], epi=lambda accs, xr: xr + accs[0])[0]
    return out, (x, h, z, xbc, dtr, cpre, xa, y, states, gn)


def _ssm_bwd(dy, nw, ws, small, saved, nb, seq):
    x, h, z, xbc, dtr, cpre, xa, y, states, gn = saved
    t, d = x.shape
    dgn = _proj_nt("ssm_bwd_dgn", dy, ws["out"], D_INNER, BF)
    d_out = _wgrad("ssm_w_out_g", gn, dy, D_INNER, d, tm=D_INNER // 2, tn=d)
    dyssd, dz, d_normw = _gated_norm_bwd(y, z, small["ssm_norm_w"], dgn)
    dxs, dbm, dcm, ddtr, d_dtb, d_alog, d_dsk = _ssd_bwd(
        xa, dtr, small["dt_bias"], small["a_log"], small["d_skip"], states, dyssd, nb, seq)
    dxbc_x, dcw_x, dcb_x = _conv_bwd("ssm_conv_bwd_x", dxs, cpre, xbc, small["conv_w"], 0, seq)
    dxbc_b, dcw_b, dcb_b = _conv_bwd("ssm_conv_bwd_b", dbm, cpre, xbc, small["conv_w"], D_INNER, seq)
    dxbc_c, dcw_c, dcb_c = _conv_bwd("ssm_conv_bwd_c", dcm, cpre, xbc, small["conv_w"], D_INNER + 1024, seq)
    dh = _seg_nn("ssm_bwd_dh", [(dz, ws["z_t"], 0), (dxbc_x, ws["xbc_t"], 0), (dxbc_b, ws["xbc_t"], D_INNER),
                                (dxbc_c, ws["xbc_t"], D_INNER + 1024), (ddtr, ws["dt_t"], 0)], d, BF, tk=1024)
    dx, dnw = _rms_bwd("ssm_bwd_rms", x, nw, dh, dy)
    g = {
        "z_t": _wgrad("ssm_w_z_g", dz, h, D_INNER, d, tm=1024, tn=d, out_dtype=F32),
        "x_t": _wgrad("ssm_w_x_g", dxbc_x, h, D_INNER, d, tm=1024, tn=d, out_dtype=F32),
        "b_t": _wgrad("ssm_w_b_g", dxbc_b, h, 1024, d, tm=1024, tn=d, out_dtype=F32),
        "c_t": _wgrad("ssm_w_c_g", dxbc_c, h, 1024, d, tm=1024, tn=d, out_dtype=F32),
        "dt_t": _wgrad("ssm_w_dt_g", ddtr, h, 1024, d, tm=1024, tn=d, out_dtype=F32),
        "out": d_out,
        "conv_w": jnp.concatenate([dcw_x[:CONV_K], dcw_b[:CONV_K], dcw_c[:CONV_K]], axis=1),
        "conv_b": jnp.concatenate([dcb_x, dcb_b, dcb_c], axis=1),
        "dt_bias": d_dtb, "a_log": d_alog, "d_skip": d_dsk, "ssm_norm_w": d_normw,
    }
    return dx, dnw, g


def _lane_masks(shape):
    lane = lax.broadcasted_iota(jnp.int32, shape, len(shape) - 1)
    return lane < QK_NOPE, (lane >= QK_NOPE) & (lane < QK_DIM)


def _swap_matrix():
    p = np.zeros((HEAD_PAD, HEAD_PAD), np.float32)
    for i in range(QK_ROPE // 2):
        p[QK_NOPE + QK_ROPE // 2 + i, QK_NOPE + i] = 1.0
        p[QK_NOPE + i, QK_NOPE + QK_ROPE // 2 + i] = 1.0
    return jnp.asarray(p)


def _rope_tables(positions_col):
    t = positions_col.shape[0]
    tm = _row_tile(t, 512)
    freq = np.zeros((1, HEAD_PAD), np.float32)
    inv = 1.0 / (ROPE_THETA ** (np.arange(0, QK_ROPE, 2, dtype=np.float32) / QK_ROPE))
    freq[0, QK_NOPE:QK_NOPE + QK_ROPE // 2] = inv
    freq[0, QK_NOPE + QK_ROPE // 2:QK_DIM] = inv
    sign = np.zeros((1, HEAD_PAD), np.float32)
    sign[0, QK_NOPE:QK_NOPE + QK_ROPE // 2] = -1.0
    sign[0, QK_NOPE + QK_ROPE // 2:QK_DIM] = 1.0

    def fn(pos, freq, sign):
        ang = pos.astype(F32) * freq
        nope, rope = _lane_masks(ang.shape)
        return jnp.where(nope, 1.0, jnp.where(rope, jnp.cos(ang), 0.0)), jnp.sin(ang) * sign

    row = lambda i: (i, 0)
    par = ((1, HEAD_PAD), lambda i: (0, 0))
    return _ew("mla_rope_tables", (t // tm,), fn,
               [(positions_col, (tm, 1), row), (jnp.asarray(freq),) + par, (jnp.asarray(sign),) + par],
               [(SDS((t, HEAD_PAD), F32), (tm, HEAD_PAD), row, False)] * 2)


def _rope(xn, cos, sin_signed, swap):
    return xn * cos + _dot_hi(xn, swap) * sin_signed


def _krope_fn(kr, w, cos, sin_signed, swap):
    _, rope = _lane_masks(kr.shape)
    ss = jnp.sum(jnp.where(rope, kr * kr, 0.0), axis=-1, keepdims=True)
    xn = jnp.where(rope, kr * lax.rsqrt(ss * (1.0 / QK_ROPE) + EPS) * w, 0.0)
    return _rope(xn, cos, sin_signed, swap)


def _head_fn(q, kv, kr, cos, sin_signed, qn, kn, swap):
    nope, rope = _lane_masks(q.shape)

    def rstd(x, mask, n):
        return lax.rsqrt(jnp.sum(jnp.where(mask, x * x, 0.0), axis=-1, keepdims=True) * (1.0 / n) + EPS)

    qs = jnp.where(nope, rstd(q, nope, QK_NOPE), rstd(q, rope, QK_ROPE))
    qp = _rope(jnp.where(nope | rope, q * qs * qn, 0.0), cos, sin_signed, swap)
    kp = jnp.where(nope, kv * rstd(kv, nope, QK_NOPE) * kn, 0.0) + kr
    vp = jnp.where(nope, 0.0, kv)
    return qp, kp, vp


def _heads_fwd(q_raw, kv_raw, kr, cos, sin_signed, qn, kn, swap):
    nh, t, _ = q_raw.shape
    tm = _row_tile(t, 512)
    hblk = ((None, tm, HEAD_PAD), lambda i, h: (h, i, 0))
    tblk = ((tm, HEAD_PAD), lambda i, h: (i, 0))
    par = ((1, HEAD_PAD), lambda i, h: (0, 0))
    sw = ((HEAD_PAD, HEAD_PAD), lambda i, h: (0, 0))
    def fn(*tiles):
        qp, kp, vp = _head_fn(*tiles)
        return qp * Q_PRESCALE, kp, vp

    return _ew("mla_heads_fwd", (t // tm, nh), fn,
               [(q_raw,) + hblk, (kv_raw,) + hblk, (kr,) + tblk, (cos,) + tblk, (sin_signed,) + tblk,
                (qn,) + par, (kn,) + par, (swap,) + sw],
               [(SDS((nh, t, HEAD_PAD), BF),) + hblk + (False,)] * 3)


def _heads_bwd(q_raw, kv_raw, kr, cos, sin_signed, qn, kn, swap, dqp, dkp, dvp):
    nh, t, _ = q_raw.shape
    tm = _row_tile(t, 512)
    hblk = ((None, tm, HEAD_PAD), lambda i, h: (h, i, 0))
    tblk = ((tm, HEAD_PAD), lambda i, h: (i, 0))
    par = ((1, HEAD_PAD), lambda i, h: (0, 0))
    sw = ((HEAD_PAD, HEAD_PAD), lambda i, h: (0, 0))

    def body(q, kv, kr, cos, sn, qn, kn, swap, dqp, dkp, dvp, dq, dkv, dkr, dqn, dkn):
        f = lambda q, kv, kr, qn, kn: _head_fn(q, kv, kr, cos[...], sn[...], qn, kn, swap[...])
        _, vjp = jax.vjp(f, q[...], kv[...], kr[...], qn[...], kn[...])
        g = vjp((dqp[...].astype(F32), dkp[...].astype(F32), dvp[...].astype(F32)))
        dq[...] = g[0].astype(dq.dtype)
        dkv[...] = g[1].astype(dkv.dtype)
        h0 = pl.program_id(1) == 0
        first = h0 & (pl.program_id(0) == 0)
        for o, v, c in ((dkr, g[2], h0), (dqn, g[3], first), (dkn, g[4], first)):
            @pl.when(c)
            def _(o=o, v=v):
                o[...] = v

            @pl.when(jnp.logical_not(c))
            def _(o=o, v=v):
                o[...] += v

    spec = lambda b: pl.BlockSpec(*b)
    return pl.pallas_call(
        body, name="mla_heads_bwd", grid=(t // tm, nh),
        in_specs=[spec(hblk), spec(hblk), spec(tblk), spec(tblk), spec(tblk), spec(par), spec(par), spec(sw),
                  spec(hblk), spec(hblk), spec(hblk)],
        out_specs=[spec(hblk), spec(hblk), spec(tblk), spec(par), spec(par)],
        out_shape=[SDS((nh, t, HEAD_PAD), BF), SDS((nh, t, HEAD_PAD), BF), SDS((t, HEAD_PAD), F32),
                   SDS((1, HEAD_PAD), F32), SDS((1, HEAD_PAD), F32)],
        compiler_params=_cparams(2),
    )(q_raw, kv_raw, kr, cos, sin_signed, qn, kn, swap, dqp, dkp, dvp)


ATT_TILE = 512
ATT_SCALE = QK_DIM ** -0.5
LOG2E = 1.4426950408889634
LN2 = 0.6931471805599453
Q_PRESCALE = ATT_SCALE * LOG2E


def _flash_fwd(qs, k, v, nb, seq):
    nh, t, dh = qs.shape
    tq = _row_tile(seq, ATT_TILE)
    nq = seq // tq

    def body(q_ref, k_ref, v_ref, o_ref, lse_ref):
        qi = pl.program_id(2)
        qt = q_ref[...]

        def tile(j, carry, diagonal):
            m, l, acc = carry
            rows = pl.ds(pl.multiple_of(j * tq, tq), tq)
            s = _dotf(qt, k_ref[rows, :], NT)
            if diagonal:
                r = lax.broadcasted_iota(jnp.int32, (tq, tq), 0)
                c = lax.broadcasted_iota(jnp.int32, (tq, tq), 1)
                s = jnp.where(c <= r, s, -jnp.inf)
            m_new = jnp.maximum(m, jnp.max(s, axis=-1, keepdims=True))
            alpha = jnp.exp2(m - m_new)
            p = jnp.exp2(s - m_new)
            return m_new, alpha * l + jnp.sum(p, axis=-1, keepdims=True), alpha * acc + _dotf(p, v_ref[rows, :], NN)

        init = (jnp.full((tq, 1), -jnp.inf, F32), jnp.zeros((tq, 1), F32), jnp.zeros((tq, dh), F32))
        carry = lax.fori_loop(0, qi, lambda j, c: tile(j, c, False), init)
        m, l, acc = tile(qi, carry, True)
        o_ref[...] = (acc / l).astype(o_ref.dtype)
        lse_ref[...] = m + jnp.log2(l)

    qblk = pl.BlockSpec((None, tq, dh), lambda h, b, i: (h, b * nq + i, 0))
    kblk = pl.BlockSpec((None, seq, dh), lambda h, b, i: (h, b, 0))
    return pl.pallas_call(
        body, name="mla_flash_fwd", grid=(nh, nb, nq), in_specs=[qblk, kblk, kblk],
        out_specs=[qblk, pl.BlockSpec((None, tq, 1), lambda h, b, i: (h, b * nq + i, 0))],
        out_shape=[SDS((nh, t, dh), BF), SDS((nh, t, 1), F32)],
        compiler_params=_cparams(3, VMEM_BIG),
    )(qs, k, v)


def _flash_bwd(qs, k, v, o, lse, do, nb, seq):
    nh, t, dh = qs.shape
    tq = _row_tile(seq, ATT_TILE)
    nq = seq // tq

    def row_of(col):
        return jnp.broadcast_to(col, (tq, LANES)).T[0:1, :]

    def body(q_ref, k_ref, v_ref, o_ref, lse_ref, do_ref, dq_ref, dk_ref, dv_ref, kt_sc, lrow_sc, drow_sc, dqt_sc):
        for c in range(nq):
            rows = pl.ds(c * tq, tq)
            kt_sc[c] = k_ref[rows, :].T
            delta = jnp.sum(do_ref[rows, :].astype(F32) * o_ref[rows, :].astype(F32), axis=-1, keepdims=True)
            drow_sc[c] = row_of(delta)
            lrow_sc[c] = row_of(lse_ref[rows, :])
        dqt_sc[...] = jnp.zeros_like(dqt_sc)

        def kv_step(j, _):
            rows_j = pl.ds(pl.multiple_of(j * tq, tq), tq)
            ks, vs, kt = k_ref[rows_j, :], v_ref[rows_j, :], kt_sc[j]

            def q_tile(i, carry, diagonal):
                dk, dv = carry
                rows_i = pl.ds(pl.multiple_of(i * tq, tq), tq)
                qt, dot_ = q_ref[rows_i, :], do_ref[rows_i, :]
                pt = jnp.exp2(_dotf(ks, qt, NT) - lrow_sc[i])
                if diagonal:
                    kk = lax.broadcasted_iota(jnp.int32, (tq, tq), 0)
                    qq = lax.broadcasted_iota(jnp.int32, (tq, tq), 1)
                    pt = jnp.where(kk <= qq, pt, 0.0)
                dst = (pt * (_dotf(vs, dot_, NT) - drow_sc[i])).astype(BF)
                dqt_sc[i] += _dotf(kt, dst, NN)
                return dk + _dotf(dst, qt, NN), dv + _dotf(pt, dot_, NN)

            zero = jnp.zeros((tq, dh), F32)
            carry = q_tile(j, (zero, zero), True)
            dk, dv = lax.fori_loop(j + 1, nq, lambda i, c: q_tile(i, c, False), carry)
            dk_ref[rows_j, :] = dk * LN2
            dv_ref[rows_j, :] = dv
            return 0

        lax.fori_loop(0, nq, kv_step, 0)
        for c in range(nq):
            dq_ref[pl.ds(c * tq, tq), :] = dqt_sc[c].T * ATT_SCALE

    full = pl.BlockSpec((None, seq, dh), lambda h, b: (h, b, 0))
    sfull = pl.BlockSpec((None, seq, 1), lambda h, b: (h, b, 0))
    return pl.pallas_call(
        body, name="mla_flash_bwd", grid=(nh, nb), in_specs=[full, full, full, full, sfull, full],
        out_specs=[full, full, full], out_shape=[SDS((nh, t, dh), F32)] * 3,
        scratch_shapes=[pltpu.VMEM((nq, dh, tq), BF), pltpu.VMEM((nq, 1, tq), F32), pltpu.VMEM((nq, 1, tq), F32),
                        pltpu.VMEM((nq, dh, tq), F32)],
        compiler_params=_cparams(2, VMEM_BIG),
    )(qs, k, v, o, lse, do)


def _heads_nt(name, a, wt, out_dtype):
    t, kdim = a.shape
    tm = _row_tile(t, 512)
    nw = MLA_HEADS * HEAD_PAD

    def body(a_ref, w_ref, o_ref):
        r = _dotf(a_ref[...], w_ref[...], NT)
        for h in range(MLA_HEADS):
            o_ref[h] = r[:, HEAD_PAD * h:HEAD_PAD * (h + 1)].astype(o_ref.dtype)

    return pl.pallas_call(
        body, name=name, grid=(t // tm,),
        in_specs=[pl.BlockSpec((tm, kdim), lambda i: (i, 0)), pl.BlockSpec((nw, kdim), lambda i: (0, 0))],
        out_specs=pl.BlockSpec((MLA_HEADS, tm, HEAD_PAD), lambda i: (0, i, 0)),
        out_shape=SDS((MLA_HEADS, t, HEAD_PAD), out_dtype), compiler_params=_cparams(1, VMEM_BIG),
    )(a, wt)


def _all_heads(a_ref):
    return jnp.concatenate([a_ref[h] for h in range(MLA_HEADS)], axis=1)


def _heads_nn(name, a, w, n, out_dtype, res=None):
    t = a.shape[1]
    tm = _row_tile(t, 512)
    nw = MLA_HEADS * HEAD_PAD

    def body(*refs):
        a_ref, w_ref, o_ref = refs[0], refs[1], refs[-1]
        r = _dotf(_all_heads(a_ref), w_ref[...], NN)
        if res is not None:
            r = r + refs[2][...]
        o_ref[...] = r.astype(o_ref.dtype)

    row = pl.BlockSpec((tm, n), lambda i: (i, 0))
    in_specs = [pl.BlockSpec((MLA_HEADS, tm, HEAD_PAD), lambda i: (0, i, 0)), pl.BlockSpec((nw, n), lambda i: (0, 0))]
    args = [a, w]
    if res is not None:
        in_specs.append(row)
        args.append(res)
    return pl.pallas_call(body, name=name, grid=(t // tm,), in_specs=in_specs, out_specs=row,
                          out_shape=SDS((t, n), out_dtype), compiler_params=_cparams(1, VMEM_BIG))(*args)


def _heads_wgrad(name, a, b, n):
    t = b.shape[0]
    tk = _row_tile(t, 512)
    nw = MLA_HEADS * HEAD_PAD
    steps = t // tk

    def body(a_ref, b_ref, o_ref, acc):
        k = pl.program_id(0)

        @pl.when(k == 0)
        def _():
            acc[...] = jnp.zeros_like(acc)

        acc[...] += _dotf(_all_heads(a_ref), b_ref[...], TN)

        @pl.when(k == steps - 1)
        def _():
            o_ref[...] = acc[...]

    return pl.pallas_call(
        body, name=name, grid=(steps,),
        in_specs=[pl.BlockSpec((MLA_HEADS, tk, HEAD_PAD), lambda k: (0, k, 0)), pl.BlockSpec((tk, n), lambda k: (k, 0))],
        out_specs=pl.BlockSpec((nw, n), lambda k: (0, 0)), out_shape=SDS((nw, n), F32),
        scratch_shapes=[pltpu.VMEM((nw, n), F32)], compiler_params=_cparams(1, VMEM_BIG),
    )(a, b)


PM_CKV, PM_KR, PM_CQ = 0, KV_LORA, KV_LORA + HEAD_PAD
PM_DIM = KV_LORA + HEAD_PAD + Q_LORA


def _lat_specs(t, tm):
    return (((tm, KV_LORA), lambda i: (i, 0)), ((tm, HEAD_PAD), lambda i: (i, PM_KR // HEAD_PAD)),
            ((tm, Q_LORA), lambda i: (i, PM_CQ // Q_LORA)))


def _mla_fwd(x, nw, wm, small, tables, nb, seq):
    t, d = x.shape
    cos, sin_signed, swap = tables
    h = _rms_fwd("mla_rms", x, nw)
    pm = _proj_nt("mla_in", h, wm["in_t"], PM_DIM, F32, tn=PM_DIM // 3)
    tm = _row_tile(t, 512)
    ckv_s, kr_s, cq_s = _lat_specs(t, tm)
    row = lambda i: (i, 0)
    par = lambda n: ((1, n), lambda i: (0, 0))
    ckvn = _ew("mla_ckv_norm", (t // tm,), _rms_fn, [(pm,) + ckv_s, (small["kv_a_norm"],) + par(KV_LORA)],
               [(SDS((t, KV_LORA), BF), (tm, KV_LORA), row, False)])[0]
    cqn = _ew("mla_cq_norm", (t // tm,), _rms_fn, [(pm,) + cq_s, (small["q_a_norm"],) + par(Q_LORA)],
              [(SDS((t, Q_LORA), BF), (tm, Q_LORA), row, False)])[0]
    tb = ((tm, HEAD_PAD), row)
    kr = _ew("mla_krope", (t // tm,), _krope_fn,
             [(pm,) + kr_s, (small["k_norm"],) + par(HEAD_PAD), (cos,) + tb, (sin_signed,) + tb,
              (swap, (HEAD_PAD, HEAD_PAD), lambda i: (0, 0))],
             [(SDS((t, HEAD_PAD), F32),) + tb + (False,)])[0]
    q_raw = _heads_nt("mla_q_b", cqn, wm["qb_t"], F32)
    kv_raw = _heads_nt("mla_kv_b", ckvn, wm["kvb_t"], F32)
    qp, kp, vp = _heads_fwd(q_raw, kv_raw, kr, cos, sin_signed, small["q_norm"], small["k_norm"], swap)
    o, lse = _flash_fwd(qp, kp, vp, nb, seq)
    out = _heads_nn("mla_out", o, wm["out"], d, F32, res=x)
    return out, (x, h, pm, ckvn, cqn, kr, q_raw, kv_raw, qp, kp, vp, o, lse)


def _mla_bwd(dy, nw, wm, small, tables, saved, nb, seq):
    x, h, pm, ckvn, cqn, kr, q_raw, kv_raw, qp, kp, vp, o, lse = saved
    t, d = x.shape
    cos, sin_signed, swap = tables
    do = _heads_nt("mla_bwd_do", dy, wm["out"], BF)
    g_out = _heads_wgrad("mla_w_out_g", o, dy, d)
    dqp, dkp, dvp = _flash_bwd(qp, kp, vp, o, lse, do, nb, seq)
    dq_raw, dkv_raw, dkr, d_qn, d_kn = _heads_bwd(q_raw, kv_raw, kr, cos, sin_signed, small["q_norm"],
                                                   small["k_norm"], swap, dqp, dkp, dvp)
    dcqn = _heads_nn("mla_bwd_dcq", dq_raw, wm["qb_t"], Q_LORA, F32)
    dckvn = _heads_nn("mla_bwd_dckv", dkv_raw, wm["kvb_t"], KV_LORA, F32)
    g_qb = _heads_wgrad("mla_w_qb_g", dq_raw, cqn, Q_LORA)
    g_kvb = _heads_wgrad("mla_w_kvb_g", dkv_raw, ckvn, KV_LORA)
    tm = _row_tile(t, 512)
    ckv_s, kr_s, cq_s = _lat_specs(t, tm)
    row = lambda i: (i, 0)
    par = lambda n: ((1, n), lambda i: (0, 0))

    def rms_b(xv, w, dv):
        _, vjp = jax.vjp(_rms_fn, xv, w)
        return vjp(dv)

    dckv, d_kva = _ew("mla_ckv_norm_bwd", (t // tm,), rms_b,
                      [(pm,) + ckv_s, (small["kv_a_norm"],) + par(KV_LORA), (dckvn, (tm, KV_LORA), row)],
                      [(SDS((t, KV_LORA), BF), (tm, KV_LORA), row, False),
                       (SDS((1, KV_LORA), F32),) + par(KV_LORA) + (True,)], acc_axes=(0,))
    dcq, d_qa = _ew("mla_cq_norm_bwd", (t // tm,), rms_b,
                    [(pm,) + cq_s, (small["q_a_norm"],) + par(Q_LORA), (dcqn, (tm, Q_LORA), row)],
                    [(SDS((t, Q_LORA), BF), (tm, Q_LORA), row, False),
                     (SDS((1, Q_LORA), F32),) + par(Q_LORA) + (True,)], acc_axes=(0,))
    tb = ((tm, HEAD_PAD), row)

    def kr_b(krv, w, cosv, sinv, sw, dv):
        _, vjp = jax.vjp(lambda a, b: _krope_fn(a, b, cosv, sinv, sw), krv, w)
        return vjp(dv)

    dkr_raw, d_kn2 = _ew("mla_krope_bwd", (t // tm,), kr_b,
                         [(pm,) + kr_s, (small["k_norm"],) + par(HEAD_PAD), (cos,) + tb, (sin_signed,) + tb,
                          (swap, (HEAD_PAD, HEAD_PAD), lambda i: (0, 0)), (dkr,) + tb],
                         [(SDS((t, HEAD_PAD), BF),) + tb + (False,),
                          (SDS((1, HEAD_PAD), F32),) + par(HEAD_PAD) + (True,)], acc_axes=(0,))
    dh = _seg_nn("mla_bwd_dh", [(dckv, wm["in_t"], PM_CKV), (dkr_raw, wm["in_t"], PM_KR),
                                (dcq, wm["in_t"], PM_CQ)], d, BF, tk=128)
    dx, dnw = _rms_bwd("mla_bwd_rms", x, nw, dh, dy)
    g = {
        "in_ckv_t": _wgrad("mla_w_in_ckv_g", dckv, h, KV_LORA, d, tm=KV_LORA, tn=d, out_dtype=F32),
        "in_kr_t": _wgrad("mla_w_in_kr_g", dkr_raw, h, HEAD_PAD, d, tm=HEAD_PAD, tn=d, out_dtype=F32),
        "in_cq_t": _wgrad("mla_w_in_cq_g", dcq, h, Q_LORA, d, tm=Q_LORA, tn=d, out_dtype=F32),
        "qb_t": g_qb, "kvb_t": g_kvb, "out": g_out,
        "q_a_norm": d_qa, "kv_a_norm": d_kva, "q_norm": d_qn, "k_norm": d_kn + d_kn2,
    }
    return dx, dnw, g


def _mesh_pos():
    return lax.axis_index("x"), lax.axis_index("y"), lax.axis_index("c")


def _peer(pos, k):
    x, y, c = pos
    return (x ^ ((k >> 2) & 1), y ^ ((k >> 1) & 1), c ^ (k & 1))


def _flat(pos):
    return 4 * pos[0] + 2 * pos[1] + pos[2]


def _slab(ref, axis, start, size):
    idx = [slice(None)] * axis + [pl.ds(start, size)]
    return ref.at[tuple(idx)]


def _all_gather(name, items):
    n = len(items)
    sizes = [a.shape[ax] for a, ax in items]

    def body(*refs):
        srcs, dsts = refs[:n], refs[n:2 * n]
        send_sems, recv_sems, local_sems = refs[2 * n:]
        pos = _mesh_pos()
        me = _flat(pos)
        local, sends = [], []
        for t, (src, dst) in enumerate(zip(srcs, dsts)):
            ax, sz = items[t][1], sizes[t]
            mine = _slab(dst, ax, me * sz, sz)
            cp = pltpu.make_async_copy(src, mine, local_sems.at[t])
            cp.start()
            local.append(cp)
            for k in range(1, N_DEV):
                rc = pltpu.make_async_remote_copy(
                    src_ref=src, dst_ref=mine, send_sem=send_sems.at[t, k - 1], recv_sem=recv_sems.at[t, k - 1],
                    device_id=_peer(pos, k), device_id_type=pl.DeviceIdType.MESH)
                rc.start()
                sends.append(rc)
        for t, (src, dst) in enumerate(zip(srcs, dsts)):
            ax, sz = items[t][1], sizes[t]
            for k in range(1, N_DEV):
                theirs = _slab(dst, ax, _flat(_peer(pos, k)) * sz, sz)
                pltpu.make_async_remote_copy(
                    src_ref=src, dst_ref=theirs, send_sem=send_sems.at[t, k - 1], recv_sem=recv_sems.at[t, k - 1],
                    device_id=_peer(pos, k), device_id_type=pl.DeviceIdType.MESH).wait_recv()
        for rc in sends:
            rc.wait_send()
        for cp in local:
            cp.wait()

    out_shape = []
    for (a, ax), sz in zip(items, sizes):
        shp = list(a.shape)
        shp[ax] = sz * N_DEV
        out_shape.append(SDS(tuple(shp), a.dtype))
    hbm = pl.BlockSpec(memory_space=pl.ANY)
    return pl.pallas_call(
        body, name=name, in_specs=[hbm] * n, out_specs=[hbm] * n, out_shape=out_shape,
        scratch_shapes=[pltpu.SemaphoreType.DMA((n, N_DEV - 1)), pltpu.SemaphoreType.DMA((n, N_DEV - 1)),
                        pltpu.SemaphoreType.DMA((n,))],
    )(*[a for a, _ in items])


def _scatter_partials(name, items):
    n = len(items)
    sizes = [a.shape[ax] // N_DEV for a, ax in items]

    def body(*refs):
        srcs, dsts = refs[:n], refs[n:2 * n]
        send_sems, recv_sems, local_sems = refs[2 * n:]
        pos = _mesh_pos()
        me = _flat(pos)
        local, sends = [], []
        for t, (src, dst) in enumerate(zip(srcs, dsts)):
            ax, sz = items[t][1], sizes[t]
            cp = pltpu.make_async_copy(_slab(src, ax, me * sz, sz), dst.at[me], local_sems.at[t])
            cp.start()
            local.append(cp)
            for k in range(1, N_DEV):
                peer = _peer(pos, k)
                rc = pltpu.make_async_remote_copy(
                    src_ref=_slab(src, ax, _flat(peer) * sz, sz), dst_ref=dst.at[me],
                    send_sem=send_sems.at[t, k - 1], recv_sem=recv_sems.at[t, k - 1],
                    device_id=peer, device_id_type=pl.DeviceIdType.MESH)
                rc.start()
                sends.append(rc)
        for t, (src, dst) in enumerate(zip(srcs, dsts)):
            ax, sz = items[t][1], sizes[t]
            for k in range(1, N_DEV):
                peer = _peer(pos, k)
                pltpu.make_async_remote_copy(
                    src_ref=_slab(src, ax, me * sz, sz), dst_ref=dst.at[_flat(peer)],
                    send_sem=send_sems.at[t, k - 1], recv_sem=recv_sems.at[t, k - 1],
                    device_id=peer, device_id_type=pl.DeviceIdType.MESH).wait_recv()
        for rc in sends:
            rc.wait_send()
        for cp in local:
            cp.wait()

    out_shape = []
    for (a, ax), sz in zip(items, sizes):
        shp = list(a.shape)
        shp[ax] = sz
        out_shape.append(SDS((N_DEV,) + tuple(shp), a.dtype))
    hbm = pl.BlockSpec(memory_space=pl.ANY)
    return pl.pallas_call(
        body, name=name, in_specs=[hbm] * n, out_specs=[hbm] * n, out_shape=out_shape,
        scratch_shapes=[pltpu.SemaphoreType.DMA((n, N_DEV - 1)), pltpu.SemaphoreType.DMA((n, N_DEV - 1)),
                        pltpu.SemaphoreType.DMA((n,))],
    )(*[a for a, _ in items])


def _adam_math(w, g, m, v):
    m = ADAM_B1 * m + (1.0 - ADAM_B1) * g
    v = ADAM_B2 * v + (1.0 - ADAM_B2) * (g * g)
    m_hat = m / (1.0 - ADAM_B1 ** ADAM_STEP)
    v_hat = v / (1.0 - ADAM_B2 ** ADAM_STEP)
    delta = -ADAM_LR * (m_hat / (jnp.sqrt(v_hat) + ADAM_EPS) + ADAM_WD * w)
    return delta, m, v


def _adam(name, land, land_blk, land_idx, w, m, v, transposed, ck):
    n, r, c = w.shape
    wblk = ((None, ck, c), lambda a, i: (a, i, 0))

    def fn(parts, w, m, v):
        g = parts[0].astype(F32)
        for s in range(1, N_DEV):
            g = g + parts[s].astype(F32)
        if transposed:
            g = g.T
        delta, m2, v2 = _adam_math(w, g, m, v)
        return g, delta, m2, v2

    return _ew(name, (n, r // ck), fn,
               [(land, land_blk, land_idx), (w,) + wblk, (m,) + wblk, (v,) + wblk],
               [(SDS(w.shape, F32),) + wblk + (False,)] * 4, vmem=VMEM_BIG)


def _prep_ffn(gate, up, down):
    def body(g, u, dn, o):
        o[0] = g[...].T.astype(BF)
        o[1] = u[...].T.astype(BF)
        o[2] = dn[...].astype(BF)

    cblk = pl.BlockSpec((None, None, D_MODEL, FF_SHARD), lambda l, i: (l, i, 0, 0))
    rblk = pl.BlockSpec((None, None, FF_SHARD, D_MODEL), lambda l, i: (l, i, 0, 0))
    return pl.pallas_call(
        body, name="prep_ffn", grid=(2, 2), in_specs=[cblk, cblk, rblk],
        out_specs=pl.BlockSpec((3, FF_SHARD, D_MODEL), lambda l, i: (2 * l + i, 0, 0)),
        out_shape=SDS((12, FF_SHARD, D_MODEL), BF), compiler_params=_cparams(2, VMEM_BIG),
    )(gate, up, down)


def _transpose_cast(name, w, dtype):
    def body(a, o):
        o[...] = a[...].T.astype(dtype)

    r, c = w.shape
    return pl.pallas_call(body, name=name, out_shape=SDS((c, r), dtype),
                          compiler_params=pltpu.CompilerParams(vmem_limit_bytes=VMEM_BIG))(w)


SMALL_SHARDED = (("norm_w", 6 * 128), ("conv_w", CONV_K * 512), ("q_a_norm", 48), ("kv_a_norm", 32))
SMALL_PACK = 3072


def _dyn(a, start, size):
    return lax.dynamic_slice_in_dim(a, start, size, axis=a.ndim - 1)


def _layout(ssm_out_all, mla_out_all, ssm_in_t, mla_in_all, qb_all, kvb_all, conv_w, conv_b, dt_bias, a_log, d_skip,
            ssm_norm_w, q_a_norm, kv_a_norm, q_norm, k_norm):
    d = ssm_in_t.shape[1]
    ssm_in_t = ssm_in_t.astype(BF)
    dt_rows = ssm_in_t[D_INNER + CONV_DIM:].reshape(SSM_GROUPS, SSM_HPG, d)
    ws = {"z_t": ssm_in_t[:D_INNER], "xbc_t": ssm_in_t[D_INNER:D_INNER + CONV_DIM],
          "dt_t": jnp.pad(dt_rows, ((0, 0), (0, LANES - SSM_HPG), (0, 0))).reshape(SSM_GROUPS * LANES, d),
          "out": ssm_out_all}
    in_t = mla_in_all.T
    kr_rows = jnp.pad(in_t[Q_LORA + KV_LORA:], ((QK_NOPE, HEAD_PAD - QK_DIM), (0, 0)))
    qb_heads = jnp.pad(qb_all.reshape(MLA_HEADS, QK_DIM, Q_LORA), ((0, 0), (0, HEAD_PAD - QK_DIM), (0, 0)))
    out_heads = jnp.pad(mla_out_all.reshape(MLA_HEADS, 64, d), ((0, 0), (64, 0), (0, 0)))
    wm = {"in_t": jnp.concatenate([in_t[Q_LORA:Q_LORA + KV_LORA], kr_rows, in_t[:Q_LORA]], axis=0),
          "qb_t": qb_heads.reshape(MLA_HEADS * HEAD_PAD, Q_LORA), "kvb_t": kvb_all,
          "out": out_heads.reshape(MLA_HEADS * HEAD_PAD, d)}
    lane_heads = lambda p: jnp.pad(p.reshape(SSM_GROUPS, SSM_HPG), ((0, 0), (0, LANES - SSM_HPG))).reshape(1, -1)
    pad_head = lambda p: jnp.pad(p.reshape(1, QK_DIM), ((0, 0), (0, HEAD_PAD - QK_DIM)))
    small = {"conv_w": conv_w, "conv_b": conv_b, "dt_bias": lane_heads(dt_bias), "a_log": lane_heads(a_log),
             "d_skip": lane_heads(d_skip), "ssm_norm_w": ssm_norm_w, "q_a_norm": q_a_norm, "kv_a_norm": kv_a_norm,
             "q_norm": pad_head(q_norm), "k_norm": pad_head(k_norm)}
    return ws, wm, small


def _local_step(x, positions, loss_target, norm_full, ffn_all, ws, wm, small):
    nb, seq, d = x.shape
    t = nb * seq
    xf = x.reshape(t, d)
    tables = list(_rope_tables(positions.reshape(t, 1))) + [_swap_matrix()]
    x1, s_f0 = _ffn_fwd("ffn0", xf, norm_full[0, 0], ffn_all, 0)
    x2, s_ssm = _ssm_fwd(x1, norm_full[0, 1], ws, small, nb, seq)
    x3, s_f1 = _ffn_fwd("ffn1", x2, norm_full[0, 2], ffn_all, 1)
    x4, s_f2 = _ffn_fwd("ffn2", x3, norm_full[1, 0], ffn_all, 2)
    x5, s_mla = _mla_fwd(x4, norm_full[1, 1], wm, small, tables, nb, seq)
    x6, s_f3 = _ffn_fwd("ffn3", x5, norm_full[1, 2], ffn_all, 3)
    dy, loss_cols = _loss_and_grad(x6, loss_target.reshape(t, d))

    dg_slab = jnp.zeros((12, D_FF, d), BF)
    dx5, dn12, dg_slab = _ffn_bwd("ffn3", dy, norm_full[1, 2], ffn_all, 3, s_f3, dg_slab)
    dx4, dn11, g_mla = _mla_bwd(dx5, norm_full[1, 1], wm, small, tables, s_mla, nb, seq)
    dx3, dn10, dg_slab = _ffn_bwd("ffn2", dx4, norm_full[1, 0], ffn_all, 2, s_f2, dg_slab)
    dx2, dn02, dg_slab = _ffn_bwd("ffn1", dx3, norm_full[0, 2], ffn_all, 1, s_f1, dg_slab)
    dx1, dn01, g_ssm = _ssm_bwd(dx2, norm_full[0, 1], ws, small, s_ssm, nb, seq)
    dx0, dn00, dg_slab = _ffn_bwd("ffn0", dx1, norm_full[0, 0], ffn_all, 0, s_f0, dg_slab)
    return loss_cols, dx0.reshape(nb, seq, d), (dn00, dn01, dn02, dn10, dn11, dn12), dg_slab, g_ssm, g_mla


def kernel(x, positions, norm_w, ffn_w_gate, ffn_w_up, ffn_w_down, ssm_w_in, ssm_conv_w, ssm_conv_b, ssm_dt_bias, ssm_a_log, ssm_d, ssm_norm_w, ssm_w_out, mla_w_in, mla_q_a_norm, mla_kv_a_norm, mla_w_q_b, mla_w_kv_b, mla_q_norm, mla_k_norm, mla_w_out, loss_target, m_norm_w, m_ffn_w_gate, m_ffn_w_up, m_ffn_w_down, m_ssm_w_in, m_ssm_conv_w, m_ssm_conv_b, m_ssm_dt_bias, m_ssm_a_log, m_ssm_d, m_ssm_norm_w, m_ssm_w_out, m_mla_w_in, m_mla_q_a_norm, m_mla_kv_a_norm, m_mla_w_q_b, m_mla_w_kv_b, m_mla_q_norm, m_mla_k_norm, m_mla_w_out, v_norm_w, v_ffn_w_gate, v_ffn_w_up, v_ffn_w_down, v_ssm_w_in, v_ssm_conv_w, v_ssm_conv_b, v_ssm_dt_bias, v_ssm_a_log, v_ssm_d, v_ssm_norm_w, v_ssm_w_out, v_mla_w_in, v_mla_q_a_norm, v_mla_kv_a_norm, v_mla_w_q_b, v_mla_w_kv_b, v_mla_q_norm, v_mla_k_norm, v_mla_w_out):
    nb, seq, d = x.shape
    t = nb * seq
    me = _flat(_mesh_pos())

    ffn_loc = _prep_ffn(ffn_w_gate, ffn_w_up, ffn_w_down)
    ssm_in_loc = _transpose_cast("prep_ssm_in", ssm_w_in[0], F32).reshape(SSM_IN_SHARD, 8, LANES)
    qb_loc = _transpose_cast("prep_q_b", mla_w_q_b[0], BF)
    kvb_loc = _transpose_cast("prep_kv_b", mla_w_kv_b[0], BF)
    small_loc = jnp.concatenate([norm_w.reshape(-1), ssm_conv_w.reshape(-1), mla_q_a_norm.reshape(-1),
                                 mla_kv_a_norm.reshape(-1)])
    small_loc = jnp.pad(small_loc, (0, SMALL_PACK - small_loc.shape[0])).reshape(1, 1, SMALL_PACK)
    (ffn_all, ssm_out_all, mla_out_all, ssm_in_all, mla_in_all, qb_all, kvb_all, small_all) = _all_gather(
        "gather_weights",
        [(ffn_loc, 1), (ssm_w_out[0].astype(BF), 0), (mla_w_out[0].astype(BF), 0), (ssm_in_loc, 0),
         (mla_w_in[0].astype(BF), 0), (qb_loc, 0), (kvb_loc, 0), (small_loc, 0)])

    sm = small_all.reshape(N_DEV, SMALL_PACK)
    norm_full = sm[:, :768].reshape(N_DEV, 6, 128).transpose(1, 0, 2).reshape(2, 3, 1, d)
    conv_w_full = sm[:, 768:768 + 2048].reshape(N_DEV, CONV_K, 512).transpose(1, 0, 2).reshape(CONV_K, CONV_DIM)
    ws, wm, small = _layout(ssm_out_all, mla_out_all, ssm_in_all.reshape(SSM_IN_DIM, d), mla_in_all, qb_all, kvb_all,
                            conv_w_full, ssm_conv_b, ssm_dt_bias, ssm_a_log, ssm_d, ssm_norm_w,
                            sm[:, 2816:2864].reshape(1, Q_LORA), sm[:, 2864:2896].reshape(1, KV_LORA),
                            mla_q_norm, mla_k_norm)
    loss_cols, grad_x, dns, dg_slab, g_ssm, g_mla = _local_step(x, positions, loss_target, norm_full, ffn_all, ws, wm,
                                                                small)
    loss = lax.psum(jnp.sum(loss_cols), ("x", "y", "c"))
    dn00, dn01, dn02, dn10, dn11, dn12 = dns

    heads_of = lambda a: a.reshape(SSM_GROUPS, LANES, -1)[:, :SSM_HPG].reshape(SSM_HEADS, -1)
    g_ssm_in_t = jnp.concatenate([g_ssm["z_t"], g_ssm["x_t"], g_ssm["b_t"], g_ssm["c_t"], heads_of(g_ssm["dt_t"])],
                                 axis=0).reshape(SSM_IN_DIM, 8, LANES)
    g_mla_in = jnp.concatenate([g_mla["in_cq_t"], g_mla["in_ckv_t"], g_mla["in_kr_t"][QK_NOPE:QK_DIM]], axis=0).T
    g_qb = g_mla["qb_t"].reshape(MLA_HEADS, HEAD_PAD, Q_LORA)[:, :QK_DIM].reshape(MLA_HEADS * QK_DIM, Q_LORA)
    g_mla_out = g_mla["out"].reshape(MLA_HEADS, HEAD_PAD, d)[:, 64:].reshape(MLA_HEADS * 64, d)
    (l_ffn, l_ssm_out, l_mla_out, l_ssm_in, l_mla_in, l_qb, l_kvb) = _scatter_partials(
        "scatter_grads",
        [(dg_slab, 1), (g_ssm["out"], 0), (g_mla_out, 0), (g_ssm_in_t, 0), (g_mla_in, 0), (g_qb, 0),
         (g_mla["kvb_t"], 0)])

    unlane = lambda a: a.reshape(SSM_GROUPS, LANES)[:, :SSM_HPG].reshape(1, SSM_HEADS)
    small_g = jnp.concatenate([
        jnp.concatenate([dn00, dn01, dn02, dn10, dn11, dn12], axis=0).reshape(-1),
        g_ssm["conv_w"].reshape(-1), g_ssm["conv_b"].reshape(-1), unlane(g_ssm["dt_bias"]).reshape(-1),
        unlane(g_ssm["a_log"]).reshape(-1), unlane(g_ssm["d_skip"]).reshape(-1), g_ssm["ssm_norm_w"].reshape(-1),
        g_mla["q_a_norm"].reshape(-1), g_mla["kv_a_norm"].reshape(-1), g_mla["q_norm"][0, :QK_DIM],
        g_mla["k_norm"][0, :QK_DIM]])
    n_small = small_g.shape[0]
    n_small_pad = -(-n_small // LANES) * LANES
    small_g = jnp.pad(small_g, (0, n_small_pad - n_small)).reshape(1, 1, n_small_pad)
    gs = _all_gather("gather_small_grads", [(small_g, 0)])[0].reshape(N_DEV, n_small_pad)

    outs = {}

    def put(name, res, shape):
        for key, val in zip(("grad", "delta", "new_m", "new_v"), res):
            outs[(key, name)] = val.reshape(shape)

    ck = 256
    for j, (name, w, m, v) in enumerate((("ffn_w_gate", ffn_w_gate, m_ffn_w_gate, v_ffn_w_gate),
                                         ("ffn_w_up", ffn_w_up, m_ffn_w_up, v_ffn_w_up))):
        res = _adam("adam_" + name, l_ffn, (N_DEV, None, FF_SHARD, ck), lambda a, i, j=j: (0, 3 * a + j, 0, i),
                    w.reshape(4, d, FF_SHARD), m.reshape(4, d, FF_SHARD), v.reshape(4, d, FF_SHARD), True, ck)
        put(name, res, w.shape)
    res = _adam("adam_ffn_w_down", l_ffn, (N_DEV, None, 176, d), lambda a, i: (0, 3 * a + 2, i, 0),
                ffn_w_down.reshape(4, FF_SHARD, d), m_ffn_w_down.reshape(4, FF_SHARD, d),
                v_ffn_w_down.reshape(4, FF_SHARD, d), False, 176)
    put("ffn_w_down", res, ffn_w_down.shape)
    l_ssm_in2 = l_ssm_in.reshape(N_DEV, SSM_IN_SHARD, d)
    res = _adam("adam_ssm_w_in", l_ssm_in2, (N_DEV, SSM_IN_SHARD, 128), lambda a, i: (0, 0, i),
                ssm_w_in, m_ssm_w_in, v_ssm_w_in, True, 128)
    put("ssm_w_in", res, ssm_w_in.shape)
    res = _adam("adam_ssm_w_out", l_ssm_out, (N_DEV, 128, d), lambda a, i: (0, i, 0),
                ssm_w_out, m_ssm_w_out, v_ssm_w_out, False, 128)
    put("ssm_w_out", res, ssm_w_out.shape)
    res = _adam("adam_mla_w_in", l_mla_in, (N_DEV, 128, MLA_IN_DIM), lambda a, i: (0, 0, 0),
                mla_w_in, m_mla_w_in, v_mla_w_in, False, 128)
    put("mla_w_in", res, mla_w_in.shape)
    res = _adam("adam_mla_w_q_b", l_qb, (N_DEV, 192, 128), lambda a, i: (0, 0, i),
                mla_w_q_b, m_mla_w_q_b, v_mla_w_q_b, True, 128)
    put("mla_w_q_b", res, mla_w_q_b.shape)
    res = _adam("adam_mla_w_kv_b", l_kvb, (N_DEV, 256, 128), lambda a, i: (0, 0, i),
                mla_w_kv_b, m_mla_w_kv_b, v_mla_w_kv_b, True, 128)
    put("mla_w_kv_b", res, mla_w_kv_b.shape)
    res = _adam("adam_mla_w_out", l_mla_out, (N_DEV, 128, d), lambda a, i: (0, 0, 0),
                mla_w_out, m_mla_w_out, v_mla_w_out, False, 128)
    put("mla_w_out", res, mla_w_out.shape)

    small_params = (
        ("norm_w", norm_w, m_norm_w, v_norm_w, 6 * d, 6, 128), ("ssm_conv_w", ssm_conv_w, m_ssm_conv_w, v_ssm_conv_w,
                                                               CONV_K * CONV_DIM, CONV_K, 512),
        ("ssm_conv_b", ssm_conv_b, m_ssm_conv_b, v_ssm_conv_b, CONV_DIM, 0, 0),
        ("ssm_dt_bias", ssm_dt_bias, m_ssm_dt_bias, v_ssm_dt_bias, SSM_HEADS, 0, 0),
        ("ssm_a_log", ssm_a_log, m_ssm_a_log, v_ssm_a_log, SSM_HEADS, 0, 0),
        ("ssm_d", ssm_d, m_ssm_d, v_ssm_d, SSM_HEADS, 0, 0),
        ("ssm_norm_w", ssm_norm_w, m_ssm_norm_w, v_ssm_norm_w, D_INNER, 0, 0),
        ("mla_q_a_norm", mla_q_a_norm, m_mla_q_a_norm, v_mla_q_a_norm, Q_LORA, 1, 48),
        ("mla_kv_a_norm", mla_kv_a_norm, m_mla_kv_a_norm, v_mla_kv_a_norm, KV_LORA, 1, 32),
        ("mla_q_norm", mla_q_norm, m_mla_q_norm, v_mla_q_norm, QK_DIM, 0, 0),
        ("mla_k_norm", mla_k_norm, m_mla_k_norm, v_mla_k_norm, QK_DIM, 0, 0),
    )
    parts, ws_, ms_, vs_, off = [], [], [], [], 0
    for name, w, m, v, full, rows, shard in small_params:
        seg = gs[:, off:off + full]
        if rows:
            seg = _dyn(seg.reshape(N_DEV, rows, full // rows), me * shard, shard).reshape(N_DEV, rows * shard)
        parts.append(seg)
        ws_.append(w.reshape(1, -1))
        ms_.append(m.reshape(1, -1))
        vs_.append(v.reshape(1, -1))
        off += full
    n_loc = sum(p.shape[1] for p in parts)
    n_loc_pad = -(-n_loc // LANES) * LANES
    padc = lambda a, val=0.0: jnp.pad(jnp.concatenate(a, axis=1), ((0, 0), (0, n_loc_pad - n_loc)),
                                      constant_values=val)
    res = _adam("adam_small", padc(parts).reshape(N_DEV, 1, n_loc_pad), (N_DEV, 1, n_loc_pad), lambda a, i: (0, 0, 0),
                padc(ws_).reshape(1, 1, n_loc_pad), padc(ms_).reshape(1, 1, n_loc_pad),
                padc(vs_, 1.0).reshape(1, 1, n_loc_pad), False, 1)
    off = 0
    for name, w, m, v, full, rows, shard in small_params:
        nloc = w.size
        put(name, [r.reshape(-1)[off:off + nloc] for r in res], w.shape)
        off += nloc

    order = ("norm_w", "ffn_w_gate", "ffn_w_up", "ffn_w_down", "ssm_w_in", "ssm_conv_w", "ssm_conv_b", "ssm_dt_bias",
             "ssm_a_log", "ssm_d", "ssm_norm_w", "ssm_w_out", "mla_w_in", "mla_q_a_norm", "mla_kv_a_norm",
             "mla_w_q_b", "mla_w_kv_b", "mla_q_norm", "mla_k_norm", "mla_w_out")
    return (loss, grad_x, *[outs[(k, n)] for k in ("grad", "delta", "new_m", "new_v") for n in order])
```

```python
import functools
import math

import jax
import jax.numpy as jnp
import numpy as np
from jax import lax
from jax.experimental import pallas as pl
from jax.experimental.pallas import tpu as pltpu

F32 = jnp.float32
BF = jnp.bfloat16
SDS = jax.ShapeDtypeStruct

N_DEV = 8
D_MODEL = 1024
D_FF = 2816
FF_SHARD = D_FF // N_DEV
D_INNER = 2048
SSM_HEADS = 32
SSM_GROUPS = 8
SSM_HPG = 4
SSM_STATE = 128
CONV_K = 4
CONV_DIM = 4096
SSM_IN_DIM = 6176
SSM_IN_SHARD = SSM_IN_DIM // N_DEV
NORM_GROUP = 256
CHUNK = 128
MLA_HEADS = 16
Q_LORA = 384
KV_LORA = 256
QK_NOPE = 64
QK_ROPE = 32
QK_DIM = 96
MLA_IN_DIM = 672
HEAD_PAD = 128
ROPE_THETA = 10000.0
EPS = 1e-6
LANES = 128

ADAM_LR = 0.001
ADAM_B1 = 0.9
ADAM_B2 = 0.999
ADAM_EPS = 1e-08
ADAM_WD = 0.01
ADAM_STEP = 10

VMEM_BIG = 56 * 1024 * 1024

NN = ((1,), (0,))
NT = ((1,), (1,))
TN = ((0,), (0,))


def _dotf(a, b, dn):
    return lax.dot_general(a.astype(BF), b.astype(BF), (dn, ((), ())), preferred_element_type=F32)


def _dot_hi(a, b, dn=NN):
    return lax.dot_general(a, b, (dn, ((), ())), precision=lax.Precision.HIGHEST, preferred_element_type=F32)


def _sigmoid(x):
    return 1.0 / (1.0 + jnp.exp(-x))


def _silu(x):
    return x * _sigmoid(x)


def _softplus(x):
    return jnp.maximum(x, 0.0) + jnp.log(1.0 + jnp.exp(-jnp.abs(x)))


def _cparams(n_grid, vmem=None):
    return pltpu.CompilerParams(dimension_semantics=("arbitrary",) * n_grid, vmem_limit_bytes=vmem)


def _fmm(name, grid_mn, pairs, outs, *, epi=None, extras=(), n_acc=1, acc_shape=None, vmem=None, alias=None,
         joint=False, comm=None):
    if joint:
        nk_total = pairs[0][7]
        assert all(p[7] == nk_total for p in pairs)
        starts = [0] * len(pairs)
    else:
        nk_total = sum(p[7] for p in pairs)
        starts = []
        s = 0
        for p in pairs:
            starts.append(s)
            s += p[7]
    n_pairs, n_extras, n_outs = len(pairs), len(extras), len(outs)
    single = nk_total == 1

    n_ci = len(comm.arrays) if comm is not None else 0
    n_co = len(comm.out_shapes) if comm is not None else 0
    n_scratch_acc = 0 if single else n_acc

    def body(*refs):
        ab_refs = refs[: 2 * n_pairs]
        e_refs = refs[2 * n_pairs: 2 * n_pairs + n_extras]
        pos = 2 * n_pairs + n_extras + (1 if alias is not None else 0)
        ci_refs = refs[pos: pos + n_ci]
        pos += n_ci
        o_refs = refs[pos: pos + n_outs]
        co_refs = refs[pos + n_outs: pos + n_outs + n_co]
        pos += n_outs + n_co
        acc_refs = refs[pos: pos + n_scratch_acc]
        sem_refs = refs[pos + n_scratch_acc:]
        i, j, k = pl.program_id(0), pl.program_id(1), pl.program_id(2)

        if comm is not None:
            @pl.when((i == 0) & (j == 0) & (k == 0))
            def _():
                comm.start(ci_refs, co_refs, sem_refs)

        compute(ab_refs, e_refs, o_refs, acc_refs, i, j, k)

        if comm is not None:
            @pl.when((i == grid_mn[0] - 1) & (j == grid_mn[1] - 1) & (k == nk_total - 1))
            def _():
                comm.wait(ci_refs, co_refs, sem_refs)

    def compute(ab_refs, e_refs, o_refs, acc_refs, i, j, k):

        def finish(accs):
            res = epi(accs, *[e[...] for e in e_refs]) if epi is not None else accs
            if not isinstance(res, (tuple, list)):
                res = (res,)
            first = (i == 0) & (j == 0)
            for o, r, spec in zip(o_refs, res, outs):
                if spec[3]:
                    @pl.when(first)
                    def _(o=o, r=r):
                        o[...] = r.astype(o.dtype)

                    @pl.when(jnp.logical_not(first))
                    def _(o=o, r=r):
                        o[...] += r.astype(o.dtype)
                else:
                    o[...] = r.astype(o.dtype)

        if single:
            accs = [None] * n_acc
            for p, pr in enumerate(pairs):
                d = _dotf(ab_refs[2 * p][...], ab_refs[2 * p + 1][...], pr[6])
                accs[pr[8]] = d if accs[pr[8]] is None else accs[pr[8]] + d
            finish(accs)
            return

        @pl.when(k == 0)
        def _():
            for a in acc_refs:
                a[...] = jnp.zeros_like(a)

        for p, pr in enumerate(pairs):
            def step(p=p, pr=pr):
                acc_refs[pr[8]][...] += _dotf(ab_refs[2 * p][...], ab_refs[2 * p + 1][...], pr[6])

            if n_pairs == 1 or joint:
                step()
            else:
                pl.when((k >= starts[p]) & (k < starts[p] + pr[7]))(step)

        @pl.when(k == nk_total - 1)
        def _():
            finish([a[...] for a in acc_refs])

    in_specs, args = [], []
    for p, pr in enumerate(pairs):
        a, a_blk, a_idx, b, b_blk, b_idx, _, nk, _ = pr
        st = starts[p]

        def amap(i, j, k, a_idx=a_idx, st=st, nk=nk):
            return a_idx(i, j, jnp.clip(k - st, 0, nk - 1))

        def bmap(i, j, k, b_idx=b_idx, st=st, nk=nk):
            return b_idx(i, j, jnp.clip(k - st, 0, nk - 1))

        in_specs += [pl.BlockSpec(a_blk, amap), pl.BlockSpec(b_blk, bmap)]
        args += [a, b]
    for arr, blk, idx in extras:
        in_specs.append(pl.BlockSpec(blk, lambda i, j, k, idx=idx: idx(i, j)))
        args.append(arr)
    io_alias = {}
    if alias is not None:
        in_specs.append(pl.BlockSpec(memory_space=pl.ANY))
        io_alias = {len(args): 0}
        args.append(alias)
    out_specs = [pl.BlockSpec(blk, lambda i, j, k, idx=idx: idx(i, j)) for _, blk, idx, _ in outs]
    out_shape = [o[0] for o in outs]
    scratch = [] if single else [pltpu.VMEM(acc_shape, F32) for _ in range(n_acc)]
    if comm is not None:
        hbm = pl.BlockSpec(memory_space=pl.ANY)
        in_specs += [hbm] * n_ci
        args += list(comm.arrays)
        out_specs += [hbm] * n_co
        out_shape += list(comm.out_shapes)
        scratch += comm.sems
    res = pl.pallas_call(
        body, name=name, grid=(grid_mn[0], grid_mn[1], nk_total), in_specs=in_specs, out_specs=out_specs,
        out_shape=out_shape, scratch_shapes=scratch, input_output_aliases=io_alias,
        compiler_params=_cparams(3, vmem),
    )(*args)
    return res


def _ew(name, grid, fn, ins, outs, *, acc_axes=(), vmem=None):
    n_in = len(ins)

    def body(*refs):
        res = fn(*[r[...] for r in refs[:n_in]])
        if not isinstance(res, (tuple, list)):
            res = (res,)
        first = None
        for ax in acc_axes:
            c = pl.program_id(ax) == 0
            first = c if first is None else (first & c)
        for o, r, spec in zip(refs[n_in:], res, outs):
            if spec[3]:
                @pl.when(first)
                def _(o=o, r=r):
                    o[...] = r.astype(o.dtype)

                @pl.when(jnp.logical_not(first))
                def _(o=o, r=r):
                    o[...] += r.astype(o.dtype)
            else:
                o[...] = r.astype(o.dtype)

    return pl.pallas_call(
        body, name=name, grid=grid,
        in_specs=[pl.BlockSpec(blk, idx) for _, blk, idx in ins],
        out_specs=[pl.BlockSpec(blk, idx) for _, blk, idx, _ in outs],
        out_shape=[o[0] for o in outs],
        compiler_params=_cparams(len(grid), vmem),
    )(*[a for a, _, _ in ins])


def _row_tile(t, want):
    tm = min(want, t)
    assert t % tm == 0, (t, tm)
    return tm


def _rms_fn(x, w):
    return x * lax.rsqrt(jnp.mean(x * x, axis=-1, keepdims=True) + EPS) * w


def _rms_fwd(name, x, w):
    t, d = x.shape
    tm = _row_tile(t, 512)
    return _ew(name, (t // tm,), _rms_fn,
               [(x, (tm, d), lambda i: (i, 0)), (w, (1, d), lambda i: (0, 0))],
               [(SDS((t, d), BF), (tm, d), lambda i: (i, 0), False)])[0]


def _rms_bwd(name, x, w, dh, dres):
    t, d = x.shape
    tm = _row_tile(t, 512)

    def fn(x, w, dh, dres):
        _, vjp = jax.vjp(_rms_fn, x, w)
        dx, dw = vjp(dh.astype(F32))
        return dx + dres, dw

    row = lambda i: (i, 0)
    return _ew(name, (t // tm,), fn,
               [(x, (tm, d), row), (w, (1, d), lambda i: (0, 0)), (dh, (tm, d), row), (dres, (tm, d), row)],
               [(SDS((t, d), F32), (tm, d), row, False), (SDS((1, d), F32), (1, d), lambda i: (0, 0), True)],
               acc_axes=(0,))


def _loss_and_grad(y, target):
    t, d = y.shape
    tm = _row_tile(t, 512)

    def fn(y, tg):
        e = y - tg
        return e * (1.0 / d), jnp.sum(e * e, axis=0, keepdims=True) * (0.5 / d)

    row = lambda i: (i, 0)
    return _ew("loss_head", (t // tm,), fn, [(y, (tm, d), row), (target, (tm, d), row)],
               [(SDS((t, d), F32), (tm, d), row, False), (SDS((1, d), F32), (1, d), lambda i: (0, 0), True)],
               acc_axes=(0,))


def _ffn_fwd(tag, x, nw, wf, comm=None):
    t, d = x.shape
    h = _rms_fwd(tag + "_rms", x, nw)
    tm, tn = _row_tile(t, 1024), 256

    def epi(accs):
        g, u = accs
        return g, u, _silu(g) * u

    hblk = (h, (tm, d), lambda i, j, k: (i, 0))
    col = lambda i, j: (i, j)
    g, u, a, *got = _fmm(
        tag + "_up", (t // tm, D_FF // tn),
        [hblk + (wf, (None, tn, d), lambda i, j, k: (0, j, 0), NT, 1, 0),
         hblk + (wf, (None, tn, d), lambda i, j, k: (1, j, 0), NT, 1, 1)],
        [(SDS((t, D_FF), BF), (tm, tn), col, False)] * 3, epi=epi, n_acc=2, joint=True, comm=comm)
    tm2 = _row_tile(t, 512)
    y = _fmm(
        tag + "_down", (t // tm2, 1),
        [(a, (tm2, D_FF), lambda i, j, k: (i, 0), wf, (None, D_FF, d), lambda i, j, k: (2, 0, 0), NN, 1, 0)],
        [(SDS((t, d), F32), (tm2, d), lambda i, j: (i, 0), False)],
        extras=[(x, (tm2, d), lambda i, j: (i, 0))],
        epi=lambda accs, xr: xr + 0.5 * accs[0], vmem=VMEM_BIG)[0]
    return y, (x, h, g, u, a), got


def _wgrad(name, a, b, m, n, *, tm, tn, tk=512, scale=None, out=None, out_idx=None, out_dtype=BF):
    t = a.shape[0]
    tk = _row_tile(t, tk)
    epi = (lambda accs: accs[0] * scale) if scale is not None else None
    if out is None:
        spec = (SDS((m, n), out_dtype), (tm, tn), lambda i, j: (i, j), False)
    else:
        spec = (SDS(out.shape, out.dtype), (None, tm, tn), lambda i, j: (out_idx, i, j), False)
    return _fmm(name, (m // tm, n // tn),
                [(a, (tk, tm), lambda i, j, k: (k, i), b, (tk, tn), lambda i, j, k: (k, j), TN, t // tk, 0)],
                [spec], epi=epi, acc_shape=(tm, tn), alias=out, vmem=VMEM_BIG)[0]


def _ffn_bwd(tag, dy, nw, wf, saved, comm_act=None, comm_dh=None):
    x, h, g, u, a = saved
    t, d = x.shape
    tm, tn = _row_tile(t, 1024), 256

    def epi(accs, g, u):
        da = 0.5 * accs[0]
        g = g.astype(F32)
        u = u.astype(F32)
        s = _sigmoid(g)
        return da * u * (s * (1.0 + g * (1.0 - s))), da * (g * s)

    col = lambda i, j: (i, j)
    dg, du, *got_act = _fmm(
        tag + "_bwd_act", (t // tm, D_FF // tn),
        [(dy, (tm, d), lambda i, j, k: (i, 0), wf, (None, tn, d), lambda i, j, k: (2, j, 0), NT, 1, 0)],
        [(SDS((t, D_FF), BF), (tm, tn), col, False)] * 2,
        extras=[(g, (tm, tn), col), (u, (tm, tn), col)], epi=epi, comm=comm_act)
    tm2 = _row_tile(t, 256)
    dh, *got_dh = _fmm(
        tag + "_bwd_dh", (t // tm2, 1),
        [(dg, (tm2, D_FF), lambda i, j, k: (i, 0), wf, (None, D_FF, d), lambda i, j, k: (0, 0, 0), NN, 1, 0),
         (du, (tm2, D_FF), lambda i, j, k: (i, 0), wf, (None, D_FF, d), lambda i, j, k: (1, 0, 0), NN, 1, 0)],
        [(SDS((t, d), BF), (tm2, d), lambda i, j: (i, 0), False)], vmem=VMEM_BIG, joint=True, comm=comm_dh)
    dx, dnw = _rms_bwd(tag + "_bwd_rms", x, nw, dh, dy)
    half = D_FF // 2
    slab = jnp.zeros((3, D_FF, d), BF)
    slab = _wgrad(tag + "_wg", dg, h, D_FF, d, tm=half, tn=d, out=slab, out_idx=0)
    slab = _wgrad(tag + "_wu", du, h, D_FF, d, tm=half, tn=d, out=slab, out_idx=1)
    slab = _wgrad(tag + "_wd", a, dy, D_FF, d, tm=half, tn=d, scale=0.5, out=slab, out_idx=2)
    return dx, dnw, slab, got_act, got_dh


def _shift_down(cur, prev8, j):
    rolled = pltpu.roll(cur, j, 0)
    sub = lax.broadcasted_iota(jnp.int32, prev8.shape, 0)
    top = jnp.where(sub < j, pltpu.roll(prev8, j, 0), rolled[:8])
    return jnp.concatenate([top, rolled[8:]], axis=0)


def _shift_up(cur, next8, j):
    n = cur.shape[0]
    rolled = pltpu.roll(cur, n - j, 0)
    sub = lax.broadcasted_iota(jnp.int32, next8.shape, 0)
    bot = jnp.where(sub >= 8 - j, pltpu.roll(next8, 8 - j, 0), rolled[n - 8:])
    return jnp.concatenate([rolled[: n - 8], bot], axis=0)


HALO = 16


def _conv_fwd(xbc, w, b, seq):
    t, c = xbc.shape
    ts, tc = _row_tile(seq, 256), 512
    tiles_per_seq = seq // ts
    hb = ts // HALO

    def fn(cur, prev, w, b):
        i = pl.program_id(1)
        cur = cur.astype(F32)
        prev8 = jnp.where(i % tiles_per_seq == 0, 0.0, prev.astype(F32)[HALO - 8:])
        out = b + w[3:4] * cur
        for j in range(1, CONV_K):
            out = out + w[3 - j:4 - j] * _shift_down(cur, prev8, j)
        return out, _silu(out)

    return _ew("ssm_conv_fwd", (c // tc, t // ts), fn,
               [(xbc, (ts, tc), lambda j, i: (i, j)),
                (xbc, (HALO, tc), lambda j, i: (jnp.maximum(i * hb - 1, 0), j)),
                (w, (CONV_K, tc), lambda j, i: (0, j)), (b, (1, tc), lambda j, i: (0, j))],
               [(SDS((t, c), BF), (ts, tc), lambda j, i: (i, j), False)] * 2)


def _conv_bwd(tag, dxa, cpre, xbc, w, col0, seq):
    t, width = dxa.shape
    ts, tc = _row_tile(seq, 256), 512
    tiles_per_seq = seq // ts
    hb = ts // HALO
    cb0 = col0 // tc
    n_halo_blocks = t // HALO

    def dsilu(cv, dv):
        cv = cv.astype(F32)
        s = _sigmoid(cv)
        return dv.astype(F32) * (s * (1.0 + cv * (1.0 - s)))

    def fn(dxa_c, dxa_n, c_c, c_n, x_c, x_p, w):
        i = pl.program_id(1)
        dc = dsilu(c_c, dxa_c)
        last = i % tiles_per_seq == tiles_per_seq - 1
        dc_n = jnp.where(last, 0.0, dsilu(c_n, dxa_n)[:8])
        dx = w[3:4] * dc
        for j in range(1, CONV_K):
            dx = dx + w[3 - j:4 - j] * _shift_up(dc, dc_n, j)
        cur = x_c.astype(F32)
        prev8 = jnp.where(i % tiles_per_seq == 0, 0.0, x_p.astype(F32)[HALO - 8:])
        rows = [jnp.sum(dc * cur, axis=0, keepdims=True)]
        for j in range(1, CONV_K):
            rows.append(jnp.sum(dc * _shift_down(cur, prev8, j), axis=0, keepdims=True))
        sub8 = lax.broadcasted_iota(jnp.int32, (8, dc.shape[1]), 0)
        dw = jnp.zeros((8, dc.shape[1]), F32)
        for kk in range(CONV_K):
            dw = jnp.where(sub8 == kk, rows[CONV_K - 1 - kk], dw)
        return dx, dw, jnp.sum(dc, axis=0, keepdims=True)

    nxt = lambda j, i: (jnp.minimum((i + 1) * hb, n_halo_blocks - 1), j)
    nxt_off = lambda j, i: (jnp.minimum((i + 1) * hb, n_halo_blocks - 1), j + cb0)
    return _ew(tag, (width // tc, t // ts), fn,
               [(dxa, (ts, tc), lambda j, i: (i, j)), (dxa, (HALO, tc), nxt),
                (cpre, (ts, tc), lambda j, i: (i, j + cb0)), (cpre, (HALO, tc), nxt_off),
                (xbc, (ts, tc), lambda j, i: (i, j + cb0)),
                (xbc, (HALO, tc), lambda j, i: (jnp.maximum(i * hb - 1, 0), j + cb0)),
                (w, (CONV_K, tc), lambda j, i: (0, j + cb0))],
               [(SDS((t, width), BF), (ts, tc), lambda j, i: (i, j), False),
                (SDS((8, width), F32), (8, tc), lambda j, i: (0, j), True),
                (SDS((1, width), F32), (1, tc), lambda j, i: (0, j), True)],
               acc_axes=(1,))


def _ssd_chunk(xs, bm, cm, dtr, st, dtb, alog, dsk):
    ell = xs.shape[0]
    xs = xs.astype(F32)
    lane = lax.broadcasted_iota(jnp.int32, (ell, LANES), 1)
    sub = lax.broadcasted_iota(jnp.int32, (ell, LANES), 0)
    lane1 = lax.broadcasted_iota(jnp.int32, (1, LANES), 1)
    causal = sub >= lane
    dt = _softplus(dtr + dtb)
    da = dt * (-jnp.exp(alog))
    acs = _dot_hi(causal.astype(F32), da)
    acs_t = acs.T
    cb = _dotf(cm, bm, NT)
    lo = lane < 64
    ys, news = [], []
    for p in range(2):
        xp = xs[:, LANES * p:LANES * (p + 1)]
        sp = st[LANES * p:LANES * (p + 1), :]
        col, dtc, last, dsel = [], [], [], []
        y_diag = None
        for q in range(2):
            r = 2 * p + q
            col_r = jnp.sum(jnp.where(lane == r, acs, 0.0), axis=1, keepdims=True)
            row_r = jnp.sum(jnp.where(sub == r, acs_t, 0.0), axis=0, keepdims=True)
            dtc_r = jnp.sum(jnp.where(lane == r, dt, 0.0), axis=1, keepdims=True)
            decay = jnp.exp(jnp.where(causal, col_r - row_r, -jnp.inf))
            head = lo if q == 0 else jnp.logical_not(lo)
            d = _dotf(cb * decay, jnp.where(head, xp * dtc_r, 0.0), NN)
            y_diag = d if y_diag is None else y_diag + d
            col.append(col_r)
            dtc.append(dtc_r)
            last.append(jnp.sum(jnp.where(sub[:, :1] == ell - 1, col_r, 0.0), axis=0, keepdims=True))
            dsel.append(jnp.sum(jnp.where(lane1 == r, dsk, 0.0), axis=1, keepdims=True))
        y_off = _dotf(cm, sp, NT) * jnp.where(lo, jnp.exp(col[0]), jnp.exp(col[1]))
        xw = jnp.where(lo, xp * (dtc[0] * jnp.exp(last[0] - col[0])), xp * (dtc[1] * jnp.exp(last[1] - col[1])))
        new = sp * jnp.where(sub < 64, jnp.exp(last[0]), jnp.exp(last[1])) + _dotf(xw, bm, TN)
        ys.append(y_diag + y_off + jnp.where(lo, dsel[0], dsel[1]) * xp)
        news.append(new)
    return jnp.concatenate(ys, axis=1), jnp.concatenate(news, axis=0)


def _ssd_specs(nb, nc):
    row = lambda g, b, c: (b * nc + c, g)
    return row, [
        ((CHUNK, 2 * LANES), row),
        ((CHUNK, LANES), lambda g, b, c: (b * nc + c, 16 + g)),
        ((CHUNK, LANES), lambda g, b, c: (b * nc + c, 24 + g)),
    ]


def _ssd_fwd(xa, dtr, dtb, alog, dsk, nb, seq):
    t = xa.shape[0]
    nc = seq // CHUNK
    row, xspecs = _ssd_specs(nb, nc)
    par = ((1, LANES), lambda g, b, c: (0, g))

    def body(xs, bm, cm, dtr, dtb, alog, dsk, y_ref, st_out, st_ref):
        @pl.when(pl.program_id(2) == 0)
        def _():
            st_ref[...] = jnp.zeros_like(st_ref)

        st = st_ref[...]
        st_out[...] = st
        y, new = _ssd_chunk(xs[...], bm[...], cm[...], dtr[...], st, dtb[...], alog[...], dsk[...])
        y_ref[...] = y
        st_ref[...] = new

    specs = [pl.BlockSpec(b, i) for b, i in xspecs] + [pl.BlockSpec((CHUNK, LANES), row)] + [pl.BlockSpec(*par)] * 3
    return pl.pallas_call(
        body, name="ssd_fwd", grid=(SSM_GROUPS, nb, nc), in_specs=specs,
        out_specs=[pl.BlockSpec((CHUNK, 2 * LANES), row),
                   pl.BlockSpec((None, None, None, 2 * LANES, LANES), lambda g, b, c: (g, b, c, 0, 0))],
        out_shape=[SDS((t, D_INNER), F32), SDS((SSM_GROUPS, nb, nc, 2 * LANES, LANES), F32)],
        scratch_shapes=[pltpu.VMEM((2 * LANES, LANES), F32)],
        compiler_params=_cparams(3),
    )(xa, xa, xa, dtr, dtb, alog, dsk)


def _ssd_bwd(xa, dtr, dtb, alog, dsk, states, dy, nb, seq):
    t = xa.shape[0]
    nc = seq // CHUNK
    rev = lambda c: nc - 1 - c
    row = lambda g, b, c: (b * nc + rev(c), g)
    par = ((1, LANES), lambda g, b, c: (0, g))

    def body(xs, bm, cm, dtr, dtb, alog, dsk, st_in, dy, dxs, dbm, dcm, ddtr, ddtb, dalog, ddsk, dst_ref):
        @pl.when(pl.program_id(2) == 0)
        def _():
            dst_ref[...] = jnp.zeros_like(dst_ref)

        _, vjp = jax.vjp(_ssd_chunk, xs[...], bm[...], cm[...], dtr[...], st_in[...], dtb[...], alog[...], dsk[...])
        g = vjp((dy[...], dst_ref[...]))
        dxs[...] = g[0]
        dbm[...] = g[1]
        dcm[...] = g[2]
        ddtr[...] = g[3]
        dst_ref[...] = g[4]
        first = (pl.program_id(1) == 0) & (pl.program_id(2) == 0)
        for o, v in ((ddtb, g[5]), (dalog, g[6]), (ddsk, g[7])):
            @pl.when(first)
            def _(o=o, v=v):
                o[...] = v

            @pl.when(jnp.logical_not(first))
            def _(o=o, v=v):
                o[...] += v

    in_specs = [
        pl.BlockSpec((CHUNK, 2 * LANES), row),
        pl.BlockSpec((CHUNK, LANES), lambda g, b, c: (b * nc + rev(c), 16 + g)),
        pl.BlockSpec((CHUNK, LANES), lambda g, b, c: (b * nc + rev(c), 24 + g)),
        pl.BlockSpec((CHUNK, LANES), row),
        pl.BlockSpec(*par), pl.BlockSpec(*par), pl.BlockSpec(*par),
        pl.BlockSpec((None, None, None, 2 * LANES, LANES), lambda g, b, c: (g, b, rev(c), 0, 0)),
        pl.BlockSpec((CHUNK, 2 * LANES), row),
    ]
    out_specs = [pl.BlockSpec((CHUNK, 2 * LANES), row), pl.BlockSpec((CHUNK, LANES), row),
                 pl.BlockSpec((CHUNK, LANES), row), pl.BlockSpec((CHUNK, LANES), row),
                 pl.BlockSpec(*par), pl.BlockSpec(*par), pl.BlockSpec(*par)]
    out_shape = [SDS((t, D_INNER), BF), SDS((t, SSM_GROUPS * LANES), BF), SDS((t, SSM_GROUPS * LANES), BF),
                 SDS((t, SSM_GROUPS * LANES), F32)] + [SDS((1, SSM_GROUPS * LANES), F32)] * 3
    return pl.pallas_call(
        body, name="ssd_bwd", grid=(SSM_GROUPS, nb, nc), in_specs=in_specs, out_specs=out_specs,
        out_shape=out_shape, scratch_shapes=[pltpu.VMEM((2 * LANES, LANES), F32)],
        compiler_params=_cparams(3, VMEM_BIG),
    )(xa, xa, xa, dtr, dtb, alog, dsk, states, dy)


def _gated_fn(y, z, w):
    g = y * _silu(z.astype(F32))
    return g * lax.rsqrt(jnp.mean(g * g, axis=-1, keepdims=True) + EPS) * w


def _gated_norm_fwd(y, z, w):
    t = y.shape[0]
    tm = _row_tile(t, 512)
    blk = ((tm, NORM_GROUP), lambda g, i: (i, g))
    return _ew("ssm_gnorm_fwd", (SSM_GROUPS, t // tm), _gated_fn,
               [(y,) + blk, (z,) + blk, (w, (1, NORM_GROUP), lambda g, i: (0, g))],
               [(SDS((t, D_INNER), BF),) + blk + (False,)])[0]


def _gated_norm_bwd(y, z, w, dout):
    t = y.shape[0]
    tm = _row_tile(t, 512)
    blk = ((tm, NORM_GROUP), lambda g, i: (i, g))
    par = ((1, NORM_GROUP), lambda g, i: (0, g))

    def fn(y, z, w, dout):
        _, vjp = jax.vjp(_gated_fn, y, z, w)
        return vjp(dout.astype(F32))

    return _ew("ssm_gnorm_bwd", (SSM_GROUPS, t // tm), fn,
               [(y,) + blk, (z,) + blk, (w,) + par, (dout,) + blk],
               [(SDS((t, D_INNER), F32),) + blk + (False,), (SDS((t, D_INNER), BF),) + blk + (False,),
                (SDS((1, D_INNER), F32),) + par + (True,)],
               acc_axes=(1,))


def _proj_nt(name, h, wt, n, out_dtype, tn=256, comm=None):
    t, kdim = h.shape
    tm = _row_tile(t, 1024)
    res = _fmm(name, (t // tm, n // tn),
               [(h, (tm, kdim), lambda i, j, k: (i, 0), wt, (tn, kdim), lambda i, j, k: (j, 0), NT, 1, 0)],
               [(SDS((t, n), out_dtype), (tm, tn), lambda i, j: (i, j), False)], comm=comm)
    return res[0] if comm is None else res


def _seg_nn(name, parts, n, out_dtype, tk=256, comm=None):
    t = parts[0][0].shape[0]
    tm = _row_tile(t, 512)
    pairs = []
    for a, w, row0 in parts:
        kp = a.shape[1]
        tkp = min(tk, kp)
        r0 = row0 // tkp
        pairs.append((a, (tm, tkp), lambda i, j, k: (i, k), w, (tkp, n), lambda i, j, k, r0=r0: (k + r0, 0),
                      NN, kp // tkp, 0))
    res = _fmm(name, (t // tm, 1), pairs, [(SDS((t, n), out_dtype), (tm, n), lambda i, j: (i, 0), False)],
               acc_shape=(tm, n), comm=comm)
    return res[0] if comm is None else res


def _ssm_fwd(x, nw, ws, small, nb, seq, comm=None):
    t, d = x.shape
    h = _rms_fwd("ssm_rms", x, nw)
    z = _proj_nt("ssm_in_z", h, ws["z_t"], D_INNER, BF)
    if comm is None:
        xbc, got = _proj_nt("ssm_in_xbc", h, ws["xbc_t"], CONV_DIM, BF), []
    else:
        xbc, *got = _proj_nt("ssm_in_xbc", h, ws["xbc_t"], CONV_DIM, BF, comm=comm)
    dtr = _proj_nt("ssm_in_dt", h, ws["dt_t"], SSM_GROUPS * LANES, F32)
    cpre, xa = _conv_fwd(xbc, small["conv_w"], small["conv_b"], seq)
    y, states = _ssd_fwd(xa, dtr, small["dt_bias"], small["a_log"], small["d_skip"], nb, seq)
    gn = _gated_norm_fwd(y, z, small["ssm_norm_w"])
    tm = _row_tile(t, 512)
    out = _fmm("ssm_out", (t // tm, 1),
               [(gn, (tm, D_INNER), lambda i, j, k: (i, 0), ws["out"], (D_INNER, d), lambda i, j, k: (0, 0), NN, 1, 0)],
               [(SDS((t, d), F32), (tm, d), lambda i, j: (i, 0), False)],
               extras=[(x, (tm, d), lambda i, j: (i, 0))], epi=lambda accs, xr: xr + accs[0])[0]
    return out, (x, h, z, xbc, dtr, cpre, xa, y, states, gn), got


def _ssm_bwd(dy, nw, ws, small, saved, nb, seq, comm=None):
    x, h, z, xbc, dtr, cpre, xa, y, states, gn = saved
    t, d = x.shape
    dgn = _proj_nt("ssm_bwd_dgn", dy, ws["out"], D_INNER, BF)
    d_out = _wgrad("ssm_w_out_g", gn, dy, D_INNER, d, tm=D_INNER // 2, tn=d)
    dyssd, dz, d_normw = _gated_norm_bwd(y, z, small["ssm_norm_w"], dgn)
    dxs, dbm, dcm, ddtr, d_dtb, d_alog, d_dsk = _ssd_bwd(
        xa, dtr, small["dt_bias"], small["a_log"], small["d_skip"], states, dyssd, nb, seq)
    dxbc_x, dcw_x, dcb_x = _conv_bwd("ssm_conv_bwd_x", dxs, cpre, xbc, small["conv_w"], 0, seq)
    dxbc_b, dcw_b, dcb_b = _conv_bwd("ssm_conv_bwd_b", dbm, cpre, xbc, small["conv_w"], D_INNER, seq)
    dxbc_c, dcw_c, dcb_c = _conv_bwd("ssm_conv_bwd_c", dcm, cpre, xbc, small["conv_w"], D_INNER + 1024, seq)
    parts = [(dz, ws["z_t"], 0), (dxbc_x, ws["xbc_t"], 0), (dxbc_b, ws["xbc_t"], D_INNER),
             (dxbc_c, ws["xbc_t"], D_INNER + 1024), (ddtr, ws["dt_t"], 0)]
    if comm is None:
        dh, got = _seg_nn("ssm_bwd_dh", parts, d, BF, tk=1024), []
    else:
        dh, *got = _seg_nn("ssm_bwd_dh", parts, d, BF, tk=1024, comm=comm)
    dx, dnw = _rms_bwd("ssm_bwd_rms", x, nw, dh, dy)
    g = {
        "z_t": _wgrad("ssm_w_z_g", dz, h, D_INNER, d, tm=1024, tn=d, out_dtype=F32),
        "x_t": _wgrad("ssm_w_x_g", dxbc_x, h, D_INNER, d, tm=1024, tn=d, out_dtype=F32),
        "b_t": _wgrad("ssm_w_b_g", dxbc_b, h, 1024, d, tm=1024, tn=d, out_dtype=F32),
        "c_t": _wgrad("ssm_w_c_g", dxbc_c, h, 1024, d, tm=1024, tn=d, out_dtype=F32),
        "dt_t": _wgrad("ssm_w_dt_g", ddtr, h, 1024, d, tm=1024, tn=d, out_dtype=F32),
        "out": d_out,
        "conv_w": jnp.concatenate([dcw_x[:CONV_K], dcw_b[:CONV_K], dcw_c[:CONV_K]], axis=1),
        "conv_b": jnp.concatenate([dcb_x, dcb_b, dcb_c], axis=1),
        "dt_bias": d_dtb, "a_log": d_alog, "d_skip": d_dsk, "ssm_norm_w": d_normw,
    }
    return dx, dnw, g, got


def _lane_masks(shape):
    lane = lax.broadcasted_iota(jnp.int32, shape, len(shape) - 1)
    return lane < QK_NOPE, (lane >= QK_NOPE) & (lane < QK_DIM)


def _swap_matrix():
    p = np.zeros((HEAD_PAD, HEAD_PAD), np.float32)
    for i in range(QK_ROPE // 2):
        p[QK_NOPE + QK_ROPE // 2 + i, QK_NOPE + i] = 1.0
        p[QK_NOPE + i, QK_NOPE + QK_ROPE // 2 + i] = 1.0
    return jnp.asarray(p)


def _rope_tables(positions_col):
    t = positions_col.shape[0]
    tm = _row_tile(t, 512)
    freq = np.zeros((1, HEAD_PAD), np.float32)
    inv = 1.0 / (ROPE_THETA ** (np.arange(0, QK_ROPE, 2, dtype=np.float32) / QK_ROPE))
    freq[0, QK_NOPE:QK_NOPE + QK_ROPE // 2] = inv
    freq[0, QK_NOPE + QK_ROPE // 2:QK_DIM] = inv
    sign = np.zeros((1, HEAD_PAD), np.float32)
    sign[0, QK_NOPE:QK_NOPE + QK_ROPE // 2] = -1.0
    sign[0, QK_NOPE + QK_ROPE // 2:QK_DIM] = 1.0

    def fn(pos, freq, sign):
        ang = pos.astype(F32) * freq
        nope, rope = _lane_masks(ang.shape)
        return jnp.where(nope, 1.0, jnp.where(rope, jnp.cos(ang), 0.0)), jnp.sin(ang) * sign

    row = lambda i: (i, 0)
    par = ((1, HEAD_PAD), lambda i: (0, 0))
    return _ew("mla_rope_tables", (t // tm,), fn,
               [(positions_col, (tm, 1), row), (jnp.asarray(freq),) + par, (jnp.asarray(sign),) + par],
               [(SDS((t, HEAD_PAD), F32), (tm, HEAD_PAD), row, False)] * 2)


def _rope(xn, cos, sin_signed, swap):
    return xn * cos + _dot_hi(xn, swap) * sin_signed


def _krope_fn(kr, w, cos, sin_signed, swap):
    _, rope = _lane_masks(kr.shape)
    ss = jnp.sum(jnp.where(rope, kr * kr, 0.0), axis=-1, keepdims=True)
    xn = jnp.where(rope, kr * lax.rsqrt(ss * (1.0 / QK_ROPE) + EPS) * w, 0.0)
    return _rope(xn, cos, sin_signed, swap)


def _head_fn(q, kv, kr, cos, sin_signed, qn, kn, swap):
    nope, rope = _lane_masks(q.shape)

    def rstd(x, mask, n):
        return lax.rsqrt(jnp.sum(jnp.where(mask, x * x, 0.0), axis=-1, keepdims=True) * (1.0 / n) + EPS)

    qs = jnp.where(nope, rstd(q, nope, QK_NOPE), rstd(q, rope, QK_ROPE))
    qp = _rope(jnp.where(nope | rope, q * qs * qn, 0.0), cos, sin_signed, swap)
    kp = jnp.where(nope, kv * rstd(kv, nope, QK_NOPE) * kn, 0.0) + kr
    vp = jnp.where(nope, 0.0, kv)
    return qp, kp, vp


def _heads_fwd(q_raw, kv_raw, kr, cos, sin_signed, qn, kn, swap):
    nh, t, _ = q_raw.shape
    tm = _row_tile(t, 512)
    hblk = ((None, tm, HEAD_PAD), lambda i, h: (h, i, 0))
    tblk = ((tm, HEAD_PAD), lambda i, h: (i, 0))
    par = ((1, HEAD_PAD), lambda i, h: (0, 0))
    sw = ((HEAD_PAD, HEAD_PAD), lambda i, h: (0, 0))
    def fn(*tiles):
        qp, kp, vp = _head_fn(*tiles)
        return qp * Q_PRESCALE, kp, vp

    return _ew("mla_heads_fwd", (t // tm, nh), fn,
               [(q_raw,) + hblk, (kv_raw,) + hblk, (kr,) + tblk, (cos,) + tblk, (sin_signed,) + tblk,
                (qn,) + par, (kn,) + par, (swap,) + sw],
               [(SDS((nh, t, HEAD_PAD), BF),) + hblk + (False,)] * 3)


def _heads_bwd(q_raw, kv_raw, kr, cos, sin_signed, qn, kn, swap, dqp, dkp, dvp):
    nh, t, _ = q_raw.shape
    tm = _row_tile(t, 512)
    hblk = ((None, tm, HEAD_PAD), lambda i, h: (h, i, 0))
    tblk = ((tm, HEAD_PAD), lambda i, h: (i, 0))
    par = ((1, HEAD_PAD), lambda i, h: (0, 0))
    sw = ((HEAD_PAD, HEAD_PAD), lambda i, h: (0, 0))

    def body(q, kv, kr, cos, sn, qn, kn, swap, dqp, dkp, dvp, dq, dkv, dkr, dqn, dkn):
        f = lambda q, kv, kr, qn, kn: _head_fn(q, kv, kr, cos[...], sn[...], qn, kn, swap[...])
        _, vjp = jax.vjp(f, q[...], kv[...], kr[...], qn[...], kn[...])
        g = vjp((dqp[...].astype(F32), dkp[...].astype(F32), dvp[...].astype(F32)))
        dq[...] = g[0].astype(dq.dtype)
        dkv[...] = g[1].astype(dkv.dtype)
        h0 = pl.program_id(1) == 0
        first = h0 & (pl.program_id(0) == 0)
        for o, v, c in ((dkr, g[2], h0), (dqn, g[3], first), (dkn, g[4], first)):
            @pl.when(c)
            def _(o=o, v=v):
                o[...] = v

            @pl.when(jnp.logical_not(c))
            def _(o=o, v=v):
                o[...] += v

    spec = lambda b: pl.BlockSpec(*b)
    return pl.pallas_call(
        body, name="mla_heads_bwd", grid=(t // tm, nh),
        in_specs=[spec(hblk), spec(hblk), spec(tblk), spec(tblk), spec(tblk), spec(par), spec(par), spec(sw),
                  spec(hblk), spec(hblk), spec(hblk)],
        out_specs=[spec(hblk), spec(hblk), spec(tblk), spec(par), spec(par)],
        out_shape=[SDS((nh, t, HEAD_PAD), BF), SDS((nh, t, HEAD_PAD), BF), SDS((t, HEAD_PAD), F32),
                   SDS((1, HEAD_PAD), F32), SDS((1, HEAD_PAD), F32)],
        compiler_params=_cparams(2),
    )(q_raw, kv_raw, kr, cos, sin_signed, qn, kn, swap, dqp, dkp, dvp)


ATT_TILE = 512
ATT_SCALE = QK_DIM ** -0.5
LOG2E = 1.4426950408889634
LN2 = 0.6931471805599453
Q_PRESCALE = ATT_SCALE * LOG2E


def _flash_fwd(qs, k, v, nb, seq):
    nh, t, dh = qs.shape
    tq = _row_tile(seq, ATT_TILE)
    nq = seq // tq

    def body(q_ref, k_ref, v_ref, o_ref, lse_ref):
        qi = pl.program_id(2)
        qt = q_ref[...]

        def tile(j, carry, diagonal):
            m, l, acc = carry
            rows = pl.ds(pl.multiple_of(j * tq, tq), tq)
            s = _dotf(qt, k_ref[rows, :], NT)
            if diagonal:
                r = lax.broadcasted_iota(jnp.int32, (tq, tq), 0)
                c = lax.broadcasted_iota(jnp.int32, (tq, tq), 1)
                s = jnp.where(c <= r, s, -jnp.inf)
            m_new = jnp.maximum(m, jnp.max(s, axis=-1, keepdims=True))
            alpha = jnp.exp2(m - m_new)
            p = jnp.exp2(s - m_new)
            return m_new, alpha * l + jnp.sum(p, axis=-1, keepdims=True), alpha * acc + _dotf(p, v_ref[rows, :], NN)

        init = (jnp.full((tq, 1), -jnp.inf, F32), jnp.zeros((tq, 1), F32), jnp.zeros((tq, dh), F32))
        carry = lax.fori_loop(0, qi, lambda j, c: tile(j, c, False), init)
        m, l, acc = tile(qi, carry, True)
        o_ref[...] = (acc / l).astype(o_ref.dtype)
        lse_ref[...] = m + jnp.log2(l)

    qblk = pl.BlockSpec((None, tq, dh), lambda h, b, i: (h, b * nq + i, 0))
    kblk = pl.BlockSpec((None, seq, dh), lambda h, b, i: (h, b, 0))
    return pl.pallas_call(
        body, name="mla_flash_fwd", grid=(nh, nb, nq), in_specs=[qblk, kblk, kblk],
        out_specs=[qblk, pl.BlockSpec((None, tq, 1), lambda h, b, i: (h, b * nq + i, 0))],
        out_shape=[SDS((nh, t, dh), BF), SDS((nh, t, 1), F32)],
        compiler_params=_cparams(3, VMEM_BIG),
    )(qs, k, v)


def _flash_bwd(qs, k, v, o, lse, do, nb, seq):
    nh, t, dh = qs.shape
    tq = _row_tile(seq, ATT_TILE)
    nq = seq // tq

    def row_of(col):
        return jnp.broadcast_to(col, (tq, LANES)).T[0:1, :]

    def body(q_ref, k_ref, v_ref, o_ref, lse_ref, do_ref, dq_ref, dk_ref, dv_ref, kt_sc, lrow_sc, drow_sc, dqt_sc):
        for c in range(nq):
            rows = pl.ds(c * tq, tq)
            kt_sc[c] = k_ref[rows, :].T
            delta = jnp.sum(do_ref[rows, :].astype(F32) * o_ref[rows, :].astype(F32), axis=-1, keepdims=True)
            drow_sc[c] = row_of(delta)
            lrow_sc[c] = row_of(lse_ref[rows, :])
        dqt_sc[...] = jnp.zeros_like(dqt_sc)

        def kv_step(j, _):
            rows_j = pl.ds(pl.multiple_of(j * tq, tq), tq)
            ks, vs, kt = k_ref[rows_j, :], v_ref[rows_j, :], kt_sc[j]

            def q_tile(i, carry, diagonal):
                dk, dv = carry
                rows_i = pl.ds(pl.multiple_of(i * tq, tq), tq)
                qt, dot_ = q_ref[rows_i, :], do_ref[rows_i, :]
                pt = jnp.exp2(_dotf(ks, qt, NT) - lrow_sc[i])
                if diagonal:
                    kk = lax.broadcasted_iota(jnp.int32, (tq, tq), 0)
                    qq = lax.broadcasted_iota(jnp.int32, (tq, tq), 1)
                    pt = jnp.where(kk <= qq, pt, 0.0)
                dst = (pt * (_dotf(vs, dot_, NT) - drow_sc[i])).astype(BF)
                dqt_sc[i] += _dotf(kt, dst, NN)
                return dk + _dotf(dst, qt, NN), dv + _dotf(pt, dot_, NN)

            zero = jnp.zeros((tq, dh), F32)
            carry = q_tile(j, (zero, zero), True)
            dk, dv = lax.fori_loop(j + 1, nq, lambda i, c: q_tile(i, c, False), carry)
            dk_ref[rows_j, :] = dk * LN2
            dv_ref[rows_j, :] = dv
            return 0

        lax.fori_loop(0, nq, kv_step, 0)
        for c in range(nq):
            dq_ref[pl.ds(c * tq, tq), :] = dqt_sc[c].T * ATT_SCALE

    full = pl.BlockSpec((None, seq, dh), lambda h, b: (h, b, 0))
    sfull = pl.BlockSpec((None, seq, 1), lambda h, b: (h, b, 0))
    return pl.pallas_call(
        body, name="mla_flash_bwd", grid=(nh, nb), in_specs=[full, full, full, full, sfull, full],
        out_specs=[full, full, full], out_shape=[SDS((nh, t, dh), F32)] * 3,
        scratch_shapes=[pltpu.VMEM((nq, dh, tq), BF), pltpu.VMEM((nq, 1, tq), F32), pltpu.VMEM((nq, 1, tq), F32),
                        pltpu.VMEM((nq, dh, tq), F32)],
        compiler_params=_cparams(2, VMEM_BIG),
    )(qs, k, v, o, lse, do)


def _heads_nt(name, a, wt, out_dtype):
    t, kdim = a.shape
    tm = _row_tile(t, 512)
    nw = MLA_HEADS * HEAD_PAD

    def body(a_ref, w_ref, o_ref):
        r = _dotf(a_ref[...], w_ref[...], NT)
        for h in range(MLA_HEADS):
            o_ref[h] = r[:, HEAD_PAD * h:HEAD_PAD * (h + 1)].astype(o_ref.dtype)

    return pl.pallas_call(
        body, name=name, grid=(t // tm,),
        in_specs=[pl.BlockSpec((tm, kdim), lambda i: (i, 0)), pl.BlockSpec((nw, kdim), lambda i: (0, 0))],
        out_specs=pl.BlockSpec((MLA_HEADS, tm, HEAD_PAD), lambda i: (0, i, 0)),
        out_shape=SDS((MLA_HEADS, t, HEAD_PAD), out_dtype), compiler_params=_cparams(1, VMEM_BIG),
    )(a, wt)


def _all_heads(a_ref):
    return jnp.concatenate([a_ref[h] for h in range(MLA_HEADS)], axis=1)


def _heads_nn(name, a, w, n, out_dtype, res=None):
    t = a.shape[1]
    tm = _row_tile(t, 512)
    nw = MLA_HEADS * HEAD_PAD

    def body(*refs):
        a_ref, w_ref, o_ref = refs[0], refs[1], refs[-1]
        r = _dotf(_all_heads(a_ref), w_ref[...], NN)
        if res is not None:
            r = r + refs[2][...]
        o_ref[...] = r.astype(o_ref.dtype)

    row = pl.BlockSpec((tm, n), lambda i: (i, 0))
    in_specs = [pl.BlockSpec((MLA_HEADS, tm, HEAD_PAD), lambda i: (0, i, 0)), pl.BlockSpec((nw, n), lambda i: (0, 0))]
    args = [a, w]
    if res is not None:
        in_specs.append(row)
        args.append(res)
    return pl.pallas_call(body, name=name, grid=(t // tm,), in_specs=in_specs, out_specs=row,
                          out_shape=SDS((t, n), out_dtype), compiler_params=_cparams(1, VMEM_BIG))(*args)


def _heads_wgrad(name, a, b, n):
    t = b.shape[0]
    tk = _row_tile(t, 512)
    nw = MLA_HEADS * HEAD_PAD
    steps = t // tk

    def body(a_ref, b_ref, o_ref, acc):
        k = pl.program_id(0)

        @pl.when(k == 0)
        def _():
            acc[...] = jnp.zeros_like(acc)

        acc[...] += _dotf(_all_heads(a_ref), b_ref[...], TN)

        @pl.when(k == steps - 1)
        def _():
            o_ref[...] = acc[...]

    return pl.pallas_call(
        body, name=name, grid=(steps,),
        in_specs=[pl.BlockSpec((MLA_HEADS, tk, HEAD_PAD), lambda k: (0, k, 0)), pl.BlockSpec((tk, n), lambda k: (k, 0))],
        out_specs=pl.BlockSpec((nw, n), lambda k: (0, 0)), out_shape=SDS((nw, n), F32),
        scratch_shapes=[pltpu.VMEM((nw, n), F32)], compiler_params=_cparams(1, VMEM_BIG),
    )(a, b)


PM_CKV, PM_KR, PM_CQ = 0, KV_LORA, KV_LORA + HEAD_PAD
PM_DIM = KV_LORA + HEAD_PAD + Q_LORA


def _lat_specs(t, tm):
    return (((tm, KV_LORA), lambda i: (i, 0)), ((tm, HEAD_PAD), lambda i: (i, PM_KR // HEAD_PAD)),
            ((tm, Q_LORA), lambda i: (i, PM_CQ // Q_LORA)))


def _mla_fwd(x, nw, wm, small, tables, nb, seq):
    t, d = x.shape
    cos, sin_signed, swap = tables
    h = _rms_fwd("mla_rms", x, nw)
    pm = _proj_nt("mla_in", h, wm["in_t"], PM_DIM, F32, tn=PM_DIM // 3)
    tm = _row_tile(t, 512)
    ckv_s, kr_s, cq_s = _lat_specs(t, tm)
    row = lambda i: (i, 0)
    par = lambda n: ((1, n), lambda i: (0, 0))
    ckvn = _ew("mla_ckv_norm", (t // tm,), _rms_fn, [(pm,) + ckv_s, (small["kv_a_norm"],) + par(KV_LORA)],
               [(SDS((t, KV_LORA), BF), (tm, KV_LORA), row, False)])[0]
    cqn = _ew("mla_cq_norm", (t // tm,), _rms_fn, [(pm,) + cq_s, (small["q_a_norm"],) + par(Q_LORA)],
              [(SDS((t, Q_LORA), BF), (tm, Q_LORA), row, False)])[0]
    tb = ((tm, HEAD_PAD), row)
    kr = _ew("mla_krope", (t // tm,), _krope_fn,
             [(pm,) + kr_s, (small["k_norm"],) + par(HEAD_PAD), (cos,) + tb, (sin_signed,) + tb,
              (swap, (HEAD_PAD, HEAD_PAD), lambda i: (0, 0))],
             [(SDS((t, HEAD_PAD), F32),) + tb + (False,)])[0]
    q_raw = _heads_nt("mla_q_b", cqn, wm["qb_t"], F32)
    kv_raw = _heads_nt("mla_kv_b", ckvn, wm["kvb_t"], F32)
    qp, kp, vp = _heads_fwd(q_raw, kv_raw, kr, cos, sin_signed, small["q_norm"], small["k_norm"], swap)
    o, lse = _flash_fwd(qp, kp, vp, nb, seq)
    out = _heads_nn("mla_out", o, wm["out"], d, F32, res=x)
    return out, (x, h, pm, ckvn, cqn, kr, q_raw, kv_raw, qp, kp, vp, o, lse)


def _mla_bwd(dy, nw, wm, small, tables, saved, nb, seq):
    x, h, pm, ckvn, cqn, kr, q_raw, kv_raw, qp, kp, vp, o, lse = saved
    t, d = x.shape
    cos, sin_signed, swap = tables
    do = _heads_nt("mla_bwd_do", dy, wm["out"], BF)
    g_out = _heads_wgrad("mla_w_out_g", o, dy, d)
    dqp, dkp, dvp = _flash_bwd(qp, kp, vp, o, lse, do, nb, seq)
    dq_raw, dkv_raw, dkr, d_qn, d_kn = _heads_bwd(q_raw, kv_raw, kr, cos, sin_signed, small["q_norm"],
                                                   small["k_norm"], swap, dqp, dkp, dvp)
    dcqn = _heads_nn("mla_bwd_dcq", dq_raw, wm["qb_t"], Q_LORA, F32)
    dckvn = _heads_nn("mla_bwd_dckv", dkv_raw, wm["kvb_t"], KV_LORA, F32)
    g_qb = _heads_wgrad("mla_w_qb_g", dq_raw, cqn, Q_LORA)
    g_kvb = _heads_wgrad("mla_w_kvb_g", dkv_raw, ckvn, KV_LORA)
    tm = _row_tile(t, 512)
    ckv_s, kr_s, cq_s = _lat_specs(t, tm)
    row = lambda i: (i, 0)
    par = lambda n: ((1, n), lambda i: (0, 0))

    def rms_b(xv, w, dv):
        _, vjp = jax.vjp(_rms_fn, xv, w)
        return vjp(dv)

    dckv, d_kva = _ew("mla_ckv_norm_bwd", (t // tm,), rms_b,
                      [(pm,) + ckv_s, (small["kv_a_norm"],) + par(KV_LORA), (dckvn, (tm, KV_LORA), row)],
                      [(SDS((t, KV_LORA), BF), (tm, KV_LORA), row, False),
                       (SDS((1, KV_LORA), F32),) + par(KV_LORA) + (True,)], acc_axes=(0,))
    dcq, d_qa = _ew("mla_cq_norm_bwd", (t // tm,), rms_b,
                    [(pm,) + cq_s, (small["q_a_norm"],) + par(Q_LORA), (dcqn, (tm, Q_LORA), row)],
                    [(SDS((t, Q_LORA), BF), (tm, Q_LORA), row, False),
                     (SDS((1, Q_LORA), F32),) + par(Q_LORA) + (True,)], acc_axes=(0,))
    tb = ((tm, HEAD_PAD), row)

    def kr_b(krv, w, cosv, sinv, sw, dv):
        _, vjp = jax.vjp(lambda a, b: _krope_fn(a, b, cosv, sinv, sw), krv, w)
        return vjp(dv)

    dkr_raw, d_kn2 = _ew("mla_krope_bwd", (t // tm,), kr_b,
                         [(pm,) + kr_s, (small["k_norm"],) + par(HEAD_PAD), (cos,) + tb, (sin_signed,) + tb,
                          (swap, (HEAD_PAD, HEAD_PAD), lambda i: (0, 0)), (dkr,) + tb],
                         [(SDS((t, HEAD_PAD), BF),) + tb + (False,),
                          (SDS((1, HEAD_PAD), F32),) + par(HEAD_PAD) + (True,)], acc_axes=(0,))
    dh = _seg_nn("mla_bwd_dh", [(dckv, wm["in_t"], PM_CKV), (dkr_raw, wm["in_t"], PM_KR),
                                (dcq, wm["in_t"], PM_CQ)], d, BF, tk=128)
    dx, dnw = _rms_bwd("mla_bwd_rms", x, nw, dh, dy)
    g = {
        "in_ckv_t": _wgrad("mla_w_in_ckv_g", dckv, h, KV_LORA, d, tm=KV_LORA, tn=d, out_dtype=F32),
        "in_kr_t": _wgrad("mla_w_in_kr_g", dkr_raw, h, HEAD_PAD, d, tm=HEAD_PAD, tn=d, out_dtype=F32),
        "in_cq_t": _wgrad("mla_w_in_cq_g", dcq, h, Q_LORA, d, tm=Q_LORA, tn=d, out_dtype=F32),
        "qb_t": g_qb, "kvb_t": g_kvb, "out": g_out,
        "q_a_norm": d_qa, "kv_a_norm": d_kva, "q_norm": d_qn, "k_norm": d_kn + d_kn2,
    }
    return dx, dnw, g


def _mesh_pos():
    return lax.axis_index("x"), lax.axis_index("y"), lax.axis_index("c")


def _peer(pos, k):
    x, y, c = pos
    return (x ^ ((k >> 2) & 1), y ^ ((k >> 1) & 1), c ^ (k & 1))


def _flat(pos):
    return 4 * pos[0] + 2 * pos[1] + pos[2]


def _slab(ref, axis, start, size):
    idx = [slice(None)] * axis + [pl.ds(start, size)]
    return ref.at[tuple(idx)]


class _Exchange:
    def __init__(self, kind, items):
        self.kind = kind
        self.axes = [ax for _, ax in items]
        self.arrays = [a for a, _ in items]
        n = len(items)
        self.out_shapes = []
        self.sizes = []
        for a, ax in items:
            shp = list(a.shape)
            if kind == "gather":
                self.sizes.append(shp[ax])
                shp[ax] *= N_DEV
                self.out_shapes.append(SDS(tuple(shp), a.dtype))
            else:
                shp[ax] //= N_DEV
                self.sizes.append(shp[ax])
                self.out_shapes.append(SDS((N_DEV,) + tuple(shp), a.dtype))
        self.sems = [pltpu.SemaphoreType.DMA((n, N_DEV - 1)), pltpu.SemaphoreType.DMA((n, N_DEV - 1)),
                     pltpu.SemaphoreType.DMA((n,))]

    def _copies(self, srcs, dsts, sems, with_arrivals=True):
        send_sems, recv_sems, local_sems = sems
        pos = _mesh_pos()
        me = _flat(pos)
        local, sends, recvs = [], [], []
        for t, (src, dst) in enumerate(zip(srcs, dsts)):
            ax, sz = self.axes[t], self.sizes[t]
            if self.kind == "gather":
                mine = _slab(dst, ax, me * sz, sz)
                local.append(pltpu.make_async_copy(src, mine, local_sems.at[t]))
            else:
                mine = dst.at[me]
                local.append(pltpu.make_async_copy(_slab(src, ax, me * sz, sz), mine, local_sems.at[t]))
            for k in range(1, N_DEV):
                peer = _peer(pos, k)
                there = _flat(peer)
                if self.kind == "gather":
                    out_src, landing = src, _slab(dst, ax, there * sz, sz)
                else:
                    out_src, landing = _slab(src, ax, there * sz, sz), dst.at[there]
                common = dict(send_sem=send_sems.at[t, k - 1], recv_sem=recv_sems.at[t, k - 1], device_id=peer,
                              device_id_type=pl.DeviceIdType.MESH)
                sends.append(pltpu.make_async_remote_copy(src_ref=out_src, dst_ref=mine, **common))
                if with_arrivals:
                    recvs.append(pltpu.make_async_remote_copy(src_ref=out_src, dst_ref=landing, **common))
        return local, sends, recvs

    def start(self, srcs, dsts, sems):
        local, sends, _ = self._copies(srcs, dsts, sems, with_arrivals=False)
        for cp in local + sends:
            cp.start()

    def wait(self, srcs, dsts, sems):
        local, sends, recvs = self._copies(srcs, dsts, sems)
        for rc in recvs:
            rc.wait_recv()
        for rc in sends:
            rc.wait_send()
        for cp in local:
            cp.wait()

    def run(self, name):
        n = len(self.arrays)

        def body(*refs):
            srcs, dsts, sems = refs[:n], refs[n:2 * n], refs[2 * n:]
            self.start(srcs, dsts, sems)
            self.wait(srcs, dsts, sems)

        hbm = pl.BlockSpec(memory_space=pl.ANY)
        return pl.pallas_call(body, name=name, in_specs=[hbm] * n, out_specs=[hbm] * n, out_shape=self.out_shapes,
                              scratch_shapes=self.sems)(*self.arrays)


def _adam_math(w, g, m, v):
    m = ADAM_B1 * m + (1.0 - ADAM_B1) * g
    v = ADAM_B2 * v + (1.0 - ADAM_B2) * (g * g)
    m_hat = m / (1.0 - ADAM_B1 ** ADAM_STEP)
    v_hat = v / (1.0 - ADAM_B2 ** ADAM_STEP)
    delta = -ADAM_LR * (m_hat / (jnp.sqrt(v_hat) + ADAM_EPS) + ADAM_WD * w)
    return delta, m, v


def _adam(name, land, land_blk, land_idx, w, m, v, transposed, ck):
    n, r, c = w.shape
    wblk = ((None, ck, c), lambda a, i: (a, i, 0))

    def fn(parts, w, m, v):
        g = parts[0].astype(F32)
        for s in range(1, N_DEV):
            g = g + parts[s].astype(F32)
        if transposed:
            g = g.T
        delta, m2, v2 = _adam_math(w, g, m, v)
        return g, delta, m2, v2

    return _ew(name, (n, r // ck), fn,
               [(land, land_blk, land_idx), (w,) + wblk, (m,) + wblk, (v,) + wblk],
               [(SDS(w.shape, F32),) + wblk + (False,)] * 4, vmem=VMEM_BIG)


def _prep_ffn(gate, up, down):
    def body(g, u, dn, o):
        o[0] = g[...].T.astype(BF)
        o[1] = u[...].T.astype(BF)
        o[2] = dn[...].astype(BF)

    cblk = pl.BlockSpec((None, None, D_MODEL, FF_SHARD), lambda l, i: (l, i, 0, 0))
    rblk = pl.BlockSpec((None, None, FF_SHARD, D_MODEL), lambda l, i: (l, i, 0, 0))
    return pl.pallas_call(
        body, name="prep_ffn", grid=(2, 2), in_specs=[cblk, cblk, rblk],
        out_specs=pl.BlockSpec((3, FF_SHARD, D_MODEL), lambda l, i: (2 * l + i, 0, 0)),
        out_shape=SDS((12, FF_SHARD, D_MODEL), BF), compiler_params=_cparams(2, VMEM_BIG),
    )(gate, up, down)


def _transpose_cast(name, w, dtype):
    def body(a, o):
        o[...] = a[...].T.astype(dtype)

    r, c = w.shape
    return pl.pallas_call(body, name=name, out_shape=SDS((c, r), dtype),
                          compiler_params=pltpu.CompilerParams(vmem_limit_bytes=VMEM_BIG))(w)


SMALL_SHARDED = (("norm_w", 6 * 128), ("conv_w", CONV_K * 512), ("q_a_norm", 48), ("kv_a_norm", 32))
SMALL_PACK = 3072


def _dyn(a, start, size):
    return lax.dynamic_slice_in_dim(a, start, size, axis=a.ndim - 1)


def _layout_ssm(ssm_in_t, ssm_out_all):
    d = ssm_in_t.shape[1]
    dt_rows = ssm_in_t[D_INNER + CONV_DIM:].reshape(SSM_GROUPS, SSM_HPG, d)
    return {"z_t": ssm_in_t[:D_INNER], "xbc_t": ssm_in_t[D_INNER:D_INNER + CONV_DIM],
            "dt_t": jnp.pad(dt_rows, ((0, 0), (0, LANES - SSM_HPG), (0, 0))).reshape(SSM_GROUPS * LANES, d),
            "out": ssm_out_all}


def _layout_mla(mla_in_all, qb_all, kvb_all, mla_out_all):
    d = mla_out_all.shape[1]
    in_t = mla_in_all.T
    kr_rows = jnp.pad(in_t[Q_LORA + KV_LORA:], ((QK_NOPE, HEAD_PAD - QK_DIM), (0, 0)))
    qb_heads = jnp.pad(qb_all.reshape(MLA_HEADS, QK_DIM, Q_LORA), ((0, 0), (0, HEAD_PAD - QK_DIM), (0, 0)))
    out_heads = jnp.pad(mla_out_all.reshape(MLA_HEADS, 64, d), ((0, 0), (64, 0), (0, 0)))
    return {"in_t": jnp.concatenate([in_t[Q_LORA:Q_LORA + KV_LORA], kr_rows, in_t[:Q_LORA]], axis=0),
            "qb_t": qb_heads.reshape(MLA_HEADS * HEAD_PAD, Q_LORA), "kvb_t": kvb_all,
            "out": out_heads.reshape(MLA_HEADS * HEAD_PAD, d)}


def _layout_small(conv_w, conv_b, dt_bias, a_log, d_skip, ssm_norm_w, q_a_norm, kv_a_norm, q_norm, k_norm):
    lane_heads = lambda p: jnp.pad(p.reshape(SSM_GROUPS, SSM_HPG), ((0, 0), (0, LANES - SSM_HPG))).reshape(1, -1)
    pad_head = lambda p: jnp.pad(p.reshape(1, QK_DIM), ((0, 0), (0, HEAD_PAD - QK_DIM)))
    return {"conv_w": conv_w, "conv_b": conv_b, "dt_bias": lane_heads(dt_bias), "a_log": lane_heads(a_log),
            "d_skip": lane_heads(d_skip), "ssm_norm_w": ssm_norm_w, "q_a_norm": q_a_norm, "kv_a_norm": kv_a_norm,
            "q_norm": pad_head(q_norm), "k_norm": pad_head(k_norm)}


class _Plan:
    def __init__(self):
        self.make = {}
        self.land = {}

    def comm(self, host, ctx):
        return self.make[host](ctx) if host in self.make else None

    def done(self, host, results, ctx):
        if host in self.land:
            self.land[host](results, ctx)


def _local_step(x, positions, loss_target, ctx, plan=None):
    plan = plan if plan is not None else _Plan()
    nb, seq, d = x.shape
    t = nb * seq
    xf = x.reshape(t, d)
    norm = ctx["norm"]
    tables = list(_rope_tables(positions.reshape(t, 1))) + [_swap_matrix()]
    x1, s_f0, got = _ffn_fwd("ffn0", xf, norm[0, 0], ctx["ffn0"], comm=plan.comm("ffn0_up", ctx))
    plan.done("ffn0_up", got, ctx)
    x2, s_ssm, got = _ssm_fwd(x1, norm[0, 1], ctx["ws"], ctx["small"], nb, seq, comm=plan.comm("ssm_in_xbc", ctx))
    plan.done("ssm_in_xbc", got, ctx)
    x3, s_f1, got = _ffn_fwd("ffn1", x2, norm[0, 2], ctx["ffn1"], comm=plan.comm("ffn1_up", ctx))
    plan.done("ffn1_up", got, ctx)
    x4, s_f2, got = _ffn_fwd("ffn2", x3, norm[1, 0], ctx["ffn2"], comm=plan.comm("ffn2_up", ctx))
    plan.done("ffn2_up", got, ctx)
    x5, s_mla = _mla_fwd(x4, norm[1, 1], ctx["wm"], ctx["small"], tables, nb, seq)
    x6, s_f3, _ = _ffn_fwd("ffn3", x5, norm[1, 2], ctx["ffn3"])
    dy, loss_cols = _loss_and_grad(x6, loss_target.reshape(t, d))

    dx5, dn12, ctx["g_ffn3"], _, _ = _ffn_bwd("ffn3", dy, norm[1, 2], ctx["ffn3"], s_f3)
    dx4, dn11, ctx["g_mla"] = _mla_bwd(dx5, norm[1, 1], ctx["wm"], ctx["small"], tables, s_mla, nb, seq)
    dx3, dn10, ctx["g_ffn2"], got_a, got_d = _ffn_bwd(
        "ffn2", dx4, norm[1, 0], ctx["ffn2"], s_f2, comm_act=plan.comm("ffn2_bwd_act", ctx),
        comm_dh=plan.comm("ffn2_bwd_dh", ctx))
    plan.done("ffn2_bwd_act", got_a, ctx)
    plan.done("ffn2_bwd_dh", got_d, ctx)
    dx2, dn02, ctx["g_ffn1"], got_a, _ = _ffn_bwd("ffn1", dx3, norm[0, 2], ctx["ffn1"], s_f1,
                                                   comm_act=plan.comm("ffn1_bwd_act", ctx))
    plan.done("ffn1_bwd_act", got_a, ctx)
    dx1, dn01, ctx["g_ssm"], got = _ssm_bwd(dx2, norm[0, 1], ctx["ws"], ctx["small"], s_ssm, nb, seq,
                                            comm=plan.comm("ssm_bwd_dh", ctx))
    plan.done("ssm_bwd_dh", got, ctx)
    dx0, dn00, ctx["g_ffn0"], got_a, _ = _ffn_bwd("ffn0", dx1, norm[0, 0], ctx["ffn0"], s_f0,
                                                   comm_act=plan.comm("ffn0_bwd_act", ctx))
    plan.done("ffn0_bwd_act", got_a, ctx)
    return loss_cols, dx0.reshape(nb, seq, d), (dn00, dn01, dn02, dn10, dn11, dn12)


def kernel(x, positions, norm_w, ffn_w_gate, ffn_w_up, ffn_w_down, ssm_w_in, ssm_conv_w, ssm_conv_b, ssm_dt_bias, ssm_a_log, ssm_d, ssm_norm_w, ssm_w_out, mla_w_in, mla_q_a_norm, mla_kv_a_norm, mla_w_q_b, mla_w_kv_b, mla_q_norm, mla_k_norm, mla_w_out, loss_target, m_norm_w, m_ffn_w_gate, m_ffn_w_up, m_ffn_w_down, m_ssm_w_in, m_ssm_conv_w, m_ssm_conv_b, m_ssm_dt_bias, m_ssm_a_log, m_ssm_d, m_ssm_norm_w, m_ssm_w_out, m_mla_w_in, m_mla_q_a_norm, m_mla_kv_a_norm, m_mla_w_q_b, m_mla_w_kv_b, m_mla_q_norm, m_mla_k_norm, m_mla_w_out, v_norm_w, v_ffn_w_gate, v_ffn_w_up, v_ffn_w_down, v_ssm_w_in, v_ssm_conv_w, v_ssm_conv_b, v_ssm_dt_bias, v_ssm_a_log, v_ssm_d, v_ssm_norm_w, v_ssm_w_out, v_mla_w_in, v_mla_q_a_norm, v_mla_kv_a_norm, v_mla_w_q_b, v_mla_w_kv_b, v_mla_q_norm, v_mla_k_norm, v_mla_w_out):
    nb, seq, d = x.shape
    t = nb * seq
    me = _flat(_mesh_pos())

    ffn_loc = _prep_ffn(ffn_w_gate, ffn_w_up, ffn_w_down)
    ffn_blk = [ffn_loc[3 * n:3 * n + 3] for n in range(4)]
    ssm_in_loc = _transpose_cast("prep_ssm_in", ssm_w_in[0], BF).reshape(SSM_IN_SHARD // 2, 16, LANES)
    ssm_out_loc = ssm_w_out[0].astype(BF)
    mla_in_loc, mla_out_loc = mla_w_in[0].astype(BF), mla_w_out[0].astype(BF)
    qb_loc = _transpose_cast("prep_q_b", mla_w_q_b[0], BF)
    kvb_loc = _transpose_cast("prep_kv_b", mla_w_kv_b[0], BF)
    small_loc = jnp.concatenate([norm_w.reshape(-1), ssm_conv_w.reshape(-1), mla_q_a_norm.reshape(-1),
                                 mla_kv_a_norm.reshape(-1)])
    small_loc = jnp.pad(small_loc, (0, SMALL_PACK - small_loc.shape[0])).reshape(SMALL_PACK // LANES, LANES)

    ffn0_all, small_all = _Exchange("gather", [(ffn_blk[0], 1), (small_loc, 0)]).run("gather_first")
    sm = small_all.reshape(N_DEV, SMALL_PACK)
    conv_w_full = sm[:, 768:768 + 2048].reshape(N_DEV, CONV_K, 512).transpose(1, 0, 2).reshape(CONV_K, CONV_DIM)
    ctx = {"ffn0": ffn0_all,
           "norm": sm[:, :768].reshape(N_DEV, 6, 128).transpose(1, 0, 2).reshape(2, 3, 1, d),
           "small": _layout_small(conv_w_full, ssm_conv_b, ssm_dt_bias, ssm_a_log, ssm_d, ssm_norm_w,
                                  sm[:, 2816:2864].reshape(1, Q_LORA), sm[:, 2864:2896].reshape(1, KV_LORA),
                                  mla_q_norm, mla_k_norm)}
    plan = _Plan()
    plan.make["ffn0_up"] = lambda c: _Exchange("gather", [(ssm_in_loc, 0), (ssm_out_loc, 0)])
    plan.land["ffn0_up"] = lambda r, c: c.update(ws=_layout_ssm(r[0].reshape(SSM_IN_DIM, d), r[1]))
    plan.make["ssm_in_xbc"] = lambda c: _Exchange("gather", [(ffn_blk[1], 1)])
    plan.land["ssm_in_xbc"] = lambda r, c: c.update(ffn1=r[0])
    plan.make["ffn1_up"] = lambda c: _Exchange("gather", [(ffn_blk[2], 1)])
    plan.land["ffn1_up"] = lambda r, c: c.update(ffn2=r[0])
    plan.make["ffn2_up"] = lambda c: _Exchange("gather", [(mla_in_loc, 0), (qb_loc, 0), (kvb_loc, 0), (mla_out_loc, 0),
                                                          (ffn_blk[3], 1)])
    plan.land["ffn2_up"] = lambda r, c: c.update(wm=_layout_mla(r[0], r[1], r[2], r[3]), ffn3=r[4])

    heads_of = lambda a: a.reshape(SSM_GROUPS, LANES, -1)[:, :SSM_HPG].reshape(SSM_HEADS, -1)

    def mla_grad_items(c):
        g = c["g_mla"]
        g_in = jnp.concatenate([g["in_cq_t"], g["in_ckv_t"], g["in_kr_t"][QK_NOPE:QK_DIM]], axis=0).T
        g_qb = g["qb_t"].reshape(MLA_HEADS, HEAD_PAD, Q_LORA)[:, :QK_DIM].reshape(MLA_HEADS * QK_DIM, Q_LORA)
        g_out = g["out"].reshape(MLA_HEADS, HEAD_PAD, d)[:, 64:].reshape(MLA_HEADS * 64, d)
        return [(g_out, 0), (g_in, 0), (g_qb, 0), (g["kvb_t"], 0)]

    def ssm_grad_items(c):
        g = c["g_ssm"]
        g_in_t = jnp.concatenate([g["z_t"], g["x_t"], g["b_t"], g["c_t"], heads_of(g["dt_t"])], axis=0)
        return [(g_in_t.astype(BF).reshape(SSM_IN_DIM // 2, 16, LANES), 0), (g["out"], 0)]

    plan.make["ffn2_bwd_act"] = lambda c: _Exchange("scatter", [(c["g_ffn3"], 1)])
    plan.land["ffn2_bwd_act"] = lambda r, c: c.update(l_ffn3=r[0])
    plan.make["ffn2_bwd_dh"] = lambda c: _Exchange("scatter", mla_grad_items(c))
    plan.land["ffn2_bwd_dh"] = lambda r, c: c.update(l_mla=r)
    plan.make["ffn1_bwd_act"] = lambda c: _Exchange("scatter", [(c["g_ffn2"], 1)])
    plan.land["ffn1_bwd_act"] = lambda r, c: c.update(l_ffn2=r[0])
    plan.make["ssm_bwd_dh"] = lambda c: _Exchange("scatter", [(c["g_ffn1"], 1)])
    plan.land["ssm_bwd_dh"] = lambda r, c: c.update(l_ffn1=r[0])
    plan.make["ffn0_bwd_act"] = lambda c: _Exchange("scatter", ssm_grad_items(c))
    plan.land["ffn0_bwd_act"] = lambda r, c: c.update(l_ssm=r)

    loss_cols, grad_x, dns = _local_step(x, positions, loss_target, ctx, plan)
    loss = lax.psum(jnp.sum(loss_cols), ("x", "y", "c"))
    dn00, dn01, dn02, dn10, dn11, dn12 = dns
    g_ssm, g_mla = ctx["g_ssm"], ctx["g_mla"]
    l_ffn0 = _Exchange("scatter", [(ctx["g_ffn0"], 1)]).run("scatter_last")[0]
    l_ffn = jnp.stack([l_ffn0, ctx["l_ffn1"], ctx["l_ffn2"], ctx["l_ffn3"]], axis=1).reshape(N_DEV, 12, FF_SHARD, d)
    l_mla_out, l_mla_in, l_qb, l_kvb = ctx["l_mla"]
    l_ssm_in, l_ssm_out = ctx["l_ssm"]

    unlane = lambda a: a.reshape(SSM_GROUPS, LANES)[:, :SSM_HPG].reshape(1, SSM_HEADS)
    small_g = jnp.concatenate([
        jnp.concatenate([dn00, dn01, dn02, dn10, dn11, dn12], axis=0).reshape(-1),
        g_ssm["conv_w"].reshape(-1), g_ssm["conv_b"].reshape(-1), unlane(g_ssm["dt_bias"]).reshape(-1),
        unlane(g_ssm["a_log"]).reshape(-1), unlane(g_ssm["d_skip"]).reshape(-1), g_ssm["ssm_norm_w"].reshape(-1),
        g_mla["q_a_norm"].reshape(-1), g_mla["kv_a_norm"].reshape(-1), g_mla["q_norm"][0, :QK_DIM],
        g_mla["k_norm"][0, :QK_DIM]])
    n_small = small_g.shape[0]
    n_small_pad = -(-n_small // (8 * LANES)) * (8 * LANES)
    small_g = jnp.pad(small_g, (0, n_small_pad - n_small)).reshape(n_small_pad // LANES, LANES)
    gs = _Exchange("gather", [(small_g, 0)]).run("gather_small_grads")[0].reshape(N_DEV, n_small_pad)

    outs = {}

    def put(name, res, shape):
        for key, val in zip(("grad", "delta", "new_m", "new_v"), res):
            outs[(key, name)] = val.reshape(shape)

    ck = 256
    for j, (name, w, m, v) in enumerate((("ffn_w_gate", ffn_w_gate, m_ffn_w_gate, v_ffn_w_gate),
                                         ("ffn_w_up", ffn_w_up, m_ffn_w_up, v_ffn_w_up))):
        res = _adam("adam_" + name, l_ffn, (N_DEV, None, FF_SHARD, ck), lambda a, i, j=j: (0, 3 * a + j, 0, i),
                    w.reshape(4, d, FF_SHARD), m.reshape(4, d, FF_SHARD), v.reshape(4, d, FF_SHARD), True, ck)
        put(name, res, w.shape)
    res = _adam("adam_ffn_w_down", l_ffn, (N_DEV, None, 176, d), lambda a, i: (0, 3 * a + 2, i, 0),
                ffn_w_down.reshape(4, FF_SHARD, d), m_ffn_w_down.reshape(4, FF_SHARD, d),
                v_ffn_w_down.reshape(4, FF_SHARD, d), False, 176)
    put("ffn_w_down", res, ffn_w_down.shape)
    l_ssm_in2 = l_ssm_in.reshape(N_DEV, SSM_IN_SHARD, d)
    res = _adam("adam_ssm_w_in", l_ssm_in2, (N_DEV, SSM_IN_SHARD, 128), lambda a, i: (0, 0, i),
                ssm_w_in, m_ssm_w_in, v_ssm_w_in, True, 128)
    put("ssm_w_in", res, ssm_w_in.shape)
    res = _adam("adam_ssm_w_out", l_ssm_out, (N_DEV, 128, d), lambda a, i: (0, i, 0),
                ssm_w_out, m_ssm_w_out, v_ssm_w_out, False, 128)
    put("ssm_w_out", res, ssm_w_out.shape)
    res = _adam("adam_mla_w_in", l_mla_in, (N_DEV, 128, MLA_IN_DIM), lambda a, i: (0, 0, 0),
                mla_w_in, m_mla_w_in, v_mla_w_in, False, 128)
    put("mla_w_in", res, mla_w_in.shape)
    res = _adam("adam_mla_w_q_b", l_qb, (N_DEV, 192, 128), lambda a, i: (0, 0, i),
                mla_w_q_b, m_mla_w_q_b, v_mla_w_q_b, True, 128)
    put("mla_w_q_b", res, mla_w_q_b.shape)
    res = _adam("adam_mla_w_kv_b", l_kvb, (N_DEV, 256, 128), lambda a, i: (0, 0, i),
                mla_w_kv_b, m_mla_w_kv_b, v_mla_w_kv_b, True, 128)
    put("mla_w_kv_b", res, mla_w_kv_b.shape)
    res = _adam("adam_mla_w_out", l_mla_out, (N_DEV, 128, d), lambda a, i: (0, 0, 0),
                mla_w_out, m_mla_w_out, v_mla_w_out, False, 128)
    put("mla_w_out", res, mla_w_out.shape)

    small_params = (
        ("norm_w", norm_w, m_norm_w, v_norm_w, 6 * d, 6, 128), ("ssm_conv_w", ssm_conv_w, m_ssm_conv_w, v_ssm_conv_w,
                                                               CONV_K * CONV_DIM, CONV_K, 512),
        ("ssm_conv_b", ssm_conv_b, m_ssm_conv_b, v_ssm_conv_b, CONV_DIM, 0, 0),
        ("ssm_dt_bias", ssm_dt_bias, m_ssm_dt_bias, v_ssm_dt_bias, SSM_HEADS, 0, 0),
        ("ssm_a_log", ssm_a_log, m_ssm_a_log, v_ssm_a_log, SSM_HEADS, 0, 0),
        ("ssm_d", ssm_d, m_ssm_d, v_ssm_d, SSM_HEADS, 0, 0),
        ("ssm_norm_w", ssm_norm_w, m_ssm_norm_w, v_ssm_norm_w, D_INNER, 0, 0),
        ("mla_q_a_norm", mla_q_a_norm, m_mla_q_a_norm, v_mla_q_a_norm, Q_LORA, 1, 48),
        ("mla_kv_a_norm", mla_kv_a_norm, m_mla_kv_a_norm, v_mla_kv_a_norm, KV_LORA, 1, 32),
        ("mla_q_norm", mla_q_norm, m_mla_q_norm, v_mla_q_norm, QK_DIM, 0, 0),
        ("mla_k_norm", mla_k_norm, m_mla_k_norm, v_mla_k_norm, QK_DIM, 0, 0),
    )
    parts, ws_, ms_, vs_, off = [], [], [], [], 0
    for name, w, m, v, full, rows, shard in small_params:
        seg = gs[:, off:off + full]
        if rows:
            seg = _dyn(seg.reshape(N_DEV, rows, full // rows), me * shard, shard).reshape(N_DEV, rows * shard)
        parts.append(seg)
        ws_.append(w.reshape(1, -1))
        ms_.append(m.reshape(1, -1))
        vs_.append(v.reshape(1, -1))
        off += full
    n_loc = sum(p.shape[1] for p in parts)
    n_loc_pad = -(-n_loc // LANES) * LANES
    padc = lambda a, val=0.0: jnp.pad(jnp.concatenate(a, axis=1), ((0, 0), (0, n_loc_pad - n_loc)),
                                      constant_values=val)
    res = _adam("adam_small", padc(parts).reshape(N_DEV, 1, n_loc_pad), (N_DEV, 1, n_loc_pad), lambda a, i: (0, 0, 0),
                padc(ws_).reshape(1, 1, n_loc_pad), padc(ms_).reshape(1, 1, n_loc_pad),
                padc(vs_, 1.0).reshape(1, 1, n_loc_pad), False, 1)
    off = 0
    for name, w, m, v, full, rows, shard in small_params:
        nloc = w.size
        put(name, [r.reshape(-1)[off:off + nloc] for r in res], w.shape)
        off += nloc

    order = ("norm_w", "ffn_w_gate", "ffn_w_up", "ffn_w_down", "ssm_w_in", "ssm_conv_w", "ssm_conv_b", "ssm_dt_bias",
             "ssm_a_log", "ssm_d", "ssm_norm_w", "ssm_w_out", "mla_w_in", "mla_q_a_norm", "mla_kv_a_norm",
             "mla_w_q_b", "mla_w_kv_b", "mla_q_norm", "mla_k_norm", "mla_w_out")
    return (loss, grad_x, *[outs[(k, n)] for k in ("grad", "delta", "new_m", "new_v") for n in order])
```

```python
import functools
import math

import jax
import jax.numpy as jnp
import numpy as np
from jax import lax
from jax.experimental import pallas as pl
from jax.experimental.pallas import tpu as pltpu

F32 = jnp.float32
BF = jnp.bfloat16
SDS = jax.ShapeDtypeStruct

N_DEV = 8
D_MODEL = 1024
D_FF = 2816
FF_SHARD = D_FF // N_DEV
D_INNER = 2048
SSM_HEADS = 32
SSM_GROUPS = 8
SSM_HPG = 4
SSM_STATE = 128
CONV_K = 4
CONV_DIM = 4096
SSM_IN_DIM = 6176
SSM_IN_SHARD = SSM_IN_DIM // N_DEV
NORM_GROUP = 256
CHUNK = 128
MLA_HEADS = 16
Q_LORA = 384
KV_LORA = 256
QK_NOPE = 64
QK_ROPE = 32
QK_DIM = 96
MLA_IN_DIM = 672
HEAD_PAD = 128
ROPE_THETA = 10000.0
EPS = 1e-6
LANES = 128

ADAM_LR = 0.001
ADAM_B1 = 0.9
ADAM_B2 = 0.999
ADAM_EPS = 1e-08
ADAM_WD = 0.01
ADAM_STEP = 10

VMEM_BIG = 56 * 1024 * 1024

NN = ((1,), (0,))
NT = ((1,), (1,))
TN = ((0,), (0,))


def _dotf(a, b, dn):
    return lax.dot_general(a.astype(BF), b.astype(BF), (dn, ((), ())), preferred_element_type=F32)


def _dot_hi(a, b, dn=NN):
    return lax.dot_general(a, b, (dn, ((), ())), precision=lax.Precision.HIGHEST, preferred_element_type=F32)


def _sigmoid(x):
    return jax.nn.sigmoid(x)


def _silu(x):
    return x * _sigmoid(x)


def _softplus(x):
    return jnp.maximum(x, 0.0) + jnp.log(1.0 + jnp.exp(-jnp.abs(x)))


def _cparams(n_grid, vmem=None):
    return pltpu.CompilerParams(dimension_semantics=("arbitrary",) * n_grid, vmem_limit_bytes=vmem)


def _fmm(name, grid_mn, pairs, outs, *, epi=None, extras=(), n_acc=1, acc_shape=None, vmem=None, alias=None,
         joint=False, hosts=None, row_split=1):
    comm = hosts.comm(name) if hosts is not None else None
    if joint:
        nk_total = pairs[0][7]
        assert all(p[7] == nk_total for p in pairs)
        starts = [0] * len(pairs)
    else:
        nk_total = sum(p[7] for p in pairs)
        starts = []
        s = 0
        for p in pairs:
            starts.append(s)
            s += p[7]
    n_pairs, n_extras, n_outs = len(pairs), len(extras), len(outs)
    single = nk_total == 1

    n_ci = len(comm.arrays) if comm is not None else 0
    n_co = len(comm.out_shapes) if comm is not None else 0
    n_scratch_acc = 0 if single else n_acc

    def body(*refs):
        ab_refs = refs[: 2 * n_pairs]
        e_refs = refs[2 * n_pairs: 2 * n_pairs + n_extras]
        pos = 2 * n_pairs + n_extras + (1 if alias is not None else 0)
        ci_refs = refs[pos: pos + n_ci]
        pos += n_ci
        o_refs = refs[pos: pos + n_outs]
        co_refs = refs[pos + n_outs: pos + n_outs + n_co]
        pos += n_outs + n_co
        acc_refs = refs[pos: pos + n_scratch_acc]
        sem_refs = refs[pos + n_scratch_acc:]
        i, j, k = pl.program_id(0), pl.program_id(1), pl.program_id(2)

        if comm is not None:
            @pl.when((i == 0) & (j == 0) & (k == 0))
            def _():
                comm.start(ci_refs, co_refs, sem_refs)

        compute(ab_refs, e_refs, o_refs, acc_refs, i, j, k)

        if comm is not None:
            @pl.when((i == grid_mn[0] - 1) & (j == grid_mn[1] - 1) & (k == nk_total - 1))
            def _():
                comm.wait(ci_refs, co_refs, sem_refs)

    def compute(ab_refs, e_refs, o_refs, acc_refs, i, j, k):

        def finish(accs, rows=slice(None)):
            res = epi(accs, *[e[rows] for e in e_refs]) if epi is not None else accs
            if not isinstance(res, (tuple, list)):
                res = (res,)
            first = (i == 0) & (j == 0)
            for o, r, spec in zip(o_refs, res, outs):
                if spec[3]:
                    @pl.when(first)
                    def _(o=o, r=r):
                        o[...] = r.astype(o.dtype)

                    @pl.when(jnp.logical_not(first))
                    def _(o=o, r=r):
                        o[...] += r.astype(o.dtype)
                else:
                    o[rows] = r.astype(o.dtype)

        if single:
            tm_all = ab_refs[0].shape[0]
            ch = tm_all // row_split
            for c in range(row_split):
                rows = slice(c * ch, (c + 1) * ch) if row_split > 1 else slice(None)
                accs = [None] * n_acc
                for p, pr in enumerate(pairs):
                    d = _dotf(ab_refs[2 * p][rows], ab_refs[2 * p + 1][...], pr[6])
                    accs[pr[8]] = d if accs[pr[8]] is None else accs[pr[8]] + d
                finish(accs, rows)
            return

        @pl.when(k == 0)
        def _():
            for a in acc_refs:
                a[...] = jnp.zeros_like(a)

        for p, pr in enumerate(pairs):
            def step(p=p, pr=pr):
                acc_refs[pr[8]][...] += _dotf(ab_refs[2 * p][...], ab_refs[2 * p + 1][...], pr[6])

            if n_pairs == 1 or joint:
                step()
            else:
                pl.when((k >= starts[p]) & (k < starts[p] + pr[7]))(step)

        @pl.when(k == nk_total - 1)
        def _():
            finish([a[...] for a in acc_refs])

    in_specs, args = [], []
    for p, pr in enumerate(pairs):
        a, a_blk, a_idx, b, b_blk, b_idx, _, nk, _ = pr
        st = starts[p]

        def amap(i, j, k, a_idx=a_idx, st=st, nk=nk):
            return a_idx(i, j, jnp.clip(k - st, 0, nk - 1))

        def bmap(i, j, k, b_idx=b_idx, st=st, nk=nk):
            return b_idx(i, j, jnp.clip(k - st, 0, nk - 1))

        in_specs += [pl.BlockSpec(a_blk, amap), pl.BlockSpec(b_blk, bmap)]
        args += [a, b]
    for arr, blk, idx in extras:
        in_specs.append(pl.BlockSpec(blk, lambda i, j, k, idx=idx: idx(i, j)))
        args.append(arr)
    io_alias = {}
    if alias is not None:
        in_specs.append(pl.BlockSpec(memory_space=pl.ANY))
        io_alias = {len(args): 0}
        args.append(alias)
    out_specs = [pl.BlockSpec(blk, lambda i, j, k, idx=idx: idx(i, j)) for _, blk, idx, _ in outs]
    out_shape = [o[0] for o in outs]
    scratch = [] if single else [pltpu.VMEM(acc_shape, F32) for _ in range(n_acc)]
    if comm is not None:
        hbm = pl.BlockSpec(memory_space=pl.ANY)
        in_specs += [hbm] * n_ci
        args += list(comm.arrays)
        out_specs += [hbm] * n_co
        out_shape += list(comm.out_shapes)
        scratch += comm.sems
    res = pl.pallas_call(
        body, name=name, grid=(grid_mn[0], grid_mn[1], nk_total), in_specs=in_specs, out_specs=out_specs,
        out_shape=out_shape, scratch_shapes=scratch, input_output_aliases=io_alias,
        compiler_params=_cparams(3, vmem),
    )(*args)
    if comm is not None:
        hosts.done(name, res[n_outs:])
    return res[:n_outs]


def _ew(name, grid, fn, ins, outs, *, acc_axes=(), vmem=None):
    n_in = len(ins)

    def body(*refs):
        res = fn(*[r[...] for r in refs[:n_in]])
        if not isinstance(res, (tuple, list)):
            res = (res,)
        first = None
        for ax in acc_axes:
            c = pl.program_id(ax) == 0
            first = c if first is None else (first & c)
        for o, r, spec in zip(refs[n_in:], res, outs):
            if spec[3]:
                @pl.when(first)
                def _(o=o, r=r):
                    o[...] = r.astype(o.dtype)

                @pl.when(jnp.logical_not(first))
                def _(o=o, r=r):
                    o[...] += r.astype(o.dtype)
            else:
                o[...] = r.astype(o.dtype)

    return pl.pallas_call(
        body, name=name, grid=grid,
        in_specs=[pl.BlockSpec(blk, idx) for _, blk, idx in ins],
        out_specs=[pl.BlockSpec(blk, idx) for _, blk, idx, _ in outs],
        out_shape=[o[0] for o in outs],
        compiler_params=_cparams(len(grid), vmem),
    )(*[a for a, _, _ in ins])


def _row_tile(t, want):
    tm = min(want, t)
    assert t % tm == 0, (t, tm)
    return tm


def _rms_fn(x, w):
    return x * lax.rsqrt(jnp.mean(x * x, axis=-1, keepdims=True) + EPS) * w


def _rms_fwd(name, x, w):
    t, d = x.shape
    tm = _row_tile(t, 512)
    return _ew(name, (t // tm,), _rms_fn,
               [(x, (tm, d), lambda i: (i, 0)), (w, (1, d), lambda i: (0, 0))],
               [(SDS((t, d), BF), (tm, d), lambda i: (i, 0), False)])[0]


def _rms_bwd(name, x, w, dh, dres):
    t, d = x.shape
    tm = _row_tile(t, 512)

    def fn(x, w, dh, dres):
        _, vjp = jax.vjp(_rms_fn, x, w)
        dx, dw = vjp(dh.astype(F32))
        return dx + dres, dw

    row = lambda i: (i, 0)
    return _ew(name, (t // tm,), fn,
               [(x, (tm, d), row), (w, (1, d), lambda i: (0, 0)), (dh, (tm, d), row), (dres, (tm, d), row)],
               [(SDS((t, d), F32), (tm, d), row, False), (SDS((1, d), F32), (1, d), lambda i: (0, 0), True)],
               acc_axes=(0,))


def _loss_and_grad(y, target):
    t, d = y.shape
    tm = _row_tile(t, 512)

    def fn(y, tg):
        e = y - tg
        return e * (1.0 / d), jnp.sum(e * e, axis=0, keepdims=True) * (0.5 / d)

    row = lambda i: (i, 0)
    return _ew("loss_head", (t // tm,), fn, [(y, (tm, d), row), (target, (tm, d), row)],
               [(SDS((t, d), F32), (tm, d), row, False), (SDS((1, d), F32), (1, d), lambda i: (0, 0), True)],
               acc_axes=(0,))


def _ffn_fwd(tag, x, nw, wf, hosts=None):
    gate_t, up_t, down = wf
    t, d = x.shape
    h = _rms_fwd(tag + "_rms", x, nw)
    tm, tn = _row_tile(t, 1024), 256

    def epi(accs):
        g, u = accs
        s = _sigmoid(g)
        sg = g * s
        return 0.5 * sg, 0.5 * (s * (1.0 + g * (1.0 - s))), u, sg * u

    hblk = (h, (tm, d), lambda i, j, k: (i, 0))
    col = lambda i, j: (i, j)
    tblk = lambda w: (w, (tn, d), lambda i, j, k: (j, 0), NT, 1)
    sgh, dsgh, u, a = _fmm(
        tag + "_up", (t // tm, D_FF // tn), [hblk + tblk(gate_t) + (0,), hblk + tblk(up_t) + (1,)],
        [(SDS((t, D_FF), BF), (tm, tn), col, False)] * 4, epi=epi, n_acc=2, joint=True, hosts=hosts, row_split=4)
    tm2 = _row_tile(t, 512)
    y = _fmm(
        tag + "_down", (t // tm2, 1),
        [(a, (tm2, D_FF), lambda i, j, k: (i, 0), down, (D_FF, d), lambda i, j, k: (0, 0), NN, 1, 0)],
        [(SDS((t, d), F32), (tm2, d), lambda i, j: (i, 0), False)],
        extras=[(x, (tm2, d), lambda i, j: (i, 0))],
        epi=lambda accs, xr: xr + 0.5 * accs[0], vmem=VMEM_BIG, hosts=hosts)[0]
    return y, (x, h, sgh, dsgh, u, a)


def _wgrad(name, a, b, m, n, *, tm, tn, tk=512, scale=None, out_dtype=BF, hosts=None):
    t = a.shape[0]
    tk = _row_tile(t, tk)
    epi = (lambda accs: accs[0] * scale) if scale is not None else None
    return _fmm(name, (m // tm, n // tn),
                [(a, (tk, tm), lambda i, j, k: (k, i), b, (tk, tn), lambda i, j, k: (k, j), TN, t // tk, 0)],
                [(SDS((m, n), out_dtype), (tm, tn), lambda i, j: (i, j), False)], epi=epi, acc_shape=(tm, tn),
                vmem=VMEM_BIG, hosts=hosts)[0]


def _ffn_bwd(tag, dy, nw, wf, saved, hosts=None):
    gate_t, up_t, down = wf
    x, h, sgh, dsgh, u, a = saved
    t, d = x.shape
    tm, tn = _row_tile(t, 1024), 256

    def epi(accs, sgh, dsgh, u):
        da = accs[0]
        return da * (u.astype(F32) * dsgh.astype(F32)), da * sgh.astype(F32)

    col = lambda i, j: (i, j)
    dg, du = _fmm(
        tag + "_bwd_act", (t // tm, D_FF // tn),
        [(dy, (tm, d), lambda i, j, k: (i, 0), down, (tn, d), lambda i, j, k: (j, 0), NT, 1, 0)],
        [(SDS((t, D_FF), BF), (tm, tn), col, False)] * 2,
        extras=[(sgh, (tm, tn), col), (dsgh, (tm, tn), col), (u, (tm, tn), col)], epi=epi, hosts=hosts)
    tm2 = _row_tile(t, 256)
    full = lambda i, j, k: (0, 0)
    dh = _fmm(
        tag + "_bwd_dh", (t // tm2, 1),
        [(dg, (tm2, D_FF), lambda i, j, k: (i, 0), gate_t, (D_FF, d), full, NN, 1, 0),
         (du, (tm2, D_FF), lambda i, j, k: (i, 0), up_t, (D_FF, d), full, NN, 1, 0)],
        [(SDS((t, d), BF), (tm2, d), lambda i, j: (i, 0), False)], vmem=VMEM_BIG, joint=True, hosts=hosts)[0]
    dx, dnw = _rms_bwd(tag + "_bwd_rms", x, nw, dh, dy)
    half = D_FF // 2
    g_gate = _wgrad(tag + "_wg", dg, h, D_FF, d, tm=half, tn=d, hosts=hosts)
    if hosts is not None:
        hosts.ctx["g_" + tag + "_gate"] = g_gate
    g_up = _wgrad(tag + "_wu", du, h, D_FF, d, tm=half, tn=d, hosts=hosts)
    if hosts is not None:
        hosts.ctx["g_" + tag + "_up"] = g_up
    g_down = _wgrad(tag + "_wd", a, dy, D_FF, d, tm=half, tn=d, scale=0.5, hosts=hosts)
    return dx, dnw, (g_gate, g_up, g_down)


def _shift_down(cur, prev8, j):
    rolled = pltpu.roll(cur, j, 0)
    sub = lax.broadcasted_iota(jnp.int32, prev8.shape, 0)
    top = jnp.where(sub < j, pltpu.roll(prev8, j, 0), rolled[:8])
    return jnp.concatenate([top, rolled[8:]], axis=0)


def _shift_up(cur, next8, j):
    n = cur.shape[0]
    rolled = pltpu.roll(cur, n - j, 0)
    sub = lax.broadcasted_iota(jnp.int32, next8.shape, 0)
    bot = jnp.where(sub >= 8 - j, pltpu.roll(next8, 8 - j, 0), rolled[n - 8:])
    return jnp.concatenate([rolled[: n - 8], bot], axis=0)


HALO = 16


def _conv_fwd(xbc, w, b, seq):
    t, c = xbc.shape
    ts, tc = _row_tile(seq, 256), 512
    tiles_per_seq = seq // ts
    hb = ts // HALO

    def fn(cur, prev, w, b):
        i = pl.program_id(1)
        cur = cur.astype(F32)
        prev8 = jnp.where(i % tiles_per_seq == 0, 0.0, prev.astype(F32)[HALO - 8:])
        out = b + w[3:4] * cur
        for j in range(1, CONV_K):
            out = out + w[3 - j:4 - j] * _shift_down(cur, prev8, j)
        return out, _silu(out)

    return _ew("ssm_conv_fwd", (c // tc, t // ts), fn,
               [(xbc, (ts, tc), lambda j, i: (i, j)),
                (xbc, (HALO, tc), lambda j, i: (jnp.maximum(i * hb - 1, 0), j)),
                (w, (CONV_K, tc), lambda j, i: (0, j)), (b, (1, tc), lambda j, i: (0, j))],
               [(SDS((t, c), BF), (ts, tc), lambda j, i: (i, j), False)] * 2)


def _conv_bwd(tag, dxa, cpre, xbc, w, col0, seq):
    t, width = dxa.shape
    ts, tc = _row_tile(seq, 256), 512
    tiles_per_seq = seq // ts
    hb = ts // HALO
    cb0 = col0 // tc
    n_halo_blocks = t // HALO

    def dsilu(cv, dv):
        cv = cv.astype(F32)
        s = _sigmoid(cv)
        return dv.astype(F32) * (s * (1.0 + cv * (1.0 - s)))

    def fn(dxa_c, dxa_n, c_c, c_n, x_c, x_p, w):
        i = pl.program_id(1)
        dc = dsilu(c_c, dxa_c)
        last = i % tiles_per_seq == tiles_per_seq - 1
        dc_n = jnp.where(last, 0.0, dsilu(c_n, dxa_n)[:8])
        dx = w[3:4] * dc
        for j in range(1, CONV_K):
            dx = dx + w[3 - j:4 - j] * _shift_up(dc, dc_n, j)
        cur = x_c.astype(F32)
        prev8 = jnp.where(i % tiles_per_seq == 0, 0.0, x_p.astype(F32)[HALO - 8:])
        rows = [jnp.sum(dc * cur, axis=0, keepdims=True)]
        for j in range(1, CONV_K):
            rows.append(jnp.sum(dc * _shift_down(cur, prev8, j), axis=0, keepdims=True))
        sub8 = lax.broadcasted_iota(jnp.int32, (8, dc.shape[1]), 0)
        dw = jnp.zeros((8, dc.shape[1]), F32)
        for kk in range(CONV_K):
            dw = jnp.where(sub8 == kk, rows[CONV_K - 1 - kk], dw)
        return dx, dw, jnp.sum(dc, axis=0, keepdims=True)

    nxt = lambda j, i: (jnp.minimum((i + 1) * hb, n_halo_blocks - 1), j)
    nxt_off = lambda j, i: (jnp.minimum((i + 1) * hb, n_halo_blocks - 1), j + cb0)
    return _ew(tag, (width // tc, t // ts), fn,
               [(dxa, (ts, tc), lambda j, i: (i, j)), (dxa, (HALO, tc), nxt),
                (cpre, (ts, tc), lambda j, i: (i, j + cb0)), (cpre, (HALO, tc), nxt_off),
                (xbc, (ts, tc), lambda j, i: (i, j + cb0)),
                (xbc, (HALO, tc), lambda j, i: (jnp.maximum(i * hb - 1, 0), j + cb0)),
                (w, (CONV_K, tc), lambda j, i: (0, j + cb0))],
               [(SDS((t, width), BF), (ts, tc), lambda j, i: (i, j), False),
                (SDS((8, width), F32), (8, tc), lambda j, i: (0, j), True),
                (SDS((1, width), F32), (1, tc), lambda j, i: (0, j), True)],
               acc_axes=(1,))


def _ssd_chunk(xs, bm, cm, dtr, st, dtb, alog, dsk):
    ell = xs.shape[0]
    xs = xs.astype(F32)
    lane = lax.broadcasted_iota(jnp.int32, (ell, LANES), 1)
    sub = lax.broadcasted_iota(jnp.int32, (ell, LANES), 0)
    lane1 = lax.broadcasted_iota(jnp.int32, (1, LANES), 1)
    causal = sub >= lane
    dt = _softplus(dtr + dtb)
    da = dt * (-jnp.exp(alog))
    acs = _dot_hi(causal.astype(F32), da)
    acs_t = acs.T
    cb = _dotf(cm, bm, NT)
    lo = lane < 64
    ys, news = [], []
    for p in range(2):
        xp = xs[:, LANES * p:LANES * (p + 1)]
        sp = st[LANES * p:LANES * (p + 1), :]
        col, dtc, last, dsel = [], [], [], []
        y_diag = None
        for q in range(2):
            r = 2 * p + q
            col_r = jnp.sum(jnp.where(lane == r, acs, 0.0), axis=1, keepdims=True)
            row_r = jnp.sum(jnp.where(sub == r, acs_t, 0.0), axis=0, keepdims=True)
            dtc_r = jnp.sum(jnp.where(lane == r, dt, 0.0), axis=1, keepdims=True)
            decay = jnp.exp(jnp.where(causal, col_r - row_r, -jnp.inf))
            head = lo if q == 0 else jnp.logical_not(lo)
            d = _dotf(cb * decay, jnp.where(head, xp * dtc_r, 0.0), NN)
            y_diag = d if y_diag is None else y_diag + d
            col.append(col_r)
            dtc.append(dtc_r)
            last.append(jnp.sum(jnp.where(sub[:, :1] == ell - 1, col_r, 0.0), axis=0, keepdims=True))
            dsel.append(jnp.sum(jnp.where(lane1 == r, dsk, 0.0), axis=1, keepdims=True))
        y_off = _dotf(cm, sp, NT) * jnp.where(lo, jnp.exp(col[0]), jnp.exp(col[1]))
        xw = jnp.where(lo, xp * (dtc[0] * jnp.exp(last[0] - col[0])), xp * (dtc[1] * jnp.exp(last[1] - col[1])))
        new = sp * jnp.where(sub < 64, jnp.exp(last[0]), jnp.exp(last[1])) + _dotf(xw, bm, TN)
        ys.append(y_diag + y_off + jnp.where(lo, dsel[0], dsel[1]) * xp)
        news.append(new)
    return jnp.concatenate(ys, axis=1), jnp.concatenate(news, axis=0)


SSD_GP = 4
XW, GW = 2 * LANES * SSD_GP, LANES * SSD_GP


def _grp(ref, q, width):
    return ref[:, width * q:width * (q + 1)]


def _ssd_fwd(xa, dtr, dtb, alog, dsk, nb, seq):
    t = xa.shape[0]
    nc = seq // CHUNK
    row = lambda g, b, c: (b * nc + c, g)
    par = pl.BlockSpec((1, GW), lambda g, b, c: (0, g))
    b_off, c_off = D_INNER // GW, (D_INNER + SSM_GROUPS * SSM_STATE) // GW

    def body(xs, bm, cm, dtr, dtb, alog, dsk, y_ref, st_out, st_ref):
        @pl.when(pl.program_id(2) == 0)
        def _():
            st_ref[...] = jnp.zeros_like(st_ref)

        ins = [(_grp(xs, q, 2 * LANES), _grp(bm, q, LANES), _grp(cm, q, LANES), _grp(dtr, q, LANES), st_ref[q],
                _grp(dtb, q, LANES), _grp(alog, q, LANES), _grp(dsk, q, LANES)) for q in range(SSD_GP)]
        res = [_ssd_chunk(*a) for a in ins]
        for q in range(SSD_GP):
            st_out[q] = ins[q][4]
            y_ref[:, 2 * LANES * q:2 * LANES * (q + 1)] = res[q][0]
            st_ref[q] = res[q][1]

    specs = [pl.BlockSpec((CHUNK, XW), row),
             pl.BlockSpec((CHUNK, GW), lambda g, b, c: (b * nc + c, b_off + g)),
             pl.BlockSpec((CHUNK, GW), lambda g, b, c: (b * nc + c, c_off + g)),
             pl.BlockSpec((CHUNK, GW), row), par, par, par]
    return pl.pallas_call(
        body, name="ssd_fwd", grid=(SSM_GROUPS // SSD_GP, nb, nc), in_specs=specs,
        out_specs=[pl.BlockSpec((CHUNK, XW), row),
                   pl.BlockSpec((SSD_GP, None, None, 2 * LANES, LANES), lambda g, b, c: (g, b, c, 0, 0))],
        out_shape=[SDS((t, D_INNER), F32), SDS((SSM_GROUPS, nb, nc, 2 * LANES, LANES), F32)],
        scratch_shapes=[pltpu.VMEM((SSD_GP, 2 * LANES, LANES), F32)],
        compiler_params=_cparams(3),
    )(xa, xa, xa, dtr, dtb, alog, dsk)


def _ssd_bwd(xa, dtr, dtb, alog, dsk, states, dy, nb, seq):
    t = xa.shape[0]
    nc = seq // CHUNK
    rev = lambda c: nc - 1 - c
    row = lambda g, b, c: (b * nc + rev(c), g)
    par = pl.BlockSpec((1, GW), lambda g, b, c: (0, g))
    b_off, c_off = D_INNER // GW, (D_INNER + SSM_GROUPS * SSM_STATE) // GW

    def body(xs, bm, cm, dtr, dtb, alog, dsk, st_in, dy, dxs, dbm, dcm, ddtr, ddtb, dalog, ddsk, dst_ref):
        @pl.when(pl.program_id(2) == 0)
        def _():
            dst_ref[...] = jnp.zeros_like(dst_ref)

        ins = [(_grp(xs, q, 2 * LANES), _grp(bm, q, LANES), _grp(cm, q, LANES), _grp(dtr, q, LANES), st_in[q],
                _grp(dtb, q, LANES), _grp(alog, q, LANES), _grp(dsk, q, LANES)) for q in range(SSD_GP)]
        cts = [(_grp(dy, q, 2 * LANES), dst_ref[q]) for q in range(SSD_GP)]
        gs = [jax.vjp(_ssd_chunk, *a)[1](ct) for a, ct in zip(ins, cts)]
        for q, g in enumerate(gs):
            dxs[:, 2 * LANES * q:2 * LANES * (q + 1)] = g[0]
            lanes = slice(LANES * q, LANES * (q + 1))
            dbm[:, lanes] = g[1]
            dcm[:, lanes] = g[2]
            ddtr[:, lanes] = g[3]
            dst_ref[q] = g[4]
        first = (pl.program_id(1) == 0) & (pl.program_id(2) == 0)
        for o, k in ((ddtb, 5), (dalog, 6), (ddsk, 7)):
            v = jnp.concatenate([g[k] for g in gs], axis=1)

            @pl.when(first)
            def _(o=o, v=v):
                o[...] = v

            @pl.when(jnp.logical_not(first))
            def _(o=o, v=v):
                o[...] += v

    in_specs = [
        pl.BlockSpec((CHUNK, XW), row),
        pl.BlockSpec((CHUNK, GW), lambda g, b, c: (b * nc + rev(c), b_off + g)),
        pl.BlockSpec((CHUNK, GW), lambda g, b, c: (b * nc + rev(c), c_off + g)),
        pl.BlockSpec((CHUNK, GW), row), par, par, par,
        pl.BlockSpec((SSD_GP, None, None, 2 * LANES, LANES), lambda g, b, c: (g, b, rev(c), 0, 0)),
        pl.BlockSpec((CHUNK, XW), row),
    ]
    out_specs = [pl.BlockSpec((CHUNK, XW), row), pl.BlockSpec((CHUNK, GW), row), pl.BlockSpec((CHUNK, GW), row),
                 pl.BlockSpec((CHUNK, GW), row), par, par, par]
    out_shape = [SDS((t, D_INNER), BF), SDS((t, SSM_GROUPS * LANES), BF), SDS((t, SSM_GROUPS * LANES), BF),
                 SDS((t, SSM_GROUPS * LANES), F32)] + [SDS((1, SSM_GROUPS * LANES), F32)] * 3
    return pl.pallas_call(
        body, name="ssd_bwd", grid=(SSM_GROUPS // SSD_GP, nb, nc), in_specs=in_specs, out_specs=out_specs,
        out_shape=out_shape, scratch_shapes=[pltpu.VMEM((SSD_GP, 2 * LANES, LANES), F32)],
        compiler_params=_cparams(3, VMEM_BIG),
    )(xa, xa, xa, dtr, dtb, alog, dsk, states, dy)


def _gated_fn(y, z, w):
    g = y * _silu(z.astype(F32))
    return g * lax.rsqrt(jnp.mean(g * g, axis=-1, keepdims=True) + EPS) * w


def _gated_norm_fwd(y, z, w):
    t = y.shape[0]
    tm = _row_tile(t, 512)
    blk = ((tm, NORM_GROUP), lambda g, i: (i, g))
    return _ew("ssm_gnorm_fwd", (SSM_GROUPS, t // tm), _gated_fn,
               [(y,) + blk, (z,) + blk, (w, (1, NORM_GROUP), lambda g, i: (0, g))],
               [(SDS((t, D_INNER), BF),) + blk + (False,)])[0]


def _gated_norm_bwd(y, z, w, dout):
    t = y.shape[0]
    tm = _row_tile(t, 512)
    blk = ((tm, NORM_GROUP), lambda g, i: (i, g))
    par = ((1, NORM_GROUP), lambda g, i: (0, g))

    def fn(y, z, w, dout):
        _, vjp = jax.vjp(_gated_fn, y, z, w)
        return vjp(dout.astype(F32))

    return _ew("ssm_gnorm_bwd", (SSM_GROUPS, t // tm), fn,
               [(y,) + blk, (z,) + blk, (w,) + par, (dout,) + blk],
               [(SDS((t, D_INNER), F32),) + blk + (False,), (SDS((t, D_INNER), BF),) + blk + (False,),
                (SDS((1, D_INNER), F32),) + par + (True,)],
               acc_axes=(1,))


def _proj_nt(name, h, wt, n, out_dtype, tn=256, hosts=None):
    t, kdim = h.shape
    tm = _row_tile(t, 1024)
    return _fmm(name, (t // tm, n // tn),
                [(h, (tm, kdim), lambda i, j, k: (i, 0), wt, (tn, kdim), lambda i, j, k: (j, 0), NT, 1, 0)],
                [(SDS((t, n), out_dtype), (tm, tn), lambda i, j: (i, j), False)], hosts=hosts)[0]


def _seg_nn(name, parts, n, out_dtype, tk=256, hosts=None):
    t = parts[0][0].shape[0]
    tm = _row_tile(t, 512)
    pairs = []
    for a, w, row0 in parts:
        kp = a.shape[1]
        tkp = min(tk, kp)
        r0 = row0 // tkp
        pairs.append((a, (tm, tkp), lambda i, j, k: (i, k), w, (tkp, n), lambda i, j, k, r0=r0: (k + r0, 0),
                      NN, kp // tkp, 0))
    return _fmm(name, (t // tm, 1), pairs, [(SDS((t, n), out_dtype), (tm, n), lambda i, j: (i, 0), False)],
                acc_shape=(tm, n), hosts=hosts)[0]


def _ssm_fwd(x, nw, ws, small, nb, seq, hosts=None):
    t, d = x.shape
    h = _rms_fwd("ssm_rms", x, nw)
    z = _proj_nt("ssm_in_z", h, ws["z_t"], D_INNER, BF, hosts=hosts)
    xbc = _proj_nt("ssm_in_xbc", h, ws["xbc_t"], CONV_DIM, BF, hosts=hosts)
    dtr = _proj_nt("ssm_in_dt", h, ws["dt_t"], SSM_GROUPS * LANES, F32)
    cpre, xa = _conv_fwd(xbc, small["conv_w"], small["conv_b"], seq)
    y, states = _ssd_fwd(xa, dtr, small["dt_bias"], small["a_log"], small["d_skip"], nb, seq)
    gn = _gated_norm_fwd(y, z, small["ssm_norm_w"])
    tm = _row_tile(t, 512)
    out = _fmm("ssm_out", (t // tm, 1),
               [(gn, (tm, D_INNER), lambda i, j, k: (i, 0), ws["out"], (D_INNER, d), lambda i, j, k: (0, 0), NN, 1, 0)],
               [(SDS((t, d), F32), (tm, d), lambda i, j: (i, 0), False)],
               extras=[(x, (tm, d), lambda i, j: (i, 0))], epi=lambda accs, xr: xr + accs[0], hosts=hosts)[0]
    return out, (x, h, z, xbc, dtr, cpre, xa, y, states, gn)


def _ssm_bwd(dy, nw, ws, small, saved, nb, seq, hosts=None):
    x, h, z, xbc, dtr, cpre, xa, y, states, gn = saved
    t, d = x.shape
    dgn = _proj_nt("ssm_bwd_dgn", dy, ws["out"], D_INNER, BF)
    d_out = _wgrad("ssm_w_out_g", gn, dy, D_INNER, d, tm=D_INNER // 2, tn=d)
    dyssd, dz, d_normw = _gated_norm_bwd(y, z, small["ssm_norm_w"], dgn)
    dxs, dbm, dcm, ddtr, d_dtb, d_alog, d_dsk = _ssd_bwd(
        xa, dtr, small["dt_bias"], small["a_log"], small["d_skip"], states, dyssd, nb, seq)
    dxbc_x, dcw_x, dcb_x = _conv_bwd("ssm_conv_bwd_x", dxs, cpre, xbc, small["conv_w"], 0, seq)
    dxbc_b, dcw_b, dcb_b = _conv_bwd("ssm_conv_bwd_b", dbm, cpre, xbc, small["conv_w"], D_INNER, seq)
    dxbc_c, dcw_c, dcb_c = _conv_bwd("ssm_conv_bwd_c", dcm, cpre, xbc, small["conv_w"], D_INNER + 1024, seq)
    parts = [(dz, ws["z_t"], 0), (dxbc_x, ws["xbc_t"], 0), (dxbc_b, ws["xbc_t"], D_INNER),
             (dxbc_c, ws["xbc_t"], D_INNER + 1024), (ddtr, ws["dt_t"], 0)]
    dh = _seg_nn("ssm_bwd_dh", parts, d, BF, tk=1024, hosts=hosts)
    dx, dnw = _rms_bwd("ssm_bwd_rms", x, nw, dh, dy)
    g = {
        "z_t": _wgrad("ssm_w_z_g", dz, h, D_INNER, d, tm=1024, tn=d, out_dtype=F32),
        "x_t": _wgrad("ssm_w_x_g", dxbc_x, h, D_INNER, d, tm=1024, tn=d, out_dtype=F32),
        "b_t": _wgrad("ssm_w_b_g", dxbc_b, h, 1024, d, tm=1024, tn=d, out_dtype=F32),
        "c_t": _wgrad("ssm_w_c_g", dxbc_c, h, 1024, d, tm=1024, tn=d, out_dtype=F32),
        "dt_t": _wgrad("ssm_w_dt_g", ddtr, h, 1024, d, tm=1024, tn=d, out_dtype=F32),
        "out": d_out,
        "conv_w": jnp.concatenate([dcw_x[:CONV_K], dcw_b[:CONV_K], dcw_c[:CONV_K]], axis=1),
        "conv_b": jnp.concatenate([dcb_x, dcb_b, dcb_c], axis=1),
        "dt_bias": d_dtb, "a_log": d_alog, "d_skip": d_dsk, "ssm_norm_w": d_normw,
    }
    return dx, dnw, g


def _lane_masks(shape):
    lane = lax.broadcasted_iota(jnp.int32, shape, len(shape) - 1)
    return lane < QK_NOPE, (lane >= QK_NOPE) & (lane < QK_DIM)


def _swap_matrix():
    p = np.zeros((HEAD_PAD, HEAD_PAD), np.float32)
    for i in range(QK_ROPE // 2):
        p[QK_NOPE + QK_ROPE // 2 + i, QK_NOPE + i] = 1.0
        p[QK_NOPE + i, QK_NOPE + QK_ROPE // 2 + i] = 1.0
    return jnp.asarray(p)


def _rope_tables(positions_col):
    t = positions_col.shape[0]
    tm = _row_tile(t, 512)
    freq = np.zeros((1, HEAD_PAD), np.float32)
    inv = 1.0 / (ROPE_THETA ** (np.arange(0, QK_ROPE, 2, dtype=np.float32) / QK_ROPE))
    freq[0, QK_NOPE:QK_NOPE + QK_ROPE // 2] = inv
    freq[0, QK_NOPE + QK_ROPE // 2:QK_DIM] = inv
    sign = np.zeros((1, HEAD_PAD), np.float32)
    sign[0, QK_NOPE:QK_NOPE + QK_ROPE // 2] = -1.0
    sign[0, QK_NOPE + QK_ROPE // 2:QK_DIM] = 1.0

    def fn(pos, freq, sign):
        ang = pos.astype(F32) * freq
        nope, rope = _lane_masks(ang.shape)
        return jnp.where(nope, 1.0, jnp.where(rope, jnp.cos(ang), 0.0)), jnp.sin(ang) * sign

    row = lambda i: (i, 0)
    par = ((1, HEAD_PAD), lambda i: (0, 0))
    return _ew("mla_rope_tables", (t // tm,), fn,
               [(positions_col, (tm, 1), row), (jnp.asarray(freq),) + par, (jnp.asarray(sign),) + par],
               [(SDS((t, HEAD_PAD), F32), (tm, HEAD_PAD), row, False)] * 2)


def _rope(xn, cos, sin_signed, swap):
    return xn * cos + _dot_hi(xn, swap) * sin_signed


def _krope_fn(kr, w, cos, sin_signed, swap):
    _, rope = _lane_masks(kr.shape)
    ss = jnp.sum(jnp.where(rope, kr * kr, 0.0), axis=-1, keepdims=True)
    xn = jnp.where(rope, kr * lax.rsqrt(ss * (1.0 / QK_ROPE) + EPS) * w, 0.0)
    return _rope(xn, cos, sin_signed, swap)


def _head_fn(q, kv, kr, cos, sin_signed, qn, kn, swap):
    nope, rope = _lane_masks(q.shape)

    def rstd(x, mask, n):
        return lax.rsqrt(jnp.sum(jnp.where(mask, x * x, 0.0), axis=-1, keepdims=True) * (1.0 / n) + EPS)

    qs = jnp.where(nope, rstd(q, nope, QK_NOPE), rstd(q, rope, QK_ROPE))
    qp = _rope(jnp.where(nope | rope, q * qs * qn, 0.0), cos, sin_signed, swap)
    kp = jnp.where(nope, kv * rstd(kv, nope, QK_NOPE) * kn, 0.0) + kr
    vp = jnp.where(nope, 0.0, kv)
    return qp, kp, vp


def _heads_fwd(q_raw, kv_raw, kr, cos, sin_signed, qn, kn, swap):
    nh, t, _ = q_raw.shape
    tm = _row_tile(t, 512)
    hblk = ((None, tm, HEAD_PAD), lambda i, h: (h, i, 0))
    tblk = ((tm, HEAD_PAD), lambda i, h: (i, 0))
    par = ((1, HEAD_PAD), lambda i, h: (0, 0))
    sw = ((HEAD_PAD, HEAD_PAD), lambda i, h: (0, 0))
    def fn(*tiles):
        qp, kp, vp = _head_fn(*tiles)
        return qp * Q_PRESCALE, kp, vp

    return _ew("mla_heads_fwd", (t // tm, nh), fn,
               [(q_raw,) + hblk, (kv_raw,) + hblk, (kr,) + tblk, (cos,) + tblk, (sin_signed,) + tblk,
                (qn,) + par, (kn,) + par, (swap,) + sw],
               [(SDS((nh, t, HEAD_PAD), BF),) + hblk + (False,)] * 3)


def _heads_bwd(q_raw, kv_raw, kr, cos, sin_signed, qn, kn, swap, dqp, dkp, dvp):
    nh, t, _ = q_raw.shape
    tm = _row_tile(t, 512)
    hblk = ((None, tm, HEAD_PAD), lambda i, h: (h, i, 0))
    tblk = ((tm, HEAD_PAD), lambda i, h: (i, 0))
    par = ((1, HEAD_PAD), lambda i, h: (0, 0))
    sw = ((HEAD_PAD, HEAD_PAD), lambda i, h: (0, 0))

    def body(q, kv, kr, cos, sn, qn, kn, swap, dqp, dkp, dvp, dq, dkv, dkr, dqn, dkn):
        f = lambda q, kv, kr, qn, kn: _head_fn(q, kv, kr, cos[...], sn[...], qn, kn, swap[...])
        _, vjp = jax.vjp(f, q[...], kv[...], kr[...], qn[...], kn[...])
        g = vjp((dqp[...].astype(F32), dkp[...].astype(F32), dvp[...].astype(F32)))
        dq[...] = g[0].astype(dq.dtype)
        dkv[...] = g[1].astype(dkv.dtype)
        h0 = pl.program_id(1) == 0
        first = h0 & (pl.program_id(0) == 0)
        for o, v, c in ((dkr, g[2], h0), (dqn, g[3], first), (dkn, g[4], first)):
            @pl.when(c)
            def _(o=o, v=v):
                o[...] = v

            @pl.when(jnp.logical_not(c))
            def _(o=o, v=v):
                o[...] += v

    spec = lambda b: pl.BlockSpec(*b)
    return pl.pallas_call(
        body, name="mla_heads_bwd", grid=(t // tm, nh),
        in_specs=[spec(hblk), spec(hblk), spec(tblk), spec(tblk), spec(tblk), spec(par), spec(par), spec(sw),
                  spec(hblk), spec(hblk), spec(hblk)],
        out_specs=[spec(hblk), spec(hblk), spec(tblk), spec(par), spec(par)],
        out_shape=[SDS((nh, t, HEAD_PAD), BF), SDS((nh, t, HEAD_PAD), BF), SDS((t, HEAD_PAD), F32),
                   SDS((1, HEAD_PAD), F32), SDS((1, HEAD_PAD), F32)],
        compiler_params=_cparams(2),
    )(q_raw, kv_raw, kr, cos, sin_signed, qn, kn, swap, dqp, dkp, dvp)


ATT_TILE = 512
ATT_SCALE = QK_DIM ** -0.5
LOG2E = 1.4426950408889634
LN2 = 0.6931471805599453
Q_PRESCALE = ATT_SCALE * LOG2E


def _flash_fwd(qs, k, v, nb, seq, hosts=None):
    nh, t, dh = qs.shape
    tq = _row_tile(seq, ATT_TILE)
    nq = seq // tq
    name = "mla_flash_fwd"
    comm = hosts.comm(name) if hosts is not None else None
    n_ci = len(comm.arrays) if comm is not None else 0
    n_co = len(comm.out_shapes) if comm is not None else 0

    def body(*refs):
        q_ref, k_ref, v_ref = refs[:3]
        ci_refs = refs[3:3 + n_ci]
        o_ref, lse_ref = refs[3 + n_ci:5 + n_ci]
        co_refs = refs[5 + n_ci:5 + n_ci + n_co]
        sem_refs = refs[5 + n_ci + n_co:]
        ids = (pl.program_id(0), pl.program_id(1), pl.program_id(2))
        if comm is not None:
            @pl.when((ids[0] == 0) & (ids[1] == 0) & (ids[2] == 0))
            def _():
                comm.start(ci_refs, co_refs, sem_refs)

        attend(q_ref, k_ref, v_ref, o_ref, lse_ref)

        if comm is not None:
            @pl.when((ids[0] == nh - 1) & (ids[1] == nb - 1) & (ids[2] == nq - 1))
            def _():
                comm.wait(ci_refs, co_refs, sem_refs)

    def attend(q_ref, k_ref, v_ref, o_ref, lse_ref):
        qi = pl.program_id(2)
        qt = q_ref[...]

        def tile(j, carry, diagonal):
            m, l, acc = carry
            rows = pl.ds(pl.multiple_of(j * tq, tq), tq)
            s = _dotf(qt, k_ref[rows, :], NT)
            if diagonal:
                r = lax.broadcasted_iota(jnp.int32, (tq, tq), 0)
                c = lax.broadcasted_iota(jnp.int32, (tq, tq), 1)
                s = jnp.where(c <= r, s, -jnp.inf)
            m_new = jnp.maximum(m, jnp.max(s, axis=-1, keepdims=True))
            alpha = jnp.exp2(m - m_new)
            p = jnp.exp2(s - m_new)
            return m_new, alpha * l + jnp.sum(p, axis=-1, keepdims=True), alpha * acc + _dotf(p, v_ref[rows, :], NN)

        init = (jnp.full((tq, 1), -jnp.inf, F32), jnp.zeros((tq, 1), F32), jnp.zeros((tq, dh), F32))
        carry = lax.fori_loop(0, qi, lambda j, c: tile(j, c, False), init)
        m, l, acc = tile(qi, carry, True)
        o_ref[...] = (acc / l).astype(o_ref.dtype)
        lse_ref[...] = m + jnp.log2(l)

    qblk = pl.BlockSpec((None, tq, dh), lambda h, b, i: (h, b * nq + i, 0))
    kblk = pl.BlockSpec((None, seq, dh), lambda h, b, i: (h, b, 0))
    hbm = pl.BlockSpec(memory_space=pl.ANY)
    res = pl.pallas_call(
        body, name=name, grid=(nh, nb, nq), in_specs=[qblk, kblk, kblk] + [hbm] * n_ci,
        out_specs=[qblk, pl.BlockSpec((None, tq, 1), lambda h, b, i: (h, b * nq + i, 0))] + [hbm] * n_co,
        out_shape=[SDS((nh, t, dh), BF), SDS((nh, t, 1), F32)] + (list(comm.out_shapes) if comm is not None else []),
        scratch_shapes=comm.sems if comm is not None else [],
        compiler_params=_cparams(3, VMEM_BIG),
    )(qs, k, v, *(comm.arrays if comm is not None else []))
    if comm is not None:
        hosts.done(name, res[2:])
    return res[0], res[1]


def _flash_bwd(qs, k, v, o, lse, do, nb, seq):
    nh, t, dh = qs.shape
    tq = _row_tile(seq, ATT_TILE)
    nq = seq // tq

    def row_of(col):
        return jnp.broadcast_to(col, (tq, LANES)).T[0:1, :]

    def body(q_ref, k_ref, v_ref, o_ref, lse_ref, do_ref, dq_ref, dk_ref, dv_ref, kt_sc, lrow_sc, drow_sc, dqt_sc):
        for c in range(nq):
            rows = pl.ds(c * tq, tq)
            kt_sc[c] = k_ref[rows, :].T
            delta = jnp.sum(do_ref[rows, :].astype(F32) * o_ref[rows, :].astype(F32), axis=-1, keepdims=True)
            drow_sc[c] = row_of(delta)
            lrow_sc[c] = row_of(lse_ref[rows, :])
        dqt_sc[...] = jnp.zeros_like(dqt_sc)

        def kv_step(j, _):
            rows_j = pl.ds(pl.multiple_of(j * tq, tq), tq)
            ks, vs, kt = k_ref[rows_j, :], v_ref[rows_j, :], kt_sc[j]

            def q_tile(i, carry, diagonal):
                dk, dv = carry
                rows_i = pl.ds(pl.multiple_of(i * tq, tq), tq)
                qt, dot_ = q_ref[rows_i, :], do_ref[rows_i, :]
                pt = jnp.exp2(_dotf(ks, qt, NT) - lrow_sc[i])
                if diagonal:
                    kk = lax.broadcasted_iota(jnp.int32, (tq, tq), 0)
                    qq = lax.broadcasted_iota(jnp.int32, (tq, tq), 1)
                    pt = jnp.where(kk <= qq, pt, 0.0)
                dst = (pt * (_dotf(vs, dot_, NT) - drow_sc[i])).astype(BF)
                dqt_sc[i] += _dotf(kt, dst, NN)
                return dk + _dotf(dst, qt, NN), dv + _dotf(pt, dot_, NN)

            zero = jnp.zeros((tq, dh), F32)
            carry = q_tile(j, (zero, zero), True)
            dk, dv = lax.fori_loop(j + 1, nq, lambda i, c: q_tile(i, c, False), carry)
            dk_ref[rows_j, :] = dk * LN2
            dv_ref[rows_j, :] = dv
            return 0

        lax.fori_loop(0, nq, kv_step, 0)
        for c in range(nq):
            dq_ref[pl.ds(c * tq, tq), :] = dqt_sc[c].T * ATT_SCALE

    full = pl.BlockSpec((None, seq, dh), lambda h, b: (h, b, 0))
    sfull = pl.BlockSpec((None, seq, 1), lambda h, b: (h, b, 0))
    return pl.pallas_call(
        body, name="mla_flash_bwd", grid=(nh, nb), in_specs=[full, full, full, full, sfull, full],
        out_specs=[full, full, full], out_shape=[SDS((nh, t, dh), F32)] * 3,
        scratch_shapes=[pltpu.VMEM((nq, dh, tq), BF), pltpu.VMEM((nq, 1, tq), F32), pltpu.VMEM((nq, 1, tq), F32),
                        pltpu.VMEM((nq, dh, tq), F32)],
        compiler_params=_cparams(2, VMEM_BIG),
    )(qs, k, v, o, lse, do)


def _heads_nt(name, a, wt, out_dtype):
    t, kdim = a.shape
    tm = _row_tile(t, 512)
    nw = MLA_HEADS * HEAD_PAD

    def body(a_ref, w_ref, o_ref):
        r = _dotf(a_ref[...], w_ref[...], NT)
        for h in range(MLA_HEADS):
            o_ref[h] = r[:, HEAD_PAD * h:HEAD_PAD * (h + 1)].astype(o_ref.dtype)

    return pl.pallas_call(
        body, name=name, grid=(t // tm,),
        in_specs=[pl.BlockSpec((tm, kdim), lambda i: (i, 0)), pl.BlockSpec((nw, kdim), lambda i: (0, 0))],
        out_specs=pl.BlockSpec((MLA_HEADS, tm, HEAD_PAD), lambda i: (0, i, 0)),
        out_shape=SDS((MLA_HEADS, t, HEAD_PAD), out_dtype), compiler_params=_cparams(1, VMEM_BIG),
    )(a, wt)


def _all_heads(a_ref):
    return jnp.concatenate([a_ref[h] for h in range(MLA_HEADS)], axis=1)


def _heads_nn(name, a, w, n, out_dtype, res=None):
    t = a.shape[1]
    tm = _row_tile(t, 512)
    nw = MLA_HEADS * HEAD_PAD

    def body(*refs):
        a_ref, w_ref, o_ref = refs[0], refs[1], refs[-1]
        r = _dotf(_all_heads(a_ref), w_ref[...], NN)
        if res is not None:
            r = r + refs[2][...]
        o_ref[...] = r.astype(o_ref.dtype)

    row = pl.BlockSpec((tm, n), lambda i: (i, 0))
    in_specs = [pl.BlockSpec((MLA_HEADS, tm, HEAD_PAD), lambda i: (0, i, 0)), pl.BlockSpec((nw, n), lambda i: (0, 0))]
    args = [a, w]
    if res is not None:
        in_specs.append(row)
        args.append(res)
    return pl.pallas_call(body, name=name, grid=(t // tm,), in_specs=in_specs, out_specs=row,
                          out_shape=SDS((t, n), out_dtype), compiler_params=_cparams(1, VMEM_BIG))(*args)


def _heads_wgrad(name, a, b, n):
    t = b.shape[0]
    tk = _row_tile(t, 512)
    nw = MLA_HEADS * HEAD_PAD
    steps = t // tk

    def body(a_ref, b_ref, o_ref, acc):
        k = pl.program_id(0)

        @pl.when(k == 0)
        def _():
            acc[...] = jnp.zeros_like(acc)

        acc[...] += _dotf(_all_heads(a_ref), b_ref[...], TN)

        @pl.when(k == steps - 1)
        def _():
            o_ref[...] = acc[...]

    return pl.pallas_call(
        body, name=name, grid=(steps,),
        in_specs=[pl.BlockSpec((MLA_HEADS, tk, HEAD_PAD), lambda k: (0, k, 0)), pl.BlockSpec((tk, n), lambda k: (k, 0))],
        out_specs=pl.BlockSpec((nw, n), lambda k: (0, 0)), out_shape=SDS((nw, n), F32),
        scratch_shapes=[pltpu.VMEM((nw, n), F32)], compiler_params=_cparams(1, VMEM_BIG),
    )(a, b)


PM_CKV, PM_KR, PM_CQ = 0, KV_LORA, KV_LORA + HEAD_PAD
PM_DIM = KV_LORA + HEAD_PAD + Q_LORA


def _lat_specs(t, tm):
    return (((tm, KV_LORA), lambda i: (i, 0)), ((tm, HEAD_PAD), lambda i: (i, PM_KR // HEAD_PAD)),
            ((tm, Q_LORA), lambda i: (i, PM_CQ // Q_LORA)))


def _mla_fwd(x, nw, wm, small, tables, nb, seq, hosts=None):
    t, d = x.shape
    cos, sin_signed, swap = tables
    h = _rms_fwd("mla_rms", x, nw)
    pm = _proj_nt("mla_in", h, wm["in_t"], PM_DIM, F32, tn=PM_DIM // 3)
    tm = _row_tile(t, 512)
    ckv_s, kr_s, cq_s = _lat_specs(t, tm)
    row = lambda i: (i, 0)
    par = lambda n: ((1, n), lambda i: (0, 0))
    ckvn = _ew("mla_ckv_norm", (t // tm,), _rms_fn, [(pm,) + ckv_s, (small["kv_a_norm"],) + par(KV_LORA)],
               [(SDS((t, KV_LORA), BF), (tm, KV_LORA), row, False)])[0]
    cqn = _ew("mla_cq_norm", (t // tm,), _rms_fn, [(pm,) + cq_s, (small["q_a_norm"],) + par(Q_LORA)],
              [(SDS((t, Q_LORA), BF), (tm, Q_LORA), row, False)])[0]
    tb = ((tm, HEAD_PAD), row)
    kr = _ew("mla_krope", (t // tm,), _krope_fn,
             [(pm,) + kr_s, (small["k_norm"],) + par(HEAD_PAD), (cos,) + tb, (sin_signed,) + tb,
              (swap, (HEAD_PAD, HEAD_PAD), lambda i: (0, 0))],
             [(SDS((t, HEAD_PAD), F32),) + tb + (False,)])[0]
    q_raw = _heads_nt("mla_q_b", cqn, wm["qb_t"], F32)
    kv_raw = _heads_nt("mla_kv_b", ckvn, wm["kvb_t"], F32)
    qp, kp, vp = _heads_fwd(q_raw, kv_raw, kr, cos, sin_signed, small["q_norm"], small["k_norm"], swap)
    o, lse = _flash_fwd(qp, kp, vp, nb, seq, hosts=hosts)
    out = _heads_nn("mla_out", o, wm["out"], d, F32, res=x)
    return out, (x, h, pm, ckvn, cqn, kr, q_raw, kv_raw, qp, kp, vp, o, lse)


def _mla_bwd(dy, nw, wm, small, tables, saved, nb, seq):
    x, h, pm, ckvn, cqn, kr, q_raw, kv_raw, qp, kp, vp, o, lse = saved
    t, d = x.shape
    cos, sin_signed, swap = tables
    do = _heads_nt("mla_bwd_do", dy, wm["out"], BF)
    g_out = _heads_wgrad("mla_w_out_g", o, dy, d)
    dqp, dkp, dvp = _flash_bwd(qp, kp, vp, o, lse, do, nb, seq)
    dq_raw, dkv_raw, dkr, d_qn, d_kn = _heads_bwd(q_raw, kv_raw, kr, cos, sin_signed, small["q_norm"],
                                                   small["k_norm"], swap, dqp, dkp, dvp)
    dcqn = _heads_nn("mla_bwd_dcq", dq_raw, wm["qb_t"], Q_LORA, F32)
    dckvn = _heads_nn("mla_bwd_dckv", dkv_raw, wm["kvb_t"], KV_LORA, F32)
    g_qb = _heads_wgrad("mla_w_qb_g", dq_raw, cqn, Q_LORA)
    g_kvb = _heads_wgrad("mla_w_kvb_g", dkv_raw, ckvn, KV_LORA)
    tm = _row_tile(t, 512)
    ckv_s, kr_s, cq_s = _lat_specs(t, tm)
    row = lambda i: (i, 0)
    par = lambda n: ((1, n), lambda i: (0, 0))

    def rms_b(xv, w, dv):
        _, vjp = jax.vjp(_rms_fn, xv, w)
        return vjp(dv)

    dckv, d_kva = _ew("mla_ckv_norm_bwd", (t // tm,), rms_b,
                      [(pm,) + ckv_s, (small["kv_a_norm"],) + par(KV_LORA), (dckvn, (tm, KV_LORA), row)],
                      [(SDS((t, KV_LORA), BF), (tm, KV_LORA), row, False),
                       (SDS((1, KV_LORA), F32),) + par(KV_LORA) + (True,)], acc_axes=(0,))
    dcq, d_qa = _ew("mla_cq_norm_bwd", (t // tm,), rms_b,
                    [(pm,) + cq_s, (small["q_a_norm"],) + par(Q_LORA), (dcqn, (tm, Q_LORA), row)],
                    [(SDS((t, Q_LORA), BF), (tm, Q_LORA), row, False),
                     (SDS((1, Q_LORA), F32),) + par(Q_LORA) + (True,)], acc_axes=(0,))
    tb = ((tm, HEAD_PAD), row)

    def kr_b(krv, w, cosv, sinv, sw, dv):
        _, vjp = jax.vjp(lambda a, b: _krope_fn(a, b, cosv, sinv, sw), krv, w)
        return vjp(dv)

    dkr_raw, d_kn2 = _ew("mla_krope_bwd", (t // tm,), kr_b,
                         [(pm,) + kr_s, (small["k_norm"],) + par(HEAD_PAD), (cos,) + tb, (sin_signed,) + tb,
                          (swap, (HEAD_PAD, HEAD_PAD), lambda i: (0, 0)), (dkr,) + tb],
                         [(SDS((t, HEAD_PAD), BF),) + tb + (False,),
                          (SDS((1, HEAD_PAD), F32),) + par(HEAD_PAD) + (True,)], acc_axes=(0,))
    dh = _seg_nn("mla_bwd_dh", [(dckv, wm["in_t"], PM_CKV), (dkr_raw, wm["in_t"], PM_KR),
                                (dcq, wm["in_t"], PM_CQ)], d, BF, tk=128)
    dx, dnw = _rms_bwd("mla_bwd_rms", x, nw, dh, dy)
    g = {
        "in_ckv_t": _wgrad("mla_w_in_ckv_g", dckv, h, KV_LORA, d, tm=KV_LORA, tn=d, out_dtype=F32),
        "in_kr_t": _wgrad("mla_w_in_kr_g", dkr_raw, h, HEAD_PAD, d, tm=HEAD_PAD, tn=d, out_dtype=F32),
        "in_cq_t": _wgrad("mla_w_in_cq_g", dcq, h, Q_LORA, d, tm=Q_LORA, tn=d, out_dtype=F32),
        "qb_t": g_qb, "kvb_t": g_kvb, "out": g_out,
        "q_a_norm": d_qa, "kv_a_norm": d_kva, "q_norm": d_qn, "k_norm": d_kn + d_kn2,
    }
    return dx, dnw, g


def _mesh_pos():
    return lax.axis_index("x"), lax.axis_index("y"), lax.axis_index("c")


def _peer(pos, k):
    x, y, c = pos
    return (x ^ ((k >> 2) & 1), y ^ ((k >> 1) & 1), c ^ (k & 1))


def _flat(pos):
    return 4 * pos[0] + 2 * pos[1] + pos[2]


def _slab(ref, axis, start, size):
    idx = [slice(None)] * axis + [pl.ds(start, size)]
    return ref.at[tuple(idx)]


class _Exchange:
    def __init__(self, kind, items):
        self.kind = kind
        self.axes = [ax for _, ax in items]
        self.arrays = [a for a, _ in items]
        n = len(items)
        self.out_shapes = []
        self.sizes = []
        for a, ax in items:
            shp = list(a.shape)
            if kind == "gather":
                self.sizes.append(shp[ax])
                shp[ax] *= N_DEV
                self.out_shapes.append(SDS(tuple(shp), a.dtype))
            else:
                shp[ax] //= N_DEV
                self.sizes.append(shp[ax])
                self.out_shapes.append(SDS((N_DEV,) + tuple(shp), a.dtype))
        self.sems = [pltpu.SemaphoreType.DMA((n, N_DEV - 1)), pltpu.SemaphoreType.DMA((n, N_DEV - 1)),
                     pltpu.SemaphoreType.DMA((n,))]

    def _copies(self, srcs, dsts, sems, with_arrivals=True):
        send_sems, recv_sems, local_sems = sems
        pos = _mesh_pos()
        me = _flat(pos)
        local, sends, recvs = [], [], []
        for t, (src, dst) in enumerate(zip(srcs, dsts)):
            ax, sz = self.axes[t], self.sizes[t]
            if self.kind == "gather":
                mine = _slab(dst, ax, me * sz, sz)
                local.append(pltpu.make_async_copy(src, mine, local_sems.at[t]))
            else:
                mine = dst.at[me]
                local.append(pltpu.make_async_copy(_slab(src, ax, me * sz, sz), mine, local_sems.at[t]))
            for k in range(1, N_DEV):
                peer = _peer(pos, k)
                there = _flat(peer)
                if self.kind == "gather":
                    out_src, landing = src, _slab(dst, ax, there * sz, sz)
                else:
                    out_src, landing = _slab(src, ax, there * sz, sz), dst.at[there]
                common = dict(send_sem=send_sems.at[t, k - 1], recv_sem=recv_sems.at[t, k - 1], device_id=peer,
                              device_id_type=pl.DeviceIdType.MESH)
                sends.append(pltpu.make_async_remote_copy(src_ref=out_src, dst_ref=mine, **common))
                if with_arrivals:
                    recvs.append(pltpu.make_async_remote_copy(src_ref=out_src, dst_ref=landing, **common))
        return local, sends, recvs

    def start(self, srcs, dsts, sems):
        local, sends, _ = self._copies(srcs, dsts, sems, with_arrivals=False)
        for cp in local + sends:
            cp.start()

    def wait(self, srcs, dsts, sems):
        local, sends, recvs = self._copies(srcs, dsts, sems)
        for rc in recvs:
            rc.wait_recv()
        for rc in sends:
            rc.wait_send()
        for cp in local:
            cp.wait()

    def run(self, name):
        n = len(self.arrays)

        def body(*refs):
            srcs, dsts, sems = refs[:n], refs[n:2 * n], refs[2 * n:]
            self.start(srcs, dsts, sems)
            self.wait(srcs, dsts, sems)

        hbm = pl.BlockSpec(memory_space=pl.ANY)
        return pl.pallas_call(body, name=name, in_specs=[hbm] * n, out_specs=[hbm] * n, out_shape=self.out_shapes,
                              scratch_shapes=self.sems)(*self.arrays)


def _adam_math(w, g, m, v):
    m = ADAM_B1 * m + (1.0 - ADAM_B1) * g
    v = ADAM_B2 * v + (1.0 - ADAM_B2) * (g * g)
    m_hat = m / (1.0 - ADAM_B1 ** ADAM_STEP)
    v_hat = v / (1.0 - ADAM_B2 ** ADAM_STEP)
    delta = -ADAM_LR * (m_hat / (jnp.sqrt(v_hat) + ADAM_EPS) + ADAM_WD * w)
    return delta, m, v


def _adam(name, land, land_blk, land_idx, w, m, v, transposed, ck):
    n, r, c = w.shape
    wblk = ((None, ck, c), lambda a, i: (a, i, 0))

    def fn(parts, w, m, v):
        g = parts[0].astype(F32)
        for s in range(1, N_DEV):
            g = g + parts[s].astype(F32)
        if transposed:
            g = g.T
        delta, m2, v2 = _adam_math(w, g, m, v)
        return g, delta, m2, v2

    return _ew(name, (n, r // ck), fn,
               [(land, land_blk, land_idx), (w,) + wblk, (m,) + wblk, (v,) + wblk],
               [(SDS(w.shape, F32),) + wblk + (False,)] * 4, vmem=VMEM_BIG)


def _prep_ffn(gate, up, down):
    def body(g, u, dn, o):
        o[0] = g[...].T.astype(BF)
        o[1] = u[...].T.astype(BF)
        o[2] = dn[...].astype(BF)

    cblk = pl.BlockSpec((None, None, D_MODEL, FF_SHARD), lambda l, i: (l, i, 0, 0))
    rblk = pl.BlockSpec((None, None, FF_SHARD, D_MODEL), lambda l, i: (l, i, 0, 0))
    return pl.pallas_call(
        body, name="prep_ffn", grid=(2, 2), in_specs=[cblk, cblk, rblk],
        out_specs=pl.BlockSpec((3, FF_SHARD, D_MODEL), lambda l, i: (2 * l + i, 0, 0)),
        out_shape=SDS((12, FF_SHARD, D_MODEL), BF), compiler_params=_cparams(2, VMEM_BIG),
    )(gate, up, down)


def _transpose_cast(name, w, dtype):
    def body(a, o):
        o[...] = a[...].T.astype(dtype)

    r, c = w.shape
    return pl.pallas_call(body, name=name, out_shape=SDS((c, r), dtype),
                          compiler_params=pltpu.CompilerParams(vmem_limit_bytes=VMEM_BIG))(w)


SMALL_SHARDED = (("norm_w", 6 * 128), ("conv_w", CONV_K * 512), ("q_a_norm", 48), ("kv_a_norm", 32))
SMALL_PACK = 3072


def _dyn(a, start, size):
    return lax.dynamic_slice_in_dim(a, start, size, axis=a.ndim - 1)


def _layout_ssm(ssm_in_t, ssm_out_all):
    d = ssm_in_t.shape[1]
    dt_rows = ssm_in_t[D_INNER + CONV_DIM:].reshape(SSM_GROUPS, SSM_HPG, d)
    return {"z_t": ssm_in_t[:D_INNER], "xbc_t": ssm_in_t[D_INNER:D_INNER + CONV_DIM],
            "dt_t": jnp.pad(dt_rows, ((0, 0), (0, LANES - SSM_HPG), (0, 0))).reshape(SSM_GROUPS * LANES, d),
            "out": ssm_out_all}


def _layout_mla(mla_in_all, qb_all, kvb_all, mla_out_all):
    d = mla_out_all.shape[1]
    in_t = mla_in_all.T
    kr_rows = jnp.pad(in_t[Q_LORA + KV_LORA:], ((QK_NOPE, HEAD_PAD - QK_DIM), (0, 0)))
    qb_heads = jnp.pad(qb_all.reshape(MLA_HEADS, QK_DIM, Q_LORA), ((0, 0), (0, HEAD_PAD - QK_DIM), (0, 0)))
    out_heads = jnp.pad(mla_out_all.reshape(MLA_HEADS, 64, d), ((0, 0), (64, 0), (0, 0)))
    return {"in_t": jnp.concatenate([in_t[Q_LORA:Q_LORA + KV_LORA], kr_rows, in_t[:Q_LORA]], axis=0),
            "qb_t": qb_heads.reshape(MLA_HEADS * HEAD_PAD, Q_LORA), "kvb_t": kvb_all,
            "out": out_heads.reshape(MLA_HEADS * HEAD_PAD, d)}


def _layout_small(conv_w, conv_b, dt_bias, a_log, d_skip, ssm_norm_w, q_a_norm, kv_a_norm, q_norm, k_norm):
    lane_heads = lambda p: jnp.pad(p.reshape(SSM_GROUPS, SSM_HPG), ((0, 0), (0, LANES - SSM_HPG))).reshape(1, -1)
    pad_head = lambda p: jnp.pad(p.reshape(1, QK_DIM), ((0, 0), (0, HEAD_PAD - QK_DIM)))
    return {"conv_w": conv_w, "conv_b": conv_b, "dt_bias": lane_heads(dt_bias), "a_log": lane_heads(a_log),
            "d_skip": lane_heads(d_skip), "ssm_norm_w": ssm_norm_w, "q_a_norm": q_a_norm, "kv_a_norm": kv_a_norm,
            "q_norm": pad_head(q_norm), "k_norm": pad_head(k_norm)}


class _Plan:
    def __init__(self, ctx):
        self.ctx = ctx
        self.make = {}
        self.land = {}

    def ride(self, host, make, land):
        assert host not in self.make, host
        self.make[host] = make
        self.land[host] = land

    def comm(self, host):
        return self.make[host](self.ctx) if host in self.make else None

    def done(self, host, results):
        self.land[host](results, self.ctx)


def _local_step(x, positions, loss_target, ctx, plan=None):
    nb, seq, d = x.shape
    t = nb * seq
    xf = x.reshape(t, d)
    norm = ctx["norm"]
    tables = list(_rope_tables(positions.reshape(t, 1))) + [_swap_matrix()]
    x1, s_f0 = _ffn_fwd("ffn0", xf, norm[0, 0], ctx["ffn0"], plan)
    x2, s_ssm = _ssm_fwd(x1, norm[0, 1], ctx["ws"], ctx["small"], nb, seq, plan)
    x3, s_f1 = _ffn_fwd("ffn1", x2, norm[0, 2], ctx["ffn1"], plan)
    x4, s_f2 = _ffn_fwd("ffn2", x3, norm[1, 0], ctx["ffn2"], plan)
    x5, s_mla = _mla_fwd(x4, norm[1, 1], ctx["wm"], ctx["small"], tables, nb, seq, plan)
    x6, s_f3 = _ffn_fwd("ffn3", x5, norm[1, 2], ctx["ffn3"], plan)
    dy, loss_cols = _loss_and_grad(x6, loss_target.reshape(t, d))

    dx5, dn12, ctx["g_ffn3"] = _ffn_bwd("ffn3", dy, norm[1, 2], ctx["ffn3"], s_f3, plan)
    dx4, dn11, ctx["g_mla"] = _mla_bwd(dx5, norm[1, 1], ctx["wm"], ctx["small"], tables, s_mla, nb, seq)
    dx3, dn10, ctx["g_ffn2"] = _ffn_bwd("ffn2", dx4, norm[1, 0], ctx["ffn2"], s_f2, plan)
    dx2, dn02, ctx["g_ffn1"] = _ffn_bwd("ffn1", dx3, norm[0, 2], ctx["ffn1"], s_f1, plan)
    dx1, dn01, ctx["g_ssm"] = _ssm_bwd(dx2, norm[0, 1], ctx["ws"], ctx["small"], s_ssm, nb, seq, plan)
    dx0, dn00, ctx["g_ffn0"] = _ffn_bwd("ffn0", dx1, norm[0, 0], ctx["ffn0"], s_f0, plan)
    return loss_cols, dx0.reshape(nb, seq, d), (dn00, dn01, dn02, dn10, dn11, dn12)


def kernel(x, positions, norm_w, ffn_w_gate, ffn_w_up, ffn_w_down, ssm_w_in, ssm_conv_w, ssm_conv_b, ssm_dt_bias, ssm_a_log, ssm_d, ssm_norm_w, ssm_w_out, mla_w_in, mla_q_a_norm, mla_kv_a_norm, mla_w_q_b, mla_w_kv_b, mla_q_norm, mla_k_norm, mla_w_out, loss_target, m_norm_w, m_ffn_w_gate, m_ffn_w_up, m_ffn_w_down, m_ssm_w_in, m_ssm_conv_w, m_ssm_conv_b, m_ssm_dt_bias, m_ssm_a_log, m_ssm_d, m_ssm_norm_w, m_ssm_w_out, m_mla_w_in, m_mla_q_a_norm, m_mla_kv_a_norm, m_mla_w_q_b, m_mla_w_kv_b, m_mla_q_norm, m_mla_k_norm, m_mla_w_out, v_norm_w, v_ffn_w_gate, v_ffn_w_up, v_ffn_w_down, v_ssm_w_in, v_ssm_conv_w, v_ssm_conv_b, v_ssm_dt_bias, v_ssm_a_log, v_ssm_d, v_ssm_norm_w, v_ssm_w_out, v_mla_w_in, v_mla_q_a_norm, v_mla_kv_a_norm, v_mla_w_q_b, v_mla_w_kv_b, v_mla_q_norm, v_mla_k_norm, v_mla_w_out):
    nb, seq, d = x.shape
    t = nb * seq
    me = _flat(_mesh_pos())

    ffn_loc = _prep_ffn(ffn_w_gate, ffn_w_up, ffn_w_down)
    ssm_in_loc = _transpose_cast("prep_ssm_in", ssm_w_in[0], BF).reshape(SSM_IN_SHARD // 2, 16, LANES)
    ssm_out_loc = ssm_w_out[0].astype(BF)
    mla_in_loc, mla_out_loc = mla_w_in[0].astype(BF), mla_w_out[0].astype(BF)
    qb_loc = _transpose_cast("prep_q_b", mla_w_q_b[0], BF)
    kvb_loc = _transpose_cast("prep_kv_b", mla_w_kv_b[0], BF)
    small_loc = jnp.concatenate([norm_w.reshape(-1), ssm_conv_w.reshape(-1), mla_q_a_norm.reshape(-1),
                                 mla_kv_a_norm.reshape(-1)])
    small_loc = jnp.pad(small_loc, (0, SMALL_PACK - small_loc.shape[0])).reshape(SMALL_PACK // LANES, LANES)

    wloc = lambda n: [(ffn_loc[3 * n + k], 0) for k in range(3)]
    g0, u0, d0, small_all = _Exchange("gather", wloc(0) + [(small_loc, 0)]).run("gather_first")
    sm = small_all.reshape(N_DEV, SMALL_PACK)
    conv_w_full = sm[:, 768:768 + 2048].reshape(N_DEV, CONV_K, 512).transpose(1, 0, 2).reshape(CONV_K, CONV_DIM)
    ctx = {"ffn0": (g0, u0, d0), "ffn1": [None] * 3, "ffn2": [None] * 3,
           "norm": sm[:, :768].reshape(N_DEV, 6, 128).transpose(1, 0, 2).reshape(2, 3, 1, d),
           "small": _layout_small(conv_w_full, ssm_conv_b, ssm_dt_bias, ssm_a_log, ssm_d, ssm_norm_w,
                                  sm[:, 2816:2864].reshape(1, Q_LORA), sm[:, 2864:2896].reshape(1, KV_LORA),
                                  mla_q_norm, mla_k_norm)}
    plan = _Plan(ctx)

    def gather_on(host, items, land):
        plan.ride(host, lambda c: _Exchange("gather", items), land)

    def put_w(key, ks):
        def land(r, c):
            for k, arr in zip(ks, r):
                c[key][k] = arr
        return land

    gather_on("ffn0_up", [(ssm_in_loc, 0)], lambda r, c: c.update(ssm_in_t=r[0].reshape(SSM_IN_DIM, d)))
    gather_on("ffn0_down", [(ssm_out_loc, 0)], lambda r, c: c.update(ws=_layout_ssm(c["ssm_in_t"], r[0])))
    gather_on("ssm_in_z", wloc(1)[0:1], put_w("ffn1", (0,)))
    gather_on("ssm_in_xbc", wloc(1)[1:2], put_w("ffn1", (1,)))
    gather_on("ssm_out", wloc(1)[2:3], put_w("ffn1", (2,)))
    gather_on("ffn1_up", wloc(2)[0:2], put_w("ffn2", (0, 1)))
    gather_on("ffn1_down", wloc(2)[2:3], put_w("ffn2", (2,)))
    gather_on("ffn2_up", [(mla_in_loc, 0), (qb_loc, 0), (kvb_loc, 0), (mla_out_loc, 0)],
              lambda r, c: c.update(wm=_layout_mla(r[0], r[1], r[2], r[3])))
    gather_on("mla_flash_fwd", wloc(3), lambda r, c: c.update(ffn3=tuple(r)))

    heads_of = lambda a: a.reshape(SSM_GROUPS, LANES, -1)[:, :SSM_HPG].reshape(SSM_HEADS, -1)

    def mla_grad_items(c):
        g = c["g_mla"]
        g_in = jnp.concatenate([g["in_cq_t"], g["in_ckv_t"], g["in_kr_t"][QK_NOPE:QK_DIM]], axis=0).T
        g_qb = g["qb_t"].reshape(MLA_HEADS, HEAD_PAD, Q_LORA)[:, :QK_DIM].reshape(MLA_HEADS * QK_DIM, Q_LORA)
        g_out = g["out"].reshape(MLA_HEADS, HEAD_PAD, d)[:, 64:].reshape(MLA_HEADS * 64, d)
        return [(a.astype(BF), 0) for a in (g_out, g_in, g_qb, g["kvb_t"])]

    def ssm_in_grad(c):
        g = c["g_ssm"]
        g_in_t = jnp.concatenate([g["z_t"], g["x_t"], g["b_t"], g["c_t"], heads_of(g["dt_t"])], axis=0)
        return [(g_in_t.astype(BF).reshape(SSM_IN_DIM // 2, 16, LANES), 0)]

    def scatter_on(host, items_of, keys):
        plan.ride(host, lambda c: _Exchange("scatter", items_of(c)),
                  lambda r, c: c.update(dict(zip(keys, r))))

    of = lambda key, ks: (lambda c: [(c[key][k], 0) for k in ks])
    scatter_on("ffn2_bwd_act", of("g_ffn3", (0, 1)), ("l3_gate", "l3_up"))
    scatter_on("ffn2_bwd_dh", of("g_ffn3", (2,)), ("l3_down",))
    scatter_on("ffn2_wg", mla_grad_items, ("l_mla_out", "l_mla_in", "l_qb", "l_kvb"))
    scatter_on("ffn1_bwd_act", of("g_ffn2", (0, 1)), ("l2_gate", "l2_up"))
    scatter_on("ffn1_bwd_dh", of("g_ffn2", (2,)), ("l2_down",))
    scatter_on("ssm_bwd_dh", of("g_ffn1", (0, 1, 2)), ("l1_gate", "l1_up", "l1_down"))
    scatter_on("ffn0_bwd_act", ssm_in_grad, ("l_ssm_in",))
    scatter_on("ffn0_bwd_dh", lambda c: [(c["g_ssm"]["out"], 0)], ("l_ssm_out",))
    scatter_on("ffn0_wu", lambda c: [(c["g_ffn0_gate"], 0)], ("l0_gate",))
    scatter_on("ffn0_wd", lambda c: [(c["g_ffn0_up"], 0)], ("l0_up",))

    loss_cols, grad_x, dns = _local_step(x, positions, loss_target, ctx, plan)
    loss = lax.psum(jnp.sum(loss_cols), ("x", "y", "c"))
    dn00, dn01, dn02, dn10, dn11, dn12 = dns
    g_ssm, g_mla = ctx["g_ssm"], ctx["g_mla"]
    ctx["l0_down"] = _Exchange("scatter", [(ctx["g_ffn0"][2], 0)]).run("scatter_last")[0]
    l_ffn = {k: jnp.stack([ctx["l%d_%s" % (n, k)] for n in range(4)], axis=1) for k in ("gate", "up", "down")}
    l_mla_out, l_mla_in, l_qb, l_kvb = (ctx[k] for k in ("l_mla_out", "l_mla_in", "l_qb", "l_kvb"))
    l_ssm_in, l_ssm_out = ctx["l_ssm_in"], ctx["l_ssm_out"]

    unlane = lambda a: a.reshape(SSM_GROUPS, LANES)[:, :SSM_HPG].reshape(1, SSM_HEADS)
    small_g = jnp.concatenate([
        jnp.concatenate([dn00, dn01, dn02, dn10, dn11, dn12], axis=0).reshape(-1),
        g_ssm["conv_w"].reshape(-1), g_ssm["conv_b"].reshape(-1), unlane(g_ssm["dt_bias"]).reshape(-1),
        unlane(g_ssm["a_log"]).reshape(-1), unlane(g_ssm["d_skip"]).reshape(-1), g_ssm["ssm_norm_w"].reshape(-1),
        g_mla["q_a_norm"].reshape(-1), g_mla["kv_a_norm"].reshape(-1), g_mla["q_norm"][0, :QK_DIM],
        g_mla["k_norm"][0, :QK_DIM]])
    n_small = small_g.shape[0]
    n_small_pad = -(-n_small // (8 * LANES)) * (8 * LANES)
    small_g = jnp.pad(small_g, (0, n_small_pad - n_small)).reshape(n_small_pad // LANES, LANES)
    gs = _Exchange("gather", [(small_g, 0)]).run("gather_small_grads")[0].reshape(N_DEV, n_small_pad)

    outs = {}

    def put(name, res, shape):
        for key, val in zip(("grad", "delta", "new_m", "new_v"), res):
            outs[(key, name)] = val.reshape(shape)

    ck = 256
    for key, name, w, m, v in (("gate", "ffn_w_gate", ffn_w_gate, m_ffn_w_gate, v_ffn_w_gate),
                               ("up", "ffn_w_up", ffn_w_up, m_ffn_w_up, v_ffn_w_up)):
        res = _adam("adam_" + name, l_ffn[key], (N_DEV, None, FF_SHARD, ck), lambda a, i: (0, a, 0, i),
                    w.reshape(4, d, FF_SHARD), m.reshape(4, d, FF_SHARD), v.reshape(4, d, FF_SHARD), True, ck)
        put(name, res, w.shape)
    res = _adam("adam_ffn_w_down", l_ffn["down"], (N_DEV, None, 176, d), lambda a, i: (0, a, i, 0),
                ffn_w_down.reshape(4, FF_SHARD, d), m_ffn_w_down.reshape(4, FF_SHARD, d),
                v_ffn_w_down.reshape(4, FF_SHARD, d), False, 176)
    put("ffn_w_down", res, ffn_w_down.shape)
    l_ssm_in2 = l_ssm_in.reshape(N_DEV, SSM_IN_SHARD, d)
    res = _adam("adam_ssm_w_in", l_ssm_in2, (N_DEV, SSM_IN_SHARD, 128), lambda a, i: (0, 0, i),
                ssm_w_in, m_ssm_w_in, v_ssm_w_in, True, 128)
    put("ssm_w_in", res, ssm_w_in.shape)
    res = _adam("adam_ssm_w_out", l_ssm_out, (N_DEV, 128, d), lambda a, i: (0, i, 0),
                ssm_w_out, m_ssm_w_out, v_ssm_w_out, False, 128)
    put("ssm_w_out", res, ssm_w_out.shape)
    res = _adam("adam_mla_w_in", l_mla_in, (N_DEV, 128, MLA_IN_DIM), lambda a, i: (0, 0, 0),
                mla_w_in, m_mla_w_in, v_mla_w_in, False, 128)
    put("mla_w_in", res, mla_w_in.shape)
    res = _adam("adam_mla_w_q_b", l_qb, (N_DEV, 192, 128), lambda a, i: (0, 0, i),
                mla_w_q_b, m_mla_w_q_b, v_mla_w_q_b, True, 128)
    put("mla_w_q_b", res, mla_w_q_b.shape)
    res = _adam("adam_mla_w_kv_b", l_kvb, (N_DEV, 256, 128), lambda a, i: (0, 0, i),
                mla_w_kv_b, m_mla_w_kv_b, v_mla_w_kv_b, True, 128)
    put("mla_w_kv_b", res, mla_w_kv_b.shape)
    res = _adam("adam_mla_w_out", l_mla_out, (N_DEV, 128, d), lambda a, i: (0, 0, 0),
                mla_w_out, m_mla_w_out, v_mla_w_out, False, 128)
    put("mla_w_out", res, mla_w_out.shape)

    small_params = (
        ("norm_w", norm_w, m_norm_w, v_norm_w, 6 * d, 6, 128), ("ssm_conv_w", ssm_conv_w, m_ssm_conv_w, v_ssm_conv_w,
                                                               CONV_K * CONV_DIM, CONV_K, 512),
        ("ssm_conv_b", ssm_conv_b, m_ssm_conv_b, v_ssm_conv_b, CONV_DIM, 0, 0),
        ("ssm_dt_bias", ssm_dt_bias, m_ssm_dt_bias, v_ssm_dt_bias, SSM_HEADS, 0, 0),
        ("ssm_a_log", ssm_a_log, m_ssm_a_log, v_ssm_a_log, SSM_HEADS, 0, 0),
        ("ssm_d", ssm_d, m_ssm_d, v_ssm_d, SSM_HEADS, 0, 0),
        ("ssm_norm_w", ssm_norm_w, m_ssm_norm_w, v_ssm_norm_w, D_INNER, 0, 0),
        ("mla_q_a_norm", mla_q_a_norm, m_mla_q_a_norm, v_mla_q_a_norm, Q_LORA, 1, 48),
        ("mla_kv_a_norm", mla_kv_a_norm, m_mla_kv_a_norm, v_mla_kv_a_norm, KV_LORA, 1, 32),
        ("mla_q_norm", mla_q_norm, m_mla_q_norm, v_mla_q_norm, QK_DIM, 0, 0),
        ("mla_k_norm", mla_k_norm, m_mla_k_norm, v_mla_k_norm, QK_DIM, 0, 0),
    )
    parts, ws_, ms_, vs_, off = [], [], [], [], 0
    for name, w, m, v, full, rows, shard in small_params:
        seg = gs[:, off:off + full]
        if rows:
            seg = _dyn(seg.reshape(N_DEV, rows, full // rows), me * shard, shard).reshape(N_DEV, rows * shard)
        parts.append(seg)
        ws_.append(w.reshape(1, -1))
        ms_.append(m.reshape(1, -1))
        vs_.append(v.reshape(1, -1))
        off += full
    n_loc = sum(p.shape[1] for p in parts)
    n_loc_pad = -(-n_loc // LANES) * LANES
    padc = lambda a, val=0.0: jnp.pad(jnp.concatenate(a, axis=1), ((0, 0), (0, n_loc_pad - n_loc)),
                                      constant_values=val)
    res = _adam("adam_small", padc(parts).reshape(N_DEV, 1, n_loc_pad), (N_DEV, 1, n_loc_pad), lambda a, i: (0, 0, 0),
                padc(ws_).reshape(1, 1, n_loc_pad), padc(ms_).reshape(1, 1, n_loc_pad),
                padc(vs_, 1.0).reshape(1, 1, n_loc_pad), False, 1)
    off = 0
    for name, w, m, v, full, rows, shard in small_params:
        nloc = w.size
        put(name, [r.reshape(-1)[off:off + nloc] for r in res], w.shape)
        off += nloc

    order = ("norm_w", "ffn_w_gate", "ffn_w_up", "ffn_w_down", "ssm_w_in", "ssm_conv_w", "ssm_conv_b", "ssm_dt_bias",
             "ssm_a_log", "ssm_d", "ssm_norm_w", "ssm_w_out", "mla_w_in", "mla_q_a_norm", "mla_kv_a_norm",
             "mla_w_q_b", "mla_w_kv_b", "mla_q_norm", "mla_k_norm", "mla_w_out")
    return (loss, grad_x, *[outs[(k, n)] for k in ("grad", "delta", "new_m", "new_v") for n in order])
```

```python
import functools
import math

import jax
import jax.numpy as jnp
import numpy as np
from jax import lax
from jax.experimental import pallas as pl
from jax.experimental.pallas import tpu as pltpu

F32 = jnp.float32
BF = jnp.bfloat16
SDS = jax.ShapeDtypeStruct

N_DEV = 8
D_MODEL = 1024
D_FF = 2816
FF_SHARD = D_FF // N_DEV
D_INNER = 2048
SSM_HEADS = 32
SSM_GROUPS = 8
SSM_HPG = 4
SSM_STATE = 128
CONV_K = 4
CONV_DIM = 4096
SSM_IN_DIM = 6176
SSM_IN_SHARD = SSM_IN_DIM // N_DEV
NORM_GROUP = 256
CHUNK = 128
MLA_HEADS = 16
Q_LORA = 384
KV_LORA = 256
QK_NOPE = 64
QK_ROPE = 32
QK_DIM = 96
MLA_IN_DIM = 672
HEAD_PAD = 128
ROPE_THETA = 10000.0
EPS = 1e-6
LANES = 128

ADAM_LR = 0.001
ADAM_B1 = 0.9
ADAM_B2 = 0.999
ADAM_EPS = 1e-08
ADAM_WD = 0.01
ADAM_STEP = 10

VMEM_BIG = 56 * 1024 * 1024

NN = ((1,), (0,))
NT = ((1,), (1,))
TN = ((0,), (0,))


def _dotf(a, b, dn):
    return lax.dot_general(a.astype(BF), b.astype(BF), (dn, ((), ())), preferred_element_type=F32)


def _dot_hi(a, b, dn=NN):
    return lax.dot_general(a, b, (dn, ((), ())), precision=lax.Precision.HIGHEST, preferred_element_type=F32)


def _sigmoid(x):
    return jax.nn.sigmoid(x)


def _silu(x):
    return x * _sigmoid(x)


def _softplus(x):
    return jnp.maximum(x, 0.0) + jnp.log(1.0 + jnp.exp(-jnp.abs(x)))


def _cparams(n_grid, vmem=None):
    return pltpu.CompilerParams(dimension_semantics=("arbitrary",) * n_grid, vmem_limit_bytes=vmem)


def _fmm(name, grid_mn, pairs, outs, *, epi=None, extras=(), n_acc=1, acc_shape=None, vmem=None, alias=None,
         joint=False, hosts=None, row_split=1):
    comm = hosts.comm(name) if hosts is not None else None
    if joint:
        nk_total = pairs[0][7]
        assert all(p[7] == nk_total for p in pairs)
        starts = [0] * len(pairs)
    else:
        nk_total = sum(p[7] for p in pairs)
        starts = []
        s = 0
        for p in pairs:
            starts.append(s)
            s += p[7]
    n_pairs, n_extras, n_outs = len(pairs), len(extras), len(outs)
    single = nk_total == 1

    n_ci = len(comm.arrays) if comm is not None else 0
    n_co = len(comm.out_shapes) if comm is not None else 0
    n_scratch_acc = 0 if single else n_acc

    def body(*refs):
        ab_refs = refs[: 2 * n_pairs]
        e_refs = refs[2 * n_pairs: 2 * n_pairs + n_extras]
        pos = 2 * n_pairs + n_extras + (1 if alias is not None else 0)
        ci_refs = refs[pos: pos + n_ci]
        pos += n_ci
        o_refs = refs[pos: pos + n_outs]
        co_refs = refs[pos + n_outs: pos + n_outs + n_co]
        pos += n_outs + n_co
        acc_refs = refs[pos: pos + n_scratch_acc]
        sem_refs = refs[pos + n_scratch_acc:]
        i, j, k = pl.program_id(0), pl.program_id(1), pl.program_id(2)

        if comm is not None:
            @pl.when((i == 0) & (j == 0) & (k == 0))
            def _():
                comm.start(ci_refs, co_refs, sem_refs)

        compute(ab_refs, e_refs, o_refs, acc_refs, i, j, k)

        if comm is not None:
            @pl.when((i == grid_mn[0] - 1) & (j == grid_mn[1] - 1) & (k == nk_total - 1))
            def _():
                comm.wait(ci_refs, co_refs, sem_refs)

    def compute(ab_refs, e_refs, o_refs, acc_refs, i, j, k):

        def finish(accs, rows=slice(None)):
            res = epi(accs, *[e[rows] for e in e_refs]) if epi is not None else accs
            if not isinstance(res, (tuple, list)):
                res = (res,)
            first = (i == 0) & (j == 0)
            for o, r, spec in zip(o_refs, res, outs):
                if spec[3]:
                    @pl.when(first)
                    def _(o=o, r=r):
                        o[...] = r.astype(o.dtype)

                    @pl.when(jnp.logical_not(first))
                    def _(o=o, r=r):
                        o[...] += r.astype(o.dtype)
                else:
                    o[rows] = r.astype(o.dtype)

        if single:
            tm_all = ab_refs[0].shape[0]
            ch = tm_all // row_split
            for c in range(row_split):
                rows = slice(c * ch, (c + 1) * ch) if row_split > 1 else slice(None)
                accs = [None] * n_acc
                for p, pr in enumerate(pairs):
                    d = _dotf(ab_refs[2 * p][rows], ab_refs[2 * p + 1][...], pr[6])
                    accs[pr[8]] = d if accs[pr[8]] is None else accs[pr[8]] + d
                finish(accs, rows)
            return

        @pl.when(k == 0)
        def _():
            for a in acc_refs:
                a[...] = jnp.zeros_like(a)

        for p, pr in enumerate(pairs):
            def step(p=p, pr=pr):
                acc_refs[pr[8]][...] += _dotf(ab_refs[2 * p][...], ab_refs[2 * p + 1][...], pr[6])

            if n_pairs == 1 or joint:
                step()
            else:
                pl.when((k >= starts[p]) & (k < starts[p] + pr[7]))(step)

        @pl.when(k == nk_total - 1)
        def _():
            finish([a[...] for a in acc_refs])

    in_specs, args = [], []
    for p, pr in enumerate(pairs):
        a, a_blk, a_idx, b, b_blk, b_idx, _, nk, _ = pr
        st = starts[p]

        def amap(i, j, k, a_idx=a_idx, st=st, nk=nk):
            return a_idx(i, j, jnp.clip(k - st, 0, nk - 1))

        def bmap(i, j, k, b_idx=b_idx, st=st, nk=nk):
            return b_idx(i, j, jnp.clip(k - st, 0, nk - 1))

        in_specs += [pl.BlockSpec(a_blk, amap), pl.BlockSpec(b_blk, bmap)]
        args += [a, b]
    for arr, blk, idx in extras:
        in_specs.append(pl.BlockSpec(blk, lambda i, j, k, idx=idx: idx(i, j)))
        args.append(arr)
    io_alias = {}
    if alias is not None:
        in_specs.append(pl.BlockSpec(memory_space=pl.ANY))
        io_alias = {len(args): 0}
        args.append(alias)
    out_specs = [pl.BlockSpec(blk, lambda i, j, k, idx=idx: idx(i, j)) for _, blk, idx, _ in outs]
    out_shape = [o[0] for o in outs]
    scratch = [] if single else [pltpu.VMEM(acc_shape, F32) for _ in range(n_acc)]
    if comm is not None:
        hbm = pl.BlockSpec(memory_space=pl.ANY)
        in_specs += [hbm] * n_ci
        args += list(comm.arrays)
        out_specs += [hbm] * n_co
        out_shape += list(comm.out_shapes)
        scratch += comm.sems
    res = pl.pallas_call(
        body, name=name, grid=(grid_mn[0], grid_mn[1], nk_total), in_specs=in_specs, out_specs=out_specs,
        out_shape=out_shape, scratch_shapes=scratch, input_output_aliases=io_alias,
        compiler_params=_cparams(3, vmem),
    )(*args)
    if comm is not None:
        hosts.done(name, res[n_outs:])
    return res[:n_outs]


def _ew(name, grid, fn, ins, outs, *, acc_axes=(), vmem=None):
    n_in = len(ins)

    def body(*refs):
        res = fn(*[r[...] for r in refs[:n_in]])
        if not isinstance(res, (tuple, list)):
            res = (res,)
        first = None
        for ax in acc_axes:
            c = pl.program_id(ax) == 0
            first = c if first is None else (first & c)
        for o, r, spec in zip(refs[n_in:], res, outs):
            if spec[3]:
                @pl.when(first)
                def _(o=o, r=r):
                    o[...] = r.astype(o.dtype)

                @pl.when(jnp.logical_not(first))
                def _(o=o, r=r):
                    o[...] += r.astype(o.dtype)
            else:
                o[...] = r.astype(o.dtype)

    return pl.pallas_call(
        body, name=name, grid=grid,
        in_specs=[pl.BlockSpec(blk, idx) for _, blk, idx in ins],
        out_specs=[pl.BlockSpec(blk, idx) for _, blk, idx, _ in outs],
        out_shape=[o[0] for o in outs],
        compiler_params=_cparams(len(grid), vmem),
    )(*[a for a, _, _ in ins])


def _row_tile(t, want):
    tm = min(want, t)
    assert t % tm == 0, (t, tm)
    return tm


def _rms_fn(x, w):
    return x * lax.rsqrt(jnp.mean(x * x, axis=-1, keepdims=True) + EPS) * w


def _rms_fwd(name, x, w):
    t, d = x.shape
    tm = _row_tile(t, 512)
    return _ew(name, (t // tm,), _rms_fn,
               [(x, (tm, d), lambda i: (i, 0)), (w, (1, d), lambda i: (0, 0))],
               [(SDS((t, d), BF), (tm, d), lambda i: (i, 0), False)])[0]


def _rms_bwd(name, x, w, dh, dres):
    t, d = x.shape
    tm = _row_tile(t, 512)

    def fn(x, w, dh, dres):
        _, vjp = jax.vjp(_rms_fn, x, w)
        dx, dw = vjp(dh.astype(F32))
        return dx + dres, dw

    row = lambda i: (i, 0)
    return _ew(name, (t // tm,), fn,
               [(x, (tm, d), row), (w, (1, d), lambda i: (0, 0)), (dh, (tm, d), row), (dres, (tm, d), row)],
               [(SDS((t, d), F32), (tm, d), row, False), (SDS((1, d), F32), (1, d), lambda i: (0, 0), True)],
               acc_axes=(0,))


def _loss_and_grad(y, target):
    t, d = y.shape
    tm = _row_tile(t, 512)

    def fn(y, tg):
        e = y - tg
        return e * (1.0 / d), jnp.sum(e * e, axis=0, keepdims=True) * (0.5 / d)

    row = lambda i: (i, 0)
    return _ew("loss_head", (t // tm,), fn, [(y, (tm, d), row), (target, (tm, d), row)],
               [(SDS((t, d), F32), (tm, d), row, False), (SDS((1, d), F32), (1, d), lambda i: (0, 0), True)],
               acc_axes=(0,))


def _ffn_fwd(tag, x, nw, wf, hosts=None):
    gate_t, up_t = wf[0], wf[1]
    t, d = x.shape
    h = _rms_fwd(tag + "_rms", x, nw)
    tm, tn = _row_tile(t, 256), D_FF

    def epi(accs):
        g, u = accs
        s = _sigmoid(g)
        sg = g * s
        return 0.5 * sg, 0.5 * (s * (1.0 + g * (1.0 - s))), u, sg * u

    hblk = (h, (tm, d), lambda i, j, k: (i, 0))
    col = lambda i, j: (i, j)
    tblk = lambda w: (w, (tn, d), lambda i, j, k: (j, 0), NT, 1)
    sgh, dsgh, u, a = _fmm(
        tag + "_up", (t // tm, D_FF // tn), [hblk + tblk(gate_t) + (0,), hblk + tblk(up_t) + (1,)],
        [(SDS((t, D_FF), BF), (tm, tn), col, False)] * 4, epi=epi, n_acc=2, joint=True, hosts=hosts, vmem=VMEM_BIG)
    down = wf[2]
    tm2 = _row_tile(t, 512)
    y = _fmm(
        tag + "_down", (t // tm2, 1),
        [(a, (tm2, D_FF), lambda i, j, k: (i, 0), down, (D_FF, d), lambda i, j, k: (0, 0), NN, 1, 0)],
        [(SDS((t, d), F32), (tm2, d), lambda i, j: (i, 0), False)],
        extras=[(x, (tm2, d), lambda i, j: (i, 0))],
        epi=lambda accs, xr: xr + 0.5 * accs[0], vmem=VMEM_BIG, hosts=hosts)[0]
    return y, (x, h, sgh, dsgh, u, a)


def _wgrad(name, a, b, m, n, *, tm, tn, tk=512, scale=None, out_dtype=BF, hosts=None):
    t = a.shape[0]
    tk = _row_tile(t, tk)
    epi = (lambda accs: accs[0] * scale) if scale is not None else None
    return _fmm(name, (m // tm, n // tn),
                [(a, (tk, tm), lambda i, j, k: (k, i), b, (tk, tn), lambda i, j, k: (k, j), TN, t // tk, 0)],
                [(SDS((m, n), out_dtype), (tm, tn), lambda i, j: (i, j), False)], epi=epi, acc_shape=(tm, tn),
                vmem=VMEM_BIG, hosts=hosts)[0]


def _ffn_bwd(tag, dy, nw, wf, saved, hosts=None):
    gate_t, up_t, down = wf
    x, h, sgh, dsgh, u, a = saved
    t, d = x.shape
    tm, tn = _row_tile(t, 256), D_FF

    def epi(accs, sgh, dsgh, u):
        da = accs[0]
        return da * (u.astype(F32) * dsgh.astype(F32)), da * sgh.astype(F32)

    col = lambda i, j: (i, j)
    dg, du = _fmm(
        tag + "_bwd_act", (t // tm, D_FF // tn),
        [(dy, (tm, d), lambda i, j, k: (i, 0), down, (tn, d), lambda i, j, k: (j, 0), NT, 1, 0)],
        [(SDS((t, D_FF), BF), (tm, tn), col, False)] * 2,
        extras=[(sgh, (tm, tn), col), (dsgh, (tm, tn), col), (u, (tm, tn), col)], epi=epi, hosts=hosts,
        vmem=VMEM_BIG)
    tm2 = _row_tile(t, 256)
    full = lambda i, j, k: (0, 0)
    dh = _fmm(
        tag + "_bwd_dh", (t // tm2, 1),
        [(dg, (tm2, D_FF), lambda i, j, k: (i, 0), gate_t, (D_FF, d), full, NN, 1, 0),
         (du, (tm2, D_FF), lambda i, j, k: (i, 0), up_t, (D_FF, d), full, NN, 1, 0)],
        [(SDS((t, d), BF), (tm2, d), lambda i, j: (i, 0), False)], vmem=VMEM_BIG, joint=True, hosts=hosts)[0]
    dx, dnw = _rms_bwd(tag + "_bwd_rms", x, nw, dh, dy)
    half = D_FF // 2
    g_gate = _wgrad(tag + "_wg", dg, h, D_FF, d, tm=half, tn=d, hosts=hosts)
    if hosts is not None:
        hosts.ctx["g_" + tag + "_gate"] = g_gate
    g_up = _wgrad(tag + "_wu", du, h, D_FF, d, tm=half, tn=d, hosts=hosts)
    if hosts is not None:
        hosts.ctx["g_" + tag + "_up"] = g_up
    g_down = _wgrad(tag + "_wd", a, dy, D_FF, d, tm=half, tn=d, scale=0.5, hosts=hosts)
    return dx, dnw, (g_gate, g_up, g_down)


def _shift_down(cur, prev8, j):
    rolled = pltpu.roll(cur, j, 0)
    sub = lax.broadcasted_iota(jnp.int32, prev8.shape, 0)
    top = jnp.where(sub < j, pltpu.roll(prev8, j, 0), rolled[:8])
    return jnp.concatenate([top, rolled[8:]], axis=0)


def _shift_up(cur, next8, j):
    n = cur.shape[0]
    rolled = pltpu.roll(cur, n - j, 0)
    sub = lax.broadcasted_iota(jnp.int32, next8.shape, 0)
    bot = jnp.where(sub >= 8 - j, pltpu.roll(next8, 8 - j, 0), rolled[n - 8:])
    return jnp.concatenate([rolled[: n - 8], bot], axis=0)


HALO = 16


def _conv_fwd(xbc, w, b, seq):
    t, c = xbc.shape
    ts, tc = _row_tile(seq, 256), 512
    tiles_per_seq = seq // ts
    hb = ts // HALO

    def fn(cur, prev, w, b):
        i = pl.program_id(1)
        cur = cur.astype(F32)
        prev8 = jnp.where(i % tiles_per_seq == 0, 0.0, prev.astype(F32)[HALO - 8:])
        out = b + w[3:4] * cur
        for j in range(1, CONV_K):
            out = out + w[3 - j:4 - j] * _shift_down(cur, prev8, j)
        return out, _silu(out)

    return _ew("ssm_conv_fwd", (c // tc, t // ts), fn,
               [(xbc, (ts, tc), lambda j, i: (i, j)),
                (xbc, (HALO, tc), lambda j, i: (jnp.maximum(i * hb - 1, 0), j)),
                (w, (CONV_K, tc), lambda j, i: (0, j)), (b, (1, tc), lambda j, i: (0, j))],
               [(SDS((t, c), BF), (ts, tc), lambda j, i: (i, j), False)] * 2)


def _conv_bwd(tag, dxa, cpre, xbc, w, col0, seq):
    t, width = dxa.shape
    ts, tc = _row_tile(seq, 256), 512
    tiles_per_seq = seq // ts
    hb = ts // HALO
    cb0 = col0 // tc
    n_halo_blocks = t // HALO

    def dsilu(cv, dv):
        cv = cv.astype(F32)
        s = _sigmoid(cv)
        return dv.astype(F32) * (s * (1.0 + cv * (1.0 - s)))

    def fn(dxa_c, dxa_n, c_c, c_n, x_c, x_p, w):
        i = pl.program_id(1)
        dc = dsilu(c_c, dxa_c)
        last = i % tiles_per_seq == tiles_per_seq - 1
        dc_n = jnp.where(last, 0.0, dsilu(c_n, dxa_n)[:8])
        dx = w[3:4] * dc
        for j in range(1, CONV_K):
            dx = dx + w[3 - j:4 - j] * _shift_up(dc, dc_n, j)
        cur = x_c.astype(F32)
        prev8 = jnp.where(i % tiles_per_seq == 0, 0.0, x_p.astype(F32)[HALO - 8:])
        rows = [jnp.sum(dc * cur, axis=0, keepdims=True)]
        for j in range(1, CONV_K):
            rows.append(jnp.sum(dc * _shift_down(cur, prev8, j), axis=0, keepdims=True))
        sub8 = lax.broadcasted_iota(jnp.int32, (8, dc.shape[1]), 0)
        dw = jnp.zeros((8, dc.shape[1]), F32)
        for kk in range(CONV_K):
            dw = jnp.where(sub8 == kk, rows[CONV_K - 1 - kk], dw)
        return dx, dw, jnp.sum(dc, axis=0, keepdims=True)

    nxt = lambda j, i: (jnp.minimum((i + 1) * hb, n_halo_blocks - 1), j)
    nxt_off = lambda j, i: (jnp.minimum((i + 1) * hb, n_halo_blocks - 1), j + cb0)
    return _ew(tag, (width // tc, t // ts), fn,
               [(dxa, (ts, tc), lambda j, i: (i, j)), (dxa, (HALO, tc), nxt),
                (cpre, (ts, tc), lambda j, i: (i, j + cb0)), (cpre, (HALO, tc), nxt_off),
                (xbc, (ts, tc), lambda j, i: (i, j + cb0)),
                (xbc, (HALO, tc), lambda j, i: (jnp.maximum(i * hb - 1, 0), j + cb0)),
                (w, (CONV_K, tc), lambda j, i: (0, j + cb0))],
               [(SDS((t, width), BF), (ts, tc), lambda j, i: (i, j), False),
                (SDS((8, width), F32), (8, tc), lambda j, i: (0, j), True),
                (SDS((1, width), F32), (1, tc), lambda j, i: (0, j), True)],
               acc_axes=(1,))


def _ssd_chunk(xs, bm, cm, dtr, st, dtb, alog, dsk):
    ell = xs.shape[0]
    xs = xs.astype(F32)
    lane = lax.broadcasted_iota(jnp.int32, (ell, LANES), 1)
    sub = lax.broadcasted_iota(jnp.int32, (ell, LANES), 0)
    lane1 = lax.broadcasted_iota(jnp.int32, (1, LANES), 1)
    causal = sub >= lane
    dt = _softplus(dtr + dtb)
    da = dt * (-jnp.exp(alog))
    acs = _dot_hi(causal.astype(F32), da)
    acs_t = acs.T
    cb = _dotf(cm, bm, NT)
    lo = lane < 64
    ys, news = [], []
    for p in range(2):
        xp = xs[:, LANES * p:LANES * (p + 1)]
        sp = st[LANES * p:LANES * (p + 1), :]
        col, dtc, last, dsel = [], [], [], []
        y_diag = None
        for q in range(2):
            r = 2 * p + q
            col_r = jnp.sum(jnp.where(lane == r, acs, 0.0), axis=1, keepdims=True)
            row_r = jnp.sum(jnp.where(sub == r, acs_t, 0.0), axis=0, keepdims=True)
            dtc_r = jnp.sum(jnp.where(lane == r, dt, 0.0), axis=1, keepdims=True)
            decay = jnp.exp(jnp.where(causal, col_r - row_r, -jnp.inf))
            head = lo if q == 0 else jnp.logical_not(lo)
            d = _dotf(cb * decay, jnp.where(head, xp * dtc_r, 0.0), NN)
            y_diag = d if y_diag is None else y_diag + d
            col.append(col_r)
            dtc.append(dtc_r)
            last.append(jnp.sum(jnp.where(sub[:, :1] == ell - 1, col_r, 0.0), axis=0, keepdims=True))
            dsel.append(jnp.sum(jnp.where(lane1 == r, dsk, 0.0), axis=1, keepdims=True))
        y_off = _dotf(cm, sp, NT) * jnp.where(lo, jnp.exp(col[0]), jnp.exp(col[1]))
        xw = jnp.where(lo, xp * (dtc[0] * jnp.exp(last[0] - col[0])), xp * (dtc[1] * jnp.exp(last[1] - col[1])))
        new = sp * jnp.where(sub < 64, jnp.exp(last[0]), jnp.exp(last[1])) + _dotf(xw, bm, TN)
        ys.append(y_diag + y_off + jnp.where(lo, dsel[0], dsel[1]) * xp)
        news.append(new)
    return jnp.concatenate(ys, axis=1), jnp.concatenate(news, axis=0)


SSD_GP = 4
SSD_GP_BWD = 4
XW, GW = 2 * LANES * SSD_GP, LANES * SSD_GP


def _grp(ref, q, width):
    return ref[:, width * q:width * (q + 1)]


def _ssd_fwd(xa, dtr, dtb, alog, dsk, nb, seq):
    t = xa.shape[0]
    nc = seq // CHUNK
    row = lambda g, b, c: (b * nc + c, g)
    par = pl.BlockSpec((1, GW), lambda g, b, c: (0, g))
    b_off, c_off = D_INNER // GW, (D_INNER + SSM_GROUPS * SSM_STATE) // GW

    def body(xs, bm, cm, dtr, dtb, alog, dsk, y_ref, st_out, st_ref):
        @pl.when(pl.program_id(2) == 0)
        def _():
            st_ref[...] = jnp.zeros_like(st_ref)

        ins = [(_grp(xs, q, 2 * LANES), _grp(bm, q, LANES), _grp(cm, q, LANES), _grp(dtr, q, LANES), st_ref[q],
                _grp(dtb, q, LANES), _grp(alog, q, LANES), _grp(dsk, q, LANES)) for q in range(SSD_GP)]
        res = [_ssd_chunk(*a) for a in ins]
        for q in range(SSD_GP):
            st_out[q] = ins[q][4]
            y_ref[:, 2 * LANES * q:2 * LANES * (q + 1)] = res[q][0]
            st_ref[q] = res[q][1]

    specs = [pl.BlockSpec((CHUNK, XW), row),
             pl.BlockSpec((CHUNK, GW), lambda g, b, c: (b * nc + c, b_off + g)),
             pl.BlockSpec((CHUNK, GW), lambda g, b, c: (b * nc + c, c_off + g)),
             pl.BlockSpec((CHUNK, GW), row), par, par, par]
    return pl.pallas_call(
        body, name="ssd_fwd", grid=(SSM_GROUPS // SSD_GP, nb, nc), in_specs=specs,
        out_specs=[pl.BlockSpec((CHUNK, XW), row),
                   pl.BlockSpec((SSD_GP, None, None, 2 * LANES, LANES), lambda g, b, c: (g, b, c, 0, 0))],
        out_shape=[SDS((t, D_INNER), F32), SDS((SSM_GROUPS, nb, nc, 2 * LANES, LANES), F32)],
        scratch_shapes=[pltpu.VMEM((SSD_GP, 2 * LANES, LANES), F32)],
        compiler_params=_cparams(3),
    )(xa, xa, xa, dtr, dtb, alog, dsk)


def _ssd_bwd(xa, dtr, dtb, alog, dsk, states, dy, nb, seq):
    t = xa.shape[0]
    nc = seq // CHUNK
    rev = lambda c: nc - 1 - c
    row = lambda g, b, c: (b * nc + rev(c), g)
    gp = SSD_GP_BWD
    xw, gw = 2 * LANES * gp, LANES * gp
    par = pl.BlockSpec((1, gw), lambda g, b, c: (0, g))
    b_off, c_off = D_INNER // gw, (D_INNER + SSM_GROUPS * SSM_STATE) // gw

    def body(xs, bm, cm, dtr, dtb, alog, dsk, st_in, dy, dxs, dbm, dcm, ddtr, ddtb, dalog, ddsk, dst_ref):
        @pl.when(pl.program_id(2) == 0)
        def _():
            dst_ref[...] = jnp.zeros_like(dst_ref)

        ins = [(_grp(xs, q, 2 * LANES), _grp(bm, q, LANES), _grp(cm, q, LANES), _grp(dtr, q, LANES), st_in[q],
                _grp(dtb, q, LANES), _grp(alog, q, LANES), _grp(dsk, q, LANES)) for q in range(gp)]
        cts = [(_grp(dy, q, 2 * LANES), dst_ref[q]) for q in range(gp)]
        gs = [jax.vjp(_ssd_chunk, *a)[1](ct) for a, ct in zip(ins, cts)]
        for q, g in enumerate(gs):
            dxs[:, 2 * LANES * q:2 * LANES * (q + 1)] = g[0]
            lanes = slice(LANES * q, LANES * (q + 1))
            dbm[:, lanes] = g[1]
            dcm[:, lanes] = g[2]
            ddtr[:, lanes] = g[3]
            dst_ref[q] = g[4]
        first = (pl.program_id(1) == 0) & (pl.program_id(2) == 0)
        for o, k in ((ddtb, 5), (dalog, 6), (ddsk, 7)):
            v = jnp.concatenate([g[k] for g in gs], axis=1)

            @pl.when(first)
            def _(o=o, v=v):
                o[...] = v

            @pl.when(jnp.logical_not(first))
            def _(o=o, v=v):
                o[...] += v

    in_specs = [
        pl.BlockSpec((CHUNK, xw), row),
        pl.BlockSpec((CHUNK, gw), lambda g, b, c: (b * nc + rev(c), b_off + g)),
        pl.BlockSpec((CHUNK, gw), lambda g, b, c: (b * nc + rev(c), c_off + g)),
        pl.BlockSpec((CHUNK, gw), row), par, par, par,
        pl.BlockSpec((gp, None, None, 2 * LANES, LANES), lambda g, b, c: (g, b, rev(c), 0, 0)),
        pl.BlockSpec((CHUNK, xw), row),
    ]
    out_specs = [pl.BlockSpec((CHUNK, xw), row), pl.BlockSpec((CHUNK, gw), row), pl.BlockSpec((CHUNK, gw), row),
                 pl.BlockSpec((CHUNK, gw), row), par, par, par]
    out_shape = [SDS((t, D_INNER), BF), SDS((t, SSM_GROUPS * LANES), BF), SDS((t, SSM_GROUPS * LANES), BF),
                 SDS((t, SSM_GROUPS * LANES), F32)] + [SDS((1, SSM_GROUPS * LANES), F32)] * 3
    return pl.pallas_call(
        body, name="ssd_bwd", grid=(SSM_GROUPS // gp, nb, nc), in_specs=in_specs, out_specs=out_specs,
        out_shape=out_shape, scratch_shapes=[pltpu.VMEM((gp, 2 * LANES, LANES), F32)],
        compiler_params=_cparams(3, VMEM_BIG),
    )(xa, xa, xa, dtr, dtb, alog, dsk, states, dy)


def _gated_fn(y, z, w):
    g = y * _silu(z.astype(F32))
    return g * lax.rsqrt(jnp.mean(g * g, axis=-1, keepdims=True) + EPS) * w


def _gated_norm_fwd(y, z, w):
    t = y.shape[0]
    tm = _row_tile(t, 512)
    blk = ((tm, NORM_GROUP), lambda g, i: (i, g))
    return _ew("ssm_gnorm_fwd", (SSM_GROUPS, t // tm), _gated_fn,
               [(y,) + blk, (z,) + blk, (w, (1, NORM_GROUP), lambda g, i: (0, g))],
               [(SDS((t, D_INNER), BF),) + blk + (False,)])[0]


def _gated_norm_bwd(y, z, w, dout):
    t = y.shape[0]
    tm = _row_tile(t, 512)
    blk = ((tm, NORM_GROUP), lambda g, i: (i, g))
    par = ((1, NORM_GROUP), lambda g, i: (0, g))

    def fn(y, z, w, dout):
        _, vjp = jax.vjp(_gated_fn, y, z, w)
        return vjp(dout.astype(F32))

    return _ew("ssm_gnorm_bwd", (SSM_GROUPS, t // tm), fn,
               [(y,) + blk, (z,) + blk, (w,) + par, (dout,) + blk],
               [(SDS((t, D_INNER), F32),) + blk + (False,), (SDS((t, D_INNER), BF),) + blk + (False,),
                (SDS((1, D_INNER), F32),) + par + (True,)],
               acc_axes=(1,))


def _proj_nt(name, h, wt, n, out_dtype, tn=256, hosts=None):
    t, kdim = h.shape
    tm = _row_tile(t, 1024)
    return _fmm(name, (t // tm, n // tn),
                [(h, (tm, kdim), lambda i, j, k: (i, 0), wt, (tn, kdim), lambda i, j, k: (j, 0), NT, 1, 0)],
                [(SDS((t, n), out_dtype), (tm, tn), lambda i, j: (i, j), False)], hosts=hosts)[0]


def _seg_nn(name, parts, n, out_dtype, tk=256, hosts=None):
    t = parts[0][0].shape[0]
    tm = _row_tile(t, 512)
    pairs = []
    for a, w, row0 in parts:
        kp = a.shape[1]
        tkp = min(tk, kp)
        r0 = row0 // tkp
        pairs.append((a, (tm, tkp), lambda i, j, k: (i, k), w, (tkp, n), lambda i, j, k, r0=r0: (k + r0, 0),
                      NN, kp // tkp, 0))
    return _fmm(name, (t // tm, 1), pairs, [(SDS((t, n), out_dtype), (tm, n), lambda i, j: (i, 0), False)],
                acc_shape=(tm, n), hosts=hosts)[0]


def _ssm_fwd(x, nw, ws, small, nb, seq, hosts=None):
    t, d = x.shape
    h = _rms_fwd("ssm_rms", x, nw)
    z = _proj_nt("ssm_in_z", h, ws["z_t"], D_INNER, BF, hosts=hosts)
    xbc = _proj_nt("ssm_in_xbc", h, ws["xbc_t"], CONV_DIM, BF, hosts=hosts)
    dtr = _proj_nt("ssm_in_dt", h, ws["dt_t"], SSM_GROUPS * LANES, F32)
    cpre, xa = _conv_fwd(xbc, small["conv_w"], small["conv_b"], seq)
    y, states = _ssd_fwd(xa, dtr, small["dt_bias"], small["a_log"], small["d_skip"], nb, seq)
    gn = _gated_norm_fwd(y, z, small["ssm_norm_w"])
    tm = _row_tile(t, 512)
    out = _fmm("ssm_out", (t // tm, 1),
               [(gn, (tm, D_INNER), lambda i, j, k: (i, 0), ws["out"], (D_INNER, d), lambda i, j, k: (0, 0), NN, 1, 0)],
               [(SDS((t, d), F32), (tm, d), lambda i, j: (i, 0), False)],
               extras=[(x, (tm, d), lambda i, j: (i, 0))], epi=lambda accs, xr: xr + accs[0], hosts=hosts)[0]
    return out, (x, h, z, xbc, dtr, cpre, xa, y, states, gn)


def _ssm_bwd(dy, nw, ws, small, saved, nb, seq, hosts=None):
    x, h, z, xbc, dtr, cpre, xa, y, states, gn = saved
    t, d = x.shape
    dgn = _proj_nt("ssm_bwd_dgn", dy, ws["out"], D_INNER, BF)
    d_out = _wgrad("ssm_w_out_g", gn, dy, D_INNER, d, tm=D_INNER // 2, tn=d)
    dyssd, dz, d_normw = _gated_norm_bwd(y, z, small["ssm_norm_w"], dgn)
    dxs, dbm, dcm, ddtr, d_dtb, d_alog, d_dsk = _ssd_bwd(
        xa, dtr, small["dt_bias"], small["a_log"], small["d_skip"], states, dyssd, nb, seq)
    dxbc_x, dcw_x, dcb_x = _conv_bwd("ssm_conv_bwd_x", dxs, cpre, xbc, small["conv_w"], 0, seq)
    dxbc_b, dcw_b, dcb_b = _conv_bwd("ssm_conv_bwd_b", dbm, cpre, xbc, small["conv_w"], D_INNER, seq)
    dxbc_c, dcw_c, dcb_c = _conv_bwd("ssm_conv_bwd_c", dcm, cpre, xbc, small["conv_w"], D_INNER + 1024, seq)
    parts = [(dz, ws["z_t"], 0), (dxbc_x, ws["xbc_t"], 0), (dxbc_b, ws["xbc_t"], D_INNER),
             (dxbc_c, ws["xbc_t"], D_INNER + 1024), (ddtr, ws["dt_t"], 0)]
    dh = _seg_nn("ssm_bwd_dh", parts, d, BF, tk=1024, hosts=hosts)
    dx, dnw = _rms_bwd("ssm_bwd_rms", x, nw, dh, dy)
    g = {
        "z_t": _wgrad("ssm_w_z_g", dz, h, D_INNER, d, tm=1024, tn=d, out_dtype=F32),
        "x_t": _wgrad("ssm_w_x_g", dxbc_x, h, D_INNER, d, tm=1024, tn=d, out_dtype=F32),
        "b_t": _wgrad("ssm_w_b_g", dxbc_b, h, 1024, d, tm=1024, tn=d, out_dtype=F32),
        "c_t": _wgrad("ssm_w_c_g", dxbc_c, h, 1024, d, tm=1024, tn=d, out_dtype=F32),
        "dt_t": _wgrad("ssm_w_dt_g", ddtr, h, 1024, d, tm=1024, tn=d, out_dtype=F32),
        "out": d_out,
        "conv_w": jnp.concatenate([dcw_x[:CONV_K], dcw_b[:CONV_K], dcw_c[:CONV_K]], axis=1),
        "conv_b": jnp.concatenate([dcb_x, dcb_b, dcb_c], axis=1),
        "dt_bias": d_dtb, "a_log": d_alog, "d_skip": d_dsk, "ssm_norm_w": d_normw,
    }
    return dx, dnw, g


def _lane_masks(shape):
    lane = lax.broadcasted_iota(jnp.int32, shape, len(shape) - 1)
    return lane < QK_NOPE, (lane >= QK_NOPE) & (lane < QK_DIM)


def _swap_matrix():
    p = np.zeros((HEAD_PAD, HEAD_PAD), np.float32)
    for i in range(QK_ROPE // 2):
        p[QK_NOPE + QK_ROPE // 2 + i, QK_NOPE + i] = 1.0
        p[QK_NOPE + i, QK_NOPE + QK_ROPE // 2 + i] = 1.0
    return jnp.asarray(p)


def _rope_tables(positions_col):
    t = positions_col.shape[0]
    tm = _row_tile(t, 512)
    freq = np.zeros((1, HEAD_PAD), np.float32)
    inv = 1.0 / (ROPE_THETA ** (np.arange(0, QK_ROPE, 2, dtype=np.float32) / QK_ROPE))
    freq[0, QK_NOPE:QK_NOPE + QK_ROPE // 2] = inv
    freq[0, QK_NOPE + QK_ROPE // 2:QK_DIM] = inv
    sign = np.zeros((1, HEAD_PAD), np.float32)
    sign[0, QK_NOPE:QK_NOPE + QK_ROPE // 2] = -1.0
    sign[0, QK_NOPE + QK_ROPE // 2:QK_DIM] = 1.0

    def fn(pos, freq, sign):
        ang = pos.astype(F32) * freq
        nope, rope = _lane_masks(ang.shape)
        return jnp.where(nope, 1.0, jnp.where(rope, jnp.cos(ang), 0.0)), jnp.sin(ang) * sign

    row = lambda i: (i, 0)
    par = ((1, HEAD_PAD), lambda i: (0, 0))
    return _ew("mla_rope_tables", (t // tm,), fn,
               [(positions_col, (tm, 1), row), (jnp.asarray(freq),) + par, (jnp.asarray(sign),) + par],
               [(SDS((t, HEAD_PAD), F32), (tm, HEAD_PAD), row, False)] * 2)


def _rope(xn, cos, sin_signed, swap):
    return xn * cos + _dot_hi(xn, swap) * sin_signed


def _krope_fn(kr, w, cos, sin_signed, swap):
    _, rope = _lane_masks(kr.shape)
    ss = jnp.sum(jnp.where(rope, kr * kr, 0.0), axis=-1, keepdims=True)
    xn = jnp.where(rope, kr * lax.rsqrt(ss * (1.0 / QK_ROPE) + EPS) * w, 0.0)
    return _rope(xn, cos, sin_signed, swap)


def _head_fn(q, kv, kr, cos, sin_signed, qn, kn, swap):
    nope, rope = _lane_masks(q.shape)

    def rstd(x, mask, n):
        return lax.rsqrt(jnp.sum(jnp.where(mask, x * x, 0.0), axis=-1, keepdims=True) * (1.0 / n) + EPS)

    qs = jnp.where(nope, rstd(q, nope, QK_NOPE), rstd(q, rope, QK_ROPE))
    qp = _rope(jnp.where(nope | rope, q * qs * qn, 0.0), cos, sin_signed, swap)
    kp = jnp.where(nope, kv * rstd(kv, nope, QK_NOPE) * kn, 0.0) + kr
    vp = jnp.where(nope, 0.0, kv)
    return qp, kp, vp


def _heads_fwd(q_raw, kv_raw, kr, cos, sin_signed, qn, kn, swap):
    nh, t, _ = q_raw.shape
    tm = _row_tile(t, 512)
    hblk = ((None, tm, HEAD_PAD), lambda i, h: (h, i, 0))
    tblk = ((tm, HEAD_PAD), lambda i, h: (i, 0))
    par = ((1, HEAD_PAD), lambda i, h: (0, 0))
    sw = ((HEAD_PAD, HEAD_PAD), lambda i, h: (0, 0))
    def fn(*tiles):
        qp, kp, vp = _head_fn(*tiles)
        return qp * Q_PRESCALE, kp, vp

    return _ew("mla_heads_fwd", (t // tm, nh), fn,
               [(q_raw,) + hblk, (kv_raw,) + hblk, (kr,) + tblk, (cos,) + tblk, (sin_signed,) + tblk,
                (qn,) + par, (kn,) + par, (swap,) + sw],
               [(SDS((nh, t, HEAD_PAD), BF),) + hblk + (False,)] * 3)


def _heads_bwd(q_raw, kv_raw, kr, cos, sin_signed, qn, kn, swap, dqp, dkp, dvp):
    nh, t, _ = q_raw.shape
    tm = _row_tile(t, 512)
    hblk = ((None, tm, HEAD_PAD), lambda i, h: (h, i, 0))
    tblk = ((tm, HEAD_PAD), lambda i, h: (i, 0))
    par = ((1, HEAD_PAD), lambda i, h: (0, 0))
    sw = ((HEAD_PAD, HEAD_PAD), lambda i, h: (0, 0))

    def body(q, kv, kr, cos, sn, qn, kn, swap, dqp, dkp, dvp, dq, dkv, dkr, dqn, dkn):
        f = lambda q, kv, kr, qn, kn: _head_fn(q, kv, kr, cos[...], sn[...], qn, kn, swap[...])
        _, vjp = jax.vjp(f, q[...], kv[...], kr[...], qn[...], kn[...])
        g = vjp((dqp[...].astype(F32), dkp[...].astype(F32), dvp[...].astype(F32)))
        dq[...] = g[0].astype(dq.dtype)
        dkv[...] = g[1].astype(dkv.dtype)
        h0 = pl.program_id(1) == 0
        first = h0 & (pl.program_id(0) == 0)
        for o, v, c in ((dkr, g[2], h0), (dqn, g[3], first), (dkn, g[4], first)):
            @pl.when(c)
            def _(o=o, v=v):
                o[...] = v

            @pl.when(jnp.logical_not(c))
            def _(o=o, v=v):
                o[...] += v

    spec = lambda b: pl.BlockSpec(*b)
    return pl.pallas_call(
        body, name="mla_heads_bwd", grid=(t // tm, nh),
        in_specs=[spec(hblk), spec(hblk), spec(tblk), spec(tblk), spec(tblk), spec(par), spec(par), spec(sw),
                  spec(hblk), spec(hblk), spec(hblk)],
        out_specs=[spec(hblk), spec(hblk), spec(tblk), spec(par), spec(par)],
        out_shape=[SDS((nh, t, HEAD_PAD), BF), SDS((nh, t, HEAD_PAD), BF), SDS((t, HEAD_PAD), F32),
                   SDS((1, HEAD_PAD), F32), SDS((1, HEAD_PAD), F32)],
        compiler_params=_cparams(2),
    )(q_raw, kv_raw, kr, cos, sin_signed, qn, kn, swap, dqp, dkp, dvp)


ATT_TILE = 512
ATT_SCALE = QK_DIM ** -0.5
LOG2E = 1.4426950408889634
LN2 = 0.6931471805599453
Q_PRESCALE = ATT_SCALE * LOG2E


def _flash_fwd(qs, k, v, nb, seq, hosts=None):
    nh, t, dh = qs.shape
    tq = _row_tile(seq, ATT_TILE)
    nq = seq // tq
    name = "mla_flash_fwd"
    comm = hosts.comm(name) if hosts is not None else None
    n_ci = len(comm.arrays) if comm is not None else 0
    n_co = len(comm.out_shapes) if comm is not None else 0

    def body(*refs):
        q_ref, k_ref, v_ref = refs[:3]
        ci_refs = refs[3:3 + n_ci]
        o_ref, lse_ref = refs[3 + n_ci:5 + n_ci]
        co_refs = refs[5 + n_ci:5 + n_ci + n_co]
        sem_refs = refs[5 + n_ci + n_co:]
        ids = (pl.program_id(0), pl.program_id(1), pl.program_id(2))
        if comm is not None:
            @pl.when((ids[0] == 0) & (ids[1] == 0) & (ids[2] == 0))
            def _():
                comm.start(ci_refs, co_refs, sem_refs)

        attend(q_ref, k_ref, v_ref, o_ref, lse_ref)

        if comm is not None:
            @pl.when((ids[0] == nh - 1) & (ids[1] == nb - 1) & (ids[2] == nq - 1))
            def _():
                comm.wait(ci_refs, co_refs, sem_refs)

    def attend(q_ref, k_ref, v_ref, o_ref, lse_ref):
        qi = pl.program_id(2)
        qt = q_ref[...]

        def tile(j, carry, diagonal):
            m, l, acc = carry
            rows = pl.ds(pl.multiple_of(j * tq, tq), tq)
            s = _dotf(qt, k_ref[rows, :], NT)
            if diagonal:
                r = lax.broadcasted_iota(jnp.int32, (tq, tq), 0)
                c = lax.broadcasted_iota(jnp.int32, (tq, tq), 1)
                s = jnp.where(c <= r, s, -jnp.inf)
            m_new = jnp.maximum(m, jnp.max(s, axis=-1, keepdims=True))
            alpha = jnp.exp2(m - m_new)
            p = jnp.exp2(s - m_new)
            return m_new, alpha * l + jnp.sum(p, axis=-1, keepdims=True), alpha * acc + _dotf(p, v_ref[rows, :], NN)

        init = (jnp.full((tq, 1), -jnp.inf, F32), jnp.zeros((tq, 1), F32), jnp.zeros((tq, dh), F32))
        carry = lax.fori_loop(0, qi, lambda j, c: tile(j, c, False), init)
        m, l, acc = tile(qi, carry, True)
        o_ref[...] = (acc / l).astype(o_ref.dtype)
        lse_ref[...] = m + jnp.log2(l)

    qblk = pl.BlockSpec((None, tq, dh), lambda h, b, i: (h, b * nq + i, 0))
    kblk = pl.BlockSpec((None, seq, dh), lambda h, b, i: (h, b, 0))
    hbm = pl.BlockSpec(memory_space=pl.ANY)
    res = pl.pallas_call(
        body, name=name, grid=(nh, nb, nq), in_specs=[qblk, kblk, kblk] + [hbm] * n_ci,
        out_specs=[qblk, pl.BlockSpec((None, tq, 1), lambda h, b, i: (h, b * nq + i, 0))] + [hbm] * n_co,
        out_shape=[SDS((nh, t, dh), BF), SDS((nh, t, 1), F32)] + (list(comm.out_shapes) if comm is not None else []),
        scratch_shapes=comm.sems if comm is not None else [],
        compiler_params=_cparams(3, VMEM_BIG),
    )(qs, k, v, *(comm.arrays if comm is not None else []))
    if comm is not None:
        hosts.done(name, res[2:])
    return res[0], res[1]


def _flash_bwd(qs, k, v, o, lse, do, nb, seq):
    nh, t, dh = qs.shape
    tq = _row_tile(seq, ATT_TILE)
    nq = seq // tq

    def row_of(col):
        return jnp.broadcast_to(col, (tq, LANES)).T[0:1, :]

    def body(q_ref, k_ref, v_ref, o_ref, lse_ref, do_ref, dq_ref, dk_ref, dv_ref, kt_sc, lrow_sc, drow_sc, dqt_sc):
        for c in range(nq):
            rows = pl.ds(c * tq, tq)
            kt_sc[c] = k_ref[rows, :].T
            delta = jnp.sum(do_ref[rows, :].astype(F32) * o_ref[rows, :].astype(F32), axis=-1, keepdims=True)
            drow_sc[c] = row_of(delta)
            lrow_sc[c] = row_of(lse_ref[rows, :])
        dqt_sc[...] = jnp.zeros_like(dqt_sc)

        def kv_step(j, _):
            rows_j = pl.ds(pl.multiple_of(j * tq, tq), tq)
            ks, vs, kt = k_ref[rows_j, :], v_ref[rows_j, :], kt_sc[j]

            def q_tile(i, carry, diagonal):
                dk, dv = carry
                rows_i = pl.ds(pl.multiple_of(i * tq, tq), tq)
                qt, dot_ = q_ref[rows_i, :], do_ref[rows_i, :]
                pt = jnp.exp2(_dotf(ks, qt, NT) - lrow_sc[i])
                if diagonal:
                    kk = lax.broadcasted_iota(jnp.int32, (tq, tq), 0)
                    qq = lax.broadcasted_iota(jnp.int32, (tq, tq), 1)
                    pt = jnp.where(kk <= qq, pt, 0.0)
                dst = (pt * (_dotf(vs, dot_, NT) - drow_sc[i])).astype(BF)
                dqt_sc[i] += _dotf(kt, dst, NN)
                return dk + _dotf(dst, qt, NN), dv + _dotf(pt, dot_, NN)

            zero = jnp.zeros((tq, dh), F32)
            carry = q_tile(j, (zero, zero), True)
            dk, dv = lax.fori_loop(j + 1, nq, lambda i, c: q_tile(i, c, False), carry)
            dk_ref[rows_j, :] = dk * LN2
            dv_ref[rows_j, :] = dv
            return 0

        lax.fori_loop(0, nq, kv_step, 0)
        for c in range(nq):
            dq_ref[pl.ds(c * tq, tq), :] = dqt_sc[c].T * ATT_SCALE

    full = pl.BlockSpec((None, seq, dh), lambda h, b: (h, b, 0))
    sfull = pl.BlockSpec((None, seq, 1), lambda h, b: (h, b, 0))
    return pl.pallas_call(
        body, name="mla_flash_bwd", grid=(nh, nb), in_specs=[full, full, full, full, sfull, full],
        out_specs=[full, full, full], out_shape=[SDS((nh, t, dh), F32)] * 3,
        scratch_shapes=[pltpu.VMEM((nq, dh, tq), BF), pltpu.VMEM((nq, 1, tq), F32), pltpu.VMEM((nq, 1, tq), F32),
                        pltpu.VMEM((nq, dh, tq), F32)],
        compiler_params=_cparams(2, VMEM_BIG),
    )(qs, k, v, o, lse, do)


def _heads_nt(name, a, wt, out_dtype):
    t, kdim = a.shape
    tm = _row_tile(t, 512)
    nw = MLA_HEADS * HEAD_PAD

    def body(a_ref, w_ref, o_ref):
        r = _dotf(a_ref[...], w_ref[...], NT)
        for h in range(MLA_HEADS):
            o_ref[h] = r[:, HEAD_PAD * h:HEAD_PAD * (h + 1)].astype(o_ref.dtype)

    return pl.pallas_call(
        body, name=name, grid=(t // tm,),
        in_specs=[pl.BlockSpec((tm, kdim), lambda i: (i, 0)), pl.BlockSpec((nw, kdim), lambda i: (0, 0))],
        out_specs=pl.BlockSpec((MLA_HEADS, tm, HEAD_PAD), lambda i: (0, i, 0)),
        out_shape=SDS((MLA_HEADS, t, HEAD_PAD), out_dtype), compiler_params=_cparams(1, VMEM_BIG),
    )(a, wt)


def _all_heads(a_ref):
    return jnp.concatenate([a_ref[h] for h in range(MLA_HEADS)], axis=1)


def _heads_nn(name, a, w, n, out_dtype, res=None):
    t = a.shape[1]
    tm = _row_tile(t, 512)
    nw = MLA_HEADS * HEAD_PAD

    def body(*refs):
        a_ref, w_ref, o_ref = refs[0], refs[1], refs[-1]
        r = _dotf(_all_heads(a_ref), w_ref[...], NN)
        if res is not None:
            r = r + refs[2][...]
        o_ref[...] = r.astype(o_ref.dtype)

    row = pl.BlockSpec((tm, n), lambda i: (i, 0))
    in_specs = [pl.BlockSpec((MLA_HEADS, tm, HEAD_PAD), lambda i: (0, i, 0)), pl.BlockSpec((nw, n), lambda i: (0, 0))]
    args = [a, w]
    if res is not None:
        in_specs.append(row)
        args.append(res)
    return pl.pallas_call(body, name=name, grid=(t // tm,), in_specs=in_specs, out_specs=row,
                          out_shape=SDS((t, n), out_dtype), compiler_params=_cparams(1, VMEM_BIG))(*args)


def _heads_wgrad(name, a, b, n):
    t = b.shape[0]
    tk = _row_tile(t, 512)
    nw = MLA_HEADS * HEAD_PAD
    steps = t // tk

    def body(a_ref, b_ref, o_ref, acc):
        k = pl.program_id(0)

        @pl.when(k == 0)
        def _():
            acc[...] = jnp.zeros_like(acc)

        acc[...] += _dotf(_all_heads(a_ref), b_ref[...], TN)

        @pl.when(k == steps - 1)
        def _():
            o_ref[...] = acc[...]

    return pl.pallas_call(
        body, name=name, grid=(steps,),
        in_specs=[pl.BlockSpec((MLA_HEADS, tk, HEAD_PAD), lambda k: (0, k, 0)), pl.BlockSpec((tk, n), lambda k: (k, 0))],
        out_specs=pl.BlockSpec((nw, n), lambda k: (0, 0)), out_shape=SDS((nw, n), F32),
        scratch_shapes=[pltpu.VMEM((nw, n), F32)], compiler_params=_cparams(1, VMEM_BIG),
    )(a, b)


PM_CKV, PM_KR, PM_CQ = 0, KV_LORA, KV_LORA + HEAD_PAD
PM_DIM = KV_LORA + HEAD_PAD + Q_LORA


def _lat_specs(t, tm):
    return (((tm, KV_LORA), lambda i: (i, 0)), ((tm, HEAD_PAD), lambda i: (i, PM_KR // HEAD_PAD)),
            ((tm, Q_LORA), lambda i: (i, PM_CQ // Q_LORA)))


def _mla_fwd(x, nw, wm, small, tables, nb, seq, hosts=None):
    t, d = x.shape
    cos, sin_signed, swap = tables
    h = _rms_fwd("mla_rms", x, nw)
    pm = _proj_nt("mla_in", h, wm["in_t"], PM_DIM, F32, tn=PM_DIM // 3)
    tm = _row_tile(t, 512)
    ckv_s, kr_s, cq_s = _lat_specs(t, tm)
    row = lambda i: (i, 0)
    par = lambda n: ((1, n), lambda i: (0, 0))
    ckvn = _ew("mla_ckv_norm", (t // tm,), _rms_fn, [(pm,) + ckv_s, (small["kv_a_norm"],) + par(KV_LORA)],
               [(SDS((t, KV_LORA), BF), (tm, KV_LORA), row, False)])[0]
    cqn = _ew("mla_cq_norm", (t // tm,), _rms_fn, [(pm,) + cq_s, (small["q_a_norm"],) + par(Q_LORA)],
              [(SDS((t, Q_LORA), BF), (tm, Q_LORA), row, False)])[0]
    tb = ((tm, HEAD_PAD), row)
    kr = _ew("mla_krope", (t // tm,), _krope_fn,
             [(pm,) + kr_s, (small["k_norm"],) + par(HEAD_PAD), (cos,) + tb, (sin_signed,) + tb,
              (swap, (HEAD_PAD, HEAD_PAD), lambda i: (0, 0))],
             [(SDS((t, HEAD_PAD), F32),) + tb + (False,)])[0]
    q_raw = _heads_nt("mla_q_b", cqn, wm["qb_t"], F32)
    kv_raw = _heads_nt("mla_kv_b", ckvn, wm["kvb_t"], F32)
    qp, kp, vp = _heads_fwd(q_raw, kv_raw, kr, cos, sin_signed, small["q_norm"], small["k_norm"], swap)
    o, lse = _flash_fwd(qp, kp, vp, nb, seq, hosts=hosts)
    out = _heads_nn("mla_out", o, wm["out"], d, F32, res=x)
    return out, (x, h, pm, ckvn, cqn, kr, q_raw, kv_raw, qp, kp, vp, o, lse)


def _mla_bwd(dy, nw, wm, small, tables, saved, nb, seq):
    x, h, pm, ckvn, cqn, kr, q_raw, kv_raw, qp, kp, vp, o, lse = saved
    t, d = x.shape
    cos, sin_signed, swap = tables
    do = _heads_nt("mla_bwd_do", dy, wm["out"], BF)
    g_out = _heads_wgrad("mla_w_out_g", o, dy, d)
    dqp, dkp, dvp = _flash_bwd(qp, kp, vp, o, lse, do, nb, seq)
    dq_raw, dkv_raw, dkr, d_qn, d_kn = _heads_bwd(q_raw, kv_raw, kr, cos, sin_signed, small["q_norm"],
                                                   small["k_norm"], swap, dqp, dkp, dvp)
    dcqn = _heads_nn("mla_bwd_dcq", dq_raw, wm["qb_t"], Q_LORA, F32)
    dckvn = _heads_nn("mla_bwd_dckv", dkv_raw, wm["kvb_t"], KV_LORA, F32)
    g_qb = _heads_wgrad("mla_w_qb_g", dq_raw, cqn, Q_LORA)
    g_kvb = _heads_wgrad("mla_w_kvb_g", dkv_raw, ckvn, KV_LORA)
    tm = _row_tile(t, 512)
    ckv_s, kr_s, cq_s = _lat_specs(t, tm)
    row = lambda i: (i, 0)
    par = lambda n: ((1, n), lambda i: (0, 0))

    def rms_b(xv, w, dv):
        _, vjp = jax.vjp(_rms_fn, xv, w)
        return vjp(dv)

    dckv, d_kva = _ew("mla_ckv_norm_bwd", (t // tm,), rms_b,
                      [(pm,) + ckv_s, (small["kv_a_norm"],) + par(KV_LORA), (dckvn, (tm, KV_LORA), row)],
                      [(SDS((t, KV_LORA), BF), (tm, KV_LORA), row, False),
                       (SDS((1, KV_LORA), F32),) + par(KV_LORA) + (True,)], acc_axes=(0,))
    dcq, d_qa = _ew("mla_cq_norm_bwd", (t // tm,), rms_b,
                    [(pm,) + cq_s, (small["q_a_norm"],) + par(Q_LORA), (dcqn, (tm, Q_LORA), row)],
                    [(SDS((t, Q_LORA), BF), (tm, Q_LORA), row, False),
                     (SDS((1, Q_LORA), F32),) + par(Q_LORA) + (True,)], acc_axes=(0,))
    tb = ((tm, HEAD_PAD), row)

    def kr_b(krv, w, cosv, sinv, sw, dv):
        _, vjp = jax.vjp(lambda a, b: _krope_fn(a, b, cosv, sinv, sw), krv, w)
        return vjp(dv)

    dkr_raw, d_kn2 = _ew("mla_krope_bwd", (t // tm,), kr_b,
                         [(pm,) + kr_s, (small["k_norm"],) + par(HEAD_PAD), (cos,) + tb, (sin_signed,) + tb,
                          (swap, (HEAD_PAD, HEAD_PAD), lambda i: (0, 0)), (dkr,) + tb],
                         [(SDS((t, HEAD_PAD), BF),) + tb + (False,),
                          (SDS((1, HEAD_PAD), F32),) + par(HEAD_PAD) + (True,)], acc_axes=(0,))
    dh = _seg_nn("mla_bwd_dh", [(dckv, wm["in_t"], PM_CKV), (dkr_raw, wm["in_t"], PM_KR),
                                (dcq, wm["in_t"], PM_CQ)], d, BF, tk=128)
    dx, dnw = _rms_bwd("mla_bwd_rms", x, nw, dh, dy)
    g = {
        "in_ckv_t": _wgrad("mla_w_in_ckv_g", dckv, h, KV_LORA, d, tm=KV_LORA, tn=d, out_dtype=F32),
        "in_kr_t": _wgrad("mla_w_in_kr_g", dkr_raw, h, HEAD_PAD, d, tm=HEAD_PAD, tn=d, out_dtype=F32),
        "in_cq_t": _wgrad("mla_w_in_cq_g", dcq, h, Q_LORA, d, tm=Q_LORA, tn=d, out_dtype=F32),
        "qb_t": g_qb, "kvb_t": g_kvb, "out": g_out,
        "q_a_norm": d_qa, "kv_a_norm": d_kva, "q_norm": d_qn, "k_norm": d_kn + d_kn2,
    }
    return dx, dnw, g


def _mesh_pos():
    return lax.axis_index("x"), lax.axis_index("y"), lax.axis_index("c")


def _peer(pos, k):
    x, y, c = pos
    return (x ^ ((k >> 2) & 1), y ^ ((k >> 1) & 1), c ^ (k & 1))


def _flat(pos):
    return 4 * pos[0] + 2 * pos[1] + pos[2]


def _slab(ref, axis, start, size):
    idx = [slice(None)] * axis + [pl.ds(start, size)]
    return ref.at[tuple(idx)]


class _Exchange:
    def __init__(self, kind, items):
        self.kind = kind
        self.axes = [ax for _, ax in items]
        self.arrays = [a for a, _ in items]
        n = len(items)
        self.out_shapes = []
        self.sizes = []
        for a, ax in items:
            shp = list(a.shape)
            if kind == "gather":
                self.sizes.append(shp[ax])
                shp[ax] *= N_DEV
                self.out_shapes.append(SDS(tuple(shp), a.dtype))
            else:
                shp[ax] //= N_DEV
                self.sizes.append(shp[ax])
                self.out_shapes.append(SDS((N_DEV,) + tuple(shp), a.dtype))
        self.sems = [pltpu.SemaphoreType.DMA((n, N_DEV - 1)), pltpu.SemaphoreType.DMA((n, N_DEV - 1)),
                     pltpu.SemaphoreType.DMA((n,))]

    def _copies(self, srcs, dsts, sems, with_arrivals=True):
        send_sems, recv_sems, local_sems = sems
        pos = _mesh_pos()
        me = _flat(pos)
        local, sends, recvs = [], [], []
        for t, (src, dst) in enumerate(zip(srcs, dsts)):
            ax, sz = self.axes[t], self.sizes[t]
            if self.kind == "gather":
                mine = _slab(dst, ax, me * sz, sz)
                local.append(pltpu.make_async_copy(src, mine, local_sems.at[t]))
            else:
                mine = dst.at[me]
                local.append(pltpu.make_async_copy(_slab(src, ax, me * sz, sz), mine, local_sems.at[t]))
            for k in range(1, N_DEV):
                peer = _peer(pos, k)
                there = _flat(peer)
                if self.kind == "gather":
                    out_src, landing = src, _slab(dst, ax, there * sz, sz)
                else:
                    out_src, landing = _slab(src, ax, there * sz, sz), dst.at[there]
                common = dict(send_sem=send_sems.at[t, k - 1], recv_sem=recv_sems.at[t, k - 1], device_id=peer,
                              device_id_type=pl.DeviceIdType.MESH)
                sends.append(pltpu.make_async_remote_copy(src_ref=out_src, dst_ref=mine, **common))
                if with_arrivals:
                    recvs.append(pltpu.make_async_remote_copy(src_ref=out_src, dst_ref=landing, **common))
        return local, sends, recvs

    def start(self, srcs, dsts, sems):
        local, sends, _ = self._copies(srcs, dsts, sems, with_arrivals=False)
        for cp in local + sends:
            cp.start()

    def wait(self, srcs, dsts, sems):
        local, sends, recvs = self._copies(srcs, dsts, sems)
        for rc in recvs:
            rc.wait_recv()
        for rc in sends:
            rc.wait_send()
        for cp in local:
            cp.wait()

    def run(self, name):
        n = len(self.arrays)

        def body(*refs):
            srcs, dsts, sems = refs[:n], refs[n:2 * n], refs[2 * n:]
            self.start(srcs, dsts, sems)
            self.wait(srcs, dsts, sems)

        hbm = pl.BlockSpec(memory_space=pl.ANY)
        return pl.pallas_call(body, name=name, in_specs=[hbm] * n, out_specs=[hbm] * n, out_shape=self.out_shapes,
                              scratch_shapes=self.sems)(*self.arrays)


def _adam_math(w, g, m, v):
    m = ADAM_B1 * m + (1.0 - ADAM_B1) * g
    v = ADAM_B2 * v + (1.0 - ADAM_B2) * (g * g)
    m_hat = m / (1.0 - ADAM_B1 ** ADAM_STEP)
    v_hat = v / (1.0 - ADAM_B2 ** ADAM_STEP)
    delta = -ADAM_LR * (m_hat / (jnp.sqrt(v_hat) + ADAM_EPS) + ADAM_WD * w)
    return delta, m, v


def _adam(name, land, land_blk, land_idx, w, m, v, transposed, ck):
    n, r, c = w.shape
    wblk = ((None, ck, c), lambda a, i: (a, i, 0))

    def fn(parts, w, m, v):
        g = parts[0].astype(F32)
        for s in range(1, N_DEV):
            g = g + parts[s].astype(F32)
        if transposed:
            g = g.T
        delta, m2, v2 = _adam_math(w, g, m, v)
        return g, delta, m2, v2

    return _ew(name, (n, r // ck), fn,
               [(land, land_blk, land_idx), (w,) + wblk, (m,) + wblk, (v,) + wblk],
               [(SDS(w.shape, F32),) + wblk + (False,)] * 4, vmem=VMEM_BIG)


def _prep_ffn(gate, up, down):
    def body(g, u, dn, o):
        o[0] = g[...].T.astype(BF)
        o[1] = u[...].T.astype(BF)
        o[2] = dn[...].astype(BF)

    cblk = pl.BlockSpec((None, None, D_MODEL, FF_SHARD), lambda l, i: (l, i, 0, 0))
    rblk = pl.BlockSpec((None, None, FF_SHARD, D_MODEL), lambda l, i: (l, i, 0, 0))
    return pl.pallas_call(
        body, name="prep_ffn", grid=(2, 2), in_specs=[cblk, cblk, rblk],
        out_specs=pl.BlockSpec((3, FF_SHARD, D_MODEL), lambda l, i: (2 * l + i, 0, 0)),
        out_shape=SDS((12, FF_SHARD, D_MODEL), BF), compiler_params=_cparams(2, VMEM_BIG),
    )(gate, up, down)


def _transpose_cast(name, w, dtype):
    def body(a, o):
        o[...] = a[...].T.astype(dtype)

    r, c = w.shape
    return pl.pallas_call(body, name=name, out_shape=SDS((c, r), dtype),
                          compiler_params=pltpu.CompilerParams(vmem_limit_bytes=VMEM_BIG))(w)


SMALL_SHARDED = (("norm_w", 6 * 128), ("conv_w", CONV_K * 512), ("q_a_norm", 48), ("kv_a_norm", 32))
SMALL_PACK = 3072


def _dyn(a, start, size):
    return lax.dynamic_slice_in_dim(a, start, size, axis=a.ndim - 1)


def _layout_ssm(ssm_in_t, ssm_out_all):
    d = ssm_in_t.shape[1]
    dt_rows = ssm_in_t[D_INNER + CONV_DIM:].reshape(SSM_GROUPS, SSM_HPG, d)
    return {"z_t": ssm_in_t[:D_INNER], "xbc_t": ssm_in_t[D_INNER:D_INNER + CONV_DIM],
            "dt_t": jnp.pad(dt_rows, ((0, 0), (0, LANES - SSM_HPG), (0, 0))).reshape(SSM_GROUPS * LANES, d),
            "out": ssm_out_all}


def _layout_mla(mla_in_all, qb_all, kvb_all, mla_out_all):
    d = mla_out_all.shape[1]
    in_t = mla_in_all.T
    kr_rows = jnp.pad(in_t[Q_LORA + KV_LORA:], ((QK_NOPE, HEAD_PAD - QK_DIM), (0, 0)))
    qb_heads = jnp.pad(qb_all.reshape(MLA_HEADS, QK_DIM, Q_LORA), ((0, 0), (0, HEAD_PAD - QK_DIM), (0, 0)))
    out_heads = jnp.pad(mla_out_all.reshape(MLA_HEADS, 64, d), ((0, 0), (64, 0), (0, 0)))
    return {"in_t": jnp.concatenate([in_t[Q_LORA:Q_LORA + KV_LORA], kr_rows, in_t[:Q_LORA]], axis=0),
            "qb_t": qb_heads.reshape(MLA_HEADS * HEAD_PAD, Q_LORA), "kvb_t": kvb_all,
            "out": out_heads.reshape(MLA_HEADS * HEAD_PAD, d)}


def _layout_small(conv_w, conv_b, dt_bias, a_log, d_skip, ssm_norm_w, q_a_norm, kv_a_norm, q_norm, k_norm):
    lane_heads = lambda p: jnp.pad(p.reshape(SSM_GROUPS, SSM_HPG), ((0, 0), (0, LANES - SSM_HPG))).reshape(1, -1)
    pad_head = lambda p: jnp.pad(p.reshape(1, QK_DIM), ((0, 0), (0, HEAD_PAD - QK_DIM)))
    return {"conv_w": conv_w, "conv_b": conv_b, "dt_bias": lane_heads(dt_bias), "a_log": lane_heads(a_log),
            "d_skip": lane_heads(d_skip), "ssm_norm_w": ssm_norm_w, "q_a_norm": q_a_norm, "kv_a_norm": kv_a_norm,
            "q_norm": pad_head(q_norm), "k_norm": pad_head(k_norm)}


class _Plan:
    def __init__(self, ctx):
        self.ctx = ctx
        self.make = {}
        self.land = {}

    def ride(self, host, make, land):
        assert host not in self.make, host
        self.make[host] = make
        self.land[host] = land

    def comm(self, host):
        return self.make[host](self.ctx) if host in self.make else None

    def done(self, host, results):
        self.land[host](results, self.ctx)


def _local_step(x, positions, loss_target, ctx, plan=None):
    nb, seq, d = x.shape
    t = nb * seq
    xf = x.reshape(t, d)
    norm = ctx["norm"]
    tables = list(_rope_tables(positions.reshape(t, 1))) + [_swap_matrix()]
    x1, s_f0 = _ffn_fwd("ffn0", xf, norm[0, 0], ctx["ffn0"], plan)
    x2, s_ssm = _ssm_fwd(x1, norm[0, 1], ctx["ws"], ctx["small"], nb, seq, plan)
    x3, s_f1 = _ffn_fwd("ffn1", x2, norm[0, 2], ctx["ffn1"], plan)
    x4, s_f2 = _ffn_fwd("ffn2", x3, norm[1, 0], ctx["ffn2"], plan)
    x5, s_mla = _mla_fwd(x4, norm[1, 1], ctx["wm"], ctx["small"], tables, nb, seq, plan)
    x6, s_f3 = _ffn_fwd("ffn3", x5, norm[1, 2], ctx["ffn3"], plan)
    dy, loss_cols = _loss_and_grad(x6, loss_target.reshape(t, d))

    dx5, dn12, ctx["g_ffn3"] = _ffn_bwd("ffn3", dy, norm[1, 2], ctx["ffn3"], s_f3, plan)
    dx4, dn11, ctx["g_mla"] = _mla_bwd(dx5, norm[1, 1], ctx["wm"], ctx["small"], tables, s_mla, nb, seq)
    dx3, dn10, ctx["g_ffn2"] = _ffn_bwd("ffn2", dx4, norm[1, 0], ctx["ffn2"], s_f2, plan)
    dx2, dn02, ctx["g_ffn1"] = _ffn_bwd("ffn1", dx3, norm[0, 2], ctx["ffn1"], s_f1, plan)
    dx1, dn01, ctx["g_ssm"] = _ssm_bwd(dx2, norm[0, 1], ctx["ws"], ctx["small"], s_ssm, nb, seq, plan)
    dx0, dn00, ctx["g_ffn0"] = _ffn_bwd("ffn0", dx1, norm[0, 0], ctx["ffn0"], s_f0, plan)
    return loss_cols, dx0.reshape(nb, seq, d), (dn00, dn01, dn02, dn10, dn11, dn12)


def kernel(x, positions, norm_w, ffn_w_gate, ffn_w_up, ffn_w_down, ssm_w_in, ssm_conv_w, ssm_conv_b, ssm_dt_bias, ssm_a_log, ssm_d, ssm_norm_w, ssm_w_out, mla_w_in, mla_q_a_norm, mla_kv_a_norm, mla_w_q_b, mla_w_kv_b, mla_q_norm, mla_k_norm, mla_w_out, loss_target, m_norm_w, m_ffn_w_gate, m_ffn_w_up, m_ffn_w_down, m_ssm_w_in, m_ssm_conv_w, m_ssm_conv_b, m_ssm_dt_bias, m_ssm_a_log, m_ssm_d, m_ssm_norm_w, m_ssm_w_out, m_mla_w_in, m_mla_q_a_norm, m_mla_kv_a_norm, m_mla_w_q_b, m_mla_w_kv_b, m_mla_q_norm, m_mla_k_norm, m_mla_w_out, v_norm_w, v_ffn_w_gate, v_ffn_w_up, v_ffn_w_down, v_ssm_w_in, v_ssm_conv_w, v_ssm_conv_b, v_ssm_dt_bias, v_ssm_a_log, v_ssm_d, v_ssm_norm_w, v_ssm_w_out, v_mla_w_in, v_mla_q_a_norm, v_mla_kv_a_norm, v_mla_w_q_b, v_mla_w_kv_b, v_mla_q_norm, v_mla_k_norm, v_mla_w_out):
    nb, seq, d = x.shape
    t = nb * seq
    me = _flat(_mesh_pos())

    ffn_loc = _prep_ffn(ffn_w_gate, ffn_w_up, ffn_w_down)
    ssm_in_loc = _transpose_cast("prep_ssm_in", ssm_w_in[0], BF).reshape(SSM_IN_SHARD // 2, 16, LANES)
    ssm_out_loc = ssm_w_out[0].astype(BF)
    mla_in_loc, mla_out_loc = mla_w_in[0].astype(BF), mla_w_out[0].astype(BF)
    qb_loc = _transpose_cast("prep_q_b", mla_w_q_b[0], BF)
    kvb_loc = _transpose_cast("prep_kv_b", mla_w_kv_b[0], BF)
    small_loc = jnp.concatenate([norm_w.reshape(-1), ssm_conv_w.reshape(-1), mla_q_a_norm.reshape(-1),
                                 mla_kv_a_norm.reshape(-1)])
    small_loc = jnp.pad(small_loc, (0, SMALL_PACK - small_loc.shape[0])).reshape(SMALL_PACK // LANES, LANES)

    wloc = lambda n: [(ffn_loc[3 * n + k], 0) for k in range(3)]
    g0, u0, small_all = _Exchange("gather", wloc(0)[0:2] + [(small_loc, 0)]).run("gather_first")
    sm = small_all.reshape(N_DEV, SMALL_PACK)
    conv_w_full = sm[:, 768:768 + 2048].reshape(N_DEV, CONV_K, 512).transpose(1, 0, 2).reshape(CONV_K, CONV_DIM)
    ctx = {"ffn0": [g0, u0, None], "ffn1": [None] * 3, "ffn2": [None] * 3,
           "norm": sm[:, :768].reshape(N_DEV, 6, 128).transpose(1, 0, 2).reshape(2, 3, 1, d),
           "small": _layout_small(conv_w_full, ssm_conv_b, ssm_dt_bias, ssm_a_log, ssm_d, ssm_norm_w,
                                  sm[:, 2816:2864].reshape(1, Q_LORA), sm[:, 2864:2896].reshape(1, KV_LORA),
                                  mla_q_norm, mla_k_norm)}
    plan = _Plan(ctx)

    def gather_on(host, items, land):
        plan.ride(host, lambda c: _Exchange("gather", items), land)

    def put_w(key, ks):
        def land(r, c):
            for k, arr in zip(ks, r):
                c[key][k] = arr
        return land

    def land_first(r, c):
        c["ffn0"][2] = r[0]
        c["ssm_in_t"] = r[1].reshape(SSM_IN_DIM, d)

    gather_on("ffn0_up", wloc(0)[2:3] + [(ssm_in_loc, 0)], land_first)
    gather_on("ffn0_down", [(ssm_out_loc, 0)], lambda r, c: c.update(ws=_layout_ssm(c["ssm_in_t"], r[0])))
    gather_on("ssm_in_z", wloc(1)[0:1], put_w("ffn1", (0,)))
    gather_on("ssm_in_xbc", wloc(1)[1:2], put_w("ffn1", (1,)))
    gather_on("ssm_out", wloc(1)[2:3], put_w("ffn1", (2,)))
    gather_on("ffn1_up", wloc(2)[0:2], put_w("ffn2", (0, 1)))
    gather_on("ffn1_down", wloc(2)[2:3], put_w("ffn2", (2,)))
    gather_on("ffn2_up", [(mla_in_loc, 0), (qb_loc, 0), (kvb_loc, 0), (mla_out_loc, 0)],
              lambda r, c: c.update(wm=_layout_mla(r[0], r[1], r[2], r[3])))
    gather_on("mla_flash_fwd", wloc(3), lambda r, c: c.update(ffn3=tuple(r)))

    heads_of = lambda a: a.reshape(SSM_GROUPS, LANES, -1)[:, :SSM_HPG].reshape(SSM_HEADS, -1)

    def mla_grad_items(c):
        g = c["g_mla"]
        g_in = jnp.concatenate([g["in_cq_t"], g["in_ckv_t"], g["in_kr_t"][QK_NOPE:QK_DIM]], axis=0).T
        g_qb = g["qb_t"].reshape(MLA_HEADS, HEAD_PAD, Q_LORA)[:, :QK_DIM].reshape(MLA_HEADS * QK_DIM, Q_LORA)
        g_out = g["out"].reshape(MLA_HEADS, HEAD_PAD, d)[:, 64:].reshape(MLA_HEADS * 64, d)
        return [(a.astype(BF), 0) for a in (g_out, g_in, g_qb, g["kvb_t"])]

    def ssm_in_grad(c):
        g = c["g_ssm"]
        g_in_t = jnp.concatenate([g["z_t"], g["x_t"], g["b_t"], g["c_t"], heads_of(g["dt_t"])], axis=0)
        return [(g_in_t.astype(BF).reshape(SSM_IN_DIM // 2, 16, LANES), 0)]

    def scatter_on(host, items_of, keys):
        plan.ride(host, lambda c: _Exchange("scatter", items_of(c)),
                  lambda r, c: c.update(dict(zip(keys, r))))

    of = lambda key, ks: (lambda c: [(c[key][k], 0) for k in ks])
    scatter_on("ffn2_bwd_act", of("g_ffn3", (0, 1)), ("l3_gate", "l3_up"))
    scatter_on("ffn2_bwd_dh", of("g_ffn3", (2,)), ("l3_down",))
    scatter_on("ffn2_wg", mla_grad_items, ("l_mla_out", "l_mla_in", "l_qb", "l_kvb"))
    scatter_on("ffn1_bwd_act", of("g_ffn2", (0, 1)), ("l2_gate", "l2_up"))
    scatter_on("ffn1_bwd_dh", of("g_ffn2", (2,)), ("l2_down",))
    scatter_on("ssm_bwd_dh", of("g_ffn1", (0, 1, 2)), ("l1_gate", "l1_up", "l1_down"))
    scatter_on("ffn0_bwd_act", ssm_in_grad, ("l_ssm_in",))
    scatter_on("ffn0_bwd_dh", lambda c: [(c["g_ssm"]["out"], 0)], ("l_ssm_out",))
    scatter_on("ffn0_wu", lambda c: [(c["g_ffn0_gate"], 0)], ("l0_gate",))
    scatter_on("ffn0_wd", lambda c: [(c["g_ffn0_up"], 0)], ("l0_up",))

    loss_cols, grad_x, dns = _local_step(x, positions, loss_target, ctx, plan)
    loss = lax.psum(jnp.sum(loss_cols), ("x", "y", "c"))
    dn00, dn01, dn02, dn10, dn11, dn12 = dns
    g_ssm, g_mla = ctx["g_ssm"], ctx["g_mla"]
    ctx["l0_down"] = _Exchange("scatter", [(ctx["g_ffn0"][2], 0)]).run("scatter_last")[0]
    l_ffn = {k: jnp.stack([ctx["l%d_%s" % (n, k)] for n in range(4)], axis=1) for k in ("gate", "up", "down")}
    l_mla_out, l_mla_in, l_qb, l_kvb = (ctx[k] for k in ("l_mla_out", "l_mla_in", "l_qb", "l_kvb"))
    l_ssm_in, l_ssm_out = ctx["l_ssm_in"], ctx["l_ssm_out"]

    unlane = lambda a: a.reshape(SSM_GROUPS, LANES)[:, :SSM_HPG].reshape(1, SSM_HEADS)
    small_g = jnp.concatenate([
        jnp.concatenate([dn00, dn01, dn02, dn10, dn11, dn12], axis=0).reshape(-1),
        g_ssm["conv_w"].reshape(-1), g_ssm["conv_b"].reshape(-1), unlane(g_ssm["dt_bias"]).reshape(-1),
        unlane(g_ssm["a_log"]).reshape(-1), unlane(g_ssm["d_skip"]).reshape(-1), g_ssm["ssm_norm_w"].reshape(-1),
        g_mla["q_a_norm"].reshape(-1), g_mla["kv_a_norm"].reshape(-1), g_mla["q_norm"][0, :QK_DIM],
        g_mla["k_norm"][0, :QK_DIM]])
    n_small = small_g.shape[0]
    n_small_pad = -(-n_small // (8 * LANES)) * (8 * LANES)
    small_g = jnp.pad(small_g, (0, n_small_pad - n_small)).reshape(n_small_pad // LANES, LANES)
    gs = _Exchange("gather", [(small_g, 0)]).run("gather_small_grads")[0].reshape(N_DEV, n_small_pad)

    outs = {}

    def put(name, res, shape):
        for key, val in zip(("grad", "delta", "new_m", "new_v"), res):
            outs[(key, name)] = val.reshape(shape)

    ck = 256
    for key, name, w, m, v in (("gate", "ffn_w_gate", ffn_w_gate, m_ffn_w_gate, v_ffn_w_gate),
                               ("up", "ffn_w_up", ffn_w_up, m_ffn_w_up, v_ffn_w_up)):
        res = _adam("adam_" + name, l_ffn[key], (N_DEV, None, FF_SHARD, ck), lambda a, i: (0, a, 0, i),
                    w.reshape(4, d, FF_SHARD), m.reshape(4, d, FF_SHARD), v.reshape(4, d, FF_SHARD), True, ck)
        put(name, res, w.shape)
    res = _adam("adam_ffn_w_down", l_ffn["down"], (N_DEV, None, 176, d), lambda a, i: (0, a, i, 0),
                ffn_w_down.reshape(4, FF_SHARD, d), m_ffn_w_down.reshape(4, FF_SHARD, d),
                v_ffn_w_down.reshape(4, FF_SHARD, d), False, 176)
    put("ffn_w_down", res, ffn_w_down.shape)
    l_ssm_in2 = l_ssm_in.reshape(N_DEV, SSM_IN_SHARD, d)
    res = _adam("adam_ssm_w_in", l_ssm_in2, (N_DEV, SSM_IN_SHARD, 128), lambda a, i: (0, 0, i),
                ssm_w_in, m_ssm_w_in, v_ssm_w_in, True, 128)
    put("ssm_w_in", res, ssm_w_in.shape)
    res = _adam("adam_ssm_w_out", l_ssm_out, (N_DEV, 128, d), lambda a, i: (0, i, 0),
                ssm_w_out, m_ssm_w_out, v_ssm_w_out, False, 128)
    put("ssm_w_out", res, ssm_w_out.shape)
    res = _adam("adam_mla_w_in", l_mla_in, (N_DEV, 128, MLA_IN_DIM), lambda a, i: (0, 0, 0),
                mla_w_in, m_mla_w_in, v_mla_w_in, False, 128)
    put("mla_w_in", res, mla_w_in.shape)
    res = _adam("adam_mla_w_q_b", l_qb, (N_DEV, 192, 128), lambda a, i: (0, 0, i),
                mla_w_q_b, m_mla_w_q_b, v_mla_w_q_b, True, 128)
    put("mla_w_q_b", res, mla_w_q_b.shape)
    res = _adam("adam_mla_w_kv_b", l_kvb, (N_DEV, 256, 128), lambda a, i: (0, 0, i),
                mla_w_kv_b, m_mla_w_kv_b, v_mla_w_kv_b, True, 128)
    put("mla_w_kv_b", res, mla_w_kv_b.shape)
    res = _adam("adam_mla_w_out", l_mla_out, (N_DEV, 128, d), lambda a, i: (0, 0, 0),
                mla_w_out, m_mla_w_out, v_mla_w_out, False, 128)
    put("mla_w_out", res, mla_w_out.shape)

    small_params = (
        ("norm_w", norm_w, m_norm_w, v_norm_w, 6 * d, 6, 128), ("ssm_conv_w", ssm_conv_w, m_ssm_conv_w, v_ssm_conv_w,
                                                               CONV_K * CONV_DIM, CONV_K, 512),
        ("ssm_conv_b", ssm_conv_b, m_ssm_conv_b, v_ssm_conv_b, CONV_DIM, 0, 0),
        ("ssm_dt_bias", ssm_dt_bias, m_ssm_dt_bias, v_ssm_dt_bias, SSM_HEADS, 0, 0),
        ("ssm_a_log", ssm_a_log, m_ssm_a_log, v_ssm_a_log, SSM_HEADS, 0, 0),
        ("ssm_d", ssm_d, m_ssm_d, v_ssm_d, SSM_HEADS, 0, 0),
        ("ssm_norm_w", ssm_norm_w, m_ssm_norm_w, v_ssm_norm_w, D_INNER, 0, 0),
        ("mla_q_a_norm", mla_q_a_norm, m_mla_q_a_norm, v_mla_q_a_norm, Q_LORA, 1, 48),
        ("mla_kv_a_norm", mla_kv_a_norm, m_mla_kv_a_norm, v_mla_kv_a_norm, KV_LORA, 1, 32),
        ("mla_q_norm", mla_q_norm, m_mla_q_norm, v_mla_q_norm, QK_DIM, 0, 0),
        ("mla_k_norm", mla_k_norm, m_mla_k_norm, v_mla_k_norm, QK_DIM, 0, 0),
    )
    parts, ws_, ms_, vs_, off = [], [], [], [], 0
    for name, w, m, v, full, rows, shard in small_params:
        seg = gs[:, off:off + full]
        if rows:
            seg = _dyn(seg.reshape(N_DEV, rows, full // rows), me * shard, shard).reshape(N_DEV, rows * shard)
        parts.append(seg)
        ws_.append(w.reshape(1, -1))
        ms_.append(m.reshape(1, -1))
        vs_.append(v.reshape(1, -1))
        off += full
    n_loc = sum(p.shape[1] for p in parts)
    n_loc_pad = -(-n_loc // LANES) * LANES
    padc = lambda a, val=0.0: jnp.pad(jnp.concatenate(a, axis=1), ((0, 0), (0, n_loc_pad - n_loc)),
                                      constant_values=val)
    res = _adam("adam_small", padc(parts).reshape(N_DEV, 1, n_loc_pad), (N_DEV, 1, n_loc_pad), lambda a, i: (0, 0, 0),
                padc(ws_).reshape(1, 1, n_loc_pad), padc(ms_).reshape(1, 1, n_loc_pad),
                padc(vs_, 1.0).reshape(1, 1, n_loc_pad), False, 1)
    off = 0
    for name, w, m, v, full, rows, shard in small_params:
        nloc = w.size
        put(name, [r.reshape(-1)[off:off + nloc] for r in res], w.shape)
        off += nloc

    order = ("norm_w", "ffn_w_gate", "ffn_w_up", "ffn_w_down", "ssm_w_in", "ssm_conv_w", "ssm_conv_b", "ssm_dt_bias",
             "ssm_a_log", "ssm_d", "ssm_norm_w", "ssm_w_out", "mla_w_in", "mla_q_a_norm", "mla_kv_a_norm",
             "mla_w_q_b", "mla_w_kv_b", "mla_q_norm", "mla_k_norm", "mla_w_out")
    return (loss, grad_x, *[outs[(k, n)] for k in ("grad", "delta", "new_m", "new_v") for n in order])
```

```python
import functools
import math

import jax
import jax.numpy as jnp
import numpy as np
from jax import lax
from jax.experimental import pallas as pl
from jax.experimental.pallas import tpu as pltpu

F32 = jnp.float32
BF = jnp.bfloat16
SDS = jax.ShapeDtypeStruct

N_DEV = 8
D_MODEL = 1024
D_FF = 2816
FF_SHARD = D_FF // N_DEV
D_INNER = 2048
SSM_HEADS = 32
SSM_GROUPS = 8
SSM_HPG = 4
SSM_STATE = 128
CONV_K = 4
CONV_DIM = 4096
SSM_IN_DIM = 6176
SSM_IN_SHARD = SSM_IN_DIM // N_DEV
NORM_GROUP = 256
CHUNK = 128
MLA_HEADS = 16
Q_LORA = 384
KV_LORA = 256
QK_NOPE = 64
QK_ROPE = 32
QK_DIM = 96
MLA_IN_DIM = 672
HEAD_PAD = 128
ROPE_THETA = 10000.0
EPS = 1e-6
LANES = 128

ADAM_LR = 0.001
ADAM_B1 = 0.9
ADAM_B2 = 0.999
ADAM_EPS = 1e-08
ADAM_WD = 0.01
ADAM_STEP = 10

VMEM_BIG = 56 * 1024 * 1024

NN = ((1,), (0,))
NT = ((1,), (1,))
TN = ((0,), (0,))


def _dotf(a, b, dn):
    return lax.dot_general(a.astype(BF), b.astype(BF), (dn, ((), ())), preferred_element_type=F32)


def _dot_hi(a, b, dn=NN):
    return lax.dot_general(a, b, (dn, ((), ())), precision=lax.Precision.HIGHEST, preferred_element_type=F32)


def _sigmoid(x):
    return jax.nn.sigmoid(x)


def _silu(x):
    return x * _sigmoid(x)


def _softplus(x):
    return jnp.maximum(x, 0.0) + jnp.log(1.0 + jnp.exp(-jnp.abs(x)))


def _cparams(n_grid, vmem=None):
    return pltpu.CompilerParams(dimension_semantics=("arbitrary",) * n_grid, vmem_limit_bytes=vmem)


def _fmm(name, grid_mn, pairs, outs, *, epi=None, extras=(), n_acc=1, acc_shape=None, vmem=None, alias=None,
         joint=False, hosts=None, row_split=1):
    comm = hosts.comm(name) if hosts is not None else None
    if joint:
        nk_total = pairs[0][7]
        assert all(p[7] == nk_total for p in pairs)
        starts = [0] * len(pairs)
    else:
        nk_total = sum(p[7] for p in pairs)
        starts = []
        s = 0
        for p in pairs:
            starts.append(s)
            s += p[7]
    n_pairs, n_extras, n_outs = len(pairs), len(extras), len(outs)
    single = nk_total == 1

    n_ci = len(comm.arrays) if comm is not None else 0
    n_co = len(comm.out_shapes) if comm is not None else 0
    n_scratch_acc = 0 if single else n_acc

    def body(*refs):
        ab_refs = refs[: 2 * n_pairs]
        e_refs = refs[2 * n_pairs: 2 * n_pairs + n_extras]
        pos = 2 * n_pairs + n_extras + (1 if alias is not None else 0)
        ci_refs = refs[pos: pos + n_ci]
        pos += n_ci
        o_refs = refs[pos: pos + n_outs]
        co_refs = refs[pos + n_outs: pos + n_outs + n_co]
        pos += n_outs + n_co
        acc_refs = refs[pos: pos + n_scratch_acc]
        sem_refs = refs[pos + n_scratch_acc:]
        i, j, k = pl.program_id(0), pl.program_id(1), pl.program_id(2)

        if comm is not None:
            @pl.when((i == 0) & (j == 0) & (k == 0))
            def _():
                comm.start(ci_refs, co_refs, sem_refs)

        compute(ab_refs, e_refs, o_refs, acc_refs, i, j, k)

        if comm is not None:
            @pl.when((i == grid_mn[0] - 1) & (j == grid_mn[1] - 1) & (k == nk_total - 1))
            def _():
                comm.wait(ci_refs, co_refs, sem_refs)

    def compute(ab_refs, e_refs, o_refs, acc_refs, i, j, k):

        def finish(accs, rows=slice(None)):
            res = epi(accs, *[e[rows] for e in e_refs]) if epi is not None else accs
            if not isinstance(res, (tuple, list)):
                res = (res,)
            first = (i == 0) & (j == 0)
            for o, r, spec in zip(o_refs, res, outs):
                if spec[3]:
                    @pl.when(first)
                    def _(o=o, r=r):
                        o[...] = r.astype(o.dtype)

                    @pl.when(jnp.logical_not(first))
                    def _(o=o, r=r):
                        o[...] += r.astype(o.dtype)
                else:
                    o[rows] = r.astype(o.dtype)

        if single:
            tm_all = ab_refs[0].shape[0]
            ch = tm_all // row_split
            for c in range(row_split):
                rows = slice(c * ch, (c + 1) * ch) if row_split > 1 else slice(None)
                accs = [None] * n_acc
                for p, pr in enumerate(pairs):
                    d = _dotf(ab_refs[2 * p][rows], ab_refs[2 * p + 1][...], pr[6])
                    accs[pr[8]] = d if accs[pr[8]] is None else accs[pr[8]] + d
                finish(accs, rows)
            return

        @pl.when(k == 0)
        def _():
            for a in acc_refs:
                a[...] = jnp.zeros_like(a)

        for p, pr in enumerate(pairs):
            def step(p=p, pr=pr):
                acc_refs[pr[8]][...] += _dotf(ab_refs[2 * p][...], ab_refs[2 * p + 1][...], pr[6])

            if n_pairs == 1 or joint:
                step()
            else:
                pl.when((k >= starts[p]) & (k < starts[p] + pr[7]))(step)

        @pl.when(k == nk_total - 1)
        def _():
            finish([a[...] for a in acc_refs])

    in_specs, args = [], []
    for p, pr in enumerate(pairs):
        a, a_blk, a_idx, b, b_blk, b_idx, _, nk, _ = pr
        st = starts[p]

        def amap(i, j, k, a_idx=a_idx, st=st, nk=nk):
            return a_idx(i, j, jnp.clip(k - st, 0, nk - 1))

        def bmap(i, j, k, b_idx=b_idx, st=st, nk=nk):
            return b_idx(i, j, jnp.clip(k - st, 0, nk - 1))

        in_specs += [pl.BlockSpec(a_blk, amap), pl.BlockSpec(b_blk, bmap)]
        args += [a, b]
    for arr, blk, idx in extras:
        in_specs.append(pl.BlockSpec(blk, lambda i, j, k, idx=idx: idx(i, j)))
        args.append(arr)
    io_alias = {}
    if alias is not None:
        in_specs.append(pl.BlockSpec(memory_space=pl.ANY))
        io_alias = {len(args): 0}
        args.append(alias)
    out_specs = [pl.BlockSpec(blk, lambda i, j, k, idx=idx: idx(i, j)) for _, blk, idx, _ in outs]
    out_shape = [o[0] for o in outs]
    scratch = [] if single else [pltpu.VMEM(acc_shape, F32) for _ in range(n_acc)]
    if comm is not None:
        hbm = pl.BlockSpec(memory_space=pl.ANY)
        in_specs += [hbm] * n_ci
        args += list(comm.arrays)
        out_specs += [hbm] * n_co
        out_shape += list(comm.out_shapes)
        scratch += comm.sems
    res = pl.pallas_call(
        body, name=name, grid=(grid_mn[0], grid_mn[1], nk_total), in_specs=in_specs, out_specs=out_specs,
        out_shape=out_shape, scratch_shapes=scratch, input_output_aliases=io_alias,
        compiler_params=_cparams(3, vmem),
    )(*args)
    if comm is not None:
        hosts.done(name, res[n_outs:])
    return res[:n_outs]


def _ew(name, grid, fn, ins, outs, *, acc_axes=(), vmem=None):
    n_in = len(ins)

    def body(*refs):
        res = fn(*[r[...] for r in refs[:n_in]])
        if not isinstance(res, (tuple, list)):
            res = (res,)
        first = None
        for ax in acc_axes:
            c = pl.program_id(ax) == 0
            first = c if first is None else (first & c)
        for o, r, spec in zip(refs[n_in:], res, outs):
            if spec[3]:
                @pl.when(first)
                def _(o=o, r=r):
                    o[...] = r.astype(o.dtype)

                @pl.when(jnp.logical_not(first))
                def _(o=o, r=r):
                    o[...] += r.astype(o.dtype)
            else:
                o[...] = r.astype(o.dtype)

    return pl.pallas_call(
        body, name=name, grid=grid,
        in_specs=[pl.BlockSpec(blk, idx) for _, blk, idx in ins],
        out_specs=[pl.BlockSpec(blk, idx) for _, blk, idx, _ in outs],
        out_shape=[o[0] for o in outs],
        compiler_params=_cparams(len(grid), vmem),
    )(*[a for a, _, _ in ins])


def _row_tile(t, want):
    tm = min(want, t)
    assert t % tm == 0, (t, tm)
    return tm


def _rms_fn(x, w):
    return x * lax.rsqrt(jnp.mean(x * x, axis=-1, keepdims=True) + EPS) * w


def _rms_fwd(name, x, w):
    t, d = x.shape
    tm = _row_tile(t, 512)
    return _ew(name, (t // tm,), _rms_fn,
               [(x, (tm, d), lambda i: (i, 0)), (w, (1, d), lambda i: (0, 0))],
               [(SDS((t, d), BF), (tm, d), lambda i: (i, 0), False)])[0]


def _rms_bwd(name, x, w, dh, dres):
    t, d = x.shape
    tm = _row_tile(t, 512)

    def fn(x, w, dh, dres):
        _, vjp = jax.vjp(_rms_fn, x, w)
        dx, dw = vjp(dh.astype(F32))
        return dx + dres, dw

    row = lambda i: (i, 0)
    return _ew(name, (t // tm,), fn,
               [(x, (tm, d), row), (w, (1, d), lambda i: (0, 0)), (dh, (tm, d), row), (dres, (tm, d), row)],
               [(SDS((t, d), F32), (tm, d), row, False), (SDS((1, d), F32), (1, d), lambda i: (0, 0), True)],
               acc_axes=(0,))


def _loss_and_grad(y, target):
    t, d = y.shape
    tm = _row_tile(t, 512)

    def fn(y, tg):
        e = y - tg
        return e * (1.0 / d), jnp.sum(e * e, axis=0, keepdims=True) * (0.5 / d)

    row = lambda i: (i, 0)
    return _ew("loss_head", (t // tm,), fn, [(y, (tm, d), row), (target, (tm, d), row)],
               [(SDS((t, d), F32), (tm, d), row, False), (SDS((1, d), F32), (1, d), lambda i: (0, 0), True)],
               acc_axes=(0,))


def _ffn_fwd(tag, x, nw, wf, hosts=None):
    gate_t, up_t = wf[0], wf[1]
    t, d = x.shape
    h = _rms_fwd(tag + "_rms", x, nw)
    tm, tn = _row_tile(t, 256), D_FF

    def epi(accs):
        g, u = accs
        s = _sigmoid(g)
        sg = g * s
        return 0.5 * sg, 0.5 * (s * (1.0 + g * (1.0 - s))), u, sg * u

    hblk = (h, (tm, d), lambda i, j, k: (i, 0))
    col = lambda i, j: (i, j)
    tblk = lambda w: (w, (tn, d), lambda i, j, k: (j, 0), NT, 1)
    sgh, dsgh, u, a = _fmm(
        tag + "_up", (t // tm, D_FF // tn), [hblk + tblk(gate_t) + (0,), hblk + tblk(up_t) + (1,)],
        [(SDS((t, D_FF), BF), (tm, tn), col, False)] * 4, epi=epi, n_acc=2, joint=True, hosts=hosts, vmem=VMEM_BIG)
    down = wf[2]
    tm2 = _row_tile(t, 512)
    y = _fmm(
        tag + "_down", (t // tm2, 1),
        [(a, (tm2, D_FF), lambda i, j, k: (i, 0), down, (D_FF, d), lambda i, j, k: (0, 0), NN, 1, 0)],
        [(SDS((t, d), F32), (tm2, d), lambda i, j: (i, 0), False)],
        extras=[(x, (tm2, d), lambda i, j: (i, 0))],
        epi=lambda accs, xr: xr + 0.5 * accs[0], vmem=VMEM_BIG, hosts=hosts)[0]
    return y, (x, h, sgh, dsgh, u, a)


def _wgrad(name, a, b, m, n, *, tm, tn, tk=512, scale=None, out_dtype=BF, hosts=None):
    t = a.shape[0]
    tk = _row_tile(t, tk)
    epi = (lambda accs: accs[0] * scale) if scale is not None else None
    return _fmm(name, (m // tm, n // tn),
                [(a, (tk, tm), lambda i, j, k: (k, i), b, (tk, tn), lambda i, j, k: (k, j), TN, t // tk, 0)],
                [(SDS((m, n), out_dtype), (tm, tn), lambda i, j: (i, j), False)], epi=epi, acc_shape=(tm, tn),
                vmem=VMEM_BIG, hosts=hosts)[0]


def _ffn_bwd(tag, dy, nw, wf, saved, hosts=None):
    gate_t, up_t, down = wf
    x, h, sgh, dsgh, u, a = saved
    t, d = x.shape
    tm, tn = _row_tile(t, 256), D_FF

    def epi(accs, sgh, dsgh, u):
        da = accs[0]
        return da * (u.astype(F32) * dsgh.astype(F32)), da * sgh.astype(F32)

    col = lambda i, j: (i, j)
    dg, du = _fmm(
        tag + "_bwd_act", (t // tm, D_FF // tn),
        [(dy, (tm, d), lambda i, j, k: (i, 0), down, (tn, d), lambda i, j, k: (j, 0), NT, 1, 0)],
        [(SDS((t, D_FF), BF), (tm, tn), col, False)] * 2,
        extras=[(sgh, (tm, tn), col), (dsgh, (tm, tn), col), (u, (tm, tn), col)], epi=epi, hosts=hosts,
        vmem=VMEM_BIG)
    tm2 = _row_tile(t, 256)
    full = lambda i, j, k: (0, 0)
    dh = _fmm(
        tag + "_bwd_dh", (t // tm2, 1),
        [(dg, (tm2, D_FF), lambda i, j, k: (i, 0), gate_t, (D_FF, d), full, NN, 1, 0),
         (du, (tm2, D_FF), lambda i, j, k: (i, 0), up_t, (D_FF, d), full, NN, 1, 0)],
        [(SDS((t, d), BF), (tm2, d), lambda i, j: (i, 0), False)], vmem=VMEM_BIG, joint=True, hosts=hosts)[0]
    dx, dnw = _rms_bwd(tag + "_bwd_rms", x, nw, dh, dy)
    half = D_FF // 2
    g_gate = _wgrad(tag + "_wg", dg, h, D_FF, d, tm=half, tn=d, hosts=hosts)
    if hosts is not None:
        hosts.ctx["g_" + tag + "_gate"] = g_gate
    g_up = _wgrad(tag + "_wu", du, h, D_FF, d, tm=half, tn=d, hosts=hosts)
    if hosts is not None:
        hosts.ctx["g_" + tag + "_up"] = g_up
    g_down = _wgrad(tag + "_wd", a, dy, D_FF, d, tm=half, tn=d, scale=0.5, hosts=hosts)
    return dx, dnw, (g_gate, g_up, g_down)


def _shift_down(cur, prev8, j):
    rolled = pltpu.roll(cur, j, 0)
    sub = lax.broadcasted_iota(jnp.int32, prev8.shape, 0)
    top = jnp.where(sub < j, pltpu.roll(prev8, j, 0), rolled[:8])
    return jnp.concatenate([top, rolled[8:]], axis=0)


def _shift_up(cur, next8, j):
    n = cur.shape[0]
    rolled = pltpu.roll(cur, n - j, 0)
    sub = lax.broadcasted_iota(jnp.int32, next8.shape, 0)
    bot = jnp.where(sub >= 8 - j, pltpu.roll(next8, 8 - j, 0), rolled[n - 8:])
    return jnp.concatenate([rolled[: n - 8], bot], axis=0)


HALO = 16


def _conv_fwd(xbc, w, b, seq):
    t, c = xbc.shape
    ts, tc = _row_tile(seq, 256), 512
    tiles_per_seq = seq // ts
    hb = ts // HALO

    def fn(cur, prev, w, b):
        i = pl.program_id(1)
        cur = cur.astype(F32)
        prev8 = jnp.where(i % tiles_per_seq == 0, 0.0, prev.astype(F32)[HALO - 8:])
        out = b + w[3:4] * cur
        for j in range(1, CONV_K):
            out = out + w[3 - j:4 - j] * _shift_down(cur, prev8, j)
        return out, _silu(out)

    return _ew("ssm_conv_fwd", (c // tc, t // ts), fn,
               [(xbc, (ts, tc), lambda j, i: (i, j)),
                (xbc, (HALO, tc), lambda j, i: (jnp.maximum(i * hb - 1, 0), j)),
                (w, (CONV_K, tc), lambda j, i: (0, j)), (b, (1, tc), lambda j, i: (0, j))],
               [(SDS((t, c), BF), (ts, tc), lambda j, i: (i, j), False)] * 2)


def _conv_bwd(tag, dxa, cpre, xbc, w, col0, seq):
    t, width = dxa.shape
    ts, tc = _row_tile(seq, 256), 512
    tiles_per_seq = seq // ts
    hb = ts // HALO
    cb0 = col0 // tc
    n_halo_blocks = t // HALO

    def dsilu(cv, dv):
        cv = cv.astype(F32)
        s = _sigmoid(cv)
        return dv.astype(F32) * (s * (1.0 + cv * (1.0 - s)))

    def fn(dxa_c, dxa_n, c_c, c_n, x_c, x_p, w):
        i = pl.program_id(1)
        dc = dsilu(c_c, dxa_c)
        last = i % tiles_per_seq == tiles_per_seq - 1
        dc_n = jnp.where(last, 0.0, dsilu(c_n, dxa_n)[:8])
        dx = w[3:4] * dc
        for j in range(1, CONV_K):
            dx = dx + w[3 - j:4 - j] * _shift_up(dc, dc_n, j)
        cur = x_c.astype(F32)
        prev8 = jnp.where(i % tiles_per_seq == 0, 0.0, x_p.astype(F32)[HALO - 8:])
        rows = [jnp.sum(dc * cur, axis=0, keepdims=True)]
        for j in range(1, CONV_K):
            rows.append(jnp.sum(dc * _shift_down(cur, prev8, j), axis=0, keepdims=True))
        sub8 = lax.broadcasted_iota(jnp.int32, (8, dc.shape[1]), 0)
        dw = jnp.zeros((8, dc.shape[1]), F32)
        for kk in range(CONV_K):
            dw = jnp.where(sub8 == kk, rows[CONV_K - 1 - kk], dw)
        return dx, dw, jnp.sum(dc, axis=0, keepdims=True)

    nxt = lambda j, i: (jnp.minimum((i + 1) * hb, n_halo_blocks - 1), j)
    nxt_off = lambda j, i: (jnp.minimum((i + 1) * hb, n_halo_blocks - 1), j + cb0)
    return _ew(tag, (width // tc, t // ts), fn,
               [(dxa, (ts, tc), lambda j, i: (i, j)), (dxa, (HALO, tc), nxt),
                (cpre, (ts, tc), lambda j, i: (i, j + cb0)), (cpre, (HALO, tc), nxt_off),
                (xbc, (ts, tc), lambda j, i: (i, j + cb0)),
                (xbc, (HALO, tc), lambda j, i: (jnp.maximum(i * hb - 1, 0), j + cb0)),
                (w, (CONV_K, tc), lambda j, i: (0, j + cb0))],
               [(SDS((t, width), BF), (ts, tc), lambda j, i: (i, j), False),
                (SDS((8, width), F32), (8, tc), lambda j, i: (0, j), True),
                (SDS((1, width), F32), (1, tc), lambda j, i: (0, j), True)],
               acc_axes=(1,))


def _ssd_chunk(xs, bm, cm, dtr, st, dtb, alog, dsk):
    ell = xs.shape[0]
    xs = xs.astype(F32)
    lane = lax.broadcasted_iota(jnp.int32, (ell, LANES), 1)
    sub = lax.broadcasted_iota(jnp.int32, (ell, LANES), 0)
    lane1 = lax.broadcasted_iota(jnp.int32, (1, LANES), 1)
    causal = sub >= lane
    dt = _softplus(dtr + dtb)
    da = dt * (-jnp.exp(alog))
    acs = _dot_hi(causal.astype(F32), da)
    acs_t = acs.T
    cb = _dotf(cm, bm, NT)
    lo = lane < 64
    ys, news = [], []
    for p in range(2):
        xp = xs[:, LANES * p:LANES * (p + 1)]
        sp = st[LANES * p:LANES * (p + 1), :]
        col, dtc, last, dsel = [], [], [], []
        y_diag = None
        for q in range(2):
            r = 2 * p + q
            col_r = jnp.sum(jnp.where(lane == r, acs, 0.0), axis=1, keepdims=True)
            row_r = jnp.sum(jnp.where(sub == r, acs_t, 0.0), axis=0, keepdims=True)
            dtc_r = jnp.sum(jnp.where(lane == r, dt, 0.0), axis=1, keepdims=True)
            decay = jnp.exp(jnp.where(causal, col_r - row_r, -jnp.inf))
            head = lo if q == 0 else jnp.logical_not(lo)
            d = _dotf(cb * decay, jnp.where(head, xp * dtc_r, 0.0), NN)
            y_diag = d if y_diag is None else y_diag + d
            col.append(col_r)
            dtc.append(dtc_r)
            last.append(jnp.sum(jnp.where(sub[:, :1] == ell - 1, col_r, 0.0), axis=0, keepdims=True))
            dsel.append(jnp.sum(jnp.where(lane1 == r, dsk, 0.0), axis=1, keepdims=True))
        y_off = _dotf(cm, sp, NT) * jnp.where(lo, jnp.exp(col[0]), jnp.exp(col[1]))
        xw = jnp.where(lo, xp * (dtc[0] * jnp.exp(last[0] - col[0])), xp * (dtc[1] * jnp.exp(last[1] - col[1])))
        new = sp * jnp.where(sub < 64, jnp.exp(last[0]), jnp.exp(last[1])) + _dotf(xw, bm, TN)
        ys.append(y_diag + y_off + jnp.where(lo, dsel[0], dsel[1]) * xp)
        news.append(new)
    return jnp.concatenate(ys, axis=1), jnp.concatenate(news, axis=0)


SSD_GP = 4
SSD_GP_BWD = 4
XW, GW = 2 * LANES * SSD_GP, LANES * SSD_GP


def _grp(ref, q, width):
    return ref[:, width * q:width * (q + 1)]


def _ssd_fwd(xa, dtr, dtb, alog, dsk, nb, seq):
    t = xa.shape[0]
    nc = seq // CHUNK
    row = lambda g, b, c: (b * nc + c, g)
    par = pl.BlockSpec((1, GW), lambda g, b, c: (0, g))
    b_off, c_off = D_INNER // GW, (D_INNER + SSM_GROUPS * SSM_STATE) // GW

    def body(xs, bm, cm, dtr, dtb, alog, dsk, y_ref, st_out, st_ref):
        @pl.when(pl.program_id(2) == 0)
        def _():
            st_ref[...] = jnp.zeros_like(st_ref)

        ins = [(_grp(xs, q, 2 * LANES), _grp(bm, q, LANES), _grp(cm, q, LANES), _grp(dtr, q, LANES), st_ref[q],
                _grp(dtb, q, LANES), _grp(alog, q, LANES), _grp(dsk, q, LANES)) for q in range(SSD_GP)]
        res = [_ssd_chunk(*a) for a in ins]
        for q in range(SSD_GP):
            st_out[q] = ins[q][4]
            y_ref[:, 2 * LANES * q:2 * LANES * (q + 1)] = res[q][0]
            st_ref[q] = res[q][1]

    specs = [pl.BlockSpec((CHUNK, XW), row),
             pl.BlockSpec((CHUNK, GW), lambda g, b, c: (b * nc + c, b_off + g)),
             pl.BlockSpec((CHUNK, GW), lambda g, b, c: (b * nc + c, c_off + g)),
             pl.BlockSpec((CHUNK, GW), row), par, par, par]
    return pl.pallas_call(
        body, name="ssd_fwd", grid=(SSM_GROUPS // SSD_GP, nb, nc), in_specs=specs,
        out_specs=[pl.BlockSpec((CHUNK, XW), row),
                   pl.BlockSpec((SSD_GP, None, None, 2 * LANES, LANES), lambda g, b, c: (g, b, c, 0, 0))],
        out_shape=[SDS((t, D_INNER), F32), SDS((SSM_GROUPS, nb, nc, 2 * LANES, LANES), F32)],
        scratch_shapes=[pltpu.VMEM((SSD_GP, 2 * LANES, LANES), F32)],
        compiler_params=_cparams(3),
    )(xa, xa, xa, dtr, dtb, alog, dsk)


def _ssd_bwd(xa, dtr, dtb, alog, dsk, states, dy, nb, seq):
    t = xa.shape[0]
    nc = seq // CHUNK
    rev = lambda c: nc - 1 - c
    row = lambda g, b, c: (b * nc + rev(c), g)
    gp = SSD_GP_BWD
    xw, gw = 2 * LANES * gp, LANES * gp
    par = pl.BlockSpec((1, gw), lambda g, b, c: (0, g))
    b_off, c_off = D_INNER // gw, (D_INNER + SSM_GROUPS * SSM_STATE) // gw

    def body(xs, bm, cm, dtr, dtb, alog, dsk, st_in, dy, dxs, dbm, dcm, ddtr, ddtb, dalog, ddsk, dst_ref):
        @pl.when(pl.program_id(2) == 0)
        def _():
            dst_ref[...] = jnp.zeros_like(dst_ref)

        ins = [(_grp(xs, q, 2 * LANES), _grp(bm, q, LANES), _grp(cm, q, LANES), _grp(dtr, q, LANES), st_in[q],
                _grp(dtb, q, LANES), _grp(alog, q, LANES), _grp(dsk, q, LANES)) for q in range(gp)]
        cts = [(_grp(dy, q, 2 * LANES), dst_ref[q]) for q in range(gp)]
        gs = [jax.vjp(_ssd_chunk, *a)[1](ct) for a, ct in zip(ins, cts)]
        for q, g in enumerate(gs):
            dxs[:, 2 * LANES * q:2 * LANES * (q + 1)] = g[0]
            lanes = slice(LANES * q, LANES * (q + 1))
            dbm[:, lanes] = g[1]
            dcm[:, lanes] = g[2]
            ddtr[:, lanes] = g[3]
            dst_ref[q] = g[4]
        first = (pl.program_id(1) == 0) & (pl.program_id(2) == 0)
        for o, k in ((ddtb, 5), (dalog, 6), (ddsk, 7)):
            v = jnp.concatenate([g[k] for g in gs], axis=1)

            @pl.when(first)
            def _(o=o, v=v):
                o[...] = v

            @pl.when(jnp.logical_not(first))
            def _(o=o, v=v):
                o[...] += v

    in_specs = [
        pl.BlockSpec((CHUNK, xw), row),
        pl.BlockSpec((CHUNK, gw), lambda g, b, c: (b * nc + rev(c), b_off + g)),
        pl.BlockSpec((CHUNK, gw), lambda g, b, c: (b * nc + rev(c), c_off + g)),
        pl.BlockSpec((CHUNK, gw), row), par, par, par,
        pl.BlockSpec((gp, None, None, 2 * LANES, LANES), lambda g, b, c: (g, b, rev(c), 0, 0)),
        pl.BlockSpec((CHUNK, xw), row),
    ]
    out_specs = [pl.BlockSpec((CHUNK, xw), row), pl.BlockSpec((CHUNK, gw), row), pl.BlockSpec((CHUNK, gw), row),
                 pl.BlockSpec((CHUNK, gw), row), par, par, par]
    out_shape = [SDS((t, D_INNER), BF), SDS((t, SSM_GROUPS * LANES), BF), SDS((t, SSM_GROUPS * LANES), BF),
                 SDS((t, SSM_GROUPS * LANES), F32)] + [SDS((1, SSM_GROUPS * LANES), F32)] * 3
    return pl.pallas_call(
        body, name="ssd_bwd", grid=(SSM_GROUPS // gp, nb, nc), in_specs=in_specs, out_specs=out_specs,
        out_shape=out_shape, scratch_shapes=[pltpu.VMEM((gp, 2 * LANES, LANES), F32)],
        compiler_params=_cparams(3, VMEM_BIG),
    )(xa, xa, xa, dtr, dtb, alog, dsk, states, dy)


def _gated_fn(y, z, w):
    g = y * _silu(z.astype(F32))
    return g * lax.rsqrt(jnp.mean(g * g, axis=-1, keepdims=True) + EPS) * w


def _gated_norm_fwd(y, z, w):
    t = y.shape[0]
    tm = _row_tile(t, 512)
    blk = ((tm, NORM_GROUP), lambda g, i: (i, g))
    return _ew("ssm_gnorm_fwd", (SSM_GROUPS, t // tm), _gated_fn,
               [(y,) + blk, (z,) + blk, (w, (1, NORM_GROUP), lambda g, i: (0, g))],
               [(SDS((t, D_INNER), BF),) + blk + (False,)])[0]


def _gated_norm_bwd(y, z, w, dout):
    t = y.shape[0]
    tm = _row_tile(t, 512)
    blk = ((tm, NORM_GROUP), lambda g, i: (i, g))
    par = ((1, NORM_GROUP), lambda g, i: (0, g))

    def fn(y, z, w, dout):
        _, vjp = jax.vjp(_gated_fn, y, z, w)
        return vjp(dout.astype(F32))

    return _ew("ssm_gnorm_bwd", (SSM_GROUPS, t // tm), fn,
               [(y,) + blk, (z,) + blk, (w,) + par, (dout,) + blk],
               [(SDS((t, D_INNER), F32),) + blk + (False,), (SDS((t, D_INNER), BF),) + blk + (False,),
                (SDS((1, D_INNER), F32),) + par + (True,)],
               acc_axes=(1,))


def _proj_nt(name, h, wt, n, out_dtype, tn=256, hosts=None):
    t, kdim = h.shape
    tm = _row_tile(t, 1024)
    return _fmm(name, (t // tm, n // tn),
                [(h, (tm, kdim), lambda i, j, k: (i, 0), wt, (tn, kdim), lambda i, j, k: (j, 0), NT, 1, 0)],
                [(SDS((t, n), out_dtype), (tm, tn), lambda i, j: (i, j), False)], hosts=hosts)[0]


def _seg_nn(name, parts, n, out_dtype, tk=256, hosts=None):
    t = parts[0][0].shape[0]
    tm = _row_tile(t, 512)
    pairs = []
    for a, w, row0 in parts:
        kp = a.shape[1]
        tkp = min(tk, kp)
        r0 = row0 // tkp
        pairs.append((a, (tm, tkp), lambda i, j, k: (i, k), w, (tkp, n), lambda i, j, k, r0=r0: (k + r0, 0),
                      NN, kp // tkp, 0))
    return _fmm(name, (t // tm, 1), pairs, [(SDS((t, n), out_dtype), (tm, n), lambda i, j: (i, 0), False)],
                acc_shape=(tm, n), hosts=hosts)[0]


def _ssm_fwd(x, nw, ws, small, nb, seq, hosts=None):
    t, d = x.shape
    h = _rms_fwd("ssm_rms", x, nw)
    z = _proj_nt("ssm_in_z", h, ws["z_t"], D_INNER, BF, hosts=hosts)
    xbc = _proj_nt("ssm_in_xbc", h, ws["xbc_t"], CONV_DIM, BF, hosts=hosts)
    dtr = _proj_nt("ssm_in_dt", h, ws["dt_t"], SSM_GROUPS * LANES, F32)
    cpre, xa = _conv_fwd(xbc, small["conv_w"], small["conv_b"], seq)
    y, states = _ssd_fwd(xa, dtr, small["dt_bias"], small["a_log"], small["d_skip"], nb, seq)
    gn = _gated_norm_fwd(y, z, small["ssm_norm_w"])
    tm = _row_tile(t, 512)
    out = _fmm("ssm_out", (t // tm, 1),
               [(gn, (tm, D_INNER), lambda i, j, k: (i, 0), ws["out"], (D_INNER, d), lambda i, j, k: (0, 0), NN, 1, 0)],
               [(SDS((t, d), F32), (tm, d), lambda i, j: (i, 0), False)],
               extras=[(x, (tm, d), lambda i, j: (i, 0))], epi=lambda accs, xr: xr + accs[0], hosts=hosts)[0]
    return out, (x, h, z, xbc, dtr, cpre, xa, y, states, gn)


def _ssm_bwd(dy, nw, ws, small, saved, nb, seq, hosts=None):
    x, h, z, xbc, dtr, cpre, xa, y, states, gn = saved
    t, d = x.shape
    dgn = _proj_nt("ssm_bwd_dgn", dy, ws["out"], D_INNER, BF, hosts=hosts)
    d_out = _wgrad("ssm_w_out_g", gn, dy, D_INNER, d, tm=D_INNER // 2, tn=d)
    dyssd, dz, d_normw = _gated_norm_bwd(y, z, small["ssm_norm_w"], dgn)
    dxs, dbm, dcm, ddtr, d_dtb, d_alog, d_dsk = _ssd_bwd(
        xa, dtr, small["dt_bias"], small["a_log"], small["d_skip"], states, dyssd, nb, seq)
    dxbc_x, dcw_x, dcb_x = _conv_bwd("ssm_conv_bwd_x", dxs, cpre, xbc, small["conv_w"], 0, seq)
    dxbc_b, dcw_b, dcb_b = _conv_bwd("ssm_conv_bwd_b", dbm, cpre, xbc, small["conv_w"], D_INNER, seq)
    dxbc_c, dcw_c, dcb_c = _conv_bwd("ssm_conv_bwd_c", dcm, cpre, xbc, small["conv_w"], D_INNER + 1024, seq)
    parts = [(dz, ws["z_t"], 0), (dxbc_x, ws["xbc_t"], 0), (dxbc_b, ws["xbc_t"], D_INNER),
             (dxbc_c, ws["xbc_t"], D_INNER + 1024), (ddtr, ws["dt_t"], 0)]
    dh = _seg_nn("ssm_bwd_dh", parts, d, BF, tk=1024, hosts=hosts)
    dx, dnw = _rms_bwd("ssm_bwd_rms", x, nw, dh, dy)
    g = {
        "z_t": _wgrad("ssm_w_z_g", dz, h, D_INNER, d, tm=1024, tn=d, out_dtype=F32),
        "x_t": _wgrad("ssm_w_x_g", dxbc_x, h, D_INNER, d, tm=1024, tn=d, out_dtype=F32),
        "b_t": _wgrad("ssm_w_b_g", dxbc_b, h, 1024, d, tm=1024, tn=d, out_dtype=F32),
        "c_t": _wgrad("ssm_w_c_g", dxbc_c, h, 1024, d, tm=1024, tn=d, out_dtype=F32),
        "dt_t": _wgrad("ssm_w_dt_g", ddtr, h, 1024, d, tm=1024, tn=d, out_dtype=F32),
        "out": d_out,
        "conv_w": jnp.concatenate([dcw_x[:CONV_K], dcw_b[:CONV_K], dcw_c[:CONV_K]], axis=1),
        "conv_b": jnp.concatenate([dcb_x, dcb_b, dcb_c], axis=1),
        "dt_bias": d_dtb, "a_log": d_alog, "d_skip": d_dsk, "ssm_norm_w": d_normw,
    }
    return dx, dnw, g


def _lane_masks(shape):
    lane = lax.broadcasted_iota(jnp.int32, shape, len(shape) - 1)
    return lane < QK_NOPE, (lane >= QK_NOPE) & (lane < QK_DIM)


def _swap_matrix():
    p = np.zeros((HEAD_PAD, HEAD_PAD), np.float32)
    for i in range(QK_ROPE // 2):
        p[QK_NOPE + QK_ROPE // 2 + i, QK_NOPE + i] = 1.0
        p[QK_NOPE + i, QK_NOPE + QK_ROPE // 2 + i] = 1.0
    return jnp.asarray(p)


def _rope_tables(positions_col):
    t = positions_col.shape[0]
    tm = _row_tile(t, 512)
    freq = np.zeros((1, HEAD_PAD), np.float32)
    inv = 1.0 / (ROPE_THETA ** (np.arange(0, QK_ROPE, 2, dtype=np.float32) / QK_ROPE))
    freq[0, QK_NOPE:QK_NOPE + QK_ROPE // 2] = inv
    freq[0, QK_NOPE + QK_ROPE // 2:QK_DIM] = inv
    sign = np.zeros((1, HEAD_PAD), np.float32)
    sign[0, QK_NOPE:QK_NOPE + QK_ROPE // 2] = -1.0
    sign[0, QK_NOPE + QK_ROPE // 2:QK_DIM] = 1.0

    def fn(pos, freq, sign):
        ang = pos.astype(F32) * freq
        nope, rope = _lane_masks(ang.shape)
        return jnp.where(nope, 1.0, jnp.where(rope, jnp.cos(ang), 0.0)), jnp.sin(ang) * sign

    row = lambda i: (i, 0)
    par = ((1, HEAD_PAD), lambda i: (0, 0))
    return _ew("mla_rope_tables", (t // tm,), fn,
               [(positions_col, (tm, 1), row), (jnp.asarray(freq),) + par, (jnp.asarray(sign),) + par],
               [(SDS((t, HEAD_PAD), F32), (tm, HEAD_PAD), row, False)] * 2)


def _swap_lanes(x):
    lane = lax.broadcasted_iota(jnp.int32, x.shape, x.ndim - 1)
    half = QK_ROPE // 2
    first = (lane >= QK_NOPE) & (lane < QK_NOPE + half)
    second = (lane >= QK_NOPE + half) & (lane < QK_DIM)
    n = x.shape[-1]
    return jnp.where(first, pltpu.roll(x, n - half, x.ndim - 1), jnp.where(second, pltpu.roll(x, half, x.ndim - 1), 0.0))


@jax.custom_vjp
def _swap_halves(x):
    return _swap_lanes(x)


_swap_halves.defvjp(lambda x: (_swap_lanes(x), None), lambda _, g: (_swap_lanes(g),))


def _rope(xn, cos, sin_signed, swap):
    del swap
    return xn * cos + _swap_halves(xn) * sin_signed


def _krope_fn(kr, w, cos, sin_signed, swap):
    _, rope = _lane_masks(kr.shape)
    ss = jnp.sum(jnp.where(rope, kr * kr, 0.0), axis=-1, keepdims=True)
    xn = jnp.where(rope, kr * lax.rsqrt(ss * (1.0 / QK_ROPE) + EPS) * w, 0.0)
    return _rope(xn, cos, sin_signed, swap)


def _head_fn(q, kv, kr, cos, sin_signed, qn, kn, swap):
    nope, rope = _lane_masks(q.shape)

    def rstd(x, mask, n):
        return lax.rsqrt(jnp.sum(jnp.where(mask, x * x, 0.0), axis=-1, keepdims=True) * (1.0 / n) + EPS)

    qs = jnp.where(nope, rstd(q, nope, QK_NOPE), rstd(q, rope, QK_ROPE))
    qp = _rope(jnp.where(nope | rope, q * qs * qn, 0.0), cos, sin_signed, swap)
    kp = jnp.where(nope, kv * rstd(kv, nope, QK_NOPE) * kn, 0.0) + kr
    vp = jnp.where(nope, 0.0, kv)
    return qp, kp, vp


def _heads_fwd(q_raw, kv_raw, kr, cos, sin_signed, qn, kn, swap):
    nh, t, _ = q_raw.shape
    tm = _row_tile(t, 512)
    hblk = ((None, tm, HEAD_PAD), lambda i, h: (h, i, 0))
    tblk = ((tm, HEAD_PAD), lambda i, h: (i, 0))
    par = ((1, HEAD_PAD), lambda i, h: (0, 0))
    sw = ((HEAD_PAD, HEAD_PAD), lambda i, h: (0, 0))
    def fn(*tiles):
        qp, kp, vp = _head_fn(*tiles)
        return qp * Q_PRESCALE, kp, vp

    return _ew("mla_heads_fwd", (t // tm, nh), fn,
               [(q_raw,) + hblk, (kv_raw,) + hblk, (kr,) + tblk, (cos,) + tblk, (sin_signed,) + tblk,
                (qn,) + par, (kn,) + par, (swap,) + sw],
               [(SDS((nh, t, HEAD_PAD), BF),) + hblk + (False,)] * 3)


def _heads_bwd(q_raw, kv_raw, kr, cos, sin_signed, qn, kn, swap, dqp, dkp, dvp):
    nh, t, _ = q_raw.shape
    tm = _row_tile(t, 512)
    hblk = ((None, tm, HEAD_PAD), lambda i, h: (h, i, 0))
    tblk = ((tm, HEAD_PAD), lambda i, h: (i, 0))
    par = ((1, HEAD_PAD), lambda i, h: (0, 0))
    sw = ((HEAD_PAD, HEAD_PAD), lambda i, h: (0, 0))

    def body(q, kv, kr, cos, sn, qn, kn, swap, dqp, dkp, dvp, dq, dkv, dkr, dqn, dkn):
        f = lambda q, kv, kr, qn, kn: _head_fn(q, kv, kr, cos[...], sn[...], qn, kn, swap[...])
        _, vjp = jax.vjp(f, q[...], kv[...], kr[...], qn[...], kn[...])
        g = vjp((dqp[...].astype(F32), dkp[...].astype(F32), dvp[...].astype(F32)))
        dq[...] = g[0].astype(dq.dtype)
        dkv[...] = g[1].astype(dkv.dtype)
        h0 = pl.program_id(1) == 0
        first = h0 & (pl.program_id(0) == 0)
        for o, v, c in ((dkr, g[2], h0), (dqn, g[3], first), (dkn, g[4], first)):
            @pl.when(c)
            def _(o=o, v=v):
                o[...] = v

            @pl.when(jnp.logical_not(c))
            def _(o=o, v=v):
                o[...] += v

    spec = lambda b: pl.BlockSpec(*b)
    return pl.pallas_call(
        body, name="mla_heads_bwd", grid=(t // tm, nh),
        in_specs=[spec(hblk), spec(hblk), spec(tblk), spec(tblk), spec(tblk), spec(par), spec(par), spec(sw),
                  spec(hblk), spec(hblk), spec(hblk)],
        out_specs=[spec(hblk), spec(hblk), spec(tblk), spec(par), spec(par)],
        out_shape=[SDS((nh, t, HEAD_PAD), BF), SDS((nh, t, HEAD_PAD), BF), SDS((t, HEAD_PAD), F32),
                   SDS((1, HEAD_PAD), F32), SDS((1, HEAD_PAD), F32)],
        compiler_params=_cparams(2),
    )(q_raw, kv_raw, kr, cos, sin_signed, qn, kn, swap, dqp, dkp, dvp)


ATT_TILE = 512
ATT_SCALE = QK_DIM ** -0.5
LOG2E = 1.4426950408889634
LN2 = 0.6931471805599453
Q_PRESCALE = ATT_SCALE * LOG2E


def _flash_fwd(qs, k, v, nb, seq, hosts=None):
    nh, t, dh = qs.shape
    tq = _row_tile(seq, ATT_TILE)
    nq = seq // tq
    name = "mla_flash_fwd"
    comm = hosts.comm(name) if hosts is not None else None
    n_ci = len(comm.arrays) if comm is not None else 0
    n_co = len(comm.out_shapes) if comm is not None else 0

    def body(*refs):
        q_ref, k_ref, v_ref = refs[:3]
        ci_refs = refs[3:3 + n_ci]
        o_ref, lse_ref = refs[3 + n_ci:5 + n_ci]
        co_refs = refs[5 + n_ci:5 + n_ci + n_co]
        sem_refs = refs[5 + n_ci + n_co:]
        ids = (pl.program_id(0), pl.program_id(1), pl.program_id(2))
        if comm is not None:
            @pl.when((ids[0] == 0) & (ids[1] == 0) & (ids[2] == 0))
            def _():
                comm.start(ci_refs, co_refs, sem_refs)

        attend(q_ref, k_ref, v_ref, o_ref, lse_ref)

        if comm is not None:
            @pl.when((ids[0] == nh - 1) & (ids[1] == nb - 1) & (ids[2] == nq - 1))
            def _():
                comm.wait(ci_refs, co_refs, sem_refs)

    def attend(q_ref, k_ref, v_ref, o_ref, lse_ref):
        qi = pl.program_id(2)
        qt = q_ref[...]

        def tile(j, carry, diagonal):
            m, l, acc = carry
            rows = pl.ds(pl.multiple_of(j * tq, tq), tq)
            s = _dotf(qt, k_ref[rows, :], NT)
            if diagonal:
                r = lax.broadcasted_iota(jnp.int32, (tq, tq), 0)
                c = lax.broadcasted_iota(jnp.int32, (tq, tq), 1)
                s = jnp.where(c <= r, s, -jnp.inf)
            m_new = jnp.maximum(m, jnp.max(s, axis=-1, keepdims=True))
            alpha = jnp.exp2(m - m_new)
            p = jnp.exp2(s - m_new)
            return m_new, alpha * l + jnp.sum(p, axis=-1, keepdims=True), alpha * acc + _dotf(p, v_ref[rows, :], NN)

        init = (jnp.full((tq, 1), -jnp.inf, F32), jnp.zeros((tq, 1), F32), jnp.zeros((tq, dh), F32))
        carry = lax.fori_loop(0, qi, lambda j, c: tile(j, c, False), init)
        m, l, acc = tile(qi, carry, True)
        o_ref[...] = (acc / l).astype(o_ref.dtype)
        lse_ref[...] = m + jnp.log2(l)

    qblk = pl.BlockSpec((None, tq, dh), lambda h, b, i: (h, b * nq + i, 0))
    kblk = pl.BlockSpec((None, seq, dh), lambda h, b, i: (h, b, 0))
    hbm = pl.BlockSpec(memory_space=pl.ANY)
    res = pl.pallas_call(
        body, name=name, grid=(nh, nb, nq), in_specs=[qblk, kblk, kblk] + [hbm] * n_ci,
        out_specs=[qblk, pl.BlockSpec((None, tq, 1), lambda h, b, i: (h, b * nq + i, 0))] + [hbm] * n_co,
        out_shape=[SDS((nh, t, dh), BF), SDS((nh, t, 1), F32)] + (list(comm.out_shapes) if comm is not None else []),
        scratch_shapes=comm.sems if comm is not None else [],
        compiler_params=_cparams(3, VMEM_BIG),
    )(qs, k, v, *(comm.arrays if comm is not None else []))
    if comm is not None:
        hosts.done(name, res[2:])
    return res[0], res[1]


def _flash_bwd(qs, k, v, o, lse, do, nb, seq):
    nh, t, dh = qs.shape
    tq = _row_tile(seq, ATT_TILE)
    nq = seq // tq

    def row_of(col):
        return jnp.broadcast_to(col, (tq, LANES)).T[0:1, :]

    def body(q_ref, k_ref, v_ref, o_ref, lse_ref, do_ref, dq_ref, dk_ref, dv_ref, kt_sc, lrow_sc, drow_sc, dqt_sc):
        for c in range(nq):
            rows = pl.ds(c * tq, tq)
            kt_sc[c] = k_ref[rows, :].T
            delta = jnp.sum(do_ref[rows, :].astype(F32) * o_ref[rows, :].astype(F32), axis=-1, keepdims=True)
            drow_sc[c] = row_of(delta)
            lrow_sc[c] = row_of(lse_ref[rows, :])
        dqt_sc[...] = jnp.zeros_like(dqt_sc)

        def kv_step(j, _):
            rows_j = pl.ds(pl.multiple_of(j * tq, tq), tq)
            ks, vs, kt = k_ref[rows_j, :], v_ref[rows_j, :], kt_sc[j]

            def q_tile(i, carry, diagonal):
                dk, dv = carry
                rows_i = pl.ds(pl.multiple_of(i * tq, tq), tq)
                qt, dot_ = q_ref[rows_i, :], do_ref[rows_i, :]
                pt = jnp.exp2(_dotf(ks, qt, NT) - lrow_sc[i])
                if diagonal:
                    kk = lax.broadcasted_iota(jnp.int32, (tq, tq), 0)
                    qq = lax.broadcasted_iota(jnp.int32, (tq, tq), 1)
                    pt = jnp.where(kk <= qq, pt, 0.0)
                dst = (pt * (_dotf(vs, dot_, NT) - drow_sc[i])).astype(BF)
                dqt_sc[i] += _dotf(kt, dst, NN)
                return dk + _dotf(dst, qt, NN), dv + _dotf(pt, dot_, NN)

            zero = jnp.zeros((tq, dh), F32)
            carry = q_tile(j, (zero, zero), True)
            dk, dv = lax.fori_loop(j + 1, nq, lambda i, c: q_tile(i, c, False), carry)
            dk_ref[rows_j, :] = dk * LN2
            dv_ref[rows_j, :] = dv
            return 0

        lax.fori_loop(0, nq, kv_step, 0)
        for c in range(nq):
            dq_ref[pl.ds(c * tq, tq), :] = dqt_sc[c].T * ATT_SCALE

    full = pl.BlockSpec((None, seq, dh), lambda h, b: (h, b, 0))
    sfull = pl.BlockSpec((None, seq, 1), lambda h, b: (h, b, 0))
    return pl.pallas_call(
        body, name="mla_flash_bwd", grid=(nh, nb), in_specs=[full, full, full, full, sfull, full],
        out_specs=[full, full, full], out_shape=[SDS((nh, t, dh), F32)] * 3,
        scratch_shapes=[pltpu.VMEM((nq, dh, tq), BF), pltpu.VMEM((nq, 1, tq), F32), pltpu.VMEM((nq, 1, tq), F32),
                        pltpu.VMEM((nq, dh, tq), F32)],
        compiler_params=_cparams(2, VMEM_BIG),
    )(qs, k, v, o, lse, do)


def _heads_nt(name, a, wt, out_dtype):
    t, kdim = a.shape
    tm = _row_tile(t, 512)
    nw = MLA_HEADS * HEAD_PAD

    def body(a_ref, w_ref, o_ref):
        r = _dotf(a_ref[...], w_ref[...], NT)
        for h in range(MLA_HEADS):
            o_ref[h] = r[:, HEAD_PAD * h:HEAD_PAD * (h + 1)].astype(o_ref.dtype)

    return pl.pallas_call(
        body, name=name, grid=(t // tm,),
        in_specs=[pl.BlockSpec((tm, kdim), lambda i: (i, 0)), pl.BlockSpec((nw, kdim), lambda i: (0, 0))],
        out_specs=pl.BlockSpec((MLA_HEADS, tm, HEAD_PAD), lambda i: (0, i, 0)),
        out_shape=SDS((MLA_HEADS, t, HEAD_PAD), out_dtype), compiler_params=_cparams(1, VMEM_BIG),
    )(a, wt)


def _all_heads(a_ref):
    return jnp.concatenate([a_ref[h] for h in range(MLA_HEADS)], axis=1)


def _heads_nn(name, a, w, n, out_dtype, res=None):
    t = a.shape[1]
    tm = _row_tile(t, 512)
    nw = MLA_HEADS * HEAD_PAD

    def body(*refs):
        a_ref, w_ref, o_ref = refs[0], refs[1], refs[-1]
        r = _dotf(_all_heads(a_ref), w_ref[...], NN)
        if res is not None:
            r = r + refs[2][...]
        o_ref[...] = r.astype(o_ref.dtype)

    row = pl.BlockSpec((tm, n), lambda i: (i, 0))
    in_specs = [pl.BlockSpec((MLA_HEADS, tm, HEAD_PAD), lambda i: (0, i, 0)), pl.BlockSpec((nw, n), lambda i: (0, 0))]
    args = [a, w]
    if res is not None:
        in_specs.append(row)
        args.append(res)
    return pl.pallas_call(body, name=name, grid=(t // tm,), in_specs=in_specs, out_specs=row,
                          out_shape=SDS((t, n), out_dtype), compiler_params=_cparams(1, VMEM_BIG))(*args)


def _heads_wgrad(name, a, b, n):
    t = b.shape[0]
    tk = _row_tile(t, 512)
    nw = MLA_HEADS * HEAD_PAD
    steps = t // tk

    def body(a_ref, b_ref, o_ref, acc):
        k = pl.program_id(0)

        @pl.when(k == 0)
        def _():
            acc[...] = jnp.zeros_like(acc)

        acc[...] += _dotf(_all_heads(a_ref), b_ref[...], TN)

        @pl.when(k == steps - 1)
        def _():
            o_ref[...] = acc[...]

    return pl.pallas_call(
        body, name=name, grid=(steps,),
        in_specs=[pl.BlockSpec((MLA_HEADS, tk, HEAD_PAD), lambda k: (0, k, 0)), pl.BlockSpec((tk, n), lambda k: (k, 0))],
        out_specs=pl.BlockSpec((nw, n), lambda k: (0, 0)), out_shape=SDS((nw, n), F32),
        scratch_shapes=[pltpu.VMEM((nw, n), F32)], compiler_params=_cparams(1, VMEM_BIG),
    )(a, b)


PM_CKV, PM_KR, PM_CQ = 0, KV_LORA, KV_LORA + HEAD_PAD
PM_DIM = KV_LORA + HEAD_PAD + Q_LORA


def _lat_specs(t, tm):
    return (((tm, KV_LORA), lambda i: (i, 0)), ((tm, HEAD_PAD), lambda i: (i, PM_KR // HEAD_PAD)),
            ((tm, Q_LORA), lambda i: (i, PM_CQ // Q_LORA)))


def _mla_fwd(x, nw, wm, small, tables, nb, seq, hosts=None):
    t, d = x.shape
    cos, sin_signed, swap = tables
    h = _rms_fwd("mla_rms", x, nw)
    pm = _proj_nt("mla_in", h, wm["in_t"], PM_DIM, F32, tn=PM_DIM // 3)
    tm = _row_tile(t, 512)
    ckv_s, kr_s, cq_s = _lat_specs(t, tm)
    row = lambda i: (i, 0)
    par = lambda n: ((1, n), lambda i: (0, 0))
    ckvn = _ew("mla_ckv_norm", (t // tm,), _rms_fn, [(pm,) + ckv_s, (small["kv_a_norm"],) + par(KV_LORA)],
               [(SDS((t, KV_LORA), BF), (tm, KV_LORA), row, False)])[0]
    cqn = _ew("mla_cq_norm", (t // tm,), _rms_fn, [(pm,) + cq_s, (small["q_a_norm"],) + par(Q_LORA)],
              [(SDS((t, Q_LORA), BF), (tm, Q_LORA), row, False)])[0]
    tb = ((tm, HEAD_PAD), row)
    kr = _ew("mla_krope", (t // tm,), _krope_fn,
             [(pm,) + kr_s, (small["k_norm"],) + par(HEAD_PAD), (cos,) + tb, (sin_signed,) + tb,
              (swap, (HEAD_PAD, HEAD_PAD), lambda i: (0, 0))],
             [(SDS((t, HEAD_PAD), F32),) + tb + (False,)])[0]
    q_raw = _heads_nt("mla_q_b", cqn, wm["qb_t"], F32)
    kv_raw = _heads_nt("mla_kv_b", ckvn, wm["kvb_t"], F32)
    qp, kp, vp = _heads_fwd(q_raw, kv_raw, kr, cos, sin_signed, small["q_norm"], small["k_norm"], swap)
    o, lse = _flash_fwd(qp, kp, vp, nb, seq, hosts=hosts)
    out = _heads_nn("mla_out", o, wm["out"], d, F32, res=x)
    return out, (x, h, pm, ckvn, cqn, kr, q_raw, kv_raw, qp, kp, vp, o, lse)


def _mla_bwd(dy, nw, wm, small, tables, saved, nb, seq):
    x, h, pm, ckvn, cqn, kr, q_raw, kv_raw, qp, kp, vp, o, lse = saved
    t, d = x.shape
    cos, sin_signed, swap = tables
    do = _heads_nt("mla_bwd_do", dy, wm["out"], BF)
    g_out = _heads_wgrad("mla_w_out_g", o, dy, d)
    dqp, dkp, dvp = _flash_bwd(qp, kp, vp, o, lse, do, nb, seq)
    dq_raw, dkv_raw, dkr, d_qn, d_kn = _heads_bwd(q_raw, kv_raw, kr, cos, sin_signed, small["q_norm"],
                                                   small["k_norm"], swap, dqp, dkp, dvp)
    dcqn = _heads_nn("mla_bwd_dcq", dq_raw, wm["qb_t"], Q_LORA, F32)
    dckvn = _heads_nn("mla_bwd_dckv", dkv_raw, wm["kvb_t"], KV_LORA, F32)
    g_qb = _heads_wgrad("mla_w_qb_g", dq_raw, cqn, Q_LORA)
    g_kvb = _heads_wgrad("mla_w_kvb_g", dkv_raw, ckvn, KV_LORA)
    tm = _row_tile(t, 512)
    ckv_s, kr_s, cq_s = _lat_specs(t, tm)
    row = lambda i: (i, 0)
    par = lambda n: ((1, n), lambda i: (0, 0))

    def rms_b(xv, w, dv):
        _, vjp = jax.vjp(_rms_fn, xv, w)
        return vjp(dv)

    dckv, d_kva = _ew("mla_ckv_norm_bwd", (t // tm,), rms_b,
                      [(pm,) + ckv_s, (small["kv_a_norm"],) + par(KV_LORA), (dckvn, (tm, KV_LORA), row)],
                      [(SDS((t, KV_LORA), BF), (tm, KV_LORA), row, False),
                       (SDS((1, KV_LORA), F32),) + par(KV_LORA) + (True,)], acc_axes=(0,))
    dcq, d_qa = _ew("mla_cq_norm_bwd", (t // tm,), rms_b,
                    [(pm,) + cq_s, (small["q_a_norm"],) + par(Q_LORA), (dcqn, (tm, Q_LORA), row)],
                    [(SDS((t, Q_LORA), BF), (tm, Q_LORA), row, False),
                     (SDS((1, Q_LORA), F32),) + par(Q_LORA) + (True,)], acc_axes=(0,))
    tb = ((tm, HEAD_PAD), row)

    def kr_b(krv, w, cosv, sinv, sw, dv):
        _, vjp = jax.vjp(lambda a, b: _krope_fn(a, b, cosv, sinv, sw), krv, w)
        return vjp(dv)

    dkr_raw, d_kn2 = _ew("mla_krope_bwd", (t // tm,), kr_b,
                         [(pm,) + kr_s, (small["k_norm"],) + par(HEAD_PAD), (cos,) + tb, (sin_signed,) + tb,
                          (swap, (HEAD_PAD, HEAD_PAD), lambda i: (0, 0)), (dkr,) + tb],
                         [(SDS((t, HEAD_PAD), BF),) + tb + (False,),
                          (SDS((1, HEAD_PAD), F32),) + par(HEAD_PAD) + (True,)], acc_axes=(0,))
    dh = _seg_nn("mla_bwd_dh", [(dckv, wm["in_t"], PM_CKV), (dkr_raw, wm["in_t"], PM_KR),
                                (dcq, wm["in_t"], PM_CQ)], d, BF, tk=128)
    dx, dnw = _rms_bwd("mla_bwd_rms", x, nw, dh, dy)
    g = {
        "in_ckv_t": _wgrad("mla_w_in_ckv_g", dckv, h, KV_LORA, d, tm=KV_LORA, tn=d, out_dtype=F32),
        "in_kr_t": _wgrad("mla_w_in_kr_g", dkr_raw, h, HEAD_PAD, d, tm=HEAD_PAD, tn=d, out_dtype=F32),
        "in_cq_t": _wgrad("mla_w_in_cq_g", dcq, h, Q_LORA, d, tm=Q_LORA, tn=d, out_dtype=F32),
        "qb_t": g_qb, "kvb_t": g_kvb, "out": g_out,
        "q_a_norm": d_qa, "kv_a_norm": d_kva, "q_norm": d_qn, "k_norm": d_kn + d_kn2,
    }
    return dx, dnw, g


def _mesh_pos():
    return lax.axis_index("x"), lax.axis_index("y"), lax.axis_index("c")


def _peer(pos, k):
    x, y, c = pos
    return (x ^ ((k >> 2) & 1), y ^ ((k >> 1) & 1), c ^ (k & 1))


def _flat(pos):
    return 4 * pos[0] + 2 * pos[1] + pos[2]


def _slab(ref, axis, start, size):
    idx = [slice(None)] * axis + [pl.ds(start, size)]
    return ref.at[tuple(idx)]


class _Exchange:
    def __init__(self, kind, items):
        self.kind = kind
        self.axes = [ax for _, ax in items]
        self.arrays = [a for a, _ in items]
        n = len(items)
        self.out_shapes = []
        self.sizes = []
        for a, ax in items:
            shp = list(a.shape)
            if kind == "gather":
                self.sizes.append(shp[ax])
                shp[ax] *= N_DEV
                self.out_shapes.append(SDS(tuple(shp), a.dtype))
            else:
                shp[ax] //= N_DEV
                self.sizes.append(shp[ax])
                self.out_shapes.append(SDS((N_DEV,) + tuple(shp), a.dtype))
        self.sems = [pltpu.SemaphoreType.DMA((n, N_DEV - 1)), pltpu.SemaphoreType.DMA((n, N_DEV - 1)),
                     pltpu.SemaphoreType.DMA((n,))]

    def _copies(self, srcs, dsts, sems, with_arrivals=True):
        send_sems, recv_sems, local_sems = sems
        pos = _mesh_pos()
        me = _flat(pos)
        local, sends, recvs = [], [], []
        for t, (src, dst) in enumerate(zip(srcs, dsts)):
            ax, sz = self.axes[t], self.sizes[t]
            if self.kind == "gather":
                mine = _slab(dst, ax, me * sz, sz)
                local.append(pltpu.make_async_copy(src, mine, local_sems.at[t]))
            else:
                mine = dst.at[me]
                local.append(pltpu.make_async_copy(_slab(src, ax, me * sz, sz), mine, local_sems.at[t]))
            for k in range(1, N_DEV):
                peer = _peer(pos, k)
                there = _flat(peer)
                if self.kind == "gather":
                    out_src, landing = src, _slab(dst, ax, there * sz, sz)
                else:
                    out_src, landing = _slab(src, ax, there * sz, sz), dst.at[there]
                common = dict(send_sem=send_sems.at[t, k - 1], recv_sem=recv_sems.at[t, k - 1], device_id=peer,
                              device_id_type=pl.DeviceIdType.MESH)
                sends.append(pltpu.make_async_remote_copy(src_ref=out_src, dst_ref=mine, **common))
                if with_arrivals:
                    recvs.append(pltpu.make_async_remote_copy(src_ref=out_src, dst_ref=landing, **common))
        return local, sends, recvs

    def start(self, srcs, dsts, sems):
        local, sends, _ = self._copies(srcs, dsts, sems, with_arrivals=False)
        for cp in local + sends:
            cp.start()

    def wait(self, srcs, dsts, sems):
        local, sends, recvs = self._copies(srcs, dsts, sems)
        for rc in recvs:
            rc.wait_recv()
        for rc in sends:
            rc.wait_send()
        for cp in local:
            cp.wait()

    def run(self, name):
        n = len(self.arrays)

        def body(*refs):
            srcs, dsts, sems = refs[:n], refs[n:2 * n], refs[2 * n:]
            self.start(srcs, dsts, sems)
            self.wait(srcs, dsts, sems)

        hbm = pl.BlockSpec(memory_space=pl.ANY)
        return pl.pallas_call(body, name=name, in_specs=[hbm] * n, out_specs=[hbm] * n, out_shape=self.out_shapes,
                              scratch_shapes=self.sems)(*self.arrays)


def _adam_math(w, g, m, v):
    m = ADAM_B1 * m + (1.0 - ADAM_B1) * g
    v = ADAM_B2 * v + (1.0 - ADAM_B2) * (g * g)
    m_hat = m / (1.0 - ADAM_B1 ** ADAM_STEP)
    v_hat = v / (1.0 - ADAM_B2 ** ADAM_STEP)
    delta = -ADAM_LR * (m_hat / (jnp.sqrt(v_hat) + ADAM_EPS) + ADAM_WD * w)
    return delta, m, v


def _adam(name, land, land_blk, land_idx, w, m, v, transposed, ck):
    n, r, c = w.shape
    wblk = ((None, ck, c), lambda a, i: (a, i, 0))

    def fn(parts, w, m, v):
        g = parts[0].astype(F32)
        for s in range(1, N_DEV):
            g = g + parts[s].astype(F32)
        if transposed:
            g = g.T
        delta, m2, v2 = _adam_math(w, g, m, v)
        return g, delta, m2, v2

    return _ew(name, (n, r // ck), fn,
               [(land, land_blk, land_idx), (w,) + wblk, (m,) + wblk, (v,) + wblk],
               [(SDS(w.shape, F32),) + wblk + (False,)] * 4, vmem=VMEM_BIG)


def _prep_ffn(gate, up, down):
    def body(g, u, dn, o):
        o[0] = g[...].T.astype(BF)
        o[1] = u[...].T.astype(BF)
        o[2] = dn[...].astype(BF)

    cblk = pl.BlockSpec((None, None, D_MODEL, FF_SHARD), lambda l, i: (l, i, 0, 0))
    rblk = pl.BlockSpec((None, None, FF_SHARD, D_MODEL), lambda l, i: (l, i, 0, 0))
    return pl.pallas_call(
        body, name="prep_ffn", grid=(2, 2), in_specs=[cblk, cblk, rblk],
        out_specs=pl.BlockSpec((3, FF_SHARD, D_MODEL), lambda l, i: (2 * l + i, 0, 0)),
        out_shape=SDS((12, FF_SHARD, D_MODEL), BF), compiler_params=_cparams(2, VMEM_BIG),
    )(gate, up, down)


def _transpose_cast(name, w, dtype):
    def body(a, o):
        o[...] = a[...].T.astype(dtype)

    r, c = w.shape
    return pl.pallas_call(body, name=name, out_shape=SDS((c, r), dtype),
                          compiler_params=pltpu.CompilerParams(vmem_limit_bytes=VMEM_BIG))(w)


SMALL_SHARDED = (("norm_w", 6 * 128), ("conv_w", CONV_K * 512), ("q_a_norm", 48), ("kv_a_norm", 32))
SMALL_PACK = 3072


def _dyn(a, start, size):
    return lax.dynamic_slice_in_dim(a, start, size, axis=a.ndim - 1)


def _layout_ssm(ssm_in_t, ssm_out_all):
    d = ssm_in_t.shape[1]
    dt_rows = ssm_in_t[D_INNER + CONV_DIM:].reshape(SSM_GROUPS, SSM_HPG, d)
    return {"z_t": ssm_in_t[:D_INNER], "xbc_t": ssm_in_t[D_INNER:D_INNER + CONV_DIM],
            "dt_t": jnp.pad(dt_rows, ((0, 0), (0, LANES - SSM_HPG), (0, 0))).reshape(SSM_GROUPS * LANES, d),
            "out": ssm_out_all}


def _layout_mla(mla_in_all, qb_all, kvb_all, mla_out_all):
    d = mla_out_all.shape[1]
    in_t = mla_in_all.T
    kr_rows = jnp.pad(in_t[Q_LORA + KV_LORA:], ((QK_NOPE, HEAD_PAD - QK_DIM), (0, 0)))
    qb_heads = jnp.pad(qb_all.reshape(MLA_HEADS, QK_DIM, Q_LORA), ((0, 0), (0, HEAD_PAD - QK_DIM), (0, 0)))
    out_heads = jnp.pad(mla_out_all.reshape(MLA_HEADS, 64, d), ((0, 0), (64, 0), (0, 0)))
    return {"in_t": jnp.concatenate([in_t[Q_LORA:Q_LORA + KV_LORA], kr_rows, in_t[:Q_LORA]], axis=0),
            "qb_t": qb_heads.reshape(MLA_HEADS * HEAD_PAD, Q_LORA), "kvb_t": kvb_all,
            "out": out_heads.reshape(MLA_HEADS * HEAD_PAD, d)}


def _layout_small(conv_w, conv_b, dt_bias, a_log, d_skip, ssm_norm_w, q_a_norm, kv_a_norm, q_norm, k_norm):
    lane_heads = lambda p: jnp.pad(p.reshape(SSM_GROUPS, SSM_HPG), ((0, 0), (0, LANES - SSM_HPG))).reshape(1, -1)
    pad_head = lambda p: jnp.pad(p.reshape(1, QK_DIM), ((0, 0), (0, HEAD_PAD - QK_DIM)))
    return {"conv_w": conv_w, "conv_b": conv_b, "dt_bias": lane_heads(dt_bias), "a_log": lane_heads(a_log),
            "d_skip": lane_heads(d_skip), "ssm_norm_w": ssm_norm_w, "q_a_norm": q_a_norm, "kv_a_norm": kv_a_norm,
            "q_norm": pad_head(q_norm), "k_norm": pad_head(k_norm)}


class _Plan:
    def __init__(self, ctx):
        self.ctx = ctx
        self.make = {}
        self.land = {}

    def ride(self, host, make, land):
        assert host not in self.make, host
        self.make[host] = make
        self.land[host] = land

    def comm(self, host):
        return self.make[host](self.ctx) if host in self.make else None

    def done(self, host, results):
        self.land[host](results, self.ctx)


def _local_step(x, positions, loss_target, ctx, plan=None):
    nb, seq, d = x.shape
    t = nb * seq
    xf = x.reshape(t, d)
    norm = ctx["norm"]
    tables = list(_rope_tables(positions.reshape(t, 1))) + [_swap_matrix()]
    x1, s_f0 = _ffn_fwd("ffn0", xf, norm[0, 0], ctx["ffn0"], plan)
    x2, s_ssm = _ssm_fwd(x1, norm[0, 1], ctx["ws"], ctx["small"], nb, seq, plan)
    x3, s_f1 = _ffn_fwd("ffn1", x2, norm[0, 2], ctx["ffn1"], plan)
    x4, s_f2 = _ffn_fwd("ffn2", x3, norm[1, 0], ctx["ffn2"], plan)
    x5, s_mla = _mla_fwd(x4, norm[1, 1], ctx["wm"], ctx["small"], tables, nb, seq, plan)
    x6, s_f3 = _ffn_fwd("ffn3", x5, norm[1, 2], ctx["ffn3"], plan)
    dy, loss_cols = _loss_and_grad(x6, loss_target.reshape(t, d))

    dx5, dn12, ctx["g_ffn3"] = _ffn_bwd("ffn3", dy, norm[1, 2], ctx["ffn3"], s_f3, plan)
    dx4, dn11, ctx["g_mla"] = _mla_bwd(dx5, norm[1, 1], ctx["wm"], ctx["small"], tables, s_mla, nb, seq)
    dx3, dn10, ctx["g_ffn2"] = _ffn_bwd("ffn2", dx4, norm[1, 0], ctx["ffn2"], s_f2, plan)
    dx2, dn02, ctx["g_ffn1"] = _ffn_bwd("ffn1", dx3, norm[0, 2], ctx["ffn1"], s_f1, plan)
    dx1, dn01, ctx["g_ssm"] = _ssm_bwd(dx2, norm[0, 1], ctx["ws"], ctx["small"], s_ssm, nb, seq, plan)
    dx0, dn00, ctx["g_ffn0"] = _ffn_bwd("ffn0", dx1, norm[0, 0], ctx["ffn0"], s_f0, plan)
    return loss_cols, dx0.reshape(nb, seq, d), (dn00, dn01, dn02, dn10, dn11, dn12)


def kernel(x, positions, norm_w, ffn_w_gate, ffn_w_up, ffn_w_down, ssm_w_in, ssm_conv_w, ssm_conv_b, ssm_dt_bias, ssm_a_log, ssm_d, ssm_norm_w, ssm_w_out, mla_w_in, mla_q_a_norm, mla_kv_a_norm, mla_w_q_b, mla_w_kv_b, mla_q_norm, mla_k_norm, mla_w_out, loss_target, m_norm_w, m_ffn_w_gate, m_ffn_w_up, m_ffn_w_down, m_ssm_w_in, m_ssm_conv_w, m_ssm_conv_b, m_ssm_dt_bias, m_ssm_a_log, m_ssm_d, m_ssm_norm_w, m_ssm_w_out, m_mla_w_in, m_mla_q_a_norm, m_mla_kv_a_norm, m_mla_w_q_b, m_mla_w_kv_b, m_mla_q_norm, m_mla_k_norm, m_mla_w_out, v_norm_w, v_ffn_w_gate, v_ffn_w_up, v_ffn_w_down, v_ssm_w_in, v_ssm_conv_w, v_ssm_conv_b, v_ssm_dt_bias, v_ssm_a_log, v_ssm_d, v_ssm_norm_w, v_ssm_w_out, v_mla_w_in, v_mla_q_a_norm, v_mla_kv_a_norm, v_mla_w_q_b, v_mla_w_kv_b, v_mla_q_norm, v_mla_k_norm, v_mla_w_out):
    nb, seq, d = x.shape
    t = nb * seq
    me = _flat(_mesh_pos())

    ffn_loc = _prep_ffn(ffn_w_gate, ffn_w_up, ffn_w_down)
    ssm_in_loc = _transpose_cast("prep_ssm_in", ssm_w_in[0], BF)
    ssm_out_loc = ssm_w_out[0].astype(BF)
    mla_in_loc, mla_out_loc = mla_w_in[0].astype(BF), mla_w_out[0].astype(BF)
    qb_loc = _transpose_cast("prep_q_b", mla_w_q_b[0], BF)
    kvb_loc = _transpose_cast("prep_kv_b", mla_w_kv_b[0], BF)
    small_loc = jnp.concatenate([norm_w.reshape(-1), ssm_conv_w.reshape(-1), mla_q_a_norm.reshape(-1),
                                 mla_kv_a_norm.reshape(-1)])
    small_loc = jnp.pad(small_loc, (0, SMALL_PACK - small_loc.shape[0])).reshape(SMALL_PACK // LANES, LANES)

    wloc = lambda n: [(ffn_loc[3 * n + k], 0) for k in range(3)]
    g0, u0, small_all = _Exchange("gather", wloc(0)[0:2] + [(small_loc, 0)]).run("gather_first")
    sm = small_all.reshape(N_DEV, SMALL_PACK)
    conv_w_full = sm[:, 768:768 + 2048].reshape(N_DEV, CONV_K, 512).transpose(1, 0, 2).reshape(CONV_K, CONV_DIM)
    ctx = {"ffn0": [g0, u0, None], "ffn1": [None] * 3, "ffn2": [None] * 3,
           "norm": sm[:, :768].reshape(N_DEV, 6, 128).transpose(1, 0, 2).reshape(2, 3, 1, d),
           "small": _layout_small(conv_w_full, ssm_conv_b, ssm_dt_bias, ssm_a_log, ssm_d, ssm_norm_w,
                                  sm[:, 2816:2864].reshape(1, Q_LORA), sm[:, 2864:2896].reshape(1, KV_LORA),
                                  mla_q_norm, mla_k_norm)}
    plan = _Plan(ctx)

    def gather_on(host, items, land):
        plan.ride(host, lambda c: _Exchange("gather", items), land)

    def put_w(key, ks):
        def land(r, c):
            for k, arr in zip(ks, r):
                c[key][k] = arr
        return land

    half_rows = SSM_IN_SHARD // 2
    ssm_in_a, ssm_in_b = (ssm_in_loc[k * half_rows:(k + 1) * half_rows].reshape(half_rows // 2, 16, LANES)
                          for k in range(2))

    def land_first(r, c):
        c["ffn0"][2] = r[0]
        c["ssm_in_a"] = r[1].reshape(N_DEV, half_rows, d)

    def land_ssm_in(r, c):
        both = jnp.concatenate([c["ssm_in_a"], r[0].reshape(N_DEV, half_rows, d)], axis=1)
        c["ws"] = _layout_ssm(both.reshape(SSM_IN_DIM, d), None)

    def land_ssm_out(r, c):
        c["ffn1"][1] = r[0]
        c["ws"]["out"] = r[1]

    gather_on("ffn0_up", wloc(0)[2:3] + [(ssm_in_a, 0)], land_first)
    gather_on("ffn0_down", [(ssm_in_b, 0)], land_ssm_in)
    gather_on("ssm_in_z", wloc(1)[0:1], put_w("ffn1", (0,)))
    gather_on("ssm_in_xbc", wloc(1)[1:2] + [(ssm_out_loc, 0)], land_ssm_out)
    gather_on("ssm_out", wloc(1)[2:3], put_w("ffn1", (2,)))
    gather_on("ffn1_up", wloc(2)[0:2], put_w("ffn2", (0, 1)))
    gather_on("ffn1_down", wloc(2)[2:3], put_w("ffn2", (2,)))
    gather_on("ffn2_up", [(mla_in_loc, 0), (qb_loc, 0), (kvb_loc, 0), (mla_out_loc, 0)],
              lambda r, c: c.update(wm=_layout_mla(r[0], r[1], r[2], r[3])))
    gather_on("mla_flash_fwd", wloc(3), lambda r, c: c.update(ffn3=tuple(r)))

    heads_of = lambda a: a.reshape(SSM_GROUPS, LANES, -1)[:, :SSM_HPG].reshape(SSM_HEADS, -1)

    def mla_grad_items(c):
        g = c["g_mla"]
        g_in = jnp.concatenate([g["in_cq_t"], g["in_ckv_t"], g["in_kr_t"][QK_NOPE:QK_DIM]], axis=0).T
        g_qb = g["qb_t"].reshape(MLA_HEADS, HEAD_PAD, Q_LORA)[:, :QK_DIM].reshape(MLA_HEADS * QK_DIM, Q_LORA)
        g_out = g["out"].reshape(MLA_HEADS, HEAD_PAD, d)[:, 64:].reshape(MLA_HEADS * 64, d)
        return [(a.astype(BF), 0) for a in (g_out, g_in, g_qb, g["kvb_t"])]

    def ssm_in_grad(which):
        def items(c):
            if "g_ssm_in_t" not in c:
                g = c["g_ssm"]
                g_in_t = jnp.concatenate([g["z_t"], g["x_t"], g["b_t"], g["c_t"], heads_of(g["dt_t"])], axis=0)
                c["g_ssm_in_t"] = g_in_t.astype(BF).reshape(N_DEV, SSM_IN_SHARD, d)
            part = c["g_ssm_in_t"][:, which * half_rows:(which + 1) * half_rows]
            return [(part.reshape(N_DEV * half_rows // 2, 16, LANES), 0)]
        return items

    def scatter_on(host, items_of, keys):
        plan.ride(host, lambda c: _Exchange("scatter", items_of(c)),
                  lambda r, c: c.update(dict(zip(keys, r))))

    of = lambda key, k: (lambda c: [(c[key][k], 0)])
    scatter_on("ffn2_bwd_act", of("g_ffn3", 0), ("l3_gate",))
    scatter_on("ffn2_bwd_dh", of("g_ffn3", 1), ("l3_up",))
    scatter_on("ffn2_wg", of("g_ffn3", 2), ("l3_down",))
    scatter_on("ffn2_wu", mla_grad_items, ("l_mla_out", "l_mla_in", "l_qb", "l_kvb"))
    scatter_on("ffn1_bwd_act", of("g_ffn2", 0), ("l2_gate",))
    scatter_on("ffn1_bwd_dh", of("g_ffn2", 1), ("l2_up",))
    scatter_on("ffn1_wg", of("g_ffn2", 2), ("l2_down",))
    scatter_on("ssm_bwd_dgn", of("g_ffn1", 0), ("l1_gate",))
    scatter_on("ssm_bwd_dh", lambda c: [(c["g_ffn1"][1], 0), (c["g_ffn1"][2], 0)], ("l1_up", "l1_down"))
    scatter_on("ffn0_bwd_act", ssm_in_grad(0), ("l_ssm_in_a",))
    scatter_on("ffn0_bwd_dh", ssm_in_grad(1), ("l_ssm_in_b",))
    scatter_on("ffn0_wg", lambda c: [(c["g_ssm"]["out"], 0)], ("l_ssm_out",))
    scatter_on("ffn0_wu", lambda c: [(c["g_ffn0_gate"], 0)], ("l0_gate",))
    scatter_on("ffn0_wd", lambda c: [(c["g_ffn0_up"], 0)], ("l0_up",))

    loss_cols, grad_x, dns = _local_step(x, positions, loss_target, ctx, plan)
    loss = lax.psum(jnp.sum(loss_cols), ("x", "y", "c"))
    dn00, dn01, dn02, dn10, dn11, dn12 = dns
    g_ssm, g_mla = ctx["g_ssm"], ctx["g_mla"]
    ctx["l0_down"] = _Exchange("scatter", [(ctx["g_ffn0"][2], 0)]).run("scatter_last")[0]
    l_ffn = {k: jnp.stack([ctx["l%d_%s" % (n, k)] for n in range(4)], axis=1) for k in ("gate", "up", "down")}
    l_mla_out, l_mla_in, l_qb, l_kvb = (ctx[k] for k in ("l_mla_out", "l_mla_in", "l_qb", "l_kvb"))
    l_ssm_in = jnp.concatenate([ctx[k].reshape(N_DEV, half_rows, d) for k in ("l_ssm_in_a", "l_ssm_in_b")], axis=1)
    l_ssm_out = ctx["l_ssm_out"]

    unlane = lambda a: a.reshape(SSM_GROUPS, LANES)[:, :SSM_HPG].reshape(1, SSM_HEADS)
    small_g = jnp.concatenate([
        jnp.concatenate([dn00, dn01, dn02, dn10, dn11, dn12], axis=0).reshape(-1),
        g_ssm["conv_w"].reshape(-1), g_ssm["conv_b"].reshape(-1), unlane(g_ssm["dt_bias"]).reshape(-1),
        unlane(g_ssm["a_log"]).reshape(-1), unlane(g_ssm["d_skip"]).reshape(-1), g_ssm["ssm_norm_w"].reshape(-1),
        g_mla["q_a_norm"].reshape(-1), g_mla["kv_a_norm"].reshape(-1), g_mla["q_norm"][0, :QK_DIM],
        g_mla["k_norm"][0, :QK_DIM]])
    n_small = small_g.shape[0]
    n_small_pad = -(-n_small // (8 * LANES)) * (8 * LANES)
    small_g = jnp.pad(small_g, (0, n_small_pad - n_small)).reshape(n_small_pad // LANES, LANES)
    gs = _Exchange("gather", [(small_g, 0)]).run("gather_small_grads")[0].reshape(N_DEV, n_small_pad)

    outs = {}

    def put(name, res, shape):
        for key, val in zip(("grad", "delta", "new_m", "new_v"), res):
            outs[(key, name)] = val.reshape(shape)

    ck = 256
    for key, name, w, m, v in (("gate", "ffn_w_gate", ffn_w_gate, m_ffn_w_gate, v_ffn_w_gate),
                               ("up", "ffn_w_up", ffn_w_up, m_ffn_w_up, v_ffn_w_up)):
        res = _adam("adam_" + name, l_ffn[key], (N_DEV, None, FF_SHARD, ck), lambda a, i: (0, a, 0, i),
                    w.reshape(4, d, FF_SHARD), m.reshape(4, d, FF_SHARD), v.reshape(4, d, FF_SHARD), True, ck)
        put(name, res, w.shape)
    res = _adam("adam_ffn_w_down", l_ffn["down"], (N_DEV, None, 176, d), lambda a, i: (0, a, i, 0),
                ffn_w_down.reshape(4, FF_SHARD, d), m_ffn_w_down.reshape(4, FF_SHARD, d),
                v_ffn_w_down.reshape(4, FF_SHARD, d), False, 176)
    put("ffn_w_down", res, ffn_w_down.shape)
    l_ssm_in2 = l_ssm_in.reshape(N_DEV, SSM_IN_SHARD, d)
    res = _adam("adam_ssm_w_in", l_ssm_in2, (N_DEV, SSM_IN_SHARD, 128), lambda a, i: (0, 0, i),
                ssm_w_in, m_ssm_w_in, v_ssm_w_in, True, 128)
    put("ssm_w_in", res, ssm_w_in.shape)
    res = _adam("adam_ssm_w_out", l_ssm_out, (N_DEV, 128, d), lambda a, i: (0, i, 0),
                ssm_w_out, m_ssm_w_out, v_ssm_w_out, False, 128)
    put("ssm_w_out", res, ssm_w_out.shape)
    res = _adam("adam_mla_w_in", l_mla_in, (N_DEV, 128, MLA_IN_DIM), lambda a, i: (0, 0, 0),
                mla_w_in, m_mla_w_in, v_mla_w_in, False, 128)
    put("mla_w_in", res, mla_w_in.shape)
    res = _adam("adam_mla_w_q_b", l_qb, (N_DEV, 192, 128), lambda a, i: (0, 0, i),
                mla_w_q_b, m_mla_w_q_b, v_mla_w_q_b, True, 128)
    put("mla_w_q_b", res, mla_w_q_b.shape)
    res = _adam("adam_mla_w_kv_b", l_kvb, (N_DEV, 256, 128), lambda a, i: (0, 0, i),
                mla_w_kv_b, m_mla_w_kv_b, v_mla_w_kv_b, True, 128)
    put("mla_w_kv_b", res, mla_w_kv_b.shape)
    res = _adam("adam_mla_w_out", l_mla_out, (N_DEV, 128, d), lambda a, i: (0, 0, 0),
                mla_w_out, m_mla_w_out, v_mla_w_out, False, 128)
    put("mla_w_out", res, mla_w_out.shape)

    small_params = (
        ("norm_w", norm_w, m_norm_w, v_norm_w, 6 * d, 6, 128), ("ssm_conv_w", ssm_conv_w, m_ssm_conv_w, v_ssm_conv_w,
                                                               CONV_K * CONV_DIM, CONV_K, 512),
        ("ssm_conv_b", ssm_conv_b, m_ssm_conv_b, v_ssm_conv_b, CONV_DIM, 0, 0),
        ("ssm_dt_bias", ssm_dt_bias, m_ssm_dt_bias, v_ssm_dt_bias, SSM_HEADS, 0, 0),
        ("ssm_a_log", ssm_a_log, m_ssm_a_log, v_ssm_a_log, SSM_HEADS, 0, 0),
        ("ssm_d", ssm_d, m_ssm_d, v_ssm_d, SSM_HEADS, 0, 0),
        ("ssm_norm_w", ssm_norm_w, m_ssm_norm_w, v_ssm_norm_w, D_INNER, 0, 0),
        ("mla_q_a_norm", mla_q_a_norm, m_mla_q_a_norm, v_mla_q_a_norm, Q_LORA, 1, 48),
        ("mla_kv_a_norm", mla_kv_a_norm, m_mla_kv_a_norm, v_mla_kv_a_norm, KV_LORA, 1, 32),
        ("mla_q_norm", mla_q_norm, m_mla_q_norm, v_mla_q_norm, QK_DIM, 0, 0),
        ("mla_k_norm", mla_k_norm, m_mla_k_norm, v_mla_k_norm, QK_DIM, 0, 0),
    )
    parts, ws_, ms_, vs_, off = [], [], [], [], 0
    for name, w, m, v, full, rows, shard in small_params:
        seg = gs[:, off:off + full]
        if rows:
            seg = _dyn(seg.reshape(N_DEV, rows, full // rows), me * shard, shard).reshape(N_DEV, rows * shard)
        parts.append(seg)
        ws_.append(w.reshape(1, -1))
        ms_.append(m.reshape(1, -1))
        vs_.append(v.reshape(1, -1))
        off += full
    n_loc = sum(p.shape[1] for p in parts)
    n_loc_pad = -(-n_loc // LANES) * LANES
    padc = lambda a, val=0.0: jnp.pad(jnp.concatenate(a, axis=1), ((0, 0), (0, n_loc_pad - n_loc)),
                                      constant_values=val)
    res = _adam("adam_small", padc(parts).reshape(N_DEV, 1, n_loc_pad), (N_DEV, 1, n_loc_pad), lambda a, i: (0, 0, 0),
                padc(ws_).reshape(1, 1, n_loc_pad), padc(ms_).reshape(1, 1, n_loc_pad),
                padc(vs_, 1.0).reshape(1, 1, n_loc_pad), False, 1)
    off = 0
    for name, w, m, v, full, rows, shard in small_params:
        nloc = w.size
        put(name, [r.reshape(-1)[off:off + nloc] for r in res], w.shape)
        off += nloc

    order = ("norm_w", "ffn_w_gate", "ffn_w_up", "ffn_w_down", "ssm_w_in", "ssm_conv_w", "ssm_conv_b", "ssm_dt_bias",
             "ssm_a_log", "ssm_d", "ssm_norm_w", "ssm_w_out", "mla_w_in", "mla_q_a_norm", "mla_kv_a_norm",
             "mla_w_q_b", "mla_w_kv_b", "mla_q_norm", "mla_k_norm", "mla_w_out")
    return (loss, grad_x, *[outs[(k, n)] for k in ("grad", "delta", "new_m", "new_v") for n in order])
```

```python
import functools
import math

import jax
import jax.numpy as jnp
import numpy as np
from jax import lax
from jax.experimental import pallas as pl
from jax.experimental.pallas import tpu as pltpu

F32 = jnp.float32
BF = jnp.bfloat16
SDS = jax.ShapeDtypeStruct

N_DEV = 8
D_MODEL = 1024
D_FF = 2816
FF_SHARD = D_FF // N_DEV
D_INNER = 2048
SSM_HEADS = 32
SSM_GROUPS = 8
SSM_HPG = 4
SSM_STATE = 128
CONV_K = 4
CONV_DIM = 4096
SSM_IN_DIM = 6176
SSM_IN_SHARD = SSM_IN_DIM // N_DEV
NORM_GROUP = 256
CHUNK = 128
MLA_HEADS = 16
Q_LORA = 384
KV_LORA = 256
QK_NOPE = 64
QK_ROPE = 32
QK_DIM = 96
MLA_IN_DIM = 672
HEAD_PAD = 128
ROPE_THETA = 10000.0
EPS = 1e-6
LANES = 128

ADAM_LR = 0.001
ADAM_B1 = 0.9
ADAM_B2 = 0.999
ADAM_EPS = 1e-08
ADAM_WD = 0.01
ADAM_STEP = 10

VMEM_BIG = 56 * 1024 * 1024

NN = ((1,), (0,))
NT = ((1,), (1,))
TN = ((0,), (0,))


def _dotf(a, b, dn):
    return lax.dot_general(a.astype(BF), b.astype(BF), (dn, ((), ())), preferred_element_type=F32)


def _dot_hi(a, b, dn=NN):
    return lax.dot_general(a, b, (dn, ((), ())), precision=lax.Precision.HIGHEST, preferred_element_type=F32)


def _sigmoid(x):
    return jax.nn.sigmoid(x)


def _silu(x):
    return x * _sigmoid(x)


def _softplus(x):
    return jnp.maximum(x, 0.0) + jnp.log(1.0 + jnp.exp(-jnp.abs(x)))


def _cparams(n_grid, vmem=None):
    return pltpu.CompilerParams(dimension_semantics=("arbitrary",) * n_grid, vmem_limit_bytes=vmem)


def _fmm(name, grid_mn, pairs, outs, *, epi=None, extras=(), n_acc=1, acc_shape=None, vmem=None, alias=None,
         joint=False, hosts=None, row_split=1):
    comm = hosts.comm(name) if hosts is not None else None
    if joint:
        nk_total = pairs[0][7]
        assert all(p[7] == nk_total for p in pairs)
        starts = [0] * len(pairs)
    else:
        nk_total = sum(p[7] for p in pairs)
        starts = []
        s = 0
        for p in pairs:
            starts.append(s)
            s += p[7]
    n_pairs, n_extras, n_outs = len(pairs), len(extras), len(outs)
    single = nk_total == 1

    n_ci = len(comm.arrays) if comm is not None else 0
    n_co = len(comm.out_shapes) if comm is not None else 0
    n_scratch_acc = 0 if single else n_acc

    def body(*refs):
        ab_refs = refs[: 2 * n_pairs]
        e_refs = refs[2 * n_pairs: 2 * n_pairs + n_extras]
        pos = 2 * n_pairs + n_extras + (1 if alias is not None else 0)
        ci_refs = refs[pos: pos + n_ci]
        pos += n_ci
        o_refs = refs[pos: pos + n_outs]
        co_refs = refs[pos + n_outs: pos + n_outs + n_co]
        pos += n_outs + n_co
        acc_refs = refs[pos: pos + n_scratch_acc]
        sem_refs = refs[pos + n_scratch_acc:]
        i, j, k = pl.program_id(0), pl.program_id(1), pl.program_id(2)

        if comm is not None:
            @pl.when((i == 0) & (j == 0) & (k == 0))
            def _():
                comm.start(ci_refs, co_refs, sem_refs)

        compute(ab_refs, e_refs, o_refs, acc_refs, i, j, k)

        if comm is not None:
            @pl.when((i == grid_mn[0] - 1) & (j == grid_mn[1] - 1) & (k == nk_total - 1))
            def _():
                comm.wait(ci_refs, co_refs, sem_refs)

    def compute(ab_refs, e_refs, o_refs, acc_refs, i, j, k):

        def finish(accs, rows=slice(None)):
            res = epi(accs, *[e[rows] for e in e_refs]) if epi is not None else accs
            if not isinstance(res, (tuple, list)):
                res = (res,)
            first = (i == 0) & (j == 0)
            for o, r, spec in zip(o_refs, res, outs):
                if spec[3]:
                    @pl.when(first)
                    def _(o=o, r=r):
                        o[...] = r.astype(o.dtype)

                    @pl.when(jnp.logical_not(first))
                    def _(o=o, r=r):
                        o[...] += r.astype(o.dtype)
                else:
                    o[rows] = r.astype(o.dtype)

        if single:
            tm_all = ab_refs[0].shape[0]
            ch = tm_all // row_split
            for c in range(row_split):
                rows = slice(c * ch, (c + 1) * ch) if row_split > 1 else slice(None)
                accs = [None] * n_acc
                for p, pr in enumerate(pairs):
                    d = _dotf(ab_refs[2 * p][rows], ab_refs[2 * p + 1][...], pr[6])
                    accs[pr[8]] = d if accs[pr[8]] is None else accs[pr[8]] + d
                finish(accs, rows)
            return

        @pl.when(k == 0)
        def _():
            for a in acc_refs:
                a[...] = jnp.zeros_like(a)

        for p, pr in enumerate(pairs):
            def step(p=p, pr=pr):
                acc_refs[pr[8]][...] += _dotf(ab_refs[2 * p][...], ab_refs[2 * p + 1][...], pr[6])

            if n_pairs == 1 or joint:
                step()
            else:
                pl.when((k >= starts[p]) & (k < starts[p] + pr[7]))(step)

        @pl.when(k == nk_total - 1)
        def _():
            finish([a[...] for a in acc_refs])

    in_specs, args = [], []
    for p, pr in enumerate(pairs):
        a, a_blk, a_idx, b, b_blk, b_idx, _, nk, _ = pr
        st = starts[p]

        def amap(i, j, k, a_idx=a_idx, st=st, nk=nk):
            return a_idx(i, j, jnp.clip(k - st, 0, nk - 1))

        def bmap(i, j, k, b_idx=b_idx, st=st, nk=nk):
            return b_idx(i, j, jnp.clip(k - st, 0, nk - 1))

        in_specs += [pl.BlockSpec(a_blk, amap), pl.BlockSpec(b_blk, bmap)]
        args += [a, b]
    for arr, blk, idx in extras:
        in_specs.append(pl.BlockSpec(blk, lambda i, j, k, idx=idx: idx(i, j)))
        args.append(arr)
    io_alias = {}
    if alias is not None:
        in_specs.append(pl.BlockSpec(memory_space=pl.ANY))
        io_alias = {len(args): 0}
        args.append(alias)
    out_specs = [pl.BlockSpec(blk, lambda i, j, k, idx=idx: idx(i, j)) for _, blk, idx, _ in outs]
    out_shape = [o[0] for o in outs]
    scratch = [] if single else [pltpu.VMEM(acc_shape, F32) for _ in range(n_acc)]
    if comm is not None:
        hbm = pl.BlockSpec(memory_space=pl.ANY)
        in_specs += [hbm] * n_ci
        args += list(comm.arrays)
        out_specs += [hbm] * n_co
        out_shape += list(comm.out_shapes)
        scratch += comm.sems
    res = pl.pallas_call(
        body, name=name, grid=(grid_mn[0], grid_mn[1], nk_total), in_specs=in_specs, out_specs=out_specs,
        out_shape=out_shape, scratch_shapes=scratch, input_output_aliases=io_alias,
        compiler_params=_cparams(3, vmem),
    )(*args)
    if comm is not None:
        hosts.done(name, res[n_outs:])
    return res[:n_outs]


def _ew(name, grid, fn, ins, outs, *, acc_axes=(), vmem=None):
    n_in = len(ins)

    def body(*refs):
        res = fn(*[r[...] for r in refs[:n_in]])
        if not isinstance(res, (tuple, list)):
            res = (res,)
        first = None
        for ax in acc_axes:
            c = pl.program_id(ax) == 0
            first = c if first is None else (first & c)
        for o, r, spec in zip(refs[n_in:], res, outs):
            if spec[3]:
                @pl.when(first)
                def _(o=o, r=r):
                    o[...] = r.astype(o.dtype)

                @pl.when(jnp.logical_not(first))
                def _(o=o, r=r):
                    o[...] += r.astype(o.dtype)
            else:
                o[...] = r.astype(o.dtype)

    return pl.pallas_call(
        body, name=name, grid=grid,
        in_specs=[pl.BlockSpec(blk, idx) for _, blk, idx in ins],
        out_specs=[pl.BlockSpec(blk, idx) for _, blk, idx, _ in outs],
        out_shape=[o[0] for o in outs],
        compiler_params=_cparams(len(grid), vmem),
    )(*[a for a, _, _ in ins])


def _row_tile(t, want):
    tm = min(want, t)
    assert t % tm == 0, (t, tm)
    return tm


def _rms_fn(x, w):
    return x * lax.rsqrt(jnp.mean(x * x, axis=-1, keepdims=True) + EPS) * w


def _rms_fwd(name, x, w):
    t, d = x.shape
    tm = _row_tile(t, 512)
    return _ew(name, (t // tm,), _rms_fn,
               [(x, (tm, d), lambda i: (i, 0)), (w, (1, d), lambda i: (0, 0))],
               [(SDS((t, d), BF), (tm, d), lambda i: (i, 0), False)])[0]


def _rms_bwd(name, x, w, dh, dres):
    t, d = x.shape
    tm = _row_tile(t, 512)

    def fn(x, w, dh, dres):
        _, vjp = jax.vjp(_rms_fn, x, w)
        dx, dw = vjp(dh.astype(F32))
        return dx + dres, dw

    row = lambda i: (i, 0)
    return _ew(name, (t // tm,), fn,
               [(x, (tm, d), row), (w, (1, d), lambda i: (0, 0)), (dh, (tm, d), row), (dres, (tm, d), row)],
               [(SDS((t, d), F32), (tm, d), row, False), (SDS((1, d), F32), (1, d), lambda i: (0, 0), True)],
               acc_axes=(0,))


def _loss_and_grad(y, target):
    t, d = y.shape
    tm = _row_tile(t, 512)

    def fn(y, tg):
        e = y - tg
        return e * (1.0 / d), jnp.sum(e * e, axis=0, keepdims=True) * (0.5 / d)

    row = lambda i: (i, 0)
    return _ew("loss_head", (t // tm,), fn, [(y, (tm, d), row), (target, (tm, d), row)],
               [(SDS((t, d), F32), (tm, d), row, False), (SDS((1, d), F32), (1, d), lambda i: (0, 0), True)],
               acc_axes=(0,))


def _ffn_fwd(tag, x, nw, wf, hosts=None):
    gate_t, up_t = wf[0], wf[1]
    t, d = x.shape
    h = _rms_fwd(tag + "_rms", x, nw)
    tm, tn = _row_tile(t, 256), D_FF

    def epi(accs):
        g, u = accs
        s = _sigmoid(g)
        sg = g * s
        return 0.5 * sg, 0.5 * (s * (1.0 + g * (1.0 - s))), u, sg * u

    hblk = (h, (tm, d), lambda i, j, k: (i, 0))
    col = lambda i, j: (i, j)
    tblk = lambda w: (w, (tn, d), lambda i, j, k: (j, 0), NT, 1)
    sgh, dsgh, u, a = _fmm(
        tag + "_up", (t // tm, D_FF // tn), [hblk + tblk(gate_t) + (0,), hblk + tblk(up_t) + (1,)],
        [(SDS((t, D_FF), BF), (tm, tn), col, False)] * 4, epi=epi, n_acc=2, joint=True, hosts=hosts, vmem=VMEM_BIG)
    down = wf[2]
    tm2 = _row_tile(t, 512)
    y = _fmm(
        tag + "_down", (t // tm2, 1),
        [(a, (tm2, D_FF), lambda i, j, k: (i, 0), down, (D_FF, d), lambda i, j, k: (0, 0), NN, 1, 0)],
        [(SDS((t, d), F32), (tm2, d), lambda i, j: (i, 0), False)],
        extras=[(x, (tm2, d), lambda i, j: (i, 0))],
        epi=lambda accs, xr: xr + 0.5 * accs[0], vmem=VMEM_BIG, hosts=hosts)[0]
    return y, (x, h, sgh, dsgh, u, a)


def _wgrad(name, a, b, m, n, *, tm, tn, tk=2048, scale=None, out_dtype=BF, hosts=None):
    t = a.shape[0]
    tk = _row_tile(t, tk)
    epi = (lambda accs: accs[0] * scale) if scale is not None else None
    return _fmm(name, (m // tm, n // tn),
                [(a, (tk, tm), lambda i, j, k: (k, i), b, (tk, tn), lambda i, j, k: (k, j), TN, t // tk, 0)],
                [(SDS((m, n), out_dtype), (tm, tn), lambda i, j: (i, j), False)], epi=epi, acc_shape=(tm, tn),
                vmem=VMEM_BIG, hosts=hosts)[0]


def _ffn_bwd(tag, dy, nw, wf, saved, hosts=None):
    gate_t, up_t, down = wf
    x, h, sgh, dsgh, u, a = saved
    t, d = x.shape
    tm, tn = _row_tile(t, 256), D_FF

    def epi(accs, sgh, dsgh, u):
        da = accs[0]
        return da * (u.astype(F32) * dsgh.astype(F32)), da * sgh.astype(F32)

    col = lambda i, j: (i, j)
    dg, du = _fmm(
        tag + "_bwd_act", (t // tm, D_FF // tn),
        [(dy, (tm, d), lambda i, j, k: (i, 0), down, (tn, d), lambda i, j, k: (j, 0), NT, 1, 0)],
        [(SDS((t, D_FF), BF), (tm, tn), col, False)] * 2,
        extras=[(sgh, (tm, tn), col), (dsgh, (tm, tn), col), (u, (tm, tn), col)], epi=epi, hosts=hosts,
        vmem=VMEM_BIG)
    tm2 = _row_tile(t, 256)
    full = lambda i, j, k: (0, 0)
    dh = _fmm(
        tag + "_bwd_dh", (t // tm2, 1),
        [(dg, (tm2, D_FF), lambda i, j, k: (i, 0), gate_t, (D_FF, d), full, NN, 1, 0),
         (du, (tm2, D_FF), lambda i, j, k: (i, 0), up_t, (D_FF, d), full, NN, 1, 0)],
        [(SDS((t, d), BF), (tm2, d), lambda i, j: (i, 0), False)], vmem=VMEM_BIG, joint=True, hosts=hosts)[0]
    dx, dnw = _rms_bwd(tag + "_bwd_rms", x, nw, dh, dy)
    half = D_FF // 2
    g_gate = _wgrad(tag + "_wg", dg, h, D_FF, d, tm=half, tn=d, hosts=hosts)
    if hosts is not None:
        hosts.ctx["g_" + tag + "_gate"] = g_gate
    g_up = _wgrad(tag + "_wu", du, h, D_FF, d, tm=half, tn=d, hosts=hosts)
    if hosts is not None:
        hosts.ctx["g_" + tag + "_up"] = g_up
    g_down = _wgrad(tag + "_wd", a, dy, D_FF, d, tm=half, tn=d, scale=0.5, hosts=hosts)
    return dx, dnw, (g_gate, g_up, g_down)


def _shift_down(cur, prev8, j):
    rolled = pltpu.roll(cur, j, 0)
    sub = lax.broadcasted_iota(jnp.int32, prev8.shape, 0)
    top = jnp.where(sub < j, pltpu.roll(prev8, j, 0), rolled[:8])
    return jnp.concatenate([top, rolled[8:]], axis=0)


def _shift_up(cur, next8, j):
    n = cur.shape[0]
    rolled = pltpu.roll(cur, n - j, 0)
    sub = lax.broadcasted_iota(jnp.int32, next8.shape, 0)
    bot = jnp.where(sub >= 8 - j, pltpu.roll(next8, 8 - j, 0), rolled[n - 8:])
    return jnp.concatenate([rolled[: n - 8], bot], axis=0)


HALO = 16


def _conv_fwd(xbc, w, b, seq):
    t, c = xbc.shape
    ts, tc = _row_tile(seq, 256), 512
    tiles_per_seq = seq // ts
    hb = ts // HALO

    def fn(cur, prev, w, b):
        i = pl.program_id(1)
        cur = cur.astype(F32)
        prev8 = jnp.where(i % tiles_per_seq == 0, 0.0, prev.astype(F32)[HALO - 8:])
        out = b + w[3:4] * cur
        for j in range(1, CONV_K):
            out = out + w[3 - j:4 - j] * _shift_down(cur, prev8, j)
        return out, _silu(out)

    return _ew("ssm_conv_fwd", (c // tc, t // ts), fn,
               [(xbc, (ts, tc), lambda j, i: (i, j)),
                (xbc, (HALO, tc), lambda j, i: (jnp.maximum(i * hb - 1, 0), j)),
                (w, (CONV_K, tc), lambda j, i: (0, j)), (b, (1, tc), lambda j, i: (0, j))],
               [(SDS((t, c), BF), (ts, tc), lambda j, i: (i, j), False)] * 2)


def _conv_bwd(tag, dxa, cpre, xbc, w, col0, seq):
    t, width = dxa.shape
    ts, tc = _row_tile(seq, 256), 512
    tiles_per_seq = seq // ts
    hb = ts // HALO
    cb0 = col0 // tc
    n_halo_blocks = t // HALO

    def dsilu(cv, dv):
        cv = cv.astype(F32)
        s = _sigmoid(cv)
        return dv.astype(F32) * (s * (1.0 + cv * (1.0 - s)))

    def fn(dxa_c, dxa_n, c_c, c_n, x_c, x_p, w):
        i = pl.program_id(1)
        dc = dsilu(c_c, dxa_c)
        last = i % tiles_per_seq == tiles_per_seq - 1
        dc_n = jnp.where(last, 0.0, dsilu(c_n, dxa_n)[:8])
        dx = w[3:4] * dc
        for j in range(1, CONV_K):
            dx = dx + w[3 - j:4 - j] * _shift_up(dc, dc_n, j)
        cur = x_c.astype(F32)
        prev8 = jnp.where(i % tiles_per_seq == 0, 0.0, x_p.astype(F32)[HALO - 8:])
        rows = [jnp.sum(dc * cur, axis=0, keepdims=True)]
        for j in range(1, CONV_K):
            rows.append(jnp.sum(dc * _shift_down(cur, prev8, j), axis=0, keepdims=True))
        sub8 = lax.broadcasted_iota(jnp.int32, (8, dc.shape[1]), 0)
        dw = jnp.zeros((8, dc.shape[1]), F32)
        for kk in range(CONV_K):
            dw = jnp.where(sub8 == kk, rows[CONV_K - 1 - kk], dw)
        return dx, dw, jnp.sum(dc, axis=0, keepdims=True)

    nxt = lambda j, i: (jnp.minimum((i + 1) * hb, n_halo_blocks - 1), j)
    nxt_off = lambda j, i: (jnp.minimum((i + 1) * hb, n_halo_blocks - 1), j + cb0)
    return _ew(tag, (width // tc, t // ts), fn,
               [(dxa, (ts, tc), lambda j, i: (i, j)), (dxa, (HALO, tc), nxt),
                (cpre, (ts, tc), lambda j, i: (i, j + cb0)), (cpre, (HALO, tc), nxt_off),
                (xbc, (ts, tc), lambda j, i: (i, j + cb0)),
                (xbc, (HALO, tc), lambda j, i: (jnp.maximum(i * hb - 1, 0), j + cb0)),
                (w, (CONV_K, tc), lambda j, i: (0, j + cb0))],
               [(SDS((t, width), BF), (ts, tc), lambda j, i: (i, j), False),
                (SDS((8, width), F32), (8, tc), lambda j, i: (0, j), True),
                (SDS((1, width), F32), (1, tc), lambda j, i: (0, j), True)],
               acc_axes=(1,))


def _ssd_chunk(xs, bm, cm, dtr, st, dtb, alog, dsk):
    ell = xs.shape[0]
    xs = xs.astype(F32)
    lane = lax.broadcasted_iota(jnp.int32, (ell, LANES), 1)
    sub = lax.broadcasted_iota(jnp.int32, (ell, LANES), 0)
    lane1 = lax.broadcasted_iota(jnp.int32, (1, LANES), 1)
    causal = sub >= lane
    dt = _softplus(dtr + dtb)
    da = dt * (-jnp.exp(alog))
    acs = _dot_hi(causal.astype(F32), da)
    acs_t = acs.T
    cb = _dotf(cm, bm, NT)
    lo = lane < 64
    ys, news = [], []
    for p in range(2):
        xp = xs[:, LANES * p:LANES * (p + 1)]
        sp = st[LANES * p:LANES * (p + 1), :]
        col, dtc, last, dsel = [], [], [], []
        y_diag = None
        for q in range(2):
            r = 2 * p + q
            col_r = jnp.sum(jnp.where(lane == r, acs, 0.0), axis=1, keepdims=True)
            row_r = jnp.sum(jnp.where(sub == r, acs_t, 0.0), axis=0, keepdims=True)
            dtc_r = jnp.sum(jnp.where(lane == r, dt, 0.0), axis=1, keepdims=True)
            decay = jnp.exp(jnp.where(causal, col_r - row_r, -jnp.inf))
            head = lo if q == 0 else jnp.logical_not(lo)
            d = _dotf(cb * decay, jnp.where(head, xp * dtc_r, 0.0), NN)
            y_diag = d if y_diag is None else y_diag + d
            col.append(col_r)
            dtc.append(dtc_r)
            last.append(jnp.sum(jnp.where(sub[:, :1] == ell - 1, col_r, 0.0), axis=0, keepdims=True))
            dsel.append(jnp.sum(jnp.where(lane1 == r, dsk, 0.0), axis=1, keepdims=True))
        y_off = _dotf(cm, sp, NT) * jnp.where(lo, jnp.exp(col[0]), jnp.exp(col[1]))
        xw = jnp.where(lo, xp * (dtc[0] * jnp.exp(last[0] - col[0])), xp * (dtc[1] * jnp.exp(last[1] - col[1])))
        new = sp * jnp.where(sub < 64, jnp.exp(last[0]), jnp.exp(last[1])) + _dotf(xw, bm, TN)
        ys.append(y_diag + y_off + jnp.where(lo, dsel[0], dsel[1]) * xp)
        news.append(new)
    return jnp.concatenate(ys, axis=1), jnp.concatenate(news, axis=0)


SSD_GP = 4
SSD_GP_BWD = 4
XW, GW = 2 * LANES * SSD_GP, LANES * SSD_GP


def _grp(ref, q, width):
    return ref[:, width * q:width * (q + 1)]


def _ssd_fwd(xa, dtr, dtb, alog, dsk, nb, seq):
    t = xa.shape[0]
    nc = seq // CHUNK
    row = lambda g, b, c: (b * nc + c, g)
    par = pl.BlockSpec((1, GW), lambda g, b, c: (0, g))
    b_off, c_off = D_INNER // GW, (D_INNER + SSM_GROUPS * SSM_STATE) // GW

    def body(xs, bm, cm, dtr, dtb, alog, dsk, y_ref, st_out, st_ref):
        @pl.when(pl.program_id(2) == 0)
        def _():
            st_ref[...] = jnp.zeros_like(st_ref)

        ins = [(_grp(xs, q, 2 * LANES), _grp(bm, q, LANES), _grp(cm, q, LANES), _grp(dtr, q, LANES), st_ref[q],
                _grp(dtb, q, LANES), _grp(alog, q, LANES), _grp(dsk, q, LANES)) for q in range(SSD_GP)]
        res = [_ssd_chunk(*a) for a in ins]
        for q in range(SSD_GP):
            st_out[q] = ins[q][4]
            y_ref[:, 2 * LANES * q:2 * LANES * (q + 1)] = res[q][0]
            st_ref[q] = res[q][1]

    specs = [pl.BlockSpec((CHUNK, XW), row),
             pl.BlockSpec((CHUNK, GW), lambda g, b, c: (b * nc + c, b_off + g)),
             pl.BlockSpec((CHUNK, GW), lambda g, b, c: (b * nc + c, c_off + g)),
             pl.BlockSpec((CHUNK, GW), row), par, par, par]
    return pl.pallas_call(
        body, name="ssd_fwd", grid=(SSM_GROUPS // SSD_GP, nb, nc), in_specs=specs,
        out_specs=[pl.BlockSpec((CHUNK, XW), row),
                   pl.BlockSpec((SSD_GP, None, None, 2 * LANES, LANES), lambda g, b, c: (g, b, c, 0, 0))],
        out_shape=[SDS((t, D_INNER), F32), SDS((SSM_GROUPS, nb, nc, 2 * LANES, LANES), F32)],
        scratch_shapes=[pltpu.VMEM((SSD_GP, 2 * LANES, LANES), F32)],
        compiler_params=_cparams(3),
    )(xa, xa, xa, dtr, dtb, alog, dsk)


def _ssd_bwd(xa, dtr, dtb, alog, dsk, states, dy, nb, seq):
    t = xa.shape[0]
    nc = seq // CHUNK
    rev = lambda c: nc - 1 - c
    row = lambda g, b, c: (b * nc + rev(c), g)
    gp = SSD_GP_BWD
    xw, gw = 2 * LANES * gp, LANES * gp
    par = pl.BlockSpec((1, gw), lambda g, b, c: (0, g))
    b_off, c_off = D_INNER // gw, (D_INNER + SSM_GROUPS * SSM_STATE) // gw

    def body(xs, bm, cm, dtr, dtb, alog, dsk, st_in, dy, dxs, dbm, dcm, ddtr, ddtb, dalog, ddsk, dst_ref):
        @pl.when(pl.program_id(2) == 0)
        def _():
            dst_ref[...] = jnp.zeros_like(dst_ref)

        ins = [(_grp(xs, q, 2 * LANES), _grp(bm, q, LANES), _grp(cm, q, LANES), _grp(dtr, q, LANES), st_in[q],
                _grp(dtb, q, LANES), _grp(alog, q, LANES), _grp(dsk, q, LANES)) for q in range(gp)]
        cts = [(_grp(dy, q, 2 * LANES), dst_ref[q]) for q in range(gp)]
        gs = [jax.vjp(_ssd_chunk, *a)[1](ct) for a, ct in zip(ins, cts)]
        for q, g in enumerate(gs):
            dxs[:, 2 * LANES * q:2 * LANES * (q + 1)] = g[0]
            lanes = slice(LANES * q, LANES * (q + 1))
            dbm[:, lanes] = g[1]
            dcm[:, lanes] = g[2]
            ddtr[:, lanes] = g[3]
            dst_ref[q] = g[4]
        first = (pl.program_id(1) == 0) & (pl.program_id(2) == 0)
        for o, k in ((ddtb, 5), (dalog, 6), (ddsk, 7)):
            v = jnp.concatenate([g[k] for g in gs], axis=1)

            @pl.when(first)
            def _(o=o, v=v):
                o[...] = v

            @pl.when(jnp.logical_not(first))
            def _(o=o, v=v):
                o[...] += v

    in_specs = [
        pl.BlockSpec((CHUNK, xw), row),
        pl.BlockSpec((CHUNK, gw), lambda g, b, c: (b * nc + rev(c), b_off + g)),
        pl.BlockSpec((CHUNK, gw), lambda g, b, c: (b * nc + rev(c), c_off + g)),
        pl.BlockSpec((CHUNK, gw), row), par, par, par,
        pl.BlockSpec((gp, None, None, 2 * LANES, LANES), lambda g, b, c: (g, b, rev(c), 0, 0)),
        pl.BlockSpec((CHUNK, xw), row),
    ]
    out_specs = [pl.BlockSpec((CHUNK, xw), row), pl.BlockSpec((CHUNK, gw), row), pl.BlockSpec((CHUNK, gw), row),
                 pl.BlockSpec((CHUNK, gw), row), par, par, par]
    out_shape = [SDS((t, D_INNER), BF), SDS((t, SSM_GROUPS * LANES), BF), SDS((t, SSM_GROUPS * LANES), BF),
                 SDS((t, SSM_GROUPS * LANES), F32)] + [SDS((1, SSM_GROUPS * LANES), F32)] * 3
    return pl.pallas_call(
        body, name="ssd_bwd", grid=(SSM_GROUPS // gp, nb, nc), in_specs=in_specs, out_specs=out_specs,
        out_shape=out_shape, scratch_shapes=[pltpu.VMEM((gp, 2 * LANES, LANES), F32)],
        compiler_params=_cparams(3, VMEM_BIG),
    )(xa, xa, xa, dtr, dtb, alog, dsk, states, dy)


def _gated_fn(y, z, w):
    g = y * _silu(z.astype(F32))
    return g * lax.rsqrt(jnp.mean(g * g, axis=-1, keepdims=True) + EPS) * w


def _gated_norm_fwd(y, z, w):
    t = y.shape[0]
    tm = _row_tile(t, 512)
    blk = ((tm, NORM_GROUP), lambda g, i: (i, g))
    return _ew("ssm_gnorm_fwd", (SSM_GROUPS, t // tm), _gated_fn,
               [(y,) + blk, (z,) + blk, (w, (1, NORM_GROUP), lambda g, i: (0, g))],
               [(SDS((t, D_INNER), BF),) + blk + (False,)])[0]


def _gated_norm_bwd(y, z, w, dout):
    t = y.shape[0]
    tm = _row_tile(t, 512)
    blk = ((tm, NORM_GROUP), lambda g, i: (i, g))
    par = ((1, NORM_GROUP), lambda g, i: (0, g))

    def fn(y, z, w, dout):
        _, vjp = jax.vjp(_gated_fn, y, z, w)
        return vjp(dout.astype(F32))

    return _ew("ssm_gnorm_bwd", (SSM_GROUPS, t // tm), fn,
               [(y,) + blk, (z,) + blk, (w,) + par, (dout,) + blk],
               [(SDS((t, D_INNER), F32),) + blk + (False,), (SDS((t, D_INNER), BF),) + blk + (False,),
                (SDS((1, D_INNER), F32),) + par + (True,)],
               acc_axes=(1,))


def _proj_nt(name, h, wt, n, out_dtype, tn=256, hosts=None):
    t, kdim = h.shape
    tm = _row_tile(t, 1024)
    return _fmm(name, (t // tm, n // tn),
                [(h, (tm, kdim), lambda i, j, k: (i, 0), wt, (tn, kdim), lambda i, j, k: (j, 0), NT, 1, 0)],
                [(SDS((t, n), out_dtype), (tm, tn), lambda i, j: (i, j), False)], hosts=hosts)[0]


def _seg_nn(name, parts, n, out_dtype, tk=256, hosts=None):
    t = parts[0][0].shape[0]
    tm = _row_tile(t, 512)
    pairs = []
    for a, w, row0 in parts:
        kp = a.shape[1]
        tkp = min(tk, kp)
        r0 = row0 // tkp
        pairs.append((a, (tm, tkp), lambda i, j, k: (i, k), w, (tkp, n), lambda i, j, k, r0=r0: (k + r0, 0),
                      NN, kp // tkp, 0))
    return _fmm(name, (t // tm, 1), pairs, [(SDS((t, n), out_dtype), (tm, n), lambda i, j: (i, 0), False)],
                acc_shape=(tm, n), hosts=hosts)[0]


def _ssm_fwd(x, nw, ws, small, nb, seq, hosts=None):
    t, d = x.shape
    h = _rms_fwd("ssm_rms", x, nw)
    z = _proj_nt("ssm_in_z", h, ws["z_t"], D_INNER, BF, hosts=hosts)
    xbc = _proj_nt("ssm_in_xbc", h, ws["xbc_t"], CONV_DIM, BF, hosts=hosts)
    dtr = _proj_nt("ssm_in_dt", h, ws["dt_t"], SSM_GROUPS * LANES, F32)
    cpre, xa = _conv_fwd(xbc, small["conv_w"], small["conv_b"], seq)
    y, states = _ssd_fwd(xa, dtr, small["dt_bias"], small["a_log"], small["d_skip"], nb, seq)
    gn = _gated_norm_fwd(y, z, small["ssm_norm_w"])
    tm = _row_tile(t, 512)
    out = _fmm("ssm_out", (t // tm, 1),
               [(gn, (tm, D_INNER), lambda i, j, k: (i, 0), ws["out"], (D_INNER, d), lambda i, j, k: (0, 0), NN, 1, 0)],
               [(SDS((t, d), F32), (tm, d), lambda i, j: (i, 0), False)],
               extras=[(x, (tm, d), lambda i, j: (i, 0))], epi=lambda accs, xr: xr + accs[0], hosts=hosts)[0]
    return out, (x, h, z, xbc, dtr, cpre, xa, y, states, gn)


def _ssm_bwd(dy, nw, ws, small, saved, nb, seq, hosts=None):
    x, h, z, xbc, dtr, cpre, xa, y, states, gn = saved
    t, d = x.shape
    dgn = _proj_nt("ssm_bwd_dgn", dy, ws["out"], D_INNER, BF, hosts=hosts)
    d_out = _wgrad("ssm_w_out_g", gn, dy, D_INNER, d, tm=D_INNER // 2, tn=d)
    dyssd, dz, d_normw = _gated_norm_bwd(y, z, small["ssm_norm_w"], dgn)
    dxs, dbm, dcm, ddtr, d_dtb, d_alog, d_dsk = _ssd_bwd(
        xa, dtr, small["dt_bias"], small["a_log"], small["d_skip"], states, dyssd, nb, seq)
    dxbc_x, dcw_x, dcb_x = _conv_bwd("ssm_conv_bwd_x", dxs, cpre, xbc, small["conv_w"], 0, seq)
    dxbc_b, dcw_b, dcb_b = _conv_bwd("ssm_conv_bwd_b", dbm, cpre, xbc, small["conv_w"], D_INNER, seq)
    dxbc_c, dcw_c, dcb_c = _conv_bwd("ssm_conv_bwd_c", dcm, cpre, xbc, small["conv_w"], D_INNER + 1024, seq)
    parts = [(dz, ws["z_t"], 0), (dxbc_x, ws["xbc_t"], 0), (dxbc_b, ws["xbc_t"], D_INNER),
             (dxbc_c, ws["xbc_t"], D_INNER + 1024), (ddtr, ws["dt_t"], 0)]
    dh = _seg_nn("ssm_bwd_dh", parts, d, BF, tk=1024, hosts=hosts)
    dx, dnw = _rms_bwd("ssm_bwd_rms", x, nw, dh, dy)
    g = {
        "z_t": _wgrad("ssm_w_z_g", dz, h, D_INNER, d, tm=1024, tn=d, out_dtype=F32),
        "x_t": _wgrad("ssm_w_x_g", dxbc_x, h, D_INNER, d, tm=1024, tn=d, out_dtype=F32),
        "b_t": _wgrad("ssm_w_b_g", dxbc_b, h, 1024, d, tm=1024, tn=d, out_dtype=F32),
        "c_t": _wgrad("ssm_w_c_g", dxbc_c, h, 1024, d, tm=1024, tn=d, out_dtype=F32),
        "dt_t": _wgrad("ssm_w_dt_g", ddtr, h, 1024, d, tm=1024, tn=d, out_dtype=F32),
        "out": d_out,
        "conv_w": jnp.concatenate([dcw_x[:CONV_K], dcw_b[:CONV_K], dcw_c[:CONV_K]], axis=1),
        "conv_b": jnp.concatenate([dcb_x, dcb_b, dcb_c], axis=1),
        "dt_bias": d_dtb, "a_log": d_alog, "d_skip": d_dsk, "ssm_norm_w": d_normw,
    }
    return dx, dnw, g


def _lane_masks(shape):
    lane = lax.broadcasted_iota(jnp.int32, shape, len(shape) - 1)
    return lane < QK_NOPE, (lane >= QK_NOPE) & (lane < QK_DIM)


def _segment_matrix():
    p = np.zeros((HEAD_PAD, HEAD_PAD), np.float32)
    p[:QK_NOPE, :QK_NOPE] = 1.0
    p[QK_NOPE:QK_DIM, QK_NOPE:QK_DIM] = 1.0
    return jnp.asarray(p)


def _segment_rstd(x, seg):
    lane = lax.broadcasted_iota(jnp.int32, (1, x.shape[-1]), 1)
    inv_n = jnp.where(lane < QK_NOPE, 1.0 / QK_NOPE, 1.0 / QK_ROPE)
    xx = x * x
    hi = xx.astype(BF)
    lo = (xx - hi.astype(F32)).astype(BF)
    segb = seg.astype(BF)
    ss = _dotf(hi, segb, NN) + _dotf(lo, segb, NN)
    return lax.rsqrt(ss * inv_n + EPS)


def _rope_tables(positions_col):
    t = positions_col.shape[0]
    tm = _row_tile(t, 512)
    freq = np.zeros((1, HEAD_PAD), np.float32)
    inv = 1.0 / (ROPE_THETA ** (np.arange(0, QK_ROPE, 2, dtype=np.float32) / QK_ROPE))
    freq[0, QK_NOPE:QK_NOPE + QK_ROPE // 2] = inv
    freq[0, QK_NOPE + QK_ROPE // 2:QK_DIM] = inv
    sign = np.zeros((1, HEAD_PAD), np.float32)
    sign[0, QK_NOPE:QK_NOPE + QK_ROPE // 2] = -1.0
    sign[0, QK_NOPE + QK_ROPE // 2:QK_DIM] = 1.0

    def fn(pos, freq, sign):
        ang = pos.astype(F32) * freq
        nope, rope = _lane_masks(ang.shape)
        return jnp.where(nope, 1.0, jnp.where(rope, jnp.cos(ang), 0.0)), jnp.sin(ang) * sign

    row = lambda i: (i, 0)
    par = ((1, HEAD_PAD), lambda i: (0, 0))
    return _ew("mla_rope_tables", (t // tm,), fn,
               [(positions_col, (tm, 1), row), (jnp.asarray(freq),) + par, (jnp.asarray(sign),) + par],
               [(SDS((t, HEAD_PAD), F32), (tm, HEAD_PAD), row, False)] * 2)


def _swap_lanes(x):
    lane = lax.broadcasted_iota(jnp.int32, x.shape, x.ndim - 1)
    half = QK_ROPE // 2
    first = (lane >= QK_NOPE) & (lane < QK_NOPE + half)
    second = (lane >= QK_NOPE + half) & (lane < QK_DIM)
    n = x.shape[-1]
    return jnp.where(first, pltpu.roll(x, n - half, x.ndim - 1), jnp.where(second, pltpu.roll(x, half, x.ndim - 1), 0.0))


@jax.custom_vjp
def _swap_halves(x):
    return _swap_lanes(x)


_swap_halves.defvjp(lambda x: (_swap_lanes(x), None), lambda _, g: (_swap_lanes(g),))


def _rope(xn, cos, sin_signed):
    return xn * cos + _swap_halves(xn) * sin_signed


def _krope_fn(kr, w, cos, sin_signed, seg):
    _, rope = _lane_masks(kr.shape)
    xn = jnp.where(rope, kr * _segment_rstd(kr, seg) * w, 0.0)
    return _rope(xn, cos, sin_signed)


def _head_fn(q, kv, kr, cos, sin_signed, qn, kn, seg):
    nope, rope = _lane_masks(q.shape)
    qp = _rope(jnp.where(nope | rope, q * _segment_rstd(q, seg) * qn, 0.0), cos, sin_signed)
    kp = jnp.where(nope, kv * _segment_rstd(kv, seg) * kn, 0.0) + kr
    vp = jnp.where(nope, 0.0, kv)
    return qp, kp, vp


def _heads_fwd(q_raw, kv_raw, kr, cos, sin_signed, qn, kn, seg):
    nh, t, _ = q_raw.shape
    tm = _row_tile(t, 512)
    hblk = ((None, tm, HEAD_PAD), lambda i, h: (h, i, 0))
    tblk = ((tm, HEAD_PAD), lambda i, h: (i, 0))
    par = ((1, HEAD_PAD), lambda i, h: (0, 0))
    sw = ((HEAD_PAD, HEAD_PAD), lambda i, h: (0, 0))
    def fn(*tiles):
        qp, kp, vp = _head_fn(*tiles)
        return qp * Q_PRESCALE, kp, vp

    return _ew("mla_heads_fwd", (t // tm, nh), fn,
               [(q_raw,) + hblk, (kv_raw,) + hblk, (kr,) + tblk, (cos,) + tblk, (sin_signed,) + tblk,
                (qn,) + par, (kn,) + par, (seg,) + sw],
               [(SDS((nh, t, HEAD_PAD), BF),) + hblk + (False,)] * 3)


def _heads_bwd(q_raw, kv_raw, kr, cos, sin_signed, qn, kn, seg, dqp, dkp, dvp):
    nh, t, _ = q_raw.shape
    tm = _row_tile(t, 512)
    hblk = ((None, tm, HEAD_PAD), lambda i, h: (h, i, 0))
    tblk = ((tm, HEAD_PAD), lambda i, h: (i, 0))
    par = ((1, HEAD_PAD), lambda i, h: (0, 0))
    sw = ((HEAD_PAD, HEAD_PAD), lambda i, h: (0, 0))

    def body(q, kv, kr, cos, sn, qn, kn, seg, dqp, dkp, dvp, dq, dkv, dkr, dqn, dkn):
        f = lambda q, kv, kr, qn, kn: _head_fn(q, kv, kr, cos[...], sn[...], qn, kn, seg[...])
        _, vjp = jax.vjp(f, q[...], kv[...], kr[...], qn[...], kn[...])
        g = vjp((dqp[...].astype(F32), dkp[...].astype(F32), dvp[...].astype(F32)))
        dq[...] = g[0].astype(dq.dtype)
        dkv[...] = g[1].astype(dkv.dtype)
        h0 = pl.program_id(1) == 0
        first = h0 & (pl.program_id(0) == 0)
        for o, v, c in ((dkr, g[2], h0), (dqn, g[3], first), (dkn, g[4], first)):
            @pl.when(c)
            def _(o=o, v=v):
                o[...] = v

            @pl.when(jnp.logical_not(c))
            def _(o=o, v=v):
                o[...] += v

    spec = lambda b: pl.BlockSpec(*b)
    return pl.pallas_call(
        body, name="mla_heads_bwd", grid=(t // tm, nh),
        in_specs=[spec(hblk), spec(hblk), spec(tblk), spec(tblk), spec(tblk), spec(par), spec(par), spec(sw),
                  spec(hblk), spec(hblk), spec(hblk)],
        out_specs=[spec(hblk), spec(hblk), spec(tblk), spec(par), spec(par)],
        out_shape=[SDS((nh, t, HEAD_PAD), BF), SDS((nh, t, HEAD_PAD), BF), SDS((t, HEAD_PAD), F32),
                   SDS((1, HEAD_PAD), F32), SDS((1, HEAD_PAD), F32)],
        compiler_params=_cparams(2),
    )(q_raw, kv_raw, kr, cos, sin_signed, qn, kn, seg, dqp, dkp, dvp)


ATT_TILE = 512
ATT_SCALE = QK_DIM ** -0.5
LOG2E = 1.4426950408889634
LN2 = 0.6931471805599453
Q_PRESCALE = ATT_SCALE * LOG2E


def _flash_fwd(qs, k, v, nb, seq, hosts=None):
    nh, t, dh = qs.shape
    tq = _row_tile(seq, ATT_TILE)
    nq = seq // tq
    name = "mla_flash_fwd"
    comm = hosts.comm(name) if hosts is not None else None
    n_ci = len(comm.arrays) if comm is not None else 0
    n_co = len(comm.out_shapes) if comm is not None else 0

    def body(*refs):
        q_ref, k_ref, v_ref = refs[:3]
        ci_refs = refs[3:3 + n_ci]
        o_ref, lse_ref = refs[3 + n_ci:5 + n_ci]
        co_refs = refs[5 + n_ci:5 + n_ci + n_co]
        sem_refs = refs[5 + n_ci + n_co:]
        ids = (pl.program_id(0), pl.program_id(1), pl.program_id(2))
        if comm is not None:
            @pl.when((ids[0] == 0) & (ids[1] == 0) & (ids[2] == 0))
            def _():
                comm.start(ci_refs, co_refs, sem_refs)

        attend(q_ref, k_ref, v_ref, o_ref, lse_ref)

        if comm is not None:
            @pl.when((ids[0] == nh - 1) & (ids[1] == nb - 1) & (ids[2] == nq - 1))
            def _():
                comm.wait(ci_refs, co_refs, sem_refs)

    def attend(q_ref, k_ref, v_ref, o_ref, lse_ref):
        qi = pl.program_id(2)
        qt = q_ref[...]

        def tile(j, carry, diagonal):
            m, l, acc = carry
            rows = pl.ds(pl.multiple_of(j * tq, tq), tq)
            s = _dotf(qt, k_ref[rows, :], NT)
            if diagonal:
                r = lax.broadcasted_iota(jnp.int32, (tq, tq), 0)
                c = lax.broadcasted_iota(jnp.int32, (tq, tq), 1)
                s = jnp.where(c <= r, s, -jnp.inf)
            m_new = jnp.maximum(m, jnp.max(s, axis=-1, keepdims=True))
            alpha = jnp.exp2(m - m_new)
            p = jnp.exp2(s - m_new)
            return m_new, alpha * l + jnp.sum(p, axis=-1, keepdims=True), alpha * acc + _dotf(p, v_ref[rows, :], NN)

        init = (jnp.full((tq, 1), -jnp.inf, F32), jnp.zeros((tq, 1), F32), jnp.zeros((tq, dh), F32))
        carry = lax.fori_loop(0, qi, lambda j, c: tile(j, c, False), init)
        m, l, acc = tile(qi, carry, True)
        o_ref[...] = (acc / l).astype(o_ref.dtype)
        lse_ref[...] = m + jnp.log2(l)

    qblk = pl.BlockSpec((None, tq, dh), lambda h, b, i: (h, b * nq + i, 0))
    kblk = pl.BlockSpec((None, seq, dh), lambda h, b, i: (h, b, 0))
    hbm = pl.BlockSpec(memory_space=pl.ANY)
    res = pl.pallas_call(
        body, name=name, grid=(nh, nb, nq), in_specs=[qblk, kblk, kblk] + [hbm] * n_ci,
        out_specs=[qblk, pl.BlockSpec((None, tq, 1), lambda h, b, i: (h, b * nq + i, 0))] + [hbm] * n_co,
        out_shape=[SDS((nh, t, dh), BF), SDS((nh, t, 1), F32)] + (list(comm.out_shapes) if comm is not None else []),
        scratch_shapes=comm.sems if comm is not None else [],
        compiler_params=_cparams(3, VMEM_BIG),
    )(qs, k, v, *(comm.arrays if comm is not None else []))
    if comm is not None:
        hosts.done(name, res[2:])
    return res[0], res[1]


def _flash_bwd(qs, k, v, o, lse, do, nb, seq):
    nh, t, dh = qs.shape
    tq = _row_tile(seq, ATT_TILE)
    nq = seq // tq

    def row_of(col):
        return jnp.broadcast_to(col, (tq, LANES)).T[0:1, :]

    def body(q_ref, k_ref, v_ref, o_ref, lse_ref, do_ref, dq_ref, dk_ref, dv_ref, kt_sc, lrow_sc, drow_sc, dqt_sc):
        for c in range(nq):
            rows = pl.ds(c * tq, tq)
            kt_sc[c] = k_ref[rows, :].T
            delta = jnp.sum(do_ref[rows, :].astype(F32) * o_ref[rows, :].astype(F32), axis=-1, keepdims=True)
            drow_sc[c] = row_of(delta)
            lrow_sc[c] = row_of(lse_ref[rows, :])
        dqt_sc[...] = jnp.zeros_like(dqt_sc)

        def kv_step(j, _):
            rows_j = pl.ds(pl.multiple_of(j * tq, tq), tq)
            ks, vs, kt = k_ref[rows_j, :], v_ref[rows_j, :], kt_sc[j]

            def q_tile(i, carry, diagonal):
                dk, dv = carry
                rows_i = pl.ds(pl.multiple_of(i * tq, tq), tq)
                qt, dot_ = q_ref[rows_i, :], do_ref[rows_i, :]
                pt = jnp.exp2(_dotf(ks, qt, NT) - lrow_sc[i])
                if diagonal:
                    kk = lax.broadcasted_iota(jnp.int32, (tq, tq), 0)
                    qq = lax.broadcasted_iota(jnp.int32, (tq, tq), 1)
                    pt = jnp.where(kk <= qq, pt, 0.0)
                dst = (pt * (_dotf(vs, dot_, NT) - drow_sc[i])).astype(BF)
                dqt_sc[i] += _dotf(kt, dst, NN)
                return dk + _dotf(dst, qt, NN), dv + _dotf(pt, dot_, NN)

            zero = jnp.zeros((tq, dh), F32)
            carry = q_tile(j, (zero, zero), True)
            dk, dv = lax.fori_loop(j + 1, nq, lambda i, c: q_tile(i, c, False), carry)
            dk_ref[rows_j, :] = dk * LN2
            dv_ref[rows_j, :] = dv
            return 0

        lax.fori_loop(0, nq, kv_step, 0)
        for c in range(nq):
            dq_ref[pl.ds(c * tq, tq), :] = dqt_sc[c].T * ATT_SCALE

    full = pl.BlockSpec((None, seq, dh), lambda h, b: (h, b, 0))
    sfull = pl.BlockSpec((None, seq, 1), lambda h, b: (h, b, 0))
    return pl.pallas_call(
        body, name="mla_flash_bwd", grid=(nh, nb), in_specs=[full, full, full, full, sfull, full],
        out_specs=[full, full, full], out_shape=[SDS((nh, t, dh), F32)] * 3,
        scratch_shapes=[pltpu.VMEM((nq, dh, tq), BF), pltpu.VMEM((nq, 1, tq), F32), pltpu.VMEM((nq, 1, tq), F32),
                        pltpu.VMEM((nq, dh, tq), F32)],
        compiler_params=_cparams(2, VMEM_BIG),
    )(qs, k, v, o, lse, do)


def _heads_nt(name, a, wt, out_dtype):
    t, kdim = a.shape
    tm = _row_tile(t, 512)
    nw = MLA_HEADS * HEAD_PAD

    def body(a_ref, w_ref, o_ref):
        r = _dotf(a_ref[...], w_ref[...], NT)
        for h in range(MLA_HEADS):
            o_ref[h] = r[:, HEAD_PAD * h:HEAD_PAD * (h + 1)].astype(o_ref.dtype)

    return pl.pallas_call(
        body, name=name, grid=(t // tm,),
        in_specs=[pl.BlockSpec((tm, kdim), lambda i: (i, 0)), pl.BlockSpec((nw, kdim), lambda i: (0, 0))],
        out_specs=pl.BlockSpec((MLA_HEADS, tm, HEAD_PAD), lambda i: (0, i, 0)),
        out_shape=SDS((MLA_HEADS, t, HEAD_PAD), out_dtype), compiler_params=_cparams(1, VMEM_BIG),
    )(a, wt)


def _all_heads(a_ref):
    return jnp.concatenate([a_ref[h] for h in range(MLA_HEADS)], axis=1)


def _heads_nn(name, a, w, n, out_dtype, res=None):
    t = a.shape[1]
    tm = _row_tile(t, 512)
    nw = MLA_HEADS * HEAD_PAD

    def body(*refs):
        a_ref, w_ref, o_ref = refs[0], refs[1], refs[-1]
        r = _dotf(_all_heads(a_ref), w_ref[...], NN)
        if res is not None:
            r = r + refs[2][...]
        o_ref[...] = r.astype(o_ref.dtype)

    row = pl.BlockSpec((tm, n), lambda i: (i, 0))
    in_specs = [pl.BlockSpec((MLA_HEADS, tm, HEAD_PAD), lambda i: (0, i, 0)), pl.BlockSpec((nw, n), lambda i: (0, 0))]
    args = [a, w]
    if res is not None:
        in_specs.append(row)
        args.append(res)
    return pl.pallas_call(body, name=name, grid=(t // tm,), in_specs=in_specs, out_specs=row,
                          out_shape=SDS((t, n), out_dtype), compiler_params=_cparams(1, VMEM_BIG))(*args)


def _heads_wgrad(name, a, b, n):
    t = b.shape[0]
    tk = _row_tile(t, 512)
    nw = MLA_HEADS * HEAD_PAD
    steps = t // tk

    def body(a_ref, b_ref, o_ref, acc):
        k = pl.program_id(0)

        @pl.when(k == 0)
        def _():
            acc[...] = jnp.zeros_like(acc)

        acc[...] += _dotf(_all_heads(a_ref), b_ref[...], TN)

        @pl.when(k == steps - 1)
        def _():
            o_ref[...] = acc[...]

    return pl.pallas_call(
        body, name=name, grid=(steps,),
        in_specs=[pl.BlockSpec((MLA_HEADS, tk, HEAD_PAD), lambda k: (0, k, 0)), pl.BlockSpec((tk, n), lambda k: (k, 0))],
        out_specs=pl.BlockSpec((nw, n), lambda k: (0, 0)), out_shape=SDS((nw, n), F32),
        scratch_shapes=[pltpu.VMEM((nw, n), F32)], compiler_params=_cparams(1, VMEM_BIG),
    )(a, b)


PM_CKV, PM_KR, PM_CQ = 0, KV_LORA, KV_LORA + HEAD_PAD
PM_DIM = KV_LORA + HEAD_PAD + Q_LORA


def _lat_specs(t, tm):
    return (((tm, KV_LORA), lambda i: (i, 0)), ((tm, HEAD_PAD), lambda i: (i, PM_KR // HEAD_PAD)),
            ((tm, Q_LORA), lambda i: (i, PM_CQ // Q_LORA)))


def _mla_fwd(x, nw, wm, small, tables, nb, seq, hosts=None):
    t, d = x.shape
    cos, sin_signed, seg = tables
    h = _rms_fwd("mla_rms", x, nw)
    pm = _proj_nt("mla_in", h, wm["in_t"], PM_DIM, F32, tn=PM_DIM // 3)
    tm = _row_tile(t, 512)
    ckv_s, kr_s, cq_s = _lat_specs(t, tm)
    row = lambda i: (i, 0)
    par = lambda n: ((1, n), lambda i: (0, 0))
    ckvn = _ew("mla_ckv_norm", (t // tm,), _rms_fn, [(pm,) + ckv_s, (small["kv_a_norm"],) + par(KV_LORA)],
               [(SDS((t, KV_LORA), BF), (tm, KV_LORA), row, False)])[0]
    cqn = _ew("mla_cq_norm", (t // tm,), _rms_fn, [(pm,) + cq_s, (small["q_a_norm"],) + par(Q_LORA)],
              [(SDS((t, Q_LORA), BF), (tm, Q_LORA), row, False)])[0]
    tb = ((tm, HEAD_PAD), row)
    kr = _ew("mla_krope", (t // tm,), _krope_fn,
             [(pm,) + kr_s, (small["k_norm"],) + par(HEAD_PAD), (cos,) + tb, (sin_signed,) + tb,
              (seg, (HEAD_PAD, HEAD_PAD), lambda i: (0, 0))],
             [(SDS((t, HEAD_PAD), F32),) + tb + (False,)])[0]
    q_raw = _heads_nt("mla_q_b", cqn, wm["qb_t"], F32)
    kv_raw = _heads_nt("mla_kv_b", ckvn, wm["kvb_t"], F32)
    qp, kp, vp = _heads_fwd(q_raw, kv_raw, kr, cos, sin_signed, small["q_norm"], small["k_norm"], seg)
    o, lse = _flash_fwd(qp, kp, vp, nb, seq, hosts=hosts)
    out = _heads_nn("mla_out", o, wm["out"], d, F32, res=x)
    return out, (x, h, pm, ckvn, cqn, kr, q_raw, kv_raw, qp, kp, vp, o, lse)


def _mla_bwd(dy, nw, wm, small, tables, saved, nb, seq):
    x, h, pm, ckvn, cqn, kr, q_raw, kv_raw, qp, kp, vp, o, lse = saved
    t, d = x.shape
    cos, sin_signed, seg = tables
    do = _heads_nt("mla_bwd_do", dy, wm["out"], BF)
    g_out = _heads_wgrad("mla_w_out_g", o, dy, d)
    dqp, dkp, dvp = _flash_bwd(qp, kp, vp, o, lse, do, nb, seq)
    dq_raw, dkv_raw, dkr, d_qn, d_kn = _heads_bwd(q_raw, kv_raw, kr, cos, sin_signed, small["q_norm"],
                                                   small["k_norm"], seg, dqp, dkp, dvp)
    dcqn = _heads_nn("mla_bwd_dcq", dq_raw, wm["qb_t"], Q_LORA, F32)
    dckvn = _heads_nn("mla_bwd_dckv", dkv_raw, wm["kvb_t"], KV_LORA, F32)
    g_qb = _heads_wgrad("mla_w_qb_g", dq_raw, cqn, Q_LORA)
    g_kvb = _heads_wgrad("mla_w_kvb_g", dkv_raw, ckvn, KV_LORA)
    tm = _row_tile(t, 512)
    ckv_s, kr_s, cq_s = _lat_specs(t, tm)
    row = lambda i: (i, 0)
    par = lambda n: ((1, n), lambda i: (0, 0))

    def rms_b(xv, w, dv):
        _, vjp = jax.vjp(_rms_fn, xv, w)
        return vjp(dv)

    dckv, d_kva = _ew("mla_ckv_norm_bwd", (t // tm,), rms_b,
                      [(pm,) + ckv_s, (small["kv_a_norm"],) + par(KV_LORA), (dckvn, (tm, KV_LORA), row)],
                      [(SDS((t, KV_LORA), BF), (tm, KV_LORA), row, False),
                       (SDS((1, KV_LORA), F32),) + par(KV_LORA) + (True,)], acc_axes=(0,))
    dcq, d_qa = _ew("mla_cq_norm_bwd", (t // tm,), rms_b,
                    [(pm,) + cq_s, (small["q_a_norm"],) + par(Q_LORA), (dcqn, (tm, Q_LORA), row)],
                    [(SDS((t, Q_LORA), BF), (tm, Q_LORA), row, False),
                     (SDS((1, Q_LORA), F32),) + par(Q_LORA) + (True,)], acc_axes=(0,))
    tb = ((tm, HEAD_PAD), row)

    def kr_b(krv, w, cosv, sinv, sw, dv):
        _, vjp = jax.vjp(lambda a, b: _krope_fn(a, b, cosv, sinv, sw), krv, w)
        return vjp(dv)

    dkr_raw, d_kn2 = _ew("mla_krope_bwd", (t // tm,), kr_b,
                         [(pm,) + kr_s, (small["k_norm"],) + par(HEAD_PAD), (cos,) + tb, (sin_signed,) + tb,
                          (seg, (HEAD_PAD, HEAD_PAD), lambda i: (0, 0)), (dkr,) + tb],
                         [(SDS((t, HEAD_PAD), BF),) + tb + (False,),
                          (SDS((1, HEAD_PAD), F32),) + par(HEAD_PAD) + (True,)], acc_axes=(0,))
    dh = _seg_nn("mla_bwd_dh", [(dckv, wm["in_t"], PM_CKV), (dkr_raw, wm["in_t"], PM_KR),
                                (dcq, wm["in_t"], PM_CQ)], d, BF, tk=128)
    dx, dnw = _rms_bwd("mla_bwd_rms", x, nw, dh, dy)
    g = {
        "in_ckv_t": _wgrad("mla_w_in_ckv_g", dckv, h, KV_LORA, d, tm=KV_LORA, tn=d, out_dtype=F32),
        "in_kr_t": _wgrad("mla_w_in_kr_g", dkr_raw, h, HEAD_PAD, d, tm=HEAD_PAD, tn=d, out_dtype=F32),
        "in_cq_t": _wgrad("mla_w_in_cq_g", dcq, h, Q_LORA, d, tm=Q_LORA, tn=d, out_dtype=F32),
        "qb_t": g_qb, "kvb_t": g_kvb, "out": g_out,
        "q_a_norm": d_qa, "kv_a_norm": d_kva, "q_norm": d_qn, "k_norm": d_kn + d_kn2,
    }
    return dx, dnw, g


def _mesh_pos():
    return lax.axis_index("x"), lax.axis_index("y"), lax.axis_index("c")


def _peer(pos, k):
    x, y, c = pos
    return (x ^ ((k >> 2) & 1), y ^ ((k >> 1) & 1), c ^ (k & 1))


def _flat(pos):
    return 4 * pos[0] + 2 * pos[1] + pos[2]


def _slab(ref, axis, start, size):
    idx = [slice(None)] * axis + [pl.ds(start, size)]
    return ref.at[tuple(idx)]


class _Exchange:
    def __init__(self, kind, items):
        self.kind = kind
        self.axes = [ax for _, ax in items]
        self.arrays = [a for a, _ in items]
        n = len(items)
        self.out_shapes = []
        self.sizes = []
        for a, ax in items:
            shp = list(a.shape)
            if kind == "gather":
                self.sizes.append(shp[ax])
                shp[ax] *= N_DEV
                self.out_shapes.append(SDS(tuple(shp), a.dtype))
            else:
                shp[ax] //= N_DEV
                self.sizes.append(shp[ax])
                self.out_shapes.append(SDS((N_DEV,) + tuple(shp), a.dtype))
        self.sems = [pltpu.SemaphoreType.DMA((n, N_DEV - 1)), pltpu.SemaphoreType.DMA((n, N_DEV - 1)),
                     pltpu.SemaphoreType.DMA((n,))]

    def _copies(self, srcs, dsts, sems, with_arrivals=True):
        send_sems, recv_sems, local_sems = sems
        pos = _mesh_pos()
        me = _flat(pos)
        local, sends, recvs = [], [], []
        for t, (src, dst) in enumerate(zip(srcs, dsts)):
            ax, sz = self.axes[t], self.sizes[t]
            if self.kind == "gather":
                mine = _slab(dst, ax, me * sz, sz)
                local.append(pltpu.make_async_copy(src, mine, local_sems.at[t]))
            else:
                mine = dst.at[me]
                local.append(pltpu.make_async_copy(_slab(src, ax, me * sz, sz), mine, local_sems.at[t]))
            for k in range(1, N_DEV):
                peer = _peer(pos, k)
                there = _flat(peer)
                if self.kind == "gather":
                    out_src, landing = src, _slab(dst, ax, there * sz, sz)
                else:
                    out_src, landing = _slab(src, ax, there * sz, sz), dst.at[there]
                common = dict(send_sem=send_sems.at[t, k - 1], recv_sem=recv_sems.at[t, k - 1], device_id=peer,
                              device_id_type=pl.DeviceIdType.MESH)
                sends.append(pltpu.make_async_remote_copy(src_ref=out_src, dst_ref=mine, **common))
                if with_arrivals:
                    recvs.append(pltpu.make_async_remote_copy(src_ref=out_src, dst_ref=landing, **common))
        return local, sends, recvs

    def start(self, srcs, dsts, sems):
        local, sends, _ = self._copies(srcs, dsts, sems, with_arrivals=False)
        for cp in local + sends:
            cp.start()

    def wait(self, srcs, dsts, sems):
        local, sends, recvs = self._copies(srcs, dsts, sems)
        for rc in recvs:
            rc.wait_recv()
        for rc in sends:
            rc.wait_send()
        for cp in local:
            cp.wait()

    def run(self, name):
        n = len(self.arrays)

        def body(*refs):
            srcs, dsts, sems = refs[:n], refs[n:2 * n], refs[2 * n:]
            self.start(srcs, dsts, sems)
            self.wait(srcs, dsts, sems)

        hbm = pl.BlockSpec(memory_space=pl.ANY)
        return pl.pallas_call(body, name=name, in_specs=[hbm] * n, out_specs=[hbm] * n, out_shape=self.out_shapes,
                              scratch_shapes=self.sems)(*self.arrays)


def _adam_math(w, g, m, v):
    m = ADAM_B1 * m + (1.0 - ADAM_B1) * g
    v = ADAM_B2 * v + (1.0 - ADAM_B2) * (g * g)
    m_hat = m / (1.0 - ADAM_B1 ** ADAM_STEP)
    v_hat = v / (1.0 - ADAM_B2 ** ADAM_STEP)
    delta = -ADAM_LR * (m_hat / (jnp.sqrt(v_hat) + ADAM_EPS) + ADAM_WD * w)
    return delta, m, v


def _adam(name, land, land_blk, land_idx, w, m, v, transposed, ck):
    n, r, c = w.shape
    wblk = ((None, ck, c), lambda a, i: (a, i, 0))

    def fn(parts, w, m, v):
        g = parts[0].astype(F32)
        for s in range(1, N_DEV):
            g = g + parts[s].astype(F32)
        if transposed:
            g = g.T
        delta, m2, v2 = _adam_math(w, g, m, v)
        return g, delta, m2, v2

    return _ew(name, (n, r // ck), fn,
               [(land, land_blk, land_idx), (w,) + wblk, (m,) + wblk, (v,) + wblk],
               [(SDS(w.shape, F32),) + wblk + (False,)] * 4, vmem=VMEM_BIG)


def _prep_ffn(gate, up, down):
    def body(g, u, dn, o):
        o[0] = g[...].T.astype(BF)
        o[1] = u[...].T.astype(BF)
        o[2] = dn[...].astype(BF)

    cblk = pl.BlockSpec((None, None, D_MODEL, FF_SHARD), lambda l, i: (l, i, 0, 0))
    rblk = pl.BlockSpec((None, None, FF_SHARD, D_MODEL), lambda l, i: (l, i, 0, 0))
    return pl.pallas_call(
        body, name="prep_ffn", grid=(2, 2), in_specs=[cblk, cblk, rblk],
        out_specs=pl.BlockSpec((3, FF_SHARD, D_MODEL), lambda l, i: (2 * l + i, 0, 0)),
        out_shape=SDS((12, FF_SHARD, D_MODEL), BF), compiler_params=_cparams(2, VMEM_BIG),
    )(gate, up, down)


def _transpose_cast(name, w, dtype):
    def body(a, o):
        o[...] = a[...].T.astype(dtype)

    r, c = w.shape
    return pl.pallas_call(body, name=name, out_shape=SDS((c, r), dtype),
                          compiler_params=pltpu.CompilerParams(vmem_limit_bytes=VMEM_BIG))(w)


SMALL_SHARDED = (("norm_w", 6 * 128), ("conv_w", CONV_K * 512), ("q_a_norm", 48), ("kv_a_norm", 32))
SMALL_PACK = 3072


def _dyn(a, start, size):
    return lax.dynamic_slice_in_dim(a, start, size, axis=a.ndim - 1)


def _layout_ssm(ssm_in_t, ssm_out_all):
    d = ssm_in_t.shape[1]
    dt_rows = ssm_in_t[D_INNER + CONV_DIM:].reshape(SSM_GROUPS, SSM_HPG, d)
    return {"z_t": ssm_in_t[:D_INNER], "xbc_t": ssm_in_t[D_INNER:D_INNER + CONV_DIM],
            "dt_t": jnp.pad(dt_rows, ((0, 0), (0, LANES - SSM_HPG), (0, 0))).reshape(SSM_GROUPS * LANES, d),
            "out": ssm_out_all}


def _layout_mla(mla_in_all, qb_all, kvb_all, mla_out_all):
    d = mla_out_all.shape[1]
    in_t = mla_in_all.T
    kr_rows = jnp.pad(in_t[Q_LORA + KV_LORA:], ((QK_NOPE, HEAD_PAD - QK_DIM), (0, 0)))
    qb_heads = jnp.pad(qb_all.reshape(MLA_HEADS, QK_DIM, Q_LORA), ((0, 0), (0, HEAD_PAD - QK_DIM), (0, 0)))
    out_heads = jnp.pad(mla_out_all.reshape(MLA_HEADS, 64, d), ((0, 0), (64, 0), (0, 0)))
    return {"in_t": jnp.concatenate([in_t[Q_LORA:Q_LORA + KV_LORA], kr_rows, in_t[:Q_LORA]], axis=0),
            "qb_t": qb_heads.reshape(MLA_HEADS * HEAD_PAD, Q_LORA), "kvb_t": kvb_all,
            "out": out_heads.reshape(MLA_HEADS * HEAD_PAD, d)}


def _layout_small(conv_w, conv_b, dt_bias, a_log, d_skip, ssm_norm_w, q_a_norm, kv_a_norm, q_norm, k_norm):
    lane_heads = lambda p: jnp.pad(p.reshape(SSM_GROUPS, SSM_HPG), ((0, 0), (0, LANES - SSM_HPG))).reshape(1, -1)
    pad_head = lambda p: jnp.pad(p.reshape(1, QK_DIM), ((0, 0), (0, HEAD_PAD - QK_DIM)))
    return {"conv_w": conv_w, "conv_b": conv_b, "dt_bias": lane_heads(dt_bias), "a_log": lane_heads(a_log),
            "d_skip": lane_heads(d_skip), "ssm_norm_w": ssm_norm_w, "q_a_norm": q_a_norm, "kv_a_norm": kv_a_norm,
            "q_norm": pad_head(q_norm), "k_norm": pad_head(k_norm)}


class _Plan:
    def __init__(self, ctx):
        self.ctx = ctx
        self.make = {}
        self.land = {}

    def ride(self, host, make, land):
        assert host not in self.make, host
        self.make[host] = make
        self.land[host] = land

    def comm(self, host):
        return self.make[host](self.ctx) if host in self.make else None

    def done(self, host, results):
        self.land[host](results, self.ctx)


def _local_step(x, positions, loss_target, ctx, plan=None):
    nb, seq, d = x.shape
    t = nb * seq
    xf = x.reshape(t, d)
    norm = ctx["norm"]
    tables = list(_rope_tables(positions.reshape(t, 1))) + [_segment_matrix()]
    x1, s_f0 = _ffn_fwd("ffn0", xf, norm[0, 0], ctx["ffn0"], plan)
    x2, s_ssm = _ssm_fwd(x1, norm[0, 1], ctx["ws"], ctx["small"], nb, seq, plan)
    x3, s_f1 = _ffn_fwd("ffn1", x2, norm[0, 2], ctx["ffn1"], plan)
    x4, s_f2 = _ffn_fwd("ffn2", x3, norm[1, 0], ctx["ffn2"], plan)
    x5, s_mla = _mla_fwd(x4, norm[1, 1], ctx["wm"], ctx["small"], tables, nb, seq, plan)
    x6, s_f3 = _ffn_fwd("ffn3", x5, norm[1, 2], ctx["ffn3"], plan)
    dy, loss_cols = _loss_and_grad(x6, loss_target.reshape(t, d))

    dx5, dn12, ctx["g_ffn3"] = _ffn_bwd("ffn3", dy, norm[1, 2], ctx["ffn3"], s_f3, plan)
    dx4, dn11, ctx["g_mla"] = _mla_bwd(dx5, norm[1, 1], ctx["wm"], ctx["small"], tables, s_mla, nb, seq)
    dx3, dn10, ctx["g_ffn2"] = _ffn_bwd("ffn2", dx4, norm[1, 0], ctx["ffn2"], s_f2, plan)
    dx2, dn02, ctx["g_ffn1"] = _ffn_bwd("ffn1", dx3, norm[0, 2], ctx["ffn1"], s_f1, plan)
    dx1, dn01, ctx["g_ssm"] = _ssm_bwd(dx2, norm[0, 1], ctx["ws"], ctx["small"], s_ssm, nb, seq, plan)
    dx0, dn00, ctx["g_ffn0"] = _ffn_bwd("ffn0", dx1, norm[0, 0], ctx["ffn0"], s_f0, plan)
    return loss_cols, dx0.reshape(nb, seq, d), (dn00, dn01, dn02, dn10, dn11, dn12)


def kernel(x, positions, norm_w, ffn_w_gate, ffn_w_up, ffn_w_down, ssm_w_in, ssm_conv_w, ssm_conv_b, ssm_dt_bias, ssm_a_log, ssm_d, ssm_norm_w, ssm_w_out, mla_w_in, mla_q_a_norm, mla_kv_a_norm, mla_w_q_b, mla_w_kv_b, mla_q_norm, mla_k_norm, mla_w_out, loss_target, m_norm_w, m_ffn_w_gate, m_ffn_w_up, m_ffn_w_down, m_ssm_w_in, m_ssm_conv_w, m_ssm_conv_b, m_ssm_dt_bias, m_ssm_a_log, m_ssm_d, m_ssm_norm_w, m_ssm_w_out, m_mla_w_in, m_mla_q_a_norm, m_mla_kv_a_norm, m_mla_w_q_b, m_mla_w_kv_b, m_mla_q_norm, m_mla_k_norm, m_mla_w_out, v_norm_w, v_ffn_w_gate, v_ffn_w_up, v_ffn_w_down, v_ssm_w_in, v_ssm_conv_w, v_ssm_conv_b, v_ssm_dt_bias, v_ssm_a_log, v_ssm_d, v_ssm_norm_w, v_ssm_w_out, v_mla_w_in, v_mla_q_a_norm, v_mla_kv_a_norm, v_mla_w_q_b, v_mla_w_kv_b, v_mla_q_norm, v_mla_k_norm, v_mla_w_out):
    nb, seq, d = x.shape
    t = nb * seq
    me = _flat(_mesh_pos())

    ffn_loc = _prep_ffn(ffn_w_gate, ffn_w_up, ffn_w_down)
    ssm_in_loc = _transpose_cast("prep_ssm_in", ssm_w_in[0], BF)
    ssm_out_loc = ssm_w_out[0].astype(BF)
    mla_in_loc, mla_out_loc = mla_w_in[0].astype(BF), mla_w_out[0].astype(BF)
    qb_loc = _transpose_cast("prep_q_b", mla_w_q_b[0], BF)
    kvb_loc = _transpose_cast("prep_kv_b", mla_w_kv_b[0], BF)
    small_loc = jnp.concatenate([norm_w.reshape(-1), ssm_conv_w.reshape(-1), mla_q_a_norm.reshape(-1),
                                 mla_kv_a_norm.reshape(-1)])
    small_loc = jnp.pad(small_loc, (0, SMALL_PACK - small_loc.shape[0])).reshape(SMALL_PACK // LANES, LANES)

    wloc = lambda n: [(ffn_loc[3 * n + k], 0) for k in range(3)]
    g0, u0, small_all = _Exchange("gather", wloc(0)[0:2] + [(small_loc, 0)]).run("gather_first")
    sm = small_all.reshape(N_DEV, SMALL_PACK)
    conv_w_full = sm[:, 768:768 + 2048].reshape(N_DEV, CONV_K, 512).transpose(1, 0, 2).reshape(CONV_K, CONV_DIM)
    ctx = {"ffn0": [g0, u0, None], "ffn1": [None] * 3, "ffn2": [None] * 3,
           "norm": sm[:, :768].reshape(N_DEV, 6, 128).transpose(1, 0, 2).reshape(2, 3, 1, d),
           "small": _layout_small(conv_w_full, ssm_conv_b, ssm_dt_bias, ssm_a_log, ssm_d, ssm_norm_w,
                                  sm[:, 2816:2864].reshape(1, Q_LORA), sm[:, 2864:2896].reshape(1, KV_LORA),
                                  mla_q_norm, mla_k_norm)}
    plan = _Plan(ctx)

    def gather_on(host, items, land):
        plan.ride(host, lambda c: _Exchange("gather", items), land)

    def put_w(key, ks):
        def land(r, c):
            for k, arr in zip(ks, r):
                c[key][k] = arr
        return land

    half_rows = SSM_IN_SHARD // 2
    ssm_in_a, ssm_in_b = (ssm_in_loc[k * half_rows:(k + 1) * half_rows].reshape(half_rows // 2, 16, LANES)
                          for k in range(2))

    def land_first(r, c):
        c["ffn0"][2] = r[0]
        c["ssm_in_a"] = r[1].reshape(N_DEV, half_rows, d)

    def land_ssm_in(r, c):
        both = jnp.concatenate([c["ssm_in_a"], r[0].reshape(N_DEV, half_rows, d)], axis=1)
        c["ws"] = _layout_ssm(both.reshape(SSM_IN_DIM, d), None)

    def land_ssm_out(r, c):
        c["ffn1"][1] = r[0]
        c["ws"]["out"] = r[1]

    gather_on("ffn0_up", wloc(0)[2:3] + [(ssm_in_a, 0)], land_first)
    gather_on("ffn0_down", [(ssm_in_b, 0)], land_ssm_in)
    gather_on("ssm_in_z", wloc(1)[0:1], put_w("ffn1", (0,)))
    gather_on("ssm_in_xbc", wloc(1)[1:2] + [(ssm_out_loc, 0)], land_ssm_out)
    gather_on("ssm_out", wloc(1)[2:3], put_w("ffn1", (2,)))
    gather_on("ffn1_up", wloc(2)[0:2], put_w("ffn2", (0, 1)))
    gather_on("ffn1_down", wloc(2)[2:3], put_w("ffn2", (2,)))
    gather_on("ffn2_up", [(mla_in_loc, 0), (qb_loc, 0), (kvb_loc, 0), (mla_out_loc, 0)],
              lambda r, c: c.update(wm=_layout_mla(r[0], r[1], r[2], r[3])))
    gather_on("mla_flash_fwd", wloc(3), lambda r, c: c.update(ffn3=tuple(r)))

    heads_of = lambda a: a.reshape(SSM_GROUPS, LANES, -1)[:, :SSM_HPG].reshape(SSM_HEADS, -1)

    def mla_grad_items(c):
        g = c["g_mla"]
        g_in = jnp.concatenate([g["in_cq_t"], g["in_ckv_t"], g["in_kr_t"][QK_NOPE:QK_DIM]], axis=0).T
        g_qb = g["qb_t"].reshape(MLA_HEADS, HEAD_PAD, Q_LORA)[:, :QK_DIM].reshape(MLA_HEADS * QK_DIM, Q_LORA)
        g_out = g["out"].reshape(MLA_HEADS, HEAD_PAD, d)[:, 64:].reshape(MLA_HEADS * 64, d)
        return [(a.astype(BF), 0) for a in (g_out, g_in, g_qb, g["kvb_t"])]

    def ssm_in_grad(which):
        def items(c):
            if "g_ssm_in_t" not in c:
                g = c["g_ssm"]
                g_in_t = jnp.concatenate([g["z_t"], g["x_t"], g["b_t"], g["c_t"], heads_of(g["dt_t"])], axis=0)
                c["g_ssm_in_t"] = g_in_t.astype(BF).reshape(N_DEV, SSM_IN_SHARD, d)
            part = c["g_ssm_in_t"][:, which * half_rows:(which + 1) * half_rows]
            return [(part.reshape(N_DEV * half_rows // 2, 16, LANES), 0)]
        return items

    def scatter_on(host, items_of, keys):
        plan.ride(host, lambda c: _Exchange("scatter", items_of(c)),
                  lambda r, c: c.update(dict(zip(keys, r))))

    of = lambda key, k: (lambda c: [(c[key][k], 0)])
    scatter_on("ffn2_bwd_act", of("g_ffn3", 0), ("l3_gate",))
    scatter_on("ffn2_bwd_dh", of("g_ffn3", 1), ("l3_up",))
    scatter_on("ffn2_wg", of("g_ffn3", 2), ("l3_down",))
    scatter_on("ffn2_wu", mla_grad_items, ("l_mla_out", "l_mla_in", "l_qb", "l_kvb"))
    scatter_on("ffn1_bwd_act", of("g_ffn2", 0), ("l2_gate",))
    scatter_on("ffn1_bwd_dh", of("g_ffn2", 1), ("l2_up",))
    scatter_on("ffn1_wg", of("g_ffn2", 2), ("l2_down",))
    scatter_on("ssm_bwd_dgn", of("g_ffn1", 0), ("l1_gate",))
    scatter_on("ssm_bwd_dh", lambda c: [(c["g_ffn1"][1], 0), (c["g_ffn1"][2], 0)], ("l1_up", "l1_down"))
    scatter_on("ffn0_bwd_act", ssm_in_grad(0), ("l_ssm_in_a",))
    scatter_on("ffn0_bwd_dh", ssm_in_grad(1), ("l_ssm_in_b",))
    scatter_on("ffn0_wg", lambda c: [(c["g_ssm"]["out"], 0)], ("l_ssm_out",))
    scatter_on("ffn0_wu", lambda c: [(c["g_ffn0_gate"], 0)], ("l0_gate",))
    scatter_on("ffn0_wd", lambda c: [(c["g_ffn0_up"], 0)], ("l0_up",))

    loss_cols, grad_x, dns = _local_step(x, positions, loss_target, ctx, plan)
    loss = lax.psum(jnp.sum(loss_cols), ("x", "y", "c"))
    dn00, dn01, dn02, dn10, dn11, dn12 = dns
    g_ssm, g_mla = ctx["g_ssm"], ctx["g_mla"]
    ctx["l0_down"] = _Exchange("scatter", [(ctx["g_ffn0"][2], 0)]).run("scatter_last")[0]
    l_ffn = {k: jnp.stack([ctx["l%d_%s" % (n, k)] for n in range(4)], axis=1) for k in ("gate", "up", "down")}
    l_mla_out, l_mla_in, l_qb, l_kvb = (ctx[k] for k in ("l_mla_out", "l_mla_in", "l_qb", "l_kvb"))
    l_ssm_in = jnp.concatenate([ctx[k].reshape(N_DEV, half_rows, d) for k in ("l_ssm_in_a", "l_ssm_in_b")], axis=1)
    l_ssm_out = ctx["l_ssm_out"]

    unlane = lambda a: a.reshape(SSM_GROUPS, LANES)[:, :SSM_HPG].reshape(1, SSM_HEADS)
    small_g = jnp.concatenate([
        jnp.concatenate([dn00, dn01, dn02, dn10, dn11, dn12], axis=0).reshape(-1),
        g_ssm["conv_w"].reshape(-1), g_ssm["conv_b"].reshape(-1), unlane(g_ssm["dt_bias"]).reshape(-1),
        unlane(g_ssm["a_log"]).reshape(-1), unlane(g_ssm["d_skip"]).reshape(-1), g_ssm["ssm_norm_w"].reshape(-1),
        g_mla["q_a_norm"].reshape(-1), g_mla["kv_a_norm"].reshape(-1), g_mla["q_norm"][0, :QK_DIM],
        g_mla["k_norm"][0, :QK_DIM]])
    n_small = small_g.shape[0]
    n_small_pad = -(-n_small // (8 * LANES)) * (8 * LANES)
    small_g = jnp.pad(small_g, (0, n_small_pad - n_small)).reshape(n_small_pad // LANES, LANES)
    gs = _Exchange("gather", [(small_g, 0)]).run("gather_small_grads")[0].reshape(N_DEV, n_small_pad)

    outs = {}

    def put(name, res, shape):
        for key, val in zip(("grad", "delta", "new_m", "new_v"), res):
            outs[(key, name)] = val.reshape(shape)

    ck = 256
    for key, name, w, m, v in (("gate", "ffn_w_gate", ffn_w_gate, m_ffn_w_gate, v_ffn_w_gate),
                               ("up", "ffn_w_up", ffn_w_up, m_ffn_w_up, v_ffn_w_up)):
        res = _adam("adam_" + name, l_ffn[key], (N_DEV, None, FF_SHARD, ck), lambda a, i: (0, a, 0, i),
                    w.reshape(4, d, FF_SHARD), m.reshape(4, d, FF_SHARD), v.reshape(4, d, FF_SHARD), True, ck)
        put(name, res, w.shape)
    res = _adam("adam_ffn_w_down", l_ffn["down"], (N_DEV, None, 176, d), lambda a, i: (0, a, i, 0),
                ffn_w_down.reshape(4, FF_SHARD, d), m_ffn_w_down.reshape(4, FF_SHARD, d),
                v_ffn_w_down.reshape(4, FF_SHARD, d), False, 176)
    put("ffn_w_down", res, ffn_w_down.shape)
    l_ssm_in2 = l_ssm_in.reshape(N_DEV, SSM_IN_SHARD, d)
    res = _adam("adam_ssm_w_in", l_ssm_in2, (N_DEV, SSM_IN_SHARD, 128), lambda a, i: (0, 0, i),
                ssm_w_in, m_ssm_w_in, v_ssm_w_in, True, 128)
    put("ssm_w_in", res, ssm_w_in.shape)
    res = _adam("adam_ssm_w_out", l_ssm_out, (N_DEV, 128, d), lambda a, i: (0, i, 0),
                ssm_w_out, m_ssm_w_out, v_ssm_w_out, False, 128)
    put("ssm_w_out", res, ssm_w_out.shape)
    res = _adam("adam_mla_w_in", l_mla_in, (N_DEV, 128, MLA_IN_DIM), lambda a, i: (0, 0, 0),
                mla_w_in, m_mla_w_in, v_mla_w_in, False, 128)
    put("mla_w_in", res, mla_w_in.shape)
    res = _adam("adam_mla_w_q_b", l_qb, (N_DEV, 192, 128), lambda a, i: (0, 0, i),
                mla_w_q_b, m_mla_w_q_b, v_mla_w_q_b, True, 128)
    put("mla_w_q_b", res, mla_w_q_b.shape)
    res = _adam("adam_mla_w_kv_b", l_kvb, (N_DEV, 256, 128), lambda a, i: (0, 0, i),
                mla_w_kv_b, m_mla_w_kv_b, v_mla_w_kv_b, True, 128)
    put("mla_w_kv_b", res, mla_w_kv_b.shape)
    res = _adam("adam_mla_w_out", l_mla_out, (N_DEV, 128, d), lambda a, i: (0, 0, 0),
                mla_w_out, m_mla_w_out, v_mla_w_out, False, 128)
    put("mla_w_out", res, mla_w_out.shape)

    small_params = (
        ("norm_w", norm_w, m_norm_w, v_norm_w, 6 * d, 6, 128), ("ssm_conv_w", ssm_conv_w, m_ssm_conv_w, v_ssm_conv_w,
                                                               CONV_K * CONV_DIM, CONV_K, 512),
        ("ssm_conv_b", ssm_conv_b, m_ssm_conv_b, v_ssm_conv_b, CONV_DIM, 0, 0),
        ("ssm_dt_bias", ssm_dt_bias, m_ssm_dt_bias, v_ssm_dt_bias, SSM_HEADS, 0, 0),
        ("ssm_a_log", ssm_a_log, m_ssm_a_log, v_ssm_a_log, SSM_HEADS, 0, 0),
        ("ssm_d", ssm_d, m_ssm_d, v_ssm_d, SSM_HEADS, 0, 0),
        ("ssm_norm_w", ssm_norm_w, m_ssm_norm_w, v_ssm_norm_w, D_INNER, 0, 0),
        ("mla_q_a_norm", mla_q_a_norm, m_mla_q_a_norm, v_mla_q_a_norm, Q_LORA, 1, 48),
        ("mla_kv_a_norm", mla_kv_a_norm, m_mla_kv_a_norm, v_mla_kv_a_norm, KV_LORA, 1, 32),
        ("mla_q_norm", mla_q_norm, m_mla_q_norm, v_mla_q_norm, QK_DIM, 0, 0),
        ("mla_k_norm", mla_k_norm, m_mla_k_norm, v_mla_k_norm, QK_DIM, 0, 0),
    )
    parts, ws_, ms_, vs_, off = [], [], [], [], 0
    for name, w, m, v, full, rows, shard in small_params:
        seg = gs[:, off:off + full]
        if rows:
            seg = _dyn(seg.reshape(N_DEV, rows, full // rows), me * shard, shard).reshape(N_DEV, rows * shard)
        parts.append(seg)
        ws_.append(w.reshape(1, -1))
        ms_.append(m.reshape(1, -1))
        vs_.append(v.reshape(1, -1))
        off += full
    n_loc = sum(p.shape[1] for p in parts)
    n_loc_pad = -(-n_loc // LANES) * LANES
    padc = lambda a, val=0.0: jnp.pad(jnp.concatenate(a, axis=1), ((0, 0), (0, n_loc_pad - n_loc)),
                                      constant_values=val)
    res = _adam("adam_small", padc(parts).reshape(N_DEV, 1, n_loc_pad), (N_DEV, 1, n_loc_pad), lambda a, i: (0, 0, 0),
                padc(ws_).reshape(1, 1, n_loc_pad), padc(ms_).reshape(1, 1, n_loc_pad),
                padc(vs_, 1.0).reshape(1, 1, n_loc_pad), False, 1)
    off = 0
    for name, w, m, v, full, rows, shard in small_params:
        nloc = w.size
        put(name, [r.reshape(-1)[off:off + nloc] for r in res], w.shape)
        off += nloc

    order = ("norm_w", "ffn_w_gate", "ffn_w_up", "ffn_w_down", "ssm_w_in", "ssm_conv_w", "ssm_conv_b", "ssm_dt_bias",
             "ssm_a_log", "ssm_d", "ssm_norm_w", "ssm_w_out", "mla_w_in", "mla_q_a_norm", "mla_kv_a_norm",
             "mla_w_q_b", "mla_w_kv_b", "mla_q_norm", "mla_k_norm", "mla_w_out")
    return (loss, grad_x, *[outs[(k, n)] for k in ("grad", "delta", "new_m", "new_v") for n in order])
```

```python
import functools
import math

import jax
import jax.numpy as jnp
import numpy as np
from jax import lax
from jax.experimental import pallas as pl
from jax.experimental.pallas import tpu as pltpu

F32 = jnp.float32
BF = jnp.bfloat16
SDS = jax.ShapeDtypeStruct

N_DEV = 8
D_MODEL = 1024
D_FF = 2816
FF_SHARD = D_FF // N_DEV
D_INNER = 2048
SSM_HEADS = 32
SSM_GROUPS = 8
SSM_HPG = 4
SSM_STATE = 128
CONV_K = 4
CONV_DIM = 4096
SSM_IN_DIM = 6176
SSM_IN_SHARD = SSM_IN_DIM // N_DEV
NORM_GROUP = 256
CHUNK = 128
MLA_HEADS = 16
Q_LORA = 384
KV_LORA = 256
QK_NOPE = 64
QK_ROPE = 32
QK_DIM = 96
MLA_IN_DIM = 672
HEAD_PAD = 128
ROPE_THETA = 10000.0
EPS = 1e-6
LANES = 128

ADAM_LR = 0.001
ADAM_B1 = 0.9
ADAM_B2 = 0.999
ADAM_EPS = 1e-08
ADAM_WD = 0.01
ADAM_STEP = 10

VMEM_BIG = 56 * 1024 * 1024

NN = ((1,), (0,))
NT = ((1,), (1,))
TN = ((0,), (0,))


def _dotf(a, b, dn):
    return lax.dot_general(a.astype(BF), b.astype(BF), (dn, ((), ())), preferred_element_type=F32)


def _dot_hi(a, b, dn=NN):
    return lax.dot_general(a, b, (dn, ((), ())), precision=lax.Precision.HIGHEST, preferred_element_type=F32)


def _sigmoid(x):
    return jax.nn.sigmoid(x)


def _silu(x):
    return x * _sigmoid(x)


def _softplus(x):
    return jnp.maximum(x, 0.0) + jnp.log(1.0 + jnp.exp(-jnp.abs(x)))


def _cparams(n_grid, vmem=None):
    return pltpu.CompilerParams(dimension_semantics=("arbitrary",) * n_grid, vmem_limit_bytes=vmem)


def _fmm(name, grid_mn, pairs, outs, *, epi=None, extras=(), n_acc=1, acc_shape=None, vmem=None, alias=None,
         joint=False, hosts=None, row_split=1):
    comm = hosts.comm(name) if hosts is not None else None
    if joint:
        nk_total = pairs[0][7]
        assert all(p[7] == nk_total for p in pairs)
        starts = [0] * len(pairs)
    else:
        nk_total = sum(p[7] for p in pairs)
        starts = []
        s = 0
        for p in pairs:
            starts.append(s)
            s += p[7]
    n_pairs, n_extras, n_outs = len(pairs), len(extras), len(outs)
    single = nk_total == 1

    n_ci = len(comm.arrays) if comm is not None else 0
    n_co = len(comm.out_shapes) if comm is not None else 0
    n_scratch_acc = 0 if single else n_acc

    def body(*refs):
        ab_refs = refs[: 2 * n_pairs]
        e_refs = refs[2 * n_pairs: 2 * n_pairs + n_extras]
        pos = 2 * n_pairs + n_extras + (1 if alias is not None else 0)
        ci_refs = refs[pos: pos + n_ci]
        pos += n_ci
        o_refs = refs[pos: pos + n_outs]
        co_refs = refs[pos + n_outs: pos + n_outs + n_co]
        pos += n_outs + n_co
        acc_refs = refs[pos: pos + n_scratch_acc]
        sem_refs = refs[pos + n_scratch_acc:]
        i, j, k = pl.program_id(0), pl.program_id(1), pl.program_id(2)

        if comm is not None:
            @pl.when((i == 0) & (j == 0) & (k == 0))
            def _():
                comm.start(ci_refs, co_refs, sem_refs)

        compute(ab_refs, e_refs, o_refs, acc_refs, i, j, k)

        if comm is not None:
            @pl.when((i == grid_mn[0] - 1) & (j == grid_mn[1] - 1) & (k == nk_total - 1))
            def _():
                comm.wait(ci_refs, co_refs, sem_refs)

    def compute(ab_refs, e_refs, o_refs, acc_refs, i, j, k):

        def finish(accs, rows=slice(None)):
            res = epi(accs, *[e[rows] for e in e_refs]) if epi is not None else accs
            if not isinstance(res, (tuple, list)):
                res = (res,)
            first = (i == 0) & (j == 0)
            for o, r, spec in zip(o_refs, res, outs):
                if spec[3]:
                    @pl.when(first)
                    def _(o=o, r=r):
                        o[...] = r.astype(o.dtype)

                    @pl.when(jnp.logical_not(first))
                    def _(o=o, r=r):
                        o[...] += r.astype(o.dtype)
                else:
                    o[rows] = r.astype(o.dtype)

        if single:
            tm_all = ab_refs[0].shape[0]
            ch = tm_all // row_split
            for c in range(row_split):
                rows = slice(c * ch, (c + 1) * ch) if row_split > 1 else slice(None)
                accs = [None] * n_acc
                for p, pr in enumerate(pairs):
                    d = _dotf(ab_refs[2 * p][rows], ab_refs[2 * p + 1][...], pr[6])
                    accs[pr[8]] = d if accs[pr[8]] is None else accs[pr[8]] + d
                finish(accs, rows)
            return

        @pl.when(k == 0)
        def _():
            for a in acc_refs:
                a[...] = jnp.zeros_like(a)

        for p, pr in enumerate(pairs):
            def step(p=p, pr=pr):
                acc_refs[pr[8]][...] += _dotf(ab_refs[2 * p][...], ab_refs[2 * p + 1][...], pr[6])

            if n_pairs == 1 or joint:
                step()
            else:
                pl.when((k >= starts[p]) & (k < starts[p] + pr[7]))(step)

        @pl.when(k == nk_total - 1)
        def _():
            finish([a[...] for a in acc_refs])

    in_specs, args = [], []
    for p, pr in enumerate(pairs):
        a, a_blk, a_idx, b, b_blk, b_idx, _, nk, _ = pr
        st = starts[p]

        def amap(i, j, k, a_idx=a_idx, st=st, nk=nk):
            return a_idx(i, j, jnp.clip(k - st, 0, nk - 1))

        def bmap(i, j, k, b_idx=b_idx, st=st, nk=nk):
            return b_idx(i, j, jnp.clip(k - st, 0, nk - 1))

        in_specs += [pl.BlockSpec(a_blk, amap), pl.BlockSpec(b_blk, bmap)]
        args += [a, b]
    for arr, blk, idx in extras:
        in_specs.append(pl.BlockSpec(blk, lambda i, j, k, idx=idx: idx(i, j)))
        args.append(arr)
    io_alias = {}
    if alias is not None:
        in_specs.append(pl.BlockSpec(memory_space=pl.ANY))
        io_alias = {len(args): 0}
        args.append(alias)
    out_specs = [pl.BlockSpec(blk, lambda i, j, k, idx=idx: idx(i, j)) for _, blk, idx, _ in outs]
    out_shape = [o[0] for o in outs]
    scratch = [] if single else [pltpu.VMEM(acc_shape, F32) for _ in range(n_acc)]
    if comm is not None:
        hbm = pl.BlockSpec(memory_space=pl.ANY)
        in_specs += [hbm] * n_ci
        args += list(comm.arrays)
        out_specs += [hbm] * n_co
        out_shape += list(comm.out_shapes)
        scratch += comm.sems
    res = pl.pallas_call(
        body, name=name, grid=(grid_mn[0], grid_mn[1], nk_total), in_specs=in_specs, out_specs=out_specs,
        out_shape=out_shape, scratch_shapes=scratch, input_output_aliases=io_alias,
        compiler_params=_cparams(3, vmem),
    )(*args)
    if comm is not None:
        hosts.done(name, res[n_outs:])
    return res[:n_outs]


def _ew(name, grid, fn, ins, outs, *, acc_axes=(), vmem=None):
    n_in = len(ins)

    def body(*refs):
        res = fn(*[r[...] for r in refs[:n_in]])
        if not isinstance(res, (tuple, list)):
            res = (res,)
        first = None
        for ax in acc_axes:
            c = pl.program_id(ax) == 0
            first = c if first is None else (first & c)
        for o, r, spec in zip(refs[n_in:], res, outs):
            if spec[3]:
                @pl.when(first)
                def _(o=o, r=r):
                    o[...] = r.astype(o.dtype)

                @pl.when(jnp.logical_not(first))
                def _(o=o, r=r):
                    o[...] += r.astype(o.dtype)
            else:
                o[...] = r.astype(o.dtype)

    return pl.pallas_call(
        body, name=name, grid=grid,
        in_specs=[pl.BlockSpec(blk, idx) for _, blk, idx in ins],
        out_specs=[pl.BlockSpec(blk, idx) for _, blk, idx, _ in outs],
        out_shape=[o[0] for o in outs],
        compiler_params=_cparams(len(grid), vmem),
    )(*[a for a, _, _ in ins])


def _row_tile(t, want):
    tm = min(want, t)
    assert t % tm == 0, (t, tm)
    return tm


def _rms_fn(x, w):
    return x * lax.rsqrt(jnp.mean(x * x, axis=-1, keepdims=True) + EPS) * w


def _rms_fwd(name, x, w):
    t, d = x.shape
    tm = _row_tile(t, 512)
    return _ew(name, (t // tm,), _rms_fn,
               [(x, (tm, d), lambda i: (i, 0)), (w, (1, d), lambda i: (0, 0))],
               [(SDS((t, d), BF), (tm, d), lambda i: (i, 0), False)])[0]


def _rms_bwd(name, x, w, dh, dres):
    t, d = x.shape
    tm = _row_tile(t, 512)

    def fn(x, w, dh, dres):
        _, vjp = jax.vjp(_rms_fn, x, w)
        dx, dw = vjp(dh.astype(F32))
        return dx + dres, dw

    row = lambda i: (i, 0)
    return _ew(name, (t // tm,), fn,
               [(x, (tm, d), row), (w, (1, d), lambda i: (0, 0)), (dh, (tm, d), row), (dres, (tm, d), row)],
               [(SDS((t, d), F32), (tm, d), row, False), (SDS((1, d), F32), (1, d), lambda i: (0, 0), True)],
               acc_axes=(0,))


def _loss_and_grad(y, target):
    t, d = y.shape
    tm = _row_tile(t, 512)

    def fn(y, tg):
        e = y - tg
        return e * (1.0 / d), jnp.sum(e * e, axis=0, keepdims=True) * (0.5 / d)

    row = lambda i: (i, 0)
    return _ew("loss_head", (t // tm,), fn, [(y, (tm, d), row), (target, (tm, d), row)],
               [(SDS((t, d), F32), (tm, d), row, False), (SDS((1, d), F32), (1, d), lambda i: (0, 0), True)],
               acc_axes=(0,))


def _ffn_fwd(tag, x, nw, wf, hosts=None):
    gate_t, up_t = wf[0], wf[1]
    t, d = x.shape
    h = _rms_fwd(tag + "_rms", x, nw)
    tm, tn = _row_tile(t, 256), D_FF

    def epi(accs):
        g, u = accs
        s = _sigmoid(g)
        sg = g * s
        return 0.5 * sg, 0.5 * (s * (1.0 + g * (1.0 - s))), u, sg * u

    hblk = (h, (tm, d), lambda i, j, k: (i, 0))
    col = lambda i, j: (i, j)
    tblk = lambda w: (w, (tn, d), lambda i, j, k: (j, 0), NT, 1)
    sgh, dsgh, u, a = _fmm(
        tag + "_up", (t // tm, D_FF // tn), [hblk + tblk(gate_t) + (0,), hblk + tblk(up_t) + (1,)],
        [(SDS((t, D_FF), BF), (tm, tn), col, False)] * 4, epi=epi, n_acc=2, joint=True, hosts=hosts, vmem=VMEM_BIG)
    down = wf[2]
    tm2 = _row_tile(t, 512)
    y = _fmm(
        tag + "_down", (t // tm2, 1),
        [(a, (tm2, D_FF), lambda i, j, k: (i, 0), down, (D_FF, d), lambda i, j, k: (0, 0), NN, 1, 0)],
        [(SDS((t, d), F32), (tm2, d), lambda i, j: (i, 0), False)],
        extras=[(x, (tm2, d), lambda i, j: (i, 0))],
        epi=lambda accs, xr: xr + 0.5 * accs[0], vmem=VMEM_BIG, hosts=hosts)[0]
    return y, (x, h, sgh, dsgh, u, a)


def _wgrad(name, a, b, m, n, *, tm, tn, tk=2048, scale=None, out_dtype=BF, hosts=None):
    t = a.shape[0]
    tk = _row_tile(t, tk)
    epi = (lambda accs: accs[0] * scale) if scale is not None else None
    return _fmm(name, (m // tm, n // tn),
                [(a, (tk, tm), lambda i, j, k: (k, i), b, (tk, tn), lambda i, j, k: (k, j), TN, t // tk, 0)],
                [(SDS((m, n), out_dtype), (tm, tn), lambda i, j: (i, j), False)], epi=epi, acc_shape=(tm, tn),
                vmem=VMEM_BIG, hosts=hosts)[0]


def _ffn_bwd(tag, dy, nw, wf, saved, hosts=None):
    gate_t, up_t, down = wf
    x, h, sgh, dsgh, u, a = saved
    t, d = x.shape
    tm, tn = _row_tile(t, 256), D_FF

    def epi(accs, sgh, dsgh, u):
        da = accs[0]
        return da * (u.astype(F32) * dsgh.astype(F32)), da * sgh.astype(F32)

    col = lambda i, j: (i, j)
    dg, du = _fmm(
        tag + "_bwd_act", (t // tm, D_FF // tn),
        [(dy, (tm, d), lambda i, j, k: (i, 0), down, (tn, d), lambda i, j, k: (j, 0), NT, 1, 0)],
        [(SDS((t, D_FF), BF), (tm, tn), col, False)] * 2,
        extras=[(sgh, (tm, tn), col), (dsgh, (tm, tn), col), (u, (tm, tn), col)], epi=epi, hosts=hosts,
        vmem=VMEM_BIG)
    tm2 = _row_tile(t, 256)
    full = lambda i, j, k: (0, 0)
    dh = _fmm(
        tag + "_bwd_dh", (t // tm2, 1),
        [(dg, (tm2, D_FF), lambda i, j, k: (i, 0), gate_t, (D_FF, d), full, NN, 1, 0),
         (du, (tm2, D_FF), lambda i, j, k: (i, 0), up_t, (D_FF, d), full, NN, 1, 0)],
        [(SDS((t, d), BF), (tm2, d), lambda i, j: (i, 0), False)], vmem=VMEM_BIG, joint=True, hosts=hosts)[0]
    dx, dnw = _rms_bwd(tag + "_bwd_rms", x, nw, dh, dy)
    half = D_FF // 2
    g_gate = _wgrad(tag + "_wg", dg, h, D_FF, d, tm=half, tn=d, hosts=hosts)
    if hosts is not None:
        hosts.ctx["g_" + tag + "_gate"] = g_gate
    g_up = _wgrad(tag + "_wu", du, h, D_FF, d, tm=half, tn=d, hosts=hosts)
    if hosts is not None:
        hosts.ctx["g_" + tag + "_up"] = g_up
    g_down = _wgrad(tag + "_wd", a, dy, D_FF, d, tm=half, tn=d, scale=0.5, hosts=hosts)
    return dx, dnw, (g_gate, g_up, g_down)


def _shift_down(cur, prev8, j):
    rolled = pltpu.roll(cur, j, 0)
    sub = lax.broadcasted_iota(jnp.int32, prev8.shape, 0)
    top = jnp.where(sub < j, pltpu.roll(prev8, j, 0), rolled[:8])
    return jnp.concatenate([top, rolled[8:]], axis=0)


def _shift_up(cur, next8, j):
    n = cur.shape[0]
    rolled = pltpu.roll(cur, n - j, 0)
    sub = lax.broadcasted_iota(jnp.int32, next8.shape, 0)
    bot = jnp.where(sub >= 8 - j, pltpu.roll(next8, 8 - j, 0), rolled[n - 8:])
    return jnp.concatenate([rolled[: n - 8], bot], axis=0)


HALO = 16


def _conv_fwd(xbc, w, b, seq):
    t, c = xbc.shape
    ts, tc = _row_tile(seq, 512), 1024
    tiles_per_seq = seq // ts
    hb = ts // HALO

    def fn(cur, prev, w, b):
        i = pl.program_id(1)
        cur = cur.astype(F32)
        prev8 = jnp.where(i % tiles_per_seq == 0, 0.0, prev.astype(F32)[HALO - 8:])
        out = b + w[3:4] * cur
        for j in range(1, CONV_K):
            out = out + w[3 - j:4 - j] * _shift_down(cur, prev8, j)
        return out, _silu(out)

    return _ew("ssm_conv_fwd", (c // tc, t // ts), fn,
               [(xbc, (ts, tc), lambda j, i: (i, j)),
                (xbc, (HALO, tc), lambda j, i: (jnp.maximum(i * hb - 1, 0), j)),
                (w, (CONV_K, tc), lambda j, i: (0, j)), (b, (1, tc), lambda j, i: (0, j))],
               [(SDS((t, c), BF), (ts, tc), lambda j, i: (i, j), False)] * 2, vmem=VMEM_BIG)


def _conv_bwd(tag, dxa, cpre, xbc, w, col0, seq):
    t, width = dxa.shape
    ts, tc = _row_tile(seq, 512), 1024
    tiles_per_seq = seq // ts
    hb = ts // HALO
    cb0 = col0 // tc
    n_halo_blocks = t // HALO

    def dsilu(cv, dv):
        cv = cv.astype(F32)
        s = _sigmoid(cv)
        return dv.astype(F32) * (s * (1.0 + cv * (1.0 - s)))

    def fn(dxa_c, dxa_n, c_c, c_n, x_c, x_p, w):
        i = pl.program_id(1)
        dc = dsilu(c_c, dxa_c)
        last = i % tiles_per_seq == tiles_per_seq - 1
        dc_n = jnp.where(last, 0.0, dsilu(c_n, dxa_n)[:8])
        dx = w[3:4] * dc
        for j in range(1, CONV_K):
            dx = dx + w[3 - j:4 - j] * _shift_up(dc, dc_n, j)
        cur = x_c.astype(F32)
        prev8 = jnp.where(i % tiles_per_seq == 0, 0.0, x_p.astype(F32)[HALO - 8:])
        rows = [jnp.sum(dc * cur, axis=0, keepdims=True)]
        for j in range(1, CONV_K):
            rows.append(jnp.sum(dc * _shift_down(cur, prev8, j), axis=0, keepdims=True))
        sub8 = lax.broadcasted_iota(jnp.int32, (8, dc.shape[1]), 0)
        dw = jnp.zeros((8, dc.shape[1]), F32)
        for kk in range(CONV_K):
            dw = jnp.where(sub8 == kk, rows[CONV_K - 1 - kk], dw)
        return dx, dw, jnp.sum(dc, axis=0, keepdims=True)

    nxt = lambda j, i: (jnp.minimum((i + 1) * hb, n_halo_blocks - 1), j)
    nxt_off = lambda j, i: (jnp.minimum((i + 1) * hb, n_halo_blocks - 1), j + cb0)
    return _ew(tag, (width // tc, t // ts), fn,
               [(dxa, (ts, tc), lambda j, i: (i, j)), (dxa, (HALO, tc), nxt),
                (cpre, (ts, tc), lambda j, i: (i, j + cb0)), (cpre, (HALO, tc), nxt_off),
                (xbc, (ts, tc), lambda j, i: (i, j + cb0)),
                (xbc, (HALO, tc), lambda j, i: (jnp.maximum(i * hb - 1, 0), j + cb0)),
                (w, (CONV_K, tc), lambda j, i: (0, j + cb0))],
               [(SDS((t, width), BF), (ts, tc), lambda j, i: (i, j), False),
                (SDS((8, width), F32), (8, tc), lambda j, i: (0, j), True),
                (SDS((1, width), F32), (1, tc), lambda j, i: (0, j), True)],
               acc_axes=(1,), vmem=VMEM_BIG)


def _ssd_chunk(xs, bm, cm, dtr, st, dtb, alog, dsk):
    ell = xs.shape[0]
    xs = xs.astype(F32)
    lane = lax.broadcasted_iota(jnp.int32, (ell, LANES), 1)
    sub = lax.broadcasted_iota(jnp.int32, (ell, LANES), 0)
    lane1 = lax.broadcasted_iota(jnp.int32, (1, LANES), 1)
    causal = sub >= lane
    dt = _softplus(dtr + dtb)
    da = dt * (-jnp.exp(alog))
    acs = _dot_hi(causal.astype(F32), da)
    acs_t = acs.T
    cb = _dotf(cm, bm, NT)
    lo = lane < 64
    ys, news = [], []
    for p in range(2):
        xp = xs[:, LANES * p:LANES * (p + 1)]
        sp = st[LANES * p:LANES * (p + 1), :]
        col, dtc, last, dsel = [], [], [], []
        y_diag = None
        for q in range(2):
            r = 2 * p + q
            col_r = jnp.sum(jnp.where(lane == r, acs, 0.0), axis=1, keepdims=True)
            row_r = jnp.sum(jnp.where(sub == r, acs_t, 0.0), axis=0, keepdims=True)
            dtc_r = jnp.sum(jnp.where(lane == r, dt, 0.0), axis=1, keepdims=True)
            decay = jnp.exp(jnp.where(causal, col_r - row_r, -jnp.inf))
            head = lo if q == 0 else jnp.logical_not(lo)
            d = _dotf(cb * decay, jnp.where(head, xp * dtc_r, 0.0), NN)
            y_diag = d if y_diag is None else y_diag + d
            col.append(col_r)
            dtc.append(dtc_r)
            last.append(jnp.sum(jnp.where(sub[:, :1] == ell - 1, col_r, 0.0), axis=0, keepdims=True))
            dsel.append(jnp.sum(jnp.where(lane1 == r, dsk, 0.0), axis=1, keepdims=True))
        y_off = _dotf(cm, sp, NT) * jnp.where(lo, jnp.exp(col[0]), jnp.exp(col[1]))
        xw = jnp.where(lo, xp * (dtc[0] * jnp.exp(last[0] - col[0])), xp * (dtc[1] * jnp.exp(last[1] - col[1])))
        new = sp * jnp.where(sub < 64, jnp.exp(last[0]), jnp.exp(last[1])) + _dotf(xw, bm, TN)
        ys.append(y_diag + y_off + jnp.where(lo, dsel[0], dsel[1]) * xp)
        news.append(new)
    return jnp.concatenate(ys, axis=1), jnp.concatenate(news, axis=0)


SSD_GP = 4
SSD_GP_BWD = 4
XW, GW = 2 * LANES * SSD_GP, LANES * SSD_GP


def _grp(ref, q, width):
    return ref[:, width * q:width * (q + 1)]


def _ssd_fwd(xa, dtr, dtb, alog, dsk, nb, seq):
    t = xa.shape[0]
    nc = seq // CHUNK
    row = lambda g, b, c: (b * nc + c, g)
    par = pl.BlockSpec((1, GW), lambda g, b, c: (0, g))
    b_off, c_off = D_INNER // GW, (D_INNER + SSM_GROUPS * SSM_STATE) // GW

    def body(xs, bm, cm, dtr, dtb, alog, dsk, y_ref, st_out, st_ref):
        @pl.when(pl.program_id(2) == 0)
        def _():
            st_ref[...] = jnp.zeros_like(st_ref)

        ins = [(_grp(xs, q, 2 * LANES), _grp(bm, q, LANES), _grp(cm, q, LANES), _grp(dtr, q, LANES), st_ref[q],
                _grp(dtb, q, LANES), _grp(alog, q, LANES), _grp(dsk, q, LANES)) for q in range(SSD_GP)]
        res = [_ssd_chunk(*a) for a in ins]
        for q in range(SSD_GP):
            st_out[q] = ins[q][4]
            y_ref[:, 2 * LANES * q:2 * LANES * (q + 1)] = res[q][0]
            st_ref[q] = res[q][1]

    specs = [pl.BlockSpec((CHUNK, XW), row),
             pl.BlockSpec((CHUNK, GW), lambda g, b, c: (b * nc + c, b_off + g)),
             pl.BlockSpec((CHUNK, GW), lambda g, b, c: (b * nc + c, c_off + g)),
             pl.BlockSpec((CHUNK, GW), row), par, par, par]
    return pl.pallas_call(
        body, name="ssd_fwd", grid=(SSM_GROUPS // SSD_GP, nb, nc), in_specs=specs,
        out_specs=[pl.BlockSpec((CHUNK, XW), row),
                   pl.BlockSpec((SSD_GP, None, None, 2 * LANES, LANES), lambda g, b, c: (g, b, c, 0, 0))],
        out_shape=[SDS((t, D_INNER), F32), SDS((SSM_GROUPS, nb, nc, 2 * LANES, LANES), F32)],
        scratch_shapes=[pltpu.VMEM((SSD_GP, 2 * LANES, LANES), F32)],
        compiler_params=_cparams(3),
    )(xa, xa, xa, dtr, dtb, alog, dsk)


def _ssd_bwd(xa, dtr, dtb, alog, dsk, states, dy, nb, seq):
    t = xa.shape[0]
    nc = seq // CHUNK
    rev = lambda c: nc - 1 - c
    row = lambda g, b, c: (b * nc + rev(c), g)
    gp = SSD_GP_BWD
    xw, gw = 2 * LANES * gp, LANES * gp
    par = pl.BlockSpec((1, gw), lambda g, b, c: (0, g))
    b_off, c_off = D_INNER // gw, (D_INNER + SSM_GROUPS * SSM_STATE) // gw

    def body(xs, bm, cm, dtr, dtb, alog, dsk, st_in, dy, dxs, dbm, dcm, ddtr, ddtb, dalog, ddsk, dst_ref):
        @pl.when(pl.program_id(2) == 0)
        def _():
            dst_ref[...] = jnp.zeros_like(dst_ref)

        ins = [(_grp(xs, q, 2 * LANES), _grp(bm, q, LANES), _grp(cm, q, LANES), _grp(dtr, q, LANES), st_in[q],
                _grp(dtb, q, LANES), _grp(alog, q, LANES), _grp(dsk, q, LANES)) for q in range(gp)]
        cts = [(_grp(dy, q, 2 * LANES), dst_ref[q]) for q in range(gp)]
        gs = [jax.vjp(_ssd_chunk, *a)[1](ct) for a, ct in zip(ins, cts)]
        for q, g in enumerate(gs):
            dxs[:, 2 * LANES * q:2 * LANES * (q + 1)] = g[0]
            lanes = slice(LANES * q, LANES * (q + 1))
            dbm[:, lanes] = g[1]
            dcm[:, lanes] = g[2]
            ddtr[:, lanes] = g[3]
            dst_ref[q] = g[4]
        first = (pl.program_id(1) == 0) & (pl.program_id(2) == 0)
        for o, k in ((ddtb, 5), (dalog, 6), (ddsk, 7)):
            v = jnp.concatenate([g[k] for g in gs], axis=1)

            @pl.when(first)
            def _(o=o, v=v):
                o[...] = v

            @pl.when(jnp.logical_not(first))
            def _(o=o, v=v):
                o[...] += v

    in_specs = [
        pl.BlockSpec((CHUNK, xw), row),
        pl.BlockSpec((CHUNK, gw), lambda g, b, c: (b * nc + rev(c), b_off + g)),
        pl.BlockSpec((CHUNK, gw), lambda g, b, c: (b * nc + rev(c), c_off + g)),
        pl.BlockSpec((CHUNK, gw), row), par, par, par,
        pl.BlockSpec((gp, None, None, 2 * LANES, LANES), lambda g, b, c: (g, b, rev(c), 0, 0)),
        pl.BlockSpec((CHUNK, xw), row),
    ]
    out_specs = [pl.BlockSpec((CHUNK, xw), row), pl.BlockSpec((CHUNK, gw), row), pl.BlockSpec((CHUNK, gw), row),
                 pl.BlockSpec((CHUNK, gw), row), par, par, par]
    out_shape = [SDS((t, D_INNER), BF), SDS((t, SSM_GROUPS * LANES), BF), SDS((t, SSM_GROUPS * LANES), BF),
                 SDS((t, SSM_GROUPS * LANES), F32)] + [SDS((1, SSM_GROUPS * LANES), F32)] * 3
    return pl.pallas_call(
        body, name="ssd_bwd", grid=(SSM_GROUPS // gp, nb, nc), in_specs=in_specs, out_specs=out_specs,
        out_shape=out_shape, scratch_shapes=[pltpu.VMEM((gp, 2 * LANES, LANES), F32)],
        compiler_params=_cparams(3, VMEM_BIG),
    )(xa, xa, xa, dtr, dtb, alog, dsk, states, dy)


def _gated_fn(y, z, w):
    g = y * _silu(z.astype(F32))
    return g * lax.rsqrt(jnp.mean(g * g, axis=-1, keepdims=True) + EPS) * w


def _gated_norm_fwd(y, z, w):
    t = y.shape[0]
    tm = _row_tile(t, 2048)
    blk = ((tm, NORM_GROUP), lambda g, i: (i, g))
    return _ew("ssm_gnorm_fwd", (SSM_GROUPS, t // tm), _gated_fn,
               [(y,) + blk, (z,) + blk, (w, (1, NORM_GROUP), lambda g, i: (0, g))],
               [(SDS((t, D_INNER), BF),) + blk + (False,)], vmem=VMEM_BIG)[0]


def _gated_norm_bwd(y, z, w, dout):
    t = y.shape[0]
    tm = _row_tile(t, 2048)
    blk = ((tm, NORM_GROUP), lambda g, i: (i, g))
    par = ((1, NORM_GROUP), lambda g, i: (0, g))

    def fn(y, z, w, dout):
        _, vjp = jax.vjp(_gated_fn, y, z, w)
        return vjp(dout.astype(F32))

    return _ew("ssm_gnorm_bwd", (SSM_GROUPS, t // tm), fn,
               [(y,) + blk, (z,) + blk, (w,) + par, (dout,) + blk],
               [(SDS((t, D_INNER), F32),) + blk + (False,), (SDS((t, D_INNER), BF),) + blk + (False,),
                (SDS((1, D_INNER), F32),) + par + (True,)],
               acc_axes=(1,), vmem=VMEM_BIG)


def _proj_nt(name, h, wt, n, out_dtype, tn=1024, hosts=None):
    t, kdim = h.shape
    tm = _row_tile(t, 1024)
    return _fmm(name, (t // tm, n // tn),
                [(h, (tm, kdim), lambda i, j, k: (i, 0), wt, (tn, kdim), lambda i, j, k: (j, 0), NT, 1, 0)],
                [(SDS((t, n), out_dtype), (tm, tn), lambda i, j: (i, j), False)], hosts=hosts, vmem=VMEM_BIG)[0]


def _seg_nn(name, parts, n, out_dtype, tk=256, hosts=None):
    t = parts[0][0].shape[0]
    tm = _row_tile(t, 512)
    pairs = []
    for a, w, row0 in parts:
        kp = a.shape[1]
        tkp = min(tk, kp)
        r0 = row0 // tkp
        pairs.append((a, (tm, tkp), lambda i, j, k: (i, k), w, (tkp, n), lambda i, j, k, r0=r0: (k + r0, 0),
                      NN, kp // tkp, 0))
    return _fmm(name, (t // tm, 1), pairs, [(SDS((t, n), out_dtype), (tm, n), lambda i, j: (i, 0), False)],
                acc_shape=(tm, n), hosts=hosts)[0]


def _ssm_fwd(x, nw, ws, small, nb, seq, hosts=None):
    t, d = x.shape
    h = _rms_fwd("ssm_rms", x, nw)
    z = _proj_nt("ssm_in_z", h, ws["z_t"], D_INNER, BF, hosts=hosts)
    xbc = _proj_nt("ssm_in_xbc", h, ws["xbc_t"], CONV_DIM, BF, hosts=hosts)
    dtr = _proj_nt("ssm_in_dt", h, ws["dt_t"], SSM_GROUPS * LANES, F32)
    cpre, xa = _conv_fwd(xbc, small["conv_w"], small["conv_b"], seq)
    y, states = _ssd_fwd(xa, dtr, small["dt_bias"], small["a_log"], small["d_skip"], nb, seq)
    gn = _gated_norm_fwd(y, z, small["ssm_norm_w"])
    tm = _row_tile(t, 512)
    out = _fmm("ssm_out", (t // tm, 1),
               [(gn, (tm, D_INNER), lambda i, j, k: (i, 0), ws["out"], (D_INNER, d), lambda i, j, k: (0, 0), NN, 1, 0)],
               [(SDS((t, d), F32), (tm, d), lambda i, j: (i, 0), False)],
               extras=[(x, (tm, d), lambda i, j: (i, 0))], epi=lambda accs, xr: xr + accs[0], hosts=hosts)[0]
    return out, (x, h, z, xbc, dtr, cpre, xa, y, states, gn)


def _ssm_bwd(dy, nw, ws, small, saved, nb, seq, hosts=None):
    x, h, z, xbc, dtr, cpre, xa, y, states, gn = saved
    t, d = x.shape
    dgn = _proj_nt("ssm_bwd_dgn", dy, ws["out"], D_INNER, BF, hosts=hosts)
    d_out = _wgrad("ssm_w_out_g", gn, dy, D_INNER, d, tm=D_INNER // 2, tn=d)
    dyssd, dz, d_normw = _gated_norm_bwd(y, z, small["ssm_norm_w"], dgn)
    dxs, dbm, dcm, ddtr, d_dtb, d_alog, d_dsk = _ssd_bwd(
        xa, dtr, small["dt_bias"], small["a_log"], small["d_skip"], states, dyssd, nb, seq)
    dxbc_x, dcw_x, dcb_x = _conv_bwd("ssm_conv_bwd_x", dxs, cpre, xbc, small["conv_w"], 0, seq)
    dxbc_b, dcw_b, dcb_b = _conv_bwd("ssm_conv_bwd_b", dbm, cpre, xbc, small["conv_w"], D_INNER, seq)
    dxbc_c, dcw_c, dcb_c = _conv_bwd("ssm_conv_bwd_c", dcm, cpre, xbc, small["conv_w"], D_INNER + 1024, seq)
    parts = [(dz, ws["z_t"], 0), (dxbc_x, ws["xbc_t"], 0), (dxbc_b, ws["xbc_t"], D_INNER),
             (dxbc_c, ws["xbc_t"], D_INNER + 1024), (ddtr, ws["dt_t"], 0)]
    dh = _seg_nn("ssm_bwd_dh", parts, d, BF, tk=1024, hosts=hosts)
    dx, dnw = _rms_bwd("ssm_bwd_rms", x, nw, dh, dy)
    g = {
        "z_t": _wgrad("ssm_w_z_g", dz, h, D_INNER, d, tm=1024, tn=d, out_dtype=F32),
        "x_t": _wgrad("ssm_w_x_g", dxbc_x, h, D_INNER, d, tm=1024, tn=d, out_dtype=F32),
        "b_t": _wgrad("ssm_w_b_g", dxbc_b, h, 1024, d, tm=1024, tn=d, out_dtype=F32),
        "c_t": _wgrad("ssm_w_c_g", dxbc_c, h, 1024, d, tm=1024, tn=d, out_dtype=F32),
        "dt_t": _wgrad("ssm_w_dt_g", ddtr, h, 1024, d, tm=1024, tn=d, out_dtype=F32),
        "out": d_out,
        "conv_w": jnp.concatenate([dcw_x[:CONV_K], dcw_b[:CONV_K], dcw_c[:CONV_K]], axis=1),
        "conv_b": jnp.concatenate([dcb_x, dcb_b, dcb_c], axis=1),
        "dt_bias": d_dtb, "a_log": d_alog, "d_skip": d_dsk, "ssm_norm_w": d_normw,
    }
    return dx, dnw, g


def _lane_masks(shape):
    lane = lax.broadcasted_iota(jnp.int32, shape, len(shape) - 1)
    return lane < QK_NOPE, (lane >= QK_NOPE) & (lane < QK_DIM)


def _segment_matrix():
    p = np.zeros((HEAD_PAD, HEAD_PAD), np.float32)
    p[:QK_NOPE, :QK_NOPE] = 1.0
    p[QK_NOPE:QK_DIM, QK_NOPE:QK_DIM] = 1.0
    return jnp.asarray(p)


def _segment_rstd(x, seg):
    lane = lax.broadcasted_iota(jnp.int32, (1, x.shape[-1]), 1)
    inv_n = jnp.where(lane < QK_NOPE, 1.0 / QK_NOPE, 1.0 / QK_ROPE)
    xx = x * x
    hi = xx.astype(BF)
    lo = (xx - hi.astype(F32)).astype(BF)
    segb = seg.astype(BF)
    ss = _dotf(hi, segb, NN) + _dotf(lo, segb, NN)
    return lax.rsqrt(ss * inv_n + EPS)


def _rope_tables(positions_col):
    t = positions_col.shape[0]
    tm = _row_tile(t, 512)
    freq = np.zeros((1, HEAD_PAD), np.float32)
    inv = 1.0 / (ROPE_THETA ** (np.arange(0, QK_ROPE, 2, dtype=np.float32) / QK_ROPE))
    freq[0, QK_NOPE:QK_NOPE + QK_ROPE // 2] = inv
    freq[0, QK_NOPE + QK_ROPE // 2:QK_DIM] = inv
    sign = np.zeros((1, HEAD_PAD), np.float32)
    sign[0, QK_NOPE:QK_NOPE + QK_ROPE // 2] = -1.0
    sign[0, QK_NOPE + QK_ROPE // 2:QK_DIM] = 1.0

    def fn(pos, freq, sign):
        ang = pos.astype(F32) * freq
        nope, rope = _lane_masks(ang.shape)
        return jnp.where(nope, 1.0, jnp.where(rope, jnp.cos(ang), 0.0)), jnp.sin(ang) * sign

    row = lambda i: (i, 0)
    par = ((1, HEAD_PAD), lambda i: (0, 0))
    return _ew("mla_rope_tables", (t // tm,), fn,
               [(positions_col, (tm, 1), row), (jnp.asarray(freq),) + par, (jnp.asarray(sign),) + par],
               [(SDS((t, HEAD_PAD), F32), (tm, HEAD_PAD), row, False)] * 2)


def _swap_lanes(x):
    lane = lax.broadcasted_iota(jnp.int32, x.shape, x.ndim - 1)
    half = QK_ROPE // 2
    first = (lane >= QK_NOPE) & (lane < QK_NOPE + half)
    second = (lane >= QK_NOPE + half) & (lane < QK_DIM)
    n = x.shape[-1]
    return jnp.where(first, pltpu.roll(x, n - half, x.ndim - 1), jnp.where(second, pltpu.roll(x, half, x.ndim - 1), 0.0))


@jax.custom_vjp
def _swap_halves(x):
    return _swap_lanes(x)


_swap_halves.defvjp(lambda x: (_swap_lanes(x), None), lambda _, g: (_swap_lanes(g),))


def _rope(xn, cos, sin_signed):
    return xn * cos + _swap_halves(xn) * sin_signed


def _krope_fn(kr, w, cos, sin_signed, seg):
    _, rope = _lane_masks(kr.shape)
    xn = jnp.where(rope, kr * _segment_rstd(kr, seg) * w, 0.0)
    return _rope(xn, cos, sin_signed)


def _head_fn(q, kv, kr, cos, sin_signed, qn, kn, seg):
    nope, rope = _lane_masks(q.shape)
    qp = _rope(jnp.where(nope | rope, q * _segment_rstd(q, seg) * qn, 0.0), cos, sin_signed)
    kp = jnp.where(nope, kv * _segment_rstd(kv, seg) * kn, 0.0) + kr
    vp = jnp.where(nope, 0.0, kv)
    return qp, kp, vp


def _heads_fwd(q_raw, kv_raw, kr, cos, sin_signed, qn, kn, seg):
    nh, t, _ = q_raw.shape
    tm = _row_tile(t, 2048)
    hblk = ((None, tm, HEAD_PAD), lambda i, h: (h, i, 0))
    tblk = ((tm, HEAD_PAD), lambda i, h: (i, 0))
    par = ((1, HEAD_PAD), lambda i, h: (0, 0))
    sw = ((HEAD_PAD, HEAD_PAD), lambda i, h: (0, 0))
    def fn(*tiles):
        qp, kp, vp = _head_fn(*tiles)
        return qp * Q_PRESCALE, kp, vp

    return _ew("mla_heads_fwd", (t // tm, nh), fn,
               [(q_raw,) + hblk, (kv_raw,) + hblk, (kr,) + tblk, (cos,) + tblk, (sin_signed,) + tblk,
                (qn,) + par, (kn,) + par, (seg,) + sw],
               [(SDS((nh, t, HEAD_PAD), BF),) + hblk + (False,)] * 3, vmem=VMEM_BIG)


def _heads_bwd(q_raw, kv_raw, kr, cos, sin_signed, qn, kn, seg, dqp, dkp, dvp):
    nh, t, _ = q_raw.shape
    tm = _row_tile(t, 1024)
    hblk = ((None, tm, HEAD_PAD), lambda i, h: (h, i, 0))
    tblk = ((tm, HEAD_PAD), lambda i, h: (i, 0))
    par = ((1, HEAD_PAD), lambda i, h: (0, 0))
    sw = ((HEAD_PAD, HEAD_PAD), lambda i, h: (0, 0))

    def body(q, kv, kr, cos, sn, qn, kn, seg, dqp, dkp, dvp, dq, dkv, dkr, dqn, dkn):
        f = lambda q, kv, kr, qn, kn: _head_fn(q, kv, kr, cos[...], sn[...], qn, kn, seg[...])
        _, vjp = jax.vjp(f, q[...], kv[...], kr[...], qn[...], kn[...])
        g = vjp((dqp[...].astype(F32), dkp[...].astype(F32), dvp[...].astype(F32)))
        dq[...] = g[0].astype(dq.dtype)
        dkv[...] = g[1].astype(dkv.dtype)
        h0 = pl.program_id(1) == 0
        first = h0 & (pl.program_id(0) == 0)
        for o, v, c in ((dkr, g[2], h0), (dqn, g[3], first), (dkn, g[4], first)):
            @pl.when(c)
            def _(o=o, v=v):
                o[...] = v

            @pl.when(jnp.logical_not(c))
            def _(o=o, v=v):
                o[...] += v

    spec = lambda b: pl.BlockSpec(*b)
    return pl.pallas_call(
        body, name="mla_heads_bwd", grid=(t // tm, nh),
        in_specs=[spec(hblk), spec(hblk), spec(tblk), spec(tblk), spec(tblk), spec(par), spec(par), spec(sw),
                  spec(hblk), spec(hblk), spec(hblk)],
        out_specs=[spec(hblk), spec(hblk), spec(tblk), spec(par), spec(par)],
        out_shape=[SDS((nh, t, HEAD_PAD), BF), SDS((nh, t, HEAD_PAD), BF), SDS((t, HEAD_PAD), F32),
                   SDS((1, HEAD_PAD), F32), SDS((1, HEAD_PAD), F32)],
        compiler_params=_cparams(2, VMEM_BIG),
    )(q_raw, kv_raw, kr, cos, sin_signed, qn, kn, seg, dqp, dkp, dvp)


ATT_TILE = 512
ATT_SCALE = QK_DIM ** -0.5
LOG2E = 1.4426950408889634
LN2 = 0.6931471805599453
Q_PRESCALE = ATT_SCALE * LOG2E


def _flash_fwd(qs, k, v, nb, seq, hosts=None):
    nh, t, dh = qs.shape
    tq = _row_tile(seq, ATT_TILE)
    nq = seq // tq
    name = "mla_flash_fwd"
    comm = hosts.comm(name) if hosts is not None else None
    n_ci = len(comm.arrays) if comm is not None else 0
    n_co = len(comm.out_shapes) if comm is not None else 0

    def body(*refs):
        q_ref, k_ref, v_ref = refs[:3]
        ci_refs = refs[3:3 + n_ci]
        o_ref, lse_ref = refs[3 + n_ci:5 + n_ci]
        co_refs = refs[5 + n_ci:5 + n_ci + n_co]
        sem_refs = refs[5 + n_ci + n_co:]
        ids = (pl.program_id(0), pl.program_id(1), pl.program_id(2))
        if comm is not None:
            @pl.when((ids[0] == 0) & (ids[1] == 0) & (ids[2] == 0))
            def _():
                comm.start(ci_refs, co_refs, sem_refs)

        attend(q_ref, k_ref, v_ref, o_ref, lse_ref)

        if comm is not None:
            @pl.when((ids[0] == nh - 1) & (ids[1] == nb - 1) & (ids[2] == nq - 1))
            def _():
                comm.wait(ci_refs, co_refs, sem_refs)

    def attend(q_ref, k_ref, v_ref, o_ref, lse_ref):
        qi = pl.program_id(2)
        qt = q_ref[...]

        def tile(j, carry, diagonal):
            m, l, acc = carry
            rows = pl.ds(pl.multiple_of(j * tq, tq), tq)
            s = _dotf(qt, k_ref[rows, :], NT)
            if diagonal:
                r = lax.broadcasted_iota(jnp.int32, (tq, tq), 0)
                c = lax.broadcasted_iota(jnp.int32, (tq, tq), 1)
                s = jnp.where(c <= r, s, -jnp.inf)
            m_new = jnp.maximum(m, jnp.max(s, axis=-1, keepdims=True))
            alpha = jnp.exp2(m - m_new)
            p = jnp.exp2(s - m_new)
            return m_new, alpha * l + jnp.sum(p, axis=-1, keepdims=True), alpha * acc + _dotf(p, v_ref[rows, :], NN)

        init = (jnp.full((tq, 1), -jnp.inf, F32), jnp.zeros((tq, 1), F32), jnp.zeros((tq, dh), F32))
        carry = lax.fori_loop(0, qi, lambda j, c: tile(j, c, False), init)
        m, l, acc = tile(qi, carry, True)
        o_ref[...] = (acc / l).astype(o_ref.dtype)
        lse_ref[...] = m + jnp.log2(l)

    qblk = pl.BlockSpec((None, tq, dh), lambda h, b, i: (h, b * nq + i, 0))
    kblk = pl.BlockSpec((None, seq, dh), lambda h, b, i: (h, b, 0))
    hbm = pl.BlockSpec(memory_space=pl.ANY)
    res = pl.pallas_call(
        body, name=name, grid=(nh, nb, nq), in_specs=[qblk, kblk, kblk] + [hbm] * n_ci,
        out_specs=[qblk, pl.BlockSpec((None, tq, 1), lambda h, b, i: (h, b * nq + i, 0))] + [hbm] * n_co,
        out_shape=[SDS((nh, t, dh), BF), SDS((nh, t, 1), F32)] + (list(comm.out_shapes) if comm is not None else []),
        scratch_shapes=comm.sems if comm is not None else [],
        compiler_params=_cparams(3, VMEM_BIG),
    )(qs, k, v, *(comm.arrays if comm is not None else []))
    if comm is not None:
        hosts.done(name, res[2:])
    return res[0], res[1]


def _flash_bwd(qs, k, v, o, lse, do, nb, seq):
    nh, t, dh = qs.shape
    tq = _row_tile(seq, ATT_TILE)
    nq = seq // tq

    def row_of(col):
        return jnp.broadcast_to(col, (tq, LANES)).T[0:1, :]

    def body(q_ref, k_ref, v_ref, o_ref, lse_ref, do_ref, dq_ref, dk_ref, dv_ref, kt_sc, lrow_sc, drow_sc, dqt_sc):
        for c in range(nq):
            rows = pl.ds(c * tq, tq)
            kt_sc[c] = k_ref[rows, :].T
            delta = jnp.sum(do_ref[rows, :].astype(F32) * o_ref[rows, :].astype(F32), axis=-1, keepdims=True)
            drow_sc[c] = row_of(delta)
            lrow_sc[c] = row_of(lse_ref[rows, :])
        dqt_sc[...] = jnp.zeros_like(dqt_sc)

        def kv_step(j, _):
            rows_j = pl.ds(pl.multiple_of(j * tq, tq), tq)
            ks, vs, kt = k_ref[rows_j, :], v_ref[rows_j, :], kt_sc[j]

            def q_tile(i, carry, diagonal):
                dk, dv = carry
                rows_i = pl.ds(pl.multiple_of(i * tq, tq), tq)
                qt, dot_ = q_ref[rows_i, :], do_ref[rows_i, :]
                pt = jnp.exp2(_dotf(ks, qt, NT) - lrow_sc[i])
                if diagonal:
                    kk = lax.broadcasted_iota(jnp.int32, (tq, tq), 0)
                    qq = lax.broadcasted_iota(jnp.int32, (tq, tq), 1)
                    pt = jnp.where(kk <= qq, pt, 0.0)
                dst = (pt * (_dotf(vs, dot_, NT) - drow_sc[i])).astype(BF)
                dqt_sc[i] += _dotf(kt, dst, NN)
                return dk + _dotf(dst, qt, NN), dv + _dotf(pt, dot_, NN)

            zero = jnp.zeros((tq, dh), F32)
            carry = q_tile(j, (zero, zero), True)
            dk, dv = lax.fori_loop(j + 1, nq, lambda i, c: q_tile(i, c, False), carry)
            dk_ref[rows_j, :] = dk * LN2
            dv_ref[rows_j, :] = dv
            return 0

        lax.fori_loop(0, nq, kv_step, 0)
        for c in range(nq):
            dq_ref[pl.ds(c * tq, tq), :] = dqt_sc[c].T * ATT_SCALE

    full = pl.BlockSpec((None, seq, dh), lambda h, b: (h, b, 0))
    sfull = pl.BlockSpec((None, seq, 1), lambda h, b: (h, b, 0))
    return pl.pallas_call(
        body, name="mla_flash_bwd", grid=(nh, nb), in_specs=[full, full, full, full, sfull, full],
        out_specs=[full, full, full], out_shape=[SDS((nh, t, dh), F32)] * 3,
        scratch_shapes=[pltpu.VMEM((nq, dh, tq), BF), pltpu.VMEM((nq, 1, tq), F32), pltpu.VMEM((nq, 1, tq), F32),
                        pltpu.VMEM((nq, dh, tq), F32)],
        compiler_params=_cparams(2, VMEM_BIG),
    )(qs, k, v, o, lse, do)


def _heads_nt(name, a, wt, out_dtype):
    t, kdim = a.shape
    tm = _row_tile(t, 512)
    nw = MLA_HEADS * HEAD_PAD

    def body(a_ref, w_ref, o_ref):
        r = _dotf(a_ref[...], w_ref[...], NT)
        for h in range(MLA_HEADS):
            o_ref[h] = r[:, HEAD_PAD * h:HEAD_PAD * (h + 1)].astype(o_ref.dtype)

    return pl.pallas_call(
        body, name=name, grid=(t // tm,),
        in_specs=[pl.BlockSpec((tm, kdim), lambda i: (i, 0)), pl.BlockSpec((nw, kdim), lambda i: (0, 0))],
        out_specs=pl.BlockSpec((MLA_HEADS, tm, HEAD_PAD), lambda i: (0, i, 0)),
        out_shape=SDS((MLA_HEADS, t, HEAD_PAD), out_dtype), compiler_params=_cparams(1, VMEM_BIG),
    )(a, wt)


def _all_heads(a_ref):
    return jnp.concatenate([a_ref[h] for h in range(MLA_HEADS)], axis=1)


def _heads_nn(name, a, w, n, out_dtype, res=None):
    t = a.shape[1]
    tm = _row_tile(t, 512)
    nw = MLA_HEADS * HEAD_PAD

    def body(*refs):
        a_ref, w_ref, o_ref = refs[0], refs[1], refs[-1]
        r = _dotf(_all_heads(a_ref), w_ref[...], NN)
        if res is not None:
            r = r + refs[2][...]
        o_ref[...] = r.astype(o_ref.dtype)

    row = pl.BlockSpec((tm, n), lambda i: (i, 0))
    in_specs = [pl.BlockSpec((MLA_HEADS, tm, HEAD_PAD), lambda i: (0, i, 0)), pl.BlockSpec((nw, n), lambda i: (0, 0))]
    args = [a, w]
    if res is not None:
        in_specs.append(row)
        args.append(res)
    return pl.pallas_call(body, name=name, grid=(t // tm,), in_specs=in_specs, out_specs=row,
                          out_shape=SDS((t, n), out_dtype), compiler_params=_cparams(1, VMEM_BIG))(*args)


def _heads_wgrad(name, a, b, n):
    t = b.shape[0]
    tk = _row_tile(t, 512)
    nw = MLA_HEADS * HEAD_PAD
    steps = t // tk

    def body(a_ref, b_ref, o_ref, acc):
        k = pl.program_id(0)

        @pl.when(k == 0)
        def _():
            acc[...] = jnp.zeros_like(acc)

        acc[...] += _dotf(_all_heads(a_ref), b_ref[...], TN)

        @pl.when(k == steps - 1)
        def _():
            o_ref[...] = acc[...]

    return pl.pallas_call(
        body, name=name, grid=(steps,),
        in_specs=[pl.BlockSpec((MLA_HEADS, tk, HEAD_PAD), lambda k: (0, k, 0)), pl.BlockSpec((tk, n), lambda k: (k, 0))],
        out_specs=pl.BlockSpec((nw, n), lambda k: (0, 0)), out_shape=SDS((nw, n), F32),
        scratch_shapes=[pltpu.VMEM((nw, n), F32)], compiler_params=_cparams(1, VMEM_BIG),
    )(a, b)


PM_CKV, PM_KR, PM_CQ = 0, KV_LORA, KV_LORA + HEAD_PAD
PM_DIM = KV_LORA + HEAD_PAD + Q_LORA


def _lat_specs(t, tm):
    return (((tm, KV_LORA), lambda i: (i, 0)), ((tm, HEAD_PAD), lambda i: (i, PM_KR // HEAD_PAD)),
            ((tm, Q_LORA), lambda i: (i, PM_CQ // Q_LORA)))


def _mla_fwd(x, nw, wm, small, tables, nb, seq, hosts=None):
    t, d = x.shape
    cos, sin_signed, seg = tables
    h = _rms_fwd("mla_rms", x, nw)
    pm = _proj_nt("mla_in", h, wm["in_t"], PM_DIM, F32, tn=PM_DIM // 3)
    tm = _row_tile(t, 512)
    ckv_s, kr_s, cq_s = _lat_specs(t, tm)
    row = lambda i: (i, 0)
    par = lambda n: ((1, n), lambda i: (0, 0))
    ckvn = _ew("mla_ckv_norm", (t // tm,), _rms_fn, [(pm,) + ckv_s, (small["kv_a_norm"],) + par(KV_LORA)],
               [(SDS((t, KV_LORA), BF), (tm, KV_LORA), row, False)])[0]
    cqn = _ew("mla_cq_norm", (t // tm,), _rms_fn, [(pm,) + cq_s, (small["q_a_norm"],) + par(Q_LORA)],
              [(SDS((t, Q_LORA), BF), (tm, Q_LORA), row, False)])[0]
    tb = ((tm, HEAD_PAD), row)
    kr = _ew("mla_krope", (t // tm,), _krope_fn,
             [(pm,) + kr_s, (small["k_norm"],) + par(HEAD_PAD), (cos,) + tb, (sin_signed,) + tb,
              (seg, (HEAD_PAD, HEAD_PAD), lambda i: (0, 0))],
             [(SDS((t, HEAD_PAD), F32),) + tb + (False,)])[0]
    q_raw = _heads_nt("mla_q_b", cqn, wm["qb_t"], F32)
    kv_raw = _heads_nt("mla_kv_b", ckvn, wm["kvb_t"], F32)
    qp, kp, vp = _heads_fwd(q_raw, kv_raw, kr, cos, sin_signed, small["q_norm"], small["k_norm"], seg)
    o, lse = _flash_fwd(qp, kp, vp, nb, seq, hosts=hosts)
    out = _heads_nn("mla_out", o, wm["out"], d, F32, res=x)
    return out, (x, h, pm, ckvn, cqn, kr, q_raw, kv_raw, qp, kp, vp, o, lse)


def _mla_bwd(dy, nw, wm, small, tables, saved, nb, seq):
    x, h, pm, ckvn, cqn, kr, q_raw, kv_raw, qp, kp, vp, o, lse = saved
    t, d = x.shape
    cos, sin_signed, seg = tables
    do = _heads_nt("mla_bwd_do", dy, wm["out"], BF)
    g_out = _heads_wgrad("mla_w_out_g", o, dy, d)
    dqp, dkp, dvp = _flash_bwd(qp, kp, vp, o, lse, do, nb, seq)
    dq_raw, dkv_raw, dkr, d_qn, d_kn = _heads_bwd(q_raw, kv_raw, kr, cos, sin_signed, small["q_norm"],
                                                   small["k_norm"], seg, dqp, dkp, dvp)
    dcqn = _heads_nn("mla_bwd_dcq", dq_raw, wm["qb_t"], Q_LORA, F32)
    dckvn = _heads_nn("mla_bwd_dckv", dkv_raw, wm["kvb_t"], KV_LORA, F32)
    g_qb = _heads_wgrad("mla_w_qb_g", dq_raw, cqn, Q_LORA)
    g_kvb = _heads_wgrad("mla_w_kvb_g", dkv_raw, ckvn, KV_LORA)
    tm = _row_tile(t, 512)
    ckv_s, kr_s, cq_s = _lat_specs(t, tm)
    row = lambda i: (i, 0)
    par = lambda n: ((1, n), lambda i: (0, 0))

    def rms_b(xv, w, dv):
        _, vjp = jax.vjp(_rms_fn, xv, w)
        return vjp(dv)

    dckv, d_kva = _ew("mla_ckv_norm_bwd", (t // tm,), rms_b,
                      [(pm,) + ckv_s, (small["kv_a_norm"],) + par(KV_LORA), (dckvn, (tm, KV_LORA), row)],
                      [(SDS((t, KV_LORA), BF), (tm, KV_LORA), row, False),
                       (SDS((1, KV_LORA), F32),) + par(KV_LORA) + (True,)], acc_axes=(0,))
    dcq, d_qa = _ew("mla_cq_norm_bwd", (t // tm,), rms_b,
                    [(pm,) + cq_s, (small["q_a_norm"],) + par(Q_LORA), (dcqn, (tm, Q_LORA), row)],
                    [(SDS((t, Q_LORA), BF), (tm, Q_LORA), row, False),
                     (SDS((1, Q_LORA), F32),) + par(Q_LORA) + (True,)], acc_axes=(0,))
    tb = ((tm, HEAD_PAD), row)

    def kr_b(krv, w, cosv, sinv, sw, dv):
        _, vjp = jax.vjp(lambda a, b: _krope_fn(a, b, cosv, sinv, sw), krv, w)
        return vjp(dv)

    dkr_raw, d_kn2 = _ew("mla_krope_bwd", (t // tm,), kr_b,
                         [(pm,) + kr_s, (small["k_norm"],) + par(HEAD_PAD), (cos,) + tb, (sin_signed,) + tb,
                          (seg, (HEAD_PAD, HEAD_PAD), lambda i: (0, 0)), (dkr,) + tb],
                         [(SDS((t, HEAD_PAD), BF),) + tb + (False,),
                          (SDS((1, HEAD_PAD), F32),) + par(HEAD_PAD) + (True,)], acc_axes=(0,))
    dh = _seg_nn("mla_bwd_dh", [(dckv, wm["in_t"], PM_CKV), (dkr_raw, wm["in_t"], PM_KR),
                                (dcq, wm["in_t"], PM_CQ)], d, BF, tk=128)
    dx, dnw = _rms_bwd("mla_bwd_rms", x, nw, dh, dy)
    g = {
        "in_ckv_t": _wgrad("mla_w_in_ckv_g", dckv, h, KV_LORA, d, tm=KV_LORA, tn=d, out_dtype=F32),
        "in_kr_t": _wgrad("mla_w_in_kr_g", dkr_raw, h, HEAD_PAD, d, tm=HEAD_PAD, tn=d, out_dtype=F32),
        "in_cq_t": _wgrad("mla_w_in_cq_g", dcq, h, Q_LORA, d, tm=Q_LORA, tn=d, out_dtype=F32),
        "qb_t": g_qb, "kvb_t": g_kvb, "out": g_out,
        "q_a_norm": d_qa, "kv_a_norm": d_kva, "q_norm": d_qn, "k_norm": d_kn + d_kn2,
    }
    return dx, dnw, g


def _mesh_pos():
    return lax.axis_index("x"), lax.axis_index("y"), lax.axis_index("c")


def _peer(pos, k):
    x, y, c = pos
    return (x ^ ((k >> 2) & 1), y ^ ((k >> 1) & 1), c ^ (k & 1))


def _flat(pos):
    return 4 * pos[0] + 2 * pos[1] + pos[2]


def _slab(ref, axis, start, size):
    idx = [slice(None)] * axis + [pl.ds(start, size)]
    return ref.at[tuple(idx)]


class _Exchange:
    def __init__(self, kind, items):
        self.kind = kind
        self.axes = [ax for _, ax in items]
        self.arrays = [a for a, _ in items]
        n = len(items)
        self.out_shapes = []
        self.sizes = []
        for a, ax in items:
            shp = list(a.shape)
            if kind == "gather":
                self.sizes.append(shp[ax])
                shp[ax] *= N_DEV
                self.out_shapes.append(SDS(tuple(shp), a.dtype))
            else:
                shp[ax] //= N_DEV
                self.sizes.append(shp[ax])
                self.out_shapes.append(SDS((N_DEV,) + tuple(shp), a.dtype))
        self.sems = [pltpu.SemaphoreType.DMA((n, N_DEV - 1)), pltpu.SemaphoreType.DMA((n, N_DEV - 1)),
                     pltpu.SemaphoreType.DMA((n,))]

    def _copies(self, srcs, dsts, sems, with_arrivals=True):
        send_sems, recv_sems, local_sems = sems
        pos = _mesh_pos()
        me = _flat(pos)
        local, sends, recvs = [], [], []
        for t, (src, dst) in enumerate(zip(srcs, dsts)):
            ax, sz = self.axes[t], self.sizes[t]
            if self.kind == "gather":
                mine = _slab(dst, ax, me * sz, sz)
                local.append(pltpu.make_async_copy(src, mine, local_sems.at[t]))
            else:
                mine = dst.at[me]
                local.append(pltpu.make_async_copy(_slab(src, ax, me * sz, sz), mine, local_sems.at[t]))
            for k in range(1, N_DEV):
                peer = _peer(pos, k)
                there = _flat(peer)
                if self.kind == "gather":
                    out_src, landing = src, _slab(dst, ax, there * sz, sz)
                else:
                    out_src, landing = _slab(src, ax, there * sz, sz), dst.at[there]
                common = dict(send_sem=send_sems.at[t, k - 1], recv_sem=recv_sems.at[t, k - 1], device_id=peer,
                              device_id_type=pl.DeviceIdType.MESH)
                sends.append(pltpu.make_async_remote_copy(src_ref=out_src, dst_ref=mine, **common))
                if with_arrivals:
                    recvs.append(pltpu.make_async_remote_copy(src_ref=out_src, dst_ref=landing, **common))
        return local, sends, recvs

    def start(self, srcs, dsts, sems):
        local, sends, _ = self._copies(srcs, dsts, sems, with_arrivals=False)
        for cp in local + sends:
            cp.start()

    def wait(self, srcs, dsts, sems):
        local, sends, recvs = self._copies(srcs, dsts, sems)
        for rc in recvs:
            rc.wait_recv()
        for rc in sends:
            rc.wait_send()
        for cp in local:
            cp.wait()

    def run(self, name):
        n = len(self.arrays)

        def body(*refs):
            srcs, dsts, sems = refs[:n], refs[n:2 * n], refs[2 * n:]
            self.start(srcs, dsts, sems)
            self.wait(srcs, dsts, sems)

        hbm = pl.BlockSpec(memory_space=pl.ANY)
        return pl.pallas_call(body, name=name, in_specs=[hbm] * n, out_specs=[hbm] * n, out_shape=self.out_shapes,
                              scratch_shapes=self.sems)(*self.arrays)


def _adam_math(w, g, m, v):
    m = ADAM_B1 * m + (1.0 - ADAM_B1) * g
    v = ADAM_B2 * v + (1.0 - ADAM_B2) * (g * g)
    m_hat = m / (1.0 - ADAM_B1 ** ADAM_STEP)
    v_hat = v / (1.0 - ADAM_B2 ** ADAM_STEP)
    delta = -ADAM_LR * (m_hat / (jnp.sqrt(v_hat) + ADAM_EPS) + ADAM_WD * w)
    return delta, m, v


def _adam(name, land, land_blk, land_idx, w, m, v, transposed, ck):
    n, r, c = w.shape
    wblk = ((None, ck, c), lambda a, i: (a, i, 0))

    def fn(parts, w, m, v):
        g = parts[0].astype(F32)
        for s in range(1, N_DEV):
            g = g + parts[s].astype(F32)
        if transposed:
            g = g.T
        delta, m2, v2 = _adam_math(w, g, m, v)
        return g, delta, m2, v2

    return _ew(name, (n, r // ck), fn,
               [(land, land_blk, land_idx), (w,) + wblk, (m,) + wblk, (v,) + wblk],
               [(SDS(w.shape, F32),) + wblk + (False,)] * 4, vmem=VMEM_BIG)


def _prep_ffn(gate, up, down):
    def body(g, u, dn, o):
        o[0] = g[...].T.astype(BF)
        o[1] = u[...].T.astype(BF)
        o[2] = dn[...].astype(BF)

    cblk = pl.BlockSpec((None, None, D_MODEL, FF_SHARD), lambda l, i: (l, i, 0, 0))
    rblk = pl.BlockSpec((None, None, FF_SHARD, D_MODEL), lambda l, i: (l, i, 0, 0))
    return pl.pallas_call(
        body, name="prep_ffn", grid=(2, 2), in_specs=[cblk, cblk, rblk],
        out_specs=pl.BlockSpec((3, FF_SHARD, D_MODEL), lambda l, i: (2 * l + i, 0, 0)),
        out_shape=SDS((12, FF_SHARD, D_MODEL), BF), compiler_params=_cparams(2, VMEM_BIG),
    )(gate, up, down)


def _transpose_cast(name, w, dtype):
    def body(a, o):
        o[...] = a[...].T.astype(dtype)

    r, c = w.shape
    return pl.pallas_call(body, name=name, out_shape=SDS((c, r), dtype),
                          compiler_params=pltpu.CompilerParams(vmem_limit_bytes=VMEM_BIG))(w)


SMALL_SHARDED = (("norm_w", 6 * 128), ("conv_w", CONV_K * 512), ("q_a_norm", 48), ("kv_a_norm", 32))
SMALL_PACK = 3072


def _dyn(a, start, size):
    return lax.dynamic_slice_in_dim(a, start, size, axis=a.ndim - 1)


def _layout_ssm(ssm_in_t, ssm_out_all):
    d = ssm_in_t.shape[1]
    dt_rows = ssm_in_t[D_INNER + CONV_DIM:].reshape(SSM_GROUPS, SSM_HPG, d)
    return {"z_t": ssm_in_t[:D_INNER], "xbc_t": ssm_in_t[D_INNER:D_INNER + CONV_DIM],
            "dt_t": jnp.pad(dt_rows, ((0, 0), (0, LANES - SSM_HPG), (0, 0))).reshape(SSM_GROUPS * LANES, d),
            "out": ssm_out_all}


def _layout_mla(mla_in_all, qb_all, kvb_all, mla_out_all):
    d = mla_out_all.shape[1]
    in_t = mla_in_all.T
    kr_rows = jnp.pad(in_t[Q_LORA + KV_LORA:], ((QK_NOPE, HEAD_PAD - QK_DIM), (0, 0)))
    qb_heads = jnp.pad(qb_all.reshape(MLA_HEADS, QK_DIM, Q_LORA), ((0, 0), (0, HEAD_PAD - QK_DIM), (0, 0)))
    out_heads = jnp.pad(mla_out_all.reshape(MLA_HEADS, 64, d), ((0, 0), (64, 0), (0, 0)))
    return {"in_t": jnp.concatenate([in_t[Q_LORA:Q_LORA + KV_LORA], kr_rows, in_t[:Q_LORA]], axis=0),
            "qb_t": qb_heads.reshape(MLA_HEADS * HEAD_PAD, Q_LORA), "kvb_t": kvb_all,
            "out": out_heads.reshape(MLA_HEADS * HEAD_PAD, d)}


def _layout_small(conv_w, conv_b, dt_bias, a_log, d_skip, ssm_norm_w, q_a_norm, kv_a_norm, q_norm, k_norm):
    lane_heads = lambda p: jnp.pad(p.reshape(SSM_GROUPS, SSM_HPG), ((0, 0), (0, LANES - SSM_HPG))).reshape(1, -1)
    pad_head = lambda p: jnp.pad(p.reshape(1, QK_DIM), ((0, 0), (0, HEAD_PAD - QK_DIM)))
    return {"conv_w": conv_w, "conv_b": conv_b, "dt_bias": lane_heads(dt_bias), "a_log": lane_heads(a_log),
            "d_skip": lane_heads(d_skip), "ssm_norm_w": ssm_norm_w, "q_a_norm": q_a_norm, "kv_a_norm": kv_a_norm,
            "q_norm": pad_head(q_norm), "k_norm": pad_head(k_norm)}


class _Plan:
    def __init__(self, ctx):
        self.ctx = ctx
        self.make = {}
        self.land = {}

    def ride(self, host, make, land):
        assert host not in self.make, host
        self.make[host] = make
        self.land[host] = land

    def comm(self, host):
        return self.make[host](self.ctx) if host in self.make else None

    def done(self, host, results):
        self.land[host](results, self.ctx)


def _local_step(x, positions, loss_target, ctx, plan=None):
    nb, seq, d = x.shape
    t = nb * seq
    xf = x.reshape(t, d)
    norm = ctx["norm"]
    tables = list(_rope_tables(positions.reshape(t, 1))) + [_segment_matrix()]
    x1, s_f0 = _ffn_fwd("ffn0", xf, norm[0, 0], ctx["ffn0"], plan)
    x2, s_ssm = _ssm_fwd(x1, norm[0, 1], ctx["ws"], ctx["small"], nb, seq, plan)
    x3, s_f1 = _ffn_fwd("ffn1", x2, norm[0, 2], ctx["ffn1"], plan)
    x4, s_f2 = _ffn_fwd("ffn2", x3, norm[1, 0], ctx["ffn2"], plan)
    x5, s_mla = _mla_fwd(x4, norm[1, 1], ctx["wm"], ctx["small"], tables, nb, seq, plan)
    x6, s_f3 = _ffn_fwd("ffn3", x5, norm[1, 2], ctx["ffn3"], plan)
    dy, loss_cols = _loss_and_grad(x6, loss_target.reshape(t, d))

    dx5, dn12, ctx["g_ffn3"] = _ffn_bwd("ffn3", dy, norm[1, 2], ctx["ffn3"], s_f3, plan)
    dx4, dn11, ctx["g_mla"] = _mla_bwd(dx5, norm[1, 1], ctx["wm"], ctx["small"], tables, s_mla, nb, seq)
    dx3, dn10, ctx["g_ffn2"] = _ffn_bwd("ffn2", dx4, norm[1, 0], ctx["ffn2"], s_f2, plan)
    dx2, dn02, ctx["g_ffn1"] = _ffn_bwd("ffn1", dx3, norm[0, 2], ctx["ffn1"], s_f1, plan)
    dx1, dn01, ctx["g_ssm"] = _ssm_bwd(dx2, norm[0, 1], ctx["ws"], ctx["small"], s_ssm, nb, seq, plan)
    dx0, dn00, ctx["g_ffn0"] = _ffn_bwd("ffn0", dx1, norm[0, 0], ctx["ffn0"], s_f0, plan)
    return loss_cols, dx0.reshape(nb, seq, d), (dn00, dn01, dn02, dn10, dn11, dn12)


def kernel(x, positions, norm_w, ffn_w_gate, ffn_w_up, ffn_w_down, ssm_w_in, ssm_conv_w, ssm_conv_b, ssm_dt_bias, ssm_a_log, ssm_d, ssm_norm_w, ssm_w_out, mla_w_in, mla_q_a_norm, mla_kv_a_norm, mla_w_q_b, mla_w_kv_b, mla_q_norm, mla_k_norm, mla_w_out, loss_target, m_norm_w, m_ffn_w_gate, m_ffn_w_up, m_ffn_w_down, m_ssm_w_in, m_ssm_conv_w, m_ssm_conv_b, m_ssm_dt_bias, m_ssm_a_log, m_ssm_d, m_ssm_norm_w, m_ssm_w_out, m_mla_w_in, m_mla_q_a_norm, m_mla_kv_a_norm, m_mla_w_q_b, m_mla_w_kv_b, m_mla_q_norm, m_mla_k_norm, m_mla_w_out, v_norm_w, v_ffn_w_gate, v_ffn_w_up, v_ffn_w_down, v_ssm_w_in, v_ssm_conv_w, v_ssm_conv_b, v_ssm_dt_bias, v_ssm_a_log, v_ssm_d, v_ssm_norm_w, v_ssm_w_out, v_mla_w_in, v_mla_q_a_norm, v_mla_kv_a_norm, v_mla_w_q_b, v_mla_w_kv_b, v_mla_q_norm, v_mla_k_norm, v_mla_w_out):
    nb, seq, d = x.shape
    t = nb * seq
    me = _flat(_mesh_pos())

    ffn_loc = _prep_ffn(ffn_w_gate, ffn_w_up, ffn_w_down)
    ssm_in_loc = _transpose_cast("prep_ssm_in", ssm_w_in[0], BF)
    ssm_out_loc = ssm_w_out[0].astype(BF)
    mla_in_loc, mla_out_loc = mla_w_in[0].astype(BF), mla_w_out[0].astype(BF)
    qb_loc = _transpose_cast("prep_q_b", mla_w_q_b[0], BF)
    kvb_loc = _transpose_cast("prep_kv_b", mla_w_kv_b[0], BF)
    small_loc = jnp.concatenate([norm_w.reshape(-1), ssm_conv_w.reshape(-1), mla_q_a_norm.reshape(-1),
                                 mla_kv_a_norm.reshape(-1)])
    small_loc = jnp.pad(small_loc, (0, SMALL_PACK - small_loc.shape[0])).reshape(SMALL_PACK // LANES, LANES)

    wloc = lambda n: [(ffn_loc[3 * n + k], 0) for k in range(3)]
    g0, u0, small_all = _Exchange("gather", wloc(0)[0:2] + [(small_loc, 0)]).run("gather_first")
    sm = small_all.reshape(N_DEV, SMALL_PACK)
    conv_w_full = sm[:, 768:768 + 2048].reshape(N_DEV, CONV_K, 512).transpose(1, 0, 2).reshape(CONV_K, CONV_DIM)
    ctx = {"ffn0": [g0, u0, None], "ffn1": [None] * 3, "ffn2": [None] * 3,
           "norm": sm[:, :768].reshape(N_DEV, 6, 128).transpose(1, 0, 2).reshape(2, 3, 1, d),
           "small": _layout_small(conv_w_full, ssm_conv_b, ssm_dt_bias, ssm_a_log, ssm_d, ssm_norm_w,
                                  sm[:, 2816:2864].reshape(1, Q_LORA), sm[:, 2864:2896].reshape(1, KV_LORA),
                                  mla_q_norm, mla_k_norm)}
    plan = _Plan(ctx)

    def gather_on(host, items, land):
        plan.ride(host, lambda c: _Exchange("gather", items), land)

    def put_w(key, ks):
        def land(r, c):
            for k, arr in zip(ks, r):
                c[key][k] = arr
        return land

    half_rows = SSM_IN_SHARD // 2
    ssm_in_a, ssm_in_b = (ssm_in_loc[k * half_rows:(k + 1) * half_rows].reshape(half_rows // 2, 16, LANES)
                          for k in range(2))

    def land_first(r, c):
        c["ffn0"][2] = r[0]
        c["ssm_in_a"] = r[1].reshape(N_DEV, half_rows, d)

    def land_ssm_in(r, c):
        both = jnp.concatenate([c["ssm_in_a"], r[0].reshape(N_DEV, half_rows, d)], axis=1)
        c["ws"] = _layout_ssm(both.reshape(SSM_IN_DIM, d), None)

    def land_ssm_out(r, c):
        c["ffn1"][1] = r[0]
        c["ws"]["out"] = r[1]

    gather_on("ffn0_up", wloc(0)[2:3] + [(ssm_in_a, 0)], land_first)
    gather_on("ffn0_down", [(ssm_in_b, 0)], land_ssm_in)
    gather_on("ssm_in_z", wloc(1)[0:1], put_w("ffn1", (0,)))
    gather_on("ssm_in_xbc", wloc(1)[1:2] + [(ssm_out_loc, 0)], land_ssm_out)
    gather_on("ssm_out", wloc(1)[2:3], put_w("ffn1", (2,)))
    gather_on("ffn1_up", wloc(2)[0:2], put_w("ffn2", (0, 1)))
    gather_on("ffn1_down", wloc(2)[2:3], put_w("ffn2", (2,)))
    gather_on("ffn2_up", [(mla_in_loc, 0), (qb_loc, 0), (kvb_loc, 0), (mla_out_loc, 0)],
              lambda r, c: c.update(wm=_layout_mla(r[0], r[1], r[2], r[3])))
    gather_on("mla_flash_fwd", wloc(3), lambda r, c: c.update(ffn3=tuple(r)))

    heads_of = lambda a: a.reshape(SSM_GROUPS, LANES, -1)[:, :SSM_HPG].reshape(SSM_HEADS, -1)

    def mla_grad_items(c):
        g = c["g_mla"]
        g_in = jnp.concatenate([g["in_cq_t"], g["in_ckv_t"], g["in_kr_t"][QK_NOPE:QK_DIM]], axis=0).T
        g_qb = g["qb_t"].reshape(MLA_HEADS, HEAD_PAD, Q_LORA)[:, :QK_DIM].reshape(MLA_HEADS * QK_DIM, Q_LORA)
        g_out = g["out"].reshape(MLA_HEADS, HEAD_PAD, d)[:, 64:].reshape(MLA_HEADS * 64, d)
        return [(a.astype(BF), 0) for a in (g_out, g_in, g_qb, g["kvb_t"])]

    def ssm_in_grad(which):
        def items(c):
            if "g_ssm_in_t" not in c:
                g = c["g_ssm"]
                g_in_t = jnp.concatenate([g["z_t"], g["x_t"], g["b_t"], g["c_t"], heads_of(g["dt_t"])], axis=0)
                c["g_ssm_in_t"] = g_in_t.astype(BF).reshape(N_DEV, SSM_IN_SHARD, d)
            part = c["g_ssm_in_t"][:, which * half_rows:(which + 1) * half_rows]
            return [(part.reshape(N_DEV * half_rows // 2, 16, LANES), 0)]
        return items

    def scatter_on(host, items_of, keys):
        plan.ride(host, lambda c: _Exchange("scatter", items_of(c)),
                  lambda r, c: c.update(dict(zip(keys, r))))

    of = lambda key, k: (lambda c: [(c[key][k], 0)])
    scatter_on("ffn2_bwd_act", of("g_ffn3", 0), ("l3_gate",))
    scatter_on("ffn2_bwd_dh", of("g_ffn3", 1), ("l3_up",))
    scatter_on("ffn2_wg", of("g_ffn3", 2), ("l3_down",))
    scatter_on("ffn2_wu", mla_grad_items, ("l_mla_out", "l_mla_in", "l_qb", "l_kvb"))
    scatter_on("ffn1_bwd_act", of("g_ffn2", 0), ("l2_gate",))
    scatter_on("ffn1_bwd_dh", of("g_ffn2", 1), ("l2_up",))
    scatter_on("ffn1_wg", of("g_ffn2", 2), ("l2_down",))
    scatter_on("ssm_bwd_dgn", of("g_ffn1", 0), ("l1_gate",))
    scatter_on("ssm_bwd_dh", lambda c: [(c["g_ffn1"][1], 0), (c["g_ffn1"][2], 0)], ("l1_up", "l1_down"))
    scatter_on("ffn0_bwd_act", ssm_in_grad(0), ("l_ssm_in_a",))
    scatter_on("ffn0_bwd_dh", ssm_in_grad(1), ("l_ssm_in_b",))
    scatter_on("ffn0_wg", lambda c: [(c["g_ssm"]["out"], 0)], ("l_ssm_out",))
    scatter_on("ffn0_wu", lambda c: [(c["g_ffn0_gate"], 0)], ("l0_gate",))
    scatter_on("ffn0_wd", lambda c: [(c["g_ffn0_up"], 0)], ("l0_up",))

    loss_cols, grad_x, dns = _local_step(x, positions, loss_target, ctx, plan)
    loss = lax.psum(jnp.sum(loss_cols), ("x", "y", "c"))
    dn00, dn01, dn02, dn10, dn11, dn12 = dns
    g_ssm, g_mla = ctx["g_ssm"], ctx["g_mla"]
    ctx["l0_down"] = _Exchange("scatter", [(ctx["g_ffn0"][2], 0)]).run("scatter_last")[0]
    l_ffn = {k: jnp.stack([ctx["l%d_%s" % (n, k)] for n in range(4)], axis=1) for k in ("gate", "up", "down")}
    l_mla_out, l_mla_in, l_qb, l_kvb = (ctx[k] for k in ("l_mla_out", "l_mla_in", "l_qb", "l_kvb"))
    l_ssm_in = jnp.concatenate([ctx[k].reshape(N_DEV, half_rows, d) for k in ("l_ssm_in_a", "l_ssm_in_b")], axis=1)
    l_ssm_out = ctx["l_ssm_out"]

    unlane = lambda a: a.reshape(SSM_GROUPS, LANES)[:, :SSM_HPG].reshape(1, SSM_HEADS)
    small_g = jnp.concatenate([
        jnp.concatenate([dn00, dn01, dn02, dn10, dn11, dn12], axis=0).reshape(-1),
        g_ssm["conv_w"].reshape(-1), g_ssm["conv_b"].reshape(-1), unlane(g_ssm["dt_bias"]).reshape(-1),
        unlane(g_ssm["a_log"]).reshape(-1), unlane(g_ssm["d_skip"]).reshape(-1), g_ssm["ssm_norm_w"].reshape(-1),
        g_mla["q_a_norm"].reshape(-1), g_mla["kv_a_norm"].reshape(-1), g_mla["q_norm"][0, :QK_DIM],
        g_mla["k_norm"][0, :QK_DIM]])
    n_small = small_g.shape[0]
    n_small_pad = -(-n_small // (8 * LANES)) * (8 * LANES)
    small_g = jnp.pad(small_g, (0, n_small_pad - n_small)).reshape(n_small_pad // LANES, LANES)
    gs = _Exchange("gather", [(small_g, 0)]).run("gather_small_grads")[0].reshape(N_DEV, n_small_pad)

    outs = {}

    def put(name, res, shape):
        for key, val in zip(("grad", "delta", "new_m", "new_v"), res):
            outs[(key, name)] = val.reshape(shape)

    ck = 256
    for key, name, w, m, v in (("gate", "ffn_w_gate", ffn_w_gate, m_ffn_w_gate, v_ffn_w_gate),
                               ("up", "ffn_w_up", ffn_w_up, m_ffn_w_up, v_ffn_w_up)):
        res = _adam("adam_" + name, l_ffn[key], (N_DEV, None, FF_SHARD, ck), lambda a, i: (0, a, 0, i),
                    w.reshape(4, d, FF_SHARD), m.reshape(4, d, FF_SHARD), v.reshape(4, d, FF_SHARD), True, ck)
        put(name, res, w.shape)
    res = _adam("adam_ffn_w_down", l_ffn["down"], (N_DEV, None, 176, d), lambda a, i: (0, a, i, 0),
                ffn_w_down.reshape(4, FF_SHARD, d), m_ffn_w_down.reshape(4, FF_SHARD, d),
                v_ffn_w_down.reshape(4, FF_SHARD, d), False, 176)
    put("ffn_w_down", res, ffn_w_down.shape)
    l_ssm_in2 = l_ssm_in.reshape(N_DEV, SSM_IN_SHARD, d)
    res = _adam("adam_ssm_w_in", l_ssm_in2, (N_DEV, SSM_IN_SHARD, 128), lambda a, i: (0, 0, i),
                ssm_w_in, m_ssm_w_in, v_ssm_w_in, True, 128)
    put("ssm_w_in", res, ssm_w_in.shape)
    res = _adam("adam_ssm_w_out", l_ssm_out, (N_DEV, 128, d), lambda a, i: (0, i, 0),
                ssm_w_out, m_ssm_w_out, v_ssm_w_out, False, 128)
    put("ssm_w_out", res, ssm_w_out.shape)
    res = _adam("adam_mla_w_in", l_mla_in, (N_DEV, 128, MLA_IN_DIM), lambda a, i: (0, 0, 0),
                mla_w_in, m_mla_w_in, v_mla_w_in, False, 128)
    put("mla_w_in", res, mla_w_in.shape)
    res = _adam("adam_mla_w_q_b", l_qb, (N_DEV, 192, 128), lambda a, i: (0, 0, i),
                mla_w_q_b, m_mla_w_q_b, v_mla_w_q_b, True, 128)
    put("mla_w_q_b", res, mla_w_q_b.shape)
    res = _adam("adam_mla_w_kv_b", l_kvb, (N_DEV, 256, 128), lambda a, i: (0, 0, i),
                mla_w_kv_b, m_mla_w_kv_b, v_mla_w_kv_b, True, 128)
    put("mla_w_kv_b", res, mla_w_kv_b.shape)
    res = _adam("adam_mla_w_out", l_mla_out, (N_DEV, 128, d), lambda a, i: (0, 0, 0),
                mla_w_out, m_mla_w_out, v_mla_w_out, False, 128)
    put("mla_w_out", res, mla_w_out.shape)

    small_params = (
        ("norm_w", norm_w, m_norm_w, v_norm_w, 6 * d, 6, 128), ("ssm_conv_w", ssm_conv_w, m_ssm_conv_w, v_ssm_conv_w,
                                                               CONV_K * CONV_DIM, CONV_K, 512),
        ("ssm_conv_b", ssm_conv_b, m_ssm_conv_b, v_ssm_conv_b, CONV_DIM, 0, 0),
        ("ssm_dt_bias", ssm_dt_bias, m_ssm_dt_bias, v_ssm_dt_bias, SSM_HEADS, 0, 0),
        ("ssm_a_log", ssm_a_log, m_ssm_a_log, v_ssm_a_log, SSM_HEADS, 0, 0),
        ("ssm_d", ssm_d, m_ssm_d, v_ssm_d, SSM_HEADS, 0, 0),
        ("ssm_norm_w", ssm_norm_w, m_ssm_norm_w, v_ssm_norm_w, D_INNER, 0, 0),
        ("mla_q_a_norm", mla_q_a_norm, m_mla_q_a_norm, v_mla_q_a_norm, Q_LORA, 1, 48),
        ("mla_kv_a_norm", mla_kv_a_norm, m_mla_kv_a_norm, v_mla_kv_a_norm, KV_LORA, 1, 32),
        ("mla_q_norm", mla_q_norm, m_mla_q_norm, v_mla_q_norm, QK_DIM, 0, 0),
        ("mla_k_norm", mla_k_norm, m_mla_k_norm, v_mla_k_norm, QK_DIM, 0, 0),
    )
    parts, ws_, ms_, vs_, off = [], [], [], [], 0
    for name, w, m, v, full, rows, shard in small_params:
        seg = gs[:, off:off + full]
        if rows:
            seg = _dyn(seg.reshape(N_DEV, rows, full // rows), me * shard, shard).reshape(N_DEV, rows * shard)
        parts.append(seg)
        ws_.append(w.reshape(1, -1))
        ms_.append(m.reshape(1, -1))
        vs_.append(v.reshape(1, -1))
        off += full
    n_loc = sum(p.shape[1] for p in parts)
    n_loc_pad = -(-n_loc // LANES) * LANES
    padc = lambda a, val=0.0: jnp.pad(jnp.concatenate(a, axis=1), ((0, 0), (0, n_loc_pad - n_loc)),
                                      constant_values=val)
    res = _adam("adam_small", padc(parts).reshape(N_DEV, 1, n_loc_pad), (N_DEV, 1, n_loc_pad), lambda a, i: (0, 0, 0),
                padc(ws_).reshape(1, 1, n_loc_pad), padc(ms_).reshape(1, 1, n_loc_pad),
                padc(vs_, 1.0).reshape(1, 1, n_loc_pad), False, 1)
    off = 0
    for name, w, m, v, full, rows, shard in small_params:
        nloc = w.size
        put(name, [r.reshape(-1)[off:off + nloc] for r in res], w.shape)
        off += nloc

    order = ("norm_w", "ffn_w_gate", "ffn_w_up", "ffn_w_down", "ssm_w_in", "ssm_conv_w", "ssm_conv_b", "ssm_dt_bias",
             "ssm_a_log", "ssm_d", "ssm_norm_w", "ssm_w_out", "mla_w_in", "mla_q_a_norm", "mla_kv_a_norm",
             "mla_w_q_b", "mla_w_kv_b", "mla_q_norm", "mla_k_norm", "mla_w_out")
    return (loss, grad_x, *[outs[(k, n)] for k in ("grad", "delta", "new_m", "new_v") for n in order])
```

```python
import functools
import math

import jax
import jax.numpy as jnp
import numpy as np
from jax import lax
from jax.experimental import pallas as pl
from jax.experimental.pallas import tpu as pltpu

F32 = jnp.float32
BF = jnp.bfloat16
SDS = jax.ShapeDtypeStruct

N_DEV = 8
D_MODEL = 1024
D_FF = 2816
FF_SHARD = D_FF // N_DEV
D_INNER = 2048
SSM_HEADS = 32
SSM_GROUPS = 8
SSM_HPG = 4
SSM_STATE = 128
CONV_K = 4
CONV_DIM = 4096
SSM_IN_DIM = 6176
SSM_IN_SHARD = SSM_IN_DIM // N_DEV
NORM_GROUP = 256
CHUNK = 128
MLA_HEADS = 16
Q_LORA = 384
KV_LORA = 256
QK_NOPE = 64
QK_ROPE = 32
QK_DIM = 96
MLA_IN_DIM = 672
HEAD_PAD = 128
ROPE_THETA = 10000.0
EPS = 1e-6
LANES = 128

ADAM_LR = 0.001
ADAM_B1 = 0.9
ADAM_B2 = 0.999
ADAM_EPS = 1e-08
ADAM_WD = 0.01
ADAM_STEP = 10

VMEM_BIG = 56 * 1024 * 1024

NN = ((1,), (0,))
NT = ((1,), (1,))
TN = ((0,), (0,))


def _dotf(a, b, dn):
    return lax.dot_general(a.astype(BF), b.astype(BF), (dn, ((), ())), preferred_element_type=F32)


def _dot_hi(a, b, dn=NN):
    return lax.dot_general(a, b, (dn, ((), ())), precision=lax.Precision.HIGHEST, preferred_element_type=F32)


def _sigmoid(x):
    return jax.nn.sigmoid(x)


def _silu(x):
    return x * _sigmoid(x)


def _softplus(x):
    return jnp.maximum(x, 0.0) + jnp.log(1.0 + jnp.exp(-jnp.abs(x)))


def _cparams(n_grid, vmem=None):
    return pltpu.CompilerParams(dimension_semantics=("arbitrary",) * n_grid, vmem_limit_bytes=vmem)


def _fmm(name, grid_mn, pairs, outs, *, epi=None, extras=(), n_acc=1, acc_shape=None, vmem=None, alias=None,
         joint=False, hosts=None, row_split=1):
    comm = hosts.comm(name) if hosts is not None else None
    if joint:
        nk_total = pairs[0][7]
        assert all(p[7] == nk_total for p in pairs)
        starts = [0] * len(pairs)
    else:
        nk_total = sum(p[7] for p in pairs)
        starts = []
        s = 0
        for p in pairs:
            starts.append(s)
            s += p[7]
    n_pairs, n_extras, n_outs = len(pairs), len(extras), len(outs)
    single = nk_total == 1

    n_ci = len(comm.arrays) if comm is not None else 0
    n_co = len(comm.out_shapes) if comm is not None else 0
    n_scratch_acc = 0 if single else n_acc

    def body(*refs):
        ab_refs = refs[: 2 * n_pairs]
        e_refs = refs[2 * n_pairs: 2 * n_pairs + n_extras]
        pos = 2 * n_pairs + n_extras + (1 if alias is not None else 0)
        ci_refs = refs[pos: pos + n_ci]
        pos += n_ci
        o_refs = refs[pos: pos + n_outs]
        co_refs = refs[pos + n_outs: pos + n_outs + n_co]
        pos += n_outs + n_co
        acc_refs = refs[pos: pos + n_scratch_acc]
        sem_refs = refs[pos + n_scratch_acc:]
        i, j, k = pl.program_id(0), pl.program_id(1), pl.program_id(2)

        if comm is not None:
            @pl.when((i == 0) & (j == 0) & (k == 0))
            def _():
                comm.start(ci_refs, co_refs, sem_refs)

        compute(ab_refs, e_refs, o_refs, acc_refs, i, j, k)

        if comm is not None:
            @pl.when((i == grid_mn[0] - 1) & (j == grid_mn[1] - 1) & (k == nk_total - 1))
            def _():
                comm.wait(ci_refs, co_refs, sem_refs)

    def compute(ab_refs, e_refs, o_refs, acc_refs, i, j, k):

        def finish(accs, rows=slice(None)):
            res = epi(accs, *[e[rows] for e in e_refs]) if epi is not None else accs
            if not isinstance(res, (tuple, list)):
                res = (res,)
            first = (i == 0) & (j == 0)
            for o, r, spec in zip(o_refs, res, outs):
                if spec[3]:
                    @pl.when(first)
                    def _(o=o, r=r):
                        o[...] = r.astype(o.dtype)

                    @pl.when(jnp.logical_not(first))
                    def _(o=o, r=r):
                        o[...] += r.astype(o.dtype)
                else:
                    o[rows] = r.astype(o.dtype)

        if single:
            tm_all = ab_refs[0].shape[0]
            ch = tm_all // row_split
            for c in range(row_split):
                rows = slice(c * ch, (c + 1) * ch) if row_split > 1 else slice(None)
                accs = [None] * n_acc
                for p, pr in enumerate(pairs):
                    d = _dotf(ab_refs[2 * p][rows], ab_refs[2 * p + 1][...], pr[6])
                    accs[pr[8]] = d if accs[pr[8]] is None else accs[pr[8]] + d
                finish(accs, rows)
            return

        @pl.when(k == 0)
        def _():
            for a in acc_refs:
                a[...] = jnp.zeros_like(a)

        for p, pr in enumerate(pairs):
            def step(p=p, pr=pr):
                acc_refs[pr[8]][...] += _dotf(ab_refs[2 * p][...], ab_refs[2 * p + 1][...], pr[6])

            if n_pairs == 1 or joint:
                step()
            else:
                pl.when((k >= starts[p]) & (k < starts[p] + pr[7]))(step)

        @pl.when(k == nk_total - 1)
        def _():
            finish([a[...] for a in acc_refs])

    in_specs, args = [], []
    for p, pr in enumerate(pairs):
        a, a_blk, a_idx, b, b_blk, b_idx, _, nk, _ = pr
        st = starts[p]

        def amap(i, j, k, a_idx=a_idx, st=st, nk=nk):
            return a_idx(i, j, jnp.clip(k - st, 0, nk - 1))

        def bmap(i, j, k, b_idx=b_idx, st=st, nk=nk):
            return b_idx(i, j, jnp.clip(k - st, 0, nk - 1))

        in_specs += [pl.BlockSpec(a_blk, amap), pl.BlockSpec(b_blk, bmap)]
        args += [a, b]
    for arr, blk, idx in extras:
        in_specs.append(pl.BlockSpec(blk, lambda i, j, k, idx=idx: idx(i, j)))
        args.append(arr)
    io_alias = {}
    if alias is not None:
        in_specs.append(pl.BlockSpec(memory_space=pl.ANY))
        io_alias = {len(args): 0}
        args.append(alias)
    out_specs = [pl.BlockSpec(blk, lambda i, j, k, idx=idx: idx(i, j)) for _, blk, idx, _ in outs]
    out_shape = [o[0] for o in outs]
    scratch = [] if single else [pltpu.VMEM(acc_shape, F32) for _ in range(n_acc)]
    if comm is not None:
        hbm = pl.BlockSpec(memory_space=pl.ANY)
        in_specs += [hbm] * n_ci
        args += list(comm.arrays)
        out_specs += [hbm] * n_co
        out_shape += list(comm.out_shapes)
        scratch += comm.sems
    res = pl.pallas_call(
        body, name=name, grid=(grid_mn[0], grid_mn[1], nk_total), in_specs=in_specs, out_specs=out_specs,
        out_shape=out_shape, scratch_shapes=scratch, input_output_aliases=io_alias,
        compiler_params=_cparams(3, vmem),
    )(*args)
    if comm is not None:
        hosts.done(name, res[n_outs:])
    return res[:n_outs]


def _ew(name, grid, fn, ins, outs, *, acc_axes=(), vmem=None):
    n_in = len(ins)

    def body(*refs):
        res = fn(*[r[...] for r in refs[:n_in]])
        if not isinstance(res, (tuple, list)):
            res = (res,)
        first = None
        for ax in acc_axes:
            c = pl.program_id(ax) == 0
            first = c if first is None else (first & c)
        for o, r, spec in zip(refs[n_in:], res, outs):
            if spec[3]:
                @pl.when(first)
                def _(o=o, r=r):
                    o[...] = r.astype(o.dtype)

                @pl.when(jnp.logical_not(first))
                def _(o=o, r=r):
                    o[...] += r.astype(o.dtype)
            else:
                o[...] = r.astype(o.dtype)

    return pl.pallas_call(
        body, name=name, grid=grid,
        in_specs=[pl.BlockSpec(blk, idx) for _, blk, idx in ins],
        out_specs=[pl.BlockSpec(blk, idx) for _, blk, idx, _ in outs],
        out_shape=[o[0] for o in outs],
        compiler_params=_cparams(len(grid), vmem),
    )(*[a for a, _, _ in ins])


def _row_tile(t, want):
    tm = min(want, t)
    assert t % tm == 0, (t, tm)
    return tm


def _rms_fn(x, w):
    return x * lax.rsqrt(jnp.mean(x * x, axis=-1, keepdims=True) + EPS) * w


def _rms_fwd(name, x, w):
    t, d = x.shape
    tm = _row_tile(t, 512)
    return _ew(name, (t // tm,), _rms_fn,
               [(x, (tm, d), lambda i: (i, 0)), (w, (1, d), lambda i: (0, 0))],
               [(SDS((t, d), BF), (tm, d), lambda i: (i, 0), False)])[0]


def _rms_bwd(name, x, w, dh, dres):
    t, d = x.shape
    tm = _row_tile(t, 512)

    def fn(x, w, dh, dres):
        _, vjp = jax.vjp(_rms_fn, x, w)
        dx, dw = vjp(dh.astype(F32))
        return dx + dres, dw

    row = lambda i: (i, 0)
    return _ew(name, (t // tm,), fn,
               [(x, (tm, d), row), (w, (1, d), lambda i: (0, 0)), (dh, (tm, d), row), (dres, (tm, d), row)],
               [(SDS((t, d), F32), (tm, d), row, False), (SDS((1, d), F32), (1, d), lambda i: (0, 0), True)],
               acc_axes=(0,))


def _rms_bwd_epi(accs, xv, wv, dres):
    _, vjp = jax.vjp(_rms_fn, xv, wv)
    dx, dw = vjp(accs[0])
    return dx + dres, dw


def _rms_bwd_tail(x, w, dres, tm):
    t, d = x.shape
    row = lambda i, j: (i, 0)
    const = lambda i, j: (0, 0)
    extras = [(x, (tm, d), row), (w, (1, d), const), (dres, (tm, d), row)]
    outs = [(SDS((t, d), F32), (tm, d), row, False), (SDS((1, d), F32), (1, d), const, True)]
    return extras, outs


def _loss_and_grad(y, target):
    t, d = y.shape
    tm = _row_tile(t, 512)

    def fn(y, tg):
        e = y - tg
        return e * (1.0 / d), jnp.sum(e * e, axis=0, keepdims=True) * (0.5 / d)

    row = lambda i: (i, 0)
    return _ew("loss_head", (t // tm,), fn, [(y, (tm, d), row), (target, (tm, d), row)],
               [(SDS((t, d), F32), (tm, d), row, False), (SDS((1, d), F32), (1, d), lambda i: (0, 0), True)],
               acc_axes=(0,))


def _ffn_fwd(tag, x, nw, wf, hosts=None):
    gate_t, up_t = wf[0], wf[1]
    t, d = x.shape
    h = _rms_fwd(tag + "_rms", x, nw)
    tm, tn = _row_tile(t, 256), D_FF

    def epi(accs):
        g, u = accs
        s = _sigmoid(g)
        sg = g * s
        return 0.5 * sg, 0.5 * (s * (1.0 + g * (1.0 - s))), u, sg * u

    hblk = (h, (tm, d), lambda i, j, k: (i, 0))
    col = lambda i, j: (i, j)
    tblk = lambda w: (w, (tn, d), lambda i, j, k: (j, 0), NT, 1)
    sgh, dsgh, u, a = _fmm(
        tag + "_up", (t // tm, D_FF // tn), [hblk + tblk(gate_t) + (0,), hblk + tblk(up_t) + (1,)],
        [(SDS((t, D_FF), BF), (tm, tn), col, False)] * 4, epi=epi, n_acc=2, joint=True, hosts=hosts, vmem=VMEM_BIG)
    down = wf[2]
    tm2 = _row_tile(t, 512)
    y = _fmm(
        tag + "_down", (t // tm2, 1),
        [(a, (tm2, D_FF), lambda i, j, k: (i, 0), down, (D_FF, d), lambda i, j, k: (0, 0), NN, 1, 0)],
        [(SDS((t, d), F32), (tm2, d), lambda i, j: (i, 0), False)],
        extras=[(x, (tm2, d), lambda i, j: (i, 0))],
        epi=lambda accs, xr: xr + 0.5 * accs[0], vmem=VMEM_BIG, hosts=hosts)[0]
    return y, (x, h, sgh, dsgh, u, a)


def _wgrad(name, a, b, m, n, *, tm, tn, tk=2048, scale=None, out_dtype=BF, hosts=None):
    t = a.shape[0]
    tk = _row_tile(t, tk)
    epi = (lambda accs: accs[0] * scale) if scale is not None else None
    return _fmm(name, (m // tm, n // tn),
                [(a, (tk, tm), lambda i, j, k: (k, i), b, (tk, tn), lambda i, j, k: (k, j), TN, t // tk, 0)],
                [(SDS((m, n), out_dtype), (tm, tn), lambda i, j: (i, j), False)], epi=epi, acc_shape=(tm, tn),
                vmem=VMEM_BIG, hosts=hosts)[0]


def _ffn_bwd(tag, dy, nw, wf, saved, hosts=None):
    gate_t, up_t, down = wf
    x, h, sgh, dsgh, u, a = saved
    t, d = x.shape
    tm, tn = _row_tile(t, 256), D_FF

    def epi(accs, sgh, dsgh, u):
        da = accs[0]
        return da * (u.astype(F32) * dsgh.astype(F32)), da * sgh.astype(F32)

    col = lambda i, j: (i, j)
    dg, du = _fmm(
        tag + "_bwd_act", (t // tm, D_FF // tn),
        [(dy, (tm, d), lambda i, j, k: (i, 0), down, (tn, d), lambda i, j, k: (j, 0), NT, 1, 0)],
        [(SDS((t, D_FF), BF), (tm, tn), col, False)] * 2,
        extras=[(sgh, (tm, tn), col), (dsgh, (tm, tn), col), (u, (tm, tn), col)], epi=epi, hosts=hosts,
        vmem=VMEM_BIG)
    tm2 = _row_tile(t, 256)
    full = lambda i, j, k: (0, 0)
    extras, outs = _rms_bwd_tail(x, nw, dy, tm2)
    dx, dnw = _fmm(
        tag + "_bwd_dh", (t // tm2, 1),
        [(dg, (tm2, D_FF), lambda i, j, k: (i, 0), gate_t, (D_FF, d), full, NN, 1, 0),
         (du, (tm2, D_FF), lambda i, j, k: (i, 0), up_t, (D_FF, d), full, NN, 1, 0)],
        outs, extras=extras, epi=_rms_bwd_epi, vmem=VMEM_BIG, joint=True, hosts=hosts)
    half = D_FF // 2
    g_gate = _wgrad(tag + "_wg", dg, h, D_FF, d, tm=half, tn=d, hosts=hosts)
    if hosts is not None:
        hosts.ctx["g_" + tag + "_gate"] = g_gate
    g_up = _wgrad(tag + "_wu", du, h, D_FF, d, tm=half, tn=d, hosts=hosts)
    if hosts is not None:
        hosts.ctx["g_" + tag + "_up"] = g_up
    g_down = _wgrad(tag + "_wd", a, dy, D_FF, d, tm=half, tn=d, scale=0.5, hosts=hosts)
    return dx, dnw, (g_gate, g_up, g_down)


def _shift_down(cur, prev8, j):
    rolled = pltpu.roll(cur, j, 0)
    sub = lax.broadcasted_iota(jnp.int32, prev8.shape, 0)
    top = jnp.where(sub < j, pltpu.roll(prev8, j, 0), rolled[:8])
    return jnp.concatenate([top, rolled[8:]], axis=0)


def _shift_up(cur, next8, j):
    n = cur.shape[0]
    rolled = pltpu.roll(cur, n - j, 0)
    sub = lax.broadcasted_iota(jnp.int32, next8.shape, 0)
    bot = jnp.where(sub >= 8 - j, pltpu.roll(next8, 8 - j, 0), rolled[n - 8:])
    return jnp.concatenate([rolled[: n - 8], bot], axis=0)


HALO = 16


def _conv_fwd(xbc, w, b, seq):
    t, c = xbc.shape
    ts, tc = _row_tile(seq, 512), 1024
    tiles_per_seq = seq // ts
    hb = ts // HALO

    def fn(cur, prev, w, b):
        i = pl.program_id(1)
        cur = cur.astype(F32)
        prev8 = jnp.where(i % tiles_per_seq == 0, 0.0, prev.astype(F32)[HALO - 8:])
        out = b + w[3:4] * cur
        for j in range(1, CONV_K):
            out = out + w[3 - j:4 - j] * _shift_down(cur, prev8, j)
        return out, _silu(out)

    return _ew("ssm_conv_fwd", (c // tc, t // ts), fn,
               [(xbc, (ts, tc), lambda j, i: (i, j)),
                (xbc, (HALO, tc), lambda j, i: (jnp.maximum(i * hb - 1, 0), j)),
                (w, (CONV_K, tc), lambda j, i: (0, j)), (b, (1, tc), lambda j, i: (0, j))],
               [(SDS((t, c), BF), (ts, tc), lambda j, i: (i, j), False)] * 2, vmem=VMEM_BIG)


def _conv_bwd(tag, dxa, cpre, xbc, w, col0, seq):
    t, width = dxa.shape
    ts, tc = _row_tile(seq, 512), 1024
    tiles_per_seq = seq // ts
    hb = ts // HALO
    cb0 = col0 // tc
    n_halo_blocks = t // HALO

    def dsilu(cv, dv):
        cv = cv.astype(F32)
        s = _sigmoid(cv)
        return dv.astype(F32) * (s * (1.0 + cv * (1.0 - s)))

    def fn(dxa_c, dxa_n, c_c, c_n, x_c, x_p, w):
        i = pl.program_id(1)
        dc = dsilu(c_c, dxa_c)
        last = i % tiles_per_seq == tiles_per_seq - 1
        dc_n = jnp.where(last, 0.0, dsilu(c_n, dxa_n)[:8])
        dx = w[3:4] * dc
        for j in range(1, CONV_K):
            dx = dx + w[3 - j:4 - j] * _shift_up(dc, dc_n, j)
        cur = x_c.astype(F32)
        prev8 = jnp.where(i % tiles_per_seq == 0, 0.0, x_p.astype(F32)[HALO - 8:])
        rows = [jnp.sum(dc * cur, axis=0, keepdims=True)]
        for j in range(1, CONV_K):
            rows.append(jnp.sum(dc * _shift_down(cur, prev8, j), axis=0, keepdims=True))
        sub8 = lax.broadcasted_iota(jnp.int32, (8, dc.shape[1]), 0)
        dw = jnp.zeros((8, dc.shape[1]), F32)
        for kk in range(CONV_K):
            dw = jnp.where(sub8 == kk, rows[CONV_K - 1 - kk], dw)
        return dx, dw, jnp.sum(dc, axis=0, keepdims=True)

    nxt = lambda j, i: (jnp.minimum((i + 1) * hb, n_halo_blocks - 1), j)
    nxt_off = lambda j, i: (jnp.minimum((i + 1) * hb, n_halo_blocks - 1), j + cb0)
    return _ew(tag, (width // tc, t // ts), fn,
               [(dxa, (ts, tc), lambda j, i: (i, j)), (dxa, (HALO, tc), nxt),
                (cpre, (ts, tc), lambda j, i: (i, j + cb0)), (cpre, (HALO, tc), nxt_off),
                (xbc, (ts, tc), lambda j, i: (i, j + cb0)),
                (xbc, (HALO, tc), lambda j, i: (jnp.maximum(i * hb - 1, 0), j + cb0)),
                (w, (CONV_K, tc), lambda j, i: (0, j + cb0))],
               [(SDS((t, width), BF), (ts, tc), lambda j, i: (i, j), False),
                (SDS((8, width), F32), (8, tc), lambda j, i: (0, j), True),
                (SDS((1, width), F32), (1, tc), lambda j, i: (0, j), True)],
               acc_axes=(1,), vmem=VMEM_BIG)


def _ssd_chunk(xs, bm, cm, dtr, st, dtb, alog, dsk):
    ell = xs.shape[0]
    xs = xs.astype(F32)
    lane = lax.broadcasted_iota(jnp.int32, (ell, LANES), 1)
    sub = lax.broadcasted_iota(jnp.int32, (ell, LANES), 0)
    lane1 = lax.broadcasted_iota(jnp.int32, (1, LANES), 1)
    causal = sub >= lane
    dt = _softplus(dtr + dtb)
    da = dt * (-jnp.exp(alog))
    acs = _dot_hi(causal.astype(F32), da)
    acs_t = acs.T
    cb = _dotf(cm, bm, NT)
    lo = lane < 64
    ys, news = [], []
    for p in range(2):
        xp = xs[:, LANES * p:LANES * (p + 1)]
        sp = st[LANES * p:LANES * (p + 1), :]
        col, dtc, last, dsel = [], [], [], []
        y_diag = None
        for q in range(2):
            r = 2 * p + q
            col_r = jnp.sum(jnp.where(lane == r, acs, 0.0), axis=1, keepdims=True)
            row_r = jnp.sum(jnp.where(sub == r, acs_t, 0.0), axis=0, keepdims=True)
            dtc_r = jnp.sum(jnp.where(lane == r, dt, 0.0), axis=1, keepdims=True)
            decay = jnp.exp(jnp.where(causal, col_r - row_r, -jnp.inf))
            head = lo if q == 0 else jnp.logical_not(lo)
            d = _dotf(cb * decay, jnp.where(head, xp * dtc_r, 0.0), NN)
            y_diag = d if y_diag is None else y_diag + d
            col.append(col_r)
            dtc.append(dtc_r)
            last.append(jnp.sum(jnp.where(sub[:, :1] == ell - 1, col_r, 0.0), axis=0, keepdims=True))
            dsel.append(jnp.sum(jnp.where(lane1 == r, dsk, 0.0), axis=1, keepdims=True))
        y_off = _dotf(cm, sp, NT) * jnp.where(lo, jnp.exp(col[0]), jnp.exp(col[1]))
        xw = jnp.where(lo, xp * (dtc[0] * jnp.exp(last[0] - col[0])), xp * (dtc[1] * jnp.exp(last[1] - col[1])))
        new = sp * jnp.where(sub < 64, jnp.exp(last[0]), jnp.exp(last[1])) + _dotf(xw, bm, TN)
        ys.append(y_diag + y_off + jnp.where(lo, dsel[0], dsel[1]) * xp)
        news.append(new)
    return jnp.concatenate(ys, axis=1), jnp.concatenate(news, axis=0)


SSD_GP = 4
SSD_GP_BWD = 4
XW, GW = 2 * LANES * SSD_GP, LANES * SSD_GP


def _grp(ref, q, width):
    return ref[:, width * q:width * (q + 1)]


def _ssd_fwd(xa, dtr, dtb, alog, dsk, nb, seq):
    t = xa.shape[0]
    nc = seq // CHUNK
    row = lambda g, b, c: (b * nc + c, g)
    par = pl.BlockSpec((1, GW), lambda g, b, c: (0, g))
    b_off, c_off = D_INNER // GW, (D_INNER + SSM_GROUPS * SSM_STATE) // GW

    def body(xs, bm, cm, dtr, dtb, alog, dsk, y_ref, st_out, st_ref):
        @pl.when(pl.program_id(2) == 0)
        def _():
            st_ref[...] = jnp.zeros_like(st_ref)

        ins = [(_grp(xs, q, 2 * LANES), _grp(bm, q, LANES), _grp(cm, q, LANES), _grp(dtr, q, LANES), st_ref[q],
                _grp(dtb, q, LANES), _grp(alog, q, LANES), _grp(dsk, q, LANES)) for q in range(SSD_GP)]
        res = [_ssd_chunk(*a) for a in ins]
        for q in range(SSD_GP):
            st_out[q] = ins[q][4]
            y_ref[:, 2 * LANES * q:2 * LANES * (q + 1)] = res[q][0]
            st_ref[q] = res[q][1]

    specs = [pl.BlockSpec((CHUNK, XW), row),
             pl.BlockSpec((CHUNK, GW), lambda g, b, c: (b * nc + c, b_off + g)),
             pl.BlockSpec((CHUNK, GW), lambda g, b, c: (b * nc + c, c_off + g)),
             pl.BlockSpec((CHUNK, GW), row), par, par, par]
    return pl.pallas_call(
        body, name="ssd_fwd", grid=(SSM_GROUPS // SSD_GP, nb, nc), in_specs=specs,
        out_specs=[pl.BlockSpec((CHUNK, XW), row),
                   pl.BlockSpec((SSD_GP, None, None, 2 * LANES, LANES), lambda g, b, c: (g, b, c, 0, 0))],
        out_shape=[SDS((t, D_INNER), F32), SDS((SSM_GROUPS, nb, nc, 2 * LANES, LANES), F32)],
        scratch_shapes=[pltpu.VMEM((SSD_GP, 2 * LANES, LANES), F32)],
        compiler_params=_cparams(3),
    )(xa, xa, xa, dtr, dtb, alog, dsk)


def _ssd_bwd(xa, dtr, dtb, alog, dsk, states, dy, nb, seq):
    t = xa.shape[0]
    nc = seq // CHUNK
    rev = lambda c: nc - 1 - c
    row = lambda g, b, c: (b * nc + rev(c), g)
    gp = SSD_GP_BWD
    xw, gw = 2 * LANES * gp, LANES * gp
    par = pl.BlockSpec((1, gw), lambda g, b, c: (0, g))
    b_off, c_off = D_INNER // gw, (D_INNER + SSM_GROUPS * SSM_STATE) // gw

    def body(xs, bm, cm, dtr, dtb, alog, dsk, st_in, dy, dxs, dbm, dcm, ddtr, ddtb, dalog, ddsk, dst_ref):
        @pl.when(pl.program_id(2) == 0)
        def _():
            dst_ref[...] = jnp.zeros_like(dst_ref)

        ins = [(_grp(xs, q, 2 * LANES), _grp(bm, q, LANES), _grp(cm, q, LANES), _grp(dtr, q, LANES), st_in[q],
                _grp(dtb, q, LANES), _grp(alog, q, LANES), _grp(dsk, q, LANES)) for q in range(gp)]
        cts = [(_grp(dy, q, 2 * LANES), dst_ref[q]) for q in range(gp)]
        gs = [jax.vjp(_ssd_chunk, *a)[1](ct) for a, ct in zip(ins, cts)]
        for q, g in enumerate(gs):
            dxs[:, 2 * LANES * q:2 * LANES * (q + 1)] = g[0]
            lanes = slice(LANES * q, LANES * (q + 1))
            dbm[:, lanes] = g[1]
            dcm[:, lanes] = g[2]
            ddtr[:, lanes] = g[3]
            dst_ref[q] = g[4]
        first = (pl.program_id(1) == 0) & (pl.program_id(2) == 0)
        for o, k in ((ddtb, 5), (dalog, 6), (ddsk, 7)):
            v = jnp.concatenate([g[k] for g in gs], axis=1)

            @pl.when(first)
            def _(o=o, v=v):
                o[...] = v

            @pl.when(jnp.logical_not(first))
            def _(o=o, v=v):
                o[...] += v

    in_specs = [
        pl.BlockSpec((CHUNK, xw), row),
        pl.BlockSpec((CHUNK, gw), lambda g, b, c: (b * nc + rev(c), b_off + g)),
        pl.BlockSpec((CHUNK, gw), lambda g, b, c: (b * nc + rev(c), c_off + g)),
        pl.BlockSpec((CHUNK, gw), row), par, par, par,
        pl.BlockSpec((gp, None, None, 2 * LANES, LANES), lambda g, b, c: (g, b, rev(c), 0, 0)),
        pl.BlockSpec((CHUNK, xw), row),
    ]
    out_specs = [pl.BlockSpec((CHUNK, xw), row), pl.BlockSpec((CHUNK, gw), row), pl.BlockSpec((CHUNK, gw), row),
                 pl.BlockSpec((CHUNK, gw), row), par, par, par]
    out_shape = [SDS((t, D_INNER), BF), SDS((t, SSM_GROUPS * LANES), BF), SDS((t, SSM_GROUPS * LANES), BF),
                 SDS((t, SSM_GROUPS * LANES), F32)] + [SDS((1, SSM_GROUPS * LANES), F32)] * 3
    return pl.pallas_call(
        body, name="ssd_bwd", grid=(SSM_GROUPS // gp, nb, nc), in_specs=in_specs, out_specs=out_specs,
        out_shape=out_shape, scratch_shapes=[pltpu.VMEM((gp, 2 * LANES, LANES), F32)],
        compiler_params=_cparams(3, VMEM_BIG),
    )(xa, xa, xa, dtr, dtb, alog, dsk, states, dy)


def _gated_fn(y, z, w):
    g = y * _silu(z.astype(F32))
    return g * lax.rsqrt(jnp.mean(g * g, axis=-1, keepdims=True) + EPS) * w


def _gated_norm_fwd(y, z, w):
    t = y.shape[0]
    tm = _row_tile(t, 2048)
    blk = ((tm, NORM_GROUP), lambda g, i: (i, g))
    return _ew("ssm_gnorm_fwd", (SSM_GROUPS, t // tm), _gated_fn,
               [(y,) + blk, (z,) + blk, (w, (1, NORM_GROUP), lambda g, i: (0, g))],
               [(SDS((t, D_INNER), BF),) + blk + (False,)], vmem=VMEM_BIG)[0]


def _gated_norm_bwd(y, z, w, dout):
    t = y.shape[0]
    tm = _row_tile(t, 2048)
    blk = ((tm, NORM_GROUP), lambda g, i: (i, g))
    par = ((1, NORM_GROUP), lambda g, i: (0, g))

    def fn(y, z, w, dout):
        _, vjp = jax.vjp(_gated_fn, y, z, w)
        return vjp(dout.astype(F32))

    return _ew("ssm_gnorm_bwd", (SSM_GROUPS, t // tm), fn,
               [(y,) + blk, (z,) + blk, (w,) + par, (dout,) + blk],
               [(SDS((t, D_INNER), F32),) + blk + (False,), (SDS((t, D_INNER), BF),) + blk + (False,),
                (SDS((1, D_INNER), F32),) + par + (True,)],
               acc_axes=(1,), vmem=VMEM_BIG)


def _proj_nt(name, h, wt, n, out_dtype, tn=1024, hosts=None):
    t, kdim = h.shape
    tm = _row_tile(t, 1024)
    return _fmm(name, (t // tm, n // tn),
                [(h, (tm, kdim), lambda i, j, k: (i, 0), wt, (tn, kdim), lambda i, j, k: (j, 0), NT, 1, 0)],
                [(SDS((t, n), out_dtype), (tm, tn), lambda i, j: (i, j), False)], hosts=hosts, vmem=VMEM_BIG)[0]


def _seg_nn(name, parts, n, out_dtype, tk=256, hosts=None, rms=None):
    t = parts[0][0].shape[0]
    tm = _row_tile(t, 512)
    pairs = []
    for a, w, row0 in parts:
        kp = a.shape[1]
        tkp = min(tk, kp)
        r0 = row0 // tkp
        pairs.append((a, (tm, tkp), lambda i, j, k: (i, k), w, (tkp, n), lambda i, j, k, r0=r0: (k + r0, 0),
                      NN, kp // tkp, 0))
    if rms is not None:
        extras, outs = _rms_bwd_tail(rms[0], rms[1], rms[2], tm)
        return _fmm(name, (t // tm, 1), pairs, outs, extras=extras, epi=_rms_bwd_epi, acc_shape=(tm, n), hosts=hosts,
                    vmem=VMEM_BIG)
    return _fmm(name, (t // tm, 1), pairs, [(SDS((t, n), out_dtype), (tm, n), lambda i, j: (i, 0), False)],
                acc_shape=(tm, n), hosts=hosts)[0]


def _ssm_fwd(x, nw, ws, small, nb, seq, hosts=None):
    t, d = x.shape
    h = _rms_fwd("ssm_rms", x, nw)
    z = _proj_nt("ssm_in_z", h, ws["z_t"], D_INNER, BF, hosts=hosts)
    xbc = _proj_nt("ssm_in_xbc", h, ws["xbc_t"], CONV_DIM, BF, hosts=hosts)
    dtr = _proj_nt("ssm_in_dt", h, ws["dt_t"], SSM_GROUPS * LANES, F32)
    cpre, xa = _conv_fwd(xbc, small["conv_w"], small["conv_b"], seq)
    y, states = _ssd_fwd(xa, dtr, small["dt_bias"], small["a_log"], small["d_skip"], nb, seq)
    gn = _gated_norm_fwd(y, z, small["ssm_norm_w"])
    tm = _row_tile(t, 512)
    out = _fmm("ssm_out", (t // tm, 1),
               [(gn, (tm, D_INNER), lambda i, j, k: (i, 0), ws["out"], (D_INNER, d), lambda i, j, k: (0, 0), NN, 1, 0)],
               [(SDS((t, d), F32), (tm, d), lambda i, j: (i, 0), False)],
               extras=[(x, (tm, d), lambda i, j: (i, 0))], epi=lambda accs, xr: xr + accs[0], hosts=hosts)[0]
    return out, (x, h, z, xbc, dtr, cpre, xa, y, states, gn)


def _ssm_bwd(dy, nw, ws, small, saved, nb, seq, hosts=None):
    x, h, z, xbc, dtr, cpre, xa, y, states, gn = saved
    t, d = x.shape
    dgn = _proj_nt("ssm_bwd_dgn", dy, ws["out"], D_INNER, BF, hosts=hosts)
    d_out = _wgrad("ssm_w_out_g", gn, dy, D_INNER, d, tm=D_INNER // 2, tn=d)
    dyssd, dz, d_normw = _gated_norm_bwd(y, z, small["ssm_norm_w"], dgn)
    dxs, dbm, dcm, ddtr, d_dtb, d_alog, d_dsk = _ssd_bwd(
        xa, dtr, small["dt_bias"], small["a_log"], small["d_skip"], states, dyssd, nb, seq)
    dxbc_x, dcw_x, dcb_x = _conv_bwd("ssm_conv_bwd_x", dxs, cpre, xbc, small["conv_w"], 0, seq)
    dxbc_b, dcw_b, dcb_b = _conv_bwd("ssm_conv_bwd_b", dbm, cpre, xbc, small["conv_w"], D_INNER, seq)
    dxbc_c, dcw_c, dcb_c = _conv_bwd("ssm_conv_bwd_c", dcm, cpre, xbc, small["conv_w"], D_INNER + 1024, seq)
    parts = [(dz, ws["z_t"], 0), (dxbc_x, ws["xbc_t"], 0), (dxbc_b, ws["xbc_t"], D_INNER),
             (dxbc_c, ws["xbc_t"], D_INNER + 1024), (ddtr, ws["dt_t"], 0)]
    dx, dnw = _seg_nn("ssm_bwd_dh", parts, d, BF, tk=1024, hosts=hosts, rms=(x, nw, dy))
    g = {
        "z_t": _wgrad("ssm_w_z_g", dz, h, D_INNER, d, tm=1024, tn=d, out_dtype=F32),
        "x_t": _wgrad("ssm_w_x_g", dxbc_x, h, D_INNER, d, tm=1024, tn=d, out_dtype=F32),
        "b_t": _wgrad("ssm_w_b_g", dxbc_b, h, 1024, d, tm=1024, tn=d, out_dtype=F32),
        "c_t": _wgrad("ssm_w_c_g", dxbc_c, h, 1024, d, tm=1024, tn=d, out_dtype=F32),
        "dt_t": _wgrad("ssm_w_dt_g", ddtr, h, 1024, d, tm=1024, tn=d, out_dtype=F32),
        "out": d_out,
        "conv_w": jnp.concatenate([dcw_x[:CONV_K], dcw_b[:CONV_K], dcw_c[:CONV_K]], axis=1),
        "conv_b": jnp.concatenate([dcb_x, dcb_b, dcb_c], axis=1),
        "dt_bias": d_dtb, "a_log": d_alog, "d_skip": d_dsk, "ssm_norm_w": d_normw,
    }
    return dx, dnw, g


def _lane_masks(shape):
    lane = lax.broadcasted_iota(jnp.int32, shape, len(shape) - 1)
    return lane < QK_NOPE, (lane >= QK_NOPE) & (lane < QK_DIM)


def _segment_matrix():
    p = np.zeros((HEAD_PAD, HEAD_PAD), np.float32)
    p[:QK_NOPE, :QK_NOPE] = 1.0
    p[QK_NOPE:QK_DIM, QK_NOPE:QK_DIM] = 1.0
    return jnp.asarray(p)


def _segment_rstd(x, seg):
    lane = lax.broadcasted_iota(jnp.int32, (1, x.shape[-1]), 1)
    inv_n = jnp.where(lane < QK_NOPE, 1.0 / QK_NOPE, 1.0 / QK_ROPE)
    xx = x * x
    hi = xx.astype(BF)
    lo = (xx - hi.astype(F32)).astype(BF)
    segb = seg.astype(BF)
    ss = _dotf(hi, segb, NN) + _dotf(lo, segb, NN)
    return lax.rsqrt(ss * inv_n + EPS)


def _rope_tables(positions_col):
    t = positions_col.shape[0]
    tm = _row_tile(t, 512)
    freq = np.zeros((1, HEAD_PAD), np.float32)
    inv = 1.0 / (ROPE_THETA ** (np.arange(0, QK_ROPE, 2, dtype=np.float32) / QK_ROPE))
    freq[0, QK_NOPE:QK_NOPE + QK_ROPE // 2] = inv
    freq[0, QK_NOPE + QK_ROPE // 2:QK_DIM] = inv
    sign = np.zeros((1, HEAD_PAD), np.float32)
    sign[0, QK_NOPE:QK_NOPE + QK_ROPE // 2] = -1.0
    sign[0, QK_NOPE + QK_ROPE // 2:QK_DIM] = 1.0

    def fn(pos, freq, sign):
        ang = pos.astype(F32) * freq
        nope, rope = _lane_masks(ang.shape)
        return jnp.where(nope, 1.0, jnp.where(rope, jnp.cos(ang), 0.0)), jnp.sin(ang) * sign

    row = lambda i: (i, 0)
    par = ((1, HEAD_PAD), lambda i: (0, 0))
    return _ew("mla_rope_tables", (t // tm,), fn,
               [(positions_col, (tm, 1), row), (jnp.asarray(freq),) + par, (jnp.asarray(sign),) + par],
               [(SDS((t, HEAD_PAD), F32), (tm, HEAD_PAD), row, False)] * 2)


def _swap_lanes(x):
    lane = lax.broadcasted_iota(jnp.int32, x.shape, x.ndim - 1)
    half = QK_ROPE // 2
    first = (lane >= QK_NOPE) & (lane < QK_NOPE + half)
    second = (lane >= QK_NOPE + half) & (lane < QK_DIM)
    n = x.shape[-1]
    return jnp.where(first, pltpu.roll(x, n - half, x.ndim - 1), jnp.where(second, pltpu.roll(x, half, x.ndim - 1), 0.0))


@jax.custom_vjp
def _swap_halves(x):
    return _swap_lanes(x)


_swap_halves.defvjp(lambda x: (_swap_lanes(x), None), lambda _, g: (_swap_lanes(g),))


def _rope(xn, cos, sin_signed):
    return xn * cos + _swap_halves(xn) * sin_signed


def _krope_fn(kr, w, cos, sin_signed, seg):
    _, rope = _lane_masks(kr.shape)
    xn = jnp.where(rope, kr * _segment_rstd(kr, seg) * w, 0.0)
    return _rope(xn, cos, sin_signed)


def _head_fn(q, kv, kr, cos, sin_signed, qn, kn, seg):
    nope, rope = _lane_masks(q.shape)
    qp = _rope(jnp.where(nope | rope, q * _segment_rstd(q, seg) * qn, 0.0), cos, sin_signed)
    kp = jnp.where(nope, kv * _segment_rstd(kv, seg) * kn, 0.0) + kr
    vp = jnp.where(nope, 0.0, kv)
    return qp, kp, vp


def _heads_fwd(q_raw, kv_raw, kr, cos, sin_signed, qn, kn, seg):
    nh, t, _ = q_raw.shape
    tm = _row_tile(t, 2048)
    hblk = ((None, tm, HEAD_PAD), lambda i, h: (h, i, 0))
    tblk = ((tm, HEAD_PAD), lambda i, h: (i, 0))
    par = ((1, HEAD_PAD), lambda i, h: (0, 0))
    sw = ((HEAD_PAD, HEAD_PAD), lambda i, h: (0, 0))
    def fn(*tiles):
        qp, kp, vp = _head_fn(*tiles)
        return qp * Q_PRESCALE, kp, vp

    return _ew("mla_heads_fwd", (t // tm, nh), fn,
               [(q_raw,) + hblk, (kv_raw,) + hblk, (kr,) + tblk, (cos,) + tblk, (sin_signed,) + tblk,
                (qn,) + par, (kn,) + par, (seg,) + sw],
               [(SDS((nh, t, HEAD_PAD), BF),) + hblk + (False,)] * 3, vmem=VMEM_BIG)


def _heads_bwd(q_raw, kv_raw, kr, cos, sin_signed, qn, kn, seg, dqp, dkp, dvp):
    nh, t, _ = q_raw.shape
    tm = _row_tile(t, 1024)
    hblk = ((None, tm, HEAD_PAD), lambda i, h: (h, i, 0))
    tblk = ((tm, HEAD_PAD), lambda i, h: (i, 0))
    par = ((1, HEAD_PAD), lambda i, h: (0, 0))
    sw = ((HEAD_PAD, HEAD_PAD), lambda i, h: (0, 0))

    def body(q, kv, kr, cos, sn, qn, kn, seg, dqp, dkp, dvp, dq, dkv, dkr, dqn, dkn):
        f = lambda q, kv, kr, qn, kn: _head_fn(q, kv, kr, cos[...], sn[...], qn, kn, seg[...])
        _, vjp = jax.vjp(f, q[...], kv[...], kr[...], qn[...], kn[...])
        g = vjp((dqp[...].astype(F32), dkp[...].astype(F32), dvp[...].astype(F32)))
        dq[...] = g[0].astype(dq.dtype)
        dkv[...] = g[1].astype(dkv.dtype)
        h0 = pl.program_id(1) == 0
        first = h0 & (pl.program_id(0) == 0)
        for o, v, c in ((dkr, g[2], h0), (dqn, g[3], first), (dkn, g[4], first)):
            @pl.when(c)
            def _(o=o, v=v):
                o[...] = v

            @pl.when(jnp.logical_not(c))
            def _(o=o, v=v):
                o[...] += v

    spec = lambda b: pl.BlockSpec(*b)
    return pl.pallas_call(
        body, name="mla_heads_bwd", grid=(t // tm, nh),
        in_specs=[spec(hblk), spec(hblk), spec(tblk), spec(tblk), spec(tblk), spec(par), spec(par), spec(sw),
                  spec(hblk), spec(hblk), spec(hblk)],
        out_specs=[spec(hblk), spec(hblk), spec(tblk), spec(par), spec(par)],
        out_shape=[SDS((nh, t, HEAD_PAD), BF), SDS((nh, t, HEAD_PAD), BF), SDS((t, HEAD_PAD), F32),
                   SDS((1, HEAD_PAD), F32), SDS((1, HEAD_PAD), F32)],
        compiler_params=_cparams(2, VMEM_BIG),
    )(q_raw, kv_raw, kr, cos, sin_signed, qn, kn, seg, dqp, dkp, dvp)


ATT_TILE = 512
ATT_SCALE = QK_DIM ** -0.5
LOG2E = 1.4426950408889634
LN2 = 0.6931471805599453
Q_PRESCALE = ATT_SCALE * LOG2E


def _flash_fwd(qs, k, v, nb, seq, hosts=None):
    nh, t, dh = qs.shape
    tq = _row_tile(seq, ATT_TILE)
    nq = seq // tq
    name = "mla_flash_fwd"
    comm = hosts.comm(name) if hosts is not None else None
    n_ci = len(comm.arrays) if comm is not None else 0
    n_co = len(comm.out_shapes) if comm is not None else 0

    def body(*refs):
        q_ref, k_ref, v_ref = refs[:3]
        ci_refs = refs[3:3 + n_ci]
        o_ref, lse_ref = refs[3 + n_ci:5 + n_ci]
        co_refs = refs[5 + n_ci:5 + n_ci + n_co]
        sem_refs = refs[5 + n_ci + n_co:]
        ids = (pl.program_id(0), pl.program_id(1), pl.program_id(2))
        if comm is not None:
            @pl.when((ids[0] == 0) & (ids[1] == 0) & (ids[2] == 0))
            def _():
                comm.start(ci_refs, co_refs, sem_refs)

        attend(q_ref, k_ref, v_ref, o_ref, lse_ref)

        if comm is not None:
            @pl.when((ids[0] == nh - 1) & (ids[1] == nb - 1) & (ids[2] == nq - 1))
            def _():
                comm.wait(ci_refs, co_refs, sem_refs)

    def attend(q_ref, k_ref, v_ref, o_ref, lse_ref):
        qi = pl.program_id(2)
        qt = q_ref[...]

        def tile(j, carry, diagonal):
            m, l, acc = carry
            rows = pl.ds(pl.multiple_of(j * tq, tq), tq)
            s = _dotf(qt, k_ref[rows, :], NT)
            if diagonal:
                r = lax.broadcasted_iota(jnp.int32, (tq, tq), 0)
                c = lax.broadcasted_iota(jnp.int32, (tq, tq), 1)
                s = jnp.where(c <= r, s, -jnp.inf)
            m_new = jnp.maximum(m, jnp.max(s, axis=-1, keepdims=True))
            alpha = jnp.exp2(m - m_new)
            p = jnp.exp2(s - m_new)
            return m_new, alpha * l + jnp.sum(p, axis=-1, keepdims=True), alpha * acc + _dotf(p, v_ref[rows, :], NN)

        init = (jnp.full((tq, 1), -jnp.inf, F32), jnp.zeros((tq, 1), F32), jnp.zeros((tq, dh), F32))
        carry = lax.fori_loop(0, qi, lambda j, c: tile(j, c, False), init)
        m, l, acc = tile(qi, carry, True)
        o_ref[...] = (acc / l).astype(o_ref.dtype)
        lse_ref[...] = m + jnp.log2(l)

    qblk = pl.BlockSpec((None, tq, dh), lambda h, b, i: (h, b * nq + i, 0))
    kblk = pl.BlockSpec((None, seq, dh), lambda h, b, i: (h, b, 0))
    hbm = pl.BlockSpec(memory_space=pl.ANY)
    res = pl.pallas_call(
        body, name=name, grid=(nh, nb, nq), in_specs=[qblk, kblk, kblk] + [hbm] * n_ci,
        out_specs=[qblk, pl.BlockSpec((None, tq, 1), lambda h, b, i: (h, b * nq + i, 0))] + [hbm] * n_co,
        out_shape=[SDS((nh, t, dh), BF), SDS((nh, t, 1), F32)] + (list(comm.out_shapes) if comm is not None else []),
        scratch_shapes=comm.sems if comm is not None else [],
        compiler_params=_cparams(3, VMEM_BIG),
    )(qs, k, v, *(comm.arrays if comm is not None else []))
    if comm is not None:
        hosts.done(name, res[2:])
    return res[0], res[1]


def _flash_bwd(qs, k, v, o, lse, do, nb, seq):
    nh, t, dh = qs.shape
    tq = _row_tile(seq, ATT_TILE)
    nq = seq // tq

    def row_of(col):
        return jnp.broadcast_to(col, (tq, LANES)).T[0:1, :]

    def body(q_ref, k_ref, v_ref, o_ref, lse_ref, do_ref, dq_ref, dk_ref, dv_ref, kt_sc, lrow_sc, drow_sc, dqt_sc):
        for c in range(nq):
            rows = pl.ds(c * tq, tq)
            kt_sc[c] = k_ref[rows, :].T
            delta = jnp.sum(do_ref[rows, :].astype(F32) * o_ref[rows, :].astype(F32), axis=-1, keepdims=True)
            drow_sc[c] = row_of(delta)
            lrow_sc[c] = row_of(lse_ref[rows, :])
        dqt_sc[...] = jnp.zeros_like(dqt_sc)

        def kv_step(j, _):
            rows_j = pl.ds(pl.multiple_of(j * tq, tq), tq)
            ks, vs, kt = k_ref[rows_j, :], v_ref[rows_j, :], kt_sc[j]

            def q_tile(i, carry, diagonal):
                dk, dv = carry
                rows_i = pl.ds(pl.multiple_of(i * tq, tq), tq)
                qt, dot_ = q_ref[rows_i, :], do_ref[rows_i, :]
                pt = jnp.exp2(_dotf(ks, qt, NT) - lrow_sc[i])
                if diagonal:
                    kk = lax.broadcasted_iota(jnp.int32, (tq, tq), 0)
                    qq = lax.broadcasted_iota(jnp.int32, (tq, tq), 1)
                    pt = jnp.where(kk <= qq, pt, 0.0)
                dst = (pt * (_dotf(vs, dot_, NT) - drow_sc[i])).astype(BF)
                dqt_sc[i] += _dotf(kt, dst, NN)
                return dk + _dotf(dst, qt, NN), dv + _dotf(pt, dot_, NN)

            zero = jnp.zeros((tq, dh), F32)
            carry = q_tile(j, (zero, zero), True)
            dk, dv = lax.fori_loop(j + 1, nq, lambda i, c: q_tile(i, c, False), carry)
            dk_ref[rows_j, :] = dk * LN2
            dv_ref[rows_j, :] = dv
            return 0

        lax.fori_loop(0, nq, kv_step, 0)
        for c in range(nq):
            dq_ref[pl.ds(c * tq, tq), :] = dqt_sc[c].T * ATT_SCALE

    full = pl.BlockSpec((None, seq, dh), lambda h, b: (h, b, 0))
    sfull = pl.BlockSpec((None, seq, 1), lambda h, b: (h, b, 0))
    return pl.pallas_call(
        body, name="mla_flash_bwd", grid=(nh, nb), in_specs=[full, full, full, full, sfull, full],
        out_specs=[full, full, full], out_shape=[SDS((nh, t, dh), F32)] * 3,
        scratch_shapes=[pltpu.VMEM((nq, dh, tq), BF), pltpu.VMEM((nq, 1, tq), F32), pltpu.VMEM((nq, 1, tq), F32),
                        pltpu.VMEM((nq, dh, tq), F32)],
        compiler_params=_cparams(2, VMEM_BIG),
    )(qs, k, v, o, lse, do)


def _heads_nt(name, a, wt, out_dtype):
    t, kdim = a.shape
    tm = _row_tile(t, 512)
    nw = MLA_HEADS * HEAD_PAD

    def body(a_ref, w_ref, o_ref):
        r = _dotf(a_ref[...], w_ref[...], NT)
        for h in range(MLA_HEADS):
            o_ref[h] = r[:, HEAD_PAD * h:HEAD_PAD * (h + 1)].astype(o_ref.dtype)

    return pl.pallas_call(
        body, name=name, grid=(t // tm,),
        in_specs=[pl.BlockSpec((tm, kdim), lambda i: (i, 0)), pl.BlockSpec((nw, kdim), lambda i: (0, 0))],
        out_specs=pl.BlockSpec((MLA_HEADS, tm, HEAD_PAD), lambda i: (0, i, 0)),
        out_shape=SDS((MLA_HEADS, t, HEAD_PAD), out_dtype), compiler_params=_cparams(1, VMEM_BIG),
    )(a, wt)


def _all_heads(a_ref):
    return jnp.concatenate([a_ref[h] for h in range(MLA_HEADS)], axis=1)


def _heads_nn(name, a, w, n, out_dtype, res=None):
    t = a.shape[1]
    tm = _row_tile(t, 512)
    nw = MLA_HEADS * HEAD_PAD

    def body(*refs):
        a_ref, w_ref, o_ref = refs[0], refs[1], refs[-1]
        r = _dotf(_all_heads(a_ref), w_ref[...], NN)
        if res is not None:
            r = r + refs[2][...]
        o_ref[...] = r.astype(o_ref.dtype)

    row = pl.BlockSpec((tm, n), lambda i: (i, 0))
    in_specs = [pl.BlockSpec((MLA_HEADS, tm, HEAD_PAD), lambda i: (0, i, 0)), pl.BlockSpec((nw, n), lambda i: (0, 0))]
    args = [a, w]
    if res is not None:
        in_specs.append(row)
        args.append(res)
    return pl.pallas_call(body, name=name, grid=(t // tm,), in_specs=in_specs, out_specs=row,
                          out_shape=SDS((t, n), out_dtype), compiler_params=_cparams(1, VMEM_BIG))(*args)


def _heads_wgrad(name, a, b, n):
    t = b.shape[0]
    tk = _row_tile(t, 512)
    nw = MLA_HEADS * HEAD_PAD
    steps = t // tk

    def body(a_ref, b_ref, o_ref, acc):
        k = pl.program_id(0)

        @pl.when(k == 0)
        def _():
            acc[...] = jnp.zeros_like(acc)

        acc[...] += _dotf(_all_heads(a_ref), b_ref[...], TN)

        @pl.when(k == steps - 1)
        def _():
            o_ref[...] = acc[...]

    return pl.pallas_call(
        body, name=name, grid=(steps,),
        in_specs=[pl.BlockSpec((MLA_HEADS, tk, HEAD_PAD), lambda k: (0, k, 0)), pl.BlockSpec((tk, n), lambda k: (k, 0))],
        out_specs=pl.BlockSpec((nw, n), lambda k: (0, 0)), out_shape=SDS((nw, n), F32),
        scratch_shapes=[pltpu.VMEM((nw, n), F32)], compiler_params=_cparams(1, VMEM_BIG),
    )(a, b)


PM_CKV, PM_KR, PM_CQ = 0, KV_LORA, KV_LORA + HEAD_PAD
PM_DIM = KV_LORA + HEAD_PAD + Q_LORA


def _lat_specs(t, tm):
    return (((tm, KV_LORA), lambda i: (i, 0)), ((tm, HEAD_PAD), lambda i: (i, PM_KR // HEAD_PAD)),
            ((tm, Q_LORA), lambda i: (i, PM_CQ // Q_LORA)))


def _mla_fwd(x, nw, wm, small, tables, nb, seq, hosts=None):
    t, d = x.shape
    cos, sin_signed, seg = tables
    h = _rms_fwd("mla_rms", x, nw)
    pm = _proj_nt("mla_in", h, wm["in_t"], PM_DIM, F32, tn=PM_DIM // 3)
    tm = _row_tile(t, 512)
    ckv_s, kr_s, cq_s = _lat_specs(t, tm)
    row = lambda i: (i, 0)
    par = lambda n: ((1, n), lambda i: (0, 0))
    ckvn = _ew("mla_ckv_norm", (t // tm,), _rms_fn, [(pm,) + ckv_s, (small["kv_a_norm"],) + par(KV_LORA)],
               [(SDS((t, KV_LORA), BF), (tm, KV_LORA), row, False)])[0]
    cqn = _ew("mla_cq_norm", (t // tm,), _rms_fn, [(pm,) + cq_s, (small["q_a_norm"],) + par(Q_LORA)],
              [(SDS((t, Q_LORA), BF), (tm, Q_LORA), row, False)])[0]
    tb = ((tm, HEAD_PAD), row)
    kr = _ew("mla_krope", (t // tm,), _krope_fn,
             [(pm,) + kr_s, (small["k_norm"],) + par(HEAD_PAD), (cos,) + tb, (sin_signed,) + tb,
              (seg, (HEAD_PAD, HEAD_PAD), lambda i: (0, 0))],
             [(SDS((t, HEAD_PAD), F32),) + tb + (False,)])[0]
    q_raw = _heads_nt("mla_q_b", cqn, wm["qb_t"], F32)
    kv_raw = _heads_nt("mla_kv_b", ckvn, wm["kvb_t"], F32)
    qp, kp, vp = _heads_fwd(q_raw, kv_raw, kr, cos, sin_signed, small["q_norm"], small["k_norm"], seg)
    o, lse = _flash_fwd(qp, kp, vp, nb, seq, hosts=hosts)
    out = _heads_nn("mla_out", o, wm["out"], d, F32, res=x)
    return out, (x, h, pm, ckvn, cqn, kr, q_raw, kv_raw, qp, kp, vp, o, lse)


def _mla_bwd(dy, nw, wm, small, tables, saved, nb, seq):
    x, h, pm, ckvn, cqn, kr, q_raw, kv_raw, qp, kp, vp, o, lse = saved
    t, d = x.shape
    cos, sin_signed, seg = tables
    do = _heads_nt("mla_bwd_do", dy, wm["out"], BF)
    g_out = _heads_wgrad("mla_w_out_g", o, dy, d)
    dqp, dkp, dvp = _flash_bwd(qp, kp, vp, o, lse, do, nb, seq)
    dq_raw, dkv_raw, dkr, d_qn, d_kn = _heads_bwd(q_raw, kv_raw, kr, cos, sin_signed, small["q_norm"],
                                                   small["k_norm"], seg, dqp, dkp, dvp)
    dcqn = _heads_nn("mla_bwd_dcq", dq_raw, wm["qb_t"], Q_LORA, F32)
    dckvn = _heads_nn("mla_bwd_dckv", dkv_raw, wm["kvb_t"], KV_LORA, F32)
    g_qb = _heads_wgrad("mla_w_qb_g", dq_raw, cqn, Q_LORA)
    g_kvb = _heads_wgrad("mla_w_kvb_g", dkv_raw, ckvn, KV_LORA)
    tm = _row_tile(t, 512)
    ckv_s, kr_s, cq_s = _lat_specs(t, tm)
    row = lambda i: (i, 0)
    par = lambda n: ((1, n), lambda i: (0, 0))

    def rms_b(xv, w, dv):
        _, vjp = jax.vjp(_rms_fn, xv, w)
        return vjp(dv)

    dckv, d_kva = _ew("mla_ckv_norm_bwd", (t // tm,), rms_b,
                      [(pm,) + ckv_s, (small["kv_a_norm"],) + par(KV_LORA), (dckvn, (tm, KV_LORA), row)],
                      [(SDS((t, KV_LORA), BF), (tm, KV_LORA), row, False),
                       (SDS((1, KV_LORA), F32),) + par(KV_LORA) + (True,)], acc_axes=(0,))
    dcq, d_qa = _ew("mla_cq_norm_bwd", (t // tm,), rms_b,
                    [(pm,) + cq_s, (small["q_a_norm"],) + par(Q_LORA), (dcqn, (tm, Q_LORA), row)],
                    [(SDS((t, Q_LORA), BF), (tm, Q_LORA), row, False),
                     (SDS((1, Q_LORA), F32),) + par(Q_LORA) + (True,)], acc_axes=(0,))
    tb = ((tm, HEAD_PAD), row)

    def kr_b(krv, w, cosv, sinv, sw, dv):
        _, vjp = jax.vjp(lambda a, b: _krope_fn(a, b, cosv, sinv, sw), krv, w)
        return vjp(dv)

    dkr_raw, d_kn2 = _ew("mla_krope_bwd", (t // tm,), kr_b,
                         [(pm,) + kr_s, (small["k_norm"],) + par(HEAD_PAD), (cos,) + tb, (sin_signed,) + tb,
                          (seg, (HEAD_PAD, HEAD_PAD), lambda i: (0, 0)), (dkr,) + tb],
                         [(SDS((t, HEAD_PAD), BF),) + tb + (False,),
                          (SDS((1, HEAD_PAD), F32),) + par(HEAD_PAD) + (True,)], acc_axes=(0,))
    dx, dnw = _seg_nn("mla_bwd_dh", [(dckv, wm["in_t"], PM_CKV), (dkr_raw, wm["in_t"], PM_KR),
                                     (dcq, wm["in_t"], PM_CQ)], d, BF, tk=128, rms=(x, nw, dy))
    g = {
        "in_ckv_t": _wgrad("mla_w_in_ckv_g", dckv, h, KV_LORA, d, tm=KV_LORA, tn=d, out_dtype=F32),
        "in_kr_t": _wgrad("mla_w_in_kr_g", dkr_raw, h, HEAD_PAD, d, tm=HEAD_PAD, tn=d, out_dtype=F32),
        "in_cq_t": _wgrad("mla_w_in_cq_g", dcq, h, Q_LORA, d, tm=Q_LORA, tn=d, out_dtype=F32),
        "qb_t": g_qb, "kvb_t": g_kvb, "out": g_out,
        "q_a_norm": d_qa, "kv_a_norm": d_kva, "q_norm": d_qn, "k_norm": d_kn + d_kn2,
    }
    return dx, dnw, g


def _mesh_pos():
    return lax.axis_index("x"), lax.axis_index("y"), lax.axis_index("c")


def _peer(pos, k):
    x, y, c = pos
    return (x ^ ((k >> 2) & 1), y ^ ((k >> 1) & 1), c ^ (k & 1))


def _flat(pos):
    return 4 * pos[0] + 2 * pos[1] + pos[2]


def _slab(ref, axis, start, size):
    idx = [slice(None)] * axis + [pl.ds(start, size)]
    return ref.at[tuple(idx)]


class _Exchange:
    def __init__(self, kind, items):
        self.kind = kind
        self.axes = [ax for _, ax in items]
        self.arrays = [a for a, _ in items]
        n = len(items)
        self.out_shapes = []
        self.sizes = []
        for a, ax in items:
            shp = list(a.shape)
            if kind == "gather":
                self.sizes.append(shp[ax])
                shp[ax] *= N_DEV
                self.out_shapes.append(SDS(tuple(shp), a.dtype))
            else:
                shp[ax] //= N_DEV
                self.sizes.append(shp[ax])
                self.out_shapes.append(SDS((N_DEV,) + tuple(shp), a.dtype))
        self.sems = [pltpu.SemaphoreType.DMA((n, N_DEV - 1)), pltpu.SemaphoreType.DMA((n, N_DEV - 1)),
                     pltpu.SemaphoreType.DMA((n,))]

    def _copies(self, srcs, dsts, sems, with_arrivals=True):
        send_sems, recv_sems, local_sems = sems
        pos = _mesh_pos()
        me = _flat(pos)
        local, sends, recvs = [], [], []
        for t, (src, dst) in enumerate(zip(srcs, dsts)):
            ax, sz = self.axes[t], self.sizes[t]
            if self.kind == "gather":
                mine = _slab(dst, ax, me * sz, sz)
                local.append(pltpu.make_async_copy(src, mine, local_sems.at[t]))
            else:
                mine = dst.at[me]
                local.append(pltpu.make_async_copy(_slab(src, ax, me * sz, sz), mine, local_sems.at[t]))
            for k in range(1, N_DEV):
                peer = _peer(pos, k)
                there = _flat(peer)
                if self.kind == "gather":
                    out_src, landing = src, _slab(dst, ax, there * sz, sz)
                else:
                    out_src, landing = _slab(src, ax, there * sz, sz), dst.at[there]
                common = dict(send_sem=send_sems.at[t, k - 1], recv_sem=recv_sems.at[t, k - 1], device_id=peer,
                              device_id_type=pl.DeviceIdType.MESH)
                sends.append(pltpu.make_async_remote_copy(src_ref=out_src, dst_ref=mine, **common))
                if with_arrivals:
                    recvs.append(pltpu.make_async_remote_copy(src_ref=out_src, dst_ref=landing, **common))
        return local, sends, recvs

    def start(self, srcs, dsts, sems):
        local, sends, _ = self._copies(srcs, dsts, sems, with_arrivals=False)
        for cp in local + sends:
            cp.start()

    def wait(self, srcs, dsts, sems):
        local, sends, recvs = self._copies(srcs, dsts, sems)
        for rc in recvs:
            rc.wait_recv()
        for rc in sends:
            rc.wait_send()
        for cp in local:
            cp.wait()

    def run(self, name):
        n = len(self.arrays)

        def body(*refs):
            srcs, dsts, sems = refs[:n], refs[n:2 * n], refs[2 * n:]
            self.start(srcs, dsts, sems)
            self.wait(srcs, dsts, sems)

        hbm = pl.BlockSpec(memory_space=pl.ANY)
        return pl.pallas_call(body, name=name, in_specs=[hbm] * n, out_specs=[hbm] * n, out_shape=self.out_shapes,
                              scratch_shapes=self.sems)(*self.arrays)


def _adam_math(w, g, m, v):
    m = ADAM_B1 * m + (1.0 - ADAM_B1) * g
    v = ADAM_B2 * v + (1.0 - ADAM_B2) * (g * g)
    m_hat = m / (1.0 - ADAM_B1 ** ADAM_STEP)
    v_hat = v / (1.0 - ADAM_B2 ** ADAM_STEP)
    delta = -ADAM_LR * (m_hat / (jnp.sqrt(v_hat) + ADAM_EPS) + ADAM_WD * w)
    return delta, m, v


def _adam(name, land, land_blk, land_idx, w, m, v, transposed, ck):
    n, r, c = w.shape
    wblk = ((None, ck, c), lambda a, i: (a, i, 0))

    def fn(parts, w, m, v):
        g = parts[0].astype(F32)
        for s in range(1, N_DEV):
            g = g + parts[s].astype(F32)
        if transposed:
            g = g.T
        delta, m2, v2 = _adam_math(w, g, m, v)
        return g, delta, m2, v2

    return _ew(name, (n, r // ck), fn,
               [(land, land_blk, land_idx), (w,) + wblk, (m,) + wblk, (v,) + wblk],
               [(SDS(w.shape, F32),) + wblk + (False,)] * 4, vmem=VMEM_BIG)


def _prep_ffn(gate, up, down):
    def body(g, u, dn, o):
        o[0] = g[...].T.astype(BF)
        o[1] = u[...].T.astype(BF)
        o[2] = dn[...].astype(BF)

    cblk = pl.BlockSpec((None, None, D_MODEL, FF_SHARD), lambda l, i: (l, i, 0, 0))
    rblk = pl.BlockSpec((None, None, FF_SHARD, D_MODEL), lambda l, i: (l, i, 0, 0))
    return pl.pallas_call(
        body, name="prep_ffn", grid=(2, 2), in_specs=[cblk, cblk, rblk],
        out_specs=pl.BlockSpec((3, FF_SHARD, D_MODEL), lambda l, i: (2 * l + i, 0, 0)),
        out_shape=SDS((12, FF_SHARD, D_MODEL), BF), compiler_params=_cparams(2, VMEM_BIG),
    )(gate, up, down)


def _transpose_cast(name, w, dtype):
    def body(a, o):
        o[...] = a[...].T.astype(dtype)

    r, c = w.shape
    return pl.pallas_call(body, name=name, out_shape=SDS((c, r), dtype),
                          compiler_params=pltpu.CompilerParams(vmem_limit_bytes=VMEM_BIG))(w)


SMALL_SHARDED = (("norm_w", 6 * 128), ("conv_w", CONV_K * 512), ("q_a_norm", 48), ("kv_a_norm", 32))
SMALL_PACK = 3072


def _dyn(a, start, size):
    return lax.dynamic_slice_in_dim(a, start, size, axis=a.ndim - 1)


def _layout_ssm(ssm_in_t, ssm_out_all):
    d = ssm_in_t.shape[1]
    dt_rows = ssm_in_t[D_INNER + CONV_DIM:].reshape(SSM_GROUPS, SSM_HPG, d)
    return {"z_t": ssm_in_t[:D_INNER], "xbc_t": ssm_in_t[D_INNER:D_INNER + CONV_DIM],
            "dt_t": jnp.pad(dt_rows, ((0, 0), (0, LANES - SSM_HPG), (0, 0))).reshape(SSM_GROUPS * LANES, d),
            "out": ssm_out_all}


def _layout_mla(mla_in_all, qb_all, kvb_all, mla_out_all):
    d = mla_out_all.shape[1]
    in_t = mla_in_all.T
    kr_rows = jnp.pad(in_t[Q_LORA + KV_LORA:], ((QK_NOPE, HEAD_PAD - QK_DIM), (0, 0)))
    qb_heads = jnp.pad(qb_all.reshape(MLA_HEADS, QK_DIM, Q_LORA), ((0, 0), (0, HEAD_PAD - QK_DIM), (0, 0)))
    out_heads = jnp.pad(mla_out_all.reshape(MLA_HEADS, 64, d), ((0, 0), (64, 0), (0, 0)))
    return {"in_t": jnp.concatenate([in_t[Q_LORA:Q_LORA + KV_LORA], kr_rows, in_t[:Q_LORA]], axis=0),
            "qb_t": qb_heads.reshape(MLA_HEADS * HEAD_PAD, Q_LORA), "kvb_t": kvb_all,
            "out": out_heads.reshape(MLA_HEADS * HEAD_PAD, d)}


def _layout_small(conv_w, conv_b, dt_bias, a_log, d_skip, ssm_norm_w, q_a_norm, kv_a_norm, q_norm, k_norm):
    lane_heads = lambda p: jnp.pad(p.reshape(SSM_GROUPS, SSM_HPG), ((0, 0), (0, LANES - SSM_HPG))).reshape(1, -1)
    pad_head = lambda p: jnp.pad(p.reshape(1, QK_DIM), ((0, 0), (0, HEAD_PAD - QK_DIM)))
    return {"conv_w": conv_w, "conv_b": conv_b, "dt_bias": lane_heads(dt_bias), "a_log": lane_heads(a_log),
            "d_skip": lane_heads(d_skip), "ssm_norm_w": ssm_norm_w, "q_a_norm": q_a_norm, "kv_a_norm": kv_a_norm,
            "q_norm": pad_head(q_norm), "k_norm": pad_head(k_norm)}


class _Plan:
    def __init__(self, ctx):
        self.ctx = ctx
        self.make = {}
        self.land = {}

    def ride(self, host, make, land):
        assert host not in self.make, host
        self.make[host] = make
        self.land[host] = land

    def comm(self, host):
        return self.make[host](self.ctx) if host in self.make else None

    def done(self, host, results):
        self.land[host](results, self.ctx)


def _local_step(x, positions, loss_target, ctx, plan=None):
    nb, seq, d = x.shape
    t = nb * seq
    xf = x.reshape(t, d)
    norm = ctx["norm"]
    tables = list(_rope_tables(positions.reshape(t, 1))) + [_segment_matrix()]
    x1, s_f0 = _ffn_fwd("ffn0", xf, norm[0, 0], ctx["ffn0"], plan)
    x2, s_ssm = _ssm_fwd(x1, norm[0, 1], ctx["ws"], ctx["small"], nb, seq, plan)
    x3, s_f1 = _ffn_fwd("ffn1", x2, norm[0, 2], ctx["ffn1"], plan)
    x4, s_f2 = _ffn_fwd("ffn2", x3, norm[1, 0], ctx["ffn2"], plan)
    x5, s_mla = _mla_fwd(x4, norm[1, 1], ctx["wm"], ctx["small"], tables, nb, seq, plan)
    x6, s_f3 = _ffn_fwd("ffn3", x5, norm[1, 2], ctx["ffn3"], plan)
    dy, loss_cols = _loss_and_grad(x6, loss_target.reshape(t, d))

    dx5, dn12, ctx["g_ffn3"] = _ffn_bwd("ffn3", dy, norm[1, 2], ctx["ffn3"], s_f3, plan)
    dx4, dn11, ctx["g_mla"] = _mla_bwd(dx5, norm[1, 1], ctx["wm"], ctx["small"], tables, s_mla, nb, seq)
    dx3, dn10, ctx["g_ffn2"] = _ffn_bwd("ffn2", dx4, norm[1, 0], ctx["ffn2"], s_f2, plan)
    dx2, dn02, ctx["g_ffn1"] = _ffn_bwd("ffn1", dx3, norm[0, 2], ctx["ffn1"], s_f1, plan)
    dx1, dn01, ctx["g_ssm"] = _ssm_bwd(dx2, norm[0, 1], ctx["ws"], ctx["small"], s_ssm, nb, seq, plan)
    dx0, dn00, ctx["g_ffn0"] = _ffn_bwd("ffn0", dx1, norm[0, 0], ctx["ffn0"], s_f0, plan)
    return loss_cols, dx0.reshape(nb, seq, d), (dn00, dn01, dn02, dn10, dn11, dn12)


def kernel(x, positions, norm_w, ffn_w_gate, ffn_w_up, ffn_w_down, ssm_w_in, ssm_conv_w, ssm_conv_b, ssm_dt_bias, ssm_a_log, ssm_d, ssm_norm_w, ssm_w_out, mla_w_in, mla_q_a_norm, mla_kv_a_norm, mla_w_q_b, mla_w_kv_b, mla_q_norm, mla_k_norm, mla_w_out, loss_target, m_norm_w, m_ffn_w_gate, m_ffn_w_up, m_ffn_w_down, m_ssm_w_in, m_ssm_conv_w, m_ssm_conv_b, m_ssm_dt_bias, m_ssm_a_log, m_ssm_d, m_ssm_norm_w, m_ssm_w_out, m_mla_w_in, m_mla_q_a_norm, m_mla_kv_a_norm, m_mla_w_q_b, m_mla_w_kv_b, m_mla_q_norm, m_mla_k_norm, m_mla_w_out, v_norm_w, v_ffn_w_gate, v_ffn_w_up, v_ffn_w_down, v_ssm_w_in, v_ssm_conv_w, v_ssm_conv_b, v_ssm_dt_bias, v_ssm_a_log, v_ssm_d, v_ssm_norm_w, v_ssm_w_out, v_mla_w_in, v_mla_q_a_norm, v_mla_kv_a_norm, v_mla_w_q_b, v_mla_w_kv_b, v_mla_q_norm, v_mla_k_norm, v_mla_w_out):
    nb, seq, d = x.shape
    t = nb * seq
    me = _flat(_mesh_pos())

    ffn_loc = _prep_ffn(ffn_w_gate, ffn_w_up, ffn_w_down)
    ssm_in_loc = _transpose_cast("prep_ssm_in", ssm_w_in[0], BF)
    ssm_out_loc = ssm_w_out[0].astype(BF)
    mla_in_loc, mla_out_loc = mla_w_in[0].astype(BF), mla_w_out[0].astype(BF)
    qb_loc = _transpose_cast("prep_q_b", mla_w_q_b[0], BF)
    kvb_loc = _transpose_cast("prep_kv_b", mla_w_kv_b[0], BF)
    small_loc = jnp.concatenate([norm_w.reshape(-1), ssm_conv_w.reshape(-1), mla_q_a_norm.reshape(-1),
                                 mla_kv_a_norm.reshape(-1)])
    small_loc = jnp.pad(small_loc, (0, SMALL_PACK - small_loc.shape[0])).reshape(SMALL_PACK // LANES, LANES)

    wloc = lambda n: [(ffn_loc[3 * n + k], 0) for k in range(3)]
    g0, u0, small_all = _Exchange("gather", wloc(0)[0:2] + [(small_loc, 0)]).run("gather_first")
    sm = small_all.reshape(N_DEV, SMALL_PACK)
    conv_w_full = sm[:, 768:768 + 2048].reshape(N_DEV, CONV_K, 512).transpose(1, 0, 2).reshape(CONV_K, CONV_DIM)
    ctx = {"ffn0": [g0, u0, None], "ffn1": [None] * 3, "ffn2": [None] * 3,
           "norm": sm[:, :768].reshape(N_DEV, 6, 128).transpose(1, 0, 2).reshape(2, 3, 1, d),
           "small": _layout_small(conv_w_full, ssm_conv_b, ssm_dt_bias, ssm_a_log, ssm_d, ssm_norm_w,
                                  sm[:, 2816:2864].reshape(1, Q_LORA), sm[:, 2864:2896].reshape(1, KV_LORA),
                                  mla_q_norm, mla_k_norm)}
    plan = _Plan(ctx)

    def gather_on(host, items, land):
        plan.ride(host, lambda c: _Exchange("gather", items), land)

    def put_w(key, ks):
        def land(r, c):
            for k, arr in zip(ks, r):
                c[key][k] = arr
        return land

    half_rows = SSM_IN_SHARD // 2
    ssm_in_a, ssm_in_b = (ssm_in_loc[k * half_rows:(k + 1) * half_rows].reshape(half_rows // 2, 16, LANES)
                          for k in range(2))

    def land_first(r, c):
        c["ffn0"][2] = r[0]
        c["ssm_in_a"] = r[1].reshape(N_DEV, half_rows, d)

    def land_ssm_in(r, c):
        both = jnp.concatenate([c["ssm_in_a"], r[0].reshape(N_DEV, half_rows, d)], axis=1)
        c["ws"] = _layout_ssm(both.reshape(SSM_IN_DIM, d), None)

    def land_ssm_out(r, c):
        c["ffn1"][1] = r[0]
        c["ws"]["out"] = r[1]

    gather_on("ffn0_up", wloc(0)[2:3] + [(ssm_in_a, 0)], land_first)
    gather_on("ffn0_down", [(ssm_in_b, 0)], land_ssm_in)
    gather_on("ssm_in_z", wloc(1)[0:1], put_w("ffn1", (0,)))
    gather_on("ssm_in_xbc", wloc(1)[1:2] + [(ssm_out_loc, 0)], land_ssm_out)
    gather_on("ssm_out", wloc(1)[2:3], put_w("ffn1", (2,)))
    gather_on("ffn1_up", wloc(2)[0:2], put_w("ffn2", (0, 1)))
    gather_on("ffn1_down", wloc(2)[2:3], put_w("ffn2", (2,)))
    gather_on("ffn2_up", [(mla_in_loc, 0), (qb_loc, 0), (kvb_loc, 0), (mla_out_loc, 0)],
              lambda r, c: c.update(wm=_layout_mla(r[0], r[1], r[2], r[3])))
    gather_on("mla_flash_fwd", wloc(3), lambda r, c: c.update(ffn3=tuple(r)))

    heads_of = lambda a: a.reshape(SSM_GROUPS, LANES, -1)[:, :SSM_HPG].reshape(SSM_HEADS, -1)

    def mla_grad_items(c):
        g = c["g_mla"]
        g_in = jnp.concatenate([g["in_cq_t"], g["in_ckv_t"], g["in_kr_t"][QK_NOPE:QK_DIM]], axis=0).T
        g_qb = g["qb_t"].reshape(MLA_HEADS, HEAD_PAD, Q_LORA)[:, :QK_DIM].reshape(MLA_HEADS * QK_DIM, Q_LORA)
        g_out = g["out"].reshape(MLA_HEADS, HEAD_PAD, d)[:, 64:].reshape(MLA_HEADS * 64, d)
        return [(a.astype(BF), 0) for a in (g_out, g_in, g_qb, g["kvb_t"])]

    def ssm_in_grad(which):
        def items(c):
            if "g_ssm_in_t" not in c:
                g = c["g_ssm"]
                g_in_t = jnp.concatenate([g["z_t"], g["x_t"], g["b_t"], g["c_t"], heads_of(g["dt_t"])], axis=0)
                c["g_ssm_in_t"] = g_in_t.astype(BF).reshape(N_DEV, SSM_IN_SHARD, d)
            part = c["g_ssm_in_t"][:, which * half_rows:(which + 1) * half_rows]
            return [(part.reshape(N_DEV * half_rows // 2, 16, LANES), 0)]
        return items

    def scatter_on(host, items_of, keys):
        plan.ride(host, lambda c: _Exchange("scatter", items_of(c)),
                  lambda r, c: c.update(dict(zip(keys, r))))

    of = lambda key, k: (lambda c: [(c[key][k], 0)])
    scatter_on("ffn2_bwd_act", of("g_ffn3", 0), ("l3_gate",))
    scatter_on("ffn2_bwd_dh", of("g_ffn3", 1), ("l3_up",))
    scatter_on("ffn2_wg", of("g_ffn3", 2), ("l3_down",))
    scatter_on("ffn2_wu", mla_grad_items, ("l_mla_out", "l_mla_in", "l_qb", "l_kvb"))
    scatter_on("ffn1_bwd_act", of("g_ffn2", 0), ("l2_gate",))
    scatter_on("ffn1_bwd_dh", of("g_ffn2", 1), ("l2_up",))
    scatter_on("ffn1_wg", of("g_ffn2", 2), ("l2_down",))
    scatter_on("ssm_bwd_dgn", of("g_ffn1", 0), ("l1_gate",))
    scatter_on("ssm_bwd_dh", lambda c: [(c["g_ffn1"][1], 0), (c["g_ffn1"][2], 0)], ("l1_up", "l1_down"))
    scatter_on("ffn0_bwd_act", ssm_in_grad(0), ("l_ssm_in_a",))
    scatter_on("ffn0_bwd_dh", ssm_in_grad(1), ("l_ssm_in_b",))
    scatter_on("ffn0_wg", lambda c: [(c["g_ssm"]["out"], 0)], ("l_ssm_out",))
    scatter_on("ffn0_wu", lambda c: [(c["g_ffn0_gate"], 0)], ("l0_gate",))
    scatter_on("ffn0_wd", lambda c: [(c["g_ffn0_up"], 0)], ("l0_up",))

    loss_cols, grad_x, dns = _local_step(x, positions, loss_target, ctx, plan)
    loss = lax.psum(jnp.sum(loss_cols), ("x", "y", "c"))
    dn00, dn01, dn02, dn10, dn11, dn12 = dns
    g_ssm, g_mla = ctx["g_ssm"], ctx["g_mla"]
    ctx["l0_down"] = _Exchange("scatter", [(ctx["g_ffn0"][2], 0)]).run("scatter_last")[0]
    l_ffn = {k: jnp.stack([ctx["l%d_%s" % (n, k)] for n in range(4)], axis=1) for k in ("gate", "up", "down")}
    l_mla_out, l_mla_in, l_qb, l_kvb = (ctx[k] for k in ("l_mla_out", "l_mla_in", "l_qb", "l_kvb"))
    l_ssm_in = jnp.concatenate([ctx[k].reshape(N_DEV, half_rows, d) for k in ("l_ssm_in_a", "l_ssm_in_b")], axis=1)
    l_ssm_out = ctx["l_ssm_out"]

    unlane = lambda a: a.reshape(SSM_GROUPS, LANES)[:, :SSM_HPG].reshape(1, SSM_HEADS)
    small_g = jnp.concatenate([
        jnp.concatenate([dn00, dn01, dn02, dn10, dn11, dn12], axis=0).reshape(-1),
        g_ssm["conv_w"].reshape(-1), g_ssm["conv_b"].reshape(-1), unlane(g_ssm["dt_bias"]).reshape(-1),
        unlane(g_ssm["a_log"]).reshape(-1), unlane(g_ssm["d_skip"]).reshape(-1), g_ssm["ssm_norm_w"].reshape(-1),
        g_mla["q_a_norm"].reshape(-1), g_mla["kv_a_norm"].reshape(-1), g_mla["q_norm"][0, :QK_DIM],
        g_mla["k_norm"][0, :QK_DIM]])
    n_small = small_g.shape[0]
    n_small_pad = -(-n_small // (8 * LANES)) * (8 * LANES)
    small_g = jnp.pad(small_g, (0, n_small_pad - n_small)).reshape(n_small_pad // LANES, LANES)
    gs = _Exchange("gather", [(small_g, 0)]).run("gather_small_grads")[0].reshape(N_DEV, n_small_pad)

    outs = {}

    def put(name, res, shape):
        for key, val in zip(("grad", "delta", "new_m", "new_v"), res):
            outs[(key, name)] = val.reshape(shape)

    ck = 256
    for key, name, w, m, v in (("gate", "ffn_w_gate", ffn_w_gate, m_ffn_w_gate, v_ffn_w_gate),
                               ("up", "ffn_w_up", ffn_w_up, m_ffn_w_up, v_ffn_w_up)):
        res = _adam("adam_" + name, l_ffn[key], (N_DEV, None, FF_SHARD, ck), lambda a, i: (0, a, 0, i),
                    w.reshape(4, d, FF_SHARD), m.reshape(4, d, FF_SHARD), v.reshape(4, d, FF_SHARD), True, ck)
        put(name, res, w.shape)
    res = _adam("adam_ffn_w_down", l_ffn["down"], (N_DEV, None, 176, d), lambda a, i: (0, a, i, 0),
                ffn_w_down.reshape(4, FF_SHARD, d), m_ffn_w_down.reshape(4, FF_SHARD, d),
                v_ffn_w_down.reshape(4, FF_SHARD, d), False, 176)
    put("ffn_w_down", res, ffn_w_down.shape)
    l_ssm_in2 = l_ssm_in.reshape(N_DEV, SSM_IN_SHARD, d)
    res = _adam("adam_ssm_w_in", l_ssm_in2, (N_DEV, SSM_IN_SHARD, 128), lambda a, i: (0, 0, i),
                ssm_w_in, m_ssm_w_in, v_ssm_w_in, True, 128)
    put("ssm_w_in", res, ssm_w_in.shape)
    res = _adam("adam_ssm_w_out", l_ssm_out, (N_DEV, 128, d), lambda a, i: (0, i, 0),
                ssm_w_out, m_ssm_w_out, v_ssm_w_out, False, 128)
    put("ssm_w_out", res, ssm_w_out.shape)
    res = _adam("adam_mla_w_in", l_mla_in, (N_DEV, 128, MLA_IN_DIM), lambda a, i: (0, 0, 0),
                mla_w_in, m_mla_w_in, v_mla_w_in, False, 128)
    put("mla_w_in", res, mla_w_in.shape)
    res = _adam("adam_mla_w_q_b", l_qb, (N_DEV, 192, 128), lambda a, i: (0, 0, i),
                mla_w_q_b, m_mla_w_q_b, v_mla_w_q_b, True, 128)
    put("mla_w_q_b", res, mla_w_q_b.shape)
    res = _adam("adam_mla_w_kv_b", l_kvb, (N_DEV, 256, 128), lambda a, i: (0, 0, i),
                mla_w_kv_b, m_mla_w_kv_b, v_mla_w_kv_b, True, 128)
    put("mla_w_kv_b", res, mla_w_kv_b.shape)
    res = _adam("adam_mla_w_out", l_mla_out, (N_DEV, 128, d), lambda a, i: (0, 0, 0),
                mla_w_out, m_mla_w_out, v_mla_w_out, False, 128)
    put("mla_w_out", res, mla_w_out.shape)

    small_params = (
        ("norm_w", norm_w, m_norm_w, v_norm_w, 6 * d, 6, 128), ("ssm_conv_w", ssm_conv_w, m_ssm_conv_w, v_ssm_conv_w,
                                                               CONV_K * CONV_DIM, CONV_K, 512),
        ("ssm_conv_b", ssm_conv_b, m_ssm_conv_b, v_ssm_conv_b, CONV_DIM, 0, 0),
        ("ssm_dt_bias", ssm_dt_bias, m_ssm_dt_bias, v_ssm_dt_bias, SSM_HEADS, 0, 0),
        ("ssm_a_log", ssm_a_log, m_ssm_a_log, v_ssm_a_log, SSM_HEADS, 0, 0),
        ("ssm_d", ssm_d, m_ssm_d, v_ssm_d, SSM_HEADS, 0, 0),
        ("ssm_norm_w", ssm_norm_w, m_ssm_norm_w, v_ssm_norm_w, D_INNER, 0, 0),
        ("mla_q_a_norm", mla_q_a_norm, m_mla_q_a_norm, v_mla_q_a_norm, Q_LORA, 1, 48),
        ("mla_kv_a_norm", mla_kv_a_norm, m_mla_kv_a_norm, v_mla_kv_a_norm, KV_LORA, 1, 32),
        ("mla_q_norm", mla_q_norm, m_mla_q_norm, v_mla_q_norm, QK_DIM, 0, 0),
        ("mla_k_norm", mla_k_norm, m_mla_k_norm, v_mla_k_norm, QK_DIM, 0, 0),
    )
    parts, ws_, ms_, vs_, off = [], [], [], [], 0
    for name, w, m, v, full, rows, shard in small_params:
        seg = gs[:, off:off + full]
        if rows:
            seg = _dyn(seg.reshape(N_DEV, rows, full // rows), me * shard, shard).reshape(N_DEV, rows * shard)
        parts.append(seg)
        ws_.append(w.reshape(1, -1))
        ms_.append(m.reshape(1, -1))
        vs_.append(v.reshape(1, -1))
        off += full
    n_loc = sum(p.shape[1] for p in parts)
    n_loc_pad = -(-n_loc // LANES) * LANES
    padc = lambda a, val=0.0: jnp.pad(jnp.concatenate(a, axis=1), ((0, 0), (0, n_loc_pad - n_loc)),
                                      constant_values=val)
    res = _adam("adam_small", padc(parts).reshape(N_DEV, 1, n_loc_pad), (N_DEV, 1, n_loc_pad), lambda a, i: (0, 0, 0),
                padc(ws_).reshape(1, 1, n_loc_pad), padc(ms_).reshape(1, 1, n_loc_pad),
                padc(vs_, 1.0).reshape(1, 1, n_loc_pad), False, 1)
    off = 0
    for name, w, m, v, full, rows, shard in small_params:
        nloc = w.size
        put(name, [r.reshape(-1)[off:off + nloc] for r in res], w.shape)
        off += nloc

    order = ("norm_w", "ffn_w_gate", "ffn_w_up", "ffn_w_down", "ssm_w_in", "ssm_conv_w", "ssm_conv_b", "ssm_dt_bias",
             "ssm_a_log", "ssm_d", "ssm_norm_w", "ssm_w_out", "mla_w_in", "mla_q_a_norm", "mla_kv_a_norm",
             "mla_w_q_b", "mla_w_kv_b", "mla_q_norm", "mla_k_norm", "mla_w_out")
    return (loss, grad_x, *[outs[(k, n)] for k in ("grad", "delta", "new_m", "new_v") for n in order])
```

```python
import jax
import jax.numpy as jnp
import numpy as np
from jax import lax
from jax.experimental import pallas as pl
from jax.experimental.pallas import tpu as pltpu

F32 = jnp.float32
BF = jnp.bfloat16
SDS = jax.ShapeDtypeStruct

N_DEV = 8
D_MODEL = 1024
D_FF = 2816
FF_SHARD = D_FF // N_DEV
D_INNER = 2048
SSM_HEADS = 32
SSM_GROUPS = 8
SSM_HPG = 4
SSM_STATE = 128
CONV_K = 4
CONV_DIM = 4096
SSM_IN_DIM = 6176
SSM_IN_SHARD = SSM_IN_DIM // N_DEV
NORM_GROUP = 256
CHUNK = 128
MLA_HEADS = 16
Q_LORA = 384
KV_LORA = 256
QK_NOPE = 64
QK_ROPE = 32
QK_DIM = 96
MLA_IN_DIM = 672
HEAD_PAD = 128
ROPE_THETA = 10000.0
EPS = 1e-6
LANES = 128

ADAM_LR = 0.001
ADAM_B1 = 0.9
ADAM_B2 = 0.999
ADAM_EPS = 1e-08
ADAM_WD = 0.01
ADAM_STEP = 10

VMEM_BIG = 56 * 1024 * 1024

NN = ((1,), (0,))
NT = ((1,), (1,))
TN = ((0,), (0,))


def _dotf(a, b, dn):
    return lax.dot_general(a.astype(BF), b.astype(BF), (dn, ((), ())), preferred_element_type=F32)


def _dot_hi(a, b, dn=NN):
    return lax.dot_general(a, b, (dn, ((), ())), precision=lax.Precision.HIGHEST, preferred_element_type=F32)


def _sigmoid(x):
    return jax.nn.sigmoid(x)


def _silu(x):
    return x * _sigmoid(x)


def _softplus(x):
    return jnp.maximum(x, 0.0) + jnp.log(1.0 + jnp.exp(-jnp.abs(x)))


def _cparams(n_grid, vmem=None):
    return pltpu.CompilerParams(dimension_semantics=("arbitrary",) * n_grid, vmem_limit_bytes=vmem)


def _fmm(name, grid_mn, pairs, outs, *, epi=None, extras=(), n_acc=1, acc_shape=None, vmem=None, joint=False,
         hosts=None):
    comm = hosts.comm(name) if hosts is not None else None
    if joint:
        nk_total = pairs[0][7]
        assert all(p[7] == nk_total for p in pairs)
        starts = [0] * len(pairs)
    else:
        nk_total = sum(p[7] for p in pairs)
        starts = []
        s = 0
        for p in pairs:
            starts.append(s)
            s += p[7]
    n_pairs, n_extras, n_outs = len(pairs), len(extras), len(outs)
    single = nk_total == 1

    n_ci = len(comm.arrays) if comm is not None else 0
    n_co = len(comm.out_shapes) if comm is not None else 0
    n_scratch_acc = 0 if single else n_acc

    def body(*refs):
        ab_refs = refs[: 2 * n_pairs]
        e_refs = refs[2 * n_pairs: 2 * n_pairs + n_extras]
        pos = 2 * n_pairs + n_extras
        ci_refs = refs[pos: pos + n_ci]
        pos += n_ci
        o_refs = refs[pos: pos + n_outs]
        co_refs = refs[pos + n_outs: pos + n_outs + n_co]
        pos += n_outs + n_co
        acc_refs = refs[pos: pos + n_scratch_acc]
        sem_refs = refs[pos + n_scratch_acc:]
        i, j, k = pl.program_id(0), pl.program_id(1), pl.program_id(2)

        if comm is not None:
            @pl.when((i == 0) & (j == 0) & (k == 0))
            def _():
                comm.start(ci_refs, co_refs, sem_refs)

        compute(ab_refs, e_refs, o_refs, acc_refs, i, j, k)

        if comm is not None:
            @pl.when((i == grid_mn[0] - 1) & (j == grid_mn[1] - 1) & (k == nk_total - 1))
            def _():
                comm.wait(ci_refs, co_refs, sem_refs)

    def compute(ab_refs, e_refs, o_refs, acc_refs, i, j, k):

        def finish(accs):
            res = epi(accs, *[e[...] for e in e_refs]) if epi is not None else accs
            if not isinstance(res, (tuple, list)):
                res = (res,)
            first = (i == 0) & (j == 0)
            for o, r, spec in zip(o_refs, res, outs):
                if spec[3]:
                    @pl.when(first)
                    def _(o=o, r=r):
                        o[...] = r.astype(o.dtype)

                    @pl.when(jnp.logical_not(first))
                    def _(o=o, r=r):
                        o[...] += r.astype(o.dtype)
                else:
                    o[...] = r.astype(o.dtype)

        if single:
            accs = [None] * n_acc
            for p, pr in enumerate(pairs):
                d = _dotf(ab_refs[2 * p][...], ab_refs[2 * p + 1][...], pr[6])
                accs[pr[8]] = d if accs[pr[8]] is None else accs[pr[8]] + d
            finish(accs)
            return

        @pl.when(k == 0)
        def _():
            for a in acc_refs:
                a[...] = jnp.zeros_like(a)

        for p, pr in enumerate(pairs):
            def step(p=p, pr=pr):
                acc_refs[pr[8]][...] += _dotf(ab_refs[2 * p][...], ab_refs[2 * p + 1][...], pr[6])

            if n_pairs == 1 or joint:
                step()
            else:
                pl.when((k >= starts[p]) & (k < starts[p] + pr[7]))(step)

        @pl.when(k == nk_total - 1)
        def _():
            finish([a[...] for a in acc_refs])

    in_specs, args = [], []
    for p, pr in enumerate(pairs):
        a, a_blk, a_idx, b, b_blk, b_idx, _, nk, _ = pr
        st = starts[p]

        def amap(i, j, k, a_idx=a_idx, st=st, nk=nk):
            return a_idx(i, j, jnp.clip(k - st, 0, nk - 1))

        def bmap(i, j, k, b_idx=b_idx, st=st, nk=nk):
            return b_idx(i, j, jnp.clip(k - st, 0, nk - 1))

        in_specs += [pl.BlockSpec(a_blk, amap), pl.BlockSpec(b_blk, bmap)]
        args += [a, b]
    for arr, blk, idx in extras:
        in_specs.append(pl.BlockSpec(blk, lambda i, j, k, idx=idx: idx(i, j)))
        args.append(arr)
    out_specs = [pl.BlockSpec(blk, lambda i, j, k, idx=idx: idx(i, j)) for _, blk, idx, _ in outs]
    out_shape = [o[0] for o in outs]
    scratch = [] if single else [pltpu.VMEM(acc_shape, F32) for _ in range(n_acc)]
    if comm is not None:
        hbm = pl.BlockSpec(memory_space=pl.ANY)
        in_specs += [hbm] * n_ci
        args += list(comm.arrays)
        out_specs += [hbm] * n_co
        out_shape += list(comm.out_shapes)
        scratch += comm.sems
    res = pl.pallas_call(
        body, name=name, grid=(grid_mn[0], grid_mn[1], nk_total), in_specs=in_specs, out_specs=out_specs,
        out_shape=out_shape, scratch_shapes=scratch, compiler_params=_cparams(3, vmem),
    )(*args)
    if comm is not None:
        hosts.done(name, res[n_outs:])
    return res[:n_outs]


def _ew(name, grid, fn, ins, outs, *, acc_axes=(), vmem=None):
    n_in = len(ins)

    def body(*refs):
        res = fn(*[r[...] for r in refs[:n_in]])
        if not isinstance(res, (tuple, list)):
            res = (res,)
        first = None
        for ax in acc_axes:
            c = pl.program_id(ax) == 0
            first = c if first is None else (first & c)
        for o, r, spec in zip(refs[n_in:], res, outs):
            if spec[3]:
                @pl.when(first)
                def _(o=o, r=r):
                    o[...] = r.astype(o.dtype)

                @pl.when(jnp.logical_not(first))
                def _(o=o, r=r):
                    o[...] += r.astype(o.dtype)
            else:
                o[...] = r.astype(o.dtype)

    return pl.pallas_call(
        body, name=name, grid=grid,
        in_specs=[pl.BlockSpec(blk, idx) for _, blk, idx in ins],
        out_specs=[pl.BlockSpec(blk, idx) for _, blk, idx, _ in outs],
        out_shape=[o[0] for o in outs],
        compiler_params=_cparams(len(grid), vmem),
    )(*[a for a, _, _ in ins])


def _row_tile(t, want):
    tm = min(want, t)
    assert t % tm == 0, (t, tm)
    return tm


def _rms_fn(x, w):
    return x * lax.rsqrt(jnp.mean(x * x, axis=-1, keepdims=True) + EPS) * w


def _rms_fwd(name, x, w):
    t, d = x.shape
    tm = _row_tile(t, 512)
    return _ew(name, (t // tm,), _rms_fn,
               [(x, (tm, d), lambda i: (i, 0)), (w, (1, d), lambda i: (0, 0))],
               [(SDS((t, d), BF), (tm, d), lambda i: (i, 0), False)])[0]


def _rms_bwd_epi(accs, xv, wv, dres):
    _, vjp = jax.vjp(_rms_fn, xv, wv)
    dx, dw = vjp(accs[0])
    return dx + dres, dw


def _rms_bwd_tail(x, w, dres, tm):
    t, d = x.shape
    row = lambda i, j: (i, 0)
    const = lambda i, j: (0, 0)
    extras = [(x, (tm, d), row), (w, (1, d), const), (dres, (tm, d), row)]
    outs = [(SDS((t, d), F32), (tm, d), row, False), (SDS((1, d), F32), (1, d), const, True)]
    return extras, outs


def _loss_and_grad(y, target):
    t, d = y.shape
    tm = _row_tile(t, 512)

    def fn(y, tg):
        e = y - tg
        return e * (1.0 / d), jnp.sum(e * e, axis=0, keepdims=True) * (0.5 / d)

    row = lambda i: (i, 0)
    return _ew("loss_head", (t // tm,), fn, [(y, (tm, d), row), (target, (tm, d), row)],
               [(SDS((t, d), F32), (tm, d), row, False), (SDS((1, d), F32), (1, d), lambda i: (0, 0), True)],
               acc_axes=(0,))


def _ffn_fwd(tag, x, nw, wf, hosts=None):
    gate_t, up_t = wf[0], wf[1]
    t, d = x.shape
    h = _rms_fwd(tag + "_rms", x, nw)
    tm, tn = _row_tile(t, 256), D_FF

    def epi(accs):
        g, u = accs
        s = _sigmoid(g)
        sg = g * s
        return 0.5 * sg, 0.5 * (s * (1.0 + g * (1.0 - s))), u, sg * u

    hblk = (h, (tm, d), lambda i, j, k: (i, 0))
    col = lambda i, j: (i, j)
    tblk = lambda w: (w, (tn, d), lambda i, j, k: (j, 0), NT, 1)
    sgh, dsgh, u, a = _fmm(
        tag + "_up", (t // tm, D_FF // tn), [hblk + tblk(gate_t) + (0,), hblk + tblk(up_t) + (1,)],
        [(SDS((t, D_FF), BF), (tm, tn), col, False)] * 4, epi=epi, n_acc=2, joint=True, hosts=hosts, vmem=VMEM_BIG)
    down = wf[2]
    tm2 = _row_tile(t, 512)
    y = _fmm(
        tag + "_down", (t // tm2, 1),
        [(a, (tm2, D_FF), lambda i, j, k: (i, 0), down, (D_FF, d), lambda i, j, k: (0, 0), NN, 1, 0)],
        [(SDS((t, d), F32), (tm2, d), lambda i, j: (i, 0), False)],
        extras=[(x, (tm2, d), lambda i, j: (i, 0))],
        epi=lambda accs, xr: xr + 0.5 * accs[0], vmem=VMEM_BIG, hosts=hosts)[0]
    return y, (x, h, sgh, dsgh, u, a)


def _wgrad(name, a, b, m, n, *, tm, tn, tk=2048, scale=None, out_dtype=BF, hosts=None):
    t = a.shape[0]
    tk = _row_tile(t, tk)
    epi = (lambda accs: accs[0] * scale) if scale is not None else None
    return _fmm(name, (m // tm, n // tn),
                [(a, (tk, tm), lambda i, j, k: (k, i), b, (tk, tn), lambda i, j, k: (k, j), TN, t // tk, 0)],
                [(SDS((m, n), out_dtype), (tm, tn), lambda i, j: (i, j), False)], epi=epi, acc_shape=(tm, tn),
                vmem=VMEM_BIG, hosts=hosts)[0]


def _ffn_bwd(tag, dy, nw, wf, saved, hosts=None):
    gate_t, up_t, down = wf
    x, h, sgh, dsgh, u, a = saved
    t, d = x.shape
    tm, tn = _row_tile(t, 256), D_FF

    def epi(accs, sgh, dsgh, u):
        da = accs[0]
        return da * (u.astype(F32) * dsgh.astype(F32)), da * sgh.astype(F32)

    col = lambda i, j: (i, j)
    dg, du = _fmm(
        tag + "_bwd_act", (t // tm, D_FF // tn),
        [(dy, (tm, d), lambda i, j, k: (i, 0), down, (tn, d), lambda i, j, k: (j, 0), NT, 1, 0)],
        [(SDS((t, D_FF), BF), (tm, tn), col, False)] * 2,
        extras=[(sgh, (tm, tn), col), (dsgh, (tm, tn), col), (u, (tm, tn), col)], epi=epi, hosts=hosts,
        vmem=VMEM_BIG)
    tm2 = _row_tile(t, 256)
    full = lambda i, j, k: (0, 0)
    extras, outs = _rms_bwd_tail(x, nw, dy, tm2)
    dx, dnw = _fmm(
        tag + "_bwd_dh", (t // tm2, 1),
        [(dg, (tm2, D_FF), lambda i, j, k: (i, 0), gate_t, (D_FF, d), full, NN, 1, 0),
         (du, (tm2, D_FF), lambda i, j, k: (i, 0), up_t, (D_FF, d), full, NN, 1, 0)],
        outs, extras=extras, epi=_rms_bwd_epi, vmem=VMEM_BIG, joint=True, hosts=hosts)
    half = D_FF // 2
    g_gate = _wgrad(tag + "_wg", dg, h, D_FF, d, tm=half, tn=d, hosts=hosts)
    if hosts is not None:
        hosts.ctx["g_" + tag + "_gate"] = g_gate
    g_up = _wgrad(tag + "_wu", du, h, D_FF, d, tm=half, tn=d, hosts=hosts)
    if hosts is not None:
        hosts.ctx["g_" + tag + "_up"] = g_up
    g_down = _wgrad(tag + "_wd", a, dy, D_FF, d, tm=half, tn=d, scale=0.5, hosts=hosts)
    return dx, dnw, (g_gate, g_up, g_down)


def _shift_down(cur, prev8, j):
    rolled = pltpu.roll(cur, j, 0)
    sub = lax.broadcasted_iota(jnp.int32, prev8.shape, 0)
    top = jnp.where(sub < j, pltpu.roll(prev8, j, 0), rolled[:8])
    return jnp.concatenate([top, rolled[8:]], axis=0)


def _shift_up(cur, next8, j):
    n = cur.shape[0]
    rolled = pltpu.roll(cur, n - j, 0)
    sub = lax.broadcasted_iota(jnp.int32, next8.shape, 0)
    bot = jnp.where(sub >= 8 - j, pltpu.roll(next8, 8 - j, 0), rolled[n - 8:])
    return jnp.concatenate([rolled[: n - 8], bot], axis=0)


HALO = 16


def _conv_fwd(xbc, w, b, seq):
    t, c = xbc.shape
    ts, tc = _row_tile(seq, 512), 1024
    tiles_per_seq = seq // ts
    hb = ts // HALO

    def fn(cur, prev, w, b):
        i = pl.program_id(1)
        cur = cur.astype(F32)
        prev8 = jnp.where(i % tiles_per_seq == 0, 0.0, prev.astype(F32)[HALO - 8:])
        out = b + w[3:4] * cur
        for j in range(1, CONV_K):
            out = out + w[3 - j:4 - j] * _shift_down(cur, prev8, j)
        return out, _silu(out)

    return _ew("ssm_conv_fwd", (c // tc, t // ts), fn,
               [(xbc, (ts, tc), lambda j, i: (i, j)),
                (xbc, (HALO, tc), lambda j, i: (jnp.maximum(i * hb - 1, 0), j)),
                (w, (CONV_K, tc), lambda j, i: (0, j)), (b, (1, tc), lambda j, i: (0, j))],
               [(SDS((t, c), BF), (ts, tc), lambda j, i: (i, j), False)] * 2, vmem=VMEM_BIG)


def _conv_bwd(tag, dxa, cpre, xbc, w, col0, seq):
    t, width = dxa.shape
    ts, tc = _row_tile(seq, 512), 1024
    tiles_per_seq = seq // ts
    hb = ts // HALO
    cb0 = col0 // tc
    n_halo_blocks = t // HALO

    def dsilu(cv, dv):
        cv = cv.astype(F32)
        s = _sigmoid(cv)
        return dv.astype(F32) * (s * (1.0 + cv * (1.0 - s)))

    def fn(dxa_c, dxa_n, c_c, c_n, x_c, x_p, w):
        i = pl.program_id(1)
        dc = dsilu(c_c, dxa_c)
        last = i % tiles_per_seq == tiles_per_seq - 1
        dc_n = jnp.where(last, 0.0, dsilu(c_n, dxa_n)[:8])
        dx = w[3:4] * dc
        for j in range(1, CONV_K):
            dx = dx + w[3 - j:4 - j] * _shift_up(dc, dc_n, j)
        cur = x_c.astype(F32)
        prev8 = jnp.where(i % tiles_per_seq == 0, 0.0, x_p.astype(F32)[HALO - 8:])
        rows = [jnp.sum(dc * cur, axis=0, keepdims=True)]
        for j in range(1, CONV_K):
            rows.append(jnp.sum(dc * _shift_down(cur, prev8, j), axis=0, keepdims=True))
        sub8 = lax.broadcasted_iota(jnp.int32, (8, dc.shape[1]), 0)
        dw = jnp.zeros((8, dc.shape[1]), F32)
        for kk in range(CONV_K):
            dw = jnp.where(sub8 == kk, rows[CONV_K - 1 - kk], dw)
        return dx, dw, jnp.sum(dc, axis=0, keepdims=True)

    nxt = lambda j, i: (jnp.minimum((i + 1) * hb, n_halo_blocks - 1), j)
    nxt_off = lambda j, i: (jnp.minimum((i + 1) * hb, n_halo_blocks - 1), j + cb0)
    return _ew(tag, (width // tc, t // ts), fn,
               [(dxa, (ts, tc), lambda j, i: (i, j)), (dxa, (HALO, tc), nxt),
                (cpre, (ts, tc), lambda j, i: (i, j + cb0)), (cpre, (HALO, tc), nxt_off),
                (xbc, (ts, tc), lambda j, i: (i, j + cb0)),
                (xbc, (HALO, tc), lambda j, i: (jnp.maximum(i * hb - 1, 0), j + cb0)),
                (w, (CONV_K, tc), lambda j, i: (0, j + cb0))],
               [(SDS((t, width), BF), (ts, tc), lambda j, i: (i, j), False),
                (SDS((8, width), F32), (8, tc), lambda j, i: (0, j), True),
                (SDS((1, width), F32), (1, tc), lambda j, i: (0, j), True)],
               acc_axes=(1,), vmem=VMEM_BIG)


def _ssd_chunk(xs, bm, cm, dtr, st, dtb, alog, dsk):
    ell = xs.shape[0]
    xs = xs.astype(F32)
    lane = lax.broadcasted_iota(jnp.int32, (ell, LANES), 1)
    sub = lax.broadcasted_iota(jnp.int32, (ell, LANES), 0)
    lane1 = lax.broadcasted_iota(jnp.int32, (1, LANES), 1)
    causal = sub >= lane
    dt = _softplus(dtr + dtb)
    da = dt * (-jnp.exp(alog))
    acs = _dot_hi(causal.astype(F32), da)
    acs_t = acs.T
    cb = _dotf(cm, bm, NT)
    lo = lane < 64
    ys, news = [], []
    for p in range(2):
        xp = xs[:, LANES * p:LANES * (p + 1)]
        sp = st[LANES * p:LANES * (p + 1), :]
        col, dtc, last, dsel = [], [], [], []
        y_diag = None
        for q in range(2):
            r = 2 * p + q
            col_r = jnp.sum(jnp.where(lane == r, acs, 0.0), axis=1, keepdims=True)
            row_r = jnp.sum(jnp.where(sub == r, acs_t, 0.0), axis=0, keepdims=True)
            dtc_r = jnp.sum(jnp.where(lane == r, dt, 0.0), axis=1, keepdims=True)
            decay = jnp.exp(jnp.where(causal, col_r - row_r, -jnp.inf))
            head = lo if q == 0 else jnp.logical_not(lo)
            d = _dotf(cb * decay, jnp.where(head, xp * dtc_r, 0.0), NN)
            y_diag = d if y_diag is None else y_diag + d
            col.append(col_r)
            dtc.append(dtc_r)
            last.append(jnp.sum(jnp.where(sub[:, :1] == ell - 1, col_r, 0.0), axis=0, keepdims=True))
            dsel.append(jnp.sum(jnp.where(lane1 == r, dsk, 0.0), axis=1, keepdims=True))
        y_off = _dotf(cm, sp, NT) * jnp.where(lo, jnp.exp(col[0]), jnp.exp(col[1]))
        xw = jnp.where(lo, xp * (dtc[0] * jnp.exp(last[0] - col[0])), xp * (dtc[1] * jnp.exp(last[1] - col[1])))
        new = sp * jnp.where(sub < 64, jnp.exp(last[0]), jnp.exp(last[1])) + _dotf(xw, bm, TN)
        ys.append(y_diag + y_off + jnp.where(lo, dsel[0], dsel[1]) * xp)
        news.append(new)
    return jnp.concatenate(ys, axis=1), jnp.concatenate(news, axis=0)


SSD_GP = 4
SSD_GP_BWD = 4
XW, GW = 2 * LANES * SSD_GP, LANES * SSD_GP


def _grp(ref, q, width):
    return ref[:, width * q:width * (q + 1)]


def _ssd_fwd(xa, dtr, dtb, alog, dsk, nb, seq):
    t = xa.shape[0]
    nc = seq // CHUNK
    row = lambda g, b, c: (b * nc + c, g)
    par = pl.BlockSpec((1, GW), lambda g, b, c: (0, g))
    b_off, c_off = D_INNER // GW, (D_INNER + SSM_GROUPS * SSM_STATE) // GW

    def body(xs, bm, cm, dtr, dtb, alog, dsk, y_ref, st_out, st_ref):
        @pl.when(pl.program_id(2) == 0)
        def _():
            st_ref[...] = jnp.zeros_like(st_ref)

        ins = [(_grp(xs, q, 2 * LANES), _grp(bm, q, LANES), _grp(cm, q, LANES), _grp(dtr, q, LANES), st_ref[q],
                _grp(dtb, q, LANES), _grp(alog, q, LANES), _grp(dsk, q, LANES)) for q in range(SSD_GP)]
        res = [_ssd_chunk(*a) for a in ins]
        for q in range(SSD_GP):
            st_out[q] = ins[q][4]
            y_ref[:, 2 * LANES * q:2 * LANES * (q + 1)] = res[q][0]
            st_ref[q] = res[q][1]

    specs = [pl.BlockSpec((CHUNK, XW), row),
             pl.BlockSpec((CHUNK, GW), lambda g, b, c: (b * nc + c, b_off + g)),
             pl.BlockSpec((CHUNK, GW), lambda g, b, c: (b * nc + c, c_off + g)),
             pl.BlockSpec((CHUNK, GW), row), par, par, par]
    return pl.pallas_call(
        body, name="ssd_fwd", grid=(SSM_GROUPS // SSD_GP, nb, nc), in_specs=specs,
        out_specs=[pl.BlockSpec((CHUNK, XW), row),
                   pl.BlockSpec((SSD_GP, None, None, 2 * LANES, LANES), lambda g, b, c: (g, b, c, 0, 0))],
        out_shape=[SDS((t, D_INNER), F32), SDS((SSM_GROUPS, nb, nc, 2 * LANES, LANES), F32)],
        scratch_shapes=[pltpu.VMEM((SSD_GP, 2 * LANES, LANES), F32)],
        compiler_params=_cparams(3),
    )(xa, xa, xa, dtr, dtb, alog, dsk)


def _ssd_bwd(xa, dtr, dtb, alog, dsk, states, dy, nb, seq):
    t = xa.shape[0]
    nc = seq // CHUNK
    rev = lambda c: nc - 1 - c
    row = lambda g, b, c: (b * nc + rev(c), g)
    gp = SSD_GP_BWD
    xw, gw = 2 * LANES * gp, LANES * gp
    par = pl.BlockSpec((1, gw), lambda g, b, c: (0, g))
    b_off, c_off = D_INNER // gw, (D_INNER + SSM_GROUPS * SSM_STATE) // gw

    def body(xs, bm, cm, dtr, dtb, alog, dsk, st_in, dy, dxs, dbm, dcm, ddtr, ddtb, dalog, ddsk, dst_ref):
        @pl.when(pl.program_id(2) == 0)
        def _():
            dst_ref[...] = jnp.zeros_like(dst_ref)

        ins = [(_grp(xs, q, 2 * LANES), _grp(bm, q, LANES), _grp(cm, q, LANES), _grp(dtr, q, LANES), st_in[q],
                _grp(dtb, q, LANES), _grp(alog, q, LANES), _grp(dsk, q, LANES)) for q in range(gp)]
        cts = [(_grp(dy, q, 2 * LANES), dst_ref[q]) for q in range(gp)]
        gs = [jax.vjp(_ssd_chunk, *a)[1](ct) for a, ct in zip(ins, cts)]
        for q, g in enumerate(gs):
            dxs[:, 2 * LANES * q:2 * LANES * (q + 1)] = g[0]
            lanes = slice(LANES * q, LANES * (q + 1))
            dbm[:, lanes] = g[1]
            dcm[:, lanes] = g[2]
            ddtr[:, lanes] = g[3]
            dst_ref[q] = g[4]
        first = (pl.program_id(1) == 0) & (pl.program_id(2) == 0)
        for o, k in ((ddtb, 5), (dalog, 6), (ddsk, 7)):
            v = jnp.concatenate([g[k] for g in gs], axis=1)

            @pl.when(first)
            def _(o=o, v=v):
                o[...] = v

            @pl.when(jnp.logical_not(first))
            def _(o=o, v=v):
                o[...] += v

    in_specs = [
        pl.BlockSpec((CHUNK, xw), row),
        pl.BlockSpec((CHUNK, gw), lambda g, b, c: (b * nc + rev(c), b_off + g)),
        pl.BlockSpec((CHUNK, gw), lambda g, b, c: (b * nc + rev(c), c_off + g)),
        pl.BlockSpec((CHUNK, gw), row), par, par, par,
        pl.BlockSpec((gp, None, None, 2 * LANES, LANES), lambda g, b, c: (g, b, rev(c), 0, 0)),
        pl.BlockSpec((CHUNK, xw), row),
    ]
    out_specs = [pl.BlockSpec((CHUNK, xw), row), pl.BlockSpec((CHUNK, gw), row), pl.BlockSpec((CHUNK, gw), row),
                 pl.BlockSpec((CHUNK, gw), row), par, par, par]
    out_shape = [SDS((t, D_INNER), BF), SDS((t, SSM_GROUPS * LANES), BF), SDS((t, SSM_GROUPS * LANES), BF),
                 SDS((t, SSM_GROUPS * LANES), F32)] + [SDS((1, SSM_GROUPS * LANES), F32)] * 3
    return pl.pallas_call(
        body, name="ssd_bwd", grid=(SSM_GROUPS // gp, nb, nc), in_specs=in_specs, out_specs=out_specs,
        out_shape=out_shape, scratch_shapes=[pltpu.VMEM((gp, 2 * LANES, LANES), F32)],
        compiler_params=_cparams(3, VMEM_BIG),
    )(xa, xa, xa, dtr, dtb, alog, dsk, states, dy)


def _gated_fn(y, z, w):
    g = y * _silu(z.astype(F32))
    return g * lax.rsqrt(jnp.mean(g * g, axis=-1, keepdims=True) + EPS) * w


def _gated_norm_fwd(y, z, w):
    t = y.shape[0]
    tm = _row_tile(t, 2048)
    blk = ((tm, NORM_GROUP), lambda g, i: (i, g))
    return _ew("ssm_gnorm_fwd", (SSM_GROUPS, t // tm), _gated_fn,
               [(y,) + blk, (z,) + blk, (w, (1, NORM_GROUP), lambda g, i: (0, g))],
               [(SDS((t, D_INNER), BF),) + blk + (False,)], vmem=VMEM_BIG)[0]


def _gated_norm_bwd(y, z, w, dout):
    t = y.shape[0]
    tm = _row_tile(t, 2048)
    blk = ((tm, NORM_GROUP), lambda g, i: (i, g))
    par = ((1, NORM_GROUP), lambda g, i: (0, g))

    def fn(y, z, w, dout):
        _, vjp = jax.vjp(_gated_fn, y, z, w)
        return vjp(dout.astype(F32))

    return _ew("ssm_gnorm_bwd", (SSM_GROUPS, t // tm), fn,
               [(y,) + blk, (z,) + blk, (w,) + par, (dout,) + blk],
               [(SDS((t, D_INNER), F32),) + blk + (False,), (SDS((t, D_INNER), BF),) + blk + (False,),
                (SDS((1, D_INNER), F32),) + par + (True,)],
               acc_axes=(1,), vmem=VMEM_BIG)


def _proj_nt(name, h, wt, n, out_dtype, tn=1024, hosts=None):
    t, kdim = h.shape
    tm = _row_tile(t, 1024)
    return _fmm(name, (t // tm, n // tn),
                [(h, (tm, kdim), lambda i, j, k: (i, 0), wt, (tn, kdim), lambda i, j, k: (j, 0), NT, 1, 0)],
                [(SDS((t, n), out_dtype), (tm, tn), lambda i, j: (i, j), False)], hosts=hosts, vmem=VMEM_BIG)[0]


def _seg_nn(name, parts, n, out_dtype, tk=256, hosts=None, rms=None, tm=512):
    t = parts[0][0].shape[0]
    tm = _row_tile(t, tm)
    pairs = []
    for a, w, row0 in parts:
        kp = a.shape[1]
        tkp = min(tk, kp)
        r0 = row0 // tkp
        pairs.append((a, (tm, tkp), lambda i, j, k: (i, k), w, (tkp, n), lambda i, j, k, r0=r0: (k + r0, 0),
                      NN, kp // tkp, 0))
    if rms is not None:
        extras, outs = _rms_bwd_tail(rms[0], rms[1], rms[2], tm)
        return _fmm(name, (t // tm, 1), pairs, outs, extras=extras, epi=_rms_bwd_epi, acc_shape=(tm, n), hosts=hosts,
                    vmem=VMEM_BIG)
    return _fmm(name, (t // tm, 1), pairs, [(SDS((t, n), out_dtype), (tm, n), lambda i, j: (i, 0), False)],
                acc_shape=(tm, n), hosts=hosts)[0]


def _ssm_fwd(x, nw, ws, small, nb, seq, hosts=None):
    t, d = x.shape
    h = _rms_fwd("ssm_rms", x, nw)
    z = _proj_nt("ssm_in_z", h, ws["z_t"], D_INNER, BF, hosts=hosts)
    xbc = _proj_nt("ssm_in_xbc", h, ws["xbc_t"], CONV_DIM, BF, hosts=hosts)
    dtr = _proj_nt("ssm_in_dt", h, ws["dt_t"], SSM_GROUPS * LANES, F32)
    cpre, xa = _conv_fwd(xbc, small["conv_w"], small["conv_b"], seq)
    y, states = _ssd_fwd(xa, dtr, small["dt_bias"], small["a_log"], small["d_skip"], nb, seq)
    gn = _gated_norm_fwd(y, z, small["ssm_norm_w"])
    tm = _row_tile(t, 512)
    out = _fmm("ssm_out", (t // tm, 1),
               [(gn, (tm, D_INNER), lambda i, j, k: (i, 0), ws["out"], (D_INNER, d), lambda i, j, k: (0, 0), NN, 1, 0)],
               [(SDS((t, d), F32), (tm, d), lambda i, j: (i, 0), False)],
               extras=[(x, (tm, d), lambda i, j: (i, 0))], epi=lambda accs, xr: xr + accs[0], hosts=hosts)[0]
    return out, (x, h, z, xbc, dtr, cpre, xa, y, states, gn)


def _ssm_bwd(dy, nw, ws, small, saved, nb, seq, hosts=None):
    x, h, z, xbc, dtr, cpre, xa, y, states, gn = saved
    t, d = x.shape
    dgn = _proj_nt("ssm_bwd_dgn", dy, ws["out"], D_INNER, BF, hosts=hosts)
    d_out = _wgrad("ssm_w_out_g", gn, dy, D_INNER, d, tm=D_INNER // 2, tn=d)
    dyssd, dz, d_normw = _gated_norm_bwd(y, z, small["ssm_norm_w"], dgn)
    dxs, dbm, dcm, ddtr, d_dtb, d_alog, d_dsk = _ssd_bwd(
        xa, dtr, small["dt_bias"], small["a_log"], small["d_skip"], states, dyssd, nb, seq)
    dxbc_x, dcw_x, dcb_x = _conv_bwd("ssm_conv_bwd_x", dxs, cpre, xbc, small["conv_w"], 0, seq)
    dxbc_b, dcw_b, dcb_b = _conv_bwd("ssm_conv_bwd_b", dbm, cpre, xbc, small["conv_w"], D_INNER, seq)
    dxbc_c, dcw_c, dcb_c = _conv_bwd("ssm_conv_bwd_c", dcm, cpre, xbc, small["conv_w"], D_INNER + 1024, seq)
    parts = [(dz, ws["z_t"], 0), (dxbc_x, ws["xbc_t"], 0), (dxbc_b, ws["xbc_t"], D_INNER),
             (dxbc_c, ws["xbc_t"], D_INNER + 1024), (ddtr, ws["dt_t"], 0)]
    dx, dnw = _seg_nn("ssm_bwd_dh", parts, d, BF, tk=1024, hosts=hosts, rms=(x, nw, dy))
    g = {
        "z_t": _wgrad("ssm_w_z_g", dz, h, D_INNER, d, tm=1024, tn=d, out_dtype=F32),
        "x_t": _wgrad("ssm_w_x_g", dxbc_x, h, D_INNER, d, tm=1024, tn=d, out_dtype=F32),
        "b_t": _wgrad("ssm_w_b_g", dxbc_b, h, 1024, d, tm=1024, tn=d, out_dtype=F32),
        "c_t": _wgrad("ssm_w_c_g", dxbc_c, h, 1024, d, tm=1024, tn=d, out_dtype=F32),
        "dt_t": _wgrad("ssm_w_dt_g", ddtr, h, 1024, d, tm=1024, tn=d, out_dtype=F32),
        "out": d_out,
        "conv_w": jnp.concatenate([dcw_x[:CONV_K], dcw_b[:CONV_K], dcw_c[:CONV_K]], axis=1),
        "conv_b": jnp.concatenate([dcb_x, dcb_b, dcb_c], axis=1),
        "dt_bias": d_dtb, "a_log": d_alog, "d_skip": d_dsk, "ssm_norm_w": d_normw,
    }
    return dx, dnw, g


def _lane_masks(shape):
    lane = lax.broadcasted_iota(jnp.int32, shape, len(shape) - 1)
    return lane < QK_NOPE, (lane >= QK_NOPE) & (lane < QK_DIM)


def _segment_matrix():
    p = np.zeros((HEAD_PAD, HEAD_PAD), np.float32)
    p[:QK_NOPE, :QK_NOPE] = 1.0
    p[QK_NOPE:QK_DIM, QK_NOPE:QK_DIM] = 1.0
    return jnp.asarray(p)


def _segment_rstd(x, seg):
    lane = lax.broadcasted_iota(jnp.int32, (1, x.shape[-1]), 1)
    inv_n = jnp.where(lane < QK_NOPE, 1.0 / QK_NOPE, 1.0 / QK_ROPE)
    xx = x * x
    hi = xx.astype(BF)
    lo = (xx - hi.astype(F32)).astype(BF)
    segb = seg.astype(BF)
    ss = _dotf(hi, segb, NN) + _dotf(lo, segb, NN)
    return lax.rsqrt(ss * inv_n + EPS)


def _rope_tables(positions_col):
    t = positions_col.shape[0]
    tm = _row_tile(t, 512)
    freq = np.zeros((1, HEAD_PAD), np.float32)
    inv = 1.0 / (ROPE_THETA ** (np.arange(0, QK_ROPE, 2, dtype=np.float32) / QK_ROPE))
    freq[0, QK_NOPE:QK_NOPE + QK_ROPE // 2] = inv
    freq[0, QK_NOPE + QK_ROPE // 2:QK_DIM] = inv
    sign = np.zeros((1, HEAD_PAD), np.float32)
    sign[0, QK_NOPE:QK_NOPE + QK_ROPE // 2] = -1.0
    sign[0, QK_NOPE + QK_ROPE // 2:QK_DIM] = 1.0

    def fn(pos, freq, sign):
        ang = pos.astype(F32) * freq
        nope, rope = _lane_masks(ang.shape)
        return jnp.where(nope, 1.0, jnp.where(rope, jnp.cos(ang), 0.0)), jnp.sin(ang) * sign

    row = lambda i: (i, 0)
    par = ((1, HEAD_PAD), lambda i: (0, 0))
    return _ew("mla_rope_tables", (t // tm,), fn,
               [(positions_col, (tm, 1), row), (jnp.asarray(freq),) + par, (jnp.asarray(sign),) + par],
               [(SDS((t, HEAD_PAD), F32), (tm, HEAD_PAD), row, False)] * 2)


def _swap_lanes(x):
    lane = lax.broadcasted_iota(jnp.int32, x.shape, x.ndim - 1)
    half = QK_ROPE // 2
    first = (lane >= QK_NOPE) & (lane < QK_NOPE + half)
    second = (lane >= QK_NOPE + half) & (lane < QK_DIM)
    n = x.shape[-1]
    return jnp.where(first, pltpu.roll(x, n - half, x.ndim - 1), jnp.where(second, pltpu.roll(x, half, x.ndim - 1), 0.0))


@jax.custom_vjp
def _swap_halves(x):
    return _swap_lanes(x)


_swap_halves.defvjp(lambda x: (_swap_lanes(x), None), lambda _, g: (_swap_lanes(g),))


def _rope(xn, cos, sin_signed):
    return xn * cos + _swap_halves(xn) * sin_signed


def _krope_fn(kr, w, cos, sin_signed, seg):
    _, rope = _lane_masks(kr.shape)
    xn = jnp.where(rope, kr * _segment_rstd(kr, seg) * w, 0.0)
    return _rope(xn, cos, sin_signed)


def _head_fn(q, kv, kr, cos, sin_signed, qn, kn, seg):
    nope, rope = _lane_masks(q.shape)
    qp = _rope(jnp.where(nope | rope, q * _segment_rstd(q, seg) * qn, 0.0), cos, sin_signed)
    kp = jnp.where(nope, kv * _segment_rstd(kv, seg) * kn, 0.0) + kr
    vp = jnp.where(nope, 0.0, kv)
    return qp, kp, vp


def _heads_fwd(q_raw, kv_raw, kr, cos, sin_signed, qn, kn, seg):
    nh, t, _ = q_raw.shape
    tm = _row_tile(t, 2048)
    hblk = ((None, tm, HEAD_PAD), lambda i, h: (h, i, 0))
    tblk = ((tm, HEAD_PAD), lambda i, h: (i, 0))
    par = ((1, HEAD_PAD), lambda i, h: (0, 0))
    sw = ((HEAD_PAD, HEAD_PAD), lambda i, h: (0, 0))
    def fn(*tiles):
        qp, kp, vp = _head_fn(*tiles)
        return qp * Q_PRESCALE, kp, vp

    return _ew("mla_heads_fwd", (t // tm, nh), fn,
               [(q_raw,) + hblk, (kv_raw,) + hblk, (kr,) + tblk, (cos,) + tblk, (sin_signed,) + tblk,
                (qn,) + par, (kn,) + par, (seg,) + sw],
               [(SDS((nh, t, HEAD_PAD), BF),) + hblk + (False,)] * 3, vmem=VMEM_BIG)


def _heads_bwd(q_raw, kv_raw, kr, cos, sin_signed, qn, kn, seg, dqp, dkp, dvp):
    nh, t, _ = q_raw.shape
    tm = _row_tile(t, 2048)
    hblk = ((None, tm, HEAD_PAD), lambda i, h: (h, i, 0))
    tblk = ((tm, HEAD_PAD), lambda i, h: (i, 0))
    par = ((1, HEAD_PAD), lambda i, h: (0, 0))
    sw = ((HEAD_PAD, HEAD_PAD), lambda i, h: (0, 0))

    def body(q, kv, kr, cos, sn, qn, kn, seg, dqp, dkp, dvp, dq, dkv, dkr, dqn, dkn):
        f = lambda q, kv, kr, qn, kn: _head_fn(q, kv, kr, cos[...], sn[...], qn, kn, seg[...])
        _, vjp = jax.vjp(f, q[...], kv[...], kr[...], qn[...], kn[...])
        g = vjp((dqp[...].astype(F32), dkp[...].astype(F32), dvp[...].astype(F32)))
        dq[...] = g[0].astype(dq.dtype)
        dkv[...] = g[1].astype(dkv.dtype)
        h0 = pl.program_id(1) == 0
        first = h0 & (pl.program_id(0) == 0)
        for o, v, c in ((dkr, g[2], h0), (dqn, g[3], first), (dkn, g[4], first)):
            @pl.when(c)
            def _(o=o, v=v):
                o[...] = v

            @pl.when(jnp.logical_not(c))
            def _(o=o, v=v):
                o[...] += v

    spec = lambda b: pl.BlockSpec(*b)
    return pl.pallas_call(
        body, name="mla_heads_bwd", grid=(t // tm, nh),
        in_specs=[spec(hblk), spec(hblk), spec(tblk), spec(tblk), spec(tblk), spec(par), spec(par), spec(sw),
                  spec(hblk), spec(hblk), spec(hblk)],
        out_specs=[spec(hblk), spec(hblk), spec(tblk), spec(par), spec(par)],
        out_shape=[SDS((nh, t, HEAD_PAD), BF), SDS((nh, t, HEAD_PAD), BF), SDS((t, HEAD_PAD), F32),
                   SDS((1, HEAD_PAD), F32), SDS((1, HEAD_PAD), F32)],
        compiler_params=_cparams(2, VMEM_BIG),
    )(q_raw, kv_raw, kr, cos, sin_signed, qn, kn, seg, dqp, dkp, dvp)


ATT_TILE = 512
ATT_SCALE = QK_DIM ** -0.5
LOG2E = 1.4426950408889634
LN2 = 0.6931471805599453
Q_PRESCALE = ATT_SCALE * LOG2E


def _flash_fwd(qs, k, v, nb, seq, hosts=None):
    nh, t, dh = qs.shape
    tq = _row_tile(seq, ATT_TILE)
    nq = seq // tq
    name = "mla_flash_fwd"
    comm = hosts.comm(name) if hosts is not None else None
    n_ci = len(comm.arrays) if comm is not None else 0
    n_co = len(comm.out_shapes) if comm is not None else 0

    def body(*refs):
        q_ref, k_ref, v_ref = refs[:3]
        ci_refs = refs[3:3 + n_ci]
        o_ref, lse_ref = refs[3 + n_ci:5 + n_ci]
        co_refs = refs[5 + n_ci:5 + n_ci + n_co]
        sem_refs = refs[5 + n_ci + n_co:]
        ids = (pl.program_id(0), pl.program_id(1), pl.program_id(2))
        if comm is not None:
            @pl.when((ids[0] == 0) & (ids[1] == 0) & (ids[2] == 0))
            def _():
                comm.start(ci_refs, co_refs, sem_refs)

        attend(q_ref, k_ref, v_ref, o_ref, lse_ref)

        if comm is not None:
            @pl.when((ids[0] == nh - 1) & (ids[1] == nb - 1) & (ids[2] == nq - 1))
            def _():
                comm.wait(ci_refs, co_refs, sem_refs)

    def attend(q_ref, k_ref, v_ref, o_ref, lse_ref):
        qi = pl.program_id(2)
        qt = q_ref[...]

        def tile(j, carry, diagonal):
            m, l, acc = carry
            rows = pl.ds(pl.multiple_of(j * tq, tq), tq)
            s = _dotf(qt, k_ref[rows, :], NT)
            if diagonal:
                r = lax.broadcasted_iota(jnp.int32, (tq, tq), 0)
                c = lax.broadcasted_iota(jnp.int32, (tq, tq), 1)
                s = jnp.where(c <= r, s, -jnp.inf)
            m_new = jnp.maximum(m, jnp.max(s, axis=-1, keepdims=True))
            alpha = jnp.exp2(m - m_new)
            p = jnp.exp2(s - m_new)
            return m_new, alpha * l + jnp.sum(p, axis=-1, keepdims=True), alpha * acc + _dotf(p, v_ref[rows, :], NN)

        init = (jnp.full((tq, 1), -jnp.inf, F32), jnp.zeros((tq, 1), F32), jnp.zeros((tq, dh), F32))
        carry = lax.fori_loop(0, qi, lambda j, c: tile(j, c, False), init)
        m, l, acc = tile(qi, carry, True)
        o_ref[...] = (acc / l).astype(o_ref.dtype)
        lse_ref[...] = m + jnp.log2(l)

    qblk = pl.BlockSpec((None, tq, dh), lambda h, b, i: (h, b * nq + i, 0))
    kblk = pl.BlockSpec((None, seq, dh), lambda h, b, i: (h, b, 0))
    hbm = pl.BlockSpec(memory_space=pl.ANY)
    res = pl.pallas_call(
        body, name=name, grid=(nh, nb, nq), in_specs=[qblk, kblk, kblk] + [hbm] * n_ci,
        out_specs=[qblk, pl.BlockSpec((None, tq, 1), lambda h, b, i: (h, b * nq + i, 0))] + [hbm] * n_co,
        out_shape=[SDS((nh, t, dh), BF), SDS((nh, t, 1), F32)] + (list(comm.out_shapes) if comm is not None else []),
        scratch_shapes=comm.sems if comm is not None else [],
        compiler_params=_cparams(3, VMEM_BIG),
    )(qs, k, v, *(comm.arrays if comm is not None else []))
    if comm is not None:
        hosts.done(name, res[2:])
    return res[0], res[1]


def _flash_bwd(qs, k, v, o, lse, do, nb, seq):
    nh, t, dh = qs.shape
    tq = _row_tile(seq, ATT_TILE)
    nq = seq // tq

    def row_of(col):
        return jnp.broadcast_to(col, (tq, LANES)).T[0:1, :]

    def body(q_ref, k_ref, v_ref, o_ref, lse_ref, do_ref, dq_ref, dk_ref, dv_ref, kt_sc, lrow_sc, drow_sc, dqt_sc):
        for c in range(nq):
            rows = pl.ds(c * tq, tq)
            kt_sc[c] = k_ref[rows, :].T
            delta = jnp.sum(do_ref[rows, :].astype(F32) * o_ref[rows, :].astype(F32), axis=-1, keepdims=True)
            drow_sc[c] = row_of(delta)
            lrow_sc[c] = row_of(lse_ref[rows, :])
        dqt_sc[...] = jnp.zeros_like(dqt_sc)

        def kv_step(j, _):
            rows_j = pl.ds(pl.multiple_of(j * tq, tq), tq)
            ks, vs, kt = k_ref[rows_j, :], v_ref[rows_j, :], kt_sc[j]

            def q_tile(i, carry, diagonal):
                dk, dv = carry
                rows_i = pl.ds(pl.multiple_of(i * tq, tq), tq)
                qt, dot_ = q_ref[rows_i, :], do_ref[rows_i, :]
                pt = jnp.exp2(_dotf(ks, qt, NT) - lrow_sc[i])
                if diagonal:
                    kk = lax.broadcasted_iota(jnp.int32, (tq, tq), 0)
                    qq = lax.broadcasted_iota(jnp.int32, (tq, tq), 1)
                    pt = jnp.where(kk <= qq, pt, 0.0)
                dst = (pt * (_dotf(vs, dot_, NT) - drow_sc[i])).astype(BF)
                dqt_sc[i] += _dotf(kt, dst, NN)
                return dk + _dotf(dst, qt, NN), dv + _dotf(pt, dot_, NN)

            zero = jnp.zeros((tq, dh), F32)
            carry = q_tile(j, (zero, zero), True)
            dk, dv = lax.fori_loop(j + 1, nq, lambda i, c: q_tile(i, c, False), carry)
            dk_ref[rows_j, :] = dk * LN2
            dv_ref[rows_j, :] = dv
            return 0

        lax.fori_loop(0, nq, kv_step, 0)
        for c in range(nq):
            dq_ref[pl.ds(c * tq, tq), :] = dqt_sc[c].T * ATT_SCALE

    full = pl.BlockSpec((None, seq, dh), lambda h, b: (h, b, 0))
    sfull = pl.BlockSpec((None, seq, 1), lambda h, b: (h, b, 0))
    return pl.pallas_call(
        body, name="mla_flash_bwd", grid=(nh, nb), in_specs=[full, full, full, full, sfull, full],
        out_specs=[full, full, full], out_shape=[SDS((nh, t, dh), F32)] * 3,
        scratch_shapes=[pltpu.VMEM((nq, dh, tq), BF), pltpu.VMEM((nq, 1, tq), F32), pltpu.VMEM((nq, 1, tq), F32),
                        pltpu.VMEM((nq, dh, tq), F32)],
        compiler_params=_cparams(2, VMEM_BIG),
    )(qs, k, v, o, lse, do)


def _heads_nt(name, a, wt, out_dtype):
    t, kdim = a.shape
    tm = _row_tile(t, 512)
    nw = MLA_HEADS * HEAD_PAD

    def body(a_ref, w_ref, o_ref):
        r = _dotf(a_ref[...], w_ref[...], NT)
        for h in range(MLA_HEADS):
            o_ref[h] = r[:, HEAD_PAD * h:HEAD_PAD * (h + 1)].astype(o_ref.dtype)

    return pl.pallas_call(
        body, name=name, grid=(t // tm,),
        in_specs=[pl.BlockSpec((tm, kdim), lambda i: (i, 0)), pl.BlockSpec((nw, kdim), lambda i: (0, 0))],
        out_specs=pl.BlockSpec((MLA_HEADS, tm, HEAD_PAD), lambda i: (0, i, 0)),
        out_shape=SDS((MLA_HEADS, t, HEAD_PAD), out_dtype), compiler_params=_cparams(1, VMEM_BIG),
    )(a, wt)


def _all_heads(a_ref):
    return jnp.concatenate([a_ref[h] for h in range(MLA_HEADS)], axis=1)


def _heads_nn(name, a, w, n, out_dtype, res=None):
    t = a.shape[1]
    tm = _row_tile(t, 512)
    nw = MLA_HEADS * HEAD_PAD

    def body(*refs):
        a_ref, w_ref, o_ref = refs[0], refs[1], refs[-1]
        r = _dotf(_all_heads(a_ref), w_ref[...], NN)
        if res is not None:
            r = r + refs[2][...]
        o_ref[...] = r.astype(o_ref.dtype)

    row = pl.BlockSpec((tm, n), lambda i: (i, 0))
    in_specs = [pl.BlockSpec((MLA_HEADS, tm, HEAD_PAD), lambda i: (0, i, 0)), pl.BlockSpec((nw, n), lambda i: (0, 0))]
    args = [a, w]
    if res is not None:
        in_specs.append(row)
        args.append(res)
    return pl.pallas_call(body, name=name, grid=(t // tm,), in_specs=in_specs, out_specs=row,
                          out_shape=SDS((t, n), out_dtype), compiler_params=_cparams(1, VMEM_BIG))(*args)


def _heads_wgrad(name, a, b, n):
    t = b.shape[0]
    tk = _row_tile(t, 512)
    nw = MLA_HEADS * HEAD_PAD
    steps = t // tk

    def body(a_ref, b_ref, o_ref, acc):
        k = pl.program_id(0)

        @pl.when(k == 0)
        def _():
            acc[...] = jnp.zeros_like(acc)

        acc[...] += _dotf(_all_heads(a_ref), b_ref[...], TN)

        @pl.when(k == steps - 1)
        def _():
            o_ref[...] = acc[...]

    return pl.pallas_call(
        body, name=name, grid=(steps,),
        in_specs=[pl.BlockSpec((MLA_HEADS, tk, HEAD_PAD), lambda k: (0, k, 0)), pl.BlockSpec((tk, n), lambda k: (k, 0))],
        out_specs=pl.BlockSpec((nw, n), lambda k: (0, 0)), out_shape=SDS((nw, n), F32),
        scratch_shapes=[pltpu.VMEM((nw, n), F32)], compiler_params=_cparams(1, VMEM_BIG),
    )(a, b)


PM_CKV, PM_KR, PM_CQ = 0, KV_LORA, KV_LORA + HEAD_PAD
PM_DIM = KV_LORA + HEAD_PAD + Q_LORA


def _lat_specs(t, tm):
    return (((tm, KV_LORA), lambda i: (i, 0)), ((tm, HEAD_PAD), lambda i: (i, PM_KR // HEAD_PAD)),
            ((tm, Q_LORA), lambda i: (i, PM_CQ // Q_LORA)))


def _mla_fwd(x, nw, wm, small, tables, nb, seq, hosts=None):
    t, d = x.shape
    cos, sin_signed, seg = tables
    h = _rms_fwd("mla_rms", x, nw)
    pm = _proj_nt("mla_in", h, wm["in_t"], PM_DIM, F32, tn=PM_DIM // 3)
    tm = _row_tile(t, 512)
    ckv_s, kr_s, cq_s = _lat_specs(t, tm)
    row = lambda i: (i, 0)
    par = lambda n: ((1, n), lambda i: (0, 0))
    ckvn = _ew("mla_ckv_norm", (t // tm,), _rms_fn, [(pm,) + ckv_s, (small["kv_a_norm"],) + par(KV_LORA)],
               [(SDS((t, KV_LORA), BF), (tm, KV_LORA), row, False)])[0]
    cqn = _ew("mla_cq_norm", (t // tm,), _rms_fn, [(pm,) + cq_s, (small["q_a_norm"],) + par(Q_LORA)],
              [(SDS((t, Q_LORA), BF), (tm, Q_LORA), row, False)])[0]
    tb = ((tm, HEAD_PAD), row)
    kr = _ew("mla_krope", (t // tm,), _krope_fn,
             [(pm,) + kr_s, (small["k_norm"],) + par(HEAD_PAD), (cos,) + tb, (sin_signed,) + tb,
              (seg, (HEAD_PAD, HEAD_PAD), lambda i: (0, 0))],
             [(SDS((t, HEAD_PAD), F32),) + tb + (False,)])[0]
    q_raw = _heads_nt("mla_q_b", cqn, wm["qb_t"], F32)
    kv_raw = _heads_nt("mla_kv_b", ckvn, wm["kvb_t"], F32)
    qp, kp, vp = _heads_fwd(q_raw, kv_raw, kr, cos, sin_signed, small["q_norm"], small["k_norm"], seg)
    o, lse = _flash_fwd(qp, kp, vp, nb, seq, hosts=hosts)
    out = _heads_nn("mla_out", o, wm["out"], d, F32, res=x)
    return out, (x, h, pm, ckvn, cqn, kr, q_raw, kv_raw, qp, kp, vp, o, lse)


def _mla_bwd(dy, nw, wm, small, tables, saved, nb, seq):
    x, h, pm, ckvn, cqn, kr, q_raw, kv_raw, qp, kp, vp, o, lse = saved
    t, d = x.shape
    cos, sin_signed, seg = tables
    do = _heads_nt("mla_bwd_do", dy, wm["out"], BF)
    g_out = _heads_wgrad("mla_w_out_g", o, dy, d)
    dqp, dkp, dvp = _flash_bwd(qp, kp, vp, o, lse, do, nb, seq)
    dq_raw, dkv_raw, dkr, d_qn, d_kn = _heads_bwd(q_raw, kv_raw, kr, cos, sin_signed, small["q_norm"],
                                                   small["k_norm"], seg, dqp, dkp, dvp)
    dcqn = _heads_nn("mla_bwd_dcq", dq_raw, wm["qb_t"], Q_LORA, F32)
    dckvn = _heads_nn("mla_bwd_dckv", dkv_raw, wm["kvb_t"], KV_LORA, F32)
    g_qb = _heads_wgrad("mla_w_qb_g", dq_raw, cqn, Q_LORA)
    g_kvb = _heads_wgrad("mla_w_kvb_g", dkv_raw, ckvn, KV_LORA)
    tm = _row_tile(t, 512)
    ckv_s, kr_s, cq_s = _lat_specs(t, tm)
    row = lambda i: (i, 0)
    par = lambda n: ((1, n), lambda i: (0, 0))

    def rms_b(xv, w, dv):
        _, vjp = jax.vjp(_rms_fn, xv, w)
        return vjp(dv)

    dckv, d_kva = _ew("mla_ckv_norm_bwd", (t // tm,), rms_b,
                      [(pm,) + ckv_s, (small["kv_a_norm"],) + par(KV_LORA), (dckvn, (tm, KV_LORA), row)],
                      [(SDS((t, KV_LORA), BF), (tm, KV_LORA), row, False),
                       (SDS((1, KV_LORA), F32),) + par(KV_LORA) + (True,)], acc_axes=(0,))
    dcq, d_qa = _ew("mla_cq_norm_bwd", (t // tm,), rms_b,
                    [(pm,) + cq_s, (small["q_a_norm"],) + par(Q_LORA), (dcqn, (tm, Q_LORA), row)],
                    [(SDS((t, Q_LORA), BF), (tm, Q_LORA), row, False),
                     (SDS((1, Q_LORA), F32),) + par(Q_LORA) + (True,)], acc_axes=(0,))
    tb = ((tm, HEAD_PAD), row)

    def kr_b(krv, w, cosv, sinv, sw, dv):
        _, vjp = jax.vjp(lambda a, b: _krope_fn(a, b, cosv, sinv, sw), krv, w)
        return vjp(dv)

    dkr_raw, d_kn2 = _ew("mla_krope_bwd", (t // tm,), kr_b,
                         [(pm,) + kr_s, (small["k_norm"],) + par(HEAD_PAD), (cos,) + tb, (sin_signed,) + tb,
                          (seg, (HEAD_PAD, HEAD_PAD), lambda i: (0, 0)), (dkr,) + tb],
                         [(SDS((t, HEAD_PAD), BF),) + tb + (False,),
                          (SDS((1, HEAD_PAD), F32),) + par(HEAD_PAD) + (True,)], acc_axes=(0,))
    dx, dnw = _seg_nn("mla_bwd_dh", [(dckv, wm["in_t"], PM_CKV), (dkr_raw, wm["in_t"], PM_KR),
                                     (dcq, wm["in_t"], PM_CQ)], d, BF, tk=128, rms=(x, nw, dy))
    g = {
        "in_ckv_t": _wgrad("mla_w_in_ckv_g", dckv, h, KV_LORA, d, tm=KV_LORA, tn=d, out_dtype=F32),
        "in_kr_t": _wgrad("mla_w_in_kr_g", dkr_raw, h, HEAD_PAD, d, tm=HEAD_PAD, tn=d, out_dtype=F32),
        "in_cq_t": _wgrad("mla_w_in_cq_g", dcq, h, Q_LORA, d, tm=Q_LORA, tn=d, out_dtype=F32),
        "qb_t": g_qb, "kvb_t": g_kvb, "out": g_out,
        "q_a_norm": d_qa, "kv_a_norm": d_kva, "q_norm": d_qn, "k_norm": d_kn + d_kn2,
    }
    return dx, dnw, g


def _mesh_pos():
    return lax.axis_index("x"), lax.axis_index("y"), lax.axis_index("c")


def _peer(pos, k):
    x, y, c = pos
    return (x ^ ((k >> 2) & 1), y ^ ((k >> 1) & 1), c ^ (k & 1))


def _flat(pos):
    return 4 * pos[0] + 2 * pos[1] + pos[2]


def _slab(ref, axis, start, size):
    idx = [slice(None)] * axis + [pl.ds(start, size)]
    return ref.at[tuple(idx)]


class _Exchange:
    def __init__(self, kind, items):
        self.kind = kind
        self.axes = [ax for _, ax in items]
        self.arrays = [a for a, _ in items]
        n = len(items)
        self.out_shapes = []
        self.sizes = []
        for a, ax in items:
            shp = list(a.shape)
            if kind == "gather":
                self.sizes.append(shp[ax])
                shp[ax] *= N_DEV
                self.out_shapes.append(SDS(tuple(shp), a.dtype))
            else:
                shp[ax] //= N_DEV
                self.sizes.append(shp[ax])
                self.out_shapes.append(SDS((N_DEV,) + tuple(shp), a.dtype))
        self.sems = [pltpu.SemaphoreType.DMA((n, N_DEV - 1)), pltpu.SemaphoreType.DMA((n, N_DEV - 1)),
                     pltpu.SemaphoreType.DMA((n,))]

    def _copies(self, srcs, dsts, sems, with_arrivals=True):
        send_sems, recv_sems, local_sems = sems
        pos = _mesh_pos()
        me = _flat(pos)
        local, sends, recvs = [], [], []
        for t, (src, dst) in enumerate(zip(srcs, dsts)):
            ax, sz = self.axes[t], self.sizes[t]
            if self.kind == "gather":
                mine = _slab(dst, ax, me * sz, sz)
                local.append(pltpu.make_async_copy(src, mine, local_sems.at[t]))
            else:
                mine = dst.at[me]
                local.append(pltpu.make_async_copy(_slab(src, ax, me * sz, sz), mine, local_sems.at[t]))
            for k in range(1, N_DEV):
                peer = _peer(pos, k)
                there = _flat(peer)
                if self.kind == "gather":
                    out_src, landing = src, _slab(dst, ax, there * sz, sz)
                else:
                    out_src, landing = _slab(src, ax, there * sz, sz), dst.at[there]
                common = dict(send_sem=send_sems.at[t, k - 1], recv_sem=recv_sems.at[t, k - 1], device_id=peer,
                              device_id_type=pl.DeviceIdType.MESH)
                sends.append(pltpu.make_async_remote_copy(src_ref=out_src, dst_ref=mine, **common))
                if with_arrivals:
                    recvs.append(pltpu.make_async_remote_copy(src_ref=out_src, dst_ref=landing, **common))
        return local, sends, recvs

    def start(self, srcs, dsts, sems):
        local, sends, _ = self._copies(srcs, dsts, sems, with_arrivals=False)
        for cp in local + sends:
            cp.start()

    def wait(self, srcs, dsts, sems):
        local, sends, recvs = self._copies(srcs, dsts, sems)
        for rc in recvs:
            rc.wait_recv()
        for rc in sends:
            rc.wait_send()
        for cp in local:
            cp.wait()

    def run(self, name):
        n = len(self.arrays)

        def body(*refs):
            srcs, dsts, sems = refs[:n], refs[n:2 * n], refs[2 * n:]
            self.start(srcs, dsts, sems)
            self.wait(srcs, dsts, sems)

        hbm = pl.BlockSpec(memory_space=pl.ANY)
        return pl.pallas_call(body, name=name, in_specs=[hbm] * n, out_specs=[hbm] * n, out_shape=self.out_shapes,
                              scratch_shapes=self.sems)(*self.arrays)


def _adam_math(w, g, m, v):
    m = ADAM_B1 * m + (1.0 - ADAM_B1) * g
    v = ADAM_B2 * v + (1.0 - ADAM_B2) * (g * g)
    m_hat = m / (1.0 - ADAM_B1 ** ADAM_STEP)
    v_hat = v / (1.0 - ADAM_B2 ** ADAM_STEP)
    delta = -ADAM_LR * (m_hat / (jnp.sqrt(v_hat) + ADAM_EPS) + ADAM_WD * w)
    return delta, m, v


def _adam(name, land, land_blk, land_idx, w, m, v, transposed, ck):
    n, r, c = w.shape
    wblk = ((None, ck, c), lambda a, i: (a, i, 0))

    def fn(parts, w, m, v):
        g = parts[0].astype(F32)
        for s in range(1, N_DEV):
            g = g + parts[s].astype(F32)
        if transposed:
            g = g.T
        delta, m2, v2 = _adam_math(w, g, m, v)
        return g, delta, m2, v2

    return _ew(name, (n, r // ck), fn,
               [(land, land_blk, land_idx), (w,) + wblk, (m,) + wblk, (v,) + wblk],
               [(SDS(w.shape, F32),) + wblk + (False,)] * 4, vmem=VMEM_BIG)


def _prep_ffn(gate, up, down):
    def body(g, u, dn, o):
        o[0] = g[...].T.astype(BF)
        o[1] = u[...].T.astype(BF)
        o[2] = dn[...].astype(BF)

    cblk = pl.BlockSpec((None, None, D_MODEL, FF_SHARD), lambda l, i: (l, i, 0, 0))
    rblk = pl.BlockSpec((None, None, FF_SHARD, D_MODEL), lambda l, i: (l, i, 0, 0))
    return pl.pallas_call(
        body, name="prep_ffn", grid=(2, 2), in_specs=[cblk, cblk, rblk],
        out_specs=pl.BlockSpec((3, FF_SHARD, D_MODEL), lambda l, i: (2 * l + i, 0, 0)),
        out_shape=SDS((12, FF_SHARD, D_MODEL), BF), compiler_params=_cparams(2, VMEM_BIG),
    )(gate, up, down)


def _transpose_cast(name, w, dtype):
    def body(a, o):
        o[...] = a[...].T.astype(dtype)

    r, c = w.shape
    return pl.pallas_call(body, name=name, out_shape=SDS((c, r), dtype),
                          compiler_params=pltpu.CompilerParams(vmem_limit_bytes=VMEM_BIG))(w)


SMALL_SHARDED = (("norm_w", 6 * 128), ("conv_w", CONV_K * 512), ("q_a_norm", 48), ("kv_a_norm", 32))
SMALL_PACK = 3072


def _dyn(a, start, size):
    return lax.dynamic_slice_in_dim(a, start, size, axis=a.ndim - 1)


def _layout_ssm(ssm_in_t, ssm_out_all):
    d = ssm_in_t.shape[1]
    dt_rows = ssm_in_t[D_INNER + CONV_DIM:].reshape(SSM_GROUPS, SSM_HPG, d)
    return {"z_t": ssm_in_t[:D_INNER], "xbc_t": ssm_in_t[D_INNER:D_INNER + CONV_DIM],
            "dt_t": jnp.pad(dt_rows, ((0, 0), (0, LANES - SSM_HPG), (0, 0))).reshape(SSM_GROUPS * LANES, d),
            "out": ssm_out_all}


def _layout_mla(mla_in_all, qb_all, kvb_all, mla_out_all):
    d = mla_out_all.shape[1]
    in_t = mla_in_all.T
    kr_rows = jnp.pad(in_t[Q_LORA + KV_LORA:], ((QK_NOPE, HEAD_PAD - QK_DIM), (0, 0)))
    qb_heads = jnp.pad(qb_all.reshape(MLA_HEADS, QK_DIM, Q_LORA), ((0, 0), (0, HEAD_PAD - QK_DIM), (0, 0)))
    out_heads = jnp.pad(mla_out_all.reshape(MLA_HEADS, 64, d), ((0, 0), (64, 0), (0, 0)))
    return {"in_t": jnp.concatenate([in_t[Q_LORA:Q_LORA + KV_LORA], kr_rows, in_t[:Q_LORA]], axis=0),
            "qb_t": qb_heads.reshape(MLA_HEADS * HEAD_PAD, Q_LORA), "kvb_t": kvb_all,
            "out": out_heads.reshape(MLA_HEADS * HEAD_PAD, d)}


def _layout_small(conv_w, conv_b, dt_bias, a_log, d_skip, ssm_norm_w, q_a_norm, kv_a_norm, q_norm, k_norm):
    lane_heads = lambda p: jnp.pad(p.reshape(SSM_GROUPS, SSM_HPG), ((0, 0), (0, LANES - SSM_HPG))).reshape(1, -1)
    pad_head = lambda p: jnp.pad(p.reshape(1, QK_DIM), ((0, 0), (0, HEAD_PAD - QK_DIM)))
    return {"conv_w": conv_w, "conv_b": conv_b, "dt_bias": lane_heads(dt_bias), "a_log": lane_heads(a_log),
            "d_skip": lane_heads(d_skip), "ssm_norm_w": ssm_norm_w, "q_a_norm": q_a_norm, "kv_a_norm": kv_a_norm,
            "q_norm": pad_head(q_norm), "k_norm": pad_head(k_norm)}


class _Plan:
    def __init__(self, ctx):
        self.ctx = ctx
        self.make = {}
        self.land = {}

    def ride(self, host, make, land):
        assert host not in self.make, host
        self.make[host] = make
        self.land[host] = land

    def comm(self, host):
        return self.make[host](self.ctx) if host in self.make else None

    def done(self, host, results):
        self.land[host](results, self.ctx)


def _local_step(x, positions, loss_target, ctx, plan=None):
    nb, seq, d = x.shape
    t = nb * seq
    xf = x.reshape(t, d)
    norm = ctx["norm"]
    tables = list(_rope_tables(positions.reshape(t, 1))) + [_segment_matrix()]
    x1, s_f0 = _ffn_fwd("ffn0", xf, norm[0, 0], ctx["ffn0"], plan)
    x2, s_ssm = _ssm_fwd(x1, norm[0, 1], ctx["ws"], ctx["small"], nb, seq, plan)
    x3, s_f1 = _ffn_fwd("ffn1", x2, norm[0, 2], ctx["ffn1"], plan)
    x4, s_f2 = _ffn_fwd("ffn2", x3, norm[1, 0], ctx["ffn2"], plan)
    x5, s_mla = _mla_fwd(x4, norm[1, 1], ctx["wm"], ctx["small"], tables, nb, seq, plan)
    x6, s_f3 = _ffn_fwd("ffn3", x5, norm[1, 2], ctx["ffn3"], plan)
    dy, loss_cols = _loss_and_grad(x6, loss_target.reshape(t, d))

    dx5, dn12, ctx["g_ffn3"] = _ffn_bwd("ffn3", dy, norm[1, 2], ctx["ffn3"], s_f3, plan)
    dx4, dn11, ctx["g_mla"] = _mla_bwd(dx5, norm[1, 1], ctx["wm"], ctx["small"], tables, s_mla, nb, seq)
    dx3, dn10, ctx["g_ffn2"] = _ffn_bwd("ffn2", dx4, norm[1, 0], ctx["ffn2"], s_f2, plan)
    dx2, dn02, ctx["g_ffn1"] = _ffn_bwd("ffn1", dx3, norm[0, 2], ctx["ffn1"], s_f1, plan)
    dx1, dn01, ctx["g_ssm"] = _ssm_bwd(dx2, norm[0, 1], ctx["ws"], ctx["small"], s_ssm, nb, seq, plan)
    dx0, dn00, ctx["g_ffn0"] = _ffn_bwd("ffn0", dx1, norm[0, 0], ctx["ffn0"], s_f0, plan)
    return loss_cols, dx0.reshape(nb, seq, d), (dn00, dn01, dn02, dn10, dn11, dn12)


def kernel(x, positions, norm_w, ffn_w_gate, ffn_w_up, ffn_w_down, ssm_w_in, ssm_conv_w, ssm_conv_b, ssm_dt_bias, ssm_a_log, ssm_d, ssm_norm_w, ssm_w_out, mla_w_in, mla_q_a_norm, mla_kv_a_norm, mla_w_q_b, mla_w_kv_b, mla_q_norm, mla_k_norm, mla_w_out, loss_target, m_norm_w, m_ffn_w_gate, m_ffn_w_up, m_ffn_w_down, m_ssm_w_in, m_ssm_conv_w, m_ssm_conv_b, m_ssm_dt_bias, m_ssm_a_log, m_ssm_d, m_ssm_norm_w, m_ssm_w_out, m_mla_w_in, m_mla_q_a_norm, m_mla_kv_a_norm, m_mla_w_q_b, m_mla_w_kv_b, m_mla_q_norm, m_mla_k_norm, m_mla_w_out, v_norm_w, v_ffn_w_gate, v_ffn_w_up, v_ffn_w_down, v_ssm_w_in, v_ssm_conv_w, v_ssm_conv_b, v_ssm_dt_bias, v_ssm_a_log, v_ssm_d, v_ssm_norm_w, v_ssm_w_out, v_mla_w_in, v_mla_q_a_norm, v_mla_kv_a_norm, v_mla_w_q_b, v_mla_w_kv_b, v_mla_q_norm, v_mla_k_norm, v_mla_w_out):
    nb, seq, d = x.shape
    t = nb * seq
    me = _flat(_mesh_pos())

    ffn_loc = _prep_ffn(ffn_w_gate, ffn_w_up, ffn_w_down)
    ssm_in_loc = _transpose_cast("prep_ssm_in", ssm_w_in[0], BF)
    ssm_out_loc = ssm_w_out[0].astype(BF)
    mla_in_loc, mla_out_loc = mla_w_in[0].astype(BF), mla_w_out[0].astype(BF)
    qb_loc = _transpose_cast("prep_q_b", mla_w_q_b[0], BF)
    kvb_loc = _transpose_cast("prep_kv_b", mla_w_kv_b[0], BF)
    small_loc = jnp.concatenate([norm_w.reshape(-1), ssm_conv_w.reshape(-1), mla_q_a_norm.reshape(-1),
                                 mla_kv_a_norm.reshape(-1)])
    small_loc = jnp.pad(small_loc, (0, SMALL_PACK - small_loc.shape[0])).reshape(SMALL_PACK // LANES, LANES)

    wloc = lambda n: [(ffn_loc[3 * n + k], 0) for k in range(3)]
    g0, u0, small_all = _Exchange("gather", wloc(0)[0:2] + [(small_loc, 0)]).run("gather_first")
    sm = small_all.reshape(N_DEV, SMALL_PACK)
    conv_w_full = sm[:, 768:768 + 2048].reshape(N_DEV, CONV_K, 512).transpose(1, 0, 2).reshape(CONV_K, CONV_DIM)
    ctx = {"ffn0": [g0, u0, None], "ffn1": [None] * 3, "ffn2": [None] * 3,
           "norm": sm[:, :768].reshape(N_DEV, 6, 128).transpose(1, 0, 2).reshape(2, 3, 1, d),
           "small": _layout_small(conv_w_full, ssm_conv_b, ssm_dt_bias, ssm_a_log, ssm_d, ssm_norm_w,
                                  sm[:, 2816:2864].reshape(1, Q_LORA), sm[:, 2864:2896].reshape(1, KV_LORA),
                                  mla_q_norm, mla_k_norm)}
    plan = _Plan(ctx)

    def gather_on(host, items, land):
        plan.ride(host, lambda c: _Exchange("gather", items), land)

    def put_w(key, ks):
        def land(r, c):
            for k, arr in zip(ks, r):
                c[key][k] = arr
        return land

    half_rows = SSM_IN_SHARD // 2
    ssm_in_a, ssm_in_b = (ssm_in_loc[k * half_rows:(k + 1) * half_rows].reshape(half_rows // 2, 16, LANES)
                          for k in range(2))

    def land_first(r, c):
        c["ffn0"][2] = r[0]
        c["ssm_in_a"] = r[1].reshape(N_DEV, half_rows, d)

    def land_ssm_in(r, c):
        both = jnp.concatenate([c["ssm_in_a"], r[0].reshape(N_DEV, half_rows, d)], axis=1)
        c["ws"] = _layout_ssm(both.reshape(SSM_IN_DIM, d), None)

    def land_ssm_out(r, c):
        c["ffn1"][1] = r[0]
        c["ws"]["out"] = r[1]

    gather_on("ffn0_up", wloc(0)[2:3] + [(ssm_in_a, 0)], land_first)
    gather_on("ffn0_down", [(ssm_in_b, 0)], land_ssm_in)
    gather_on("ssm_in_z", wloc(1)[0:1], put_w("ffn1", (0,)))
    gather_on("ssm_in_xbc", wloc(1)[1:2] + [(ssm_out_loc, 0)], land_ssm_out)
    gather_on("ssm_out", wloc(1)[2:3], put_w("ffn1", (2,)))
    gather_on("ffn1_up", wloc(2)[0:2], put_w("ffn2", (0, 1)))
    gather_on("ffn1_down", wloc(2)[2:3], put_w("ffn2", (2,)))
    gather_on("ffn2_up", [(mla_in_loc, 0), (qb_loc, 0), (kvb_loc, 0), (mla_out_loc, 0)],
              lambda r, c: c.update(wm=_layout_mla(r[0], r[1], r[2], r[3])))
    gather_on("mla_flash_fwd", wloc(3), lambda r, c: c.update(ffn3=tuple(r)))

    heads_of = lambda a: a.reshape(SSM_GROUPS, LANES, -1)[:, :SSM_HPG].reshape(SSM_HEADS, -1)

    def mla_grad_items(c):
        g = c["g_mla"]
        g_in = jnp.concatenate([g["in_cq_t"], g["in_ckv_t"], g["in_kr_t"][QK_NOPE:QK_DIM]], axis=0).T
        g_qb = g["qb_t"].reshape(MLA_HEADS, HEAD_PAD, Q_LORA)[:, :QK_DIM].reshape(MLA_HEADS * QK_DIM, Q_LORA)
        g_out = g["out"].reshape(MLA_HEADS, HEAD_PAD, d)[:, 64:].reshape(MLA_HEADS * 64, d)
        return [(a.astype(BF), 0) for a in (g_out, g_in, g_qb, g["kvb_t"])]

    def ssm_in_grad(which):
        def items(c):
            if "g_ssm_in_t" not in c:
                g = c["g_ssm"]
                g_in_t = jnp.concatenate([g["z_t"], g["x_t"], g["b_t"], g["c_t"], heads_of(g["dt_t"])], axis=0)
                c["g_ssm_in_t"] = g_in_t.astype(BF).reshape(N_DEV, SSM_IN_SHARD, d)
            part = c["g_ssm_in_t"][:, which * half_rows:(which + 1) * half_rows]
            return [(part.reshape(N_DEV * half_rows // 2, 16, LANES), 0)]
        return items

    def scatter_on(host, items_of, keys):
        plan.ride(host, lambda c: _Exchange("scatter", items_of(c)),
                  lambda r, c: c.update(dict(zip(keys, r))))

    of = lambda key, k: (lambda c: [(c[key][k], 0)])
    scatter_on("ffn2_bwd_act", of("g_ffn3", 0), ("l3_gate",))
    scatter_on("ffn2_bwd_dh", of("g_ffn3", 1), ("l3_up",))
    scatter_on("ffn2_wg", of("g_ffn3", 2), ("l3_down",))
    scatter_on("ffn2_wu", mla_grad_items, ("l_mla_out", "l_mla_in", "l_qb", "l_kvb"))
    scatter_on("ffn1_bwd_act", of("g_ffn2", 0), ("l2_gate",))
    scatter_on("ffn1_bwd_dh", of("g_ffn2", 1), ("l2_up",))
    scatter_on("ffn1_wg", of("g_ffn2", 2), ("l2_down",))
    scatter_on("ssm_bwd_dgn", of("g_ffn1", 0), ("l1_gate",))
    scatter_on("ssm_bwd_dh", lambda c: [(c["g_ffn1"][1], 0), (c["g_ffn1"][2], 0)], ("l1_up", "l1_down"))
    scatter_on("ffn0_bwd_act", ssm_in_grad(0), ("l_ssm_in_a",))
    scatter_on("ffn0_bwd_dh", ssm_in_grad(1), ("l_ssm_in_b",))
    scatter_on("ffn0_wg", lambda c: [(c["g_ssm"]["out"], 0)], ("l_ssm_out",))
    scatter_on("ffn0_wu", lambda c: [(c["g_ffn0_gate"], 0)], ("l0_gate",))
    scatter_on("ffn0_wd", lambda c: [(c["g_ffn0_up"], 0)], ("l0_up",))

    loss_cols, grad_x, dns = _local_step(x, positions, loss_target, ctx, plan)
    loss = lax.psum(jnp.sum(loss_cols), ("x", "y", "c"))
    dn00, dn01, dn02, dn10, dn11, dn12 = dns
    g_ssm, g_mla = ctx["g_ssm"], ctx["g_mla"]
    ctx["l0_down"] = _Exchange("scatter", [(ctx["g_ffn0"][2], 0)]).run("scatter_last")[0]
    l_ffn = {k: jnp.stack([ctx["l%d_%s" % (n, k)] for n in range(4)], axis=1) for k in ("gate", "up", "down")}
    l_mla_out, l_mla_in, l_qb, l_kvb = (ctx[k] for k in ("l_mla_out", "l_mla_in", "l_qb", "l_kvb"))
    l_ssm_in = jnp.concatenate([ctx[k].reshape(N_DEV, half_rows, d) for k in ("l_ssm_in_a", "l_ssm_in_b")], axis=1)
    l_ssm_out = ctx["l_ssm_out"]

    unlane = lambda a: a.reshape(SSM_GROUPS, LANES)[:, :SSM_HPG].reshape(1, SSM_HEADS)
    small_g = jnp.concatenate([
        jnp.concatenate([dn00, dn01, dn02, dn10, dn11, dn12], axis=0).reshape(-1),
        g_ssm["conv_w"].reshape(-1), g_ssm["conv_b"].reshape(-1), unlane(g_ssm["dt_bias"]).reshape(-1),
        unlane(g_ssm["a_log"]).reshape(-1), unlane(g_ssm["d_skip"]).reshape(-1), g_ssm["ssm_norm_w"].reshape(-1),
        g_mla["q_a_norm"].reshape(-1), g_mla["kv_a_norm"].reshape(-1), g_mla["q_norm"][0, :QK_DIM],
        g_mla["k_norm"][0, :QK_DIM]])
    n_small = small_g.shape[0]
    n_small_pad = -(-n_small // (8 * LANES)) * (8 * LANES)
    small_g = jnp.pad(small_g, (0, n_small_pad - n_small)).reshape(n_small_pad // LANES, LANES)
    gs = _Exchange("gather", [(small_g, 0)]).run("gather_small_grads")[0].reshape(N_DEV, n_small_pad)

    outs = {}

    def put(name, res, shape):
        for key, val in zip(("grad", "delta", "new_m", "new_v"), res):
            outs[(key, name)] = val.reshape(shape)

    ck = 256
    for key, name, w, m, v in (("gate", "ffn_w_gate", ffn_w_gate, m_ffn_w_gate, v_ffn_w_gate),
                               ("up", "ffn_w_up", ffn_w_up, m_ffn_w_up, v_ffn_w_up)):
        res = _adam("adam_" + name, l_ffn[key], (N_DEV, None, FF_SHARD, ck), lambda a, i: (0, a, 0, i),
                    w.reshape(4, d, FF_SHARD), m.reshape(4, d, FF_SHARD), v.reshape(4, d, FF_SHARD), True, ck)
        put(name, res, w.shape)
    res = _adam("adam_ffn_w_down", l_ffn["down"], (N_DEV, None, 176, d), lambda a, i: (0, a, i, 0),
                ffn_w_down.reshape(4, FF_SHARD, d), m_ffn_w_down.reshape(4, FF_SHARD, d),
                v_ffn_w_down.reshape(4, FF_SHARD, d), False, 176)
    put("ffn_w_down", res, ffn_w_down.shape)
    l_ssm_in2 = l_ssm_in.reshape(N_DEV, SSM_IN_SHARD, d)
    res = _adam("adam_ssm_w_in", l_ssm_in2, (N_DEV, SSM_IN_SHARD, 128), lambda a, i: (0, 0, i),
                ssm_w_in, m_ssm_w_in, v_ssm_w_in, True, 128)
    put("ssm_w_in", res, ssm_w_in.shape)
    res = _adam("adam_ssm_w_out", l_ssm_out, (N_DEV, 128, d), lambda a, i: (0, i, 0),
                ssm_w_out, m_ssm_w_out, v_ssm_w_out, False, 128)
    put("ssm_w_out", res, ssm_w_out.shape)
    res = _adam("adam_mla_w_in", l_mla_in, (N_DEV, 128, MLA_IN_DIM), lambda a, i: (0, 0, 0),
                mla_w_in, m_mla_w_in, v_mla_w_in, False, 128)
    put("mla_w_in", res, mla_w_in.shape)
    res = _adam("adam_mla_w_q_b", l_qb, (N_DEV, 192, 128), lambda a, i: (0, 0, i),
                mla_w_q_b, m_mla_w_q_b, v_mla_w_q_b, True, 128)
    put("mla_w_q_b", res, mla_w_q_b.shape)
    res = _adam("adam_mla_w_kv_b", l_kvb, (N_DEV, 256, 128), lambda a, i: (0, 0, i),
                mla_w_kv_b, m_mla_w_kv_b, v_mla_w_kv_b, True, 128)
    put("mla_w_kv_b", res, mla_w_kv_b.shape)
    res = _adam("adam_mla_w_out", l_mla_out, (N_DEV, 128, d), lambda a, i: (0, 0, 0),
                mla_w_out, m_mla_w_out, v_mla_w_out, False, 128)
    put("mla_w_out", res, mla_w_out.shape)

    small_params = (
        ("norm_w", norm_w, m_norm_w, v_norm_w, 6 * d, 6, 128), ("ssm_conv_w", ssm_conv_w, m_ssm_conv_w, v_ssm_conv_w,
                                                               CONV_K * CONV_DIM, CONV_K, 512),
        ("ssm_conv_b", ssm_conv_b, m_ssm_conv_b, v_ssm_conv_b, CONV_DIM, 0, 0),
        ("ssm_dt_bias", ssm_dt_bias, m_ssm_dt_bias, v_ssm_dt_bias, SSM_HEADS, 0, 0),
        ("ssm_a_log", ssm_a_log, m_ssm_a_log, v_ssm_a_log, SSM_HEADS, 0, 0),
        ("ssm_d", ssm_d, m_ssm_d, v_ssm_d, SSM_HEADS, 0, 0),
        ("ssm_norm_w", ssm_norm_w, m_ssm_norm_w, v_ssm_norm_w, D_INNER, 0, 0),
        ("mla_q_a_norm", mla_q_a_norm, m_mla_q_a_norm, v_mla_q_a_norm, Q_LORA, 1, 48),
        ("mla_kv_a_norm", mla_kv_a_norm, m_mla_kv_a_norm, v_mla_kv_a_norm, KV_LORA, 1, 32),
        ("mla_q_norm", mla_q_norm, m_mla_q_norm, v_mla_q_norm, QK_DIM, 0, 0),
        ("mla_k_norm", mla_k_norm, m_mla_k_norm, v_mla_k_norm, QK_DIM, 0, 0),
    )
    parts, ws_, ms_, vs_, off = [], [], [], [], 0
    for name, w, m, v, full, rows, shard in small_params:
        seg = gs[:, off:off + full]
        if rows:
            seg = _dyn(seg.reshape(N_DEV, rows, full // rows), me * shard, shard).reshape(N_DEV, rows * shard)
        parts.append(seg)
        ws_.append(w.reshape(1, -1))
        ms_.append(m.reshape(1, -1))
        vs_.append(v.reshape(1, -1))
        off += full
    n_loc = sum(p.shape[1] for p in parts)
    n_loc_pad = -(-n_loc // LANES) * LANES
    padc = lambda a, val=0.0: jnp.pad(jnp.concatenate(a, axis=1), ((0, 0), (0, n_loc_pad - n_loc)),
                                      constant_values=val)
    res = _adam("adam_small", padc(parts).reshape(N_DEV, 1, n_loc_pad), (N_DEV, 1, n_loc_pad), lambda a, i: (0, 0, 0),
                padc(ws_).reshape(1, 1, n_loc_pad), padc(ms_).reshape(1, 1, n_loc_pad),
                padc(vs_, 1.0).reshape(1, 1, n_loc_pad), False, 1)
    off = 0
    for name, w, m, v, full, rows, shard in small_params:
        nloc = w.size
        put(name, [r.reshape(-1)[off:off + nloc] for r in res], w.shape)
        off += nloc

    order = ("norm_w", "ffn_w_gate", "ffn_w_up", "ffn_w_down", "ssm_w_in", "ssm_conv_w", "ssm_conv_b", "ssm_dt_bias",
             "ssm_a_log", "ssm_d", "ssm_norm_w", "ssm_w_out", "mla_w_in", "mla_q_a_norm", "mla_kv_a_norm",
             "mla_w_q_b", "mla_w_kv_b", "mla_q_norm", "mla_k_norm", "mla_w_out")
    return (loss, grad_x, *[outs[(k, n)] for k in ("grad", "delta", "new_m", "new_v") for n in order])
```

```python
import jax
import jax.numpy as jnp
import numpy as np
from jax import lax
from jax.experimental import pallas as pl
from jax.experimental.pallas import tpu as pltpu

F32 = jnp.float32
BF = jnp.bfloat16
SDS = jax.ShapeDtypeStruct

N_DEV = 8
D_MODEL = 1024
D_FF = 2816
FF_SHARD = D_FF // N_DEV
D_INNER = 2048
SSM_HEADS = 32
SSM_GROUPS = 8
SSM_HPG = 4
SSM_STATE = 128
CONV_K = 4
CONV_DIM = 4096
SSM_IN_DIM = 6176
SSM_IN_SHARD = SSM_IN_DIM // N_DEV
NORM_GROUP = 256
CHUNK = 128
MLA_HEADS = 16
Q_LORA = 384
KV_LORA = 256
QK_NOPE = 64
QK_ROPE = 32
QK_DIM = 96
MLA_IN_DIM = 672
HEAD_PAD = 128
ROPE_THETA = 10000.0
EPS = 1e-6
LANES = 128

ADAM_LR = 0.001
ADAM_B1 = 0.9
ADAM_B2 = 0.999
ADAM_EPS = 1e-08
ADAM_WD = 0.01
ADAM_STEP = 10

VMEM_BIG = 56 * 1024 * 1024

NN = ((1,), (0,))
NT = ((1,), (1,))
TN = ((0,), (0,))


def _dotf(a, b, dn):
    return lax.dot_general(a.astype(BF), b.astype(BF), (dn, ((), ())), preferred_element_type=F32)


def _dot_hi(a, b, dn=NN):
    return lax.dot_general(a, b, (dn, ((), ())), precision=lax.Precision.HIGHEST, preferred_element_type=F32)


def _sigmoid(x):
    return jax.nn.sigmoid(x)


def _silu(x):
    return x * _sigmoid(x)


def _softplus(x):
    return jnp.maximum(x, 0.0) + jnp.log(1.0 + jnp.exp(-jnp.abs(x)))


def _cparams(n_grid, vmem=None):
    return pltpu.CompilerParams(dimension_semantics=("arbitrary",) * n_grid, vmem_limit_bytes=vmem)


def _fmm(name, grid_mn, pairs, outs, *, epi=None, extras=(), n_acc=1, acc_shape=None, vmem=None, joint=False,
         hosts=None):
    comm = hosts.comm(name) if hosts is not None else None
    if joint:
        nk_total = pairs[0][7]
        assert all(p[7] == nk_total for p in pairs)
        starts = [0] * len(pairs)
    else:
        nk_total = sum(p[7] for p in pairs)
        starts = []
        s = 0
        for p in pairs:
            starts.append(s)
            s += p[7]
    n_pairs, n_extras, n_outs = len(pairs), len(extras), len(outs)
    single = nk_total == 1

    n_ci = len(comm.arrays) if comm is not None else 0
    n_co = len(comm.out_shapes) if comm is not None else 0
    n_scratch_acc = 0 if single else n_acc

    def body(*refs):
        ab_refs = refs[: 2 * n_pairs]
        e_refs = refs[2 * n_pairs: 2 * n_pairs + n_extras]
        pos = 2 * n_pairs + n_extras
        ci_refs = refs[pos: pos + n_ci]
        pos += n_ci
        o_refs = refs[pos: pos + n_outs]
        co_refs = refs[pos + n_outs: pos + n_outs + n_co]
        pos += n_outs + n_co
        acc_refs = refs[pos: pos + n_scratch_acc]
        sem_refs = refs[pos + n_scratch_acc:]
        i, j, k = pl.program_id(0), pl.program_id(1), pl.program_id(2)

        if comm is not None:
            @pl.when((i == 0) & (j == 0) & (k == 0))
            def _():
                comm.start(ci_refs, co_refs, sem_refs)

        compute(ab_refs, e_refs, o_refs, acc_refs, i, j, k)

        if comm is not None:
            @pl.when((i == grid_mn[0] - 1) & (j == grid_mn[1] - 1) & (k == nk_total - 1))
            def _():
                comm.wait(ci_refs, co_refs, sem_refs)

    def compute(ab_refs, e_refs, o_refs, acc_refs, i, j, k):

        def finish(accs):
            res = epi(accs, *[e[...] for e in e_refs]) if epi is not None else accs
            if not isinstance(res, (tuple, list)):
                res = (res,)
            first = (i == 0) & (j == 0)
            for o, r, spec in zip(o_refs, res, outs):
                if spec[3]:
                    @pl.when(first)
                    def _(o=o, r=r):
                        o[...] = r.astype(o.dtype)

                    @pl.when(jnp.logical_not(first))
                    def _(o=o, r=r):
                        o[...] += r.astype(o.dtype)
                else:
                    o[...] = r.astype(o.dtype)

        if single:
            accs = [None] * n_acc
            for p, pr in enumerate(pairs):
                d = _dotf(ab_refs[2 * p][...], ab_refs[2 * p + 1][...], pr[6])
                accs[pr[8]] = d if accs[pr[8]] is None else accs[pr[8]] + d
            finish(accs)
            return

        @pl.when(k == 0)
        def _():
            for a in acc_refs:
                a[...] = jnp.zeros_like(a)

        for p, pr in enumerate(pairs):
            def step(p=p, pr=pr):
                acc_refs[pr[8]][...] += _dotf(ab_refs[2 * p][...], ab_refs[2 * p + 1][...], pr[6])

            if n_pairs == 1 or joint:
                step()
            else:
                pl.when((k >= starts[p]) & (k < starts[p] + pr[7]))(step)

        @pl.when(k == nk_total - 1)
        def _():
            finish([a[...] for a in acc_refs])

    in_specs, args = [], []
    for p, pr in enumerate(pairs):
        a, a_blk, a_idx, b, b_blk, b_idx, _, nk, _ = pr
        st = starts[p]

        def amap(i, j, k, a_idx=a_idx, st=st, nk=nk):
            return a_idx(i, j, jnp.clip(k - st, 0, nk - 1))

        def bmap(i, j, k, b_idx=b_idx, st=st, nk=nk):
            return b_idx(i, j, jnp.clip(k - st, 0, nk - 1))

        in_specs += [pl.BlockSpec(a_blk, amap), pl.BlockSpec(b_blk, bmap)]
        args += [a, b]
    for arr, blk, idx in extras:
        in_specs.append(pl.BlockSpec(blk, lambda i, j, k, idx=idx: idx(i, j)))
        args.append(arr)
    out_specs = [pl.BlockSpec(blk, lambda i, j, k, idx=idx: idx(i, j)) for _, blk, idx, _ in outs]
    out_shape = [o[0] for o in outs]
    scratch = [] if single else [pltpu.VMEM(acc_shape, F32) for _ in range(n_acc)]
    if comm is not None:
        hbm = pl.BlockSpec(memory_space=pl.ANY)
        in_specs += [hbm] * n_ci
        args += list(comm.arrays)
        out_specs += [hbm] * n_co
        out_shape += list(comm.out_shapes)
        scratch += comm.sems
    res = pl.pallas_call(
        body, name=name, grid=(grid_mn[0], grid_mn[1], nk_total), in_specs=in_specs, out_specs=out_specs,
        out_shape=out_shape, scratch_shapes=scratch, compiler_params=_cparams(3, vmem),
    )(*args)
    if comm is not None:
        hosts.done(name, res[n_outs:])
    return res[:n_outs]


def _ew(name, grid, fn, ins, outs, *, acc_axes=(), vmem=None):
    n_in = len(ins)

    def body(*refs):
        res = fn(*[r[...] for r in refs[:n_in]])
        if not isinstance(res, (tuple, list)):
            res = (res,)
        first = None
        for ax in acc_axes:
            c = pl.program_id(ax) == 0
            first = c if first is None else (first & c)
        for o, r, spec in zip(refs[n_in:], res, outs):
            if spec[3]:
                @pl.when(first)
                def _(o=o, r=r):
                    o[...] = r.astype(o.dtype)

                @pl.when(jnp.logical_not(first))
                def _(o=o, r=r):
                    o[...] += r.astype(o.dtype)
            else:
                o[...] = r.astype(o.dtype)

    return pl.pallas_call(
        body, name=name, grid=grid,
        in_specs=[pl.BlockSpec(blk, idx) for _, blk, idx in ins],
        out_specs=[pl.BlockSpec(blk, idx) for _, blk, idx, _ in outs],
        out_shape=[o[0] for o in outs],
        compiler_params=_cparams(len(grid), vmem),
    )(*[a for a, _, _ in ins])


def _row_tile(t, want):
    tm = min(want, t)
    assert t % tm == 0, (t, tm)
    return tm


def _rms_fn(x, w):
    return x * lax.rsqrt(jnp.mean(x * x, axis=-1, keepdims=True) + EPS) * w


def _rms_fwd(name, x, w):
    t, d = x.shape
    tm = _row_tile(t, 512)
    return _ew(name, (t // tm,), _rms_fn,
               [(x, (tm, d), lambda i: (i, 0)), (w, (1, d), lambda i: (0, 0))],
               [(SDS((t, d), BF), (tm, d), lambda i: (i, 0), False)])[0]


def _rms_bwd_epi(accs, xv, wv, dres):
    _, vjp = jax.vjp(_rms_fn, xv, wv)
    dx, dw = vjp(accs[0])
    return dx + dres, dw


def _rms_bwd_tail(x, w, dres, tm):
    t, d = x.shape
    row = lambda i, j: (i, 0)
    const = lambda i, j: (0, 0)
    extras = [(x, (tm, d), row), (w, (1, d), const), (dres, (tm, d), row)]
    outs = [(SDS((t, d), F32), (tm, d), row, False), (SDS((1, d), F32), (1, d), const, True)]
    return extras, outs


def _loss_and_grad(y, target):
    t, d = y.shape
    tm = _row_tile(t, 512)

    def fn(y, tg):
        e = y - tg
        return e * (1.0 / d), jnp.sum(e * e, axis=0, keepdims=True) * (0.5 / d)

    row = lambda i: (i, 0)
    return _ew("loss_head", (t // tm,), fn, [(y, (tm, d), row), (target, (tm, d), row)],
               [(SDS((t, d), F32), (tm, d), row, False), (SDS((1, d), F32), (1, d), lambda i: (0, 0), True)],
               acc_axes=(0,))


def _ffn_fwd(tag, x, nw, wf, hosts=None):
    gate_t, up_t = wf[0], wf[1]
    t, d = x.shape
    h = _rms_fwd(tag + "_rms", x, nw)
    tm, tn = _row_tile(t, 256), D_FF

    def epi(accs):
        g, u = accs
        s = _sigmoid(g)
        sg = g * s
        return 0.5 * sg, 0.5 * (s * (1.0 + g * (1.0 - s))), u, sg * u

    hblk = (h, (tm, d), lambda i, j, k: (i, 0))
    col = lambda i, j: (i, j)
    tblk = lambda w: (w, (tn, d), lambda i, j, k: (j, 0), NT, 1)
    sgh, dsgh, u, a = _fmm(
        tag + "_up", (t // tm, D_FF // tn), [hblk + tblk(gate_t) + (0,), hblk + tblk(up_t) + (1,)],
        [(SDS((t, D_FF), BF), (tm, tn), col, False)] * 4, epi=epi, n_acc=2, joint=True, hosts=hosts, vmem=VMEM_BIG)
    down = wf[2]
    tm2 = _row_tile(t, 512)
    y = _fmm(
        tag + "_down", (t // tm2, 1),
        [(a, (tm2, D_FF), lambda i, j, k: (i, 0), down, (D_FF, d), lambda i, j, k: (0, 0), NN, 1, 0)],
        [(SDS((t, d), F32), (tm2, d), lambda i, j: (i, 0), False)],
        extras=[(x, (tm2, d), lambda i, j: (i, 0))],
        epi=lambda accs, xr: xr + 0.5 * accs[0], vmem=VMEM_BIG, hosts=hosts)[0]
    return y, (x, h, sgh, dsgh, u, a)


def _wgrad(name, a, b, m, n, *, tm, tn, tk=2048, scale=None, out_dtype=BF, hosts=None):
    t = a.shape[0]
    tk = _row_tile(t, tk)
    epi = (lambda accs: accs[0] * scale) if scale is not None else None
    return _fmm(name, (m // tm, n // tn),
                [(a, (tk, tm), lambda i, j, k: (k, i), b, (tk, tn), lambda i, j, k: (k, j), TN, t // tk, 0)],
                [(SDS((m, n), out_dtype), (tm, tn), lambda i, j: (i, j), False)], epi=epi, acc_shape=(tm, tn),
                vmem=VMEM_BIG, hosts=hosts)[0]


def _ffn_bwd(tag, dy, nw, wf, saved, hosts=None):
    gate_t, up_t, down = wf
    x, h, sgh, dsgh, u, a = saved
    t, d = x.shape
    tm, tn = _row_tile(t, 256), D_FF

    def epi(accs, sgh, dsgh, u):
        da = accs[0]
        return da * (u.astype(F32) * dsgh.astype(F32)), da * sgh.astype(F32)

    col = lambda i, j: (i, j)
    dg, du = _fmm(
        tag + "_bwd_act", (t // tm, D_FF // tn),
        [(dy, (tm, d), lambda i, j, k: (i, 0), down, (tn, d), lambda i, j, k: (j, 0), NT, 1, 0)],
        [(SDS((t, D_FF), BF), (tm, tn), col, False)] * 2,
        extras=[(sgh, (tm, tn), col), (dsgh, (tm, tn), col), (u, (tm, tn), col)], epi=epi, hosts=hosts,
        vmem=VMEM_BIG)
    tm2 = _row_tile(t, 256)
    full = lambda i, j, k: (0, 0)
    extras, outs = _rms_bwd_tail(x, nw, dy, tm2)
    dx, dnw = _fmm(
        tag + "_bwd_dh", (t // tm2, 1),
        [(dg, (tm2, D_FF), lambda i, j, k: (i, 0), gate_t, (D_FF, d), full, NN, 1, 0),
         (du, (tm2, D_FF), lambda i, j, k: (i, 0), up_t, (D_FF, d), full, NN, 1, 0)],
        outs, extras=extras, epi=_rms_bwd_epi, vmem=VMEM_BIG, joint=True, hosts=hosts)
    half = D_FF // 2
    g_gate = _wgrad(tag + "_wg", dg, h, D_FF, d, tm=half, tn=d, hosts=hosts)
    if hosts is not None:
        hosts.ctx["g_" + tag + "_gate"] = g_gate
    g_up = _wgrad(tag + "_wu", du, h, D_FF, d, tm=half, tn=d, hosts=hosts)
    if hosts is not None:
        hosts.ctx["g_" + tag + "_up"] = g_up
    g_down = _wgrad(tag + "_wd", a, dy, D_FF, d, tm=half, tn=d, scale=0.5, hosts=hosts)
    return dx, dnw, (g_gate, g_up, g_down)


def _shift_down(cur, prev8, j):
    rolled = pltpu.roll(cur, j, 0)
    sub = lax.broadcasted_iota(jnp.int32, prev8.shape, 0)
    top = jnp.where(sub < j, pltpu.roll(prev8, j, 0), rolled[:8])
    return jnp.concatenate([top, rolled[8:]], axis=0)


def _shift_up(cur, next8, j):
    n = cur.shape[0]
    rolled = pltpu.roll(cur, n - j, 0)
    sub = lax.broadcasted_iota(jnp.int32, next8.shape, 0)
    bot = jnp.where(sub >= 8 - j, pltpu.roll(next8, 8 - j, 0), rolled[n - 8:])
    return jnp.concatenate([rolled[: n - 8], bot], axis=0)


HALO = 16


def _conv_fwd(xbc, w, b, seq):
    t, c = xbc.shape
    ts, tc = _row_tile(seq, 512), 1024
    tiles_per_seq = seq // ts
    hb = ts // HALO

    def fn(cur, prev, w, b):
        i = pl.program_id(1)
        cur = cur.astype(F32)
        prev8 = jnp.where(i % tiles_per_seq == 0, 0.0, prev.astype(F32)[HALO - 8:])
        out = b + w[3:4] * cur
        for j in range(1, CONV_K):
            out = out + w[3 - j:4 - j] * _shift_down(cur, prev8, j)
        return out, _silu(out)

    return _ew("ssm_conv_fwd", (c // tc, t // ts), fn,
               [(xbc, (ts, tc), lambda j, i: (i, j)),
                (xbc, (HALO, tc), lambda j, i: (jnp.maximum(i * hb - 1, 0), j)),
                (w, (CONV_K, tc), lambda j, i: (0, j)), (b, (1, tc), lambda j, i: (0, j))],
               [(SDS((t, c), BF), (ts, tc), lambda j, i: (i, j), False)] * 2, vmem=VMEM_BIG)


def _conv_bwd(tag, dxa, cpre, xbc, w, col0, seq):
    t, width = dxa.shape
    ts, tc = _row_tile(seq, 512), 1024
    tiles_per_seq = seq // ts
    hb = ts // HALO
    cb0 = col0 // tc
    n_halo_blocks = t // HALO

    def dsilu(cv, dv):
        cv = cv.astype(F32)
        s = _sigmoid(cv)
        return dv.astype(F32) * (s * (1.0 + cv * (1.0 - s)))

    def fn(dxa_c, dxa_n, c_c, c_n, x_c, x_p, w):
        i = pl.program_id(1)
        dc = dsilu(c_c, dxa_c)
        last = i % tiles_per_seq == tiles_per_seq - 1
        dc_n = jnp.where(last, 0.0, dsilu(c_n, dxa_n)[:8])
        dx = w[3:4] * dc
        for j in range(1, CONV_K):
            dx = dx + w[3 - j:4 - j] * _shift_up(dc, dc_n, j)
        cur = x_c.astype(F32)
        prev8 = jnp.where(i % tiles_per_seq == 0, 0.0, x_p.astype(F32)[HALO - 8:])
        rows = [jnp.sum(dc * cur, axis=0, keepdims=True)]
        for j in range(1, CONV_K):
            rows.append(jnp.sum(dc * _shift_down(cur, prev8, j), axis=0, keepdims=True))
        sub8 = lax.broadcasted_iota(jnp.int32, (8, dc.shape[1]), 0)
        dw = jnp.zeros((8, dc.shape[1]), F32)
        for kk in range(CONV_K):
            dw = jnp.where(sub8 == kk, rows[CONV_K - 1 - kk], dw)
        return dx, dw, jnp.sum(dc, axis=0, keepdims=True)

    nxt = lambda j, i: (jnp.minimum((i + 1) * hb, n_halo_blocks - 1), j)
    nxt_off = lambda j, i: (jnp.minimum((i + 1) * hb, n_halo_blocks - 1), j + cb0)
    return _ew(tag, (width // tc, t // ts), fn,
               [(dxa, (ts, tc), lambda j, i: (i, j)), (dxa, (HALO, tc), nxt),
                (cpre, (ts, tc), lambda j, i: (i, j + cb0)), (cpre, (HALO, tc), nxt_off),
                (xbc, (ts, tc), lambda j, i: (i, j + cb0)),
                (xbc, (HALO, tc), lambda j, i: (jnp.maximum(i * hb - 1, 0), j + cb0)),
                (w, (CONV_K, tc), lambda j, i: (0, j + cb0))],
               [(SDS((t, width), BF), (ts, tc), lambda j, i: (i, j), False),
                (SDS((8, width), F32), (8, tc), lambda j, i: (0, j), True),
                (SDS((1, width), F32), (1, tc), lambda j, i: (0, j), True)],
               acc_axes=(1,), vmem=VMEM_BIG)


def _ssd_chunk(xs, bm, cm, dtr, st, dtb, alog, dsk):
    ell = xs.shape[0]
    xs = xs.astype(F32)
    lane = lax.broadcasted_iota(jnp.int32, (ell, LANES), 1)
    sub = lax.broadcasted_iota(jnp.int32, (ell, LANES), 0)
    lane1 = lax.broadcasted_iota(jnp.int32, (1, LANES), 1)
    causal = sub >= lane
    dt = _softplus(dtr + dtb)
    da = dt * (-jnp.exp(alog))
    acs = _dot_hi(causal.astype(F32), da)
    acs_t = acs.T
    cb = _dotf(cm, bm, NT)
    lo = lane < 64
    ys, news = [], []
    for p in range(2):
        xp = xs[:, LANES * p:LANES * (p + 1)]
        sp = st[LANES * p:LANES * (p + 1), :]
        col, dtc, last, dsel = [], [], [], []
        y_diag = None
        for q in range(2):
            r = 2 * p + q
            col_r = jnp.sum(jnp.where(lane == r, acs, 0.0), axis=1, keepdims=True)
            row_r = jnp.sum(jnp.where(sub == r, acs_t, 0.0), axis=0, keepdims=True)
            dtc_r = jnp.sum(jnp.where(lane == r, dt, 0.0), axis=1, keepdims=True)
            decay = jnp.exp(jnp.where(causal, col_r - row_r, -jnp.inf))
            head = lo if q == 0 else jnp.logical_not(lo)
            d = _dotf(cb * decay, jnp.where(head, xp * dtc_r, 0.0), NN)
            y_diag = d if y_diag is None else y_diag + d
            col.append(col_r)
            dtc.append(dtc_r)
            last.append(jnp.sum(jnp.where(sub[:, :1] == ell - 1, col_r, 0.0), axis=0, keepdims=True))
            dsel.append(jnp.sum(jnp.where(lane1 == r, dsk, 0.0), axis=1, keepdims=True))
        y_off = _dotf(cm, sp, NT) * jnp.where(lo, jnp.exp(col[0]), jnp.exp(col[1]))
        xw = jnp.where(lo, xp * (dtc[0] * jnp.exp(last[0] - col[0])), xp * (dtc[1] * jnp.exp(last[1] - col[1])))
        new = sp * jnp.where(sub < 64, jnp.exp(last[0]), jnp.exp(last[1])) + _dotf(xw, bm, TN)
        ys.append(y_diag + y_off + jnp.where(lo, dsel[0], dsel[1]) * xp)
        news.append(new)
    return jnp.concatenate(ys, axis=1), jnp.concatenate(news, axis=0)


SSD_GP = 8
SSD_GP_BWD = 4
XW, GW = 2 * LANES * SSD_GP, LANES * SSD_GP


def _grp(ref, q, width):
    return ref[:, width * q:width * (q + 1)]


def _ssd_fwd(xa, dtr, dtb, alog, dsk, nb, seq):
    t = xa.shape[0]
    nc = seq // CHUNK
    row = lambda g, b, c: (b * nc + c, g)
    par = pl.BlockSpec((1, GW), lambda g, b, c: (0, g))
    b_off, c_off = D_INNER // GW, (D_INNER + SSM_GROUPS * SSM_STATE) // GW

    def body(xs, bm, cm, dtr, dtb, alog, dsk, y_ref, st_out, st_ref):
        @pl.when(pl.program_id(2) == 0)
        def _():
            st_ref[...] = jnp.zeros_like(st_ref)

        ins = [(_grp(xs, q, 2 * LANES), _grp(bm, q, LANES), _grp(cm, q, LANES), _grp(dtr, q, LANES), st_ref[q],
                _grp(dtb, q, LANES), _grp(alog, q, LANES), _grp(dsk, q, LANES)) for q in range(SSD_GP)]
        res = [_ssd_chunk(*a) for a in ins]
        for q in range(SSD_GP):
            st_out[q] = ins[q][4]
            y_ref[:, 2 * LANES * q:2 * LANES * (q + 1)] = res[q][0]
            st_ref[q] = res[q][1]

    specs = [pl.BlockSpec((CHUNK, XW), row),
             pl.BlockSpec((CHUNK, GW), lambda g, b, c: (b * nc + c, b_off + g)),
             pl.BlockSpec((CHUNK, GW), lambda g, b, c: (b * nc + c, c_off + g)),
             pl.BlockSpec((CHUNK, GW), row), par, par, par]
    return pl.pallas_call(
        body, name="ssd_fwd", grid=(SSM_GROUPS // SSD_GP, nb, nc), in_specs=specs,
        out_specs=[pl.BlockSpec((CHUNK, XW), row),
                   pl.BlockSpec((SSD_GP, None, None, 2 * LANES, LANES), lambda g, b, c: (g, b, c, 0, 0))],
        out_shape=[SDS((t, D_INNER), F32), SDS((SSM_GROUPS, nb, nc, 2 * LANES, LANES), F32)],
        scratch_shapes=[pltpu.VMEM((SSD_GP, 2 * LANES, LANES), F32)],
        compiler_params=_cparams(3),
    )(xa, xa, xa, dtr, dtb, alog, dsk)


def _ssd_bwd(xa, dtr, dtb, alog, dsk, states, dy, nb, seq):
    t = xa.shape[0]
    nc = seq // CHUNK
    rev = lambda c: nc - 1 - c
    row = lambda g, b, c: (b * nc + rev(c), g)
    gp = SSD_GP_BWD
    xw, gw = 2 * LANES * gp, LANES * gp
    par = pl.BlockSpec((1, gw), lambda g, b, c: (0, g))
    b_off, c_off = D_INNER // gw, (D_INNER + SSM_GROUPS * SSM_STATE) // gw

    def body(xs, bm, cm, dtr, dtb, alog, dsk, st_in, dy, dxs, dbm, dcm, ddtr, ddtb, dalog, ddsk, dst_ref):
        @pl.when(pl.program_id(2) == 0)
        def _():
            dst_ref[...] = jnp.zeros_like(dst_ref)

        ins = [(_grp(xs, q, 2 * LANES), _grp(bm, q, LANES), _grp(cm, q, LANES), _grp(dtr, q, LANES), st_in[q],
                _grp(dtb, q, LANES), _grp(alog, q, LANES), _grp(dsk, q, LANES)) for q in range(gp)]
        cts = [(_grp(dy, q, 2 * LANES), dst_ref[q]) for q in range(gp)]
        gs = [jax.vjp(_ssd_chunk, *a)[1](ct) for a, ct in zip(ins, cts)]
        for q, g in enumerate(gs):
            dxs[:, 2 * LANES * q:2 * LANES * (q + 1)] = g[0]
            lanes = slice(LANES * q, LANES * (q + 1))
            dbm[:, lanes] = g[1]
            dcm[:, lanes] = g[2]
            ddtr[:, lanes] = g[3]
            dst_ref[q] = g[4]
        first = (pl.program_id(1) == 0) & (pl.program_id(2) == 0)
        for o, k in ((ddtb, 5), (dalog, 6), (ddsk, 7)):
            v = jnp.concatenate([g[k] for g in gs], axis=1)

            @pl.when(first)
            def _(o=o, v=v):
                o[...] = v

            @pl.when(jnp.logical_not(first))
            def _(o=o, v=v):
                o[...] += v

    in_specs = [
        pl.BlockSpec((CHUNK, xw), row),
        pl.BlockSpec((CHUNK, gw), lambda g, b, c: (b * nc + rev(c), b_off + g)),
        pl.BlockSpec((CHUNK, gw), lambda g, b, c: (b * nc + rev(c), c_off + g)),
        pl.BlockSpec((CHUNK, gw), row), par, par, par,
        pl.BlockSpec((gp, None, None, 2 * LANES, LANES), lambda g, b, c: (g, b, rev(c), 0, 0)),
        pl.BlockSpec((CHUNK, xw), row),
    ]
    out_specs = [pl.BlockSpec((CHUNK, xw), row), pl.BlockSpec((CHUNK, gw), row), pl.BlockSpec((CHUNK, gw), row),
                 pl.BlockSpec((CHUNK, gw), row), par, par, par]
    out_shape = [SDS((t, D_INNER), BF), SDS((t, SSM_GROUPS * LANES), BF), SDS((t, SSM_GROUPS * LANES), BF),
                 SDS((t, SSM_GROUPS * LANES), F32)] + [SDS((1, SSM_GROUPS * LANES), F32)] * 3
    return pl.pallas_call(
        body, name="ssd_bwd", grid=(SSM_GROUPS // gp, nb, nc), in_specs=in_specs, out_specs=out_specs,
        out_shape=out_shape, scratch_shapes=[pltpu.VMEM((gp, 2 * LANES, LANES), F32)],
        compiler_params=_cparams(3, VMEM_BIG),
    )(xa, xa, xa, dtr, dtb, alog, dsk, states, dy)


def _gated_fn(y, z, w):
    g = y * _silu(z.astype(F32))
    return g * lax.rsqrt(jnp.mean(g * g, axis=-1, keepdims=True) + EPS) * w


def _gated_norm_fwd(y, z, w):
    t = y.shape[0]
    tm = _row_tile(t, 2048)
    blk = ((tm, NORM_GROUP), lambda g, i: (i, g))
    return _ew("ssm_gnorm_fwd", (SSM_GROUPS, t // tm), _gated_fn,
               [(y,) + blk, (z,) + blk, (w, (1, NORM_GROUP), lambda g, i: (0, g))],
               [(SDS((t, D_INNER), BF),) + blk + (False,)], vmem=VMEM_BIG)[0]


def _gated_norm_bwd(y, z, w, dout):
    t = y.shape[0]
    tm = _row_tile(t, 2048)
    blk = ((tm, NORM_GROUP), lambda g, i: (i, g))
    par = ((1, NORM_GROUP), lambda g, i: (0, g))

    def fn(y, z, w, dout):
        _, vjp = jax.vjp(_gated_fn, y, z, w)
        return vjp(dout.astype(F32))

    return _ew("ssm_gnorm_bwd", (SSM_GROUPS, t // tm), fn,
               [(y,) + blk, (z,) + blk, (w,) + par, (dout,) + blk],
               [(SDS((t, D_INNER), F32),) + blk + (False,), (SDS((t, D_INNER), BF),) + blk + (False,),
                (SDS((1, D_INNER), F32),) + par + (True,)],
               acc_axes=(1,), vmem=VMEM_BIG)


def _proj_nt(name, h, wt, n, out_dtype, tn=1024, hosts=None):
    t, kdim = h.shape
    tm = _row_tile(t, 1024)
    return _fmm(name, (t // tm, n // tn),
                [(h, (tm, kdim), lambda i, j, k: (i, 0), wt, (tn, kdim), lambda i, j, k: (j, 0), NT, 1, 0)],
                [(SDS((t, n), out_dtype), (tm, tn), lambda i, j: (i, j), False)], hosts=hosts, vmem=VMEM_BIG)[0]


def _seg_nn(name, parts, n, out_dtype, tk=256, hosts=None, rms=None, tm=512):
    t = parts[0][0].shape[0]
    tm = _row_tile(t, tm)
    pairs = []
    for a, w, row0 in parts:
        kp = a.shape[1]
        tkp = min(tk, kp)
        r0 = row0 // tkp
        pairs.append((a, (tm, tkp), lambda i, j, k: (i, k), w, (tkp, n), lambda i, j, k, r0=r0: (k + r0, 0),
                      NN, kp // tkp, 0))
    if rms is not None:
        extras, outs = _rms_bwd_tail(rms[0], rms[1], rms[2], tm)
        return _fmm(name, (t // tm, 1), pairs, outs, extras=extras, epi=_rms_bwd_epi, acc_shape=(tm, n), hosts=hosts,
                    vmem=VMEM_BIG)
    return _fmm(name, (t // tm, 1), pairs, [(SDS((t, n), out_dtype), (tm, n), lambda i, j: (i, 0), False)],
                acc_shape=(tm, n), hosts=hosts)[0]


def _ssm_fwd(x, nw, ws, small, nb, seq, hosts=None):
    t, d = x.shape
    h = _rms_fwd("ssm_rms", x, nw)
    z = _proj_nt("ssm_in_z", h, ws["z_t"], D_INNER, BF, hosts=hosts)
    xbc = _proj_nt("ssm_in_xbc", h, ws["xbc_t"], CONV_DIM, BF, hosts=hosts)
    dtr = _proj_nt("ssm_in_dt", h, ws["dt_t"], SSM_GROUPS * LANES, F32)
    cpre, xa = _conv_fwd(xbc, small["conv_w"], small["conv_b"], seq)
    y, states = _ssd_fwd(xa, dtr, small["dt_bias"], small["a_log"], small["d_skip"], nb, seq)
    gn = _gated_norm_fwd(y, z, small["ssm_norm_w"])
    tm = _row_tile(t, 512)
    out = _fmm("ssm_out", (t // tm, 1),
               [(gn, (tm, D_INNER), lambda i, j, k: (i, 0), ws["out"], (D_INNER, d), lambda i, j, k: (0, 0), NN, 1, 0)],
               [(SDS((t, d), F32), (tm, d), lambda i, j: (i, 0), False)],
               extras=[(x, (tm, d), lambda i, j: (i, 0))], epi=lambda accs, xr: xr + accs[0], hosts=hosts)[0]
    return out, (x, h, z, xbc, dtr, cpre, xa, y, states, gn)


def _ssm_bwd(dy, nw, ws, small, saved, nb, seq, hosts=None):
    x, h, z, xbc, dtr, cpre, xa, y, states, gn = saved
    t, d = x.shape
    dgn = _proj_nt("ssm_bwd_dgn", dy, ws["out"], D_INNER, BF, hosts=hosts)
    d_out = _wgrad("ssm_w_out_g", gn, dy, D_INNER, d, tm=D_INNER // 2, tn=d)
    dyssd, dz, d_normw = _gated_norm_bwd(y, z, small["ssm_norm_w"], dgn)
    dxs, dbm, dcm, ddtr, d_dtb, d_alog, d_dsk = _ssd_bwd(
        xa, dtr, small["dt_bias"], small["a_log"], small["d_skip"], states, dyssd, nb, seq)
    dxbc_x, dcw_x, dcb_x = _conv_bwd("ssm_conv_bwd_x", dxs, cpre, xbc, small["conv_w"], 0, seq)
    dxbc_b, dcw_b, dcb_b = _conv_bwd("ssm_conv_bwd_b", dbm, cpre, xbc, small["conv_w"], D_INNER, seq)
    dxbc_c, dcw_c, dcb_c = _conv_bwd("ssm_conv_bwd_c", dcm, cpre, xbc, small["conv_w"], D_INNER + 1024, seq)
    parts = [(dz, ws["z_t"], 0), (dxbc_x, ws["xbc_t"], 0), (dxbc_b, ws["xbc_t"], D_INNER),
             (dxbc_c, ws["xbc_t"], D_INNER + 1024), (ddtr, ws["dt_t"], 0)]
    dx, dnw = _seg_nn("ssm_bwd_dh", parts, d, BF, tk=1024, hosts=hosts, rms=(x, nw, dy))
    g = {
        "z_t": _wgrad("ssm_w_z_g", dz, h, D_INNER, d, tm=1024, tn=d, out_dtype=F32),
        "x_t": _wgrad("ssm_w_x_g", dxbc_x, h, D_INNER, d, tm=1024, tn=d, out_dtype=F32),
        "b_t": _wgrad("ssm_w_b_g", dxbc_b, h, 1024, d, tm=1024, tn=d, out_dtype=F32),
        "c_t": _wgrad("ssm_w_c_g", dxbc_c, h, 1024, d, tm=1024, tn=d, out_dtype=F32),
        "dt_t": _wgrad("ssm_w_dt_g", ddtr, h, 1024, d, tm=1024, tn=d, out_dtype=F32),
        "out": d_out,
        "conv_w": jnp.concatenate([dcw_x[:CONV_K], dcw_b[:CONV_K], dcw_c[:CONV_K]], axis=1),
        "conv_b": jnp.concatenate([dcb_x, dcb_b, dcb_c], axis=1),
        "dt_bias": d_dtb, "a_log": d_alog, "d_skip": d_dsk, "ssm_norm_w": d_normw,
    }
    return dx, dnw, g


def _lane_masks(shape):
    lane = lax.broadcasted_iota(jnp.int32, shape, len(shape) - 1)
    return lane < QK_NOPE, (lane >= QK_NOPE) & (lane < QK_DIM)


def _segment_matrix():
    p = np.zeros((HEAD_PAD, HEAD_PAD), np.float32)
    p[:QK_NOPE, :QK_NOPE] = 1.0
    p[QK_NOPE:QK_DIM, QK_NOPE:QK_DIM] = 1.0
    return jnp.asarray(p)


def _segment_rstd(x, seg):
    lane = lax.broadcasted_iota(jnp.int32, (1, x.shape[-1]), 1)
    inv_n = jnp.where(lane < QK_NOPE, 1.0 / QK_NOPE, 1.0 / QK_ROPE)
    xx = x * x
    hi = xx.astype(BF)
    lo = (xx - hi.astype(F32)).astype(BF)
    segb = seg.astype(BF)
    ss = _dotf(hi, segb, NN) + _dotf(lo, segb, NN)
    return lax.rsqrt(ss * inv_n + EPS)


def _rope_tables(positions_col):
    t = positions_col.shape[0]
    tm = _row_tile(t, 512)
    freq = np.zeros((1, HEAD_PAD), np.float32)
    inv = 1.0 / (ROPE_THETA ** (np.arange(0, QK_ROPE, 2, dtype=np.float32) / QK_ROPE))
    freq[0, QK_NOPE:QK_NOPE + QK_ROPE // 2] = inv
    freq[0, QK_NOPE + QK_ROPE // 2:QK_DIM] = inv
    sign = np.zeros((1, HEAD_PAD), np.float32)
    sign[0, QK_NOPE:QK_NOPE + QK_ROPE // 2] = -1.0
    sign[0, QK_NOPE + QK_ROPE // 2:QK_DIM] = 1.0

    def fn(pos, freq, sign):
        ang = pos.astype(F32) * freq
        nope, rope = _lane_masks(ang.shape)
        return jnp.where(nope, 1.0, jnp.where(rope, jnp.cos(ang), 0.0)), jnp.sin(ang) * sign

    row = lambda i: (i, 0)
    par = ((1, HEAD_PAD), lambda i: (0, 0))
    return _ew("mla_rope_tables", (t // tm,), fn,
               [(positions_col, (tm, 1), row), (jnp.asarray(freq),) + par, (jnp.asarray(sign),) + par],
               [(SDS((t, HEAD_PAD), F32), (tm, HEAD_PAD), row, False)] * 2)


def _swap_lanes(x):
    lane = lax.broadcasted_iota(jnp.int32, x.shape, x.ndim - 1)
    half = QK_ROPE // 2
    first = (lane >= QK_NOPE) & (lane < QK_NOPE + half)
    second = (lane >= QK_NOPE + half) & (lane < QK_DIM)
    n = x.shape[-1]
    return jnp.where(first, pltpu.roll(x, n - half, x.ndim - 1), jnp.where(second, pltpu.roll(x, half, x.ndim - 1), 0.0))


@jax.custom_vjp
def _swap_halves(x):
    return _swap_lanes(x)


_swap_halves.defvjp(lambda x: (_swap_lanes(x), None), lambda _, g: (_swap_lanes(g),))


def _rope(xn, cos, sin_signed):
    return xn * cos + _swap_halves(xn) * sin_signed


def _krope_fn(kr, w, cos, sin_signed, seg):
    _, rope = _lane_masks(kr.shape)
    xn = jnp.where(rope, kr * _segment_rstd(kr, seg) * w, 0.0)
    return _rope(xn, cos, sin_signed)


def _head_fn(q, kv, kr, cos, sin_signed, qn, kn, seg):
    nope, rope = _lane_masks(q.shape)
    qp = _rope(jnp.where(nope | rope, q * _segment_rstd(q, seg) * qn, 0.0), cos, sin_signed)
    kp = jnp.where(nope, kv * _segment_rstd(kv, seg) * kn, 0.0) + kr
    vp = jnp.where(nope, 0.0, kv)
    return qp, kp, vp


def _heads_fwd(q_raw, kv_raw, kr, cos, sin_signed, qn, kn, seg):
    nh, t, _ = q_raw.shape
    tm = _row_tile(t, 2048)
    hblk = ((None, tm, HEAD_PAD), lambda i, h: (h, i, 0))
    tblk = ((tm, HEAD_PAD), lambda i, h: (i, 0))
    par = ((1, HEAD_PAD), lambda i, h: (0, 0))
    sw = ((HEAD_PAD, HEAD_PAD), lambda i, h: (0, 0))
    def fn(*tiles):
        qp, kp, vp = _head_fn(*tiles)
        return qp * Q_PRESCALE, kp, vp

    return _ew("mla_heads_fwd", (t // tm, nh), fn,
               [(q_raw,) + hblk, (kv_raw,) + hblk, (kr,) + tblk, (cos,) + tblk, (sin_signed,) + tblk,
                (qn,) + par, (kn,) + par, (seg,) + sw],
               [(SDS((nh, t, HEAD_PAD), BF),) + hblk + (False,)] * 3, vmem=VMEM_BIG)


def _heads_bwd(q_raw, kv_raw, kr, cos, sin_signed, qn, kn, seg, dqp, dkp, dvp):
    nh, t, _ = q_raw.shape
    tm = _row_tile(t, 2048)
    hblk = ((None, tm, HEAD_PAD), lambda i, h: (h, i, 0))
    tblk = ((tm, HEAD_PAD), lambda i, h: (i, 0))
    par = ((1, HEAD_PAD), lambda i, h: (0, 0))
    sw = ((HEAD_PAD, HEAD_PAD), lambda i, h: (0, 0))

    def body(q, kv, kr, cos, sn, qn, kn, seg, dqp, dkp, dvp, dq, dkv, dkr, dqn, dkn):
        f = lambda q, kv, kr, qn, kn: _head_fn(q, kv, kr, cos[...], sn[...], qn, kn, seg[...])
        _, vjp = jax.vjp(f, q[...], kv[...], kr[...], qn[...], kn[...])
        g = vjp((dqp[...].astype(F32), dkp[...].astype(F32), dvp[...].astype(F32)))
        dq[...] = g[0].astype(dq.dtype)
        dkv[...] = g[1].astype(dkv.dtype)
        h0 = pl.program_id(1) == 0
        first = h0 & (pl.program_id(0) == 0)
        for o, v, c in ((dkr, g[2], h0), (dqn, g[3], first), (dkn, g[4], first)):
            @pl.when(c)
            def _(o=o, v=v):
                o[...] = v

            @pl.when(jnp.logical_not(c))
            def _(o=o, v=v):
                o[...] += v

    spec = lambda b: pl.BlockSpec(*b)
    return pl.pallas_call(
        body, name="mla_heads_bwd", grid=(t // tm, nh),
        in_specs=[spec(hblk), spec(hblk), spec(tblk), spec(tblk), spec(tblk), spec(par), spec(par), spec(sw),
                  spec(hblk), spec(hblk), spec(hblk)],
        out_specs=[spec(hblk), spec(hblk), spec(tblk), spec(par), spec(par)],
        out_shape=[SDS((nh, t, HEAD_PAD), BF), SDS((nh, t, HEAD_PAD), BF), SDS((t, HEAD_PAD), F32),
                   SDS((1, HEAD_PAD), F32), SDS((1, HEAD_PAD), F32)],
        compiler_params=_cparams(2, VMEM_BIG),
    )(q_raw, kv_raw, kr, cos, sin_signed, qn, kn, seg, dqp, dkp, dvp)


ATT_TILE = 512
ATT_SCALE = QK_DIM ** -0.5
LOG2E = 1.4426950408889634
LN2 = 0.6931471805599453
Q_PRESCALE = ATT_SCALE * LOG2E


def _flash_fwd(qs, k, v, nb, seq, hosts=None):
    nh, t, dh = qs.shape
    tq = _row_tile(seq, ATT_TILE)
    nq = seq // tq
    name = "mla_flash_fwd"
    comm = hosts.comm(name) if hosts is not None else None
    n_ci = len(comm.arrays) if comm is not None else 0
    n_co = len(comm.out_shapes) if comm is not None else 0

    def body(*refs):
        q_ref, k_ref, v_ref = refs[:3]
        ci_refs = refs[3:3 + n_ci]
        o_ref, lse_ref = refs[3 + n_ci:5 + n_ci]
        co_refs = refs[5 + n_ci:5 + n_ci + n_co]
        sem_refs = refs[5 + n_ci + n_co:]
        ids = (pl.program_id(0), pl.program_id(1), pl.program_id(2))
        if comm is not None:
            @pl.when((ids[0] == 0) & (ids[1] == 0) & (ids[2] == 0))
            def _():
                comm.start(ci_refs, co_refs, sem_refs)

        attend(q_ref, k_ref, v_ref, o_ref, lse_ref)

        if comm is not None:
            @pl.when((ids[0] == nh - 1) & (ids[1] == nb - 1) & (ids[2] == nq - 1))
            def _():
                comm.wait(ci_refs, co_refs, sem_refs)

    def attend(q_ref, k_ref, v_ref, o_ref, lse_ref):
        qi = pl.program_id(2)
        qt = q_ref[...]

        def tile(j, carry, diagonal):
            m, l, acc = carry
            rows = pl.ds(pl.multiple_of(j * tq, tq), tq)
            s = _dotf(qt, k_ref[rows, :], NT)
            if diagonal:
                r = lax.broadcasted_iota(jnp.int32, (tq, tq), 0)
                c = lax.broadcasted_iota(jnp.int32, (tq, tq), 1)
                s = jnp.where(c <= r, s, -jnp.inf)
            m_new = jnp.maximum(m, jnp.max(s, axis=-1, keepdims=True))
            alpha = jnp.exp2(m - m_new)
            p = jnp.exp2(s - m_new)
            return m_new, alpha * l + jnp.sum(p, axis=-1, keepdims=True), alpha * acc + _dotf(p, v_ref[rows, :], NN)

        init = (jnp.full((tq, 1), -jnp.inf, F32), jnp.zeros((tq, 1), F32), jnp.zeros((tq, dh), F32))
        carry = lax.fori_loop(0, qi, lambda j, c: tile(j, c, False), init)
        m, l, acc = tile(qi, carry, True)
        o_ref[...] = (acc / l).astype(o_ref.dtype)
        lse_ref[...] = m + jnp.log2(l)

    qblk = pl.BlockSpec((None, tq, dh), lambda h, b, i: (h, b * nq + i, 0))
    kblk = pl.BlockSpec((None, seq, dh), lambda h, b, i: (h, b, 0))
    hbm = pl.BlockSpec(memory_space=pl.ANY)
    res = pl.pallas_call(
        body, name=name, grid=(nh, nb, nq), in_specs=[qblk, kblk, kblk] + [hbm] * n_ci,
        out_specs=[qblk, pl.BlockSpec((None, tq, 1), lambda h, b, i: (h, b * nq + i, 0))] + [hbm] * n_co,
        out_shape=[SDS((nh, t, dh), BF), SDS((nh, t, 1), F32)] + (list(comm.out_shapes) if comm is not None else []),
        scratch_shapes=comm.sems if comm is not None else [],
        compiler_params=_cparams(3, VMEM_BIG),
    )(qs, k, v, *(comm.arrays if comm is not None else []))
    if comm is not None:
        hosts.done(name, res[2:])
    return res[0], res[1]


def _flash_bwd(qs, k, v, o, lse, do, nb, seq):
    nh, t, dh = qs.shape
    tq = _row_tile(seq, ATT_TILE)
    nq = seq // tq

    def row_of(col):
        return jnp.broadcast_to(col, (tq, LANES)).T[0:1, :]

    def body(q_ref, k_ref, v_ref, o_ref, lse_ref, do_ref, dq_ref, dk_ref, dv_ref, kt_sc, lrow_sc, drow_sc, dqt_sc):
        for c in range(nq):
            rows = pl.ds(c * tq, tq)
            kt_sc[c] = k_ref[rows, :].T
            delta = jnp.sum(do_ref[rows, :].astype(F32) * o_ref[rows, :].astype(F32), axis=-1, keepdims=True)
            drow_sc[c] = row_of(delta)
            lrow_sc[c] = row_of(lse_ref[rows, :])
        dqt_sc[...] = jnp.zeros_like(dqt_sc)

        def kv_step(j, _):
            rows_j = pl.ds(pl.multiple_of(j * tq, tq), tq)
            ks, vs, kt = k_ref[rows_j, :], v_ref[rows_j, :], kt_sc[j]

            def q_tile(i, carry, diagonal):
                dk, dv = carry
                rows_i = pl.ds(pl.multiple_of(i * tq, tq), tq)
                qt, dot_ = q_ref[rows_i, :], do_ref[rows_i, :]
                pt = jnp.exp2(_dotf(ks, qt, NT) - lrow_sc[i])
                if diagonal:
                    kk = lax.broadcasted_iota(jnp.int32, (tq, tq), 0)
                    qq = lax.broadcasted_iota(jnp.int32, (tq, tq), 1)
                    pt = jnp.where(kk <= qq, pt, 0.0)
                dst = (pt * (_dotf(vs, dot_, NT) - drow_sc[i])).astype(BF)
                dqt_sc[i] += _dotf(kt, dst, NN)
                return dk + _dotf(dst, qt, NN), dv + _dotf(pt, dot_, NN)

            zero = jnp.zeros((tq, dh), F32)
            carry = q_tile(j, (zero, zero), True)
            dk, dv = lax.fori_loop(j + 1, nq, lambda i, c: q_tile(i, c, False), carry)
            dk_ref[rows_j, :] = dk * LN2
            dv_ref[rows_j, :] = dv
            return 0

        lax.fori_loop(0, nq, kv_step, 0)
        for c in range(nq):
            dq_ref[pl.ds(c * tq, tq), :] = dqt_sc[c].T * ATT_SCALE

    full = pl.BlockSpec((None, seq, dh), lambda h, b: (h, b, 0))
    sfull = pl.BlockSpec((None, seq, 1), lambda h, b: (h, b, 0))
    return pl.pallas_call(
        body, name="mla_flash_bwd", grid=(nh, nb), in_specs=[full, full, full, full, sfull, full],
        out_specs=[full, full, full], out_shape=[SDS((nh, t, dh), F32)] * 3,
        scratch_shapes=[pltpu.VMEM((nq, dh, tq), BF), pltpu.VMEM((nq, 1, tq), F32), pltpu.VMEM((nq, 1, tq), F32),
                        pltpu.VMEM((nq, dh, tq), F32)],
        compiler_params=_cparams(2, VMEM_BIG),
    )(qs, k, v, o, lse, do)


def _heads_nt(name, a, wt, out_dtype):
    t, kdim = a.shape
    tm = _row_tile(t, 512)
    nw = MLA_HEADS * HEAD_PAD

    def body(a_ref, w_ref, o_ref):
        r = _dotf(a_ref[...], w_ref[...], NT)
        for h in range(MLA_HEADS):
            o_ref[h] = r[:, HEAD_PAD * h:HEAD_PAD * (h + 1)].astype(o_ref.dtype)

    return pl.pallas_call(
        body, name=name, grid=(t // tm,),
        in_specs=[pl.BlockSpec((tm, kdim), lambda i: (i, 0)), pl.BlockSpec((nw, kdim), lambda i: (0, 0))],
        out_specs=pl.BlockSpec((MLA_HEADS, tm, HEAD_PAD), lambda i: (0, i, 0)),
        out_shape=SDS((MLA_HEADS, t, HEAD_PAD), out_dtype), compiler_params=_cparams(1, VMEM_BIG),
    )(a, wt)


def _all_heads(a_ref):
    return jnp.concatenate([a_ref[h] for h in range(MLA_HEADS)], axis=1)


def _heads_nn(name, a, w, n, out_dtype, res=None):
    t = a.shape[1]
    tm = _row_tile(t, 512)
    nw = MLA_HEADS * HEAD_PAD

    def body(*refs):
        a_ref, w_ref, o_ref = refs[0], refs[1], refs[-1]
        r = _dotf(_all_heads(a_ref), w_ref[...], NN)
        if res is not None:
            r = r + refs[2][...]
        o_ref[...] = r.astype(o_ref.dtype)

    row = pl.BlockSpec((tm, n), lambda i: (i, 0))
    in_specs = [pl.BlockSpec((MLA_HEADS, tm, HEAD_PAD), lambda i: (0, i, 0)), pl.BlockSpec((nw, n), lambda i: (0, 0))]
    args = [a, w]
    if res is not None:
        in_specs.append(row)
        args.append(res)
    return pl.pallas_call(body, name=name, grid=(t // tm,), in_specs=in_specs, out_specs=row,
                          out_shape=SDS((t, n), out_dtype), compiler_params=_cparams(1, VMEM_BIG))(*args)


def _heads_wgrad(name, a, b, n):
    t = b.shape[0]
    tk = _row_tile(t, 512)
    nw = MLA_HEADS * HEAD_PAD
    steps = t // tk

    def body(a_ref, b_ref, o_ref, acc):
        k = pl.program_id(0)

        @pl.when(k == 0)
        def _():
            acc[...] = jnp.zeros_like(acc)

        acc[...] += _dotf(_all_heads(a_ref), b_ref[...], TN)

        @pl.when(k == steps - 1)
        def _():
            o_ref[...] = acc[...]

    return pl.pallas_call(
        body, name=name, grid=(steps,),
        in_specs=[pl.BlockSpec((MLA_HEADS, tk, HEAD_PAD), lambda k: (0, k, 0)), pl.BlockSpec((tk, n), lambda k: (k, 0))],
        out_specs=pl.BlockSpec((nw, n), lambda k: (0, 0)), out_shape=SDS((nw, n), F32),
        scratch_shapes=[pltpu.VMEM((nw, n), F32)], compiler_params=_cparams(1, VMEM_BIG),
    )(a, b)


PM_CKV, PM_KR, PM_CQ = 0, KV_LORA, KV_LORA + HEAD_PAD
PM_DIM = KV_LORA + HEAD_PAD + Q_LORA


def _lat_specs(t, tm):
    return (((tm, KV_LORA), lambda i: (i, 0)), ((tm, HEAD_PAD), lambda i: (i, PM_KR // HEAD_PAD)),
            ((tm, Q_LORA), lambda i: (i, PM_CQ // Q_LORA)))


def _mla_fwd(x, nw, wm, small, tables, nb, seq, hosts=None):
    t, d = x.shape
    cos, sin_signed, seg = tables
    h = _rms_fwd("mla_rms", x, nw)
    pm = _proj_nt("mla_in", h, wm["in_t"], PM_DIM, F32, tn=PM_DIM // 3)
    tm = _row_tile(t, 512)
    ckv_s, kr_s, cq_s = _lat_specs(t, tm)
    row = lambda i: (i, 0)
    par = lambda n: ((1, n), lambda i: (0, 0))
    ckvn = _ew("mla_ckv_norm", (t // tm,), _rms_fn, [(pm,) + ckv_s, (small["kv_a_norm"],) + par(KV_LORA)],
               [(SDS((t, KV_LORA), BF), (tm, KV_LORA), row, False)])[0]
    cqn = _ew("mla_cq_norm", (t // tm,), _rms_fn, [(pm,) + cq_s, (small["q_a_norm"],) + par(Q_LORA)],
              [(SDS((t, Q_LORA), BF), (tm, Q_LORA), row, False)])[0]
    tb = ((tm, HEAD_PAD), row)
    kr = _ew("mla_krope", (t // tm,), _krope_fn,
             [(pm,) + kr_s, (small["k_norm"],) + par(HEAD_PAD), (cos,) + tb, (sin_signed,) + tb,
              (seg, (HEAD_PAD, HEAD_PAD), lambda i: (0, 0))],
             [(SDS((t, HEAD_PAD), F32),) + tb + (False,)])[0]
    q_raw = _heads_nt("mla_q_b", cqn, wm["qb_t"], F32)
    kv_raw = _heads_nt("mla_kv_b", ckvn, wm["kvb_t"], F32)
    qp, kp, vp = _heads_fwd(q_raw, kv_raw, kr, cos, sin_signed, small["q_norm"], small["k_norm"], seg)
    o, lse = _flash_fwd(qp, kp, vp, nb, seq, hosts=hosts)
    out = _heads_nn("mla_out", o, wm["out"], d, F32, res=x)
    return out, (x, h, pm, ckvn, cqn, kr, q_raw, kv_raw, qp, kp, vp, o, lse)


def _mla_bwd(dy, nw, wm, small, tables, saved, nb, seq):
    x, h, pm, ckvn, cqn, kr, q_raw, kv_raw, qp, kp, vp, o, lse = saved
    t, d = x.shape
    cos, sin_signed, seg = tables
    do = _heads_nt("mla_bwd_do", dy, wm["out"], BF)
    g_out = _heads_wgrad("mla_w_out_g", o, dy, d)
    dqp, dkp, dvp = _flash_bwd(qp, kp, vp, o, lse, do, nb, seq)
    dq_raw, dkv_raw, dkr, d_qn, d_kn = _heads_bwd(q_raw, kv_raw, kr, cos, sin_signed, small["q_norm"],
                                                   small["k_norm"], seg, dqp, dkp, dvp)
    dcqn = _heads_nn("mla_bwd_dcq", dq_raw, wm["qb_t"], Q_LORA, F32)
    dckvn = _heads_nn("mla_bwd_dckv", dkv_raw, wm["kvb_t"], KV_LORA, F32)
    g_qb = _heads_wgrad("mla_w_qb_g", dq_raw, cqn, Q_LORA)
    g_kvb = _heads_wgrad("mla_w_kvb_g", dkv_raw, ckvn, KV_LORA)
    tm = _row_tile(t, 512)
    ckv_s, kr_s, cq_s = _lat_specs(t, tm)
    row = lambda i: (i, 0)
    par = lambda n: ((1, n), lambda i: (0, 0))

    def rms_b(xv, w, dv):
        _, vjp = jax.vjp(_rms_fn, xv, w)
        return vjp(dv)

    dckv, d_kva = _ew("mla_ckv_norm_bwd", (t // tm,), rms_b,
                      [(pm,) + ckv_s, (small["kv_a_norm"],) + par(KV_LORA), (dckvn, (tm, KV_LORA), row)],
                      [(SDS((t, KV_LORA), BF), (tm, KV_LORA), row, False),
                       (SDS((1, KV_LORA), F32),) + par(KV_LORA) + (True,)], acc_axes=(0,))
    dcq, d_qa = _ew("mla_cq_norm_bwd", (t // tm,), rms_b,
                    [(pm,) + cq_s, (small["q_a_norm"],) + par(Q_LORA), (dcqn, (tm, Q_LORA), row)],
                    [(SDS((t, Q_LORA), BF), (tm, Q_LORA), row, False),
                     (SDS((1, Q_LORA), F32),) + par(Q_LORA) + (True,)], acc_axes=(0,))
    tb = ((tm, HEAD_PAD), row)

    def kr_b(krv, w, cosv, sinv, sw, dv):
        _, vjp = jax.vjp(lambda a, b: _krope_fn(a, b, cosv, sinv, sw), krv, w)
        return vjp(dv)

    dkr_raw, d_kn2 = _ew("mla_krope_bwd", (t // tm,), kr_b,
                         [(pm,) + kr_s, (small["k_norm"],) + par(HEAD_PAD), (cos,) + tb, (sin_signed,) + tb,
                          (seg, (HEAD_PAD, HEAD_PAD), lambda i: (0, 0)), (dkr,) + tb],
                         [(SDS((t, HEAD_PAD), BF),) + tb + (False,),
                          (SDS((1, HEAD_PAD), F32),) + par(HEAD_PAD) + (True,)], acc_axes=(0,))
    dx, dnw = _seg_nn("mla_bwd_dh", [(dckv, wm["in_t"], PM_CKV), (dkr_raw, wm["in_t"], PM_KR),
                                     (dcq, wm["in_t"], PM_CQ)], d, BF, tk=128, rms=(x, nw, dy))
    g = {
        "in_ckv_t": _wgrad("mla_w_in_ckv_g", dckv, h, KV_LORA, d, tm=KV_LORA, tn=d, out_dtype=F32),
        "in_kr_t": _wgrad("mla_w_in_kr_g", dkr_raw, h, HEAD_PAD, d, tm=HEAD_PAD, tn=d, out_dtype=F32),
        "in_cq_t": _wgrad("mla_w_in_cq_g", dcq, h, Q_LORA, d, tm=Q_LORA, tn=d, out_dtype=F32),
        "qb_t": g_qb, "kvb_t": g_kvb, "out": g_out,
        "q_a_norm": d_qa, "kv_a_norm": d_kva, "q_norm": d_qn, "k_norm": d_kn + d_kn2,
    }
    return dx, dnw, g


def _mesh_pos():
    return lax.axis_index("x"), lax.axis_index("y"), lax.axis_index("c")


def _peer(pos, k):
    x, y, c = pos
    return (x ^ ((k >> 2) & 1), y ^ ((k >> 1) & 1), c ^ (k & 1))


def _flat(pos):
    return 4 * pos[0] + 2 * pos[1] + pos[2]


def _slab(ref, axis, start, size):
    idx = [slice(None)] * axis + [pl.ds(start, size)]
    return ref.at[tuple(idx)]


class _Exchange:
    def __init__(self, kind, items):
        self.kind = kind
        self.axes = [ax for _, ax in items]
        self.arrays = [a for a, _ in items]
        n = len(items)
        self.out_shapes = []
        self.sizes = []
        for a, ax in items:
            shp = list(a.shape)
            if kind == "gather":
                self.sizes.append(shp[ax])
                shp[ax] *= N_DEV
                self.out_shapes.append(SDS(tuple(shp), a.dtype))
            else:
                shp[ax] //= N_DEV
                self.sizes.append(shp[ax])
                self.out_shapes.append(SDS((N_DEV,) + tuple(shp), a.dtype))
        self.sems = [pltpu.SemaphoreType.DMA((n, N_DEV - 1)), pltpu.SemaphoreType.DMA((n, N_DEV - 1)),
                     pltpu.SemaphoreType.DMA((n,))]

    def _copies(self, srcs, dsts, sems, with_arrivals=True):
        send_sems, recv_sems, local_sems = sems
        pos = _mesh_pos()
        me = _flat(pos)
        local, sends, recvs = [], [], []
        for t, (src, dst) in enumerate(zip(srcs, dsts)):
            ax, sz = self.axes[t], self.sizes[t]
            if self.kind == "gather":
                mine = _slab(dst, ax, me * sz, sz)
                local.append(pltpu.make_async_copy(src, mine, local_sems.at[t]))
            else:
                mine = dst.at[me]
                local.append(pltpu.make_async_copy(_slab(src, ax, me * sz, sz), mine, local_sems.at[t]))
            for k in range(1, N_DEV):
                peer = _peer(pos, k)
                there = _flat(peer)
                if self.kind == "gather":
                    out_src, landing = src, _slab(dst, ax, there * sz, sz)
                else:
                    out_src, landing = _slab(src, ax, there * sz, sz), dst.at[there]
                common = dict(send_sem=send_sems.at[t, k - 1], recv_sem=recv_sems.at[t, k - 1], device_id=peer,
                              device_id_type=pl.DeviceIdType.MESH)
                sends.append(pltpu.make_async_remote_copy(src_ref=out_src, dst_ref=mine, **common))
                if with_arrivals:
                    recvs.append(pltpu.make_async_remote_copy(src_ref=out_src, dst_ref=landing, **common))
        return local, sends, recvs

    def start(self, srcs, dsts, sems):
        local, sends, _ = self._copies(srcs, dsts, sems, with_arrivals=False)
        for cp in local + sends:
            cp.start()

    def wait(self, srcs, dsts, sems):
        local, sends, recvs = self._copies(srcs, dsts, sems)
        for rc in recvs:
            rc.wait_recv()
        for rc in sends:
            rc.wait_send()
        for cp in local:
            cp.wait()

    def run(self, name):
        n = len(self.arrays)

        def body(*refs):
            srcs, dsts, sems = refs[:n], refs[n:2 * n], refs[2 * n:]
            self.start(srcs, dsts, sems)
            self.wait(srcs, dsts, sems)

        hbm = pl.BlockSpec(memory_space=pl.ANY)
        return pl.pallas_call(body, name=name, in_specs=[hbm] * n, out_specs=[hbm] * n, out_shape=self.out_shapes,
                              scratch_shapes=self.sems)(*self.arrays)


def _adam_math(w, g, m, v):
    m = ADAM_B1 * m + (1.0 - ADAM_B1) * g
    v = ADAM_B2 * v + (1.0 - ADAM_B2) * (g * g)
    m_hat = m / (1.0 - ADAM_B1 ** ADAM_STEP)
    v_hat = v / (1.0 - ADAM_B2 ** ADAM_STEP)
    delta = -ADAM_LR * (m_hat / (jnp.sqrt(v_hat) + ADAM_EPS) + ADAM_WD * w)
    return delta, m, v


def _adam(name, land, land_blk, land_idx, w, m, v, transposed, ck):
    n, r, c = w.shape
    wblk = ((None, ck, c), lambda a, i: (a, i, 0))

    def fn(parts, w, m, v):
        g = parts[0].astype(F32)
        for s in range(1, N_DEV):
            g = g + parts[s].astype(F32)
        if transposed:
            g = g.T
        delta, m2, v2 = _adam_math(w, g, m, v)
        return g, delta, m2, v2

    return _ew(name, (n, r // ck), fn,
               [(land, land_blk, land_idx), (w,) + wblk, (m,) + wblk, (v,) + wblk],
               [(SDS(w.shape, F32),) + wblk + (False,)] * 4, vmem=VMEM_BIG)


def _prep_ffn(gate, up, down):
    def body(g, u, dn, o):
        o[0] = g[...].T.astype(BF)
        o[1] = u[...].T.astype(BF)
        o[2] = dn[...].astype(BF)

    cblk = pl.BlockSpec((None, None, D_MODEL, FF_SHARD), lambda l, i: (l, i, 0, 0))
    rblk = pl.BlockSpec((None, None, FF_SHARD, D_MODEL), lambda l, i: (l, i, 0, 0))
    return pl.pallas_call(
        body, name="prep_ffn", grid=(2, 2), in_specs=[cblk, cblk, rblk],
        out_specs=pl.BlockSpec((3, FF_SHARD, D_MODEL), lambda l, i: (2 * l + i, 0, 0)),
        out_shape=SDS((12, FF_SHARD, D_MODEL), BF), compiler_params=_cparams(2, VMEM_BIG),
    )(gate, up, down)


def _transpose_cast(name, w, dtype):
    def body(a, o):
        o[...] = a[...].T.astype(dtype)

    r, c = w.shape
    return pl.pallas_call(body, name=name, out_shape=SDS((c, r), dtype),
                          compiler_params=pltpu.CompilerParams(vmem_limit_bytes=VMEM_BIG))(w)


SMALL_SHARDED = (("norm_w", 6 * 128), ("conv_w", CONV_K * 512), ("q_a_norm", 48), ("kv_a_norm", 32))
SMALL_PACK = 3072


def _dyn(a, start, size):
    return lax.dynamic_slice_in_dim(a, start, size, axis=a.ndim - 1)


def _layout_ssm(ssm_in_t, ssm_out_all):
    d = ssm_in_t.shape[1]
    dt_rows = ssm_in_t[D_INNER + CONV_DIM:].reshape(SSM_GROUPS, SSM_HPG, d)
    return {"z_t": ssm_in_t[:D_INNER], "xbc_t": ssm_in_t[D_INNER:D_INNER + CONV_DIM],
            "dt_t": jnp.pad(dt_rows, ((0, 0), (0, LANES - SSM_HPG), (0, 0))).reshape(SSM_GROUPS * LANES, d),
            "out": ssm_out_all}


def _layout_mla(mla_in_all, qb_all, kvb_all, mla_out_all):
    d = mla_out_all.shape[1]
    in_t = mla_in_all.T
    kr_rows = jnp.pad(in_t[Q_LORA + KV_LORA:], ((QK_NOPE, HEAD_PAD - QK_DIM), (0, 0)))
    qb_heads = jnp.pad(qb_all.reshape(MLA_HEADS, QK_DIM, Q_LORA), ((0, 0), (0, HEAD_PAD - QK_DIM), (0, 0)))
    out_heads = jnp.pad(mla_out_all.reshape(MLA_HEADS, 64, d), ((0, 0), (64, 0), (0, 0)))
    return {"in_t": jnp.concatenate([in_t[Q_LORA:Q_LORA + KV_LORA], kr_rows, in_t[:Q_LORA]], axis=0),
            "qb_t": qb_heads.reshape(MLA_HEADS * HEAD_PAD, Q_LORA), "kvb_t": kvb_all,
            "out": out_heads.reshape(MLA_HEADS * HEAD_PAD, d)}


def _layout_small(conv_w, conv_b, dt_bias, a_log, d_skip, ssm_norm_w, q_a_norm, kv_a_norm, q_norm, k_norm):
    lane_heads = lambda p: jnp.pad(p.reshape(SSM_GROUPS, SSM_HPG), ((0, 0), (0, LANES - SSM_HPG))).reshape(1, -1)
    pad_head = lambda p: jnp.pad(p.reshape(1, QK_DIM), ((0, 0), (0, HEAD_PAD - QK_DIM)))
    return {"conv_w": conv_w, "conv_b": conv_b, "dt_bias": lane_heads(dt_bias), "a_log": lane_heads(a_log),
            "d_skip": lane_heads(d_skip), "ssm_norm_w": ssm_norm_w, "q_a_norm": q_a_norm, "kv_a_norm": kv_a_norm,
            "q_norm": pad_head(q_norm), "k_norm": pad_head(k_norm)}


class _Plan:
    def __init__(self, ctx):
        self.ctx = ctx
        self.make = {}
        self.land = {}

    def ride(self, host, make, land):
        assert host not in self.make, host
        self.make[host] = make
        self.land[host] = land

    def comm(self, host):
        return self.make[host](self.ctx) if host in self.make else None

    def done(self, host, results):
        self.land[host](results, self.ctx)


def _local_step(x, positions, loss_target, ctx, plan=None):
    nb, seq, d = x.shape
    t = nb * seq
    xf = x.reshape(t, d)
    norm = ctx["norm"]
    tables = list(_rope_tables(positions.reshape(t, 1))) + [_segment_matrix()]
    x1, s_f0 = _ffn_fwd("ffn0", xf, norm[0, 0], ctx["ffn0"], plan)
    x2, s_ssm = _ssm_fwd(x1, norm[0, 1], ctx["ws"], ctx["small"], nb, seq, plan)
    x3, s_f1 = _ffn_fwd("ffn1", x2, norm[0, 2], ctx["ffn1"], plan)
    x4, s_f2 = _ffn_fwd("ffn2", x3, norm[1, 0], ctx["ffn2"], plan)
    x5, s_mla = _mla_fwd(x4, norm[1, 1], ctx["wm"], ctx["small"], tables, nb, seq, plan)
    x6, s_f3 = _ffn_fwd("ffn3", x5, norm[1, 2], ctx["ffn3"], plan)
    dy, loss_cols = _loss_and_grad(x6, loss_target.reshape(t, d))

    dx5, dn12, ctx["g_ffn3"] = _ffn_bwd("ffn3", dy, norm[1, 2], ctx["ffn3"], s_f3, plan)
    dx4, dn11, ctx["g_mla"] = _mla_bwd(dx5, norm[1, 1], ctx["wm"], ctx["small"], tables, s_mla, nb, seq)
    dx3, dn10, ctx["g_ffn2"] = _ffn_bwd("ffn2", dx4, norm[1, 0], ctx["ffn2"], s_f2, plan)
    dx2, dn02, ctx["g_ffn1"] = _ffn_bwd("ffn1", dx3, norm[0, 2], ctx["ffn1"], s_f1, plan)
    dx1, dn01, ctx["g_ssm"] = _ssm_bwd(dx2, norm[0, 1], ctx["ws"], ctx["small"], s_ssm, nb, seq, plan)
    dx0, dn00, ctx["g_ffn0"] = _ffn_bwd("ffn0", dx1, norm[0, 0], ctx["ffn0"], s_f0, plan)
    return loss_cols, dx0.reshape(nb, seq, d), (dn00, dn01, dn02, dn10, dn11, dn12)


def kernel(x, positions, norm_w, ffn_w_gate, ffn_w_up, ffn_w_down, ssm_w_in, ssm_conv_w, ssm_conv_b, ssm_dt_bias, ssm_a_log, ssm_d, ssm_norm_w, ssm_w_out, mla_w_in, mla_q_a_norm, mla_kv_a_norm, mla_w_q_b, mla_w_kv_b, mla_q_norm, mla_k_norm, mla_w_out, loss_target, m_norm_w, m_ffn_w_gate, m_ffn_w_up, m_ffn_w_down, m_ssm_w_in, m_ssm_conv_w, m_ssm_conv_b, m_ssm_dt_bias, m_ssm_a_log, m_ssm_d, m_ssm_norm_w, m_ssm_w_out, m_mla_w_in, m_mla_q_a_norm, m_mla_kv_a_norm, m_mla_w_q_b, m_mla_w_kv_b, m_mla_q_norm, m_mla_k_norm, m_mla_w_out, v_norm_w, v_ffn_w_gate, v_ffn_w_up, v_ffn_w_down, v_ssm_w_in, v_ssm_conv_w, v_ssm_conv_b, v_ssm_dt_bias, v_ssm_a_log, v_ssm_d, v_ssm_norm_w, v_ssm_w_out, v_mla_w_in, v_mla_q_a_norm, v_mla_kv_a_norm, v_mla_w_q_b, v_mla_w_kv_b, v_mla_q_norm, v_mla_k_norm, v_mla_w_out):
    nb, seq, d = x.shape
    t = nb * seq
    me = _flat(_mesh_pos())

    ffn_loc = _prep_ffn(ffn_w_gate, ffn_w_up, ffn_w_down)
    ssm_in_loc = _transpose_cast("prep_ssm_in", ssm_w_in[0], BF)
    ssm_out_loc = ssm_w_out[0].astype(BF)
    mla_in_loc, mla_out_loc = mla_w_in[0].astype(BF), mla_w_out[0].astype(BF)
    qb_loc = _transpose_cast("prep_q_b", mla_w_q_b[0], BF)
    kvb_loc = _transpose_cast("prep_kv_b", mla_w_kv_b[0], BF)
    small_loc = jnp.concatenate([norm_w.reshape(-1), ssm_conv_w.reshape(-1), mla_q_a_norm.reshape(-1),
                                 mla_kv_a_norm.reshape(-1)])
    small_loc = jnp.pad(small_loc, (0, SMALL_PACK - small_loc.shape[0])).reshape(SMALL_PACK // LANES, LANES)

    wloc = lambda n: [(ffn_loc[3 * n + k], 0) for k in range(3)]
    g0, u0, small_all = _Exchange("gather", wloc(0)[0:2] + [(small_loc, 0)]).run("gather_first")
    sm = small_all.reshape(N_DEV, SMALL_PACK)
    conv_w_full = sm[:, 768:768 + 2048].reshape(N_DEV, CONV_K, 512).transpose(1, 0, 2).reshape(CONV_K, CONV_DIM)
    ctx = {"ffn0": [g0, u0, None], "ffn1": [None] * 3, "ffn2": [None] * 3,
           "norm": sm[:, :768].reshape(N_DEV, 6, 128).transpose(1, 0, 2).reshape(2, 3, 1, d),
           "small": _layout_small(conv_w_full, ssm_conv_b, ssm_dt_bias, ssm_a_log, ssm_d, ssm_norm_w,
                                  sm[:, 2816:2864].reshape(1, Q_LORA), sm[:, 2864:2896].reshape(1, KV_LORA),
                                  mla_q_norm, mla_k_norm)}
    plan = _Plan(ctx)

    def gather_on(host, items, land):
        plan.ride(host, lambda c: _Exchange("gather", items), land)

    def put_w(key, ks):
        def land(r, c):
            for k, arr in zip(ks, r):
                c[key][k] = arr
        return land

    half_rows = SSM_IN_SHARD // 2
    ssm_in_a, ssm_in_b = (ssm_in_loc[k * half_rows:(k + 1) * half_rows].reshape(half_rows // 2, 16, LANES)
                          for k in range(2))

    def land_first(r, c):
        c["ffn0"][2] = r[0]
        c["ssm_in_a"] = r[1].reshape(N_DEV, half_rows, d)

    def land_ssm_in(r, c):
        both = jnp.concatenate([c["ssm_in_a"], r[0].reshape(N_DEV, half_rows, d)], axis=1)
        c["ws"] = _layout_ssm(both.reshape(SSM_IN_DIM, d), None)

    def land_ssm_out(r, c):
        c["ffn1"][1] = r[0]
        c["ws"]["out"] = r[1]

    gather_on("ffn0_up", wloc(0)[2:3] + [(ssm_in_a, 0)], land_first)
    gather_on("ffn0_down", [(ssm_in_b, 0)], land_ssm_in)
    gather_on("ssm_in_z", wloc(1)[0:1], put_w("ffn1", (0,)))
    gather_on("ssm_in_xbc", wloc(1)[1:2] + [(ssm_out_loc, 0)], land_ssm_out)
    gather_on("ssm_out", wloc(1)[2:3], put_w("ffn1", (2,)))
    gather_on("ffn1_up", wloc(2)[0:2], put_w("ffn2", (0, 1)))
    gather_on("ffn1_down", wloc(2)[2:3], put_w("ffn2", (2,)))
    gather_on("ffn2_up", [(mla_in_loc, 0), (qb_loc, 0), (kvb_loc, 0), (mla_out_loc, 0)],
              lambda r, c: c.update(wm=_layout_mla(r[0], r[1], r[2], r[3])))
    gather_on("mla_flash_fwd", wloc(3), lambda r, c: c.update(ffn3=tuple(r)))

    heads_of = lambda a: a.reshape(SSM_GROUPS, LANES, -1)[:, :SSM_HPG].reshape(SSM_HEADS, -1)

    def mla_grad_items(c):
        g = c["g_mla"]
        g_in = jnp.concatenate([g["in_cq_t"], g["in_ckv_t"], g["in_kr_t"][QK_NOPE:QK_DIM]], axis=0).T
        g_qb = g["qb_t"].reshape(MLA_HEADS, HEAD_PAD, Q_LORA)[:, :QK_DIM].reshape(MLA_HEADS * QK_DIM, Q_LORA)
        g_out = g["out"].reshape(MLA_HEADS, HEAD_PAD, d)[:, 64:].reshape(MLA_HEADS * 64, d)
        return [(a.astype(BF), 0) for a in (g_out, g_in, g_qb, g["kvb_t"])]

    def ssm_in_grad(which):
        def items(c):
            if "g_ssm_in_t" not in c:
                g = c["g_ssm"]
                g_in_t = jnp.concatenate([g["z_t"], g["x_t"], g["b_t"], g["c_t"], heads_of(g["dt_t"])], axis=0)
                c["g_ssm_in_t"] = g_in_t.astype(BF).reshape(N_DEV, SSM_IN_SHARD, d)
            part = c["g_ssm_in_t"][:, which * half_rows:(which + 1) * half_rows]
            return [(part.reshape(N_DEV * half_rows // 2, 16, LANES), 0)]
        return items

    def scatter_on(host, items_of, keys):
        plan.ride(host, lambda c: _Exchange("scatter", items_of(c)),
                  lambda r, c: c.update(dict(zip(keys, r))))

    of = lambda key, k: (lambda c: [(c[key][k], 0)])
    scatter_on("ffn2_bwd_act", of("g_ffn3", 0), ("l3_gate",))
    scatter_on("ffn2_bwd_dh", of("g_ffn3", 1), ("l3_up",))
    scatter_on("ffn2_wg", of("g_ffn3", 2), ("l3_down",))
    scatter_on("ffn2_wu", mla_grad_items, ("l_mla_out", "l_mla_in", "l_qb", "l_kvb"))
    scatter_on("ffn1_bwd_act", of("g_ffn2", 0), ("l2_gate",))
    scatter_on("ffn1_bwd_dh", of("g_ffn2", 1), ("l2_up",))
    scatter_on("ffn1_wg", of("g_ffn2", 2), ("l2_down",))
    scatter_on("ssm_bwd_dgn", of("g_ffn1", 0), ("l1_gate",))
    scatter_on("ssm_bwd_dh", lambda c: [(c["g_ffn1"][1], 0), (c["g_ffn1"][2], 0)], ("l1_up", "l1_down"))
    scatter_on("ffn0_bwd_act", ssm_in_grad(0), ("l_ssm_in_a",))
    scatter_on("ffn0_bwd_dh", ssm_in_grad(1), ("l_ssm_in_b",))
    scatter_on("ffn0_wg", lambda c: [(c["g_ssm"]["out"], 0)], ("l_ssm_out",))
    scatter_on("ffn0_wu", lambda c: [(c["g_ffn0_gate"], 0)], ("l0_gate",))
    scatter_on("ffn0_wd", lambda c: [(c["g_ffn0_up"], 0)], ("l0_up",))

    loss_cols, grad_x, dns = _local_step(x, positions, loss_target, ctx, plan)
    loss = lax.psum(jnp.sum(loss_cols), ("x", "y", "c"))
    dn00, dn01, dn02, dn10, dn11, dn12 = dns
    g_ssm, g_mla = ctx["g_ssm"], ctx["g_mla"]
    ctx["l0_down"] = _Exchange("scatter", [(ctx["g_ffn0"][2], 0)]).run("scatter_last")[0]
    l_ffn = {k: jnp.stack([ctx["l%d_%s" % (n, k)] for n in range(4)], axis=1) for k in ("gate", "up", "down")}
    l_mla_out, l_mla_in, l_qb, l_kvb = (ctx[k] for k in ("l_mla_out", "l_mla_in", "l_qb", "l_kvb"))
    l_ssm_in = jnp.concatenate([ctx[k].reshape(N_DEV, half_rows, d) for k in ("l_ssm_in_a", "l_ssm_in_b")], axis=1)
    l_ssm_out = ctx["l_ssm_out"]

    unlane = lambda a: a.reshape(SSM_GROUPS, LANES)[:, :SSM_HPG].reshape(1, SSM_HEADS)
    small_g = jnp.concatenate([
        jnp.concatenate([dn00, dn01, dn02, dn10, dn11, dn12], axis=0).reshape(-1),
        g_ssm["conv_w"].reshape(-1), g_ssm["conv_b"].reshape(-1), unlane(g_ssm["dt_bias"]).reshape(-1),
        unlane(g_ssm["a_log"]).reshape(-1), unlane(g_ssm["d_skip"]).reshape(-1), g_ssm["ssm_norm_w"].reshape(-1),
        g_mla["q_a_norm"].reshape(-1), g_mla["kv_a_norm"].reshape(-1), g_mla["q_norm"][0, :QK_DIM],
        g_mla["k_norm"][0, :QK_DIM]])
    n_small = small_g.shape[0]
    n_small_pad = -(-n_small // (8 * LANES)) * (8 * LANES)
    small_g = jnp.pad(small_g, (0, n_small_pad - n_small)).reshape(n_small_pad // LANES, LANES)
    gs = _Exchange("gather", [(small_g, 0)]).run("gather_small_grads")[0].reshape(N_DEV, n_small_pad)

    outs = {}

    def put(name, res, shape):
        for key, val in zip(("grad", "delta", "new_m", "new_v"), res):
            outs[(key, name)] = val.reshape(shape)

    ck = 256
    for key, name, w, m, v in (("gate", "ffn_w_gate", ffn_w_gate, m_ffn_w_gate, v_ffn_w_gate),
                               ("up", "ffn_w_up", ffn_w_up, m_ffn_w_up, v_ffn_w_up)):
        res = _adam("adam_" + name, l_ffn[key], (N_DEV, None, FF_SHARD, ck), lambda a, i: (0, a, 0, i),
                    w.reshape(4, d, FF_SHARD), m.reshape(4, d, FF_SHARD), v.reshape(4, d, FF_SHARD), True, ck)
        put(name, res, w.shape)
    res = _adam("adam_ffn_w_down", l_ffn["down"], (N_DEV, None, 176, d), lambda a, i: (0, a, i, 0),
                ffn_w_down.reshape(4, FF_SHARD, d), m_ffn_w_down.reshape(4, FF_SHARD, d),
                v_ffn_w_down.reshape(4, FF_SHARD, d), False, 176)
    put("ffn_w_down", res, ffn_w_down.shape)
    l_ssm_in2 = l_ssm_in.reshape(N_DEV, SSM_IN_SHARD, d)
    res = _adam("adam_ssm_w_in", l_ssm_in2, (N_DEV, SSM_IN_SHARD, 128), lambda a, i: (0, 0, i),
                ssm_w_in, m_ssm_w_in, v_ssm_w_in, True, 128)
    put("ssm_w_in", res, ssm_w_in.shape)
    res = _adam("adam_ssm_w_out", l_ssm_out, (N_DEV, 128, d), lambda a, i: (0, i, 0),
                ssm_w_out, m_ssm_w_out, v_ssm_w_out, False, 128)
    put("ssm_w_out", res, ssm_w_out.shape)
    res = _adam("adam_mla_w_in", l_mla_in, (N_DEV, 128, MLA_IN_DIM), lambda a, i: (0, 0, 0),
                mla_w_in, m_mla_w_in, v_mla_w_in, False, 128)
    put("mla_w_in", res, mla_w_in.shape)
    res = _adam("adam_mla_w_q_b", l_qb, (N_DEV, 192, 128), lambda a, i: (0, 0, i),
                mla_w_q_b, m_mla_w_q_b, v_mla_w_q_b, True, 128)
    put("mla_w_q_b", res, mla_w_q_b.shape)
    res = _adam("adam_mla_w_kv_b", l_kvb, (N_DEV, 256, 128), lambda a, i: (0, 0, i),
                mla_w_kv_b, m_mla_w_kv_b, v_mla_w_kv_b, True, 128)
    put("mla_w_kv_b", res, mla_w_kv_b.shape)
    res = _adam("adam_mla_w_out", l_mla_out, (N_DEV, 128, d), lambda a, i: (0, 0, 0),
                mla_w_out, m_mla_w_out, v_mla_w_out, False, 128)
    put("mla_w_out", res, mla_w_out.shape)

    small_params = (
        ("norm_w", norm_w, m_norm_w, v_norm_w, 6 * d, 6, 128), ("ssm_conv_w", ssm_conv_w, m_ssm_conv_w, v_ssm_conv_w,
                                                               CONV_K * CONV_DIM, CONV_K, 512),
        ("ssm_conv_b", ssm_conv_b, m_ssm_conv_b, v_ssm_conv_b, CONV_DIM, 0, 0),
        ("ssm_dt_bias", ssm_dt_bias, m_ssm_dt_bias, v_ssm_dt_bias, SSM_HEADS, 0, 0),
        ("ssm_a_log", ssm_a_log, m_ssm_a_log, v_ssm_a_log, SSM_HEADS, 0, 0),
        ("ssm_d", ssm_d, m_ssm_d, v_ssm_d, SSM_HEADS, 0, 0),
        ("ssm_norm_w", ssm_norm_w, m_ssm_norm_w, v_ssm_norm_w, D_INNER, 0, 0),
        ("mla_q_a_norm", mla_q_a_norm, m_mla_q_a_norm, v_mla_q_a_norm, Q_LORA, 1, 48),
        ("mla_kv_a_norm", mla_kv_a_norm, m_mla_kv_a_norm, v_mla_kv_a_norm, KV_LORA, 1, 32),
        ("mla_q_norm", mla_q_norm, m_mla_q_norm, v_mla_q_norm, QK_DIM, 0, 0),
        ("mla_k_norm", mla_k_norm, m_mla_k_norm, v_mla_k_norm, QK_DIM, 0, 0),
    )
    parts, ws_, ms_, vs_, off = [], [], [], [], 0
    for name, w, m, v, full, rows, shard in small_params:
        seg = gs[:, off:off + full]
        if rows:
            seg = _dyn(seg.reshape(N_DEV, rows, full // rows), me * shard, shard).reshape(N_DEV, rows * shard)
        parts.append(seg)
        ws_.append(w.reshape(1, -1))
        ms_.append(m.reshape(1, -1))
        vs_.append(v.reshape(1, -1))
        off += full
    n_loc = sum(p.shape[1] for p in parts)
    n_loc_pad = -(-n_loc // LANES) * LANES
    padc = lambda a, val=0.0: jnp.pad(jnp.concatenate(a, axis=1), ((0, 0), (0, n_loc_pad - n_loc)),
                                      constant_values=val)
    res = _adam("adam_small", padc(parts).reshape(N_DEV, 1, n_loc_pad), (N_DEV, 1, n_loc_pad), lambda a, i: (0, 0, 0),
                padc(ws_).reshape(1, 1, n_loc_pad), padc(ms_).reshape(1, 1, n_loc_pad),
                padc(vs_, 1.0).reshape(1, 1, n_loc_pad), False, 1)
    off = 0
    for name, w, m, v, full, rows, shard in small_params:
        nloc = w.size
        put(name, [r.reshape(-1)[off:off + nloc] for r in res], w.shape)
        off += nloc

    order = ("norm_w", "ffn_w_gate", "ffn_w_up", "ffn_w_down", "ssm_w_in", "ssm_conv_w", "ssm_conv_b", "ssm_dt_bias",
             "ssm_a_log", "ssm_d", "ssm_norm_w", "ssm_w_out", "mla_w_in", "mla_q_a_norm", "mla_kv_a_norm",
             "mla_w_q_b", "mla_w_kv_b", "mla_q_norm", "mla_k_norm", "mla_w_out")
    return (loss, grad_x, *[outs[(k, n)] for k in ("grad", "delta", "new_m", "new_v") for n in order])
```
